```python
import math
import jax, jax.numpy as jnp
from jax import lax
import numpy as np

D_MODEL = 2048
BATCH = 2
SEQ = 4096
DEPTH = 2

GRID_W = 64
CTX_LEN = 256
EPS = 1e-6
NEG_STATE = -1e30

D_MIX = D_MODEL
H_A = 4
D_NOPE = 128
D_ROPE = 64
D_V = 128
Q_RANK = 384
KV_RANK = 128
ROPE_AXIS = D_ROPE // 2
ROPE_BASE = 10000.0
Q_BLOCK = 128
D_MLA = H_A * D_V
D_SSD = D_MIX // 4
P_B = 64
H_B = D_SSD // P_B
G_B = 2
N_B = 128
SSD_CONV = 5
SSD_CHUNK = 128
CONV_CH = D_SSD + 2 * G_B * N_B
D_MLSTM = D_MIX // 4
H_C = 4
DV_C = D_MLSTM // H_C
DQK_C = DV_C // 2
MLSTM_CHUNK = 128
D_S5 = D_MIX - D_MLA - D_SSD - D_MLSTM
S5_GROUP = 16
G_S5 = D_S5 // S5_GROUP
P_S5 = 64
A_COLS = Q_RANK + KV_RANK + D_ROPE
B_COLS = D_SSD + CONV_CH + 2 * H_B
C_COLS = 2 * H_C * DQK_C + 2 * D_MLSTM + 4 * H_C
D_COLS = D_S5
N_IN = A_COLS + B_COLS + C_COLS + D_COLS
N_GROUPS = 4
EXPERTS_PER_GROUP = 8
N_EXPERTS = N_GROUPS * EXPERTS_PER_GROUP
TOP_K = 2
D_EXPERT = 512
MOE_BLOCK = 128

kernel_name = "hybrid_parallel_heads_diffusion_block"

F32 = jnp.float32


def rmsnorm(x, w=None):
    xf = x.astype(F32)
    y = xf * lax.rsqrt(jnp.mean(xf * xf, axis=-1, keepdims=True) + EPS)
    if w is not None:
        y = y * w.astype(F32)
    return y.astype(x.dtype)


def flip(t):
    return jnp.flip(t, axis=1)


def axial_rope_tables(L):
    pos = jnp.arange(L)
    row = (pos // GRID_W).astype(F32)
    col = (pos % GRID_W).astype(F32)
    inv_freq = ROPE_BASE ** (-jnp.arange(ROPE_AXIS // 2, dtype=F32) * 2.0 / ROPE_AXIS)
    ang = jnp.stack([row[:, None] * inv_freq, col[:, None] * inv_freq], axis=1)
    return jnp.cos(ang), jnp.sin(ang)


def apply_rope2d(x, cos, sin):
    xs = x.astype(F32).reshape(x.shape[:-1] + (2, 2, ROPE_AXIS // 2))
    x1, x2 = xs[..., 0, :], xs[..., 1, :]
    c = cos[None, :, None]
    s = sin[None, :, None]
    out = jnp.stack([x1 * c - x2 * s, x1 * s + x2 * c], axis=-2)
    return out.reshape(x.shape).astype(x.dtype)


def block_attention(q_nope, q_rope, k_nope, k_rope, v):
    B, Lq, H, _ = q_nope.shape
    nb = Lq // Q_BLOCK
    scale = (D_NOPE + D_ROPE) ** -0.5

    def to_blocks(t):
        return t.reshape((B, nb, Q_BLOCK) + t.shape[2:]).swapaxes(0, 1)

    def one_block(qb):
        qn, qr = qb
        s = jnp.einsum('bqhd,bkhd->bhqk', qn, k_nope) + jnp.einsum('bqhd,bkd->bhqk', qr, k_rope)
        p = jax.nn.softmax(s.astype(F32) * scale, axis=-1).astype(v.dtype)
        return jnp.einsum('bhqk,bkhd->bqhd', p, v)

    out = lax.map(one_block, (to_blocks(q_nope), to_blocks(q_rope)))
    return out.swapaxes(0, 1).reshape(B, Lq, H, D_V)


def mla_mixer(ax, ac, q_norm_w, kv_norm_w, w_uq, w_ukv, want_ctx):
    B, L, _ = ax.shape
    cos, sin = axial_rope_tables(L)

    def project(a, rope, want_q):
        Lt = a.shape[1]
        cq, ckv, kr = jnp.split(a, [Q_RANK, Q_RANK + KV_RANK], axis=-1)
        kv = (rmsnorm(ckv, kv_norm_w) @ w_ukv).reshape(B, Lt, H_A, D_NOPE + D_V)
        k_nope, v = kv[..., :D_NOPE], kv[..., D_NOPE:]
        k_rope = kr[:, :, None, :]
        if rope:
            k_rope = apply_rope2d(k_rope, cos, sin)
        k_rope = k_rope[:, :, 0]
        if not want_q:
            return None, None, k_nope, k_rope, v
        q = (rmsnorm(cq, q_norm_w) @ w_uq).reshape(B, Lt, H_A, D_NOPE + D_ROPE)
        q_nope, q_rope = q[..., :D_NOPE], q[..., D_NOPE:]
        if rope:
            q_rope = apply_rope2d(q_rope, cos, sin)
        return q_nope, q_rope, k_nope, k_rope, v

    qn_c, qr_c, kn_c, kr_c, v_c = project(ac, False, want_ctx)
    qn_x, qr_x, kn_x, kr_x, v_x = project(ax, True, True)
    kn = jnp.concatenate([kn_x, kn_c], axis=1)
    kr = jnp.concatenate([kr_x, kr_c], axis=1)
    vv = jnp.concatenate([v_x, v_c], axis=1)
    y_x = rmsnorm(block_attention(qn_x, qr_x, kn, kr, vv).reshape(B, L, D_MLA))
    y_c = None
    if want_ctx:
        y_c = rmsnorm(block_attention(qn_c, qr_c, kn_c, kr_c, v_c).reshape(B, ac.shape[1], D_MLA))
    return y_x, y_c


def centred_depthwise_conv(x, w, b):
    C = x.shape[-1]
    K = w.shape[0]
    y = lax.conv_general_dilated(x, w[:, None, :], window_strides=(1,),
                                 padding=[((K - 1) // 2, (K - 1) // 2)],
                                 dimension_numbers=('NWC', 'WIO', 'NWC'),
                                 feature_group_count=C)
    return y + b


def ssd_chunked(x, dt, A, Bm, Cm, h0, want_y):
    B, L, H, P = x.shape
    nc = L // SSD_CHUNK
    rep = H // Bm.shape[2]
    Bh = jnp.repeat(Bm, rep, axis=2).reshape(B, nc, SSD_CHUNK, H, -1)
    Ch = jnp.repeat(Cm, rep, axis=2).reshape(B, nc, SSD_CHUNK, H, -1)
    xc = x.reshape(B, nc, SSD_CHUNK, H, P)
    dtc = dt.reshape(B, nc, SSD_CHUNK, H)
    acum = jnp.cumsum(dtc * A, axis=2)
    a_tot = acum[:, :, -1]
    w_end = jnp.exp(a_tot[:, :, None] - acum) * dtc
    s_loc = jnp.einsum('bcqh,bcqhn,bcqhp->bchpn', w_end, Bh, xc)

    def step(h, inp):
        s, dec = inp
        return jnp.exp(dec)[:, :, None, None] * h + s, h

    h_fin, h_start = lax.scan(step, h0, (s_loc.swapaxes(0, 1), a_tot.swapaxes(0, 1)))
    if not want_y:
        return None, h_fin
    h_start = h_start.swapaxes(0, 1)
    causal = jnp.tril(jnp.ones((SSD_CHUNK, SSD_CHUNK), dtype=bool))
    seg = acum[:, :, :, None, :] - acum[:, :, None, :, :]
    decay = jnp.exp(jnp.where(causal[None, None, :, :, None], seg, -jnp.inf))
    scores = jnp.einsum('bcihn,bcjhn->bcijh', Ch, Bh) * decay
    y = (jnp.einsum('bcijh,bcjh,bcjhp->bcihp', scores, dtc, xc)
         + jnp.exp(acum)[..., None] * jnp.einsum('bcihn,bchpn->bcihp', Ch, h_start))
    return y.reshape(B, L, H, P), h_fin


def ssd_mixer(bx, bc, conv_w, conv_b, a_log, dt_bias, d_skip, want_ctx):
    A = -jnp.exp(a_log.astype(F32))
    dtb = dt_bias.astype(F32)
    dsk = d_skip.astype(F32)

    def prep(b):
        Bt, Lt, _ = b.shape
        z, xbc, dt = jnp.split(b, [D_SSD, D_SSD + CONV_CH], axis=-1)
        xbc = jax.nn.silu(centred_depthwise_conv(xbc, conv_w, conv_b)).astype(F32)
        xs, Bm, Cm = jnp.split(xbc, [D_SSD, D_SSD + G_B * N_B], axis=-1)
        xs = xs.reshape(Bt, Lt, H_B, P_B)
        Bm = Bm.reshape(Bt, Lt, G_B, N_B)
        Cm = Cm.reshape(Bt, Lt, G_B, N_B)
        dt = jax.nn.softplus(dt.astype(F32).reshape(Bt, Lt, 2, H_B) + dtb)
        return z, xs, Bm, Cm, dt[:, :, 0], dt[:, :, 1]

    def run(b, h0f, h0b, want_y):
        z, xs, Bm, Cm, dtf, dtbk = prep(b)
        yf, hf = ssd_chunked(xs, dtf, A[0], Bm, Cm, h0f, want_y)
        yb, hb = ssd_chunked(flip(xs), flip(dtbk), A[1], flip(Bm), flip(Cm), h0b, want_y)
        if not want_y:
            return None, hf, hb
        y = yf + flip(yb) + dsk[:, None] * xs
        y = y.reshape(b.shape[0], b.shape[1], D_SSD) * jax.nn.silu(z.astype(F32))
        return rmsnorm(y), hf, hb

    h0 = jnp.zeros((bc.shape[0], H_B, P_B, N_B), F32)
    y_c, hf_c, hb_c = run(bc, h0, h0, want_ctx)
    y_x, _, _ = run(bx, hf_c, hb_c, True)
    return y_x, y_c


def mlstm_chunked(q, k, v, log_i, log_f, state0, want_y):
    B, L, H, _ = q.shape
    nc = L // MLSTM_CHUNK

    def chunks(t):
        return t.reshape((B, nc, MLSTM_CHUNK) + t.shape[2:]).swapaxes(0, 1)

    causal = jnp.tril(jnp.ones((MLSTM_CHUNK, MLSTM_CHUNK), dtype=bool))[None, :, :, None]

    def step(carry, inp):
        C, n, m = carry
        qc, kc, vc, li, lf = inp
        b = jnp.cumsum(lf, axis=1)
        g = b[:, -1]
        w_log = g[:, None] - b + li
        m_new = jnp.maximum(g + m, jnp.max(w_log, axis=1))
        wj = jnp.exp(w_log - m_new[:, None])
        dec = jnp.exp(g + m - m_new)
        C_new = dec[:, :, None, None] * C + jnp.einsum('bjh,bjhv,bjhk->bhvk', wj, vc, kc)
        n_new = dec[:, :, None] * n + jnp.einsum('bjh,bjhk->bhk', wj, kc)
        if not want_y:
            return (C_new, n_new, m_new), None
        dmat = jnp.where(causal, b[:, :, None] - b[:, None] + li[:, None], -jnp.inf)
        inter_log = b + m[:, None]
        m_row = jnp.maximum(inter_log, jnp.max(dmat, axis=2))
        s = jnp.einsum('bihk,bjhk->bijh', qc, kc) * jnp.exp(dmat - m_row[:, :, None])
        w_inter = jnp.exp(inter_log - m_row)
        num = jnp.einsum('bijh,bjhv->bihv', s, vc) + w_inter[..., None] * jnp.einsum('bhvk,bihk->bihv', C, qc)
        den = jnp.sum(s, axis=2) + w_inter * jnp.einsum('bhk,bihk->bih', n, qc)
        h = num / jnp.maximum(jnp.abs(den), jnp.exp(-m_row))[..., None]
        return (C_new, n_new, m_new), h

    state, hs = lax.scan(step, state0, (chunks(q), chunks(k), chunks(v), chunks(log_i), chunks(log_f)))
    if not want_y:
        return None, state
    return hs.swapaxes(0, 1).reshape(B, L, H, v.shape[-1]), state


def mlstm_mixer(cx, cc, gate_b, want_ctx):
    gb = gate_b.astype(F32)
    o1 = H_C * DQK_C

    def prep(t):
        Bt, Lt, _ = t.shape
        q, k, v, o, g = jnp.split(t, [o1, 2 * o1, 2 * o1 + D_MLSTM, 2 * o1 + 2 * D_MLSTM], axis=-1)
        q = q.astype(F32).reshape(Bt, Lt, H_C, DQK_C) * DQK_C ** -0.5
        k = k.astype(F32).reshape(Bt, Lt, H_C, DQK_C)
        v = v.astype(F32).reshape(Bt, Lt, H_C, DV_C)
        g = g.astype(F32).reshape(Bt, Lt, 2, 2, H_C) + gb
        li = g[:, :, :, 0]
        lf = jax.nn.log_sigmoid(g[:, :, :, 1])
        return q, k, v, o, li, lf

    def run(t, s_f, s_b, want_y):
        q, k, v, o, li, lf = prep(t)
        h_f, s_f_new = mlstm_chunked(q, k, v, li[:, :, 0], lf[:, :, 0], s_f, want_y)
        h_b, s_b_new = mlstm_chunked(flip(q), flip(k), flip(v), flip(li[:, :, 1]), flip(lf[:, :, 1]), s_b, want_y)
        if not want_y:
            return None, s_f_new, s_b_new
        h = rmsnorm(h_f + flip(h_b))
        y = h.reshape(t.shape[0], t.shape[1], D_MLSTM) * jax.nn.sigmoid(o.astype(F32))
        return y, s_f_new, s_b_new

    Bt = cc.shape[0]
    s0 = (jnp.zeros((Bt, H_C, DV_C, DQK_C), F32), jnp.zeros((Bt, H_C, DQK_C), F32),
          jnp.full((Bt, H_C), NEG_STATE, F32))
    y_c, sf_c, sb_c = run(cc, s0, s0, want_ctx)
    y_x, _, _ = run(cx, sf_c, sb_c, True)
    return y_x, y_c


def s5_discretise(lam_re, lam_im, log_dt, b_re, b_im):
    lr = lam_re.astype(F32)
    lim = lam_im.astype(F32)
    dt = jnp.exp(log_dt.astype(F32))[..., None]
    mag = jnp.exp(lr * dt)
    ar = mag * jnp.cos(lim * dt)
    ai = mag * jnp.sin(lim * dt)
    den = lr * lr + lim * lim
    cr = ((ar - 1.0) * lr + ai * lim) / den
    ci = (ai * lr - (ar - 1.0) * lim) / den
    br_, bi_ = b_re.astype(F32), b_im.astype(F32)
    bbar_re = cr[..., None] * br_ - ci[..., None] * bi_
    bbar_im = cr[..., None] * bi_ + ci[..., None] * br_
    return ar, ai, bbar_re, bbar_im


def complex_linear_scan(ar, ai, ur, ui, x0r, x0i):
    ur = ur.at[:, 0].add(ar * x0r - ai * x0i)
    ui = ui.at[:, 0].add(ar * x0i + ai * x0r)
    a_r = jnp.broadcast_to(ar, ur.shape)
    a_i = jnp.broadcast_to(ai, ur.shape)

    def combine(e1, e2):
        a1r, a1i, b1r, b1i = e1
        a2r, a2i, b2r, b2i = e2
        return (a2r * a1r - a2i * a1i, a2r * a1i + a2i * a1r,
                a2r * b1r - a2i * b1i + b2r, a2r * b1i + a2i * b1r + b2i)

    _, _, xr, xi = lax.associative_scan(combine, (a_r, a_i, ur, ui), axis=1)
    return xr, xi


def s5_mixer(dx, dc, lam_re, lam_im, log_dt, b_re, b_im, c_re, c_im, d_skip, glu_w, glu_b, want_ctx):
    ar, ai, bbr, bbi = s5_discretise(lam_re, lam_im, log_dt, b_re, b_im)
    cr = c_re.astype(F32)
    ci = c_im.astype(F32)

    def drive(u, d):
        ug = u.reshape(u.shape[0], u.shape[1], G_S5, S5_GROUP)
        return jnp.einsum('blgc,gpc->blgp', ug, bbr[d]), jnp.einsum('blgc,gpc->blgp', ug, bbi[d])

    def readout(xr, xi):
        y = jnp.einsum('gcp,blgp->blgc', cr, xr) - jnp.einsum('gcp,blgp->blgc', ci, xi)
        return y.reshape(y.shape[0], y.shape[1], D_S5)

    def run(u, s0f, s0b, want_y):
        u = u.astype(F32)
        ufr, ufi = drive(u, 0)
        ubr, ubi = drive(flip(u), 1)
        xfr, xfi = complex_linear_scan(ar[0], ai[0], ufr, ufi, s0f[0], s0f[1])
        xbr, xbi = complex_linear_scan(ar[1], ai[1], ubr, ubi, s0b[0], s0b[1])
        sf = (xfr[:, -1], xfi[:, -1])
        sb = (xbr[:, -1], xbi[:, -1])
        if not want_y:
            return None, sf, sb
        y = readout(xfr, xfi) + flip(readout(xbr, xbi)) + d_skip.astype(F32) * u
        y = jax.nn.gelu(y)
        y = y * jax.nn.sigmoid(y @ glu_w.astype(F32) + glu_b.astype(F32))
        return rmsnorm(y), sf, sb

    z0 = jnp.zeros((dc.shape[0], G_S5, P_S5), F32)
    y_c, sf_c, sb_c = run(dc, (z0, z0), (z0, z0), want_ctx)
    y_x, _, _ = run(dx, sf_c, sb_c, True)
    return y_x, y_c


def hybrid_mixer(hx, hc, lp, want_ctx):
    zx = hx @ lp["w_in"]
    zc = hc @ lp["w_in"]
    cuts = [A_COLS, A_COLS + B_COLS, A_COLS + B_COLS + C_COLS]
    ax, bx, cx, dx = jnp.split(zx, cuts, axis=-1)
    ac, bc, cc, dc = jnp.split(zc, cuts, axis=-1)
    ya_x, ya_c = mla_mixer(ax, ac, lp["mla_q_norm_w"], lp["mla_kv_norm_w"], lp["mla_w_uq"], lp["mla_w_ukv"], want_ctx)
    yb_x, yb_c = ssd_mixer(bx, bc, lp["ssd_conv_w"], lp["ssd_conv_b"], lp["ssd_a_log"], lp["ssd_dt_bias"], lp["ssd_d"], want_ctx)
    yc_x, yc_c = mlstm_mixer(cx, cc, lp["mlstm_gate_b"], want_ctx)
    yd_x, yd_c = s5_mixer(dx, dc, lp["s5_lam_re"], lp["s5_lam_im"], lp["s5_log_dt"], lp["s5_b_re"], lp["s5_b_im"],
                          lp["s5_c_re"], lp["s5_c_im"], lp["s5_d"], lp["s5_glu_w"], lp["s5_glu_b"], want_ctx)

    def merge(ya, yb, yc, yd):
        y = jnp.concatenate([ya.astype(F32), yb.astype(F32), yc.astype(F32), yd.astype(F32)], axis=-1)
        y = (y * lp["out_norm_w"].astype(F32)).astype(hx.dtype)
        return y @ lp["w_out"]

    y_x = merge(ya_x, yb_x, yc_x, yd_x)
    y_c = merge(ya_c, yb_c, yc_c, yd_c) if want_ctx else None
    return y_x, y_c


def hier_moe(h, w_group, w_expert, w_gate, w_up, w_down):
    T, D = h.shape
    hf = h.astype(F32)
    g_prob = jax.nn.softmax(hf @ w_group.astype(F32), axis=-1)
    g_w, g_idx = lax.top_k(g_prob, 1)
    e_logits = (hf @ w_expert.astype(F32)).reshape(T, N_GROUPS, EXPERTS_PER_GROUP)
    e_in = jnp.take_along_axis(e_logits, g_idx[:, :, None], axis=1)[:, 0]
    e_val, e_idx = lax.top_k(e_in, TOP_K)
    weights = (g_w * jax.nn.softmax(e_val, axis=-1)).reshape(-1)
    experts = (g_idx * EXPERTS_PER_GROUP + e_idx).reshape(-1).astype(jnp.int32)
    tokens = jnp.repeat(jnp.arange(T, dtype=jnp.int32), TOP_K)
    M = T * TOP_K
    order = jnp.argsort(experts)
    se, stok, sw = experts[order], tokens[order], weights[order]
    counts = jnp.bincount(experts, length=N_EXPERTS)
    starts = jnp.cumsum(counts) - counts
    pcounts = (counts + MOE_BLOCK - 1) // MOE_BLOCK * MOE_BLOCK
    pends = jnp.cumsum(pcounts)
    pstarts = pends - pcounts
    dest = pstarts[se] + (jnp.arange(M, dtype=jnp.int32) - starts[se])
    nb = (M + N_EXPERTS * (MOE_BLOCK - 1) + MOE_BLOCK - 1) // MOE_BLOCK
    buf_tok = jnp.full((nb * MOE_BLOCK,), T, jnp.int32).at[dest].set(stok)
    buf_w = jnp.zeros((nb * MOE_BLOCK,), F32).at[dest].set(sw)
    block_e = jnp.clip(jnp.searchsorted(pends, jnp.arange(nb, dtype=jnp.int32) * MOE_BLOCK, side='right'),
                       0, N_EXPERTS - 1)
    h_pad = jnp.concatenate([h, jnp.zeros((1, D), h.dtype)], axis=0)
    xb = h_pad[buf_tok].reshape(nb, MOE_BLOCK, D)

    def expert_block(args):
        xblk, e = args
        return (jax.nn.silu(xblk @ w_gate[e]) * (xblk @ w_up[e])) @ w_down[e]

    yb = lax.map(expert_block, (xb, block_e)).reshape(nb * MOE_BLOCK, D)
    out = jnp.zeros((T + 1, D), h.dtype).at[buf_tok].add((yb * buf_w[:, None]).astype(h.dtype))
    return out[:T]


def hybrid_layer(x, ctx, mod_x, mod_c, lp, last):
    B, L, D = x.shape
    sh1, sc1, g1, sh2, sc2, g2 = [mod_x[:, i][:, None] for i in range(6)]
    csh1, csc1, cg1, csh2, csc2, cg2 = [mod_c[i] for i in range(6)]
    hx = rmsnorm(x, lp["norm1_w"]) * (1 + sc1) + sh1
    hc = rmsnorm(ctx, lp["norm1_w"]) * (1 + csc1) + csh1
    yx, yc = hybrid_mixer(hx, hc, lp, not last)
    x = x + g1 * yx
    hx2 = rmsnorm(x, lp["norm2_w"]) * (1 + sc2) + sh2
    if last:
        f = hier_moe(hx2.reshape(B * L, D), lp["moe_w_group"], lp["moe_w_expert"], lp["moe_w_gate"],
                     lp["moe_w_up"], lp["moe_w_down"])
        return x + g2 * f.reshape(B, L, D), None
    ctx = ctx + cg1 * yc
    hc2 = rmsnorm(ctx, lp["norm2_w"]) * (1 + csc2) + csh2
    Lc = ctx.shape[1]
    f = hier_moe(jnp.concatenate([hx2.reshape(B * L, D), hc2.reshape(B * Lc, D)], axis=0),
                 lp["moe_w_group"], lp["moe_w_expert"], lp["moe_w_gate"], lp["moe_w_up"], lp["moe_w_down"])
    x = x + g2 * f[:B * L].reshape(B, L, D)
    ctx = ctx + cg2 * f[B * L:].reshape(B, Lc, D)
    return x, ctx


def setup_inputs(seed: int = 0) -> dict:
    key = jax.random.key(seed)
    keys = iter(jax.random.split(key, 64))

    def nrm(shape, scale):
        return jax.random.normal(next(keys), shape, F32) * scale

    def unif(shape, lo, hi):
        return jax.random.uniform(next(keys), shape, F32, lo, hi)

    x = nrm((BATCH, SEQ, D_MODEL), 1.0)
    c = nrm((BATCH, D_MODEL), 1.0)
    ctx = nrm((BATCH, CTX_LEN, D_MODEL), 1.0)
    c_ctx = nrm((D_MODEL,), 1.0)
    mod_w = nrm((DEPTH, D_MODEL, 6 * D_MODEL), 0.5 * D_MODEL ** -0.5)
    mod_b = nrm((DEPTH, 6 * D_MODEL), 0.02)
    norm1_w = 1.0 + nrm((DEPTH, D_MODEL), 0.02)
    w_in = nrm((DEPTH, D_MODEL, N_IN), D_MODEL ** -0.5)
    mla_q_norm_w = 1.0 + nrm((DEPTH, Q_RANK), 0.02)
    mla_kv_norm_w = 1.0 + nrm((DEPTH, KV_RANK), 0.02)
    mla_w_uq = nrm((DEPTH, Q_RANK, H_A * (D_NOPE + D_ROPE)), Q_RANK ** -0.5)
    mla_w_ukv = nrm((DEPTH, KV_RANK, H_A * (D_NOPE + D_V)), KV_RANK ** -0.5)
    ssd_conv_w = nrm((DEPTH, SSD_CONV, CONV_CH), SSD_CONV ** -0.5)
    ssd_conv_b = nrm((DEPTH, CONV_CH), 0.02)
    ssd_a_log = jnp.log(unif((DEPTH, 2, H_B), 1.0, 16.0))
    dt0 = jnp.exp(unif((DEPTH, 2, H_B), math.log(1e-3), math.log(1e-1)))
    ssd_dt_bias = dt0 + jnp.log(-jnp.expm1(-dt0))
    ssd_d = 1.0 + nrm((DEPTH, H_B), 0.1)
    i_b = nrm((DEPTH, 2, 1, H_C), 0.1)
    f_b = jnp.linspace(3.0, 6.0, H_C, dtype=F32) + nrm((DEPTH, 2, 1, H_C), 0.1)
    mlstm_gate_b = jnp.concatenate([i_b, f_b], axis=2)
    s5_lam_re = -0.5 + nrm((DEPTH, 2, G_S5, P_S5), 0.01)
    s5_lam_im = math.pi * jnp.arange(P_S5, dtype=F32) + nrm((DEPTH, 2, G_S5, P_S5), 0.01)
    s5_log_dt = unif((DEPTH, 2, G_S5), math.log(1e-3), math.log(1e-1))
    s5_b_re = nrm((DEPTH, G_S5, P_S5, S5_GROUP), (2 * S5_GROUP) ** -0.5)
    s5_b_im = nrm((DEPTH, G_S5, P_S5, S5_GROUP), (2 * S5_GROUP) ** -0.5)
    s5_c_re = nrm((DEPTH, G_S5, S5_GROUP, P_S5), (2 * P_S5) ** -0.5)
    s5_c_im = nrm((DEPTH, G_S5, S5_GROUP, P_S5), (2 * P_S5) ** -0.5)
    s5_d = nrm((DEPTH, D_S5), 1.0)
    s5_glu_w = nrm((DEPTH, D_S5, D_S5), D_S5 ** -0.5)
    s5_glu_b = nrm((DEPTH, D_S5), 0.02)
    out_norm_w = 1.0 + nrm((DEPTH, D_MIX), 0.02)
    w_out = nrm((DEPTH, D_MIX, D_MODEL), D_MIX ** -0.5)
    norm2_w = 1.0 + nrm((DEPTH, D_MODEL), 0.02)
    moe_w_group = nrm((DEPTH, D_MODEL, N_GROUPS), D_MODEL ** -0.5)
    moe_w_expert = nrm((DEPTH, D_MODEL, N_EXPERTS), D_MODEL ** -0.5)
    moe_w_gate = nrm((DEPTH, N_EXPERTS, D_MODEL, D_EXPERT), D_MODEL ** -0.5)
    moe_w_up = nrm((DEPTH, N_EXPERTS, D_MODEL, D_EXPERT), D_MODEL ** -0.5)
    moe_w_down = nrm((DEPTH, N_EXPERTS, D_EXPERT, D_MODEL), D_EXPERT ** -0.5)
    final_norm_w = 1.0 + nrm((D_MODEL,), 0.02)
    return {"x": x, "c": c, "ctx": ctx, "c_ctx": c_ctx, "mod_w": mod_w, "mod_b": mod_b,
            "norm1_w": norm1_w, "w_in": w_in, "mla_q_norm_w": mla_q_norm_w, "mla_kv_norm_w": mla_kv_norm_w,
            "mla_w_uq": mla_w_uq, "mla_w_ukv": mla_w_ukv, "ssd_conv_w": ssd_conv_w, "ssd_conv_b": ssd_conv_b,
            "ssd_a_log": ssd_a_log, "ssd_dt_bias": ssd_dt_bias, "ssd_d": ssd_d, "mlstm_gate_b": mlstm_gate_b,
            "s5_lam_re": s5_lam_re, "s5_lam_im": s5_lam_im, "s5_log_dt": s5_log_dt, "s5_b_re": s5_b_re,
            "s5_b_im": s5_b_im, "s5_c_re": s5_c_re, "s5_c_im": s5_c_im, "s5_d": s5_d, "s5_glu_w": s5_glu_w,
            "s5_glu_b": s5_glu_b, "out_norm_w": out_norm_w, "w_out": w_out, "norm2_w": norm2_w,
            "moe_w_group": moe_w_group, "moe_w_expert": moe_w_expert, "moe_w_gate": moe_w_gate,
            "moe_w_up": moe_w_up, "moe_w_down": moe_w_down, "final_norm_w": final_norm_w}


def reference(x, c, ctx, c_ctx, mod_w, mod_b, norm1_w, w_in, mla_q_norm_w, mla_kv_norm_w, mla_w_uq, mla_w_ukv,
              ssd_conv_w, ssd_conv_b, ssd_a_log, ssd_dt_bias, ssd_d, mlstm_gate_b, s5_lam_re, s5_lam_im,
              s5_log_dt, s5_b_re, s5_b_im, s5_c_re, s5_c_im, s5_d, s5_glu_w, s5_glu_b, out_norm_w, w_out,
              norm2_w, moe_w_group, moe_w_expert, moe_w_gate, moe_w_up, moe_w_down, final_norm_w):
    B = x.shape[0]
    for l in range(DEPTH):
        mod_x = (jax.nn.silu(c) @ mod_w[l] + mod_b[l]).reshape(B, 6, D_MODEL)
        mod_c = (jax.nn.silu(c_ctx) @ mod_w[l] + mod_b[l]).reshape(6, D_MODEL)
        lp = {"norm1_w": norm1_w[l], "w_in": w_in[l], "mla_q_norm_w": mla_q_norm_w[l],
              "mla_kv_norm_w": mla_kv_norm_w[l], "mla_w_uq": mla_w_uq[l], "mla_w_ukv": mla_w_ukv[l],
              "ssd_conv_w": ssd_conv_w[l], "ssd_conv_b": ssd_conv_b[l], "ssd_a_log": ssd_a_log[l],
              "ssd_dt_bias": ssd_dt_bias[l], "ssd_d": ssd_d[l], "mlstm_gate_b": mlstm_gate_b[l],
              "s5_lam_re": s5_lam_re[l], "s5_lam_im": s5_lam_im[l], "s5_log_dt": s5_log_dt[l],
              "s5_b_re": s5_b_re[l], "s5_b_im": s5_b_im[l], "s5_c_re": s5_c_re[l], "s5_c_im": s5_c_im[l],
              "s5_d": s5_d[l], "s5_glu_w": s5_glu_w[l], "s5_glu_b": s5_glu_b[l], "out_norm_w": out_norm_w[l],
              "w_out": w_out[l], "norm2_w": norm2_w[l], "moe_w_group": moe_w_group[l],
              "moe_w_expert": moe_w_expert[l], "moe_w_gate": moe_w_gate[l], "moe_w_up": moe_w_up[l],
              "moe_w_down": moe_w_down[l]}
        x, ctx = hybrid_layer(x, ctx, mod_x, mod_c, lp, l == DEPTH - 1)
    return rmsnorm(x, final_norm_w)
```

```python
import functools
import math

import jax
import jax.numpy as jnp
from jax import lax
from jax.experimental import pallas as pl
from jax.experimental.pallas import tpu as pltpu

F32 = jnp.float32
BF16 = jnp.bfloat16

D_MODEL = 2048
BATCH = 2
SEQ = 4096
DEPTH = 2
GRID_W = 64
CTX_LEN = 256
EPS = 1e-6
NEG_STATE = -1e30

H_A = 4
D_NOPE = 128
D_ROPE = 64
D_V = 128
Q_RANK = 384
KV_RANK = 128
ROPE_AXIS = D_ROPE // 2
ROPE_BASE = 10000.0
D_MLA = H_A * D_V
D_SSD = 512
P_B = 64
H_B = D_SSD // P_B
G_B = 2
N_B = 128
SSD_CONV = 5
SSD_CHUNK = 128
CONV_CH = D_SSD + 2 * G_B * N_B
D_MLSTM = 512
H_C = 4
DV_C = D_MLSTM // H_C
DQK_C = DV_C // 2
MLSTM_CHUNK = 128
D_S5 = 512
S5_GROUP = 16
G_S5 = D_S5 // S5_GROUP
P_S5 = 64
A_COLS = Q_RANK + KV_RANK + D_ROPE
B_COLS = D_SSD + CONV_CH + 2 * H_B
C_COLS = 2 * H_C * DQK_C + 2 * D_MLSTM + 4 * H_C
N_GROUPS = 4
EXPERTS_PER_GROUP = 8
N_EXPERTS = N_GROUPS * EXPERTS_PER_GROUP
TOP_K = 2
D_EXPERT = 512

T_X = BATCH * SEQ
T_C = BATCH * CTX_LEN
T_ALL = T_X + T_C

LANES = 128
VMEM_LIMIT_BYTES = 56 * 1024 * 1024

ROW_TILE = 512
TILES_PER_BATCH = SEQ // ROW_TILE
N_X_TILES = T_X // ROW_TILE

Z_CQ = 0
Z_CKV = 384
Z_KR = 512
Z_KRR = 640
Z_SZ = 768
Z_XBC = 1280
Z_DT = 2304
Z_MQ = 2432
Z_MK = 2688
Z_MV = 2944
Z_MO = 3456
Z_MG = 3968
Z_D = 4096
Z_COLS = 4608

MOE_ROWS = 256
ROUTER_COLS = 128


def _cparams(*sem):
    return pltpu.CompilerParams(dimension_semantics=sem, vmem_limit_bytes=VMEM_LIMIT_BYTES)


def _mod_row(i):
    return jnp.minimum(i // TILES_PER_BATCH, BATCH)


def _mod_kernel(a_ref, w_ref, b_ref, o_ref):
    a = a_ref[...]
    a = a * jax.nn.sigmoid(a)
    o_ref[...] = jnp.dot(a.astype(BF16), w_ref[...].astype(BF16), preferred_element_type=F32) + b_ref[...]


def _modulation(cc, w, b):
    n = w.shape[1]
    tn = 1024
    return pl.pallas_call(
        _mod_kernel,
        grid=(n // tn,),
        in_specs=[pl.BlockSpec((8, D_MODEL), lambda j: (0, 0)),
                  pl.BlockSpec((D_MODEL, tn), lambda j: (0, j)),
                  pl.BlockSpec((1, tn), lambda j: (0, j))],
        out_specs=pl.BlockSpec((8, tn), lambda j: (0, j)),
        out_shape=jax.ShapeDtypeStruct((8, n), F32),
        compiler_params=_cparams("arbitrary"),
        name="modulation",
    )(cc, w, b.reshape(1, n))


def _prenorm_kernel(x_ref, w_ref, mod_ref, o_ref, *, sh_idx, sc_idx):
    x = x_ref[...]
    y = x * lax.rsqrt(jnp.mean(x * x, axis=-1, keepdims=True) + EPS) * w_ref[...]
    m = mod_ref[0]
    y = y * (1.0 + m[sc_idx:sc_idx + 1, :]) + m[sh_idx:sh_idx + 1, :]
    o_ref[...] = y.astype(o_ref.dtype)


def _prenorm(x, w, modtab, sh_idx, sc_idx):
    rows = x.shape[0]
    return pl.pallas_call(
        functools.partial(_prenorm_kernel, sh_idx=sh_idx, sc_idx=sc_idx),
        grid=(rows // ROW_TILE,),
        in_specs=[pl.BlockSpec((ROW_TILE, D_MODEL), lambda i: (i, 0)),
                  pl.BlockSpec((1, D_MODEL), lambda i: (0, 0)),
                  pl.BlockSpec((1, 6, D_MODEL), lambda i: (_mod_row(i), 0, 0))],
        out_specs=pl.BlockSpec((ROW_TILE, D_MODEL), lambda i: (i, 0)),
        out_shape=jax.ShapeDtypeStruct((rows, D_MODEL), BF16),
        compiler_params=_cparams("arbitrary"),
        name="prenorm",
    )(x, w.reshape(1, D_MODEL), modtab)


def _prenorm_router_kernel(x_ref, w_ref, mod_ref, wr_ref, o_ref, lg_ref, *, sh_idx, sc_idx):
    x = x_ref[...]
    y = x * lax.rsqrt(jnp.mean(x * x, axis=-1, keepdims=True) + EPS) * w_ref[...]
    m = mod_ref[0]
    y = y * (1.0 + m[sc_idx:sc_idx + 1, :]) + m[sh_idx:sh_idx + 1, :]
    o_ref[...] = y.astype(o_ref.dtype)
    lg_ref[...] = jnp.dot(y, wr_ref[...], preferred_element_type=F32, precision=lax.Precision.HIGHEST)


def _prenorm_router(x, w, modtab, w_router, sh_idx, sc_idx):
    rows = x.shape[0]
    return pl.pallas_call(
        functools.partial(_prenorm_router_kernel, sh_idx=sh_idx, sc_idx=sc_idx),
        grid=(rows // ROW_TILE,),
        in_specs=[pl.BlockSpec((ROW_TILE, D_MODEL), lambda i: (i, 0)),
                  pl.BlockSpec((1, D_MODEL), lambda i: (0, 0)),
                  pl.BlockSpec((1, 6, D_MODEL), lambda i: (_mod_row(i), 0, 0)),
                  pl.BlockSpec((D_MODEL, ROUTER_COLS), lambda i: (0, 0))],
        out_specs=[pl.BlockSpec((ROW_TILE, D_MODEL), lambda i: (i, 0)),
                   pl.BlockSpec((ROW_TILE, ROUTER_COLS), lambda i: (i, 0))],
        out_shape=[jax.ShapeDtypeStruct((rows, D_MODEL), BF16),
                   jax.ShapeDtypeStruct((rows, ROUTER_COLS), F32)],
        compiler_params=_cparams("arbitrary"),
        name="prenorm_router",
    )(x, w.reshape(1, D_MODEL), modtab, w_router)


def _mm_kernel(a_ref, w_ref, o_ref):
    o_ref[...] = jnp.dot(a_ref[...], w_ref[...], preferred_element_type=F32).astype(o_ref.dtype)


def _mm(a, w, tm, tn, out_dtype=F32):
    m, k = a.shape
    n = w.shape[1]
    return pl.pallas_call(
        _mm_kernel,
        grid=(n // tn, m // tm),
        in_specs=[pl.BlockSpec((tm, k), lambda j, i: (i, 0)),
                  pl.BlockSpec((k, tn), lambda j, i: (0, j))],
        out_specs=pl.BlockSpec((tm, tn), lambda j, i: (i, j)),
        out_shape=jax.ShapeDtypeStruct((m, n), out_dtype),
        compiler_params=_cparams("arbitrary", "arbitrary"),
        name="in_proj",
    )(a, w)


def _rms(x, w=None):
    y = x * lax.rsqrt(jnp.mean(x * x, axis=-1, keepdims=True) + EPS)
    return y if w is None else y * w


def _mla_prep_kernel(za_ref, cs_ref, qw_ref, kvw_ref, wqa_ref, wqb_ref, wk_ref, wv_ref, q_ref, k_ref, v_ref):
    za = za_ref[...]
    cos = cs_ref[:, :LANES]
    sin = cs_ref[:, LANES:]
    qn = _rms(za[:, Z_CQ:Z_CQ + Q_RANK], qw_ref[...]).astype(BF16)
    kvn = _rms(za[:, Z_CKV:Z_CKV + KV_RANK], kvw_ref[...]).astype(BF16)
    qa = jnp.dot(qn, wqa_ref[...], preferred_element_type=F32)
    qb = jnp.dot(qn, wqb_ref[...], preferred_element_type=F32)
    kn = jnp.dot(kvn, wk_ref[...], preferred_element_type=F32)
    v = jnp.dot(kvn, wv_ref[...], preferred_element_type=F32)
    kr = (za[:, Z_KR:Z_KR + LANES] * cos + za[:, Z_KRR:Z_KRR + LANES] * sin).astype(BF16)
    for h in range(H_A):
        c0 = h * 2 * LANES
        q_ref[:, c0:c0 + LANES] = qa[:, c0:c0 + LANES].astype(BF16)
        q_ref[:, c0 + LANES:c0 + 2 * LANES] = (
            qa[:, c0 + LANES:c0 + 2 * LANES] * cos + qb[:, h * LANES:(h + 1) * LANES] * sin).astype(BF16)
        k_ref[:, c0:c0 + LANES] = kn[:, h * LANES:(h + 1) * LANES].astype(BF16)
        k_ref[:, c0 + LANES:c0 + 2 * LANES] = kr
    v_ref[...] = v.astype(BF16)


def _mla_prep(z, cs, qw, kvw, wqa, wqb, wk, wv):
    za_w = Z_SZ
    const = lambda i: (0, 0)
    return pl.pallas_call(
        _mla_prep_kernel,
        grid=(T_ALL // ROW_TILE,),
        in_specs=[pl.BlockSpec((ROW_TILE, za_w), lambda i: (i, 0)),
                  pl.BlockSpec((ROW_TILE, 2 * LANES),
                               lambda i: (jnp.where(i < N_X_TILES, i % TILES_PER_BATCH, TILES_PER_BATCH), 0)),
                  pl.BlockSpec((1, Q_RANK), const),
                  pl.BlockSpec((1, KV_RANK), const),
                  pl.BlockSpec(wqa.shape, const),
                  pl.BlockSpec(wqb.shape, const),
                  pl.BlockSpec(wk.shape, const),
                  pl.BlockSpec(wv.shape, const)],
        out_specs=[pl.BlockSpec((ROW_TILE, H_A * 2 * LANES), lambda i: (i, 0)),
                   pl.BlockSpec((ROW_TILE, H_A * 2 * LANES), lambda i: (i, 0)),
                   pl.BlockSpec((ROW_TILE, D_MLA), lambda i: (i, 0))],
        out_shape=[jax.ShapeDtypeStruct((T_ALL, H_A * 2 * LANES), BF16),
                   jax.ShapeDtypeStruct((T_ALL, H_A * 2 * LANES), BF16),
                   jax.ShapeDtypeStruct((T_ALL, D_MLA), BF16)],
        compiler_params=_cparams("arbitrary"),
        name="mla_prep",
    )(z, cs, qw.reshape(1, Q_RANK), kvw.reshape(1, KV_RANK), wqa, wqb, wk, wv)


_NT = (((1,), (1,)), ((), ()))


def _attn_kernel(*refs, n_seg):
    q_ref = refs[0]
    k_refs = refs[1:1 + n_seg]
    v_refs = refs[1 + n_seg:1 + 2 * n_seg]
    w_ref = refs[1 + 2 * n_seg]
    o_ref = refs[2 + 2 * n_seg]
    acc_ref = refs[3 + 2 * n_seg]
    scale = (D_NOPE + D_ROPE) ** -0.5
    for h in range(H_A):
        q = q_ref[:, h * 2 * LANES:(h + 1) * 2 * LANES]
        ss = [lax.dot_general(q, k[:, h * 2 * LANES:(h + 1) * 2 * LANES], _NT, preferred_element_type=F32)
              for k in k_refs]
        m = functools.reduce(jnp.maximum, [jnp.max(s, axis=-1, keepdims=True) for s in ss])
        ps = [jnp.exp((s - m) * scale) for s in ss]
        l = functools.reduce(lambda a, b: a + b, [jnp.sum(p, axis=-1, keepdims=True) for p in ps])
        o = functools.reduce(lambda a, b: a + b, [
            jnp.dot(p.astype(BF16), v[:, h * D_V:(h + 1) * D_V], preferred_element_type=F32)
            for p, v in zip(ps, v_refs)])
        acc_ref[:, h * D_V:(h + 1) * D_V] = o / l
    o_ref[...] = (_rms(acc_ref[...]) * w_ref[...]).astype(o_ref.dtype)


ATTN_TQ = 256


def _attention_latent(q, k, v, onw):
    qt = SEQ // ATTN_TQ
    kw = H_A * 2 * LANES
    return pl.pallas_call(
        functools.partial(_attn_kernel, n_seg=2),
        grid=(BATCH, qt),
        in_specs=[pl.BlockSpec((ATTN_TQ, kw), lambda b, i: (b * qt + i, 0)),
                  pl.BlockSpec((SEQ, kw), lambda b, i: (b, 0)),
                  pl.BlockSpec((CTX_LEN, kw), lambda b, i: (T_X // CTX_LEN + b, 0)),
                  pl.BlockSpec((SEQ, D_MLA), lambda b, i: (b, 0)),
                  pl.BlockSpec((CTX_LEN, D_MLA), lambda b, i: (T_X // CTX_LEN + b, 0)),
                  pl.BlockSpec((1, D_MLA), lambda b, i: (0, 0))],
        out_specs=pl.BlockSpec((ATTN_TQ, D_MLA), lambda b, i: (b * qt + i, 0)),
        out_shape=jax.ShapeDtypeStruct((T_X, D_MLA), BF16),
        scratch_shapes=[pltpu.VMEM((ATTN_TQ, D_MLA), F32)],
        compiler_params=_cparams("arbitrary", "arbitrary"),
        name="attn_latent",
    )(q, k, k, v, v, onw)


def _attention_ctx(q, k, v, onw):
    kw = H_A * 2 * LANES
    blk = lambda b: (T_X // CTX_LEN + b, 0)
    return pl.pallas_call(
        functools.partial(_attn_kernel, n_seg=1),
        grid=(BATCH,),
        in_specs=[pl.BlockSpec((CTX_LEN, kw), blk),
                  pl.BlockSpec((CTX_LEN, kw), blk),
                  pl.BlockSpec((CTX_LEN, D_MLA), blk),
                  pl.BlockSpec((1, D_MLA), lambda b: (0, 0))],
        out_specs=pl.BlockSpec((CTX_LEN, D_MLA), lambda b: (b, 0)),
        out_shape=jax.ShapeDtypeStruct((T_C, D_MLA), BF16),
        scratch_shapes=[pltpu.VMEM((CTX_LEN, D_MLA), F32)],
        compiler_params=_cparams("arbitrary"),
        name="attn_ctx",
    )(q, k, v, onw)


def _outproj_kernel(a0, a1, a2, a3, w0, w1, w2, w3, x_ref, mod_ref, o_ref, *, g_idx):
    acc = jnp.dot(a0[...], w0[...], preferred_element_type=F32)
    acc += jnp.dot(a1[...], w1[...], preferred_element_type=F32)
    acc += jnp.dot(a2[...], w2[...], preferred_element_type=F32)
    acc += jnp.dot(a3[...], w3[...], preferred_element_type=F32)
    g = mod_ref[0][g_idx:g_idx + 1, :]
    o_ref[...] = x_ref[...] + g * acc


def _outproj(ys, w, x, modtab, g_idx):
    kq = D_MODEL // 4
    a_specs = [pl.BlockSpec((ROW_TILE, kq), lambda i: (i, 0)) for _ in range(4)]
    w_specs = [pl.BlockSpec((kq, D_MODEL), functools.partial(lambda i, r: (r, 0), r=r)) for r in range(4)]
    return pl.pallas_call(
        functools.partial(_outproj_kernel, g_idx=g_idx),
        grid=(T_ALL // ROW_TILE,),
        in_specs=a_specs + w_specs + [
            pl.BlockSpec((ROW_TILE, D_MODEL), lambda i: (i, 0)),
            pl.BlockSpec((1, 6, D_MODEL), lambda i: (_mod_row(i), 0, 0))],
        out_specs=pl.BlockSpec((ROW_TILE, D_MODEL), lambda i: (i, 0)),
        out_shape=jax.ShapeDtypeStruct((T_ALL, D_MODEL), F32),
        compiler_params=_cparams("arbitrary"),
        name="out_proj",
    )(*ys, w, w, w, w, x, modtab)


def _moe_kernel(be_ref, first_ref, valid_ref, x_ref, wg_ref, wu_ref, wd_ref, o_ref, wg_s, wu_s, wd_s):
    i = pl.program_id(0)

    @pl.when(first_ref[i] == 1)
    def _():
        wg_s[...] = wg_ref[0].astype(BF16)
        wu_s[...] = wu_ref[0].astype(BF16)
        wd_s[...] = wd_ref[0].astype(BF16)

    @pl.when(valid_ref[i] == 1)
    def _():
        x = x_ref[...]
        g = jnp.dot(x, wg_s[...], preferred_element_type=F32)
        u = jnp.dot(x, wu_s[...], preferred_element_type=F32)
        h = (g * jax.nn.sigmoid(g) * u).astype(BF16)
        o_ref[...] = jnp.dot(h, wd_s[...], preferred_element_type=F32)

    @pl.when(valid_ref[i] == 0)
    def _():
        o_ref[...] = jnp.zeros_like(o_ref)


def _moe_experts(xb, block_e, first, valid, wg, wu, wd):
    nb = xb.shape[0] // MOE_ROWS
    grid_spec = pltpu.PrefetchScalarGridSpec(
        num_scalar_prefetch=3,
        grid=(nb,),
        in_specs=[pl.BlockSpec((MOE_ROWS, D_MODEL), lambda i, be, fi, va: (i, 0)),
                  pl.BlockSpec((1, D_MODEL, D_EXPERT), lambda i, be, fi, va: (be[i], 0, 0)),
                  pl.BlockSpec((1, D_MODEL, D_EXPERT), lambda i, be, fi, va: (be[i], 0, 0)),
                  pl.BlockSpec((1, D_EXPERT, D_MODEL), lambda i, be, fi, va: (be[i], 0, 0))],
        out_specs=pl.BlockSpec((MOE_ROWS, D_MODEL), lambda i, be, fi, va: (i, 0)),
        scratch_shapes=[pltpu.VMEM((D_MODEL, D_EXPERT), BF16),
                        pltpu.VMEM((D_MODEL, D_EXPERT), BF16),
                        pltpu.VMEM((D_EXPERT, D_MODEL), BF16)])
    return pl.pallas_call(
        _moe_kernel,
        grid_spec=grid_spec,
        out_shape=jax.ShapeDtypeStruct((nb * MOE_ROWS, D_MODEL), F32),
        compiler_params=_cparams("arbitrary"),
        name="moe_experts",
    )(block_e, first, valid, xb, wg, wu, wd)


def _moe(h_bf16, logits, wg, wu, wd):
    t = h_bf16.shape[0]
    g_prob = jax.nn.softmax(logits[:, :N_GROUPS], axis=-1)
    g_w, g_idx = lax.top_k(g_prob, 1)
    e_logits = logits[:, N_GROUPS:N_GROUPS + N_EXPERTS].reshape(t, N_GROUPS, EXPERTS_PER_GROUP)
    e_in = jnp.take_along_axis(e_logits, g_idx[:, :, None], axis=1)[:, 0]
    e_val, e_idx = lax.top_k(e_in, TOP_K)
    weights = g_w * jax.nn.softmax(e_val, axis=-1)
    experts = (g_idx * EXPERTS_PER_GROUP + e_idx).astype(jnp.int32)
    m = t * TOP_K
    flat_e = experts.reshape(-1)
    onehot = (flat_e[:, None] == jnp.arange(N_EXPERTS, dtype=jnp.int32)[None, :]).astype(jnp.int32)
    csum = jnp.cumsum(onehot, axis=0)
    rank = jnp.take_along_axis(csum, flat_e[:, None], axis=1)[:, 0] - 1
    counts = csum[-1]
    pcounts = (counts + MOE_ROWS - 1) // MOE_ROWS * MOE_ROWS
    pends = jnp.cumsum(pcounts)
    pstarts = pends - pcounts
    dest = pstarts[flat_e] + rank
    nb = (m + N_EXPERTS * (MOE_ROWS - 1) + MOE_ROWS - 1) // MOE_ROWS
    tokens = jnp.repeat(jnp.arange(t, dtype=jnp.int32), TOP_K)
    buf_tok = jnp.zeros((nb * MOE_ROWS,), jnp.int32).at[dest].set(tokens)
    bstart = jnp.arange(nb, dtype=jnp.int32) * MOE_ROWS
    block_e = jnp.clip(jnp.searchsorted(pends, bstart, side='right'), 0, N_EXPERTS - 1).astype(jnp.int32)
    valid = (bstart < pends[-1]).astype(jnp.int32)
    first = jnp.concatenate([jnp.ones((1,), jnp.int32), (block_e[1:] != block_e[:-1]).astype(jnp.int32)])
    xb = h_bf16[buf_tok]
    yb = _moe_experts(xb, block_e, first, valid, wg, wu, wd)
    d2 = dest.reshape(t, TOP_K)
    return yb[d2[:, 0]] * weights[:, 0:1] + yb[d2[:, 1]] * weights[:, 1:2]


def _final_norm_kernel(x_ref, w_ref, o_ref):
    o_ref[...] = _rms(x_ref[...], w_ref[...])


def _final_norm(x, w):
    rows = x.shape[0]
    return pl.pallas_call(
        _final_norm_kernel,
        grid=(rows // ROW_TILE,),
        in_specs=[pl.BlockSpec((ROW_TILE, D_MODEL), lambda i: (i, 0)),
                  pl.BlockSpec((1, D_MODEL), lambda i: (0, 0))],
        out_specs=pl.BlockSpec((ROW_TILE, D_MODEL), lambda i: (i, 0)),
        out_shape=jax.ShapeDtypeStruct((rows, D_MODEL), F32),
        compiler_params=_cparams("arbitrary"),
        name="final_norm",
    )(x, w.reshape(1, D_MODEL))


def _split_streams(a):
    return a[:T_X].reshape(BATCH, SEQ, -1), a[T_X:].reshape(BATCH, CTX_LEN, -1)


def _merge_streams(ax, ac):
    return jnp.concatenate([ax.reshape(T_X, -1), ac.reshape(T_C, -1)], axis=0)


def _flip(t):
    return jnp.flip(t, axis=1)


def _ssd_chunked(x, dt, a_h, bm, cm, h0):
    b, l, h, p = x.shape
    nc = l // SSD_CHUNK
    rep = h // bm.shape[2]
    bh = jnp.repeat(bm, rep, axis=2).reshape(b, nc, SSD_CHUNK, h, -1)
    ch = jnp.repeat(cm, rep, axis=2).reshape(b, nc, SSD_CHUNK, h, -1)
    xc = x.reshape(b, nc, SSD_CHUNK, h, p)
    dtc = dt.reshape(b, nc, SSD_CHUNK, h)
    acum = jnp.cumsum(dtc * a_h, axis=2)
    a_tot = acum[:, :, -1]
    w_end = jnp.exp(a_tot[:, :, None] - acum) * dtc
    s_loc = jnp.einsum('bcqh,bcqhn,bcqhp->bchpn', w_end, bh, xc)

    def step(hs, inp):
        s, dec = inp
        return jnp.exp(dec)[:, :, None, None] * hs + s, hs

    h_fin, h_start = lax.scan(step, h0, (s_loc.swapaxes(0, 1), a_tot.swapaxes(0, 1)))
    h_start = h_start.swapaxes(0, 1)
    causal = jnp.tril(jnp.ones((SSD_CHUNK, SSD_CHUNK), dtype=bool))
    seg = acum[:, :, :, None, :] - acum[:, :, None, :, :]
    decay = jnp.exp(jnp.where(causal[None, None, :, :, None], seg, -jnp.inf))
    scores = jnp.einsum('bcihn,bcjhn->bcijh', ch, bh) * decay
    y = (jnp.einsum('bcijh,bcjh,bcjhp->bcihp', scores, dtc, xc)
         + jnp.exp(acum)[..., None] * jnp.einsum('bcihn,bchpn->bcihp', ch, h_start))
    return y.reshape(b, l, h, p), h_fin


def _ssd_mixer(z, conv_w, conv_b, a_log, dt_bias, d_skip):
    a_neg = -jnp.exp(a_log.astype(F32))
    gate_x, gate_c = _split_streams(z[:, Z_SZ:Z_SZ + D_SSD])
    xbc_x, xbc_c = _split_streams(z[:, Z_XBC:Z_XBC + CONV_CH])
    dt_x, dt_c = _split_streams(z[:, Z_DT:Z_DT + 2 * H_B])

    def prep(xbc, dt):
        bt, lt, c = xbc.shape
        k = conv_w.shape[0]
        y = lax.conv_general_dilated(xbc, conv_w[:, None, :], window_strides=(1,),
                                     padding=[((k - 1) // 2, (k - 1) // 2)],
                                     dimension_numbers=('NWC', 'WIO', 'NWC'), feature_group_count=c) + conv_b
        y = jax.nn.silu(y)
        xs = y[..., :D_SSD].reshape(bt, lt, H_B, P_B)
        bm = y[..., D_SSD:D_SSD + G_B * N_B].reshape(bt, lt, G_B, N_B)
        cm = y[..., D_SSD + G_B * N_B:].reshape(bt, lt, G_B, N_B)
        dts = jax.nn.softplus(dt.reshape(bt, lt, 2, H_B) + dt_bias)
        return xs, bm, cm, dts[:, :, 0], dts[:, :, 1]

    def run(xbc, dt, gate, h0f, h0b):
        xs, bm, cm, dtf, dtb = prep(xbc, dt)
        yf, hf = _ssd_chunked(xs, dtf, a_neg[0], bm, cm, h0f)
        yb, hb = _ssd_chunked(_flip(xs), _flip(dtb), a_neg[1], _flip(bm), _flip(cm), h0b)
        y = yf + _flip(yb) + d_skip[:, None] * xs
        y = y.reshape(xbc.shape[0], xbc.shape[1], D_SSD) * jax.nn.silu(gate)
        return _rms(y), hf, hb

    h0 = jnp.zeros((BATCH, H_B, P_B, N_B), F32)
    y_c, hf_c, hb_c = run(xbc_c, dt_c, gate_c, h0, h0)
    y_x, _, _ = run(xbc_x, dt_x, gate_x, hf_c, hb_c)
    return _merge_streams(y_x, y_c)


def _mlstm_chunked(q, k, v, log_i, log_f, state0):
    b, l, h, _ = q.shape
    nc = l // MLSTM_CHUNK

    def chunks(t):
        return t.reshape((b, nc, MLSTM_CHUNK) + t.shape[2:]).swapaxes(0, 1)

    causal = jnp.tril(jnp.ones((MLSTM_CHUNK, MLSTM_CHUNK), dtype=bool))[None, :, :, None]

    def step(carry, inp):
        c_st, n_st, m_st = carry
        qc, kc, vc, li, lf = inp
        bcum = jnp.cumsum(lf, axis=1)
        g = bcum[:, -1]
        w_log = g[:, None] - bcum + li
        m_new = jnp.maximum(g + m_st, jnp.max(w_log, axis=1))
        wj = jnp.exp(w_log - m_new[:, None])
        dec = jnp.exp(g + m_st - m_new)
        c_new = dec[:, :, None, None] * c_st + jnp.einsum('bjh,bjhv,bjhk->bhvk', wj, vc, kc)
        n_new = dec[:, :, None] * n_st + jnp.einsum('bjh,bjhk->bhk', wj, kc)
        dmat = jnp.where(causal, bcum[:, :, None] - bcum[:, None] + li[:, None], -jnp.inf)
        inter_log = bcum + m_st[:, None]
        m_row = jnp.maximum(inter_log, jnp.max(dmat, axis=2))
        s = jnp.einsum('bihk,bjhk->bijh', qc, kc) * jnp.exp(dmat - m_row[:, :, None])
        w_inter = jnp.exp(inter_log - m_row)
        num = jnp.einsum('bijh,bjhv->bihv', s, vc) + w_inter[..., None] * jnp.einsum('bhvk,bihk->bihv', c_st, qc)
        den = jnp.sum(s, axis=2) + w_inter * jnp.einsum('bhk,bihk->bih', n_st, qc)
        hh = num / jnp.maximum(jnp.abs(den), jnp.exp(-m_row))[..., None]
        return (c_new, n_new, m_new), hh

    state, hs = lax.scan(step, state0, (chunks(q), chunks(k), chunks(v), chunks(log_i), chunks(log_f)))
    return hs.swapaxes(0, 1).reshape(b, l, h, v.shape[-1]), state


def _mlstm_mixer(z, gate_b):
    q_x, q_c = _split_streams(z[:, Z_MQ:Z_MQ + H_C * DQK_C])
    k_x, k_c = _split_streams(z[:, Z_MK:Z_MK + H_C * DQK_C])
    v_x, v_c = _split_streams(z[:, Z_MV:Z_MV + D_MLSTM])
    o_x, o_c = _split_streams(z[:, Z_MO:Z_MO + D_MLSTM])
    g_x, g_c = _split_streams(z[:, Z_MG:Z_MG + 4 * H_C])

    def run(q, k, v, o, g, s_f, s_b):
        bt, lt, _ = q.shape
        q = q.reshape(bt, lt, H_C, DQK_C) * DQK_C ** -0.5
        k = k.reshape(bt, lt, H_C, DQK_C)
        v = v.reshape(bt, lt, H_C, DV_C)
        g = g.reshape(bt, lt, 2, 2, H_C) + gate_b
        li = g[:, :, :, 0]
        lf = jax.nn.log_sigmoid(g[:, :, :, 1])
        h_f, s_f_new = _mlstm_chunked(q, k, v, li[:, :, 0], lf[:, :, 0], s_f)
        h_b, s_b_new = _mlstm_chunked(_flip(q), _flip(k), _flip(v), _flip(li[:, :, 1]), _flip(lf[:, :, 1]), s_b)
        hh = _rms(h_f + _flip(h_b))
        y = hh.reshape(bt, lt, D_MLSTM) * jax.nn.sigmoid(o)
        return y, s_f_new, s_b_new

    s0 = (jnp.zeros((BATCH, H_C, DV_C, DQK_C), F32), jnp.zeros((BATCH, H_C, DQK_C), F32),
          jnp.full((BATCH, H_C), NEG_STATE, F32))
    y_c, sf_c, sb_c = run(q_c, k_c, v_c, o_c, g_c, s0, s0)
    y_x, _, _ = run(q_x, k_x, v_x, o_x, g_x, sf_c, sb_c)
    return _merge_streams(y_x, y_c)


def _complex_scan(ar, ai, ur, ui, x0r, x0i):
    ur = ur.at[:, 0].add(ar * x0r - ai * x0i)
    ui = ui.at[:, 0].add(ar * x0i + ai * x0r)
    a_r = jnp.broadcast_to(ar, ur.shape)
    a_i = jnp.broadcast_to(ai, ur.shape)

    def combine(e1, e2):
        a1r, a1i, b1r, b1i = e1
        a2r, a2i, b2r, b2i = e2
        return (a2r * a1r - a2i * a1i, a2r * a1i + a2i * a1r,
                a2r * b1r - a2i * b1i + b2r, a2r * b1i + a2i * b1r + b2i)

    _, _, xr, xi = lax.associative_scan(combine, (a_r, a_i, ur, ui), axis=1)
    return xr, xi


def _s5_mixer(z, lam_re, lam_im, log_dt, b_re, b_im, c_re, c_im, d_skip, glu_w, glu_b):
    dt = jnp.exp(log_dt)[..., None]
    mag = jnp.exp(lam_re * dt)
    ar = mag * jnp.cos(lam_im * dt)
    ai = mag * jnp.sin(lam_im * dt)
    den = lam_re * lam_re + lam_im * lam_im
    cr_ = ((ar - 1.0) * lam_re + ai * lam_im) / den
    ci_ = (ai * lam_re - (ar - 1.0) * lam_im) / den
    bbr = cr_[..., None] * b_re - ci_[..., None] * b_im
    bbi = cr_[..., None] * b_im + ci_[..., None] * b_re
    u_x, u_c = _split_streams(z[:, Z_D:Z_D + D_S5])

    def drive(u, d):
        ug = u.reshape(u.shape[0], u.shape[1], G_S5, S5_GROUP)
        return jnp.einsum('blgc,gpc->blgp', ug, bbr[d]), jnp.einsum('blgc,gpc->blgp', ug, bbi[d])

    def readout(xr, xi):
        y = jnp.einsum('gcp,blgp->blgc', c_re, xr) - jnp.einsum('gcp,blgp->blgc', c_im, xi)
        return y.reshape(y.shape[0], y.shape[1], D_S5)

    def run(u, s0f, s0b):
        ufr, ufi = drive(u, 0)
        ubr, ubi = drive(_flip(u), 1)
        xfr, xfi = _complex_scan(ar[0], ai[0], ufr, ufi, s0f[0], s0f[1])
        xbr, xbi = _complex_scan(ar[1], ai[1], ubr, ubi, s0b[0], s0b[1])
        sf = (xfr[:, -1], xfi[:, -1])
        sb = (xbr[:, -1], xbi[:, -1])
        y = readout(xfr, xfi) + _flip(readout(xbr, xbi)) + d_skip * u
        y = jax.nn.gelu(y)
        y = y * jax.nn.sigmoid(y @ glu_w + glu_b)
        return _rms(y), sf, sb

    z0 = jnp.zeros((BATCH, G_S5, P_S5), F32)
    y_c, sf_c, sb_c = run(u_c, (z0, z0), (z0, z0))
    y_x, _, _ = run(u_x, sf_c, sb_c)
    return _merge_streams(y_x, y_c)


def _rope_tables():
    pos = jnp.arange(SEQ)
    row = (pos // GRID_W).astype(F32)
    col = (pos % GRID_W).astype(F32)
    inv_freq = ROPE_BASE ** (-jnp.arange(ROPE_AXIS // 2, dtype=F32) * 2.0 / ROPE_AXIS)
    ang_r = row[:, None] * inv_freq
    ang_c = col[:, None] * inv_freq
    zeros = jnp.zeros((SEQ, LANES - D_ROPE), F32)
    cos = jnp.concatenate([jnp.cos(ang_r), jnp.cos(ang_r), jnp.cos(ang_c), jnp.cos(ang_c), zeros], axis=1)
    sin = jnp.concatenate([jnp.sin(ang_r), jnp.sin(ang_r), jnp.sin(ang_c), jnp.sin(ang_c), zeros], axis=1)
    cos_c = jnp.concatenate([jnp.ones((ROW_TILE, D_ROPE), F32), jnp.zeros((ROW_TILE, LANES - D_ROPE), F32)], axis=1)
    sin_c = jnp.zeros((ROW_TILE, LANES), F32)
    return jnp.concatenate([jnp.concatenate([cos, sin], axis=1), jnp.concatenate([cos_c, sin_c], axis=1)], axis=0)


def _rot_cols(w):
    q = ROPE_AXIS // 2
    return jnp.concatenate([-w[:, q:2 * q], w[:, 0:q], -w[:, 3 * q:4 * q], w[:, 2 * q:3 * q]], axis=1)


def _layout_w_in(w):
    k = w.shape[0]
    zpad = lambda n: jnp.zeros((k, n), w.dtype)
    b0 = A_COLS
    c0 = A_COLS + B_COLS
    d0 = A_COLS + B_COLS + C_COLS
    kr = w[:, Q_RANK + KV_RANK:A_COLS]
    o1 = H_C * DQK_C
    cols = [w[:, 0:Q_RANK + KV_RANK], kr, zpad(LANES - D_ROPE), _rot_cols(kr), zpad(LANES - D_ROPE),
            w[:, b0:b0 + D_SSD + CONV_CH], w[:, b0 + D_SSD + CONV_CH:c0], zpad(LANES - 2 * H_B),
            w[:, c0:c0 + 2 * o1 + 2 * D_MLSTM], w[:, c0 + 2 * o1 + 2 * D_MLSTM:d0], zpad(LANES - 4 * H_C),
            w[:, d0:]]
    out = jnp.concatenate(cols, axis=1)
    assert out.shape[1] == Z_COLS
    return out.astype(BF16)


def _layout_mla(w_uq, w_ukv):
    k = w_uq.shape[0]
    qa, qb, wk, wv = [], [], [], []
    for h in range(H_A):
        base = h * (D_NOPE + D_ROPE)
        rope = w_uq[:, base + D_NOPE:base + D_NOPE + D_ROPE]
        qa += [w_uq[:, base:base + D_NOPE], rope, jnp.zeros((k, LANES - D_ROPE), w_uq.dtype)]
        qb += [_rot_cols(rope), jnp.zeros((k, LANES - D_ROPE), w_uq.dtype)]
        kb = h * (D_NOPE + D_V)
        wk.append(w_ukv[:, kb:kb + D_NOPE])
        wv.append(w_ukv[:, kb + D_NOPE:kb + D_NOPE + D_V])
    cat = lambda xs: jnp.concatenate(xs, axis=1).astype(BF16)
    return cat(qa), cat(qb), cat(wk), cat(wv)


def _layer(xall, modtab, p, cs, last):
    hx = _prenorm(xall, p["norm1_w"], modtab, 0, 1)
    z = _mm(hx, _layout_w_in(p["w_in"]), ROW_TILE, Z_COLS // 3)
    onw = p["out_norm_w"]
    wqa, wqb, wk, wv = _layout_mla(p["mla_w_uq"], p["mla_w_ukv"])
    q, k, v = _mla_prep(z, cs, p["mla_q_norm_w"], p["mla_kv_norm_w"], wqa, wqb, wk, wv)
    onw_a = onw[:D_MLA].reshape(1, D_MLA)
    ya = jnp.concatenate([_attention_latent(q, k, v, onw_a), _attention_ctx(q, k, v, onw_a)], axis=0)
    yb = (_ssd_mixer(z, p["ssd_conv_w"], p["ssd_conv_b"], p["ssd_a_log"], p["ssd_dt_bias"], p["ssd_d"])
          * onw[D_MLA:D_MLA + D_SSD]).astype(BF16)
    yc = (_mlstm_mixer(z, p["mlstm_gate_b"]) * onw[D_MLA + D_SSD:D_MLA + D_SSD + D_MLSTM]).astype(BF16)
    yd = (_s5_mixer(z, p["s5_lam_re"], p["s5_lam_im"], p["s5_log_dt"], p["s5_b_re"], p["s5_b_im"],
                    p["s5_c_re"], p["s5_c_im"], p["s5_d"], p["s5_glu_w"], p["s5_glu_b"])
          * onw[D_MLA + D_SSD + D_MLSTM:]).astype(BF16)
    xall = _outproj([ya, yb, yc, yd], p["w_out"].astype(BF16), xall, modtab, 2)

    w_router = jnp.concatenate([p["moe_w_group"], p["moe_w_expert"],
                                jnp.zeros((D_MODEL, ROUTER_COLS - N_GROUPS - N_EXPERTS), F32)], axis=1)
    n_tok = T_X if last else T_ALL
    h2, logits = _prenorm_router(xall[:n_tok], p["norm2_w"], modtab, w_router, 3, 4)
    f = _moe(h2, logits, p["moe_w_gate"], p["moe_w_up"], p["moe_w_down"])
    g2_rows = jnp.concatenate([jnp.repeat(modtab[:BATCH, 5], SEQ, axis=0),
                               jnp.broadcast_to(modtab[BATCH, 5], (T_C, D_MODEL))], axis=0)[:n_tok]
    return xall[:n_tok] + g2_rows * f


def kernel(x, c, ctx, c_ctx, mod_w, mod_b, norm1_w, w_in, mla_q_norm_w, mla_kv_norm_w, mla_w_uq, mla_w_ukv,
           ssd_conv_w, ssd_conv_b, ssd_a_log, ssd_dt_bias, ssd_d, mlstm_gate_b, s5_lam_re, s5_lam_im,
           s5_log_dt, s5_b_re, s5_b_im, s5_c_re, s5_c_im, s5_d, s5_glu_w, s5_glu_b, out_norm_w, w_out,
           norm2_w, moe_w_group, moe_w_expert, moe_w_gate, moe_w_up, moe_w_down, final_norm_w):
    stacked = {"norm1_w": norm1_w, "w_in": w_in, "mla_q_norm_w": mla_q_norm_w, "mla_kv_norm_w": mla_kv_norm_w,
               "mla_w_uq": mla_w_uq, "mla_w_ukv": mla_w_ukv, "ssd_conv_w": ssd_conv_w, "ssd_conv_b": ssd_conv_b,
               "ssd_a_log": ssd_a_log, "ssd_dt_bias": ssd_dt_bias, "ssd_d": ssd_d, "mlstm_gate_b": mlstm_gate_b,
               "s5_lam_re": s5_lam_re, "s5_lam_im": s5_lam_im, "s5_log_dt": s5_log_dt, "s5_b_re": s5_b_re,
               "s5_b_im": s5_b_im, "s5_c_re": s5_c_re, "s5_c_im": s5_c_im, "s5_d": s5_d, "s5_glu_w": s5_glu_w,
               "s5_glu_b": s5_glu_b, "out_norm_w": out_norm_w, "w_out": w_out, "norm2_w": norm2_w,
               "moe_w_group": moe_w_group, "moe_w_expert": moe_w_expert, "moe_w_gate": moe_w_gate,
               "moe_w_up": moe_w_up, "moe_w_down": moe_w_down}
    cs = _rope_tables()
    cc = jnp.concatenate([c, c_ctx[None, :], jnp.zeros((8 - BATCH - 1, D_MODEL), F32)], axis=0)
    xall = jnp.concatenate([x.reshape(T_X, D_MODEL), ctx.reshape(T_C, D_MODEL)], axis=0)
    for l in range(DEPTH):
        p = {name: val[l] for name, val in stacked.items()}
        modtab = _modulation(cc, mod_w[l], mod_b[l])[:BATCH + 1].reshape(BATCH + 1, 6, D_MODEL)
        xall = _layer(xall, modtab, p, cs, l == DEPTH - 1)
    return _final_norm(xall, final_norm_w).reshape(BATCH, SEQ, D_MODEL)
```

```python
import functools
import math

import jax
import jax.numpy as jnp
from jax import lax
from jax.experimental import pallas as pl
from jax.experimental.pallas import tpu as pltpu

F32 = jnp.float32
BF16 = jnp.bfloat16

D_MODEL = 2048
BATCH = 2
SEQ = 4096
DEPTH = 2
GRID_W = 64
CTX_LEN = 256
EPS = 1e-6
NEG_STATE = -1e30

H_A = 4
D_NOPE = 128
D_ROPE = 64
D_V = 128
Q_RANK = 384
KV_RANK = 128
ROPE_AXIS = D_ROPE // 2
ROPE_BASE = 10000.0
D_MLA = H_A * D_V
D_SSD = 512
P_B = 64
H_B = D_SSD // P_B
G_B = 2
N_B = 128
SSD_CONV = 5
SSD_CHUNK = 128
CONV_CH = D_SSD + 2 * G_B * N_B
D_MLSTM = 512
H_C = 4
DV_C = D_MLSTM // H_C
DQK_C = DV_C // 2
MLSTM_CHUNK = 128
D_S5 = 512
S5_GROUP = 16
G_S5 = D_S5 // S5_GROUP
P_S5 = 64
A_COLS = Q_RANK + KV_RANK + D_ROPE
B_COLS = D_SSD + CONV_CH + 2 * H_B
C_COLS = 2 * H_C * DQK_C + 2 * D_MLSTM + 4 * H_C
N_GROUPS = 4
EXPERTS_PER_GROUP = 8
N_EXPERTS = N_GROUPS * EXPERTS_PER_GROUP
TOP_K = 2
D_EXPERT = 512

T_X = BATCH * SEQ
T_C = BATCH * CTX_LEN
T_ALL = T_X + T_C

LANES = 128
VMEM_LIMIT_BYTES = 56 * 1024 * 1024

ROW_TILE = 512
TILES_PER_BATCH = SEQ // ROW_TILE
N_X_TILES = T_X // ROW_TILE

Z_CQ = 0
Z_CKV = 384
Z_KR = 512
Z_KRR = 640
Z_SZ = 768
Z_XBC = 1280
Z_DT = 2304
Z_MQ = 2432
Z_MK = 2688
Z_MV = 2944
Z_MO = 3456
Z_MG = 3968
Z_D = 4096
Z_COLS = 4608

MOE_ROWS = 256
ROUTER_COLS = 128


def _cparams(*sem):
    return pltpu.CompilerParams(dimension_semantics=sem, vmem_limit_bytes=VMEM_LIMIT_BYTES)


def _mod_row(i):
    return jnp.minimum(i // TILES_PER_BATCH, BATCH)


def _mod_kernel(a_ref, w_ref, b_ref, o_ref):
    a = a_ref[...]
    a = a * jax.nn.sigmoid(a)
    o_ref[...] = jnp.dot(a.astype(BF16), w_ref[...].astype(BF16), preferred_element_type=F32) + b_ref[...]


def _modulation(cc, w, b):
    n = w.shape[1]
    tn = 1024
    return pl.pallas_call(
        _mod_kernel,
        grid=(n // tn,),
        in_specs=[pl.BlockSpec((8, D_MODEL), lambda j: (0, 0)),
                  pl.BlockSpec((D_MODEL, tn), lambda j: (0, j)),
                  pl.BlockSpec((1, tn), lambda j: (0, j))],
        out_specs=pl.BlockSpec((8, tn), lambda j: (0, j)),
        out_shape=jax.ShapeDtypeStruct((8, n), F32),
        compiler_params=_cparams("arbitrary"),
        name="modulation",
    )(cc, w, b.reshape(1, n))


def _prenorm_kernel(x_ref, w_ref, mod_ref, o_ref, *, sh_idx, sc_idx):
    x = x_ref[...]
    y = x * lax.rsqrt(jnp.mean(x * x, axis=-1, keepdims=True) + EPS) * w_ref[...]
    m = mod_ref[0]
    y = y * (1.0 + m[sc_idx:sc_idx + 1, :]) + m[sh_idx:sh_idx + 1, :]
    o_ref[...] = y.astype(o_ref.dtype)


def _prenorm(x, w, modtab, sh_idx, sc_idx):
    rows = x.shape[0]
    return pl.pallas_call(
        functools.partial(_prenorm_kernel, sh_idx=sh_idx, sc_idx=sc_idx),
        grid=(rows // ROW_TILE,),
        in_specs=[pl.BlockSpec((ROW_TILE, D_MODEL), lambda i: (i, 0)),
                  pl.BlockSpec((1, D_MODEL), lambda i: (0, 0)),
                  pl.BlockSpec((1, 6, D_MODEL), lambda i: (_mod_row(i), 0, 0))],
        out_specs=pl.BlockSpec((ROW_TILE, D_MODEL), lambda i: (i, 0)),
        out_shape=jax.ShapeDtypeStruct((rows, D_MODEL), BF16),
        compiler_params=_cparams("arbitrary"),
        name="prenorm",
    )(x, w.reshape(1, D_MODEL), modtab)


def _prenorm_router_kernel(x_ref, w_ref, mod_ref, wr_ref, o_ref, lg_ref, *, sh_idx, sc_idx):
    x = x_ref[...]
    y = x * lax.rsqrt(jnp.mean(x * x, axis=-1, keepdims=True) + EPS) * w_ref[...]
    m = mod_ref[0]
    y = y * (1.0 + m[sc_idx:sc_idx + 1, :]) + m[sh_idx:sh_idx + 1, :]
    o_ref[...] = y.astype(o_ref.dtype)
    lg_ref[...] = jnp.dot(y, wr_ref[...], preferred_element_type=F32, precision=lax.Precision.HIGHEST)


def _prenorm_router(x, w, modtab, w_router, sh_idx, sc_idx):
    rows = x.shape[0]
    return pl.pallas_call(
        functools.partial(_prenorm_router_kernel, sh_idx=sh_idx, sc_idx=sc_idx),
        grid=(rows // ROW_TILE,),
        in_specs=[pl.BlockSpec((ROW_TILE, D_MODEL), lambda i: (i, 0)),
                  pl.BlockSpec((1, D_MODEL), lambda i: (0, 0)),
                  pl.BlockSpec((1, 6, D_MODEL), lambda i: (_mod_row(i), 0, 0)),
                  pl.BlockSpec((D_MODEL, ROUTER_COLS), lambda i: (0, 0))],
        out_specs=[pl.BlockSpec((ROW_TILE, D_MODEL), lambda i: (i, 0)),
                   pl.BlockSpec((ROW_TILE, ROUTER_COLS), lambda i: (i, 0))],
        out_shape=[jax.ShapeDtypeStruct((rows, D_MODEL), BF16),
                   jax.ShapeDtypeStruct((rows, ROUTER_COLS), F32)],
        compiler_params=_cparams("arbitrary"),
        name="prenorm_router",
    )(x, w.reshape(1, D_MODEL), modtab, w_router)


def _mm_kernel(a_ref, w_ref, o_ref):
    o_ref[...] = jnp.dot(a_ref[...], w_ref[...], preferred_element_type=F32).astype(o_ref.dtype)


def _mm(a, w, tm, tn, out_dtype=F32):
    m, k = a.shape
    n = w.shape[1]
    return pl.pallas_call(
        _mm_kernel,
        grid=(n // tn, m // tm),
        in_specs=[pl.BlockSpec((tm, k), lambda j, i: (i, 0)),
                  pl.BlockSpec((k, tn), lambda j, i: (0, j))],
        out_specs=pl.BlockSpec((tm, tn), lambda j, i: (i, j)),
        out_shape=jax.ShapeDtypeStruct((m, n), out_dtype),
        compiler_params=_cparams("arbitrary", "arbitrary"),
        name="in_proj",
    )(a, w)


def _rms(x, w=None):
    y = x * lax.rsqrt(jnp.mean(x * x, axis=-1, keepdims=True) + EPS)
    return y if w is None else y * w


def _mla_prep_kernel(za_ref, cs_ref, qw_ref, kvw_ref, wqa_ref, wqb_ref, wk_ref, wv_ref, q_ref, k_ref, v_ref):
    za = za_ref[...]
    cos = cs_ref[:, :LANES]
    sin = cs_ref[:, LANES:]
    qn = _rms(za[:, Z_CQ:Z_CQ + Q_RANK], qw_ref[...]).astype(BF16)
    kvn = _rms(za[:, Z_CKV:Z_CKV + KV_RANK], kvw_ref[...]).astype(BF16)
    qa = jnp.dot(qn, wqa_ref[...], preferred_element_type=F32)
    qb = jnp.dot(qn, wqb_ref[...], preferred_element_type=F32)
    kn = jnp.dot(kvn, wk_ref[...], preferred_element_type=F32)
    v = jnp.dot(kvn, wv_ref[...], preferred_element_type=F32)
    kr = (za[:, Z_KR:Z_KR + LANES] * cos + za[:, Z_KRR:Z_KRR + LANES] * sin).astype(BF16)
    for h in range(H_A):
        c0 = h * 2 * LANES
        q_ref[:, c0:c0 + LANES] = qa[:, c0:c0 + LANES].astype(BF16)
        q_ref[:, c0 + LANES:c0 + 2 * LANES] = (
            qa[:, c0 + LANES:c0 + 2 * LANES] * cos + qb[:, h * LANES:(h + 1) * LANES] * sin).astype(BF16)
        k_ref[:, c0:c0 + LANES] = kn[:, h * LANES:(h + 1) * LANES].astype(BF16)
        k_ref[:, c0 + LANES:c0 + 2 * LANES] = kr
    v_ref[...] = v.astype(BF16)


def _mla_prep(z, cs, qw, kvw, wqa, wqb, wk, wv):
    za_w = Z_SZ
    const = lambda i: (0, 0)
    return pl.pallas_call(
        _mla_prep_kernel,
        grid=(T_ALL // ROW_TILE,),
        in_specs=[pl.BlockSpec((ROW_TILE, za_w), lambda i: (i, 0)),
                  pl.BlockSpec((ROW_TILE, 2 * LANES),
                               lambda i: (jnp.where(i < N_X_TILES, i % TILES_PER_BATCH, TILES_PER_BATCH), 0)),
                  pl.BlockSpec((1, Q_RANK), const),
                  pl.BlockSpec((1, KV_RANK), const),
                  pl.BlockSpec(wqa.shape, const),
                  pl.BlockSpec(wqb.shape, const),
                  pl.BlockSpec(wk.shape, const),
                  pl.BlockSpec(wv.shape, const)],
        out_specs=[pl.BlockSpec((ROW_TILE, H_A * 2 * LANES), lambda i: (i, 0)),
                   pl.BlockSpec((ROW_TILE, H_A * 2 * LANES), lambda i: (i, 0)),
                   pl.BlockSpec((ROW_TILE, D_MLA), lambda i: (i, 0))],
        out_shape=[jax.ShapeDtypeStruct((T_ALL, H_A * 2 * LANES), BF16),
                   jax.ShapeDtypeStruct((T_ALL, H_A * 2 * LANES), BF16),
                   jax.ShapeDtypeStruct((T_ALL, D_MLA), BF16)],
        compiler_params=_cparams("arbitrary"),
        name="mla_prep",
    )(z, cs, qw.reshape(1, Q_RANK), kvw.reshape(1, KV_RANK), wqa, wqb, wk, wv)


_NT = (((1,), (1,)), ((), ()))


def _attn_kernel(*refs, n_seg):
    q_ref = refs[0]
    k_refs = refs[1:1 + n_seg]
    v_refs = refs[1 + n_seg:1 + 2 * n_seg]
    w_ref = refs[1 + 2 * n_seg]
    o_ref = refs[2 + 2 * n_seg]
    acc_ref = refs[3 + 2 * n_seg]
    scale = (D_NOPE + D_ROPE) ** -0.5
    for h in range(H_A):
        q = q_ref[:, h * 2 * LANES:(h + 1) * 2 * LANES]
        ss = [lax.dot_general(q, k[:, h * 2 * LANES:(h + 1) * 2 * LANES], _NT, preferred_element_type=F32)
              for k in k_refs]
        m = functools.reduce(jnp.maximum, [jnp.max(s, axis=-1, keepdims=True) for s in ss])
        ps = [jnp.exp((s - m) * scale) for s in ss]
        l = functools.reduce(lambda a, b: a + b, [jnp.sum(p, axis=-1, keepdims=True) for p in ps])
        o = functools.reduce(lambda a, b: a + b, [
            jnp.dot(p.astype(BF16), v[:, h * D_V:(h + 1) * D_V], preferred_element_type=F32)
            for p, v in zip(ps, v_refs)])
        acc_ref[:, h * D_V:(h + 1) * D_V] = o / l
    o_ref[...] = (_rms(acc_ref[...]) * w_ref[...]).astype(o_ref.dtype)


ATTN_TQ = 256


def _attention_latent(q, k, v, onw):
    qt = SEQ // ATTN_TQ
    kw = H_A * 2 * LANES
    return pl.pallas_call(
        functools.partial(_attn_kernel, n_seg=2),
        grid=(BATCH, qt),
        in_specs=[pl.BlockSpec((ATTN_TQ, kw), lambda b, i: (b * qt + i, 0)),
                  pl.BlockSpec((SEQ, kw), lambda b, i: (b, 0)),
                  pl.BlockSpec((CTX_LEN, kw), lambda b, i: (T_X // CTX_LEN + b, 0)),
                  pl.BlockSpec((SEQ, D_MLA), lambda b, i: (b, 0)),
                  pl.BlockSpec((CTX_LEN, D_MLA), lambda b, i: (T_X // CTX_LEN + b, 0)),
                  pl.BlockSpec((1, D_MLA), lambda b, i: (0, 0))],
        out_specs=pl.BlockSpec((ATTN_TQ, D_MLA), lambda b, i: (b * qt + i, 0)),
        out_shape=jax.ShapeDtypeStruct((T_X, D_MLA), BF16),
        scratch_shapes=[pltpu.VMEM((ATTN_TQ, D_MLA), F32)],
        compiler_params=_cparams("arbitrary", "arbitrary"),
        name="attn_latent",
    )(q, k, k, v, v, onw)


def _attention_ctx(q, k, v, onw):
    kw = H_A * 2 * LANES
    blk = lambda b: (T_X // CTX_LEN + b, 0)
    return pl.pallas_call(
        functools.partial(_attn_kernel, n_seg=1),
        grid=(BATCH,),
        in_specs=[pl.BlockSpec((CTX_LEN, kw), blk),
                  pl.BlockSpec((CTX_LEN, kw), blk),
                  pl.BlockSpec((CTX_LEN, D_MLA), blk),
                  pl.BlockSpec((1, D_MLA), lambda b: (0, 0))],
        out_specs=pl.BlockSpec((CTX_LEN, D_MLA), lambda b: (b, 0)),
        out_shape=jax.ShapeDtypeStruct((T_C, D_MLA), BF16),
        scratch_shapes=[pltpu.VMEM((CTX_LEN, D_MLA), F32)],
        compiler_params=_cparams("arbitrary"),
        name="attn_ctx",
    )(q, k, v, onw)


def _outproj_kernel(a0, a1, a2, a3, w0, w1, w2, w3, x_ref, mod_ref, o_ref, *, g_idx):
    acc = jnp.dot(a0[...], w0[...], preferred_element_type=F32)
    acc += jnp.dot(a1[...], w1[...], preferred_element_type=F32)
    acc += jnp.dot(a2[...], w2[...], preferred_element_type=F32)
    acc += jnp.dot(a3[...], w3[...], preferred_element_type=F32)
    g = mod_ref[0][g_idx:g_idx + 1, :]
    o_ref[...] = x_ref[...] + g * acc


def _outproj(ys, w, x, modtab, g_idx):
    kq = D_MODEL // 4
    a_specs = [pl.BlockSpec((ROW_TILE, kq), lambda i: (i, 0)) for _ in range(4)]
    w_specs = [pl.BlockSpec((kq, D_MODEL), functools.partial(lambda i, r: (r, 0), r=r)) for r in range(4)]
    return pl.pallas_call(
        functools.partial(_outproj_kernel, g_idx=g_idx),
        grid=(T_ALL // ROW_TILE,),
        in_specs=a_specs + w_specs + [
            pl.BlockSpec((ROW_TILE, D_MODEL), lambda i: (i, 0)),
            pl.BlockSpec((1, 6, D_MODEL), lambda i: (_mod_row(i), 0, 0))],
        out_specs=pl.BlockSpec((ROW_TILE, D_MODEL), lambda i: (i, 0)),
        out_shape=jax.ShapeDtypeStruct((T_ALL, D_MODEL), F32),
        compiler_params=_cparams("arbitrary"),
        name="out_proj",
    )(*ys, w, w, w, w, x, modtab)


def _moe_kernel(be_ref, first_ref, valid_ref, x_ref, wg_ref, wu_ref, wd_ref, o_ref, wg_s, wu_s, wd_s):
    i = pl.program_id(0)

    @pl.when(first_ref[i] == 1)
    def _():
        wg_s[...] = wg_ref[0].astype(BF16)
        wu_s[...] = wu_ref[0].astype(BF16)
        wd_s[...] = wd_ref[0].astype(BF16)

    @pl.when(valid_ref[i] == 1)
    def _():
        x = x_ref[...]
        g = jnp.dot(x, wg_s[...], preferred_element_type=F32)
        u = jnp.dot(x, wu_s[...], preferred_element_type=F32)
        h = (g * jax.nn.sigmoid(g) * u).astype(BF16)
        o_ref[...] = jnp.dot(h, wd_s[...], preferred_element_type=F32)

    @pl.when(valid_ref[i] == 0)
    def _():
        o_ref[...] = jnp.zeros_like(o_ref)


def _moe_experts(xb, block_e, first, valid, wg, wu, wd):
    nb = xb.shape[0] // MOE_ROWS
    grid_spec = pltpu.PrefetchScalarGridSpec(
        num_scalar_prefetch=3,
        grid=(nb,),
        in_specs=[pl.BlockSpec((MOE_ROWS, D_MODEL), lambda i, be, fi, va: (i, 0)),
                  pl.BlockSpec((1, D_MODEL, D_EXPERT), lambda i, be, fi, va: (be[i], 0, 0)),
                  pl.BlockSpec((1, D_MODEL, D_EXPERT), lambda i, be, fi, va: (be[i], 0, 0)),
                  pl.BlockSpec((1, D_EXPERT, D_MODEL), lambda i, be, fi, va: (be[i], 0, 0))],
        out_specs=pl.BlockSpec((MOE_ROWS, D_MODEL), lambda i, be, fi, va: (i, 0)),
        scratch_shapes=[pltpu.VMEM((D_MODEL, D_EXPERT), BF16),
                        pltpu.VMEM((D_MODEL, D_EXPERT), BF16),
                        pltpu.VMEM((D_EXPERT, D_MODEL), BF16)])
    return pl.pallas_call(
        _moe_kernel,
        grid_spec=grid_spec,
        out_shape=jax.ShapeDtypeStruct((nb * MOE_ROWS, D_MODEL), F32),
        compiler_params=_cparams("arbitrary"),
        name="moe_experts",
    )(block_e, first, valid, xb, wg, wu, wd)


def _moe(h_bf16, logits, wg, wu, wd):
    t = h_bf16.shape[0]
    g_prob = jax.nn.softmax(logits[:, :N_GROUPS], axis=-1)
    g_w, g_idx = lax.top_k(g_prob, 1)
    e_logits = logits[:, N_GROUPS:N_GROUPS + N_EXPERTS].reshape(t, N_GROUPS, EXPERTS_PER_GROUP)
    e_in = jnp.take_along_axis(e_logits, g_idx[:, :, None], axis=1)[:, 0]
    e_val, e_idx = lax.top_k(e_in, TOP_K)
    weights = g_w * jax.nn.softmax(e_val, axis=-1)
    experts = (g_idx * EXPERTS_PER_GROUP + e_idx).astype(jnp.int32)
    m = t * TOP_K
    flat_e = experts.reshape(-1)
    onehot = (flat_e[:, None] == jnp.arange(N_EXPERTS, dtype=jnp.int32)[None, :]).astype(jnp.int32)
    csum = jnp.cumsum(onehot, axis=0)
    rank = jnp.take_along_axis(csum, flat_e[:, None], axis=1)[:, 0] - 1
    counts = csum[-1]
    pcounts = (counts + MOE_ROWS - 1) // MOE_ROWS * MOE_ROWS
    pends = jnp.cumsum(pcounts)
    pstarts = pends - pcounts
    dest = pstarts[flat_e] + rank
    nb = (m + N_EXPERTS * (MOE_ROWS - 1) + MOE_ROWS - 1) // MOE_ROWS
    tokens = jnp.repeat(jnp.arange(t, dtype=jnp.int32), TOP_K)
    buf_tok = jnp.zeros((nb * MOE_ROWS,), jnp.int32).at[dest].set(tokens)
    bstart = jnp.arange(nb, dtype=jnp.int32) * MOE_ROWS
    block_e = jnp.clip(jnp.searchsorted(pends, bstart, side='right'), 0, N_EXPERTS - 1).astype(jnp.int32)
    valid = (bstart < pends[-1]).astype(jnp.int32)
    first = jnp.concatenate([jnp.ones((1,), jnp.int32), (block_e[1:] != block_e[:-1]).astype(jnp.int32)])
    xb = h_bf16[buf_tok]
    yb = _moe_experts(xb, block_e, first, valid, wg, wu, wd)
    d2 = dest.reshape(t, TOP_K)
    return yb[d2[:, 0]] * weights[:, 0:1] + yb[d2[:, 1]] * weights[:, 1:2]


def _final_norm_kernel(x_ref, w_ref, o_ref):
    o_ref[...] = _rms(x_ref[...], w_ref[...])


def _final_norm(x, w):
    rows = x.shape[0]
    return pl.pallas_call(
        _final_norm_kernel,
        grid=(rows // ROW_TILE,),
        in_specs=[pl.BlockSpec((ROW_TILE, D_MODEL), lambda i: (i, 0)),
                  pl.BlockSpec((1, D_MODEL), lambda i: (0, 0))],
        out_specs=pl.BlockSpec((ROW_TILE, D_MODEL), lambda i: (i, 0)),
        out_shape=jax.ShapeDtypeStruct((rows, D_MODEL), F32),
        compiler_params=_cparams("arbitrary"),
        name="final_norm",
    )(x, w.reshape(1, D_MODEL))


def _split_streams(a):
    return a[:T_X].reshape(BATCH, SEQ, -1), a[T_X:].reshape(BATCH, CTX_LEN, -1)


def _merge_streams(ax, ac):
    return jnp.concatenate([ax.reshape(T_X, -1), ac.reshape(T_C, -1)], axis=0)


def _flip(t):
    return jnp.flip(t, axis=1)


def _ssd_chunked(x, dt, a_h, bm, cm, h0):
    b, l, h, p = x.shape
    nc = l // SSD_CHUNK
    rep = h // bm.shape[2]
    bh = jnp.repeat(bm, rep, axis=2).reshape(b, nc, SSD_CHUNK, h, -1)
    ch = jnp.repeat(cm, rep, axis=2).reshape(b, nc, SSD_CHUNK, h, -1)
    xc = x.reshape(b, nc, SSD_CHUNK, h, p)
    dtc = dt.reshape(b, nc, SSD_CHUNK, h)
    acum = jnp.cumsum(dtc * a_h, axis=2)
    a_tot = acum[:, :, -1]
    w_end = jnp.exp(a_tot[:, :, None] - acum) * dtc
    s_loc = jnp.einsum('bcqh,bcqhn,bcqhp->bchpn', w_end, bh, xc)

    def step(hs, inp):
        s, dec = inp
        return jnp.exp(dec)[:, :, None, None] * hs + s, hs

    h_fin, h_start = lax.scan(step, h0, (s_loc.swapaxes(0, 1), a_tot.swapaxes(0, 1)))
    h_start = h_start.swapaxes(0, 1)
    causal = jnp.tril(jnp.ones((SSD_CHUNK, SSD_CHUNK), dtype=bool))
    seg = acum[:, :, :, None, :] - acum[:, :, None, :, :]
    decay = jnp.exp(jnp.where(causal[None, None, :, :, None], seg, -jnp.inf))
    scores = jnp.einsum('bcihn,bcjhn->bcijh', ch, bh) * decay
    y = (jnp.einsum('bcijh,bcjh,bcjhp->bcihp', scores, dtc, xc)
         + jnp.exp(acum)[..., None] * jnp.einsum('bcihn,bchpn->bcihp', ch, h_start))
    return y.reshape(b, l, h, p), h_fin


def _ssd_mixer(z, conv_w, conv_b, a_log, dt_bias, d_skip):
    a_neg = -jnp.exp(a_log.astype(F32))
    gate_x, gate_c = _split_streams(z[:, Z_SZ:Z_SZ + D_SSD])
    xbc_x, xbc_c = _split_streams(z[:, Z_XBC:Z_XBC + CONV_CH])
    dt_x, dt_c = _split_streams(z[:, Z_DT:Z_DT + 2 * H_B])

    def prep(xbc, dt):
        bt, lt, c = xbc.shape
        k = conv_w.shape[0]
        y = lax.conv_general_dilated(xbc, conv_w[:, None, :], window_strides=(1,),
                                     padding=[((k - 1) // 2, (k - 1) // 2)],
                                     dimension_numbers=('NWC', 'WIO', 'NWC'), feature_group_count=c) + conv_b
        y = jax.nn.silu(y)
        xs = y[..., :D_SSD].reshape(bt, lt, H_B, P_B)
        bm = y[..., D_SSD:D_SSD + G_B * N_B].reshape(bt, lt, G_B, N_B)
        cm = y[..., D_SSD + G_B * N_B:].reshape(bt, lt, G_B, N_B)
        dts = jax.nn.softplus(dt.reshape(bt, lt, 2, H_B) + dt_bias)
        return xs, bm, cm, dts[:, :, 0], dts[:, :, 1]

    def run(xbc, dt, gate, h0f, h0b):
        xs, bm, cm, dtf, dtb = prep(xbc, dt)
        yf, hf = _ssd_chunked(xs, dtf, a_neg[0], bm, cm, h0f)
        yb, hb = _ssd_chunked(_flip(xs), _flip(dtb), a_neg[1], _flip(bm), _flip(cm), h0b)
        y = yf + _flip(yb) + d_skip[:, None] * xs
        y = y.reshape(xbc.shape[0], xbc.shape[1], D_SSD) * jax.nn.silu(gate)
        return _rms(y), hf, hb

    h0 = jnp.zeros((BATCH, H_B, P_B, N_B), F32)
    y_c, hf_c, hb_c = run(xbc_c, dt_c, gate_c, h0, h0)
    y_x, _, _ = run(xbc_x, dt_x, gate_x, hf_c, hb_c)
    return _merge_streams(y_x, y_c)


def _mlstm_chunked(q, k, v, log_i, log_f, state0):
    b, l, h, _ = q.shape
    nc = l // MLSTM_CHUNK

    def chunks(t):
        return t.reshape((b, nc, MLSTM_CHUNK) + t.shape[2:]).swapaxes(0, 1)

    causal = jnp.tril(jnp.ones((MLSTM_CHUNK, MLSTM_CHUNK), dtype=bool))[None, :, :, None]

    def step(carry, inp):
        c_st, n_st, m_st = carry
        qc, kc, vc, li, lf = inp
        bcum = jnp.cumsum(lf, axis=1)
        g = bcum[:, -1]
        w_log = g[:, None] - bcum + li
        m_new = jnp.maximum(g + m_st, jnp.max(w_log, axis=1))
        wj = jnp.exp(w_log - m_new[:, None])
        dec = jnp.exp(g + m_st - m_new)
        c_new = dec[:, :, None, None] * c_st + jnp.einsum('bjh,bjhv,bjhk->bhvk', wj, vc, kc)
        n_new = dec[:, :, None] * n_st + jnp.einsum('bjh,bjhk->bhk', wj, kc)
        dmat = jnp.where(causal, bcum[:, :, None] - bcum[:, None] + li[:, None], -jnp.inf)
        inter_log = bcum + m_st[:, None]
        m_row = jnp.maximum(inter_log, jnp.max(dmat, axis=2))
        s = jnp.einsum('bihk,bjhk->bijh', qc, kc) * jnp.exp(dmat - m_row[:, :, None])
        w_inter = jnp.exp(inter_log - m_row)
        num = jnp.einsum('bijh,bjhv->bihv', s, vc) + w_inter[..., None] * jnp.einsum('bhvk,bihk->bihv', c_st, qc)
        den = jnp.sum(s, axis=2) + w_inter * jnp.einsum('bhk,bihk->bih', n_st, qc)
        hh = num / jnp.maximum(jnp.abs(den), jnp.exp(-m_row))[..., None]
        return (c_new, n_new, m_new), hh

    state, hs = lax.scan(step, state0, (chunks(q), chunks(k), chunks(v), chunks(log_i), chunks(log_f)))
    return hs.swapaxes(0, 1).reshape(b, l, h, v.shape[-1]), state


def _mlstm_mixer(z, gate_b):
    q_x, q_c = _split_streams(z[:, Z_MQ:Z_MQ + H_C * DQK_C])
    k_x, k_c = _split_streams(z[:, Z_MK:Z_MK + H_C * DQK_C])
    v_x, v_c = _split_streams(z[:, Z_MV:Z_MV + D_MLSTM])
    o_x, o_c = _split_streams(z[:, Z_MO:Z_MO + D_MLSTM])
    g_x, g_c = _split_streams(z[:, Z_MG:Z_MG + 4 * H_C])

    def run(q, k, v, o, g, s_f, s_b):
        bt, lt, _ = q.shape
        q = q.reshape(bt, lt, H_C, DQK_C) * DQK_C ** -0.5
        k = k.reshape(bt, lt, H_C, DQK_C)
        v = v.reshape(bt, lt, H_C, DV_C)
        g = g.reshape(bt, lt, 2, 2, H_C) + gate_b
        li = g[:, :, :, 0]
        lf = jax.nn.log_sigmoid(g[:, :, :, 1])
        h_f, s_f_new = _mlstm_chunked(q, k, v, li[:, :, 0], lf[:, :, 0], s_f)
        h_b, s_b_new = _mlstm_chunked(_flip(q), _flip(k), _flip(v), _flip(li[:, :, 1]), _flip(lf[:, :, 1]), s_b)
        hh = _rms(h_f + _flip(h_b))
        y = hh.reshape(bt, lt, D_MLSTM) * jax.nn.sigmoid(o)
        return y, s_f_new, s_b_new

    s0 = (jnp.zeros((BATCH, H_C, DV_C, DQK_C), F32), jnp.zeros((BATCH, H_C, DQK_C), F32),
          jnp.full((BATCH, H_C), NEG_STATE, F32))
    y_c, sf_c, sb_c = run(q_c, k_c, v_c, o_c, g_c, s0, s0)
    y_x, _, _ = run(q_x, k_x, v_x, o_x, g_x, sf_c, sb_c)
    return _merge_streams(y_x, y_c)


def _complex_scan(ar, ai, ur, ui, x0r, x0i):
    ur = ur.at[:, 0].add(ar * x0r - ai * x0i)
    ui = ui.at[:, 0].add(ar * x0i + ai * x0r)
    a_r = jnp.broadcast_to(ar, ur.shape)
    a_i = jnp.broadcast_to(ai, ur.shape)

    def combine(e1, e2):
        a1r, a1i, b1r, b1i = e1
        a2r, a2i, b2r, b2i = e2
        return (a2r * a1r - a2i * a1i, a2r * a1i + a2i * a1r,
                a2r * b1r - a2i * b1i + b2r, a2r * b1i + a2i * b1r + b2i)

    _, _, xr, xi = lax.associative_scan(combine, (a_r, a_i, ur, ui), axis=1)
    return xr, xi


def _s5_mixer(z, lam_re, lam_im, log_dt, b_re, b_im, c_re, c_im, d_skip, glu_w, glu_b):
    dt = jnp.exp(log_dt)[..., None]
    mag = jnp.exp(lam_re * dt)
    ar = mag * jnp.cos(lam_im * dt)
    ai = mag * jnp.sin(lam_im * dt)
    den = lam_re * lam_re + lam_im * lam_im
    cr_ = ((ar - 1.0) * lam_re + ai * lam_im) / den
    ci_ = (ai * lam_re - (ar - 1.0) * lam_im) / den
    bbr = cr_[..., None] * b_re - ci_[..., None] * b_im
    bbi = cr_[..., None] * b_im + ci_[..., None] * b_re
    u_x, u_c = _split_streams(z[:, Z_D:Z_D + D_S5])

    def drive(u, d):
        ug = u.reshape(u.shape[0], u.shape[1], G_S5, S5_GROUP)
        return jnp.einsum('blgc,gpc->blgp', ug, bbr[d]), jnp.einsum('blgc,gpc->blgp', ug, bbi[d])

    def readout(xr, xi):
        y = jnp.einsum('gcp,blgp->blgc', c_re, xr) - jnp.einsum('gcp,blgp->blgc', c_im, xi)
        return y.reshape(y.shape[0], y.shape[1], D_S5)

    def run(u, s0f, s0b):
        ufr, ufi = drive(u, 0)
        ubr, ubi = drive(_flip(u), 1)
        xfr, xfi = _complex_scan(ar[0], ai[0], ufr, ufi, s0f[0], s0f[1])
        xbr, xbi = _complex_scan(ar[1], ai[1], ubr, ubi, s0b[0], s0b[1])
        sf = (xfr[:, -1], xfi[:, -1])
        sb = (xbr[:, -1], xbi[:, -1])
        y = readout(xfr, xfi) + _flip(readout(xbr, xbi)) + d_skip * u
        y = jax.nn.gelu(y)
        y = y * jax.nn.sigmoid(y @ glu_w + glu_b)
        return _rms(y), sf, sb

    z0 = jnp.zeros((BATCH, G_S5, P_S5), F32)
    y_c, sf_c, sb_c = run(u_c, (z0, z0), (z0, z0))
    y_x, _, _ = run(u_x, sf_c, sb_c)
    return _merge_streams(y_x, y_c)


S5_Q = 256
S5_SEG = S5_Q // 8
S5_LANES = G_S5 * P_S5
S5_SLAB = 512
S5_NSLAB = S5_LANES // S5_SLAB


def _s5_dir_kernel(u_ref, perm_ref, wbr_ref, wbi_ref, atab_ref, apr_ref, api_ref, wcr_ref, wci_ref, y_ref,
                   ur_s, ui_s, xr_s, xi_s, st_s, car_s, *, reverse):
    @pl.when(pl.program_id(1) == 0)
    def _():
        st_s[...] = jnp.zeros_like(st_s)

    up = jnp.dot(perm_ref[...], u_ref[...].astype(BF16), preferred_element_type=F32).astype(BF16)
    for m in range(S5_NSLAB):
        um = up[:, m * LANES:(m + 1) * LANES]
        ur_s[:, m * S5_SLAB:(m + 1) * S5_SLAB] = jnp.dot(um, wbr_ref[m], preferred_element_type=F32)
        ui_s[:, m * S5_SLAB:(m + 1) * S5_SLAB] = jnp.dot(um, wbi_ref[m], preferred_element_type=F32)

    per = 4
    for grp in range(S5_LANES // (per * LANES)):
        cols = [grp * per * LANES + j * LANES for j in range(per)]
        a_r = [atab_ref[0:8, c0:c0 + LANES] for c0 in cols]
        a_i = [atab_ref[8:16, c0:c0 + LANES] for c0 in cols]

        def body(i, carry, cols=cols, a_r=a_r, a_i=a_i):
            t = (S5_SEG - 1 - i) if reverse else i
            r0 = pl.multiple_of(t * 8, 8)
            new = []
            for j, c0 in enumerate(cols):
                xr, xi = carry[2 * j], carry[2 * j + 1]
                nr = a_r[j] * xr - a_i[j] * xi + ur_s[pl.ds(r0, 8), c0:c0 + LANES]
                ni = a_r[j] * xi + a_i[j] * xr + ui_s[pl.ds(r0, 8), c0:c0 + LANES]
                ur_s[pl.ds(r0, 8), c0:c0 + LANES] = nr
                ui_s[pl.ds(r0, 8), c0:c0 + LANES] = ni
                new += [nr, ni]
            return tuple(new)

        lax.fori_loop(0, S5_SEG, body, tuple(jnp.zeros((8, LANES), F32) for _ in range(2 * per)), unroll=2)

    as_r = atab_ref[16:17, :]
    as_i = atab_ref[17:18, :]
    end_row = 0 if reverse else 8 * (S5_SEG - 1)
    cr = st_s[0:1, :]
    ci = st_s[1:2, :]
    for k in (range(7, -1, -1) if reverse else range(8)):
        car_s[k:k + 1, :] = cr
        car_s[8 + k:9 + k, :] = ci
        er = ur_s[end_row + k:end_row + k + 1, :]
        ei = ui_s[end_row + k:end_row + k + 1, :]
        cr, ci = er + as_r * cr - as_i * ci, ei + as_r * ci + as_i * cr
    st_s[0:1, :] = cr
    st_s[1:2, :] = ci

    for m in range(S5_NSLAB):
        cs = slice(m * S5_SLAB, (m + 1) * S5_SLAB)
        c_r = jnp.concatenate([car_s[0:8, cs], car_s[0:8, cs]], axis=0)
        c_i = jnp.concatenate([car_s[8:16, cs], car_s[8:16, cs]], axis=0)

        def fix(i, _, cs=cs, c_r=c_r, c_i=c_i):
            r0 = pl.multiple_of(i * 16, 16)
            p_r = apr_ref[pl.ds(r0, 16), cs]
            p_i = api_ref[pl.ds(r0, 16), cs]
            xr_s[pl.ds(r0, 16), cs] = (ur_s[pl.ds(r0, 16), cs] + p_r * c_r - p_i * c_i).astype(BF16)
            xi_s[pl.ds(r0, 16), cs] = (ui_s[pl.ds(r0, 16), cs] + p_r * c_i + p_i * c_r).astype(BF16)
            return 0

        lax.fori_loop(0, S5_Q // 16, fix, 0, unroll=2)

    for m in range(S5_NSLAB):
        cs = slice(m * S5_SLAB, (m + 1) * S5_SLAB)
        y_ref[:, m * LANES:(m + 1) * LANES] = (
            jnp.dot(xr_s[:, cs], wcr_ref[m], preferred_element_type=F32)
            - jnp.dot(xi_s[:, cs], wci_ref[m], preferred_element_type=F32))


def _s5_direction(z, perm, wbr, wbi, atab, apr, api, wcr, wci, reverse):
    n_x = SEQ // S5_Q
    ctx_blk = T_X // S5_Q

    def rows(b, s):
        xs = (n_x - s) if reverse else (s - 1)
        return jnp.where(s == 0, ctx_blk + b, b * n_x + xs)

    const2 = lambda b, s: (0, 0)
    const3 = lambda b, s: (0, 0, 0)
    return pl.pallas_call(
        functools.partial(_s5_dir_kernel, reverse=reverse),
        grid=(BATCH, n_x + 1),
        in_specs=[pl.BlockSpec((S5_Q, D_S5), lambda b, s: (rows(b, s), Z_D // D_S5)),
                  pl.BlockSpec((S5_Q, S5_Q), const2),
                  pl.BlockSpec(wbr.shape, const3),
                  pl.BlockSpec(wbi.shape, const3),
                  pl.BlockSpec(atab.shape, const2),
                  pl.BlockSpec(apr.shape, const2),
                  pl.BlockSpec(api.shape, const2),
                  pl.BlockSpec(wcr.shape, const3),
                  pl.BlockSpec(wci.shape, const3)],
        out_specs=pl.BlockSpec((S5_Q, D_S5), lambda b, s: (rows(b, s), 0)),
        out_shape=jax.ShapeDtypeStruct((T_ALL, D_S5), F32),
        scratch_shapes=[pltpu.VMEM((S5_Q, S5_LANES), F32), pltpu.VMEM((S5_Q, S5_LANES), F32),
                        pltpu.VMEM((S5_Q, S5_LANES), BF16), pltpu.VMEM((S5_Q, S5_LANES), BF16),
                        pltpu.VMEM((8, S5_LANES), F32), pltpu.VMEM((16, S5_LANES), F32)],
        compiler_params=_cparams("arbitrary", "arbitrary"),
        name="s5_bwd" if reverse else "s5_fwd",
    )(z, perm, wbr, wbi, atab, apr, api, wcr, wci)


def _s5_final_kernel(yf_ref, yb_ref, u_ref, permt_ref, d_ref, gw_ref, gb_ref, onw_ref, o_ref):
    ys = yf_ref[...] + yb_ref[...]
    hi = ys.astype(BF16)
    r1 = ys - hi.astype(F32)
    mid = r1.astype(BF16)
    lo = (r1 - mid.astype(F32)).astype(BF16)
    pt = permt_ref[...]
    y = (jnp.dot(pt, hi, preferred_element_type=F32) + jnp.dot(pt, mid, preferred_element_type=F32)
         + jnp.dot(pt, lo, preferred_element_type=F32))
    y = jax.nn.gelu(y + d_ref[...] * u_ref[...])
    gate = jax.nn.sigmoid(jnp.dot(y.astype(BF16), gw_ref[...], preferred_element_type=F32) + gb_ref[...])
    o_ref[...] = (_rms(y * gate) * onw_ref[...]).astype(o_ref.dtype)


def _s5_final(yf, yb, z, permt, d_skip, glu_w, glu_b, onw):
    row = lambda i: (i, 0)
    const = lambda i: (0, 0)
    return pl.pallas_call(
        _s5_final_kernel,
        grid=(T_ALL // S5_Q,),
        in_specs=[pl.BlockSpec((S5_Q, D_S5), row),
                  pl.BlockSpec((S5_Q, D_S5), row),
                  pl.BlockSpec((S5_Q, D_S5), lambda i: (i, Z_D // D_S5)),
                  pl.BlockSpec((S5_Q, S5_Q), const),
                  pl.BlockSpec((1, D_S5), const),
                  pl.BlockSpec((D_S5, D_S5), const),
                  pl.BlockSpec((1, D_S5), const),
                  pl.BlockSpec((1, D_S5), const)],
        out_specs=pl.BlockSpec((S5_Q, D_S5), row),
        out_shape=jax.ShapeDtypeStruct((T_ALL, D_S5), BF16),
        compiler_params=_cparams("arbitrary"),
        name="s5_final",
    )(yf, yb, z, permt, d_skip.reshape(1, D_S5), glu_w.astype(BF16), glu_b.reshape(1, D_S5), onw.reshape(1, D_S5))


def _s5_tables(lam_re, lam_im, log_dt, b_re, b_im, c_re, c_im):
    dt = jnp.exp(log_dt)[..., None]
    mag = jnp.exp(lam_re * dt)
    ar = mag * jnp.cos(lam_im * dt)
    ai = mag * jnp.sin(lam_im * dt)
    den = lam_re * lam_re + lam_im * lam_im
    cr_ = ((ar - 1.0) * lam_re + ai * lam_im) / den
    ci_ = (ai * lam_re - (ar - 1.0) * lam_im) / den
    bbr = cr_[..., None] * b_re - ci_[..., None] * b_im
    bbi = cr_[..., None] * b_im + ci_[..., None] * b_re
    gps = S5_SLAB // P_S5
    eye = jnp.eye(gps, dtype=F32)

    def drive_w(bb):
        t = bb.reshape(S5_NSLAB, gps, P_S5, S5_GROUP)
        w = jnp.einsum('mgpc,gh->mgchp', t, eye)
        return w.reshape(S5_NSLAB, gps * S5_GROUP, gps * P_S5).astype(BF16)

    def read_w(cc):
        t = cc.reshape(S5_NSLAB, gps, S5_GROUP, P_S5)
        w = jnp.einsum('mgcp,gh->mgphc', t, eye)
        return w.reshape(S5_NSLAB, gps * P_S5, gps * S5_GROUP).astype(BF16)

    pows_r, pows_i = [], []
    for d in range(2):
        a_r = ar[d].reshape(1, S5_LANES)
        a_i = ai[d].reshape(1, S5_LANES)
        pr, pi = [a_r], [a_i]
        for _ in range(S5_SEG - 1):
            pr, pi = pr + [pr[-1] * a_r - pi[-1] * a_i], pi + [pr[-1] * a_i + pi[-1] * a_r]
        pows_r.append(pr)
        pows_i.append(pi)

    out = []
    for d in range(2):
        pr, pi = pows_r[d], pows_i[d]
        order = list(range(S5_SEG))
        if d == 1:
            order = order[::-1]
        apr = jnp.concatenate([jnp.broadcast_to(pr[k], (8, S5_LANES)) for k in order], axis=0)
        api = jnp.concatenate([jnp.broadcast_to(pi[k], (8, S5_LANES)) for k in order], axis=0)
        atab = jnp.concatenate([jnp.broadcast_to(pr[0], (8, S5_LANES)), jnp.broadcast_to(pi[0], (8, S5_LANES)),
                                pr[-1], pi[-1], jnp.zeros((6, S5_LANES), F32)], axis=0)
        out.append((drive_w(bbr[d]), drive_w(bbi[d]), atab, apr, api))
    r = jnp.arange(S5_Q)
    src = (r % 8) * S5_SEG + r // 8
    perm = (src[:, None] == jnp.arange(S5_Q)[None, :]).astype(BF16)
    return out, read_w(c_re), read_w(c_im), perm


def _s5_mixer_pallas(z, p, onw):
    dirs, wcr, wci, perm = _s5_tables(p["s5_lam_re"], p["s5_lam_im"], p["s5_log_dt"], p["s5_b_re"], p["s5_b_im"],
                                      p["s5_c_re"], p["s5_c_im"])
    yf = _s5_direction(z, perm, *dirs[0], wcr, wci, reverse=False)
    yb = _s5_direction(z, perm, *dirs[1], wcr, wci, reverse=True)
    return _s5_final(yf, yb, z, perm.T, p["s5_d"], p["s5_glu_w"], p["s5_glu_b"], onw)


def _rope_tables():
    pos = jnp.arange(SEQ)
    row = (pos // GRID_W).astype(F32)
    col = (pos % GRID_W).astype(F32)
    inv_freq = ROPE_BASE ** (-jnp.arange(ROPE_AXIS // 2, dtype=F32) * 2.0 / ROPE_AXIS)
    ang_r = row[:, None] * inv_freq
    ang_c = col[:, None] * inv_freq
    zeros = jnp.zeros((SEQ, LANES - D_ROPE), F32)
    cos = jnp.concatenate([jnp.cos(ang_r), jnp.cos(ang_r), jnp.cos(ang_c), jnp.cos(ang_c), zeros], axis=1)
    sin = jnp.concatenate([jnp.sin(ang_r), jnp.sin(ang_r), jnp.sin(ang_c), jnp.sin(ang_c), zeros], axis=1)
    cos_c = jnp.concatenate([jnp.ones((ROW_TILE, D_ROPE), F32), jnp.zeros((ROW_TILE, LANES - D_ROPE), F32)], axis=1)
    sin_c = jnp.zeros((ROW_TILE, LANES), F32)
    return jnp.concatenate([jnp.concatenate([cos, sin], axis=1), jnp.concatenate([cos_c, sin_c], axis=1)], axis=0)


def _rot_cols(w):
    q = ROPE_AXIS // 2
    return jnp.concatenate([-w[:, q:2 * q], w[:, 0:q], -w[:, 3 * q:4 * q], w[:, 2 * q:3 * q]], axis=1)


def _layout_w_in(w):
    k = w.shape[0]
    zpad = lambda n: jnp.zeros((k, n), w.dtype)
    b0 = A_COLS
    c0 = A_COLS + B_COLS
    d0 = A_COLS + B_COLS + C_COLS
    kr = w[:, Q_RANK + KV_RANK:A_COLS]
    o1 = H_C * DQK_C
    cols = [w[:, 0:Q_RANK + KV_RANK], kr, zpad(LANES - D_ROPE), _rot_cols(kr), zpad(LANES - D_ROPE),
            w[:, b0:b0 + D_SSD + CONV_CH], w[:, b0 + D_SSD + CONV_CH:c0], zpad(LANES - 2 * H_B),
            w[:, c0:c0 + 2 * o1 + 2 * D_MLSTM], w[:, c0 + 2 * o1 + 2 * D_MLSTM:d0], zpad(LANES - 4 * H_C),
            w[:, d0:]]
    out = jnp.concatenate(cols, axis=1)
    assert out.shape[1] == Z_COLS
    return out.astype(BF16)


def _layout_mla(w_uq, w_ukv):
    k = w_uq.shape[0]
    qa, qb, wk, wv = [], [], [], []
    for h in range(H_A):
        base = h * (D_NOPE + D_ROPE)
        rope = w_uq[:, base + D_NOPE:base + D_NOPE + D_ROPE]
        qa += [w_uq[:, base:base + D_NOPE], rope, jnp.zeros((k, LANES - D_ROPE), w_uq.dtype)]
        qb += [_rot_cols(rope), jnp.zeros((k, LANES - D_ROPE), w_uq.dtype)]
        kb = h * (D_NOPE + D_V)
        wk.append(w_ukv[:, kb:kb + D_NOPE])
        wv.append(w_ukv[:, kb + D_NOPE:kb + D_NOPE + D_V])
    cat = lambda xs: jnp.concatenate(xs, axis=1).astype(BF16)
    return cat(qa), cat(qb), cat(wk), cat(wv)


def _layer(xall, modtab, p, cs, last):
    hx = _prenorm(xall, p["norm1_w"], modtab, 0, 1)
    z = _mm(hx, _layout_w_in(p["w_in"]), ROW_TILE, Z_COLS // 3)
    onw = p["out_norm_w"]
    wqa, wqb, wk, wv = _layout_mla(p["mla_w_uq"], p["mla_w_ukv"])
    q, k, v = _mla_prep(z, cs, p["mla_q_norm_w"], p["mla_kv_norm_w"], wqa, wqb, wk, wv)
    onw_a = onw[:D_MLA].reshape(1, D_MLA)
    ya = jnp.concatenate([_attention_latent(q, k, v, onw_a), _attention_ctx(q, k, v, onw_a)], axis=0)
    yb = (_ssd_mixer(z, p["ssd_conv_w"], p["ssd_conv_b"], p["ssd_a_log"], p["ssd_dt_bias"], p["ssd_d"])
          * onw[D_MLA:D_MLA + D_SSD]).astype(BF16)
    yc = (_mlstm_mixer(z, p["mlstm_gate_b"]) * onw[D_MLA + D_SSD:D_MLA + D_SSD + D_MLSTM]).astype(BF16)
    yd = _s5_mixer_pallas(z, p, onw[D_MLA + D_SSD + D_MLSTM:])
    xall = _outproj([ya, yb, yc, yd], p["w_out"].astype(BF16), xall, modtab, 2)

    w_router = jnp.concatenate([p["moe_w_group"], p["moe_w_expert"],
                                jnp.zeros((D_MODEL, ROUTER_COLS - N_GROUPS - N_EXPERTS), F32)], axis=1)
    n_tok = T_X if last else T_ALL
    h2, logits = _prenorm_router(xall[:n_tok], p["norm2_w"], modtab, w_router, 3, 4)
    f = _moe(h2, logits, p["moe_w_gate"], p["moe_w_up"], p["moe_w_down"])
    g2_rows = jnp.concatenate([jnp.repeat(modtab[:BATCH, 5], SEQ, axis=0),
                               jnp.broadcast_to(modtab[BATCH, 5], (T_C, D_MODEL))], axis=0)[:n_tok]
    return xall[:n_tok] + g2_rows * f


def kernel(x, c, ctx, c_ctx, mod_w, mod_b, norm1_w, w_in, mla_q_norm_w, mla_kv_norm_w, mla_w_uq, mla_w_ukv,
           ssd_conv_w, ssd_conv_b, ssd_a_log, ssd_dt_bias, ssd_d, mlstm_gate_b, s5_lam_re, s5_lam_im,
           s5_log_dt, s5_b_re, s5_b_im, s5_c_re, s5_c_im, s5_d, s5_glu_w, s5_glu_b, out_norm_w, w_out,
           norm2_w, moe_w_group, moe_w_expert, moe_w_gate, moe_w_up, moe_w_down, final_norm_w):
    stacked = {"norm1_w": norm1_w, "w_in": w_in, "mla_q_norm_w": mla_q_norm_w, "mla_kv_norm_w": mla_kv_norm_w,
               "mla_w_uq": mla_w_uq, "mla_w_ukv": mla_w_ukv, "ssd_conv_w": ssd_conv_w, "ssd_conv_b": ssd_conv_b,
               "ssd_a_log": ssd_a_log, "ssd_dt_bias": ssd_dt_bias, "ssd_d": ssd_d, "mlstm_gate_b": mlstm_gate_b,
               "s5_lam_re": s5_lam_re, "s5_lam_im": s5_lam_im, "s5_log_dt": s5_log_dt, "s5_b_re": s5_b_re,
               "s5_b_im": s5_b_im, "s5_c_re": s5_c_re, "s5_c_im": s5_c_im, "s5_d": s5_d, "s5_glu_w": s5_glu_w,
               "s5_glu_b": s5_glu_b, "out_norm_w": out_norm_w, "w_out": w_out, "norm2_w": norm2_w,
               "moe_w_group": moe_w_group, "moe_w_expert": moe_w_expert, "moe_w_gate": moe_w_gate,
               "moe_w_up": moe_w_up, "moe_w_down": moe_w_down}
    cs = _rope_tables()
    cc = jnp.concatenate([c, c_ctx[None, :], jnp.zeros((8 - BATCH - 1, D_MODEL), F32)], axis=0)
    xall = jnp.concatenate([x.reshape(T_X, D_MODEL), ctx.reshape(T_C, D_MODEL)], axis=0)
    for l in range(DEPTH):
        p = {name: val[l] for name, val in stacked.items()}
        modtab = _modulation(cc, mod_w[l], mod_b[l])[:BATCH + 1].reshape(BATCH + 1, 6, D_MODEL)
        xall = _layer(xall, modtab, p, cs, l == DEPTH - 1)
    return _final_norm(xall, final_norm_w).reshape(BATCH, SEQ, D_MODEL)
```

```python
import functools
import math

import jax
import jax.numpy as jnp
from jax import lax
from jax.experimental import pallas as pl
from jax.experimental.pallas import tpu as pltpu

F32 = jnp.float32
BF16 = jnp.bfloat16

D_MODEL = 2048
BATCH = 2
SEQ = 4096
DEPTH = 2
GRID_W = 64
CTX_LEN = 256
EPS = 1e-6
NEG_STATE = -1e30

H_A = 4
D_NOPE = 128
D_ROPE = 64
D_V = 128
Q_RANK = 384
KV_RANK = 128
ROPE_AXIS = D_ROPE // 2
ROPE_BASE = 10000.0
D_MLA = H_A * D_V
D_SSD = 512
P_B = 64
H_B = D_SSD // P_B
G_B = 2
N_B = 128
SSD_CONV = 5
SSD_CHUNK = 128
CONV_CH = D_SSD + 2 * G_B * N_B
D_MLSTM = 512
H_C = 4
DV_C = D_MLSTM // H_C
DQK_C = DV_C // 2
MLSTM_CHUNK = 128
D_S5 = 512
S5_GROUP = 16
G_S5 = D_S5 // S5_GROUP
P_S5 = 64
A_COLS = Q_RANK + KV_RANK + D_ROPE
B_COLS = D_SSD + CONV_CH + 2 * H_B
C_COLS = 2 * H_C * DQK_C + 2 * D_MLSTM + 4 * H_C
N_GROUPS = 4
EXPERTS_PER_GROUP = 8
N_EXPERTS = N_GROUPS * EXPERTS_PER_GROUP
TOP_K = 2
D_EXPERT = 512

T_X = BATCH * SEQ
T_C = BATCH * CTX_LEN
T_ALL = T_X + T_C

LANES = 128
VMEM_LIMIT_BYTES = 56 * 1024 * 1024

ROW_TILE = 512
TILES_PER_BATCH = SEQ // ROW_TILE
N_X_TILES = T_X // ROW_TILE

Z_CQ = 0
Z_CKV = 384
Z_KR = 512
Z_KRR = 640
Z_MQ = 768
Z_XBC = 1024
Z_SZ = 2048
Z_MV = 2560
Z_MO = 3072
Z_D = 3584
Z_MK = 4096
Z_DT = 4352
Z_MG = 4480
Z_COLS = 4608

MOE_ROWS = 256
ROUTER_COLS = 128


def _cparams(*sem):
    return pltpu.CompilerParams(dimension_semantics=sem, vmem_limit_bytes=VMEM_LIMIT_BYTES)


def _mod_row(i):
    return jnp.minimum(i // TILES_PER_BATCH, BATCH)


def _mod_kernel(a_ref, w_ref, b_ref, o_ref):
    a = a_ref[...]
    a = a * jax.nn.sigmoid(a)
    o_ref[...] = jnp.dot(a.astype(BF16), w_ref[...].astype(BF16), preferred_element_type=F32) + b_ref[...]


def _modulation(cc, w, b):
    n = w.shape[1]
    tn = 1024
    return pl.pallas_call(
        _mod_kernel,
        grid=(n // tn,),
        in_specs=[pl.BlockSpec((8, D_MODEL), lambda j: (0, 0)),
                  pl.BlockSpec((D_MODEL, tn), lambda j: (0, j)),
                  pl.BlockSpec((1, tn), lambda j: (0, j))],
        out_specs=pl.BlockSpec((8, tn), lambda j: (0, j)),
        out_shape=jax.ShapeDtypeStruct((8, n), F32),
        compiler_params=_cparams("arbitrary"),
        name="modulation",
    )(cc, w, b.reshape(1, n))


def _prenorm_kernel(x_ref, w_ref, mod_ref, o_ref, *, sh_idx, sc_idx):
    x = x_ref[...]
    y = x * lax.rsqrt(jnp.mean(x * x, axis=-1, keepdims=True) + EPS) * w_ref[...]
    m = mod_ref[0]
    y = y * (1.0 + m[sc_idx:sc_idx + 1, :]) + m[sh_idx:sh_idx + 1, :]
    o_ref[...] = y.astype(o_ref.dtype)


def _prenorm(x, w, modtab, sh_idx, sc_idx):
    rows = x.shape[0]
    return pl.pallas_call(
        functools.partial(_prenorm_kernel, sh_idx=sh_idx, sc_idx=sc_idx),
        grid=(rows // ROW_TILE,),
        in_specs=[pl.BlockSpec((ROW_TILE, D_MODEL), lambda i: (i, 0)),
                  pl.BlockSpec((1, D_MODEL), lambda i: (0, 0)),
                  pl.BlockSpec((1, 6, D_MODEL), lambda i: (_mod_row(i), 0, 0))],
        out_specs=pl.BlockSpec((ROW_TILE, D_MODEL), lambda i: (i, 0)),
        out_shape=jax.ShapeDtypeStruct((rows, D_MODEL), BF16),
        compiler_params=_cparams("arbitrary"),
        name="prenorm",
    )(x, w.reshape(1, D_MODEL), modtab)


def _prenorm_router_kernel(x_ref, w_ref, mod_ref, wr_ref, o_ref, lg_ref, *, sh_idx, sc_idx):
    x = x_ref[...]
    y = x * lax.rsqrt(jnp.mean(x * x, axis=-1, keepdims=True) + EPS) * w_ref[...]
    m = mod_ref[0]
    y = y * (1.0 + m[sc_idx:sc_idx + 1, :]) + m[sh_idx:sh_idx + 1, :]
    o_ref[...] = y.astype(o_ref.dtype)
    lg_ref[...] = jnp.dot(y, wr_ref[...], preferred_element_type=F32, precision=lax.Precision.HIGHEST)


def _prenorm_router(x, w, modtab, w_router, sh_idx, sc_idx):
    rows = x.shape[0]
    return pl.pallas_call(
        functools.partial(_prenorm_router_kernel, sh_idx=sh_idx, sc_idx=sc_idx),
        grid=(rows // ROW_TILE,),
        in_specs=[pl.BlockSpec((ROW_TILE, D_MODEL), lambda i: (i, 0)),
                  pl.BlockSpec((1, D_MODEL), lambda i: (0, 0)),
                  pl.BlockSpec((1, 6, D_MODEL), lambda i: (_mod_row(i), 0, 0)),
                  pl.BlockSpec((D_MODEL, ROUTER_COLS), lambda i: (0, 0))],
        out_specs=[pl.BlockSpec((ROW_TILE, D_MODEL), lambda i: (i, 0)),
                   pl.BlockSpec((ROW_TILE, ROUTER_COLS), lambda i: (i, 0))],
        out_shape=[jax.ShapeDtypeStruct((rows, D_MODEL), BF16),
                   jax.ShapeDtypeStruct((rows, ROUTER_COLS), F32)],
        compiler_params=_cparams("arbitrary"),
        name="prenorm_router",
    )(x, w.reshape(1, D_MODEL), modtab, w_router)


def _mm_kernel(a_ref, w_ref, o_ref):
    o_ref[...] = jnp.dot(a_ref[...], w_ref[...], preferred_element_type=F32).astype(o_ref.dtype)


def _mm(a, w, tm, tn, out_dtype=F32):
    m, k = a.shape
    n = w.shape[1]
    return pl.pallas_call(
        _mm_kernel,
        grid=(n // tn, m // tm),
        in_specs=[pl.BlockSpec((tm, k), lambda j, i: (i, 0)),
                  pl.BlockSpec((k, tn), lambda j, i: (0, j))],
        out_specs=pl.BlockSpec((tm, tn), lambda j, i: (i, j)),
        out_shape=jax.ShapeDtypeStruct((m, n), out_dtype),
        compiler_params=_cparams("arbitrary", "arbitrary"),
        name="in_proj",
    )(a, w)


def _rms(x, w=None):
    y = x * lax.rsqrt(jnp.mean(x * x, axis=-1, keepdims=True) + EPS)
    return y if w is None else y * w


def _mla_prep_kernel(za_ref, cs_ref, qw_ref, kvw_ref, wqa_ref, wqb_ref, wk_ref, wv_ref, q_ref, k_ref, v_ref):
    za = za_ref[...]
    cos = cs_ref[:, :LANES]
    sin = cs_ref[:, LANES:]
    qn = _rms(za[:, Z_CQ:Z_CQ + Q_RANK], qw_ref[...]).astype(BF16)
    kvn = _rms(za[:, Z_CKV:Z_CKV + KV_RANK], kvw_ref[...]).astype(BF16)
    qa = jnp.dot(qn, wqa_ref[...], preferred_element_type=F32)
    qb = jnp.dot(qn, wqb_ref[...], preferred_element_type=F32)
    kn = jnp.dot(kvn, wk_ref[...], preferred_element_type=F32)
    v = jnp.dot(kvn, wv_ref[...], preferred_element_type=F32)
    kr = (za[:, Z_KR:Z_KR + LANES] * cos + za[:, Z_KRR:Z_KRR + LANES] * sin).astype(BF16)
    for h in range(H_A):
        c0 = h * 2 * LANES
        q_ref[:, c0:c0 + LANES] = qa[:, c0:c0 + LANES].astype(BF16)
        q_ref[:, c0 + LANES:c0 + 2 * LANES] = (
            qa[:, c0 + LANES:c0 + 2 * LANES] * cos + qb[:, h * LANES:(h + 1) * LANES] * sin).astype(BF16)
        k_ref[:, c0:c0 + LANES] = kn[:, h * LANES:(h + 1) * LANES].astype(BF16)
        k_ref[:, c0 + LANES:c0 + 2 * LANES] = kr
    v_ref[...] = v.astype(BF16)


def _mla_prep(z, cs, qw, kvw, wqa, wqb, wk, wv):
    za_w = Z_MQ
    const = lambda i: (0, 0)
    return pl.pallas_call(
        _mla_prep_kernel,
        grid=(T_ALL // ROW_TILE,),
        in_specs=[pl.BlockSpec((ROW_TILE, za_w), lambda i: (i, 0)),
                  pl.BlockSpec((ROW_TILE, 2 * LANES),
                               lambda i: (jnp.where(i < N_X_TILES, i % TILES_PER_BATCH, TILES_PER_BATCH), 0)),
                  pl.BlockSpec((1, Q_RANK), const),
                  pl.BlockSpec((1, KV_RANK), const),
                  pl.BlockSpec(wqa.shape, const),
                  pl.BlockSpec(wqb.shape, const),
                  pl.BlockSpec(wk.shape, const),
                  pl.BlockSpec(wv.shape, const)],
        out_specs=[pl.BlockSpec((ROW_TILE, H_A * 2 * LANES), lambda i: (i, 0)),
                   pl.BlockSpec((ROW_TILE, H_A * 2 * LANES), lambda i: (i, 0)),
                   pl.BlockSpec((ROW_TILE, D_MLA), lambda i: (i, 0))],
        out_shape=[jax.ShapeDtypeStruct((T_ALL, H_A * 2 * LANES), BF16),
                   jax.ShapeDtypeStruct((T_ALL, H_A * 2 * LANES), BF16),
                   jax.ShapeDtypeStruct((T_ALL, D_MLA), BF16)],
        compiler_params=_cparams("arbitrary"),
        name="mla_prep",
    )(z, cs, qw.reshape(1, Q_RANK), kvw.reshape(1, KV_RANK), wqa, wqb, wk, wv)


_NT = (((1,), (1,)), ((), ()))


def _attn_kernel(*refs, n_seg):
    q_ref = refs[0]
    k_refs = refs[1:1 + n_seg]
    v_refs = refs[1 + n_seg:1 + 2 * n_seg]
    w_ref = refs[1 + 2 * n_seg]
    o_ref = refs[2 + 2 * n_seg]
    acc_ref = refs[3 + 2 * n_seg]
    scale = (D_NOPE + D_ROPE) ** -0.5
    for h in range(H_A):
        q = q_ref[:, h * 2 * LANES:(h + 1) * 2 * LANES]
        ss = [lax.dot_general(q, k[:, h * 2 * LANES:(h + 1) * 2 * LANES], _NT, preferred_element_type=F32)
              for k in k_refs]
        m = functools.reduce(jnp.maximum, [jnp.max(s, axis=-1, keepdims=True) for s in ss])
        ps = [jnp.exp((s - m) * scale) for s in ss]
        l = functools.reduce(lambda a, b: a + b, [jnp.sum(p, axis=-1, keepdims=True) for p in ps])
        o = functools.reduce(lambda a, b: a + b, [
            jnp.dot(p.astype(BF16), v[:, h * D_V:(h + 1) * D_V], preferred_element_type=F32)
            for p, v in zip(ps, v_refs)])
        acc_ref[:, h * D_V:(h + 1) * D_V] = o / l
    o_ref[...] = (_rms(acc_ref[...]) * w_ref[...]).astype(o_ref.dtype)


ATTN_TQ = 256


def _attention_latent(q, k, v, onw):
    qt = SEQ // ATTN_TQ
    kw = H_A * 2 * LANES
    return pl.pallas_call(
        functools.partial(_attn_kernel, n_seg=2),
        grid=(BATCH, qt),
        in_specs=[pl.BlockSpec((ATTN_TQ, kw), lambda b, i: (b * qt + i, 0)),
                  pl.BlockSpec((SEQ, kw), lambda b, i: (b, 0)),
                  pl.BlockSpec((CTX_LEN, kw), lambda b, i: (T_X // CTX_LEN + b, 0)),
                  pl.BlockSpec((SEQ, D_MLA), lambda b, i: (b, 0)),
                  pl.BlockSpec((CTX_LEN, D_MLA), lambda b, i: (T_X // CTX_LEN + b, 0)),
                  pl.BlockSpec((1, D_MLA), lambda b, i: (0, 0))],
        out_specs=pl.BlockSpec((ATTN_TQ, D_MLA), lambda b, i: (b * qt + i, 0)),
        out_shape=jax.ShapeDtypeStruct((T_X, D_MLA), BF16),
        scratch_shapes=[pltpu.VMEM((ATTN_TQ, D_MLA), F32)],
        compiler_params=_cparams("arbitrary", "arbitrary"),
        name="attn_latent",
    )(q, k, k, v, v, onw)


def _attention_ctx(q, k, v, onw):
    kw = H_A * 2 * LANES
    blk = lambda b: (T_X // CTX_LEN + b, 0)
    return pl.pallas_call(
        functools.partial(_attn_kernel, n_seg=1),
        grid=(BATCH,),
        in_specs=[pl.BlockSpec((CTX_LEN, kw), blk),
                  pl.BlockSpec((CTX_LEN, kw), blk),
                  pl.BlockSpec((CTX_LEN, D_MLA), blk),
                  pl.BlockSpec((1, D_MLA), lambda b: (0, 0))],
        out_specs=pl.BlockSpec((CTX_LEN, D_MLA), lambda b: (b, 0)),
        out_shape=jax.ShapeDtypeStruct((T_C, D_MLA), BF16),
        scratch_shapes=[pltpu.VMEM((CTX_LEN, D_MLA), F32)],
        compiler_params=_cparams("arbitrary"),
        name="attn_ctx",
    )(q, k, v, onw)


def _outproj_kernel(a0, a1, a2, a3, w0, w1, w2, w3, x_ref, mod_ref, o_ref, *, g_idx):
    acc = jnp.dot(a0[...], w0[...], preferred_element_type=F32)
    acc += jnp.dot(a1[...], w1[...], preferred_element_type=F32)
    acc += jnp.dot(a2[...], w2[...], preferred_element_type=F32)
    acc += jnp.dot(a3[...], w3[...], preferred_element_type=F32)
    g = mod_ref[0][g_idx:g_idx + 1, :]
    o_ref[...] = x_ref[...] + g * acc


def _outproj(ys, w, x, modtab, g_idx):
    kq = D_MODEL // 4
    a_specs = [pl.BlockSpec((ROW_TILE, kq), lambda i: (i, 0)) for _ in range(4)]
    w_specs = [pl.BlockSpec((kq, D_MODEL), functools.partial(lambda i, r: (r, 0), r=r)) for r in range(4)]
    return pl.pallas_call(
        functools.partial(_outproj_kernel, g_idx=g_idx),
        grid=(T_ALL // ROW_TILE,),
        in_specs=a_specs + w_specs + [
            pl.BlockSpec((ROW_TILE, D_MODEL), lambda i: (i, 0)),
            pl.BlockSpec((1, 6, D_MODEL), lambda i: (_mod_row(i), 0, 0))],
        out_specs=pl.BlockSpec((ROW_TILE, D_MODEL), lambda i: (i, 0)),
        out_shape=jax.ShapeDtypeStruct((T_ALL, D_MODEL), F32),
        compiler_params=_cparams("arbitrary"),
        name="out_proj",
    )(*ys, w, w, w, w, x, modtab)


def _moe_kernel(be_ref, first_ref, valid_ref, x_ref, wg_ref, wu_ref, wd_ref, o_ref, wg_s, wu_s, wd_s):
    i = pl.program_id(0)

    @pl.when(first_ref[i] == 1)
    def _():
        wg_s[...] = wg_ref[0].astype(BF16)
        wu_s[...] = wu_ref[0].astype(BF16)
        wd_s[...] = wd_ref[0].astype(BF16)

    @pl.when(valid_ref[i] == 1)
    def _():
        x = x_ref[...]
        g = jnp.dot(x, wg_s[...], preferred_element_type=F32)
        u = jnp.dot(x, wu_s[...], preferred_element_type=F32)
        h = (g * jax.nn.sigmoid(g) * u).astype(BF16)
        o_ref[...] = jnp.dot(h, wd_s[...], preferred_element_type=F32)

    @pl.when(valid_ref[i] == 0)
    def _():
        o_ref[...] = jnp.zeros_like(o_ref)


def _moe_experts(xb, block_e, first, valid, wg, wu, wd):
    nb = xb.shape[0] // MOE_ROWS
    grid_spec = pltpu.PrefetchScalarGridSpec(
        num_scalar_prefetch=3,
        grid=(nb,),
        in_specs=[pl.BlockSpec((MOE_ROWS, D_MODEL), lambda i, be, fi, va: (i, 0)),
                  pl.BlockSpec((1, D_MODEL, D_EXPERT), lambda i, be, fi, va: (be[i], 0, 0)),
                  pl.BlockSpec((1, D_MODEL, D_EXPERT), lambda i, be, fi, va: (be[i], 0, 0)),
                  pl.BlockSpec((1, D_EXPERT, D_MODEL), lambda i, be, fi, va: (be[i], 0, 0))],
        out_specs=pl.BlockSpec((MOE_ROWS, D_MODEL), lambda i, be, fi, va: (i, 0)),
        scratch_shapes=[pltpu.VMEM((D_MODEL, D_EXPERT), BF16),
                        pltpu.VMEM((D_MODEL, D_EXPERT), BF16),
                        pltpu.VMEM((D_EXPERT, D_MODEL), BF16)])
    return pl.pallas_call(
        _moe_kernel,
        grid_spec=grid_spec,
        out_shape=jax.ShapeDtypeStruct((nb * MOE_ROWS, D_MODEL), F32),
        compiler_params=_cparams("arbitrary"),
        name="moe_experts",
    )(block_e, first, valid, xb, wg, wu, wd)


def _moe(h_bf16, logits, wg, wu, wd):
    t = h_bf16.shape[0]
    g_prob = jax.nn.softmax(logits[:, :N_GROUPS], axis=-1)
    g_w, g_idx = lax.top_k(g_prob, 1)
    e_logits = logits[:, N_GROUPS:N_GROUPS + N_EXPERTS].reshape(t, N_GROUPS, EXPERTS_PER_GROUP)
    e_in = jnp.take_along_axis(e_logits, g_idx[:, :, None], axis=1)[:, 0]
    e_val, e_idx = lax.top_k(e_in, TOP_K)
    weights = g_w * jax.nn.softmax(e_val, axis=-1)
    experts = (g_idx * EXPERTS_PER_GROUP + e_idx).astype(jnp.int32)
    m = t * TOP_K
    flat_e = experts.reshape(-1)
    onehot = (flat_e[:, None] == jnp.arange(N_EXPERTS, dtype=jnp.int32)[None, :]).astype(jnp.int32)
    csum = jnp.cumsum(onehot, axis=0)
    rank = jnp.take_along_axis(csum, flat_e[:, None], axis=1)[:, 0] - 1
    counts = csum[-1]
    pcounts = (counts + MOE_ROWS - 1) // MOE_ROWS * MOE_ROWS
    pends = jnp.cumsum(pcounts)
    pstarts = pends - pcounts
    dest = pstarts[flat_e] + rank
    nb = (m + N_EXPERTS * (MOE_ROWS - 1) + MOE_ROWS - 1) // MOE_ROWS
    tokens = jnp.repeat(jnp.arange(t, dtype=jnp.int32), TOP_K)
    buf_tok = jnp.zeros((nb * MOE_ROWS,), jnp.int32).at[dest].set(tokens)
    bstart = jnp.arange(nb, dtype=jnp.int32) * MOE_ROWS
    block_e = jnp.clip(jnp.searchsorted(pends, bstart, side='right'), 0, N_EXPERTS - 1).astype(jnp.int32)
    valid = (bstart < pends[-1]).astype(jnp.int32)
    first = jnp.concatenate([jnp.ones((1,), jnp.int32), (block_e[1:] != block_e[:-1]).astype(jnp.int32)])
    xb = h_bf16[buf_tok]
    yb = _moe_experts(xb, block_e, first, valid, wg, wu, wd)
    d2 = dest.reshape(t, TOP_K)
    return yb[d2[:, 0]] * weights[:, 0:1] + yb[d2[:, 1]] * weights[:, 1:2]


def _final_norm_kernel(x_ref, w_ref, o_ref):
    o_ref[...] = _rms(x_ref[...], w_ref[...])


def _final_norm(x, w):
    rows = x.shape[0]
    return pl.pallas_call(
        _final_norm_kernel,
        grid=(rows // ROW_TILE,),
        in_specs=[pl.BlockSpec((ROW_TILE, D_MODEL), lambda i: (i, 0)),
                  pl.BlockSpec((1, D_MODEL), lambda i: (0, 0))],
        out_specs=pl.BlockSpec((ROW_TILE, D_MODEL), lambda i: (i, 0)),
        out_shape=jax.ShapeDtypeStruct((rows, D_MODEL), F32),
        compiler_params=_cparams("arbitrary"),
        name="final_norm",
    )(x, w.reshape(1, D_MODEL))


def _split_streams(a):
    return a[:T_X].reshape(BATCH, SEQ, -1), a[T_X:].reshape(BATCH, CTX_LEN, -1)


def _merge_streams(ax, ac):
    return jnp.concatenate([ax.reshape(T_X, -1), ac.reshape(T_C, -1)], axis=0)


def _flip(t):
    return jnp.flip(t, axis=1)


def _ssd_chunked(x, dt, a_h, bm, cm, h0):
    b, l, h, p = x.shape
    nc = l // SSD_CHUNK
    rep = h // bm.shape[2]
    bh = jnp.repeat(bm, rep, axis=2).reshape(b, nc, SSD_CHUNK, h, -1)
    ch = jnp.repeat(cm, rep, axis=2).reshape(b, nc, SSD_CHUNK, h, -1)
    xc = x.reshape(b, nc, SSD_CHUNK, h, p)
    dtc = dt.reshape(b, nc, SSD_CHUNK, h)
    acum = jnp.cumsum(dtc * a_h, axis=2)
    a_tot = acum[:, :, -1]
    w_end = jnp.exp(a_tot[:, :, None] - acum) * dtc
    s_loc = jnp.einsum('bcqh,bcqhn,bcqhp->bchpn', w_end, bh, xc)

    def step(hs, inp):
        s, dec = inp
        return jnp.exp(dec)[:, :, None, None] * hs + s, hs

    h_fin, h_start = lax.scan(step, h0, (s_loc.swapaxes(0, 1), a_tot.swapaxes(0, 1)))
    h_start = h_start.swapaxes(0, 1)
    causal = jnp.tril(jnp.ones((SSD_CHUNK, SSD_CHUNK), dtype=bool))
    seg = acum[:, :, :, None, :] - acum[:, :, None, :, :]
    decay = jnp.exp(jnp.where(causal[None, None, :, :, None], seg, -jnp.inf))
    scores = jnp.einsum('bcihn,bcjhn->bcijh', ch, bh) * decay
    y = (jnp.einsum('bcijh,bcjh,bcjhp->bcihp', scores, dtc, xc)
         + jnp.exp(acum)[..., None] * jnp.einsum('bcihn,bchpn->bcihp', ch, h_start))
    return y.reshape(b, l, h, p), h_fin


def _ssd_mixer(z, conv_w, conv_b, a_log, dt_bias, d_skip):
    a_neg = -jnp.exp(a_log.astype(F32))
    gate_x, gate_c = _split_streams(z[:, Z_SZ:Z_SZ + D_SSD])
    xbc_x, xbc_c = _split_streams(z[:, Z_XBC:Z_XBC + CONV_CH])
    dt_x, dt_c = _split_streams(z[:, Z_DT:Z_DT + 2 * H_B])

    def prep(xbc, dt):
        bt, lt, c = xbc.shape
        k = conv_w.shape[0]
        y = lax.conv_general_dilated(xbc, conv_w[:, None, :], window_strides=(1,),
                                     padding=[((k - 1) // 2, (k - 1) // 2)],
                                     dimension_numbers=('NWC', 'WIO', 'NWC'), feature_group_count=c) + conv_b
        y = jax.nn.silu(y)
        xs = y[..., :D_SSD].reshape(bt, lt, H_B, P_B)
        bm = y[..., D_SSD:D_SSD + G_B * N_B].reshape(bt, lt, G_B, N_B)
        cm = y[..., D_SSD + G_B * N_B:].reshape(bt, lt, G_B, N_B)
        dts = jax.nn.softplus(dt.reshape(bt, lt, 2, H_B) + dt_bias)
        return xs, bm, cm, dts[:, :, 0], dts[:, :, 1]

    def run(xbc, dt, gate, h0f, h0b):
        xs, bm, cm, dtf, dtb = prep(xbc, dt)
        yf, hf = _ssd_chunked(xs, dtf, a_neg[0], bm, cm, h0f)
        yb, hb = _ssd_chunked(_flip(xs), _flip(dtb), a_neg[1], _flip(bm), _flip(cm), h0b)
        y = yf + _flip(yb) + d_skip[:, None] * xs
        y = y.reshape(xbc.shape[0], xbc.shape[1], D_SSD) * jax.nn.silu(gate)
        return _rms(y), hf, hb

    h0 = jnp.zeros((BATCH, H_B, P_B, N_B), F32)
    y_c, hf_c, hb_c = run(xbc_c, dt_c, gate_c, h0, h0)
    y_x, _, _ = run(xbc_x, dt_x, gate_x, hf_c, hb_c)
    return _merge_streams(y_x, y_c)


def _mlstm_chunked(q, k, v, log_i, log_f, state0):
    b, l, h, _ = q.shape
    nc = l // MLSTM_CHUNK

    def chunks(t):
        return t.reshape((b, nc, MLSTM_CHUNK) + t.shape[2:]).swapaxes(0, 1)

    causal = jnp.tril(jnp.ones((MLSTM_CHUNK, MLSTM_CHUNK), dtype=bool))[None, :, :, None]

    def step(carry, inp):
        c_st, n_st, m_st = carry
        qc, kc, vc, li, lf = inp
        bcum = jnp.cumsum(lf, axis=1)
        g = bcum[:, -1]
        w_log = g[:, None] - bcum + li
        m_new = jnp.maximum(g + m_st, jnp.max(w_log, axis=1))
        wj = jnp.exp(w_log - m_new[:, None])
        dec = jnp.exp(g + m_st - m_new)
        c_new = dec[:, :, None, None] * c_st + jnp.einsum('bjh,bjhv,bjhk->bhvk', wj, vc, kc)
        n_new = dec[:, :, None] * n_st + jnp.einsum('bjh,bjhk->bhk', wj, kc)
        dmat = jnp.where(causal, bcum[:, :, None] - bcum[:, None] + li[:, None], -jnp.inf)
        inter_log = bcum + m_st[:, None]
        m_row = jnp.maximum(inter_log, jnp.max(dmat, axis=2))
        s = jnp.einsum('bihk,bjhk->bijh', qc, kc) * jnp.exp(dmat - m_row[:, :, None])
        w_inter = jnp.exp(inter_log - m_row)
        num = jnp.einsum('bijh,bjhv->bihv', s, vc) + w_inter[..., None] * jnp.einsum('bhvk,bihk->bihv', c_st, qc)
        den = jnp.sum(s, axis=2) + w_inter * jnp.einsum('bhk,bihk->bih', n_st, qc)
        hh = num / jnp.maximum(jnp.abs(den), jnp.exp(-m_row))[..., None]
        return (c_new, n_new, m_new), hh

    state, hs = lax.scan(step, state0, (chunks(q), chunks(k), chunks(v), chunks(log_i), chunks(log_f)))
    return hs.swapaxes(0, 1).reshape(b, l, h, v.shape[-1]), state


def _mlstm_mixer(z, gate_b):
    q_x, q_c = _split_streams(z[:, Z_MQ:Z_MQ + H_C * DQK_C])
    k_x, k_c = _split_streams(z[:, Z_MK:Z_MK + H_C * DQK_C])
    v_x, v_c = _split_streams(z[:, Z_MV:Z_MV + D_MLSTM])
    o_x, o_c = _split_streams(z[:, Z_MO:Z_MO + D_MLSTM])
    g_x, g_c = _split_streams(z[:, Z_MG:Z_MG + 4 * H_C])

    def run(q, k, v, o, g, s_f, s_b):
        bt, lt, _ = q.shape
        q = q.reshape(bt, lt, H_C, DQK_C) * DQK_C ** -0.5
        k = k.reshape(bt, lt, H_C, DQK_C)
        v = v.reshape(bt, lt, H_C, DV_C)
        g = g.reshape(bt, lt, 2, 2, H_C) + gate_b
        li = g[:, :, :, 0]
        lf = jax.nn.log_sigmoid(g[:, :, :, 1])
        h_f, s_f_new = _mlstm_chunked(q, k, v, li[:, :, 0], lf[:, :, 0], s_f)
        h_b, s_b_new = _mlstm_chunked(_flip(q), _flip(k), _flip(v), _flip(li[:, :, 1]), _flip(lf[:, :, 1]), s_b)
        hh = _rms(h_f + _flip(h_b))
        y = hh.reshape(bt, lt, D_MLSTM) * jax.nn.sigmoid(o)
        return y, s_f_new, s_b_new

    s0 = (jnp.zeros((BATCH, H_C, DV_C, DQK_C), F32), jnp.zeros((BATCH, H_C, DQK_C), F32),
          jnp.full((BATCH, H_C), NEG_STATE, F32))
    y_c, sf_c, sb_c = run(q_c, k_c, v_c, o_c, g_c, s0, s0)
    y_x, _, _ = run(q_x, k_x, v_x, o_x, g_x, sf_c, sb_c)
    return _merge_streams(y_x, y_c)


def _complex_scan(ar, ai, ur, ui, x0r, x0i):
    ur = ur.at[:, 0].add(ar * x0r - ai * x0i)
    ui = ui.at[:, 0].add(ar * x0i + ai * x0r)
    a_r = jnp.broadcast_to(ar, ur.shape)
    a_i = jnp.broadcast_to(ai, ur.shape)

    def combine(e1, e2):
        a1r, a1i, b1r, b1i = e1
        a2r, a2i, b2r, b2i = e2
        return (a2r * a1r - a2i * a1i, a2r * a1i + a2i * a1r,
                a2r * b1r - a2i * b1i + b2r, a2r * b1i + a2i * b1r + b2i)

    _, _, xr, xi = lax.associative_scan(combine, (a_r, a_i, ur, ui), axis=1)
    return xr, xi


def _s5_mixer(z, lam_re, lam_im, log_dt, b_re, b_im, c_re, c_im, d_skip, glu_w, glu_b):
    dt = jnp.exp(log_dt)[..., None]
    mag = jnp.exp(lam_re * dt)
    ar = mag * jnp.cos(lam_im * dt)
    ai = mag * jnp.sin(lam_im * dt)
    den = lam_re * lam_re + lam_im * lam_im
    cr_ = ((ar - 1.0) * lam_re + ai * lam_im) / den
    ci_ = (ai * lam_re - (ar - 1.0) * lam_im) / den
    bbr = cr_[..., None] * b_re - ci_[..., None] * b_im
    bbi = cr_[..., None] * b_im + ci_[..., None] * b_re
    u_x, u_c = _split_streams(z[:, Z_D:Z_D + D_S5])

    def drive(u, d):
        ug = u.reshape(u.shape[0], u.shape[1], G_S5, S5_GROUP)
        return jnp.einsum('blgc,gpc->blgp', ug, bbr[d]), jnp.einsum('blgc,gpc->blgp', ug, bbi[d])

    def readout(xr, xi):
        y = jnp.einsum('gcp,blgp->blgc', c_re, xr) - jnp.einsum('gcp,blgp->blgc', c_im, xi)
        return y.reshape(y.shape[0], y.shape[1], D_S5)

    def run(u, s0f, s0b):
        ufr, ufi = drive(u, 0)
        ubr, ubi = drive(_flip(u), 1)
        xfr, xfi = _complex_scan(ar[0], ai[0], ufr, ufi, s0f[0], s0f[1])
        xbr, xbi = _complex_scan(ar[1], ai[1], ubr, ubi, s0b[0], s0b[1])
        sf = (xfr[:, -1], xfi[:, -1])
        sb = (xbr[:, -1], xbi[:, -1])
        y = readout(xfr, xfi) + _flip(readout(xbr, xbi)) + d_skip * u
        y = jax.nn.gelu(y)
        y = y * jax.nn.sigmoid(y @ glu_w + glu_b)
        return _rms(y), sf, sb

    z0 = jnp.zeros((BATCH, G_S5, P_S5), F32)
    y_c, sf_c, sb_c = run(u_c, (z0, z0), (z0, z0))
    y_x, _, _ = run(u_x, sf_c, sb_c)
    return _merge_streams(y_x, y_c)


SEQ_TILE = 256
X_SEQ_TILES = SEQ // SEQ_TILE
HALO = 8


def _split3(x):
    hi = x.astype(BF16)
    r1 = x - hi.astype(F32)
    mid = r1.astype(BF16)
    lo = (r1 - mid.astype(F32)).astype(BF16)
    return hi, mid, lo


def _dot3_left(sel, x):
    hi, mid, lo = _split3(x)
    return (jnp.dot(sel, hi, preferred_element_type=F32) + jnp.dot(sel, mid, preferred_element_type=F32)
            + jnp.dot(sel, lo, preferred_element_type=F32))


def _dot3_right(x, sel):
    hi, mid, lo = _split3(x)
    return (jnp.dot(hi, sel, preferred_element_type=F32) + jnp.dot(mid, sel, preferred_element_type=F32)
            + jnp.dot(lo, sel, preferred_element_type=F32))


def _ssd_prep_kernel(cur_ref, prev_ref, next_ref, dt_ref, cw_ref, cb_ref, dtb_ref, xo_ref, dto_ref, ext_s):
    i = pl.program_id(0)
    is_ctx = i >= T_X // SEQ_TILE
    first = jnp.logical_or(is_ctx, i % X_SEQ_TILES == 0)
    last = jnp.logical_or(is_ctx, i % X_SEQ_TILES == X_SEQ_TILES - 1)
    ext_s[0:HALO, :] = jnp.where(first, 0.0, prev_ref[...])
    ext_s[HALO:HALO + SEQ_TILE, :] = cur_ref[...]
    ext_s[HALO + SEQ_TILE:, :] = jnp.where(last, 0.0, next_ref[...])
    half = (SSD_CONV - 1) // 2
    acc = cb_ref[...] + cw_ref[0:1, :] * ext_s[HALO - half:HALO - half + SEQ_TILE, :]
    for k in range(1, SSD_CONV):
        acc = acc + cw_ref[k:k + 1, :] * ext_s[HALO - half + k:HALO - half + k + SEQ_TILE, :]
    xo_ref[...] = acc * jax.nn.sigmoid(acc)
    lane = lax.broadcasted_iota(jnp.int32, (SEQ_TILE, LANES), 1)
    dto_ref[...] = jnp.where(lane < 2 * H_B, jax.nn.softplus(dt_ref[...] + dtb_ref[...]), 0.0)


def _ssd_prep(z, conv_w, conv_b, dt_bias):
    n_tiles = T_ALL // SEQ_TILE
    per = SEQ_TILE // HALO
    cwp = jnp.concatenate([conv_w, jnp.zeros((8 - SSD_CONV, CONV_CH), F32)], axis=0)
    dtb = jnp.concatenate([dt_bias.reshape(1, 2 * H_B), jnp.zeros((1, LANES - 2 * H_B), F32)], axis=1)
    xc = Z_XBC // CONV_CH
    return pl.pallas_call(
        _ssd_prep_kernel,
        grid=(n_tiles,),
        in_specs=[pl.BlockSpec((SEQ_TILE, CONV_CH), lambda i: (i, xc)),
                  pl.BlockSpec((HALO, CONV_CH), lambda i: (jnp.maximum(i * per - 1, 0), xc)),
                  pl.BlockSpec((HALO, CONV_CH), lambda i: (jnp.minimum((i + 1) * per, T_ALL // HALO - 1), xc)),
                  pl.BlockSpec((SEQ_TILE, LANES), lambda i: (i, Z_DT // LANES)),
                  pl.BlockSpec((8, CONV_CH), lambda i: (0, 0)),
                  pl.BlockSpec((1, CONV_CH), lambda i: (0, 0)),
                  pl.BlockSpec((1, LANES), lambda i: (0, 0))],
        out_specs=[pl.BlockSpec((SEQ_TILE, CONV_CH), lambda i: (i, 0)),
                   pl.BlockSpec((SEQ_TILE, LANES), lambda i: (i, 0))],
        out_shape=[jax.ShapeDtypeStruct((T_ALL, CONV_CH), F32),
                   jax.ShapeDtypeStruct((T_ALL, LANES), F32)],
        scratch_shapes=[pltpu.VMEM((SEQ_TILE + 2 * HALO, CONV_CH), F32)],
        compiler_params=_cparams("arbitrary"),
        name="ssd_prep",
    )(z, z, z, z, cwp, conv_b.reshape(1, CONV_CH), dtb)


_TN = (((0,), (0,)), ((), ()))
NEG_BIG = -1e30


def _ssd_one_direction(xbc, dtp, arow, tri, expand, h_ref, y_ref, d, rev):
    q = SSD_CHUNK
    a = dtp * arow
    acum = _dot3_left(tri, a)
    acum_t = acum.T
    dt_t = dtp.T
    edge = 0 if rev else q - 1
    atot = acum[edge:edge + 1, :]
    pieces = jnp.concatenate([jnp.exp(atot - acum) * dtp, jnp.exp(acum),
                              jnp.broadcast_to(jnp.exp(atot), (8, LANES))], axis=0)
    ex = _dot3_right(pieces, expand)
    wend_x = ex[0:q]
    eacum_x = ex[q:2 * q]
    dec_x = ex[2 * q:2 * q + 1]
    xs = xbc[:, 0:D_SSD]
    xw = (xs * wend_x).astype(BF16)
    xs_b = xs.astype(BF16)
    h_old = h_ref[...]
    h_b = h_old.astype(BF16)
    ri = lax.broadcasted_iota(jnp.int32, (q, q), 0)
    ci = lax.broadcasted_iota(jnp.int32, (q, q), 1)
    mask = (ci >= ri) if rev else (ci <= ri)
    lo_half = lax.broadcasted_iota(jnp.int32, (q, LANES), 1) < P_B
    hpg = H_B // G_B
    gw = hpg * P_B
    for g in range(G_B):
        bg = xbc[:, D_SSD + g * N_B:D_SSD + (g + 1) * N_B].astype(BF16)
        cg = xbc[:, D_SSD + G_B * N_B + g * N_B:D_SSD + G_B * N_B + (g + 1) * N_B].astype(BF16)
        cb = lax.dot_general(cg, bg, _NT, preferred_element_type=F32)
        inter = jnp.dot(cg, h_b[:, g * gw:(g + 1) * gw], preferred_element_type=F32)
        for j in range(hpg // 2):
            ms = []
            for hh in range(2):
                hc = H_B * d + hpg * g + 2 * j + hh
                seg = acum[:, hc:hc + 1] - acum_t[hc:hc + 1, :]
                dec = jnp.exp(jnp.where(mask, seg, NEG_BIG))
                ms.append((cb * dec * dt_t[hc:hc + 1, :]).astype(BF16))
            c0 = g * gw + 2 * j * P_B
            xp = xs_b[:, c0:c0 + LANES]
            zero = jnp.zeros_like(xp)
            rhs = jnp.concatenate([jnp.where(lo_half, xp, zero), jnp.where(lo_half, zero, xp)], axis=0)
            y_intra = jnp.dot(jnp.concatenate(ms, axis=1), rhs, preferred_element_type=F32)
            y_ref[:, c0:c0 + LANES] = y_intra + eacum_x[:, c0:c0 + LANES] * inter[:, 2 * j * P_B:2 * j * P_B + LANES]
        upd = lax.dot_general(bg, xw[:, g * gw:(g + 1) * gw], _TN, preferred_element_type=F32)
        h_ref[:, g * gw:(g + 1) * gw] = dec_x[:, g * gw:(g + 1) * gw] * h_old[:, g * gw:(g + 1) * gw] + upd


def _ssd_scan_kernel(xf_ref, dtf_ref, xb_ref, dtb_ref, arow_ref, trif_ref, trib_ref, ef_ref, eb_ref,
                     yf_ref, yb_ref, hf_s, hb_s):
    @pl.when(pl.program_id(1) == 0)
    def _():
        hf_s[...] = jnp.zeros_like(hf_s)
        hb_s[...] = jnp.zeros_like(hb_s)

    _ssd_one_direction(xf_ref[...], dtf_ref[...], arow_ref[...], trif_ref[...], ef_ref[...], hf_s, yf_ref, 0, False)
    _ssd_one_direction(xb_ref[...], dtb_ref[...], arow_ref[...], trib_ref[...], eb_ref[...], hb_s, yb_ref, 1, True)


def _chunk_rows(b, s, rev, chunk):
    n_c = CTX_LEN // chunk
    n_x = SEQ // chunk
    ctx_blk = T_X // chunk + b * n_c + ((n_c - 1 - s) if rev else s)
    x_blk = b * n_x + ((n_x + n_c - 1 - s) if rev else (s - n_c))
    return jnp.where(s < n_c, ctx_blk, x_blk)


def _ssd_scan(xact, dtp, a_log):
    q = SSD_CHUNK
    steps = (SEQ + CTX_LEN) // q
    a_neg = -jnp.exp(a_log)
    arow = jnp.concatenate([a_neg.reshape(1, 2 * H_B), jnp.zeros((1, LANES - 2 * H_B), F32)], axis=1)
    r = jnp.arange(q)
    tri_f = (r[None, :] <= r[:, None]).astype(BF16)
    tri_b = (r[None, :] >= r[:, None]).astype(BF16)
    col_head = jnp.arange(D_SSD) // P_B
    lane = jnp.arange(LANES)
    exp_f = (lane[:, None] == col_head[None, :]).astype(BF16)
    exp_b = (lane[:, None] == col_head[None, :] + H_B).astype(BF16)
    fwd = lambda b, s: (_chunk_rows(b, s, False, q), 0)
    bwd = lambda b, s: (_chunk_rows(b, s, True, q), 0)
    const = lambda b, s: (0, 0)
    return pl.pallas_call(
        _ssd_scan_kernel,
        grid=(BATCH, steps),
        in_specs=[pl.BlockSpec((q, CONV_CH), fwd), pl.BlockSpec((q, LANES), fwd),
                  pl.BlockSpec((q, CONV_CH), bwd), pl.BlockSpec((q, LANES), bwd),
                  pl.BlockSpec((1, LANES), const),
                  pl.BlockSpec((q, q), const), pl.BlockSpec((q, q), const),
                  pl.BlockSpec((LANES, D_SSD), const), pl.BlockSpec((LANES, D_SSD), const)],
        out_specs=[pl.BlockSpec((q, D_SSD), fwd), pl.BlockSpec((q, D_SSD), bwd)],
        out_shape=[jax.ShapeDtypeStruct((T_ALL, D_SSD), F32), jax.ShapeDtypeStruct((T_ALL, D_SSD), F32)],
        scratch_shapes=[pltpu.VMEM((N_B, D_SSD), F32), pltpu.VMEM((N_B, D_SSD), F32)],
        compiler_params=_cparams("arbitrary", "arbitrary"),
        name="ssd_scan",
    )(xact, dtp, xact, dtp, arow, tri_f, tri_b, exp_f, exp_b)


def _ssd_final_kernel(yf_ref, yb_ref, xs_ref, gate_ref, d_ref, onw_ref, o_ref):
    gate = gate_ref[...]
    y = (yf_ref[...] + yb_ref[...] + d_ref[...] * xs_ref[...]) * (gate * jax.nn.sigmoid(gate))
    o_ref[...] = (_rms(y) * onw_ref[...]).astype(o_ref.dtype)


def _ssd_final(yf, yb, xact, z, d_skip, onw):
    row = lambda i: (i, 0)
    const = lambda i: (0, 0)
    return pl.pallas_call(
        _ssd_final_kernel,
        grid=(T_ALL // ROW_TILE,),
        in_specs=[pl.BlockSpec((ROW_TILE, D_SSD), row), pl.BlockSpec((ROW_TILE, D_SSD), row),
                  pl.BlockSpec((ROW_TILE, D_SSD), row),
                  pl.BlockSpec((ROW_TILE, D_SSD), lambda i: (i, Z_SZ // D_SSD)),
                  pl.BlockSpec((1, D_SSD), const), pl.BlockSpec((1, D_SSD), const)],
        out_specs=pl.BlockSpec((ROW_TILE, D_SSD), row),
        out_shape=jax.ShapeDtypeStruct((T_ALL, D_SSD), BF16),
        compiler_params=_cparams("arbitrary"),
        name="ssd_final",
    )(yf, yb, xact, z, jnp.repeat(d_skip, P_B).reshape(1, D_SSD), onw.reshape(1, D_SSD))


def _ssd_mixer_pallas(z, p, onw):
    xact, dtp = _ssd_prep(z, p["ssd_conv_w"], p["ssd_conv_b"], p["ssd_dt_bias"])
    yf, yb = _ssd_scan(xact, dtp, p["ssd_a_log"])
    return _ssd_final(yf, yb, xact, z, p["ssd_d"], onw)


GATE_LANE0 = 2 * H_B


def _mlstm_one_direction(q, k, v, gi, gf, bi, bf, tri, st_ref, m_ref, h_ref, d, rev):
    n = MLSTM_CHUNK
    li = gi + bi
    lf = jax.nn.log_sigmoid(gf + bf)
    b = _dot3_left(tri, lf)
    b_t = b.T
    li_t = li.T
    edge = 0 if rev else n - 1
    gtot = b[edge:edge + 1, :]
    m_old = m_ref[0:1, :]
    w_log = gtot - b + li
    m_new = jnp.maximum(gtot + m_old, jnp.max(w_log, axis=0, keepdims=True))
    wj = jnp.exp(w_log - m_new)
    dec = jnp.exp(gtot + m_old - m_new)
    inter_log = b + m_old
    m_ref[0:1, :] = m_new
    ri = lax.broadcasted_iota(jnp.int32, (n, n), 0)
    ci = lax.broadcasted_iota(jnp.int32, (n, n), 1)
    mask = (ci >= ri) if rev else (ci <= ri)
    lo_half = lax.broadcasted_iota(jnp.int32, (n, LANES), 1) < DQK_C
    row_lo = lax.broadcasted_iota(jnp.int32, (2 * DQK_C, 2 * DV_C), 0) < DQK_C
    ones = jnp.ones((n, DV_C), F32)
    for j in range(H_C // 2):
        qp = q[:, j * LANES:(j + 1) * LANES] * DQK_C ** -0.5
        kp = k[:, j * LANES:(j + 1) * LANES].astype(BF16)
        st_old = st_ref[j]
        st_b = st_old.astype(BF16)
        upds = []
        for hh in range(2):
            h = 2 * j + hh
            gl = GATE_LANE0 + H_C * d + h
            qm = jnp.where(lo_half if hh == 0 else jnp.logical_not(lo_half), qp, 0.0).astype(BF16)
            qk = lax.dot_general(qm, kp, _NT, preferred_element_type=F32)
            dmat = jnp.where(mask, b[:, gl:gl + 1] - b_t[gl:gl + 1, :] + li_t[gl:gl + 1, :], NEG_BIG)
            il = inter_log[:, gl:gl + 1]
            m_row = jnp.maximum(il, jnp.max(dmat, axis=1, keepdims=True))
            s = qk * jnp.exp(dmat - m_row)
            w_inter = jnp.exp(il - m_row)
            vh = v[:, h * DV_C:(h + 1) * DV_C]
            qs = jnp.dot(qm, st_b, preferred_element_type=F32)
            num = jnp.dot(s.astype(BF16), vh.astype(BF16), preferred_element_type=F32) + w_inter * qs[:, :DV_C]
            den = jnp.sum(s, axis=1, keepdims=True) + w_inter * qs[:, DV_C:]
            h_ref[:, h * DV_C:(h + 1) * DV_C] = num / jnp.maximum(jnp.abs(den), jnp.exp(-m_row))
            rhs = (wj[:, gl:gl + 1] * jnp.concatenate([vh, ones], axis=1)).astype(BF16)
            upds.append(lax.dot_general(kp, rhs, _TN, preferred_element_type=F32))
        ga = GATE_LANE0 + H_C * d + 2 * j
        decv = jnp.where(row_lo, dec[:, ga:ga + 1], dec[:, ga + 1:ga + 2])
        st_ref[j] = decv * st_old + jnp.where(row_lo, upds[0], upds[1])


def _mlstm_scan_kernel(qf, kf, vf, gif, gff, qb, kb, vb, gib, gfb, bi_ref, bf_ref, trif_ref, trib_ref,
                       hf_ref, hb_ref, stf_s, stb_s, mf_s, mb_s):
    @pl.when(pl.program_id(1) == 0)
    def _():
        stf_s[...] = jnp.zeros_like(stf_s)
        stb_s[...] = jnp.zeros_like(stb_s)
        mf_s[...] = jnp.full_like(mf_s, NEG_STATE)
        mb_s[...] = jnp.full_like(mb_s, NEG_STATE)

    _mlstm_one_direction(qf[...], kf[...], vf[...], gif[...], gff[...], bi_ref[...], bf_ref[...], trif_ref[...],
                         stf_s, mf_s, hf_ref, 0, False)
    _mlstm_one_direction(qb[...], kb[...], vb[...], gib[...], gfb[...], bi_ref[...], bf_ref[...], trib_ref[...],
                         stb_s, mb_s, hb_ref, 1, True)


def _mlstm_scan(z, gate_b):
    n = MLSTM_CHUNK
    steps = (SEQ + CTX_LEN) // n
    qkw = H_C * DQK_C
    pad = lambda t: jnp.concatenate([jnp.zeros((1, GATE_LANE0), F32), t.reshape(1, 2 * H_C),
                                     jnp.zeros((1, LANES - GATE_LANE0 - 2 * H_C), F32)], axis=1)
    bi = pad(gate_b[:, 0, :])
    bf = pad(gate_b[:, 1, :])
    r = jnp.arange(n)
    tri_f = (r[None, :] <= r[:, None]).astype(BF16)
    tri_b = (r[None, :] >= r[:, None]).astype(BF16)

    def specs(rev):
        rows = lambda b, s: _chunk_rows(b, s, rev, n)
        return [pl.BlockSpec((n, qkw), lambda b, s: (rows(b, s), Z_MQ // qkw)),
                pl.BlockSpec((n, qkw), lambda b, s: (rows(b, s), Z_MK // qkw)),
                pl.BlockSpec((n, D_MLSTM), lambda b, s: (rows(b, s), Z_MV // D_MLSTM)),
                pl.BlockSpec((n, LANES), lambda b, s: (rows(b, s), Z_MG // LANES)),
                pl.BlockSpec((n, LANES), lambda b, s: (rows(b, s), Z_DT // LANES))]

    const = lambda b, s: (0, 0)
    return pl.pallas_call(
        _mlstm_scan_kernel,
        grid=(BATCH, steps),
        in_specs=specs(False) + specs(True) + [
            pl.BlockSpec((1, LANES), const), pl.BlockSpec((1, LANES), const),
            pl.BlockSpec((n, n), const), pl.BlockSpec((n, n), const)],
        out_specs=[pl.BlockSpec((n, D_MLSTM), lambda b, s: (_chunk_rows(b, s, False, n), 0)),
                   pl.BlockSpec((n, D_MLSTM), lambda b, s: (_chunk_rows(b, s, True, n), 0))],
        out_shape=[jax.ShapeDtypeStruct((T_ALL, D_MLSTM), F32), jax.ShapeDtypeStruct((T_ALL, D_MLSTM), F32)],
        scratch_shapes=[pltpu.VMEM((H_C // 2, 2 * DQK_C, 2 * DV_C), F32),
                        pltpu.VMEM((H_C // 2, 2 * DQK_C, 2 * DV_C), F32),
                        pltpu.VMEM((8, LANES), F32), pltpu.VMEM((8, LANES), F32)],
        compiler_params=_cparams("arbitrary", "arbitrary"),
        name="mlstm_scan",
    )(z, z, z, z, z, z, z, z, z, z, bi, bf, tri_f, tri_b)


def _mlstm_final_kernel(hf_ref, hb_ref, o_ref_in, onw_ref, o_ref):
    gate = jax.nn.sigmoid(o_ref_in[...])
    onw = onw_ref[...]
    for h in range(H_C):
        cs = slice(h * DV_C, (h + 1) * DV_C)
        hn = _rms(hf_ref[:, cs] + hb_ref[:, cs])
        o_ref[:, cs] = (hn * gate[:, cs] * onw[:, cs]).astype(o_ref.dtype)


def _mlstm_final(hf, hb, z, onw):
    row = lambda i: (i, 0)
    return pl.pallas_call(
        _mlstm_final_kernel,
        grid=(T_ALL // ROW_TILE,),
        in_specs=[pl.BlockSpec((ROW_TILE, D_MLSTM), row), pl.BlockSpec((ROW_TILE, D_MLSTM), row),
                  pl.BlockSpec((ROW_TILE, D_MLSTM), lambda i: (i, Z_MO // D_MLSTM)),
                  pl.BlockSpec((1, D_MLSTM), lambda i: (0, 0))],
        out_specs=pl.BlockSpec((ROW_TILE, D_MLSTM), row),
        out_shape=jax.ShapeDtypeStruct((T_ALL, D_MLSTM), BF16),
        compiler_params=_cparams("arbitrary"),
        name="mlstm_final",
    )(hf, hb, z, onw.reshape(1, D_MLSTM))


def _mlstm_mixer_pallas(z, p, onw):
    hf, hb = _mlstm_scan(z, p["mlstm_gate_b"])
    return _mlstm_final(hf, hb, z, onw)


S5_Q = 256
S5_SEG = S5_Q // 8
S5_LANES = G_S5 * P_S5
S5_SLAB = 512
S5_NSLAB = S5_LANES // S5_SLAB


def _s5_dir_kernel(u_ref, perm_ref, wbr_ref, wbi_ref, atab_ref, apr_ref, api_ref, wcr_ref, wci_ref, y_ref,
                   ur_s, ui_s, xr_s, xi_s, st_s, car_s, *, reverse):
    @pl.when(pl.program_id(1) == 0)
    def _():
        st_s[...] = jnp.zeros_like(st_s)

    up = jnp.dot(perm_ref[...], u_ref[...].astype(BF16), preferred_element_type=F32).astype(BF16)
    for m in range(S5_NSLAB):
        um = up[:, m * LANES:(m + 1) * LANES]
        ur_s[:, m * S5_SLAB:(m + 1) * S5_SLAB] = jnp.dot(um, wbr_ref[m], preferred_element_type=F32)
        ui_s[:, m * S5_SLAB:(m + 1) * S5_SLAB] = jnp.dot(um, wbi_ref[m], preferred_element_type=F32)

    per = 4
    for grp in range(S5_LANES // (per * LANES)):
        cols = [grp * per * LANES + j * LANES for j in range(per)]
        a_r = [atab_ref[0:8, c0:c0 + LANES] for c0 in cols]
        a_i = [atab_ref[8:16, c0:c0 + LANES] for c0 in cols]

        def body(i, carry, cols=cols, a_r=a_r, a_i=a_i):
            t = (S5_SEG - 1 - i) if reverse else i
            r0 = pl.multiple_of(t * 8, 8)
            new = []
            for j, c0 in enumerate(cols):
                xr, xi = carry[2 * j], carry[2 * j + 1]
                nr = a_r[j] * xr - a_i[j] * xi + ur_s[pl.ds(r0, 8), c0:c0 + LANES]
                ni = a_r[j] * xi + a_i[j] * xr + ui_s[pl.ds(r0, 8), c0:c0 + LANES]
                ur_s[pl.ds(r0, 8), c0:c0 + LANES] = nr
                ui_s[pl.ds(r0, 8), c0:c0 + LANES] = ni
                new += [nr, ni]
            return tuple(new)

        lax.fori_loop(0, S5_SEG, body, tuple(jnp.zeros((8, LANES), F32) for _ in range(2 * per)), unroll=2)

    as_r = atab_ref[16:17, :]
    as_i = atab_ref[17:18, :]
    end_row = 0 if reverse else 8 * (S5_SEG - 1)
    cr = st_s[0:1, :]
    ci = st_s[1:2, :]
    for k in (range(7, -1, -1) if reverse else range(8)):
        car_s[k:k + 1, :] = cr
        car_s[8 + k:9 + k, :] = ci
        er = ur_s[end_row + k:end_row + k + 1, :]
        ei = ui_s[end_row + k:end_row + k + 1, :]
        cr, ci = er + as_r * cr - as_i * ci, ei + as_r * ci + as_i * cr
    st_s[0:1, :] = cr
    st_s[1:2, :] = ci

    for m in range(S5_NSLAB):
        cs = slice(m * S5_SLAB, (m + 1) * S5_SLAB)
        c_r = jnp.concatenate([car_s[0:8, cs], car_s[0:8, cs]], axis=0)
        c_i = jnp.concatenate([car_s[8:16, cs], car_s[8:16, cs]], axis=0)

        def fix(i, _, cs=cs, c_r=c_r, c_i=c_i):
            r0 = pl.multiple_of(i * 16, 16)
            p_r = apr_ref[pl.ds(r0, 16), cs]
            p_i = api_ref[pl.ds(r0, 16), cs]
            xr_s[pl.ds(r0, 16), cs] = (ur_s[pl.ds(r0, 16), cs] + p_r * c_r - p_i * c_i).astype(BF16)
            xi_s[pl.ds(r0, 16), cs] = (ui_s[pl.ds(r0, 16), cs] + p_r * c_i + p_i * c_r).astype(BF16)
            return 0

        lax.fori_loop(0, S5_Q // 16, fix, 0, unroll=2)

    for m in range(S5_NSLAB):
        cs = slice(m * S5_SLAB, (m + 1) * S5_SLAB)
        y_ref[:, m * LANES:(m + 1) * LANES] = (
            jnp.dot(xr_s[:, cs], wcr_ref[m], preferred_element_type=F32)
            - jnp.dot(xi_s[:, cs], wci_ref[m], preferred_element_type=F32))


def _s5_direction(z, perm, wbr, wbi, atab, apr, api, wcr, wci, reverse):
    n_x = SEQ // S5_Q
    ctx_blk = T_X // S5_Q

    def rows(b, s):
        xs = (n_x - s) if reverse else (s - 1)
        return jnp.where(s == 0, ctx_blk + b, b * n_x + xs)

    const2 = lambda b, s: (0, 0)
    const3 = lambda b, s: (0, 0, 0)
    return pl.pallas_call(
        functools.partial(_s5_dir_kernel, reverse=reverse),
        grid=(BATCH, n_x + 1),
        in_specs=[pl.BlockSpec((S5_Q, D_S5), lambda b, s: (rows(b, s), Z_D // D_S5)),
                  pl.BlockSpec((S5_Q, S5_Q), const2),
                  pl.BlockSpec(wbr.shape, const3),
                  pl.BlockSpec(wbi.shape, const3),
                  pl.BlockSpec(atab.shape, const2),
                  pl.BlockSpec(apr.shape, const2),
                  pl.BlockSpec(api.shape, const2),
                  pl.BlockSpec(wcr.shape, const3),
                  pl.BlockSpec(wci.shape, const3)],
        out_specs=pl.BlockSpec((S5_Q, D_S5), lambda b, s: (rows(b, s), 0)),
        out_shape=jax.ShapeDtypeStruct((T_ALL, D_S5), F32),
        scratch_shapes=[pltpu.VMEM((S5_Q, S5_LANES), F32), pltpu.VMEM((S5_Q, S5_LANES), F32),
                        pltpu.VMEM((S5_Q, S5_LANES), BF16), pltpu.VMEM((S5_Q, S5_LANES), BF16),
                        pltpu.VMEM((8, S5_LANES), F32), pltpu.VMEM((16, S5_LANES), F32)],
        compiler_params=_cparams("arbitrary", "arbitrary"),
        name="s5_bwd" if reverse else "s5_fwd",
    )(z, perm, wbr, wbi, atab, apr, api, wcr, wci)


def _s5_final_kernel(yf_ref, yb_ref, u_ref, permt_ref, d_ref, gw_ref, gb_ref, onw_ref, o_ref):
    ys = yf_ref[...] + yb_ref[...]
    hi = ys.astype(BF16)
    r1 = ys - hi.astype(F32)
    mid = r1.astype(BF16)
    lo = (r1 - mid.astype(F32)).astype(BF16)
    pt = permt_ref[...]
    y = (jnp.dot(pt, hi, preferred_element_type=F32) + jnp.dot(pt, mid, preferred_element_type=F32)
         + jnp.dot(pt, lo, preferred_element_type=F32))
    y = jax.nn.gelu(y + d_ref[...] * u_ref[...])
    gate = jax.nn.sigmoid(jnp.dot(y.astype(BF16), gw_ref[...], preferred_element_type=F32) + gb_ref[...])
    o_ref[...] = (_rms(y * gate) * onw_ref[...]).astype(o_ref.dtype)


def _s5_final(yf, yb, z, permt, d_skip, glu_w, glu_b, onw):
    row = lambda i: (i, 0)
    const = lambda i: (0, 0)
    return pl.pallas_call(
        _s5_final_kernel,
        grid=(T_ALL // S5_Q,),
        in_specs=[pl.BlockSpec((S5_Q, D_S5), row),
                  pl.BlockSpec((S5_Q, D_S5), row),
                  pl.BlockSpec((S5_Q, D_S5), lambda i: (i, Z_D // D_S5)),
                  pl.BlockSpec((S5_Q, S5_Q), const),
                  pl.BlockSpec((1, D_S5), const),
                  pl.BlockSpec((D_S5, D_S5), const),
                  pl.BlockSpec((1, D_S5), const),
                  pl.BlockSpec((1, D_S5), const)],
        out_specs=pl.BlockSpec((S5_Q, D_S5), row),
        out_shape=jax.ShapeDtypeStruct((T_ALL, D_S5), BF16),
        compiler_params=_cparams("arbitrary"),
        name="s5_final",
    )(yf, yb, z, permt, d_skip.reshape(1, D_S5), glu_w.astype(BF16), glu_b.reshape(1, D_S5), onw.reshape(1, D_S5))


def _s5_tables(lam_re, lam_im, log_dt, b_re, b_im, c_re, c_im):
    dt = jnp.exp(log_dt)[..., None]
    mag = jnp.exp(lam_re * dt)
    ar = mag * jnp.cos(lam_im * dt)
    ai = mag * jnp.sin(lam_im * dt)
    den = lam_re * lam_re + lam_im * lam_im
    cr_ = ((ar - 1.0) * lam_re + ai * lam_im) / den
    ci_ = (ai * lam_re - (ar - 1.0) * lam_im) / den
    bbr = cr_[..., None] * b_re - ci_[..., None] * b_im
    bbi = cr_[..., None] * b_im + ci_[..., None] * b_re
    gps = S5_SLAB // P_S5
    eye = jnp.eye(gps, dtype=F32)

    def drive_w(bb):
        t = bb.reshape(S5_NSLAB, gps, P_S5, S5_GROUP)
        w = jnp.einsum('mgpc,gh->mgchp', t, eye)
        return w.reshape(S5_NSLAB, gps * S5_GROUP, gps * P_S5).astype(BF16)

    def read_w(cc):
        t = cc.reshape(S5_NSLAB, gps, S5_GROUP, P_S5)
        w = jnp.einsum('mgcp,gh->mgphc', t, eye)
        return w.reshape(S5_NSLAB, gps * P_S5, gps * S5_GROUP).astype(BF16)

    pows_r, pows_i = [], []
    for d in range(2):
        a_r = ar[d].reshape(1, S5_LANES)
        a_i = ai[d].reshape(1, S5_LANES)
        pr, pi = [a_r], [a_i]
        for _ in range(S5_SEG - 1):
            pr, pi = pr + [pr[-1] * a_r - pi[-1] * a_i], pi + [pr[-1] * a_i + pi[-1] * a_r]
        pows_r.append(pr)
        pows_i.append(pi)

    out = []
    for d in range(2):
        pr, pi = pows_r[d], pows_i[d]
        order = list(range(S5_SEG))
        if d == 1:
            order = order[::-1]
        apr = jnp.concatenate([jnp.broadcast_to(pr[k], (8, S5_LANES)) for k in order], axis=0)
        api = jnp.concatenate([jnp.broadcast_to(pi[k], (8, S5_LANES)) for k in order], axis=0)
        atab = jnp.concatenate([jnp.broadcast_to(pr[0], (8, S5_LANES)), jnp.broadcast_to(pi[0], (8, S5_LANES)),
                                pr[-1], pi[-1], jnp.zeros((6, S5_LANES), F32)], axis=0)
        out.append((drive_w(bbr[d]), drive_w(bbi[d]), atab, apr, api))
    r = jnp.arange(S5_Q)
    src = (r % 8) * S5_SEG + r // 8
    perm = (src[:, None] == jnp.arange(S5_Q)[None, :]).astype(BF16)
    return out, read_w(c_re), read_w(c_im), perm


def _s5_mixer_pallas(z, p, onw):
    dirs, wcr, wci, perm = _s5_tables(p["s5_lam_re"], p["s5_lam_im"], p["s5_log_dt"], p["s5_b_re"], p["s5_b_im"],
                                      p["s5_c_re"], p["s5_c_im"])
    yf = _s5_direction(z, perm, *dirs[0], wcr, wci, reverse=False)
    yb = _s5_direction(z, perm, *dirs[1], wcr, wci, reverse=True)
    return _s5_final(yf, yb, z, perm.T, p["s5_d"], p["s5_glu_w"], p["s5_glu_b"], onw)


def _rope_tables():
    pos = jnp.arange(SEQ)
    row = (pos // GRID_W).astype(F32)
    col = (pos % GRID_W).astype(F32)
    inv_freq = ROPE_BASE ** (-jnp.arange(ROPE_AXIS // 2, dtype=F32) * 2.0 / ROPE_AXIS)
    ang_r = row[:, None] * inv_freq
    ang_c = col[:, None] * inv_freq
    zeros = jnp.zeros((SEQ, LANES - D_ROPE), F32)
    cos = jnp.concatenate([jnp.cos(ang_r), jnp.cos(ang_r), jnp.cos(ang_c), jnp.cos(ang_c), zeros], axis=1)
    sin = jnp.concatenate([jnp.sin(ang_r), jnp.sin(ang_r), jnp.sin(ang_c), jnp.sin(ang_c), zeros], axis=1)
    cos_c = jnp.concatenate([jnp.ones((ROW_TILE, D_ROPE), F32), jnp.zeros((ROW_TILE, LANES - D_ROPE), F32)], axis=1)
    sin_c = jnp.zeros((ROW_TILE, LANES), F32)
    return jnp.concatenate([jnp.concatenate([cos, sin], axis=1), jnp.concatenate([cos_c, sin_c], axis=1)], axis=0)


def _rot_cols(w):
    q = ROPE_AXIS // 2
    return jnp.concatenate([-w[:, q:2 * q], w[:, 0:q], -w[:, 3 * q:4 * q], w[:, 2 * q:3 * q]], axis=1)


def _layout_w_in(w):
    k = w.shape[0]
    zpad = lambda n: jnp.zeros((k, n), w.dtype)
    b0 = A_COLS
    c0 = A_COLS + B_COLS
    d0 = A_COLS + B_COLS + C_COLS
    kr = w[:, Q_RANK + KV_RANK:A_COLS]
    o1 = H_C * DQK_C
    cm = c0 + 2 * o1
    gb = cm + 2 * D_MLSTM
    cols = [w[:, 0:Q_RANK + KV_RANK], kr, zpad(LANES - D_ROPE), _rot_cols(kr), zpad(LANES - D_ROPE),
            w[:, c0:c0 + o1],
            w[:, b0 + D_SSD:b0 + D_SSD + CONV_CH],
            w[:, b0:b0 + D_SSD],
            w[:, cm:cm + 2 * D_MLSTM],
            w[:, d0:],
            w[:, c0 + o1:cm],
            w[:, b0 + D_SSD + CONV_CH:c0], w[:, gb + H_C:gb + 2 * H_C], w[:, gb + 3 * H_C:gb + 4 * H_C],
            zpad(LANES - GATE_LANE0 - 2 * H_C),
            zpad(GATE_LANE0), w[:, gb:gb + H_C], w[:, gb + 2 * H_C:gb + 3 * H_C],
            zpad(LANES - GATE_LANE0 - 2 * H_C)]
    out = jnp.concatenate(cols, axis=1)
    assert out.shape[1] == Z_COLS
    return out.astype(BF16)


def _layout_mla(w_uq, w_ukv):
    k = w_uq.shape[0]
    qa, qb, wk, wv = [], [], [], []
    for h in range(H_A):
        base = h * (D_NOPE + D_ROPE)
        rope = w_uq[:, base + D_NOPE:base + D_NOPE + D_ROPE]
        qa += [w_uq[:, base:base + D_NOPE], rope, jnp.zeros((k, LANES - D_ROPE), w_uq.dtype)]
        qb += [_rot_cols(rope), jnp.zeros((k, LANES - D_ROPE), w_uq.dtype)]
        kb = h * (D_NOPE + D_V)
        wk.append(w_ukv[:, kb:kb + D_NOPE])
        wv.append(w_ukv[:, kb + D_NOPE:kb + D_NOPE + D_V])
    cat = lambda xs: jnp.concatenate(xs, axis=1).astype(BF16)
    return cat(qa), cat(qb), cat(wk), cat(wv)


def _layer(xall, modtab, p, cs, last):
    hx = _prenorm(xall, p["norm1_w"], modtab, 0, 1)
    z = _mm(hx, _layout_w_in(p["w_in"]), ROW_TILE, Z_COLS // 3)
    onw = p["out_norm_w"]
    wqa, wqb, wk, wv = _layout_mla(p["mla_w_uq"], p["mla_w_ukv"])
    q, k, v = _mla_prep(z, cs, p["mla_q_norm_w"], p["mla_kv_norm_w"], wqa, wqb, wk, wv)
    onw_a = onw[:D_MLA].reshape(1, D_MLA)
    ya = jnp.concatenate([_attention_latent(q, k, v, onw_a), _attention_ctx(q, k, v, onw_a)], axis=0)
    yb = _ssd_mixer_pallas(z, p, onw[D_MLA:D_MLA + D_SSD])
    yc = _mlstm_mixer_pallas(z, p, onw[D_MLA + D_SSD:D_MLA + D_SSD + D_MLSTM])
    yd = _s5_mixer_pallas(z, p, onw[D_MLA + D_SSD + D_MLSTM:])
    xall = _outproj([ya, yb, yc, yd], p["w_out"].astype(BF16), xall, modtab, 2)

    w_router = jnp.concatenate([p["moe_w_group"], p["moe_w_expert"],
                                jnp.zeros((D_MODEL, ROUTER_COLS - N_GROUPS - N_EXPERTS), F32)], axis=1)
    n_tok = T_X if last else T_ALL
    h2, logits = _prenorm_router(xall[:n_tok], p["norm2_w"], modtab, w_router, 3, 4)
    f = _moe(h2, logits, p["moe_w_gate"], p["moe_w_up"], p["moe_w_down"])
    g2_rows = jnp.concatenate([jnp.repeat(modtab[:BATCH, 5], SEQ, axis=0),
                               jnp.broadcast_to(modtab[BATCH, 5], (T_C, D_MODEL))], axis=0)[:n_tok]
    return xall[:n_tok] + g2_rows * f


def kernel(x, c, ctx, c_ctx, mod_w, mod_b, norm1_w, w_in, mla_q_norm_w, mla_kv_norm_w, mla_w_uq, mla_w_ukv,
           ssd_conv_w, ssd_conv_b, ssd_a_log, ssd_dt_bias, ssd_d, mlstm_gate_b, s5_lam_re, s5_lam_im,
           s5_log_dt, s5_b_re, s5_b_im, s5_c_re, s5_c_im, s5_d, s5_glu_w, s5_glu_b, out_norm_w, w_out,
           norm2_w, moe_w_group, moe_w_expert, moe_w_gate, moe_w_up, moe_w_down, final_norm_w):
    stacked = {"norm1_w": norm1_w, "w_in": w_in, "mla_q_norm_w": mla_q_norm_w, "mla_kv_norm_w": mla_kv_norm_w,
               "mla_w_uq": mla_w_uq, "mla_w_ukv": mla_w_ukv, "ssd_conv_w": ssd_conv_w, "ssd_conv_b": ssd_conv_b,
               "ssd_a_log": ssd_a_log, "ssd_dt_bias": ssd_dt_bias, "ssd_d": ssd_d, "mlstm_gate_b": mlstm_gate_b,
               "s5_lam_re": s5_lam_re, "s5_lam_im": s5_lam_im, "s5_log_dt": s5_log_dt, "s5_b_re": s5_b_re,
               "s5_b_im": s5_b_im, "s5_c_re": s5_c_re, "s5_c_im": s5_c_im, "s5_d": s5_d, "s5_glu_w": s5_glu_w,
               "s5_glu_b": s5_glu_b, "out_norm_w": out_norm_w, "w_out": w_out, "norm2_w": norm2_w,
               "moe_w_group": moe_w_group, "moe_w_expert": moe_w_expert, "moe_w_gate": moe_w_gate,
               "moe_w_up": moe_w_up, "moe_w_down": moe_w_down}
    cs = _rope_tables()
    cc = jnp.concatenate([c, c_ctx[None, :], jnp.zeros((8 - BATCH - 1, D_MODEL), F32)], axis=0)
    xall = jnp.concatenate([x.reshape(T_X, D_MODEL), ctx.reshape(T_C, D_MODEL)], axis=0)
    for l in range(DEPTH):
        p = {name: val[l] for name, val in stacked.items()}
        modtab = _modulation(cc, mod_w[l], mod_b[l])[:BATCH + 1].reshape(BATCH + 1, 6, D_MODEL)
        xall = _layer(xall, modtab, p, cs, l == DEPTH - 1)
    return _final_norm(xall, final_norm_w).reshape(BATCH, SEQ, D_MODEL)
```

```python
import functools
import math

import jax
import jax.numpy as jnp
from jax import lax
from jax.experimental import pallas as pl
from jax.experimental.pallas import tpu as pltpu

F32 = jnp.float32
BF16 = jnp.bfloat16

D_MODEL = 2048
BATCH = 2
SEQ = 4096
DEPTH = 2
GRID_W = 64
CTX_LEN = 256
EPS = 1e-6
NEG_STATE = -1e30

H_A = 4
D_NOPE = 128
D_ROPE = 64
D_V = 128
Q_RANK = 384
KV_RANK = 128
ROPE_AXIS = D_ROPE // 2
ROPE_BASE = 10000.0
D_MLA = H_A * D_V
D_SSD = 512
P_B = 64
H_B = D_SSD // P_B
G_B = 2
N_B = 128
SSD_CONV = 5
SSD_CHUNK = 128
CONV_CH = D_SSD + 2 * G_B * N_B
D_MLSTM = 512
H_C = 4
DV_C = D_MLSTM // H_C
DQK_C = DV_C // 2
MLSTM_CHUNK = 128
D_S5 = 512
S5_GROUP = 16
G_S5 = D_S5 // S5_GROUP
P_S5 = 64
A_COLS = Q_RANK + KV_RANK + D_ROPE
B_COLS = D_SSD + CONV_CH + 2 * H_B
C_COLS = 2 * H_C * DQK_C + 2 * D_MLSTM + 4 * H_C
N_GROUPS = 4
EXPERTS_PER_GROUP = 8
N_EXPERTS = N_GROUPS * EXPERTS_PER_GROUP
TOP_K = 2
D_EXPERT = 512

T_X = BATCH * SEQ
T_C = BATCH * CTX_LEN
T_ALL = T_X + T_C

LANES = 128
VMEM_LIMIT_BYTES = 56 * 1024 * 1024

ROW_TILE = 512
TILES_PER_BATCH = SEQ // ROW_TILE
N_X_TILES = T_X // ROW_TILE

Z_CQ = 0
Z_CKV = 384
Z_KR = 512
Z_KRR = 640
Z_MQ = 768
Z_XBC = 1024
Z_SZ = 2048
Z_MV = 2560
Z_MO = 3072
Z_D = 3584
Z_MK = 4096
Z_DT = 4352
Z_MG = 4480
Z_COLS = 4608

MOE_ROWS = 256
ROUTER_COLS = 128


def _cparams(*sem):
    return pltpu.CompilerParams(dimension_semantics=sem, vmem_limit_bytes=VMEM_LIMIT_BYTES)


def _mod_row(i):
    return jnp.minimum(i // TILES_PER_BATCH, BATCH)


def _mod_kernel(a_ref, w_ref, b_ref, o_ref):
    a = a_ref[...]
    a = a * jax.nn.sigmoid(a)
    o_ref[...] = jnp.dot(a.astype(BF16), w_ref[...].astype(BF16), preferred_element_type=F32) + b_ref[...]


def _modulation(cc, w, b):
    n = w.shape[1]
    tn = 1024
    return pl.pallas_call(
        _mod_kernel,
        grid=(n // tn,),
        in_specs=[pl.BlockSpec((8, D_MODEL), lambda j: (0, 0)),
                  pl.BlockSpec((D_MODEL, tn), lambda j: (0, j)),
                  pl.BlockSpec((1, tn), lambda j: (0, j))],
        out_specs=pl.BlockSpec((8, tn), lambda j: (0, j)),
        out_shape=jax.ShapeDtypeStruct((8, n), F32),
        compiler_params=_cparams("arbitrary"),
        name="modulation",
    )(cc, w, b.reshape(1, n))


def _prenorm_kernel(x_ref, w_ref, mod_ref, o_ref, *, sh_idx, sc_idx):
    x = x_ref[...]
    y = x * lax.rsqrt(jnp.mean(x * x, axis=-1, keepdims=True) + EPS) * w_ref[...]
    m = mod_ref[0]
    y = y * (1.0 + m[sc_idx:sc_idx + 1, :]) + m[sh_idx:sh_idx + 1, :]
    o_ref[...] = y.astype(o_ref.dtype)


def _prenorm(x, w, modtab, sh_idx, sc_idx):
    rows = x.shape[0]
    return pl.pallas_call(
        functools.partial(_prenorm_kernel, sh_idx=sh_idx, sc_idx=sc_idx),
        grid=(rows // ROW_TILE,),
        in_specs=[pl.BlockSpec((ROW_TILE, D_MODEL), lambda i: (i, 0)),
                  pl.BlockSpec((1, D_MODEL), lambda i: (0, 0)),
                  pl.BlockSpec((1, 6, D_MODEL), lambda i: (_mod_row(i), 0, 0))],
        out_specs=pl.BlockSpec((ROW_TILE, D_MODEL), lambda i: (i, 0)),
        out_shape=jax.ShapeDtypeStruct((rows, D_MODEL), BF16),
        compiler_params=_cparams("arbitrary"),
        name="prenorm",
    )(x, w.reshape(1, D_MODEL), modtab)


R_E0, R_E1, R_W0, R_W1, R_K0, R_K1 = range(6)
NO_LANE = 2 * LANES


def _prenorm_router_kernel(x_ref, w_ref, mod_ref, wr_ref, tri_ref, h_ref, route_ref, cnt_ref, cnt_s,
                           *, sh_idx, sc_idx):
    @pl.when(pl.program_id(0) == 0)
    def _():
        cnt_s[...] = jnp.zeros_like(cnt_s)

    x = x_ref[...]
    y = x * lax.rsqrt(jnp.mean(x * x, axis=-1, keepdims=True) + EPS) * w_ref[...]
    m = mod_ref[0]
    y = y * (1.0 + m[sc_idx:sc_idx + 1, :]) + m[sh_idx:sh_idx + 1, :]
    h_ref[...] = y
    lg = jnp.dot(y, wr_ref[...], preferred_element_type=F32, precision=lax.Precision.HIGHEST)
    lane = lax.broadcasted_iota(jnp.int32, lg.shape, 1)

    def first_max(v):
        top = jnp.max(v, axis=1, keepdims=True)
        return top, jnp.min(jnp.where(v == top, lane, NO_LANE), axis=1, keepdims=True)

    is_g = lane < N_GROUPS
    g_top, g_idx = first_max(jnp.where(is_g, lg, NEG_BIG))
    g_w = 1.0 / jnp.sum(jnp.where(is_g, jnp.exp(lg - g_top), 0.0), axis=1, keepdims=True)
    lo = N_GROUPS + EXPERTS_PER_GROUP * g_idx
    el = jnp.where(jnp.logical_and(lane >= lo, lane < lo + EXPERTS_PER_GROUP), lg, NEG_BIG)
    v1, i1 = first_max(el)
    v2, i2 = first_max(jnp.where(lane == i1, NEG_BIG, el))
    t = jnp.exp(v2 - v1)
    w0 = g_w / (1.0 + t)
    w1 = g_w * t / (1.0 + t)
    e0 = i1 - N_GROUPS
    e1 = i2 - N_GROUPS
    hit0 = lane == e0
    hit1 = lane == e1
    onehot = jnp.logical_or(hit0, hit1).astype(F32)
    before = cnt_s[0:1, :] + jnp.dot(tri_ref[...], onehot.astype(BF16), preferred_element_type=F32)
    k0 = jnp.sum(jnp.where(hit0, before, 0.0), axis=1, keepdims=True)
    k1 = jnp.sum(jnp.where(hit1, before, 0.0), axis=1, keepdims=True)
    cnt_s[0:1, :] = cnt_s[0:1, :] + jnp.sum(onehot, axis=0, keepdims=True)
    cnt_ref[...] = cnt_s[...]
    rec = jnp.zeros(lg.shape, F32)
    for ln, val in ((R_E0, e0.astype(F32)), (R_E1, e1.astype(F32)), (R_W0, w0), (R_W1, w1), (R_K0, k0), (R_K1, k1)):
        rec = jnp.where(lane == ln, val, rec)
    route_ref[...] = rec


def _prenorm_router(x, w, modtab, w_router, sh_idx, sc_idx):
    rows = x.shape[0]
    r = jnp.arange(ROW_TILE)
    tri = (r[None, :] < r[:, None]).astype(BF16)
    return pl.pallas_call(
        functools.partial(_prenorm_router_kernel, sh_idx=sh_idx, sc_idx=sc_idx),
        grid=(rows // ROW_TILE,),
        in_specs=[pl.BlockSpec((ROW_TILE, D_MODEL), lambda i: (i, 0)),
                  pl.BlockSpec((1, D_MODEL), lambda i: (0, 0)),
                  pl.BlockSpec((1, 6, D_MODEL), lambda i: (_mod_row(i), 0, 0)),
                  pl.BlockSpec((D_MODEL, ROUTER_COLS), lambda i: (0, 0)),
                  pl.BlockSpec((ROW_TILE, ROW_TILE), lambda i: (0, 0))],
        out_specs=[pl.BlockSpec((ROW_TILE, D_MODEL), lambda i: (i, 0)),
                   pl.BlockSpec((ROW_TILE, ROUTER_COLS), lambda i: (i, 0)),
                   pl.BlockSpec((8, ROUTER_COLS), lambda i: (0, 0))],
        out_shape=[jax.ShapeDtypeStruct((rows, D_MODEL), F32),
                   jax.ShapeDtypeStruct((rows, ROUTER_COLS), F32),
                   jax.ShapeDtypeStruct((8, ROUTER_COLS), F32)],
        scratch_shapes=[pltpu.VMEM((8, ROUTER_COLS), F32)],
        compiler_params=_cparams("arbitrary"),
        name="prenorm_router",
    )(x, w.reshape(1, D_MODEL), modtab, w_router, tri)


def _mm_kernel(a_ref, w_ref, o_ref):
    o_ref[...] = jnp.dot(a_ref[...], w_ref[...], preferred_element_type=F32).astype(o_ref.dtype)


def _mm(a, w, tm, tn, out_dtype=F32):
    m, k = a.shape
    n = w.shape[1]
    return pl.pallas_call(
        _mm_kernel,
        grid=(n // tn, m // tm),
        in_specs=[pl.BlockSpec((tm, k), lambda j, i: (i, 0)),
                  pl.BlockSpec((k, tn), lambda j, i: (0, j))],
        out_specs=pl.BlockSpec((tm, tn), lambda j, i: (i, j)),
        out_shape=jax.ShapeDtypeStruct((m, n), out_dtype),
        compiler_params=_cparams("arbitrary", "arbitrary"),
        name="in_proj",
    )(a, w)


def _rms(x, w=None):
    y = x * lax.rsqrt(jnp.mean(x * x, axis=-1, keepdims=True) + EPS)
    return y if w is None else y * w


def _mla_prep_kernel(za_ref, cs_ref, qw_ref, kvw_ref, wqa_ref, wqb_ref, wk_ref, wv_ref, q_ref, k_ref, v_ref):
    za = za_ref[...]
    cos = cs_ref[:, :LANES]
    sin = cs_ref[:, LANES:]
    qn = _rms(za[:, Z_CQ:Z_CQ + Q_RANK], qw_ref[...]).astype(BF16)
    kvn = _rms(za[:, Z_CKV:Z_CKV + KV_RANK], kvw_ref[...]).astype(BF16)
    qa = jnp.dot(qn, wqa_ref[...], preferred_element_type=F32)
    qb = jnp.dot(qn, wqb_ref[...], preferred_element_type=F32)
    kn = jnp.dot(kvn, wk_ref[...], preferred_element_type=F32)
    v = jnp.dot(kvn, wv_ref[...], preferred_element_type=F32)
    kr = (za[:, Z_KR:Z_KR + LANES] * cos + za[:, Z_KRR:Z_KRR + LANES] * sin).astype(BF16)
    for h in range(H_A):
        c0 = h * 2 * LANES
        q_ref[:, c0:c0 + LANES] = qa[:, c0:c0 + LANES].astype(BF16)
        q_ref[:, c0 + LANES:c0 + 2 * LANES] = (
            qa[:, c0 + LANES:c0 + 2 * LANES] * cos + qb[:, h * LANES:(h + 1) * LANES] * sin).astype(BF16)
        k_ref[:, c0:c0 + LANES] = kn[:, h * LANES:(h + 1) * LANES].astype(BF16)
        k_ref[:, c0 + LANES:c0 + 2 * LANES] = kr
    v_ref[...] = v.astype(BF16)


def _mla_prep(z, cs, qw, kvw, wqa, wqb, wk, wv):
    za_w = Z_MQ
    const = lambda i: (0, 0)
    return pl.pallas_call(
        _mla_prep_kernel,
        grid=(T_ALL // ROW_TILE,),
        in_specs=[pl.BlockSpec((ROW_TILE, za_w), lambda i: (i, 0)),
                  pl.BlockSpec((ROW_TILE, 2 * LANES),
                               lambda i: (jnp.where(i < N_X_TILES, i % TILES_PER_BATCH, TILES_PER_BATCH), 0)),
                  pl.BlockSpec((1, Q_RANK), const),
                  pl.BlockSpec((1, KV_RANK), const),
                  pl.BlockSpec(wqa.shape, const),
                  pl.BlockSpec(wqb.shape, const),
                  pl.BlockSpec(wk.shape, const),
                  pl.BlockSpec(wv.shape, const)],
        out_specs=[pl.BlockSpec((ROW_TILE, H_A * 2 * LANES), lambda i: (i, 0)),
                   pl.BlockSpec((ROW_TILE, H_A * 2 * LANES), lambda i: (i, 0)),
                   pl.BlockSpec((ROW_TILE, D_MLA), lambda i: (i, 0))],
        out_shape=[jax.ShapeDtypeStruct((T_ALL, H_A * 2 * LANES), BF16),
                   jax.ShapeDtypeStruct((T_ALL, H_A * 2 * LANES), BF16),
                   jax.ShapeDtypeStruct((T_ALL, D_MLA), BF16)],
        compiler_params=_cparams("arbitrary"),
        name="mla_prep",
    )(z, cs, qw.reshape(1, Q_RANK), kvw.reshape(1, KV_RANK), wqa, wqb, wk, wv)


_NT = (((1,), (1,)), ((), ()))


def _attn_kernel(*refs, n_seg):
    q_ref = refs[0]
    k_refs = refs[1:1 + n_seg]
    v_refs = refs[1 + n_seg:1 + 2 * n_seg]
    w_ref = refs[1 + 2 * n_seg]
    o_ref = refs[2 + 2 * n_seg]
    acc_ref = refs[3 + 2 * n_seg]
    scale = (D_NOPE + D_ROPE) ** -0.5
    for h in range(H_A):
        q = q_ref[:, h * 2 * LANES:(h + 1) * 2 * LANES]
        ss = [lax.dot_general(q, k[:, h * 2 * LANES:(h + 1) * 2 * LANES], _NT, preferred_element_type=F32)
              for k in k_refs]
        m = functools.reduce(jnp.maximum, [jnp.max(s, axis=-1, keepdims=True) for s in ss])
        ps = [jnp.exp((s - m) * scale) for s in ss]
        l = functools.reduce(lambda a, b: a + b, [jnp.sum(p, axis=-1, keepdims=True) for p in ps])
        o = functools.reduce(lambda a, b: a + b, [
            jnp.dot(p.astype(BF16), v[:, h * D_V:(h + 1) * D_V], preferred_element_type=F32)
            for p, v in zip(ps, v_refs)])
        acc_ref[:, h * D_V:(h + 1) * D_V] = o / l
    o_ref[...] = (_rms(acc_ref[...]) * w_ref[...]).astype(o_ref.dtype)


ATTN_TQ = 256


def _attention_latent(q, k, v, onw):
    qt = SEQ // ATTN_TQ
    kw = H_A * 2 * LANES
    return pl.pallas_call(
        functools.partial(_attn_kernel, n_seg=2),
        grid=(BATCH, qt),
        in_specs=[pl.BlockSpec((ATTN_TQ, kw), lambda b, i: (b * qt + i, 0)),
                  pl.BlockSpec((SEQ, kw), lambda b, i: (b, 0)),
                  pl.BlockSpec((CTX_LEN, kw), lambda b, i: (T_X // CTX_LEN + b, 0)),
                  pl.BlockSpec((SEQ, D_MLA), lambda b, i: (b, 0)),
                  pl.BlockSpec((CTX_LEN, D_MLA), lambda b, i: (T_X // CTX_LEN + b, 0)),
                  pl.BlockSpec((1, D_MLA), lambda b, i: (0, 0))],
        out_specs=pl.BlockSpec((ATTN_TQ, D_MLA), lambda b, i: (b * qt + i, 0)),
        out_shape=jax.ShapeDtypeStruct((T_X, D_MLA), BF16),
        scratch_shapes=[pltpu.VMEM((ATTN_TQ, D_MLA), F32)],
        compiler_params=_cparams("arbitrary", "arbitrary"),
        name="attn_latent",
    )(q, k, k, v, v, onw)


def _attention_ctx(q, k, v, onw):
    kw = H_A * 2 * LANES
    blk = lambda b: (T_X // CTX_LEN + b, 0)
    return pl.pallas_call(
        functools.partial(_attn_kernel, n_seg=1),
        grid=(BATCH,),
        in_specs=[pl.BlockSpec((CTX_LEN, kw), blk),
                  pl.BlockSpec((CTX_LEN, kw), blk),
                  pl.BlockSpec((CTX_LEN, D_MLA), blk),
                  pl.BlockSpec((1, D_MLA), lambda b: (0, 0))],
        out_specs=pl.BlockSpec((CTX_LEN, D_MLA), lambda b: (b, 0)),
        out_shape=jax.ShapeDtypeStruct((T_C, D_MLA), BF16),
        scratch_shapes=[pltpu.VMEM((CTX_LEN, D_MLA), F32)],
        compiler_params=_cparams("arbitrary"),
        name="attn_ctx",
    )(q, k, v, onw)


def _outproj_kernel(a0, a1, a2, a3, w0, w1, w2, w3, x_ref, mod_ref, o_ref, *, g_idx):
    acc = jnp.dot(a0[...], w0[...], preferred_element_type=F32)
    acc += jnp.dot(a1[...], w1[...], preferred_element_type=F32)
    acc += jnp.dot(a2[...], w2[...], preferred_element_type=F32)
    acc += jnp.dot(a3[...], w3[...], preferred_element_type=F32)
    g = mod_ref[0][g_idx:g_idx + 1, :]
    o_ref[...] = x_ref[...] + g * acc


def _outproj(ys, w, x, modtab, g_idx):
    kq = D_MODEL // 4
    a_specs = [pl.BlockSpec((ROW_TILE, kq), lambda i: (i, 0)) for _ in range(4)]
    w_specs = [pl.BlockSpec((kq, D_MODEL), functools.partial(lambda i, r: (r, 0), r=r)) for r in range(4)]
    return pl.pallas_call(
        functools.partial(_outproj_kernel, g_idx=g_idx),
        grid=(T_ALL // ROW_TILE,),
        in_specs=a_specs + w_specs + [
            pl.BlockSpec((ROW_TILE, D_MODEL), lambda i: (i, 0)),
            pl.BlockSpec((1, 6, D_MODEL), lambda i: (_mod_row(i), 0, 0))],
        out_specs=pl.BlockSpec((ROW_TILE, D_MODEL), lambda i: (i, 0)),
        out_shape=jax.ShapeDtypeStruct((T_ALL, D_MODEL), F32),
        compiler_params=_cparams("arbitrary"),
        name="out_proj",
    )(*ys, w, w, w, w, x, modtab)


def _moe_kernel(be_ref, first_ref, valid_ref, x_ref, wg_ref, wu_ref, wd_ref, o_ref, wg_s, wu_s, wd_s):
    i = pl.program_id(0)

    @pl.when(first_ref[i] == 1)
    def _():
        wg_s[...] = wg_ref[0].astype(BF16)
        wu_s[...] = wu_ref[0].astype(BF16)
        wd_s[...] = wd_ref[0].astype(BF16)

    @pl.when(valid_ref[i] == 1)
    def _():
        x = x_ref[...].astype(BF16)
        g = jnp.dot(x, wg_s[...], preferred_element_type=F32)
        u = jnp.dot(x, wu_s[...], preferred_element_type=F32)
        h = (g * jax.nn.sigmoid(g) * u).astype(BF16)
        o_ref[...] = jnp.dot(h, wd_s[...], preferred_element_type=F32)

    @pl.when(valid_ref[i] == 0)
    def _():
        o_ref[...] = jnp.zeros_like(o_ref)


def _moe_experts(xb, block_e, first, valid, wg, wu, wd):
    nb = xb.shape[0] // MOE_ROWS
    grid_spec = pltpu.PrefetchScalarGridSpec(
        num_scalar_prefetch=3,
        grid=(nb,),
        in_specs=[pl.BlockSpec((MOE_ROWS, D_MODEL), lambda i, be, fi, va: (i, 0)),
                  pl.BlockSpec((1, D_MODEL, D_EXPERT), lambda i, be, fi, va: (be[i], 0, 0)),
                  pl.BlockSpec((1, D_MODEL, D_EXPERT), lambda i, be, fi, va: (be[i], 0, 0)),
                  pl.BlockSpec((1, D_EXPERT, D_MODEL), lambda i, be, fi, va: (be[i], 0, 0))],
        out_specs=pl.BlockSpec((MOE_ROWS, D_MODEL), lambda i, be, fi, va: (i, 0)),
        scratch_shapes=[pltpu.VMEM((D_MODEL, D_EXPERT), BF16),
                        pltpu.VMEM((D_MODEL, D_EXPERT), BF16),
                        pltpu.VMEM((D_EXPERT, D_MODEL), BF16)])
    return pl.pallas_call(
        _moe_kernel,
        grid_spec=grid_spec,
        out_shape=jax.ShapeDtypeStruct((nb * MOE_ROWS, D_MODEL), F32),
        compiler_params=_cparams("arbitrary"),
        name="moe_experts",
    )(block_e, first, valid, xb, wg, wu, wd)


MOE_TILE = 256


def _row_copy(src, src_row, dst, dst_row, sem):
    return pltpu.make_async_copy(src.at[pl.ds(src_row, 1)], dst.at[pl.ds(dst_row, 1)], sem)


def _dispatch_kernel(dest_ref, h_ref, xb_ref, zero_s, sem, zsem):
    @pl.when(pl.program_id(0) == 0)
    def _():
        zero_s[...] = jnp.zeros_like(zero_s)
        n_blocks = xb_ref.shape[0] // MOE_ROWS
        fill = lambda j: pltpu.make_async_copy(zero_s, xb_ref.at[pl.ds(j * MOE_ROWS, MOE_ROWS)], zsem)
        for j in range(n_blocks):
            fill(j).start()
        for j in range(n_blocks):
            fill(j).wait()

    def issue(r, carry):
        for k in range(TOP_K):
            _row_copy(h_ref, r, xb_ref, dest_ref[0, 0, TOP_K * r + k], sem).start()
        return carry

    lax.fori_loop(0, MOE_TILE, issue, 0, unroll=8)

    def drain(r, carry):
        for k in range(TOP_K):
            _row_copy(h_ref, 0, xb_ref, 0, sem).wait()
        return carry

    lax.fori_loop(0, MOE_TILE, drain, 0, unroll=8)


def _dispatch(h, dest3, n_rows):
    n_tok = h.shape[0]
    return pl.pallas_call(
        _dispatch_kernel,
        grid=(n_tok // MOE_TILE,),
        in_specs=[pl.BlockSpec((1, 1, TOP_K * MOE_TILE), lambda i: (i, 0, 0), memory_space=pltpu.SMEM),
                  pl.BlockSpec((MOE_TILE, D_MODEL), lambda i: (i, 0))],
        out_specs=pl.BlockSpec(memory_space=pl.ANY),
        out_shape=jax.ShapeDtypeStruct((n_rows, D_MODEL), F32),
        scratch_shapes=[pltpu.VMEM((MOE_ROWS, D_MODEL), F32), pltpu.SemaphoreType.DMA(()),
                        pltpu.SemaphoreType.DMA(())],
        compiler_params=_cparams("arbitrary"),
        name="moe_dispatch",
    )(dest3, h)


def _combine_kernel(dest_ref, yb_ref, x_ref, route_ref, mod_ref, o_ref, buf, sem):
    def issue(r, carry):
        for k in range(TOP_K):
            _row_copy(yb_ref, dest_ref[0, 0, TOP_K * r + k], buf.at[k], r, sem).start()
        return carry

    lax.fori_loop(0, MOE_TILE, issue, 0, unroll=8)

    def drain(r, carry):
        for k in range(TOP_K):
            _row_copy(yb_ref, 0, buf.at[k], 0, sem).wait()
        return carry

    lax.fori_loop(0, MOE_TILE, drain, 0, unroll=8)
    route = route_ref[...]
    f = buf[0] * route[:, R_W0:R_W0 + 1] + buf[1] * route[:, R_W1:R_W1 + 1]
    o_ref[...] = x_ref[...] + mod_ref[0][5:6, :] * f


def _combine(yb, dest3, x, route, modtab):
    n_tok = x.shape[0]
    tiles_per_batch = SEQ // MOE_TILE
    return pl.pallas_call(
        _combine_kernel,
        grid=(n_tok // MOE_TILE,),
        in_specs=[pl.BlockSpec((1, 1, TOP_K * MOE_TILE), lambda i: (i, 0, 0), memory_space=pltpu.SMEM),
                  pl.BlockSpec(memory_space=pl.ANY),
                  pl.BlockSpec((MOE_TILE, D_MODEL), lambda i: (i, 0)),
                  pl.BlockSpec((MOE_TILE, ROUTER_COLS), lambda i: (i, 0)),
                  pl.BlockSpec((1, 6, D_MODEL), lambda i: (jnp.minimum(i // tiles_per_batch, BATCH), 0, 0))],
        out_specs=pl.BlockSpec((MOE_TILE, D_MODEL), lambda i: (i, 0)),
        out_shape=jax.ShapeDtypeStruct((n_tok, D_MODEL), F32),
        scratch_shapes=[pltpu.VMEM((TOP_K, MOE_TILE, D_MODEL), F32), pltpu.SemaphoreType.DMA(())],
        compiler_params=_cparams("arbitrary"),
        name="moe_combine",
    )(dest3, yb, x, route, modtab)


def _moe(x, h, route, cnt, modtab, wg, wu, wd):
    t = h.shape[0]
    experts = route[:, R_E0:R_E1 + 1].astype(jnp.int32)
    rank = route[:, R_K0:R_K1 + 1].astype(jnp.int32)
    counts = cnt[0, :N_EXPERTS].astype(jnp.int32)
    pcounts = (counts + MOE_ROWS - 1) // MOE_ROWS * MOE_ROWS
    pends = jnp.cumsum(pcounts)
    pstarts = pends - pcounts
    dest = pstarts[experts] + rank
    nb = (t * TOP_K + N_EXPERTS * (MOE_ROWS - 1) + MOE_ROWS - 1) // MOE_ROWS
    bstart = jnp.arange(nb, dtype=jnp.int32) * MOE_ROWS
    block_e = jnp.clip(jnp.searchsorted(pends, bstart, side='right'), 0, N_EXPERTS - 1).astype(jnp.int32)
    valid = (bstart < pends[-1]).astype(jnp.int32)
    first = jnp.concatenate([jnp.ones((1,), jnp.int32), (block_e[1:] != block_e[:-1]).astype(jnp.int32)])
    dest3 = dest.reshape(t // MOE_TILE, 1, TOP_K * MOE_TILE)
    xb = _dispatch(h, dest3, nb * MOE_ROWS)
    yb = _moe_experts(xb, block_e, first, valid, wg, wu, wd)
    return _combine(yb, dest3, x, route, modtab)


def _final_norm_kernel(x_ref, w_ref, o_ref):
    o_ref[...] = _rms(x_ref[...], w_ref[...])


def _final_norm(x, w):
    rows = x.shape[0]
    return pl.pallas_call(
        _final_norm_kernel,
        grid=(rows // ROW_TILE,),
        in_specs=[pl.BlockSpec((ROW_TILE, D_MODEL), lambda i: (i, 0)),
                  pl.BlockSpec((1, D_MODEL), lambda i: (0, 0))],
        out_specs=pl.BlockSpec((ROW_TILE, D_MODEL), lambda i: (i, 0)),
        out_shape=jax.ShapeDtypeStruct((rows, D_MODEL), F32),
        compiler_params=_cparams("arbitrary"),
        name="final_norm",
    )(x, w.reshape(1, D_MODEL))


def _split_streams(a):
    return a[:T_X].reshape(BATCH, SEQ, -1), a[T_X:].reshape(BATCH, CTX_LEN, -1)


def _merge_streams(ax, ac):
    return jnp.concatenate([ax.reshape(T_X, -1), ac.reshape(T_C, -1)], axis=0)


def _flip(t):
    return jnp.flip(t, axis=1)


def _ssd_chunked(x, dt, a_h, bm, cm, h0):
    b, l, h, p = x.shape
    nc = l // SSD_CHUNK
    rep = h // bm.shape[2]
    bh = jnp.repeat(bm, rep, axis=2).reshape(b, nc, SSD_CHUNK, h, -1)
    ch = jnp.repeat(cm, rep, axis=2).reshape(b, nc, SSD_CHUNK, h, -1)
    xc = x.reshape(b, nc, SSD_CHUNK, h, p)
    dtc = dt.reshape(b, nc, SSD_CHUNK, h)
    acum = jnp.cumsum(dtc * a_h, axis=2)
    a_tot = acum[:, :, -1]
    w_end = jnp.exp(a_tot[:, :, None] - acum) * dtc
    s_loc = jnp.einsum('bcqh,bcqhn,bcqhp->bchpn', w_end, bh, xc)

    def step(hs, inp):
        s, dec = inp
        return jnp.exp(dec)[:, :, None, None] * hs + s, hs

    h_fin, h_start = lax.scan(step, h0, (s_loc.swapaxes(0, 1), a_tot.swapaxes(0, 1)))
    h_start = h_start.swapaxes(0, 1)
    causal = jnp.tril(jnp.ones((SSD_CHUNK, SSD_CHUNK), dtype=bool))
    seg = acum[:, :, :, None, :] - acum[:, :, None, :, :]
    decay = jnp.exp(jnp.where(causal[None, None, :, :, None], seg, -jnp.inf))
    scores = jnp.einsum('bcihn,bcjhn->bcijh', ch, bh) * decay
    y = (jnp.einsum('bcijh,bcjh,bcjhp->bcihp', scores, dtc, xc)
         + jnp.exp(acum)[..., None] * jnp.einsum('bcihn,bchpn->bcihp', ch, h_start))
    return y.reshape(b, l, h, p), h_fin


def _ssd_mixer(z, conv_w, conv_b, a_log, dt_bias, d_skip):
    a_neg = -jnp.exp(a_log.astype(F32))
    gate_x, gate_c = _split_streams(z[:, Z_SZ:Z_SZ + D_SSD])
    xbc_x, xbc_c = _split_streams(z[:, Z_XBC:Z_XBC + CONV_CH])
    dt_x, dt_c = _split_streams(z[:, Z_DT:Z_DT + 2 * H_B])

    def prep(xbc, dt):
        bt, lt, c = xbc.shape
        k = conv_w.shape[0]
        y = lax.conv_general_dilated(xbc, conv_w[:, None, :], window_strides=(1,),
                                     padding=[((k - 1) // 2, (k - 1) // 2)],
                                     dimension_numbers=('NWC', 'WIO', 'NWC'), feature_group_count=c) + conv_b
        y = jax.nn.silu(y)
        xs = y[..., :D_SSD].reshape(bt, lt, H_B, P_B)
        bm = y[..., D_SSD:D_SSD + G_B * N_B].reshape(bt, lt, G_B, N_B)
        cm = y[..., D_SSD + G_B * N_B:].reshape(bt, lt, G_B, N_B)
        dts = jax.nn.softplus(dt.reshape(bt, lt, 2, H_B) + dt_bias)
        return xs, bm, cm, dts[:, :, 0], dts[:, :, 1]

    def run(xbc, dt, gate, h0f, h0b):
        xs, bm, cm, dtf, dtb = prep(xbc, dt)
        yf, hf = _ssd_chunked(xs, dtf, a_neg[0], bm, cm, h0f)
        yb, hb = _ssd_chunked(_flip(xs), _flip(dtb), a_neg[1], _flip(bm), _flip(cm), h0b)
        y = yf + _flip(yb) + d_skip[:, None] * xs
        y = y.reshape(xbc.shape[0], xbc.shape[1], D_SSD) * jax.nn.silu(gate)
        return _rms(y), hf, hb

    h0 = jnp.zeros((BATCH, H_B, P_B, N_B), F32)
    y_c, hf_c, hb_c = run(xbc_c, dt_c, gate_c, h0, h0)
    y_x, _, _ = run(xbc_x, dt_x, gate_x, hf_c, hb_c)
    return _merge_streams(y_x, y_c)


def _mlstm_chunked(q, k, v, log_i, log_f, state0):
    b, l, h, _ = q.shape
    nc = l // MLSTM_CHUNK

    def chunks(t):
        return t.reshape((b, nc, MLSTM_CHUNK) + t.shape[2:]).swapaxes(0, 1)

    causal = jnp.tril(jnp.ones((MLSTM_CHUNK, MLSTM_CHUNK), dtype=bool))[None, :, :, None]

    def step(carry, inp):
        c_st, n_st, m_st = carry
        qc, kc, vc, li, lf = inp
        bcum = jnp.cumsum(lf, axis=1)
        g = bcum[:, -1]
        w_log = g[:, None] - bcum + li
        m_new = jnp.maximum(g + m_st, jnp.max(w_log, axis=1))
        wj = jnp.exp(w_log - m_new[:, None])
        dec = jnp.exp(g + m_st - m_new)
        c_new = dec[:, :, None, None] * c_st + jnp.einsum('bjh,bjhv,bjhk->bhvk', wj, vc, kc)
        n_new = dec[:, :, None] * n_st + jnp.einsum('bjh,bjhk->bhk', wj, kc)
        dmat = jnp.where(causal, bcum[:, :, None] - bcum[:, None] + li[:, None], -jnp.inf)
        inter_log = bcum + m_st[:, None]
        m_row = jnp.maximum(inter_log, jnp.max(dmat, axis=2))
        s = jnp.einsum('bihk,bjhk->bijh', qc, kc) * jnp.exp(dmat - m_row[:, :, None])
        w_inter = jnp.exp(inter_log - m_row)
        num = jnp.einsum('bijh,bjhv->bihv', s, vc) + w_inter[..., None] * jnp.einsum('bhvk,bihk->bihv', c_st, qc)
        den = jnp.sum(s, axis=2) + w_inter * jnp.einsum('bhk,bihk->bih', n_st, qc)
        hh = num / jnp.maximum(jnp.abs(den), jnp.exp(-m_row))[..., None]
        return (c_new, n_new, m_new), hh

    state, hs = lax.scan(step, state0, (chunks(q), chunks(k), chunks(v), chunks(log_i), chunks(log_f)))
    return hs.swapaxes(0, 1).reshape(b, l, h, v.shape[-1]), state


def _mlstm_mixer(z, gate_b):
    q_x, q_c = _split_streams(z[:, Z_MQ:Z_MQ + H_C * DQK_C])
    k_x, k_c = _split_streams(z[:, Z_MK:Z_MK + H_C * DQK_C])
    v_x, v_c = _split_streams(z[:, Z_MV:Z_MV + D_MLSTM])
    o_x, o_c = _split_streams(z[:, Z_MO:Z_MO + D_MLSTM])
    g_x, g_c = _split_streams(z[:, Z_MG:Z_MG + 4 * H_C])

    def run(q, k, v, o, g, s_f, s_b):
        bt, lt, _ = q.shape
        q = q.reshape(bt, lt, H_C, DQK_C) * DQK_C ** -0.5
        k = k.reshape(bt, lt, H_C, DQK_C)
        v = v.reshape(bt, lt, H_C, DV_C)
        g = g.reshape(bt, lt, 2, 2, H_C) + gate_b
        li = g[:, :, :, 0]
        lf = jax.nn.log_sigmoid(g[:, :, :, 1])
        h_f, s_f_new = _mlstm_chunked(q, k, v, li[:, :, 0], lf[:, :, 0], s_f)
        h_b, s_b_new = _mlstm_chunked(_flip(q), _flip(k), _flip(v), _flip(li[:, :, 1]), _flip(lf[:, :, 1]), s_b)
        hh = _rms(h_f + _flip(h_b))
        y = hh.reshape(bt, lt, D_MLSTM) * jax.nn.sigmoid(o)
        return y, s_f_new, s_b_new

    s0 = (jnp.zeros((BATCH, H_C, DV_C, DQK_C), F32), jnp.zeros((BATCH, H_C, DQK_C), F32),
          jnp.full((BATCH, H_C), NEG_STATE, F32))
    y_c, sf_c, sb_c = run(q_c, k_c, v_c, o_c, g_c, s0, s0)
    y_x, _, _ = run(q_x, k_x, v_x, o_x, g_x, sf_c, sb_c)
    return _merge_streams(y_x, y_c)


def _complex_scan(ar, ai, ur, ui, x0r, x0i):
    ur = ur.at[:, 0].add(ar * x0r - ai * x0i)
    ui = ui.at[:, 0].add(ar * x0i + ai * x0r)
    a_r = jnp.broadcast_to(ar, ur.shape)
    a_i = jnp.broadcast_to(ai, ur.shape)

    def combine(e1, e2):
        a1r, a1i, b1r, b1i = e1
        a2r, a2i, b2r, b2i = e2
        return (a2r * a1r - a2i * a1i, a2r * a1i + a2i * a1r,
                a2r * b1r - a2i * b1i + b2r, a2r * b1i + a2i * b1r + b2i)

    _, _, xr, xi = lax.associative_scan(combine, (a_r, a_i, ur, ui), axis=1)
    return xr, xi


def _s5_mixer(z, lam_re, lam_im, log_dt, b_re, b_im, c_re, c_im, d_skip, glu_w, glu_b):
    dt = jnp.exp(log_dt)[..., None]
    mag = jnp.exp(lam_re * dt)
    ar = mag * jnp.cos(lam_im * dt)
    ai = mag * jnp.sin(lam_im * dt)
    den = lam_re * lam_re + lam_im * lam_im
    cr_ = ((ar - 1.0) * lam_re + ai * lam_im) / den
    ci_ = (ai * lam_re - (ar - 1.0) * lam_im) / den
    bbr = cr_[..., None] * b_re - ci_[..., None] * b_im
    bbi = cr_[..., None] * b_im + ci_[..., None] * b_re
    u_x, u_c = _split_streams(z[:, Z_D:Z_D + D_S5])

    def drive(u, d):
        ug = u.reshape(u.shape[0], u.shape[1], G_S5, S5_GROUP)
        return jnp.einsum('blgc,gpc->blgp', ug, bbr[d]), jnp.einsum('blgc,gpc->blgp', ug, bbi[d])

    def readout(xr, xi):
        y = jnp.einsum('gcp,blgp->blgc', c_re, xr) - jnp.einsum('gcp,blgp->blgc', c_im, xi)
        return y.reshape(y.shape[0], y.shape[1], D_S5)

    def run(u, s0f, s0b):
        ufr, ufi = drive(u, 0)
        ubr, ubi = drive(_flip(u), 1)
        xfr, xfi = _complex_scan(ar[0], ai[0], ufr, ufi, s0f[0], s0f[1])
        xbr, xbi = _complex_scan(ar[1], ai[1], ubr, ubi, s0b[0], s0b[1])
        sf = (xfr[:, -1], xfi[:, -1])
        sb = (xbr[:, -1], xbi[:, -1])
        y = readout(xfr, xfi) + _flip(readout(xbr, xbi)) + d_skip * u
        y = jax.nn.gelu(y)
        y = y * jax.nn.sigmoid(y @ glu_w + glu_b)
        return _rms(y), sf, sb

    z0 = jnp.zeros((BATCH, G_S5, P_S5), F32)
    y_c, sf_c, sb_c = run(u_c, (z0, z0), (z0, z0))
    y_x, _, _ = run(u_x, sf_c, sb_c)
    return _merge_streams(y_x, y_c)


SEQ_TILE = 256
X_SEQ_TILES = SEQ // SEQ_TILE
HALO = 8


def _split3(x):
    hi = x.astype(BF16)
    r1 = x - hi.astype(F32)
    mid = r1.astype(BF16)
    lo = (r1 - mid.astype(F32)).astype(BF16)
    return hi, mid, lo


def _dot3_left(sel, x):
    hi, mid, lo = _split3(x)
    return (jnp.dot(sel, hi, preferred_element_type=F32) + jnp.dot(sel, mid, preferred_element_type=F32)
            + jnp.dot(sel, lo, preferred_element_type=F32))


def _dot3_right(x, sel):
    hi, mid, lo = _split3(x)
    return (jnp.dot(hi, sel, preferred_element_type=F32) + jnp.dot(mid, sel, preferred_element_type=F32)
            + jnp.dot(lo, sel, preferred_element_type=F32))


def _ssd_prep_kernel(cur_ref, prev_ref, next_ref, dt_ref, cw_ref, cb_ref, dtb_ref, xo_ref, dto_ref, ext_s):
    i = pl.program_id(0)
    is_ctx = i >= T_X // SEQ_TILE
    first = jnp.logical_or(is_ctx, i % X_SEQ_TILES == 0)
    last = jnp.logical_or(is_ctx, i % X_SEQ_TILES == X_SEQ_TILES - 1)
    ext_s[0:HALO, :] = jnp.where(first, 0.0, prev_ref[...])
    ext_s[HALO:HALO + SEQ_TILE, :] = cur_ref[...]
    ext_s[HALO + SEQ_TILE:, :] = jnp.where(last, 0.0, next_ref[...])
    half = (SSD_CONV - 1) // 2
    acc = cb_ref[...] + cw_ref[0:1, :] * ext_s[HALO - half:HALO - half + SEQ_TILE, :]
    for k in range(1, SSD_CONV):
        acc = acc + cw_ref[k:k + 1, :] * ext_s[HALO - half + k:HALO - half + k + SEQ_TILE, :]
    xo_ref[...] = acc * jax.nn.sigmoid(acc)
    lane = lax.broadcasted_iota(jnp.int32, (SEQ_TILE, LANES), 1)
    dto_ref[...] = jnp.where(lane < 2 * H_B, jax.nn.softplus(dt_ref[...] + dtb_ref[...]), 0.0)


def _ssd_prep(z, conv_w, conv_b, dt_bias):
    n_tiles = T_ALL // SEQ_TILE
    per = SEQ_TILE // HALO
    cwp = jnp.concatenate([conv_w, jnp.zeros((8 - SSD_CONV, CONV_CH), F32)], axis=0)
    dtb = jnp.concatenate([dt_bias.reshape(1, 2 * H_B), jnp.zeros((1, LANES - 2 * H_B), F32)], axis=1)
    xc = Z_XBC // CONV_CH
    return pl.pallas_call(
        _ssd_prep_kernel,
        grid=(n_tiles,),
        in_specs=[pl.BlockSpec((SEQ_TILE, CONV_CH), lambda i: (i, xc)),
                  pl.BlockSpec((HALO, CONV_CH), lambda i: (jnp.maximum(i * per - 1, 0), xc)),
                  pl.BlockSpec((HALO, CONV_CH), lambda i: (jnp.minimum((i + 1) * per, T_ALL // HALO - 1), xc)),
                  pl.BlockSpec((SEQ_TILE, LANES), lambda i: (i, Z_DT // LANES)),
                  pl.BlockSpec((8, CONV_CH), lambda i: (0, 0)),
                  pl.BlockSpec((1, CONV_CH), lambda i: (0, 0)),
                  pl.BlockSpec((1, LANES), lambda i: (0, 0))],
        out_specs=[pl.BlockSpec((SEQ_TILE, CONV_CH), lambda i: (i, 0)),
                   pl.BlockSpec((SEQ_TILE, LANES), lambda i: (i, 0))],
        out_shape=[jax.ShapeDtypeStruct((T_ALL, CONV_CH), F32),
                   jax.ShapeDtypeStruct((T_ALL, LANES), F32)],
        scratch_shapes=[pltpu.VMEM((SEQ_TILE + 2 * HALO, CONV_CH), F32)],
        compiler_params=_cparams("arbitrary"),
        name="ssd_prep",
    )(z, z, z, z, cwp, conv_b.reshape(1, CONV_CH), dtb)


_TN = (((0,), (0,)), ((), ()))
NEG_BIG = -1e30


def _ssd_one_direction(xbc, dtp, arow, tri, expand, h_ref, y_ref, d, rev):
    q = SSD_CHUNK
    a = dtp * arow
    acum = _dot3_left(tri, a)
    acum_t = acum.T
    dt_t = dtp.T
    edge = 0 if rev else q - 1
    atot = acum[edge:edge + 1, :]
    pieces = jnp.concatenate([jnp.exp(atot - acum) * dtp, jnp.exp(acum),
                              jnp.broadcast_to(jnp.exp(atot), (8, LANES))], axis=0)
    ex = _dot3_right(pieces, expand)
    wend_x = ex[0:q]
    eacum_x = ex[q:2 * q]
    dec_x = ex[2 * q:2 * q + 1]
    xs = xbc[:, 0:D_SSD]
    xw = (xs * wend_x).astype(BF16)
    xs_b = xs.astype(BF16)
    h_old = h_ref[...]
    h_b = h_old.astype(BF16)
    ri = lax.broadcasted_iota(jnp.int32, (q, q), 0)
    ci = lax.broadcasted_iota(jnp.int32, (q, q), 1)
    mask = (ci >= ri) if rev else (ci <= ri)
    lo_half = lax.broadcasted_iota(jnp.int32, (q, LANES), 1) < P_B
    hpg = H_B // G_B
    gw = hpg * P_B
    for g in range(G_B):
        bg = xbc[:, D_SSD + g * N_B:D_SSD + (g + 1) * N_B].astype(BF16)
        cg = xbc[:, D_SSD + G_B * N_B + g * N_B:D_SSD + G_B * N_B + (g + 1) * N_B].astype(BF16)
        cb = lax.dot_general(cg, bg, _NT, preferred_element_type=F32)
        inter = jnp.dot(cg, h_b[:, g * gw:(g + 1) * gw], preferred_element_type=F32)
        for j in range(hpg // 2):
            ms = []
            for hh in range(2):
                hc = H_B * d + hpg * g + 2 * j + hh
                seg = acum[:, hc:hc + 1] - acum_t[hc:hc + 1, :]
                dec = jnp.exp(jnp.where(mask, seg, NEG_BIG))
                ms.append((cb * dec * dt_t[hc:hc + 1, :]).astype(BF16))
            c0 = g * gw + 2 * j * P_B
            xp = xs_b[:, c0:c0 + LANES]
            zero = jnp.zeros_like(xp)
            rhs = jnp.concatenate([jnp.where(lo_half, xp, zero), jnp.where(lo_half, zero, xp)], axis=0)
            y_intra = jnp.dot(jnp.concatenate(ms, axis=1), rhs, preferred_element_type=F32)
            y_ref[:, c0:c0 + LANES] = y_intra + eacum_x[:, c0:c0 + LANES] * inter[:, 2 * j * P_B:2 * j * P_B + LANES]
        upd = lax.dot_general(bg, xw[:, g * gw:(g + 1) * gw], _TN, preferred_element_type=F32)
        h_ref[:, g * gw:(g + 1) * gw] = dec_x[:, g * gw:(g + 1) * gw] * h_old[:, g * gw:(g + 1) * gw] + upd


def _ssd_scan_kernel(xf_ref, dtf_ref, xb_ref, dtb_ref, arow_ref, trif_ref, trib_ref, ef_ref, eb_ref,
                     yf_ref, yb_ref, hf_s, hb_s):
    @pl.when(pl.program_id(1) == 0)
    def _():
        hf_s[...] = jnp.zeros_like(hf_s)
        hb_s[...] = jnp.zeros_like(hb_s)

    _ssd_one_direction(xf_ref[...], dtf_ref[...], arow_ref[...], trif_ref[...], ef_ref[...], hf_s, yf_ref, 0, False)
    _ssd_one_direction(xb_ref[...], dtb_ref[...], arow_ref[...], trib_ref[...], eb_ref[...], hb_s, yb_ref, 1, True)


def _chunk_rows(b, s, rev, chunk):
    n_c = CTX_LEN // chunk
    n_x = SEQ // chunk
    ctx_blk = T_X // chunk + b * n_c + ((n_c - 1 - s) if rev else s)
    x_blk = b * n_x + ((n_x + n_c - 1 - s) if rev else (s - n_c))
    return jnp.where(s < n_c, ctx_blk, x_blk)


def _ssd_scan(xact, dtp, a_log):
    q = SSD_CHUNK
    steps = (SEQ + CTX_LEN) // q
    a_neg = -jnp.exp(a_log)
    arow = jnp.concatenate([a_neg.reshape(1, 2 * H_B), jnp.zeros((1, LANES - 2 * H_B), F32)], axis=1)
    r = jnp.arange(q)
    tri_f = (r[None, :] <= r[:, None]).astype(BF16)
    tri_b = (r[None, :] >= r[:, None]).astype(BF16)
    col_head = jnp.arange(D_SSD) // P_B
    lane = jnp.arange(LANES)
    exp_f = (lane[:, None] == col_head[None, :]).astype(BF16)
    exp_b = (lane[:, None] == col_head[None, :] + H_B).astype(BF16)
    fwd = lambda b, s: (_chunk_rows(b, s, False, q), 0)
    bwd = lambda b, s: (_chunk_rows(b, s, True, q), 0)
    const = lambda b, s: (0, 0)
    return pl.pallas_call(
        _ssd_scan_kernel,
        grid=(BATCH, steps),
        in_specs=[pl.BlockSpec((q, CONV_CH), fwd), pl.BlockSpec((q, LANES), fwd),
                  pl.BlockSpec((q, CONV_CH), bwd), pl.BlockSpec((q, LANES), bwd),
                  pl.BlockSpec((1, LANES), const),
                  pl.BlockSpec((q, q), const), pl.BlockSpec((q, q), const),
                  pl.BlockSpec((LANES, D_SSD), const), pl.BlockSpec((LANES, D_SSD), const)],
        out_specs=[pl.BlockSpec((q, D_SSD), fwd), pl.BlockSpec((q, D_SSD), bwd)],
        out_shape=[jax.ShapeDtypeStruct((T_ALL, D_SSD), F32), jax.ShapeDtypeStruct((T_ALL, D_SSD), F32)],
        scratch_shapes=[pltpu.VMEM((N_B, D_SSD), F32), pltpu.VMEM((N_B, D_SSD), F32)],
        compiler_params=_cparams("arbitrary", "arbitrary"),
        name="ssd_scan",
    )(xact, dtp, xact, dtp, arow, tri_f, tri_b, exp_f, exp_b)


def _ssd_final_kernel(yf_ref, yb_ref, xs_ref, gate_ref, d_ref, onw_ref, o_ref):
    gate = gate_ref[...]
    y = (yf_ref[...] + yb_ref[...] + d_ref[...] * xs_ref[...]) * (gate * jax.nn.sigmoid(gate))
    o_ref[...] = (_rms(y) * onw_ref[...]).astype(o_ref.dtype)


def _ssd_final(yf, yb, xact, z, d_skip, onw):
    row = lambda i: (i, 0)
    const = lambda i: (0, 0)
    return pl.pallas_call(
        _ssd_final_kernel,
        grid=(T_ALL // ROW_TILE,),
        in_specs=[pl.BlockSpec((ROW_TILE, D_SSD), row), pl.BlockSpec((ROW_TILE, D_SSD), row),
                  pl.BlockSpec((ROW_TILE, D_SSD), row),
                  pl.BlockSpec((ROW_TILE, D_SSD), lambda i: (i, Z_SZ // D_SSD)),
                  pl.BlockSpec((1, D_SSD), const), pl.BlockSpec((1, D_SSD), const)],
        out_specs=pl.BlockSpec((ROW_TILE, D_SSD), row),
        out_shape=jax.ShapeDtypeStruct((T_ALL, D_SSD), BF16),
        compiler_params=_cparams("arbitrary"),
        name="ssd_final",
    )(yf, yb, xact, z, jnp.repeat(d_skip, P_B).reshape(1, D_SSD), onw.reshape(1, D_SSD))


def _ssd_mixer_pallas(z, p, onw):
    xact, dtp = _ssd_prep(z, p["ssd_conv_w"], p["ssd_conv_b"], p["ssd_dt_bias"])
    yf, yb = _ssd_scan(xact, dtp, p["ssd_a_log"])
    return _ssd_final(yf, yb, xact, z, p["ssd_d"], onw)


GATE_LANE0 = 2 * H_B


def _mlstm_one_direction(q, k, v, gi, gf, bi, bf, tri, st_ref, m_ref, h_ref, d, rev):
    n = MLSTM_CHUNK
    li = gi + bi
    lf = jax.nn.log_sigmoid(gf + bf)
    b = _dot3_left(tri, lf)
    b_t = b.T
    li_t = li.T
    edge = 0 if rev else n - 1
    gtot = b[edge:edge + 1, :]
    m_old = m_ref[0:1, :]
    w_log = gtot - b + li
    m_new = jnp.maximum(gtot + m_old, jnp.max(w_log, axis=0, keepdims=True))
    wj = jnp.exp(w_log - m_new)
    dec = jnp.exp(gtot + m_old - m_new)
    inter_log = b + m_old
    m_ref[0:1, :] = m_new
    ri = lax.broadcasted_iota(jnp.int32, (n, n), 0)
    ci = lax.broadcasted_iota(jnp.int32, (n, n), 1)
    mask = (ci >= ri) if rev else (ci <= ri)
    lo_half = lax.broadcasted_iota(jnp.int32, (n, LANES), 1) < DQK_C
    row_lo = lax.broadcasted_iota(jnp.int32, (2 * DQK_C, 2 * DV_C), 0) < DQK_C
    ones = jnp.ones((n, DV_C), F32)
    for j in range(H_C // 2):
        qp = q[:, j * LANES:(j + 1) * LANES] * DQK_C ** -0.5
        kp = k[:, j * LANES:(j + 1) * LANES].astype(BF16)
        st_old = st_ref[j]
        st_b = st_old.astype(BF16)
        upds = []
        for hh in range(2):
            h = 2 * j + hh
            gl = GATE_LANE0 + H_C * d + h
            qm = jnp.where(lo_half if hh == 0 else jnp.logical_not(lo_half), qp, 0.0).astype(BF16)
            qk = lax.dot_general(qm, kp, _NT, preferred_element_type=F32)
            dmat = jnp.where(mask, b[:, gl:gl + 1] - b_t[gl:gl + 1, :] + li_t[gl:gl + 1, :], NEG_BIG)
            il = inter_log[:, gl:gl + 1]
            m_row = jnp.maximum(il, jnp.max(dmat, axis=1, keepdims=True))
            s = qk * jnp.exp(dmat - m_row)
            w_inter = jnp.exp(il - m_row)
            vh = v[:, h * DV_C:(h + 1) * DV_C]
            qs = jnp.dot(qm, st_b, preferred_element_type=F32)
            num = jnp.dot(s.astype(BF16), vh.astype(BF16), preferred_element_type=F32) + w_inter * qs[:, :DV_C]
            den = jnp.sum(s, axis=1, keepdims=True) + w_inter * qs[:, DV_C:]
            h_ref[:, h * DV_C:(h + 1) * DV_C] = num / jnp.maximum(jnp.abs(den), jnp.exp(-m_row))
            rhs = (wj[:, gl:gl + 1] * jnp.concatenate([vh, ones], axis=1)).astype(BF16)
            upds.append(lax.dot_general(kp, rhs, _TN, preferred_element_type=F32))
        ga = GATE_LANE0 + H_C * d + 2 * j
        decv = jnp.where(row_lo, dec[:, ga:ga + 1], dec[:, ga + 1:ga + 2])
        st_ref[j] = decv * st_old + jnp.where(row_lo, upds[0], upds[1])


def _mlstm_scan_kernel(qf, kf, vf, gif, gff, qb, kb, vb, gib, gfb, bi_ref, bf_ref, trif_ref, trib_ref,
                       hf_ref, hb_ref, stf_s, stb_s, mf_s, mb_s):
    @pl.when(pl.program_id(1) == 0)
    def _():
        stf_s[...] = jnp.zeros_like(stf_s)
        stb_s[...] = jnp.zeros_like(stb_s)
        mf_s[...] = jnp.full_like(mf_s, NEG_STATE)
        mb_s[...] = jnp.full_like(mb_s, NEG_STATE)

    _mlstm_one_direction(qf[...], kf[...], vf[...], gif[...], gff[...], bi_ref[...], bf_ref[...], trif_ref[...],
                         stf_s, mf_s, hf_ref, 0, False)
    _mlstm_one_direction(qb[...], kb[...], vb[...], gib[...], gfb[...], bi_ref[...], bf_ref[...], trib_ref[...],
                         stb_s, mb_s, hb_ref, 1, True)


def _mlstm_scan(z, gate_b):
    n = MLSTM_CHUNK
    steps = (SEQ + CTX_LEN) // n
    qkw = H_C * DQK_C
    pad = lambda t: jnp.concatenate([jnp.zeros((1, GATE_LANE0), F32), t.reshape(1, 2 * H_C),
                                     jnp.zeros((1, LANES - GATE_LANE0 - 2 * H_C), F32)], axis=1)
    bi = pad(gate_b[:, 0, :])
    bf = pad(gate_b[:, 1, :])
    r = jnp.arange(n)
    tri_f = (r[None, :] <= r[:, None]).astype(BF16)
    tri_b = (r[None, :] >= r[:, None]).astype(BF16)

    def specs(rev):
        rows = lambda b, s: _chunk_rows(b, s, rev, n)
        return [pl.BlockSpec((n, qkw), lambda b, s: (rows(b, s), Z_MQ // qkw)),
                pl.BlockSpec((n, qkw), lambda b, s: (rows(b, s), Z_MK // qkw)),
                pl.BlockSpec((n, D_MLSTM), lambda b, s: (rows(b, s), Z_MV // D_MLSTM)),
                pl.BlockSpec((n, LANES), lambda b, s: (rows(b, s), Z_MG // LANES)),
                pl.BlockSpec((n, LANES), lambda b, s: (rows(b, s), Z_DT // LANES))]

    const = lambda b, s: (0, 0)
    return pl.pallas_call(
        _mlstm_scan_kernel,
        grid=(BATCH, steps),
        in_specs=specs(False) + specs(True) + [
            pl.BlockSpec((1, LANES), const), pl.BlockSpec((1, LANES), const),
            pl.BlockSpec((n, n), const), pl.BlockSpec((n, n), const)],
        out_specs=[pl.BlockSpec((n, D_MLSTM), lambda b, s: (_chunk_rows(b, s, False, n), 0)),
                   pl.BlockSpec((n, D_MLSTM), lambda b, s: (_chunk_rows(b, s, True, n), 0))],
        out_shape=[jax.ShapeDtypeStruct((T_ALL, D_MLSTM), F32), jax.ShapeDtypeStruct((T_ALL, D_MLSTM), F32)],
        scratch_shapes=[pltpu.VMEM((H_C // 2, 2 * DQK_C, 2 * DV_C), F32),
                        pltpu.VMEM((H_C // 2, 2 * DQK_C, 2 * DV_C), F32),
                        pltpu.VMEM((8, LANES), F32), pltpu.VMEM((8, LANES), F32)],
        compiler_params=_cparams("arbitrary", "arbitrary"),
        name="mlstm_scan",
    )(z, z, z, z, z, z, z, z, z, z, bi, bf, tri_f, tri_b)


def _mlstm_final_kernel(hf_ref, hb_ref, o_ref_in, onw_ref, o_ref):
    gate = jax.nn.sigmoid(o_ref_in[...])
    onw = onw_ref[...]
    for h in range(H_C):
        cs = slice(h * DV_C, (h + 1) * DV_C)
        hn = _rms(hf_ref[:, cs] + hb_ref[:, cs])
        o_ref[:, cs] = (hn * gate[:, cs] * onw[:, cs]).astype(o_ref.dtype)


def _mlstm_final(hf, hb, z, onw):
    row = lambda i: (i, 0)
    return pl.pallas_call(
        _mlstm_final_kernel,
        grid=(T_ALL // ROW_TILE,),
        in_specs=[pl.BlockSpec((ROW_TILE, D_MLSTM), row), pl.BlockSpec((ROW_TILE, D_MLSTM), row),
                  pl.BlockSpec((ROW_TILE, D_MLSTM), lambda i: (i, Z_MO // D_MLSTM)),
                  pl.BlockSpec((1, D_MLSTM), lambda i: (0, 0))],
        out_specs=pl.BlockSpec((ROW_TILE, D_MLSTM), row),
        out_shape=jax.ShapeDtypeStruct((T_ALL, D_MLSTM), BF16),
        compiler_params=_cparams("arbitrary"),
        name="mlstm_final",
    )(hf, hb, z, onw.reshape(1, D_MLSTM))


def _mlstm_mixer_pallas(z, p, onw):
    hf, hb = _mlstm_scan(z, p["mlstm_gate_b"])
    return _mlstm_final(hf, hb, z, onw)


S5_Q = 256
S5_SEG = S5_Q // 8
S5_LANES = G_S5 * P_S5
S5_SLAB = 512
S5_NSLAB = S5_LANES // S5_SLAB


def _s5_dir_kernel(u_ref, perm_ref, wbr_ref, wbi_ref, atab_ref, apr_ref, api_ref, wcr_ref, wci_ref, y_ref,
                   ur_s, ui_s, xr_s, xi_s, st_s, car_s, *, reverse):
    @pl.when(pl.program_id(1) == 0)
    def _():
        st_s[...] = jnp.zeros_like(st_s)

    up = jnp.dot(perm_ref[...], u_ref[...].astype(BF16), preferred_element_type=F32).astype(BF16)
    for m in range(S5_NSLAB):
        um = up[:, m * LANES:(m + 1) * LANES]
        ur_s[:, m * S5_SLAB:(m + 1) * S5_SLAB] = jnp.dot(um, wbr_ref[m], preferred_element_type=F32)
        ui_s[:, m * S5_SLAB:(m + 1) * S5_SLAB] = jnp.dot(um, wbi_ref[m], preferred_element_type=F32)

    per = 4
    for grp in range(S5_LANES // (per * LANES)):
        cols = [grp * per * LANES + j * LANES for j in range(per)]
        a_r = [atab_ref[0:8, c0:c0 + LANES] for c0 in cols]
        a_i = [atab_ref[8:16, c0:c0 + LANES] for c0 in cols]

        def body(i, carry, cols=cols, a_r=a_r, a_i=a_i):
            t = (S5_SEG - 1 - i) if reverse else i
            r0 = pl.multiple_of(t * 8, 8)
            new = []
            for j, c0 in enumerate(cols):
                xr, xi = carry[2 * j], carry[2 * j + 1]
                nr = a_r[j] * xr - a_i[j] * xi + ur_s[pl.ds(r0, 8), c0:c0 + LANES]
                ni = a_r[j] * xi + a_i[j] * xr + ui_s[pl.ds(r0, 8), c0:c0 + LANES]
                ur_s[pl.ds(r0, 8), c0:c0 + LANES] = nr
                ui_s[pl.ds(r0, 8), c0:c0 + LANES] = ni
                new += [nr, ni]
            return tuple(new)

        lax.fori_loop(0, S5_SEG, body, tuple(jnp.zeros((8, LANES), F32) for _ in range(2 * per)), unroll=2)

    as_r = atab_ref[16:17, :]
    as_i = atab_ref[17:18, :]
    end_row = 0 if reverse else 8 * (S5_SEG - 1)
    cr = st_s[0:1, :]
    ci = st_s[1:2, :]
    for k in (range(7, -1, -1) if reverse else range(8)):
        car_s[k:k + 1, :] = cr
        car_s[8 + k:9 + k, :] = ci
        er = ur_s[end_row + k:end_row + k + 1, :]
        ei = ui_s[end_row + k:end_row + k + 1, :]
        cr, ci = er + as_r * cr - as_i * ci, ei + as_r * ci + as_i * cr
    st_s[0:1, :] = cr
    st_s[1:2, :] = ci

    for m in range(S5_NSLAB):
        cs = slice(m * S5_SLAB, (m + 1) * S5_SLAB)
        c_r = jnp.concatenate([car_s[0:8, cs], car_s[0:8, cs]], axis=0)
        c_i = jnp.concatenate([car_s[8:16, cs], car_s[8:16, cs]], axis=0)

        def fix(i, _, cs=cs, c_r=c_r, c_i=c_i):
            r0 = pl.multiple_of(i * 16, 16)
            p_r = apr_ref[pl.ds(r0, 16), cs]
            p_i = api_ref[pl.ds(r0, 16), cs]
            xr_s[pl.ds(r0, 16), cs] = (ur_s[pl.ds(r0, 16), cs] + p_r * c_r - p_i * c_i).astype(BF16)
            xi_s[pl.ds(r0, 16), cs] = (ui_s[pl.ds(r0, 16), cs] + p_r * c_i + p_i * c_r).astype(BF16)
            return 0

        lax.fori_loop(0, S5_Q // 16, fix, 0, unroll=2)

    for m in range(S5_NSLAB):
        cs = slice(m * S5_SLAB, (m + 1) * S5_SLAB)
        y_ref[:, m * LANES:(m + 1) * LANES] = (
            jnp.dot(xr_s[:, cs], wcr_ref[m], preferred_element_type=F32)
            - jnp.dot(xi_s[:, cs], wci_ref[m], preferred_element_type=F32))


def _s5_direction(z, perm, wbr, wbi, atab, apr, api, wcr, wci, reverse):
    n_x = SEQ // S5_Q
    ctx_blk = T_X // S5_Q

    def rows(b, s):
        xs = (n_x - s) if reverse else (s - 1)
        return jnp.where(s == 0, ctx_blk + b, b * n_x + xs)

    const2 = lambda b, s: (0, 0)
    const3 = lambda b, s: (0, 0, 0)
    return pl.pallas_call(
        functools.partial(_s5_dir_kernel, reverse=reverse),
        grid=(BATCH, n_x + 1),
        in_specs=[pl.BlockSpec((S5_Q, D_S5), lambda b, s: (rows(b, s), Z_D // D_S5)),
                  pl.BlockSpec((S5_Q, S5_Q), const2),
                  pl.BlockSpec(wbr.shape, const3),
                  pl.BlockSpec(wbi.shape, const3),
                  pl.BlockSpec(atab.shape, const2),
                  pl.BlockSpec(apr.shape, const2),
                  pl.BlockSpec(api.shape, const2),
                  pl.BlockSpec(wcr.shape, const3),
                  pl.BlockSpec(wci.shape, const3)],
        out_specs=pl.BlockSpec((S5_Q, D_S5), lambda b, s: (rows(b, s), 0)),
        out_shape=jax.ShapeDtypeStruct((T_ALL, D_S5), F32),
        scratch_shapes=[pltpu.VMEM((S5_Q, S5_LANES), F32), pltpu.VMEM((S5_Q, S5_LANES), F32),
                        pltpu.VMEM((S5_Q, S5_LANES), BF16), pltpu.VMEM((S5_Q, S5_LANES), BF16),
                        pltpu.VMEM((8, S5_LANES), F32), pltpu.VMEM((16, S5_LANES), F32)],
        compiler_params=_cparams("arbitrary", "arbitrary"),
        name="s5_bwd" if reverse else "s5_fwd",
    )(z, perm, wbr, wbi, atab, apr, api, wcr, wci)


def _s5_final_kernel(yf_ref, yb_ref, u_ref, permt_ref, d_ref, gw_ref, gb_ref, onw_ref, o_ref):
    ys = yf_ref[...] + yb_ref[...]
    hi = ys.astype(BF16)
    r1 = ys - hi.astype(F32)
    mid = r1.astype(BF16)
    lo = (r1 - mid.astype(F32)).astype(BF16)
    pt = permt_ref[...]
    y = (jnp.dot(pt, hi, preferred_element_type=F32) + jnp.dot(pt, mid, preferred_element_type=F32)
         + jnp.dot(pt, lo, preferred_element_type=F32))
    y = jax.nn.gelu(y + d_ref[...] * u_ref[...])
    gate = jax.nn.sigmoid(jnp.dot(y.astype(BF16), gw_ref[...], preferred_element_type=F32) + gb_ref[...])
    o_ref[...] = (_rms(y * gate) * onw_ref[...]).astype(o_ref.dtype)


def _s5_final(yf, yb, z, permt, d_skip, glu_w, glu_b, onw):
    row = lambda i: (i, 0)
    const = lambda i: (0, 0)
    return pl.pallas_call(
        _s5_final_kernel,
        grid=(T_ALL // S5_Q,),
        in_specs=[pl.BlockSpec((S5_Q, D_S5), row),
                  pl.BlockSpec((S5_Q, D_S5), row),
                  pl.BlockSpec((S5_Q, D_S5), lambda i: (i, Z_D // D_S5)),
                  pl.BlockSpec((S5_Q, S5_Q), const),
                  pl.BlockSpec((1, D_S5), const),
                  pl.BlockSpec((D_S5, D_S5), const),
                  pl.BlockSpec((1, D_S5), const),
                  pl.BlockSpec((1, D_S5), const)],
        out_specs=pl.BlockSpec((S5_Q, D_S5), row),
        out_shape=jax.ShapeDtypeStruct((T_ALL, D_S5), BF16),
        compiler_params=_cparams("arbitrary"),
        name="s5_final",
    )(yf, yb, z, permt, d_skip.reshape(1, D_S5), glu_w.astype(BF16), glu_b.reshape(1, D_S5), onw.reshape(1, D_S5))


def _s5_tables(lam_re, lam_im, log_dt, b_re, b_im, c_re, c_im):
    dt = jnp.exp(log_dt)[..., None]
    mag = jnp.exp(lam_re * dt)
    ar = mag * jnp.cos(lam_im * dt)
    ai = mag * jnp.sin(lam_im * dt)
    den = lam_re * lam_re + lam_im * lam_im
    cr_ = ((ar - 1.0) * lam_re + ai * lam_im) / den
    ci_ = (ai * lam_re - (ar - 1.0) * lam_im) / den
    bbr = cr_[..., None] * b_re - ci_[..., None] * b_im
    bbi = cr_[..., None] * b_im + ci_[..., None] * b_re
    gps = S5_SLAB // P_S5
    eye = jnp.eye(gps, dtype=F32)

    def drive_w(bb):
        t = bb.reshape(S5_NSLAB, gps, P_S5, S5_GROUP)
        w = jnp.einsum('mgpc,gh->mgchp', t, eye)
        return w.reshape(S5_NSLAB, gps * S5_GROUP, gps * P_S5).astype(BF16)

    def read_w(cc):
        t = cc.reshape(S5_NSLAB, gps, S5_GROUP, P_S5)
        w = jnp.einsum('mgcp,gh->mgphc', t, eye)
        return w.reshape(S5_NSLAB, gps * P_S5, gps * S5_GROUP).astype(BF16)

    pows_r, pows_i = [], []
    for d in range(2):
        a_r = ar[d].reshape(1, S5_LANES)
        a_i = ai[d].reshape(1, S5_LANES)
        pr, pi = [a_r], [a_i]
        for _ in range(S5_SEG - 1):
            pr, pi = pr + [pr[-1] * a_r - pi[-1] * a_i], pi + [pr[-1] * a_i + pi[-1] * a_r]
        pows_r.append(pr)
        pows_i.append(pi)

    out = []
    for d in range(2):
        pr, pi = pows_r[d], pows_i[d]
        order = list(range(S5_SEG))
        if d == 1:
            order = order[::-1]
        apr = jnp.concatenate([jnp.broadcast_to(pr[k], (8, S5_LANES)) for k in order], axis=0)
        api = jnp.concatenate([jnp.broadcast_to(pi[k], (8, S5_LANES)) for k in order], axis=0)
        atab = jnp.concatenate([jnp.broadcast_to(pr[0], (8, S5_LANES)), jnp.broadcast_to(pi[0], (8, S5_LANES)),
                                pr[-1], pi[-1], jnp.zeros((6, S5_LANES), F32)], axis=0)
        out.append((drive_w(bbr[d]), drive_w(bbi[d]), atab, apr, api))
    r = jnp.arange(S5_Q)
    src = (r % 8) * S5_SEG + r // 8
    perm = (src[:, None] == jnp.arange(S5_Q)[None, :]).astype(BF16)
    return out, read_w(c_re), read_w(c_im), perm


def _s5_mixer_pallas(z, p, onw):
    dirs, wcr, wci, perm = _s5_tables(p["s5_lam_re"], p["s5_lam_im"], p["s5_log_dt"], p["s5_b_re"], p["s5_b_im"],
                                      p["s5_c_re"], p["s5_c_im"])
    yf = _s5_direction(z, perm, *dirs[0], wcr, wci, reverse=False)
    yb = _s5_direction(z, perm, *dirs[1], wcr, wci, reverse=True)
    return _s5_final(yf, yb, z, perm.T, p["s5_d"], p["s5_glu_w"], p["s5_glu_b"], onw)


def _rope_tables():
    pos = jnp.arange(SEQ)
    row = (pos // GRID_W).astype(F32)
    col = (pos % GRID_W).astype(F32)
    inv_freq = ROPE_BASE ** (-jnp.arange(ROPE_AXIS // 2, dtype=F32) * 2.0 / ROPE_AXIS)
    ang_r = row[:, None] * inv_freq
    ang_c = col[:, None] * inv_freq
    zeros = jnp.zeros((SEQ, LANES - D_ROPE), F32)
    cos = jnp.concatenate([jnp.cos(ang_r), jnp.cos(ang_r), jnp.cos(ang_c), jnp.cos(ang_c), zeros], axis=1)
    sin = jnp.concatenate([jnp.sin(ang_r), jnp.sin(ang_r), jnp.sin(ang_c), jnp.sin(ang_c), zeros], axis=1)
    cos_c = jnp.concatenate([jnp.ones((ROW_TILE, D_ROPE), F32), jnp.zeros((ROW_TILE, LANES - D_ROPE), F32)], axis=1)
    sin_c = jnp.zeros((ROW_TILE, LANES), F32)
    return jnp.concatenate([jnp.concatenate([cos, sin], axis=1), jnp.concatenate([cos_c, sin_c], axis=1)], axis=0)


def _rot_cols(w):
    q = ROPE_AXIS // 2
    return jnp.concatenate([-w[:, q:2 * q], w[:, 0:q], -w[:, 3 * q:4 * q], w[:, 2 * q:3 * q]], axis=1)


W_IN_B0 = A_COLS
W_IN_C0 = A_COLS + B_COLS
W_IN_D0 = A_COLS + B_COLS + C_COLS
W_IN_CM = W_IN_C0 + 2 * H_C * DQK_C
W_IN_WIDE = ((Z_CQ, 0, Q_RANK + KV_RANK),
             (Z_MQ, W_IN_C0, H_C * DQK_C),
             (Z_XBC, W_IN_B0 + D_SSD, CONV_CH),
             (Z_SZ, W_IN_B0, D_SSD),
             (Z_MV, W_IN_CM, 2 * D_MLSTM),
             (Z_D, W_IN_D0, D_S5),
             (Z_MK, W_IN_C0 + H_C * DQK_C, H_C * DQK_C))
W_IN_ROWS = 256


def _w_in_layout_kernel(w_ref, small_ref, o_ref):
    for dst, src, width in W_IN_WIDE:
        o_ref[:, dst:dst + width] = w_ref[:, src:src + width].astype(BF16)
    o_ref[:, Z_KR:Z_MQ] = small_ref[:, 0:2 * LANES]
    o_ref[:, Z_DT:Z_COLS] = small_ref[:, 2 * LANES:4 * LANES]


def _layout_w_in_pallas(w):
    k, n = w.shape
    zpad = lambda c: jnp.zeros((k, c), w.dtype)
    kr = w[:, Q_RANK + KV_RANK:A_COLS]
    gb = W_IN_CM + 2 * D_MLSTM
    small = jnp.concatenate([
        kr, zpad(LANES - D_ROPE), _rot_cols(kr), zpad(LANES - D_ROPE),
        w[:, W_IN_B0 + D_SSD + CONV_CH:W_IN_C0], w[:, gb + H_C:gb + 2 * H_C], w[:, gb + 3 * H_C:gb + 4 * H_C],
        zpad(LANES - GATE_LANE0 - 2 * H_C),
        zpad(GATE_LANE0), w[:, gb:gb + H_C], w[:, gb + 2 * H_C:gb + 3 * H_C],
        zpad(LANES - GATE_LANE0 - 2 * H_C)], axis=1).astype(BF16)
    return pl.pallas_call(
        _w_in_layout_kernel,
        grid=(k // W_IN_ROWS,),
        in_specs=[pl.BlockSpec((W_IN_ROWS, n), lambda i: (i, 0)),
                  pl.BlockSpec((W_IN_ROWS, 4 * LANES), lambda i: (i, 0))],
        out_specs=pl.BlockSpec((W_IN_ROWS, Z_COLS), lambda i: (i, 0)),
        out_shape=jax.ShapeDtypeStruct((k, Z_COLS), BF16),
        compiler_params=_cparams("arbitrary"),
        name="w_in_layout",
    )(w, small)


def _layout_w_in(w):
    k = w.shape[0]
    zpad = lambda n: jnp.zeros((k, n), w.dtype)
    b0 = A_COLS
    c0 = A_COLS + B_COLS
    d0 = A_COLS + B_COLS + C_COLS
    kr = w[:, Q_RANK + KV_RANK:A_COLS]
    o1 = H_C * DQK_C
    cm = c0 + 2 * o1
    gb = cm + 2 * D_MLSTM
    cols = [w[:, 0:Q_RANK + KV_RANK], kr, zpad(LANES - D_ROPE), _rot_cols(kr), zpad(LANES - D_ROPE),
            w[:, c0:c0 + o1],
            w[:, b0 + D_SSD:b0 + D_SSD + CONV_CH],
            w[:, b0:b0 + D_SSD],
            w[:, cm:cm + 2 * D_MLSTM],
            w[:, d0:],
            w[:, c0 + o1:cm],
            w[:, b0 + D_SSD + CONV_CH:c0], w[:, gb + H_C:gb + 2 * H_C], w[:, gb + 3 * H_C:gb + 4 * H_C],
            zpad(LANES - GATE_LANE0 - 2 * H_C),
            zpad(GATE_LANE0), w[:, gb:gb + H_C], w[:, gb + 2 * H_C:gb + 3 * H_C],
            zpad(LANES - GATE_LANE0 - 2 * H_C)]
    out = jnp.concatenate(cols, axis=1)
    assert out.shape[1] == Z_COLS
    return out.astype(BF16)


def _layout_mla(w_uq, w_ukv):
    k = w_uq.shape[0]
    qa, qb, wk, wv = [], [], [], []
    for h in range(H_A):
        base = h * (D_NOPE + D_ROPE)
        rope = w_uq[:, base + D_NOPE:base + D_NOPE + D_ROPE]
        qa += [w_uq[:, base:base + D_NOPE], rope, jnp.zeros((k, LANES - D_ROPE), w_uq.dtype)]
        qb += [_rot_cols(rope), jnp.zeros((k, LANES - D_ROPE), w_uq.dtype)]
        kb = h * (D_NOPE + D_V)
        wk.append(w_ukv[:, kb:kb + D_NOPE])
        wv.append(w_ukv[:, kb + D_NOPE:kb + D_NOPE + D_V])
    cat = lambda xs: jnp.concatenate(xs, axis=1).astype(BF16)
    return cat(qa), cat(qb), cat(wk), cat(wv)


def _layer(xall, modtab, p, cs, last):
    hx = _prenorm(xall, p["norm1_w"], modtab, 0, 1)
    z = _mm(hx, _layout_w_in_pallas(p["w_in"]), ROW_TILE, Z_COLS // 3)
    onw = p["out_norm_w"]
    wqa, wqb, wk, wv = _layout_mla(p["mla_w_uq"], p["mla_w_ukv"])
    q, k, v = _mla_prep(z, cs, p["mla_q_norm_w"], p["mla_kv_norm_w"], wqa, wqb, wk, wv)
    onw_a = onw[:D_MLA].reshape(1, D_MLA)
    ya = jnp.concatenate([_attention_latent(q, k, v, onw_a), _attention_ctx(q, k, v, onw_a)], axis=0)
    yb = _ssd_mixer_pallas(z, p, onw[D_MLA:D_MLA + D_SSD])
    yc = _mlstm_mixer_pallas(z, p, onw[D_MLA + D_SSD:D_MLA + D_SSD + D_MLSTM])
    yd = _s5_mixer_pallas(z, p, onw[D_MLA + D_SSD + D_MLSTM:])
    xall = _outproj([ya, yb, yc, yd], p["w_out"].astype(BF16), xall, modtab, 2)

    w_router = jnp.concatenate([p["moe_w_group"], p["moe_w_expert"],
                                jnp.zeros((D_MODEL, ROUTER_COLS - N_GROUPS - N_EXPERTS), F32)], axis=1)
    n_tok = T_X if last else T_ALL
    x_mid = xall[:n_tok]
    h2, route, cnt = _prenorm_router(x_mid, p["norm2_w"], modtab, w_router, 3, 4)
    return _moe(x_mid, h2, route, cnt, modtab, p["moe_w_gate"], p["moe_w_up"], p["moe_w_down"])


def kernel(x, c, ctx, c_ctx, mod_w, mod_b, norm1_w, w_in, mla_q_norm_w, mla_kv_norm_w, mla_w_uq, mla_w_ukv,
           ssd_conv_w, ssd_conv_b, ssd_a_log, ssd_dt_bias, ssd_d, mlstm_gate_b, s5_lam_re, s5_lam_im,
           s5_log_dt, s5_b_re, s5_b_im, s5_c_re, s5_c_im, s5_d, s5_glu_w, s5_glu_b, out_norm_w, w_out,
           norm2_w, moe_w_group, moe_w_expert, moe_w_gate, moe_w_up, moe_w_down, final_norm_w):
    stacked = {"norm1_w": norm1_w, "w_in": w_in, "mla_q_norm_w": mla_q_norm_w, "mla_kv_norm_w": mla_kv_norm_w,
               "mla_w_uq": mla_w_uq, "mla_w_ukv": mla_w_ukv, "ssd_conv_w": ssd_conv_w, "ssd_conv_b": ssd_conv_b,
               "ssd_a_log": ssd_a_log, "ssd_dt_bias": ssd_dt_bias, "ssd_d": ssd_d, "mlstm_gate_b": mlstm_gate_b,
               "s5_lam_re": s5_lam_re, "s5_lam_im": s5_lam_im, "s5_log_dt": s5_log_dt, "s5_b_re": s5_b_re,
               "s5_b_im": s5_b_im, "s5_c_re": s5_c_re, "s5_c_im": s5_c_im, "s5_d": s5_d, "s5_glu_w": s5_glu_w,
               "s5_glu_b": s5_glu_b, "out_norm_w": out_norm_w, "w_out": w_out, "norm2_w": norm2_w,
               "moe_w_group": moe_w_group, "moe_w_expert": moe_w_expert, "moe_w_gate": moe_w_gate,
               "moe_w_up": moe_w_up, "moe_w_down": moe_w_down}
    cs = _rope_tables()
    cc = jnp.concatenate([c, c_ctx[None, :], jnp.zeros((8 - BATCH - 1, D_MODEL), F32)], axis=0)
    xall = jnp.concatenate([x.reshape(T_X, D_MODEL), ctx.reshape(T_C, D_MODEL)], axis=0)
    for l in range(DEPTH):
        p = {name: val[l] for name, val in stacked.items()}
        modtab = _modulation(cc, mod_w[l], mod_b[l])[:BATCH + 1].reshape(BATCH + 1, 6, D_MODEL)
        xall = _layer(xall, modtab, p, cs, l == DEPTH - 1)
    return _final_norm(xall, final_norm_w).reshape(BATCH, SEQ, D_MODEL)
```

```python
import functools
import math

import jax
import jax.numpy as jnp
from jax import lax
from jax.experimental import pallas as pl
from jax.experimental.pallas import tpu as pltpu

F32 = jnp.float32
BF16 = jnp.bfloat16

D_MODEL = 2048
BATCH = 2
SEQ = 4096
DEPTH = 2
GRID_W = 64
CTX_LEN = 256
EPS = 1e-6
NEG_STATE = -1e30
NEG_BIG = -1e30

H_A = 4
D_NOPE = 128
D_ROPE = 64
D_V = 128
Q_RANK = 384
KV_RANK = 128
ROPE_AXIS = D_ROPE // 2
ROPE_BASE = 10000.0
D_MLA = H_A * D_V
D_SSD = 512
P_B = 64
H_B = D_SSD // P_B
G_B = 2
N_B = 128
SSD_CONV = 5
SSD_CHUNK = 128
CONV_CH = D_SSD + 2 * G_B * N_B
D_MLSTM = 512
H_C = 4
DV_C = D_MLSTM // H_C
DQK_C = DV_C // 2
MLSTM_CHUNK = 128
D_S5 = 512
S5_GROUP = 16
G_S5 = D_S5 // S5_GROUP
P_S5 = 64
A_COLS = Q_RANK + KV_RANK + D_ROPE
B_COLS = D_SSD + CONV_CH + 2 * H_B
C_COLS = 2 * H_C * DQK_C + 2 * D_MLSTM + 4 * H_C
N_GROUPS = 4
EXPERTS_PER_GROUP = 8
N_EXPERTS = N_GROUPS * EXPERTS_PER_GROUP
TOP_K = 2
D_EXPERT = 512

PB = CTX_LEN + SEQ
T_X = BATCH * SEQ
T_ALL = BATCH * PB

LANES = 128
VMEM_LIMIT_BYTES = 56 * 1024 * 1024

TILE = 256
TPB = PB // TILE
XT = SEQ // TILE
MM_ROWS = 512

Z_CQ = 0
Z_CKV = 384
Z_KR = 512
Z_KRR = 640
Z_MQ = 768
Z_XBC = 1024
Z_SZ = 2048
Z_MV = 2560
Z_MO = 3072
Z_D = 3584
Z_MK = 4096
Z_DT = 4352
Z_MG = 4480
Z_COLS = 4608
GATE_LANE0 = 2 * H_B

MOE_ROWS = 256
ROUTER_COLS = 128

_NT = (((1,), (1,)), ((), ()))
_TN = (((0,), (0,)), ((), ()))


def _cparams(*sem):
    return pltpu.CompilerParams(dimension_semantics=sem, vmem_limit_bytes=VMEM_LIMIT_BYTES)


def _tile_mod_row(i):
    return jnp.where(i % TPB == 0, BATCH, i // TPB)


def _token_tile(i, latent_only):
    return (i // XT) * TPB + 1 + i % XT if latent_only else i


def _rms(x, w=None):
    y = x * lax.rsqrt(jnp.mean(x * x, axis=-1, keepdims=True) + EPS)
    return y if w is None else y * w


def _split3(x):
    hi = x.astype(BF16)
    r1 = x - hi.astype(F32)
    mid = r1.astype(BF16)
    lo = (r1 - mid.astype(F32)).astype(BF16)
    return hi, mid, lo


def _dot3_left(sel, x):
    hi, mid, lo = _split3(x)
    return (jnp.dot(sel, hi, preferred_element_type=F32) + jnp.dot(sel, mid, preferred_element_type=F32)
            + jnp.dot(sel, lo, preferred_element_type=F32))


def _dot3_right(x, sel):
    hi, mid, lo = _split3(x)
    return (jnp.dot(hi, sel, preferred_element_type=F32) + jnp.dot(mid, sel, preferred_element_type=F32)
            + jnp.dot(lo, sel, preferred_element_type=F32))


def _mod_kernel(a_ref, w_ref, b_ref, o_ref):
    a = a_ref[...]
    a = a * jax.nn.sigmoid(a)
    o_ref[...] = jnp.dot(a.astype(BF16), w_ref[...].astype(BF16), preferred_element_type=F32) + b_ref[...]


def _modulation(cc, mod_w, mod_b, l):
    n = mod_w.shape[2]
    tn = 1024
    return pl.pallas_call(
        _mod_kernel,
        grid=(n // tn,),
        in_specs=[pl.BlockSpec((8, D_MODEL), lambda j: (0, 0)),
                  pl.BlockSpec((None, D_MODEL, tn), lambda j: (l, 0, j)),
                  pl.BlockSpec((None, 1, tn), lambda j: (l, 0, j))],
        out_specs=pl.BlockSpec((8, tn), lambda j: (0, j)),
        out_shape=jax.ShapeDtypeStruct((8, n), F32),
        compiler_params=_cparams("arbitrary"),
        name="modulation",
    )(cc, mod_w, mod_b.reshape(DEPTH, 1, n))


def _prenorm_kernel(x_ref, w_ref, mod_ref, o_ref, *, sh_idx, sc_idx):
    y = _rms(x_ref[...], w_ref[...])
    m = mod_ref[0]
    y = y * (1.0 + m[sc_idx:sc_idx + 1, :]) + m[sh_idx:sh_idx + 1, :]
    o_ref[...] = y.astype(o_ref.dtype)


def _prenorm(x, w, modtab, sh_idx, sc_idx):
    rows = x.shape[0]
    return pl.pallas_call(
        functools.partial(_prenorm_kernel, sh_idx=sh_idx, sc_idx=sc_idx),
        grid=(rows // TILE,),
        in_specs=[pl.BlockSpec((TILE, D_MODEL), lambda i: (i, 0)),
                  pl.BlockSpec((1, D_MODEL), lambda i: (0, 0)),
                  pl.BlockSpec((1, 6, D_MODEL), lambda i: (_tile_mod_row(i), 0, 0))],
        out_specs=pl.BlockSpec((TILE, D_MODEL), lambda i: (i, 0)),
        out_shape=jax.ShapeDtypeStruct((rows, D_MODEL), BF16),
        compiler_params=_cparams("arbitrary"),
        name="prenorm",
    )(x, w.reshape(1, D_MODEL), modtab)


def _mm_kernel(a_ref, w_ref, o_ref):
    o_ref[...] = jnp.dot(a_ref[...], w_ref[...], preferred_element_type=F32).astype(o_ref.dtype)


def _mm(a, w, tm, tn, out_dtype=F32):
    m, k = a.shape
    n = w.shape[1]
    return pl.pallas_call(
        _mm_kernel,
        grid=(n // tn, m // tm),
        in_specs=[pl.BlockSpec((tm, k), lambda j, i: (i, 0)),
                  pl.BlockSpec((k, tn), lambda j, i: (0, j))],
        out_specs=pl.BlockSpec((tm, tn), lambda j, i: (i, j)),
        out_shape=jax.ShapeDtypeStruct((m, n), out_dtype),
        compiler_params=_cparams("arbitrary", "arbitrary"),
        name="in_proj",
    )(a, w)


W_IN_B0 = A_COLS
W_IN_C0 = A_COLS + B_COLS
W_IN_D0 = A_COLS + B_COLS + C_COLS
W_IN_CM = W_IN_C0 + 2 * H_C * DQK_C
W_IN_GB = W_IN_CM + 2 * D_MLSTM
W_IN_WIDE = ((Z_CQ, 0, Q_RANK + KV_RANK),
             (Z_MQ, W_IN_C0, H_C * DQK_C),
             (Z_XBC, W_IN_B0 + D_SSD, CONV_CH),
             (Z_SZ, W_IN_B0, D_SSD),
             (Z_MV, W_IN_CM, 2 * D_MLSTM),
             (Z_D, W_IN_D0, D_S5),
             (Z_MK, W_IN_C0 + H_C * DQK_C, H_C * DQK_C))
W_IN_ROWS = 256


def _rot_cols(w):
    q = ROPE_AXIS // 2
    return jnp.concatenate([-w[:, q:2 * q], w[:, 0:q], -w[:, 3 * q:4 * q], w[:, 2 * q:3 * q]], axis=1)


def _w_in_layout_kernel(w_ref, small_ref, o_ref):
    for dst, src, width in W_IN_WIDE:
        o_ref[:, dst:dst + width] = w_ref[:, src:src + width].astype(BF16)
    o_ref[:, Z_KR:Z_MQ] = small_ref[:, 0:2 * LANES]
    o_ref[:, Z_DT:Z_COLS] = small_ref[:, 2 * LANES:4 * LANES]


def _layout_w_in(w_in, l):
    _, k, n = w_in.shape
    w = w_in[l]
    zpad = lambda c: jnp.zeros((k, c), F32)
    kr = w[:, Q_RANK + KV_RANK:A_COLS]
    gb = W_IN_GB
    small = jnp.concatenate([
        kr, zpad(LANES - D_ROPE), _rot_cols(kr), zpad(LANES - D_ROPE),
        w[:, W_IN_B0 + D_SSD + CONV_CH:W_IN_C0], w[:, gb + H_C:gb + 2 * H_C], w[:, gb + 3 * H_C:gb + 4 * H_C],
        zpad(LANES - GATE_LANE0 - 2 * H_C),
        zpad(GATE_LANE0), w[:, gb:gb + H_C], w[:, gb + 2 * H_C:gb + 3 * H_C],
        zpad(LANES - GATE_LANE0 - 2 * H_C)], axis=1).astype(BF16)
    return pl.pallas_call(
        _w_in_layout_kernel,
        grid=(k // W_IN_ROWS,),
        in_specs=[pl.BlockSpec((None, W_IN_ROWS, n), lambda i: (l, i, 0)),
                  pl.BlockSpec((W_IN_ROWS, 4 * LANES), lambda i: (i, 0))],
        out_specs=pl.BlockSpec((W_IN_ROWS, Z_COLS), lambda i: (i, 0)),
        out_shape=jax.ShapeDtypeStruct((k, Z_COLS), BF16),
        compiler_params=_cparams("arbitrary"),
        name="w_in_layout",
    )(w_in, small)


def _mla_prep_kernel(za_ref, cs_ref, qw_ref, kvw_ref, wqa_ref, wqb_ref, wk_ref, wv_ref, q_ref, k_ref, v_ref):
    za = za_ref[...]
    cos = cs_ref[:, :LANES]
    sin = cs_ref[:, LANES:]
    qn = _rms(za[:, Z_CQ:Z_CQ + Q_RANK], qw_ref[...]).astype(BF16)
    kvn = _rms(za[:, Z_CKV:Z_CKV + KV_RANK], kvw_ref[...]).astype(BF16)
    qa = jnp.dot(qn, wqa_ref[...], preferred_element_type=F32)
    qb = jnp.dot(qn, wqb_ref[...], preferred_element_type=F32)
    kn = jnp.dot(kvn, wk_ref[...], preferred_element_type=F32)
    v = jnp.dot(kvn, wv_ref[...], preferred_element_type=F32)
    kr = (za[:, Z_KR:Z_KR + LANES] * cos + za[:, Z_KRR:Z_KRR + LANES] * sin).astype(BF16)
    for h in range(H_A):
        c0 = h * 2 * LANES
        q_ref[:, c0:c0 + LANES] = qa[:, c0:c0 + LANES].astype(BF16)
        q_ref[:, c0 + LANES:c0 + 2 * LANES] = (
            qa[:, c0 + LANES:c0 + 2 * LANES] * cos + qb[:, h * LANES:(h + 1) * LANES] * sin).astype(BF16)
        k_ref[:, c0:c0 + LANES] = kn[:, h * LANES:(h + 1) * LANES].astype(BF16)
        k_ref[:, c0 + LANES:c0 + 2 * LANES] = kr
    v_ref[...] = v.astype(BF16)


ATT_W = H_A * 2 * LANES


def _mla_prep(z, cs, qw, kvw, wqa, wqb, wk, wv):
    const = lambda i: (0, 0)
    rope_blk = lambda i: (jnp.where(i % TPB == 0, XT, i % TPB - 1), 0)
    return pl.pallas_call(
        _mla_prep_kernel,
        grid=(T_ALL // TILE,),
        in_specs=[pl.BlockSpec((TILE, Z_MQ), lambda i: (i, 0)),
                  pl.BlockSpec((TILE, 2 * LANES), rope_blk),
                  pl.BlockSpec((1, Q_RANK), const),
                  pl.BlockSpec((1, KV_RANK), const),
                  pl.BlockSpec(wqa.shape, const),
                  pl.BlockSpec(wqb.shape, const),
                  pl.BlockSpec(wk.shape, const),
                  pl.BlockSpec(wv.shape, const)],
        out_specs=[pl.BlockSpec((TILE, ATT_W), lambda i: (i, 0)),
                   pl.BlockSpec((TILE, ATT_W), lambda i: (i, 0)),
                   pl.BlockSpec((TILE, D_MLA), lambda i: (i, 0))],
        out_shape=[jax.ShapeDtypeStruct((T_ALL, ATT_W), BF16),
                   jax.ShapeDtypeStruct((T_ALL, ATT_W), BF16),
                   jax.ShapeDtypeStruct((T_ALL, D_MLA), BF16)],
        compiler_params=_cparams("arbitrary"),
        name="mla_prep",
    )(z, cs, qw.reshape(1, Q_RANK), kvw.reshape(1, KV_RANK), wqa, wqb, wk, wv)


def _attn_tile(q_ref, k_ref, v_ref, w_ref, o_ref, acc_ref, n_keys):
    scale2 = (D_NOPE + D_ROPE) ** -0.5 * math.log2(math.e)
    for h in range(H_A):
        q = q_ref[:, h * 2 * LANES:(h + 1) * 2 * LANES]
        s = lax.dot_general(q, k_ref[0:n_keys, h * 2 * LANES:(h + 1) * 2 * LANES], _NT, preferred_element_type=F32)
        m = jnp.max(s, axis=-1, keepdims=True)
        p = jnp.exp2((s - m) * scale2)
        l = jnp.sum(p, axis=-1, keepdims=True)
        o = jnp.dot(p.astype(BF16), v_ref[0:n_keys, h * D_V:(h + 1) * D_V], preferred_element_type=F32)
        acc_ref[:, h * D_V:(h + 1) * D_V] = o / l
    o_ref[...] = (_rms(acc_ref[...]) * w_ref[...]).astype(o_ref.dtype)


def _attn_kernel(q_ref, k_ref, v_ref, w_ref, o_ref, acc_ref):
    @pl.when(pl.program_id(1) == 0)
    def _():
        _attn_tile(q_ref, k_ref, v_ref, w_ref, o_ref, acc_ref, CTX_LEN)

    @pl.when(pl.program_id(1) != 0)
    def _():
        _attn_tile(q_ref, k_ref, v_ref, w_ref, o_ref, acc_ref, PB)


def _attention(q, k, v, onw):
    return pl.pallas_call(
        _attn_kernel,
        grid=(BATCH, TPB),
        in_specs=[pl.BlockSpec((TILE, ATT_W), lambda b, i: (b * TPB + i, 0)),
                  pl.BlockSpec((None, PB, ATT_W), lambda b, i: (b, 0, 0)),
                  pl.BlockSpec((None, PB, D_MLA), lambda b, i: (b, 0, 0)),
                  pl.BlockSpec((1, D_MLA), lambda b, i: (0, 0))],
        out_specs=pl.BlockSpec((TILE, D_MLA), lambda b, i: (b * TPB + i, 0)),
        out_shape=jax.ShapeDtypeStruct((T_ALL, D_MLA), BF16),
        scratch_shapes=[pltpu.VMEM((TILE, D_MLA), F32)],
        compiler_params=_cparams("arbitrary", "arbitrary"),
        name="attention",
    )(q, k.reshape(BATCH, PB, ATT_W), v.reshape(BATCH, PB, D_MLA), onw)


def _outproj_kernel(a0, a1, a2, a3, w0, w1, w2, w3, x_ref, mod_ref, o_ref, *, g_idx):
    acc = jnp.dot(a0[...], w0[...], preferred_element_type=F32)
    acc += jnp.dot(a1[...], w1[...], preferred_element_type=F32)
    acc += jnp.dot(a2[...], w2[...], preferred_element_type=F32)
    acc += jnp.dot(a3[...], w3[...], preferred_element_type=F32)
    g = mod_ref[0][g_idx:g_idx + 1, :]
    o_ref[...] = x_ref[...] + g * acc


def _outproj(ys, w, x, modtab, g_idx):
    kq = D_MODEL // 4
    a_specs = [pl.BlockSpec((TILE, kq), lambda i: (i, 0)) for _ in range(4)]
    w_specs = [pl.BlockSpec((kq, D_MODEL), functools.partial(lambda i, r: (r, 0), r=r)) for r in range(4)]
    return pl.pallas_call(
        functools.partial(_outproj_kernel, g_idx=g_idx),
        grid=(T_ALL // TILE,),
        in_specs=a_specs + w_specs + [
            pl.BlockSpec((TILE, D_MODEL), lambda i: (i, 0)),
            pl.BlockSpec((1, 6, D_MODEL), lambda i: (_tile_mod_row(i), 0, 0))],
        out_specs=pl.BlockSpec((TILE, D_MODEL), lambda i: (i, 0)),
        out_shape=jax.ShapeDtypeStruct((T_ALL, D_MODEL), F32),
        compiler_params=_cparams("arbitrary"),
        name="out_proj",
    )(*ys, w, w, w, w, x, modtab)


R_E0, R_E1, R_W0, R_W1, R_K0, R_K1 = range(6)
NO_LANE = 2 * LANES


def _prenorm_router_kernel(x_ref, w_ref, mod_ref, wr_ref, tri_ref, h_ref, route_ref, cnt_ref, cnt_s,
                           *, sh_idx, sc_idx):
    @pl.when(pl.program_id(0) == 0)
    def _():
        cnt_s[...] = jnp.zeros_like(cnt_s)

    y = _rms(x_ref[...], w_ref[...])
    m = mod_ref[0]
    y = y * (1.0 + m[sc_idx:sc_idx + 1, :]) + m[sh_idx:sh_idx + 1, :]
    h_ref[...] = y
    lg = jnp.dot(y, wr_ref[...], preferred_element_type=F32, precision=lax.Precision.HIGHEST)
    lane = lax.broadcasted_iota(jnp.int32, lg.shape, 1)

    def first_max(v):
        top = jnp.max(v, axis=1, keepdims=True)
        return top, jnp.min(jnp.where(v == top, lane, NO_LANE), axis=1, keepdims=True)

    is_g = lane < N_GROUPS
    g_top, g_idx = first_max(jnp.where(is_g, lg, NEG_BIG))
    g_w = 1.0 / jnp.sum(jnp.where(is_g, jnp.exp(lg - g_top), 0.0), axis=1, keepdims=True)
    lo = N_GROUPS + EXPERTS_PER_GROUP * g_idx
    el = jnp.where(jnp.logical_and(lane >= lo, lane < lo + EXPERTS_PER_GROUP), lg, NEG_BIG)
    v1, i1 = first_max(el)
    v2, i2 = first_max(jnp.where(lane == i1, NEG_BIG, el))
    t = jnp.exp(v2 - v1)
    w0 = g_w / (1.0 + t)
    w1 = g_w * t / (1.0 + t)
    e0 = i1 - N_GROUPS
    e1 = i2 - N_GROUPS
    hit0 = lane == e0
    hit1 = lane == e1
    onehot = jnp.logical_or(hit0, hit1).astype(F32)
    before = cnt_s[0:1, :] + jnp.dot(tri_ref[...], onehot.astype(BF16), preferred_element_type=F32)
    k0 = jnp.sum(jnp.where(hit0, before, 0.0), axis=1, keepdims=True)
    k1 = jnp.sum(jnp.where(hit1, before, 0.0), axis=1, keepdims=True)
    cnt_s[0:1, :] = cnt_s[0:1, :] + jnp.sum(onehot, axis=0, keepdims=True)
    cnt_ref[...] = cnt_s[...]
    rec = jnp.zeros(lg.shape, F32)
    for ln, val in ((R_E0, e0.astype(F32)), (R_E1, e1.astype(F32)), (R_W0, w0), (R_W1, w1), (R_K0, k0), (R_K1, k1)):
        rec = jnp.where(lane == ln, val, rec)
    route_ref[...] = rec


def _prenorm_router(x, w, modtab, w_router, sh_idx, sc_idx, latent_only):
    n_tok = T_X if latent_only else T_ALL
    r = jnp.arange(TILE)
    tri = (r[None, :] < r[:, None]).astype(BF16)
    src = lambda i: _token_tile(i, latent_only)
    return pl.pallas_call(
        functools.partial(_prenorm_router_kernel, sh_idx=sh_idx, sc_idx=sc_idx),
        grid=(n_tok // TILE,),
        in_specs=[pl.BlockSpec((TILE, D_MODEL), lambda i: (src(i), 0)),
                  pl.BlockSpec((1, D_MODEL), lambda i: (0, 0)),
                  pl.BlockSpec((1, 6, D_MODEL), lambda i: (_tile_mod_row(src(i)), 0, 0)),
                  pl.BlockSpec((D_MODEL, ROUTER_COLS), lambda i: (0, 0)),
                  pl.BlockSpec((TILE, TILE), lambda i: (0, 0))],
        out_specs=[pl.BlockSpec((TILE, D_MODEL), lambda i: (i, 0)),
                   pl.BlockSpec((TILE, ROUTER_COLS), lambda i: (i, 0)),
                   pl.BlockSpec((8, ROUTER_COLS), lambda i: (0, 0))],
        out_shape=[jax.ShapeDtypeStruct((n_tok, D_MODEL), F32),
                   jax.ShapeDtypeStruct((n_tok, ROUTER_COLS), F32),
                   jax.ShapeDtypeStruct((8, ROUTER_COLS), F32)],
        scratch_shapes=[pltpu.VMEM((8, ROUTER_COLS), F32)],
        compiler_params=_cparams("arbitrary"),
        name="prenorm_router",
    )(x, w.reshape(1, D_MODEL), modtab, w_router, tri)


def _row_copy(src, src_row, dst, dst_row, sem):
    return pltpu.make_async_copy(src.at[pl.ds(src_row, 1)], dst.at[pl.ds(dst_row, 1)], sem)


def _dispatch_kernel(dest_ref, h_ref, xb_ref, sem):
    def issue(r, carry):
        for k in range(TOP_K):
            _row_copy(h_ref, r, xb_ref, dest_ref[0, 0, TOP_K * r + k], sem).start(priority=k)
        return carry

    lax.fori_loop(0, TILE, issue, 0, unroll=8)

    def drain(r, carry):
        for k in range(TOP_K):
            _row_copy(h_ref, 0, xb_ref, 0, sem).wait()
        return carry

    lax.fori_loop(0, TILE, drain, 0, unroll=8)


def _dispatch(h, dest3):
    n_tok = h.shape[0]
    return pl.pallas_call(
        _dispatch_kernel,
        grid=(n_tok // TILE,),
        in_specs=[pl.BlockSpec((1, 1, TOP_K * TILE), lambda i: (i, 0, 0), memory_space=pltpu.SMEM),
                  pl.BlockSpec((TILE, D_MODEL), lambda i: (i, 0))],
        out_specs=pl.BlockSpec(memory_space=pl.ANY),
        out_shape=jax.ShapeDtypeStruct((n_tok * TOP_K, D_MODEL), F32),
        scratch_shapes=[pltpu.SemaphoreType.DMA(())],
        compiler_params=_cparams("arbitrary"),
        name="moe_dispatch",
    )(dest3, h)


def _moe_kernel(tile_ref, exp_ref, lo_ref, hi_ref, flag_ref, x_ref, wg_ref, wu_ref, wd_ref, o_ref,
                wg_s, wu_s, wd_s):
    i = pl.program_id(0)
    flags = flag_ref[i]

    @pl.when(flags % 2 == 1)
    def _():
        wg_s[...] = wg_ref[...].astype(BF16)
        wu_s[...] = wu_ref[...].astype(BF16)
        wd_s[...] = wd_ref[...].astype(BF16)

    @pl.when(flags >= 4)
    def _():
        x = x_ref[...].astype(BF16)
        g = jnp.dot(x, wg_s[...], preferred_element_type=F32)
        u = jnp.dot(x, wu_s[...], preferred_element_type=F32)
        row = lax.broadcasted_iota(jnp.int32, g.shape, 0)
        mine = jnp.logical_and(row >= lo_ref[i], row < hi_ref[i])
        h = jnp.where(mine, g * jax.nn.sigmoid(g) * u, 0.0).astype(BF16)
        res = jnp.dot(h, wd_s[...], preferred_element_type=F32)

        @pl.when((flags // 2) % 2 == 1)
        def _():
            o_ref[...] = res

        @pl.when((flags // 2) % 2 == 0)
        def _():
            o_ref[...] += res


def _moe_experts(xb, meta, wg, wu, wd, l):
    n_items = meta[0].shape[0]
    wmap = lambda i, ti, ex, lo, hi, fl: (l, ex[i], 0, 0)
    xmap = lambda i, ti, ex, lo, hi, fl: (ti[i], 0)
    grid_spec = pltpu.PrefetchScalarGridSpec(
        num_scalar_prefetch=5,
        grid=(n_items,),
        in_specs=[pl.BlockSpec((MOE_ROWS, D_MODEL), xmap),
                  pl.BlockSpec((None, None, D_MODEL, D_EXPERT), wmap),
                  pl.BlockSpec((None, None, D_MODEL, D_EXPERT), wmap),
                  pl.BlockSpec((None, None, D_EXPERT, D_MODEL), wmap)],
        out_specs=pl.BlockSpec((MOE_ROWS, D_MODEL), xmap),
        scratch_shapes=[pltpu.VMEM((D_MODEL, D_EXPERT), BF16),
                        pltpu.VMEM((D_MODEL, D_EXPERT), BF16),
                        pltpu.VMEM((D_EXPERT, D_MODEL), BF16)])
    return pl.pallas_call(
        _moe_kernel,
        grid_spec=grid_spec,
        out_shape=jax.ShapeDtypeStruct(xb.shape, F32),
        compiler_params=_cparams("arbitrary"),
        name="moe_experts",
    )(*meta, xb, wg, wu, wd)


def _combine_kernel(dest_ref, yb_ref, x_ref, route_ref, mod_ref, fw_ref, o_ref, buf, sem, *, final):
    def issue(r, carry):
        for k in range(TOP_K):
            _row_copy(yb_ref, dest_ref[0, 0, TOP_K * r + k], buf.at[k], r, sem).start(priority=k)
        return carry

    lax.fori_loop(0, TILE, issue, 0, unroll=8)

    def drain(r, carry):
        for k in range(TOP_K):
            _row_copy(yb_ref, 0, buf.at[k], 0, sem).wait()
        return carry

    lax.fori_loop(0, TILE, drain, 0, unroll=8)
    route = route_ref[...]
    f = buf[0] * route[:, R_W0:R_W0 + 1] + buf[1] * route[:, R_W1:R_W1 + 1]
    y = x_ref[...] + mod_ref[0][5:6, :] * f
    o_ref[...] = _rms(y, fw_ref[...]) if final else y


def _combine(yb, dest3, x, route, modtab, final_w, latent_only):
    n_tok = route.shape[0]
    src = lambda i: _token_tile(i, latent_only)
    return pl.pallas_call(
        functools.partial(_combine_kernel, final=latent_only),
        grid=(n_tok // TILE,),
        in_specs=[pl.BlockSpec((1, 1, TOP_K * TILE), lambda i: (i, 0, 0), memory_space=pltpu.SMEM),
                  pl.BlockSpec(memory_space=pl.ANY),
                  pl.BlockSpec((TILE, D_MODEL), lambda i: (src(i), 0)),
                  pl.BlockSpec((TILE, ROUTER_COLS), lambda i: (i, 0)),
                  pl.BlockSpec((1, 6, D_MODEL), lambda i: (_tile_mod_row(src(i)), 0, 0)),
                  pl.BlockSpec((1, D_MODEL), lambda i: (0, 0))],
        out_specs=pl.BlockSpec((TILE, D_MODEL), lambda i: (i, 0)),
        out_shape=jax.ShapeDtypeStruct((n_tok, D_MODEL), F32),
        scratch_shapes=[pltpu.VMEM((TOP_K, TILE, D_MODEL), F32), pltpu.SemaphoreType.DMA(())],
        compiler_params=_cparams("arbitrary"),
        name="moe_combine",
    )(dest3, yb, x, route, modtab, final_w.reshape(1, D_MODEL))


def _moe_plan(route, cnt):
    t = route.shape[0]
    n_tiles = t * TOP_K // MOE_ROWS
    n_items = n_tiles + N_EXPERTS - 1
    experts = route[:, R_E0:R_E1 + 1].astype(jnp.int32)
    rank = route[:, R_K0:R_K1 + 1].astype(jnp.int32)
    counts = cnt[0, :N_EXPERTS].astype(jnp.int32)
    ends = jnp.cumsum(counts)
    starts = ends - counts
    dest = starts[experts] + rank
    t_first = starts // MOE_ROWS
    per_e = jnp.where(counts > 0, (ends - 1) // MOE_ROWS - t_first + 1, 0)
    item_end = jnp.cumsum(per_e)
    item_start = item_end - per_e
    total = item_end[-1]
    w = jnp.arange(n_items, dtype=jnp.int32)
    valid = w < total
    wc = jnp.minimum(w, total - 1)
    ex = jnp.clip(jnp.searchsorted(item_end, wc, side='right'), 0, N_EXPERTS - 1).astype(jnp.int32)
    tile = (t_first[ex] + wc - item_start[ex]).astype(jnp.int32)
    lo = jnp.where(valid, jnp.maximum(starts[ex], tile * MOE_ROWS) - tile * MOE_ROWS, 0).astype(jnp.int32)
    hi = jnp.where(valid, jnp.minimum(ends[ex], (tile + 1) * MOE_ROWS) - tile * MOE_ROWS, 0).astype(jnp.int32)
    one = jnp.ones((1,), bool)
    new_e = jnp.concatenate([one, ex[1:] != ex[:-1]])
    new_t = jnp.concatenate([one, tile[1:] != tile[:-1]])
    flags = (new_e.astype(jnp.int32) + 2 * new_t.astype(jnp.int32) + 4 * valid.astype(jnp.int32))
    dest3 = dest.reshape(t // TILE, 1, TOP_K * TILE)
    return dest3, (tile, ex, lo, hi, flags)


def _moe(x, h, route, cnt, modtab, wg, wu, wd, l, final_w, latent_only):
    dest3, meta = _moe_plan(route, cnt)
    xb = _dispatch(h, dest3)
    yb = _moe_experts(xb, meta, wg, wu, wd, l)
    return _combine(yb, dest3, x, route, modtab, final_w, latent_only)


def _scan_chunk(s, rev, chunk):
    n_c = CTX_LEN // chunk
    n_all = PB // chunk
    if not rev:
        return s
    return jnp.where(s < n_c, n_c - 1 - s, n_all + n_c - 1 - s)


HALO = 8


def _ssd_prep_kernel(cur_ref, prev_ref, next_ref, dt_ref, cw_ref, cb_ref, dtb_ref, xo_ref, dto_ref, ext_s):
    j = pl.program_id(0) % TPB
    first = jnp.logical_or(j == 0, j == 1)
    last = jnp.logical_or(j == 0, j == TPB - 1)
    ext_s[0:HALO, :] = jnp.where(first, 0.0, prev_ref[...])
    ext_s[HALO:HALO + TILE, :] = cur_ref[...]
    ext_s[HALO + TILE:, :] = jnp.where(last, 0.0, next_ref[...])
    half = (SSD_CONV - 1) // 2
    acc = cb_ref[...] + cw_ref[0:1, :] * ext_s[HALO - half:HALO - half + TILE, :]
    for k in range(1, SSD_CONV):
        acc = acc + cw_ref[k:k + 1, :] * ext_s[HALO - half + k:HALO - half + k + TILE, :]
    xo_ref[...] = acc * jax.nn.sigmoid(acc)
    lane = lax.broadcasted_iota(jnp.int32, (TILE, LANES), 1)
    dto_ref[...] = jnp.where(lane < 2 * H_B, jax.nn.softplus(dt_ref[...] + dtb_ref[...]), 0.0)


def _ssd_prep(z, conv_w, conv_b, dt_bias):
    n_tiles = T_ALL // TILE
    per = TILE // HALO
    cwp = jnp.concatenate([conv_w, jnp.zeros((8 - SSD_CONV, CONV_CH), F32)], axis=0)
    dtb = jnp.concatenate([dt_bias.reshape(1, 2 * H_B), jnp.zeros((1, LANES - 2 * H_B), F32)], axis=1)
    xc = Z_XBC // CONV_CH
    return pl.pallas_call(
        _ssd_prep_kernel,
        grid=(n_tiles,),
        in_specs=[pl.BlockSpec((TILE, CONV_CH), lambda i: (i, xc)),
                  pl.BlockSpec((HALO, CONV_CH), lambda i: (jnp.maximum(i * per - 1, 0), xc)),
                  pl.BlockSpec((HALO, CONV_CH), lambda i: (jnp.minimum((i + 1) * per, T_ALL // HALO - 1), xc)),
                  pl.BlockSpec((TILE, LANES), lambda i: (i, Z_DT // LANES)),
                  pl.BlockSpec((8, CONV_CH), lambda i: (0, 0)),
                  pl.BlockSpec((1, CONV_CH), lambda i: (0, 0)),
                  pl.BlockSpec((1, LANES), lambda i: (0, 0))],
        out_specs=[pl.BlockSpec((TILE, CONV_CH), lambda i: (i, 0)),
                   pl.BlockSpec((TILE, LANES), lambda i: (i, 0))],
        out_shape=[jax.ShapeDtypeStruct((T_ALL, CONV_CH), F32),
                   jax.ShapeDtypeStruct((T_ALL, LANES), F32)],
        scratch_shapes=[pltpu.VMEM((TILE + 2 * HALO, CONV_CH), F32)],
        compiler_params=_cparams("arbitrary"),
        name="ssd_prep",
    )(z, z, z, z, cwp, conv_b.reshape(1, CONV_CH), dtb)


def _ssd_one_direction(xbc, dtp, arow, tri, expand, h_ref, y_ref, b, d, rev):
    q = SSD_CHUNK
    a = dtp * arow
    acum = _dot3_left(tri, a)
    acum_t = acum.T
    dt_t = dtp.T
    edge = 0 if rev else q - 1
    atot = acum[edge:edge + 1, :]
    pieces = jnp.concatenate([jnp.exp(atot - acum) * dtp, jnp.exp(acum),
                              jnp.broadcast_to(jnp.exp(atot), (8, LANES))], axis=0)
    ex = _dot3_right(pieces, expand)
    wend_x = ex[0:q]
    eacum_x = ex[q:2 * q]
    dec_x = ex[2 * q:2 * q + 1]
    xs = xbc[:, 0:D_SSD]
    xw = (xs * wend_x).astype(BF16)
    xs_b = xs.astype(BF16)
    h_old = h_ref[b]
    h_b = h_old.astype(BF16)
    ri = lax.broadcasted_iota(jnp.int32, (q, q), 0)
    ci = lax.broadcasted_iota(jnp.int32, (q, q), 1)
    mask = (ci >= ri) if rev else (ci <= ri)
    lo_half = lax.broadcasted_iota(jnp.int32, (q, LANES), 1) < P_B
    hpg = H_B // G_B
    gw = hpg * P_B
    for g in range(G_B):
        bg = xbc[:, D_SSD + g * N_B:D_SSD + (g + 1) * N_B].astype(BF16)
        cg = xbc[:, D_SSD + G_B * N_B + g * N_B:D_SSD + G_B * N_B + (g + 1) * N_B].astype(BF16)
        cb = lax.dot_general(cg, bg, _NT, preferred_element_type=F32)
        inter = jnp.dot(cg, h_b[:, g * gw:(g + 1) * gw], preferred_element_type=F32)
        for j in range(hpg // 2):
            ms = []
            for hh in range(2):
                hc = H_B * d + hpg * g + 2 * j + hh
                seg = acum[:, hc:hc + 1] - acum_t[hc:hc + 1, :]
                dec = jnp.exp(jnp.where(mask, seg, NEG_BIG))
                ms.append((cb * dec * dt_t[hc:hc + 1, :]).astype(BF16))
            c0 = g * gw + 2 * j * P_B
            xp = xs_b[:, c0:c0 + LANES]
            zero = jnp.zeros_like(xp)
            rhs = jnp.concatenate([jnp.where(lo_half, xp, zero), jnp.where(lo_half, zero, xp)], axis=0)
            y_intra = jnp.dot(jnp.concatenate(ms, axis=1), rhs, preferred_element_type=F32)
            y_ref[b, :, c0:c0 + LANES] = (
                y_intra + eacum_x[:, c0:c0 + LANES] * inter[:, 2 * j * P_B:2 * j * P_B + LANES])
        upd = lax.dot_general(bg, xw[:, g * gw:(g + 1) * gw], _TN, preferred_element_type=F32)
        h_ref[b, :, g * gw:(g + 1) * gw] = dec_x[:, g * gw:(g + 1) * gw] * h_old[:, g * gw:(g + 1) * gw] + upd


def _ssd_scan_kernel(xf_ref, dtf_ref, xb_ref, dtb_ref, arow_ref, trif_ref, trib_ref, ef_ref, eb_ref,
                     yf_ref, yb_ref, hf_s, hb_s):
    @pl.when(pl.program_id(0) == 0)
    def _():
        hf_s[...] = jnp.zeros_like(hf_s)
        hb_s[...] = jnp.zeros_like(hb_s)

    for b in range(BATCH):
        _ssd_one_direction(xf_ref[b], dtf_ref[b], arow_ref[...], trif_ref[...], ef_ref[...], hf_s, yf_ref,
                           b, 0, False)
        _ssd_one_direction(xb_ref[b], dtb_ref[b], arow_ref[...], trib_ref[...], eb_ref[...], hb_s, yb_ref,
                           b, 1, True)


def _ssd_scan(xact, dtp, a_log):
    q = SSD_CHUNK
    steps = PB // q
    a_neg = -jnp.exp(a_log)
    arow = jnp.concatenate([a_neg.reshape(1, 2 * H_B), jnp.zeros((1, LANES - 2 * H_B), F32)], axis=1)
    r = jnp.arange(q)
    tri_f = (r[None, :] <= r[:, None]).astype(BF16)
    tri_b = (r[None, :] >= r[:, None]).astype(BF16)
    col_head = jnp.arange(D_SSD) // P_B
    lane = jnp.arange(LANES)
    exp_f = (lane[:, None] == col_head[None, :]).astype(BF16)
    exp_b = (lane[:, None] == col_head[None, :] + H_B).astype(BF16)
    fwd = lambda s: (0, _scan_chunk(s, False, q), 0)
    bwd = lambda s: (0, _scan_chunk(s, True, q), 0)
    const = lambda s: (0, 0)
    x3 = xact.reshape(BATCH, PB, CONV_CH)
    d3 = dtp.reshape(BATCH, PB, LANES)
    yf, yb = pl.pallas_call(
        _ssd_scan_kernel,
        grid=(steps,),
        in_specs=[pl.BlockSpec((BATCH, q, CONV_CH), fwd), pl.BlockSpec((BATCH, q, LANES), fwd),
                  pl.BlockSpec((BATCH, q, CONV_CH), bwd), pl.BlockSpec((BATCH, q, LANES), bwd),
                  pl.BlockSpec((1, LANES), const),
                  pl.BlockSpec((q, q), const), pl.BlockSpec((q, q), const),
                  pl.BlockSpec((LANES, D_SSD), const), pl.BlockSpec((LANES, D_SSD), const)],
        out_specs=[pl.BlockSpec((BATCH, q, D_SSD), fwd), pl.BlockSpec((BATCH, q, D_SSD), bwd)],
        out_shape=[jax.ShapeDtypeStruct((BATCH, PB, D_SSD), F32), jax.ShapeDtypeStruct((BATCH, PB, D_SSD), F32)],
        scratch_shapes=[pltpu.VMEM((BATCH, N_B, D_SSD), F32), pltpu.VMEM((BATCH, N_B, D_SSD), F32)],
        compiler_params=_cparams("arbitrary"),
        name="ssd_scan",
    )(x3, d3, x3, d3, arow, tri_f, tri_b, exp_f, exp_b)
    return yf.reshape(T_ALL, D_SSD), yb.reshape(T_ALL, D_SSD)


def _ssd_final_kernel(yf_ref, yb_ref, xs_ref, gate_ref, d_ref, onw_ref, o_ref):
    gate = gate_ref[...]
    y = (yf_ref[...] + yb_ref[...] + d_ref[...] * xs_ref[...]) * (gate * jax.nn.sigmoid(gate))
    o_ref[...] = (_rms(y) * onw_ref[...]).astype(o_ref.dtype)


def _ssd_final(yf, yb, xact, z, d_skip, onw):
    row = lambda i: (i, 0)
    const = lambda i: (0, 0)
    return pl.pallas_call(
        _ssd_final_kernel,
        grid=(T_ALL // MM_ROWS,),
        in_specs=[pl.BlockSpec((MM_ROWS, D_SSD), row), pl.BlockSpec((MM_ROWS, D_SSD), row),
                  pl.BlockSpec((MM_ROWS, D_SSD), row),
                  pl.BlockSpec((MM_ROWS, D_SSD), lambda i: (i, Z_SZ // D_SSD)),
                  pl.BlockSpec((1, D_SSD), const), pl.BlockSpec((1, D_SSD), const)],
        out_specs=pl.BlockSpec((MM_ROWS, D_SSD), row),
        out_shape=jax.ShapeDtypeStruct((T_ALL, D_SSD), BF16),
        compiler_params=_cparams("arbitrary"),
        name="ssd_final",
    )(yf, yb, xact, z, jnp.repeat(d_skip, P_B).reshape(1, D_SSD), onw.reshape(1, D_SSD))


def _ssd_mixer(z, p, onw):
    xact, dtp = _ssd_prep(z, p["ssd_conv_w"], p["ssd_conv_b"], p["ssd_dt_bias"])
    yf, yb = _ssd_scan(xact, dtp, p["ssd_a_log"])
    return _ssd_final(yf, yb, xact, z, p["ssd_d"], onw)


def _mlstm_one_direction(q, k, v, gi, gf, bi, bf, tri, st_ref, m_ref, h_ref, b, d, rev):
    n = MLSTM_CHUNK
    li = gi + bi
    lf = jax.nn.log_sigmoid(gf + bf)
    bc = _dot3_left(tri, lf)
    b_t = bc.T
    li_t = li.T
    edge = 0 if rev else n - 1
    gtot = bc[edge:edge + 1, :]
    m_old = m_ref[b, 0:1, :]
    w_log = gtot - bc + li
    m_new = jnp.maximum(gtot + m_old, jnp.max(w_log, axis=0, keepdims=True))
    wj = jnp.exp(w_log - m_new)
    dec = jnp.exp(gtot + m_old - m_new)
    inter_log = bc + m_old
    m_ref[b, 0:1, :] = m_new
    ri = lax.broadcasted_iota(jnp.int32, (n, n), 0)
    ci = lax.broadcasted_iota(jnp.int32, (n, n), 1)
    mask = (ci >= ri) if rev else (ci <= ri)
    lo_half = lax.broadcasted_iota(jnp.int32, (n, LANES), 1) < DQK_C
    row_lo = lax.broadcasted_iota(jnp.int32, (2 * DQK_C, 2 * DV_C), 0) < DQK_C
    ones = jnp.ones((n, DV_C), F32)
    for j in range(H_C // 2):
        qp = q[:, j * LANES:(j + 1) * LANES] * DQK_C ** -0.5
        kp = k[:, j * LANES:(j + 1) * LANES].astype(BF16)
        st_old = st_ref[b, j]
        st_b = st_old.astype(BF16)
        upds = []
        for hh in range(2):
            h = 2 * j + hh
            gl = GATE_LANE0 + H_C * d + h
            qm = jnp.where(lo_half if hh == 0 else jnp.logical_not(lo_half), qp, 0.0).astype(BF16)
            qk = lax.dot_general(qm, kp, _NT, preferred_element_type=F32)
            dmat = jnp.where(mask, bc[:, gl:gl + 1] - b_t[gl:gl + 1, :] + li_t[gl:gl + 1, :], NEG_BIG)
            il = inter_log[:, gl:gl + 1]
            m_row = jnp.maximum(il, jnp.max(dmat, axis=1, keepdims=True))
            s = qk * jnp.exp(dmat - m_row)
            w_inter = jnp.exp(il - m_row)
            vh = v[:, h * DV_C:(h + 1) * DV_C]
            qs = jnp.dot(qm, st_b, preferred_element_type=F32)
            num = jnp.dot(s.astype(BF16), vh.astype(BF16), preferred_element_type=F32) + w_inter * qs[:, :DV_C]
            den = jnp.sum(s, axis=1, keepdims=True) + w_inter * qs[:, DV_C:]
            h_ref[b, :, h * DV_C:(h + 1) * DV_C] = num / jnp.maximum(jnp.abs(den), jnp.exp(-m_row))
            rhs = (wj[:, gl:gl + 1] * jnp.concatenate([vh, ones], axis=1)).astype(BF16)
            upds.append(lax.dot_general(kp, rhs, _TN, preferred_element_type=F32))
        ga = GATE_LANE0 + H_C * d + 2 * j
        decv = jnp.where(row_lo, dec[:, ga:ga + 1], dec[:, ga + 1:ga + 2])
        st_ref[b, j] = decv * st_old + jnp.where(row_lo, upds[0], upds[1])


def _mlstm_scan_kernel(qf, kf, vf, gif, gff, qb, kb, vb, gib, gfb, bi_ref, bf_ref, trif_ref, trib_ref,
                       hf_ref, hb_ref, stf_s, stb_s, mf_s, mb_s):
    @pl.when(pl.program_id(0) == 0)
    def _():
        stf_s[...] = jnp.zeros_like(stf_s)
        stb_s[...] = jnp.zeros_like(stb_s)
        mf_s[...] = jnp.full_like(mf_s, NEG_STATE)
        mb_s[...] = jnp.full_like(mb_s, NEG_STATE)

    for b in range(BATCH):
        _mlstm_one_direction(qf[b], kf[b], vf[b], gif[b], gff[b], bi_ref[...], bf_ref[...], trif_ref[...],
                             stf_s, mf_s, hf_ref, b, 0, False)
        _mlstm_one_direction(qb[b], kb[b], vb[b], gib[b], gfb[b], bi_ref[...], bf_ref[...], trib_ref[...],
                             stb_s, mb_s, hb_ref, b, 1, True)


def _mlstm_scan(z, gate_b):
    n = MLSTM_CHUNK
    steps = PB // n
    qkw = H_C * DQK_C
    pad = lambda t: jnp.concatenate([jnp.zeros((1, GATE_LANE0), F32), t.reshape(1, 2 * H_C),
                                     jnp.zeros((1, LANES - GATE_LANE0 - 2 * H_C), F32)], axis=1)
    bi = pad(gate_b[:, 0, :])
    bf = pad(gate_b[:, 1, :])
    r = jnp.arange(n)
    tri_f = (r[None, :] <= r[:, None]).astype(BF16)
    tri_b = (r[None, :] >= r[:, None]).astype(BF16)
    z3 = z.reshape(BATCH, PB, Z_COLS)

    def specs(rev):
        ch = lambda s: _scan_chunk(s, rev, n)
        return [pl.BlockSpec((BATCH, n, qkw), lambda s: (0, ch(s), Z_MQ // qkw)),
                pl.BlockSpec((BATCH, n, qkw), lambda s: (0, ch(s), Z_MK // qkw)),
                pl.BlockSpec((BATCH, n, D_MLSTM), lambda s: (0, ch(s), Z_MV // D_MLSTM)),
                pl.BlockSpec((BATCH, n, LANES), lambda s: (0, ch(s), Z_MG // LANES)),
                pl.BlockSpec((BATCH, n, LANES), lambda s: (0, ch(s), Z_DT // LANES))]

    const = lambda s: (0, 0)
    n_pairs = H_C // 2
    hf, hb = pl.pallas_call(
        _mlstm_scan_kernel,
        grid=(steps,),
        in_specs=specs(False) + specs(True) + [
            pl.BlockSpec((1, LANES), const), pl.BlockSpec((1, LANES), const),
            pl.BlockSpec((n, n), const), pl.BlockSpec((n, n), const)],
        out_specs=[pl.BlockSpec((BATCH, n, D_MLSTM), lambda s: (0, _scan_chunk(s, False, n), 0)),
                   pl.BlockSpec((BATCH, n, D_MLSTM), lambda s: (0, _scan_chunk(s, True, n), 0))],
        out_shape=[jax.ShapeDtypeStruct((BATCH, PB, D_MLSTM), F32), jax.ShapeDtypeStruct((BATCH, PB, D_MLSTM), F32)],
        scratch_shapes=[pltpu.VMEM((BATCH, n_pairs, 2 * DQK_C, 2 * DV_C), F32),
                        pltpu.VMEM((BATCH, n_pairs, 2 * DQK_C, 2 * DV_C), F32),
                        pltpu.VMEM((BATCH, 8, LANES), F32), pltpu.VMEM((BATCH, 8, LANES), F32)],
        compiler_params=_cparams("arbitrary"),
        name="mlstm_scan",
    )(z3, z3, z3, z3, z3, z3, z3, z3, z3, z3, bi, bf, tri_f, tri_b)
    return hf.reshape(T_ALL, D_MLSTM), hb.reshape(T_ALL, D_MLSTM)


def _mlstm_final_kernel(hf_ref, hb_ref, og_ref, onw_ref, o_ref):
    gate = jax.nn.sigmoid(og_ref[...])
    onw = onw_ref[...]
    for h in range(H_C):
        cs = slice(h * DV_C, (h + 1) * DV_C)
        hn = _rms(hf_ref[:, cs] + hb_ref[:, cs])
        o_ref[:, cs] = (hn * gate[:, cs] * onw[:, cs]).astype(o_ref.dtype)


def _mlstm_final(hf, hb, z, onw):
    row = lambda i: (i, 0)
    return pl.pallas_call(
        _mlstm_final_kernel,
        grid=(T_ALL // MM_ROWS,),
        in_specs=[pl.BlockSpec((MM_ROWS, D_MLSTM), row), pl.BlockSpec((MM_ROWS, D_MLSTM), row),
                  pl.BlockSpec((MM_ROWS, D_MLSTM), lambda i: (i, Z_MO // D_MLSTM)),
                  pl.BlockSpec((1, D_MLSTM), lambda i: (0, 0))],
        out_specs=pl.BlockSpec((MM_ROWS, D_MLSTM), row),
        out_shape=jax.ShapeDtypeStruct((T_ALL, D_MLSTM), BF16),
        compiler_params=_cparams("arbitrary"),
        name="mlstm_final",
    )(hf, hb, z, onw.reshape(1, D_MLSTM))


def _mlstm_mixer(z, p, onw):
    hf, hb = _mlstm_scan(z, p["mlstm_gate_b"])
    return _mlstm_final(hf, hb, z, onw)


S5_Q = 256
S5_SEG = S5_Q // 8
S5_LANES = G_S5 * P_S5
S5_SLAB = 512
S5_NSLAB = S5_LANES // S5_SLAB


def _s5_dir_kernel(u_ref, perm_ref, wbr_ref, wbi_ref, atab_ref, apr_ref, api_ref, wcr_ref, wci_ref, y_ref,
                   ur_s, ui_s, xr_s, xi_s, st_s, car_s, *, reverse):
    @pl.when(pl.program_id(0) == 0)
    def _():
        st_s[...] = jnp.zeros_like(st_s)

    for b in range(BATCH):
        up = jnp.dot(perm_ref[...], u_ref[b].astype(BF16), preferred_element_type=F32).astype(BF16)
        for m in range(S5_NSLAB):
            um = up[:, m * LANES:(m + 1) * LANES]
            ur_s[b, :, m * S5_SLAB:(m + 1) * S5_SLAB] = jnp.dot(um, wbr_ref[m], preferred_element_type=F32)
            ui_s[b, :, m * S5_SLAB:(m + 1) * S5_SLAB] = jnp.dot(um, wbi_ref[m], preferred_element_type=F32)

    per = 4
    for grp in range(S5_LANES // (per * LANES)):
        cols = [grp * per * LANES + j * LANES for j in range(per)]
        a_r = [atab_ref[0:8, c0:c0 + LANES] for c0 in cols]
        a_i = [atab_ref[8:16, c0:c0 + LANES] for c0 in cols]

        def body(i, carry, cols=cols, a_r=a_r, a_i=a_i):
            t = (S5_SEG - 1 - i) if reverse else i
            r0 = pl.multiple_of(t * 8, 8)
            new = []
            for b in range(BATCH):
                for j, c0 in enumerate(cols):
                    xr, xi = carry[2 * (b * per + j)], carry[2 * (b * per + j) + 1]
                    nr = a_r[j] * xr - a_i[j] * xi + ur_s[b, pl.ds(r0, 8), c0:c0 + LANES]
                    ni = a_r[j] * xi + a_i[j] * xr + ui_s[b, pl.ds(r0, 8), c0:c0 + LANES]
                    ur_s[b, pl.ds(r0, 8), c0:c0 + LANES] = nr
                    ui_s[b, pl.ds(r0, 8), c0:c0 + LANES] = ni
                    new += [nr, ni]
            return tuple(new)

        lax.fori_loop(0, S5_SEG, body, tuple(jnp.zeros((8, LANES), F32) for _ in range(2 * per * BATCH)), unroll=2)

    as_r = atab_ref[16:17, :]
    as_i = atab_ref[17:18, :]
    end_row = 0 if reverse else 8 * (S5_SEG - 1)
    for b in range(BATCH):
        cr = st_s[b, 0:1, :]
        ci = st_s[b, 1:2, :]
        for k in (range(7, -1, -1) if reverse else range(8)):
            car_s[b, k:k + 1, :] = cr
            car_s[b, 8 + k:9 + k, :] = ci
            er = ur_s[b, end_row + k:end_row + k + 1, :]
            ei = ui_s[b, end_row + k:end_row + k + 1, :]
            cr, ci = er + as_r * cr - as_i * ci, ei + as_r * ci + as_i * cr
        st_s[b, 0:1, :] = cr
        st_s[b, 1:2, :] = ci

    for b in range(BATCH):
        for m in range(S5_NSLAB):
            cs = slice(m * S5_SLAB, (m + 1) * S5_SLAB)
            c_r = jnp.concatenate([car_s[b, 0:8, cs], car_s[b, 0:8, cs]], axis=0)
            c_i = jnp.concatenate([car_s[b, 8:16, cs], car_s[b, 8:16, cs]], axis=0)

            def fix(i, _, b=b, cs=cs, c_r=c_r, c_i=c_i):
                r0 = pl.multiple_of(i * 16, 16)
                p_r = apr_ref[pl.ds(r0, 16), cs]
                p_i = api_ref[pl.ds(r0, 16), cs]
                xr_s[b, pl.ds(r0, 16), cs] = (ur_s[b, pl.ds(r0, 16), cs] + p_r * c_r - p_i * c_i).astype(BF16)
                xi_s[b, pl.ds(r0, 16), cs] = (ui_s[b, pl.ds(r0, 16), cs] + p_r * c_i + p_i * c_r).astype(BF16)
                return 0

            lax.fori_loop(0, S5_Q // 16, fix, 0, unroll=2)

    for b in range(BATCH):
        for m in range(S5_NSLAB):
            cs = slice(m * S5_SLAB, (m + 1) * S5_SLAB)
            y_ref[b, :, m * LANES:(m + 1) * LANES] = (
                jnp.dot(xr_s[b, :, cs], wcr_ref[m], preferred_element_type=F32)
                - jnp.dot(xi_s[b, :, cs], wci_ref[m], preferred_element_type=F32))


def _s5_direction(z3, perm, wbr, wbi, atab, apr, api, wcr, wci, reverse):
    steps = PB // S5_Q
    chunk = lambda s: _scan_chunk(s, reverse, S5_Q)
    const2 = lambda s: (0, 0)
    const3 = lambda s: (0, 0, 0)
    return pl.pallas_call(
        functools.partial(_s5_dir_kernel, reverse=reverse),
        grid=(steps,),
        in_specs=[pl.BlockSpec((BATCH, S5_Q, D_S5), lambda s: (0, chunk(s), Z_D // D_S5)),
                  pl.BlockSpec((S5_Q, S5_Q), const2),
                  pl.BlockSpec(wbr.shape, const3),
                  pl.BlockSpec(wbi.shape, const3),
                  pl.BlockSpec(atab.shape, const2),
                  pl.BlockSpec(apr.shape, const2),
                  pl.BlockSpec(api.shape, const2),
                  pl.BlockSpec(wcr.shape, const3),
                  pl.BlockSpec(wci.shape, const3)],
        out_specs=pl.BlockSpec((BATCH, S5_Q, D_S5), lambda s: (0, chunk(s), 0)),
        out_shape=jax.ShapeDtypeStruct((BATCH, PB, D_S5), F32),
        scratch_shapes=[pltpu.VMEM((BATCH, S5_Q, S5_LANES), F32), pltpu.VMEM((BATCH, S5_Q, S5_LANES), F32),
                        pltpu.VMEM((BATCH, S5_Q, S5_LANES), BF16), pltpu.VMEM((BATCH, S5_Q, S5_LANES), BF16),
                        pltpu.VMEM((BATCH, 8, S5_LANES), F32), pltpu.VMEM((BATCH, 16, S5_LANES), F32)],
        compiler_params=_cparams("arbitrary"),
        name="s5_bwd" if reverse else "s5_fwd",
    )(z3, perm, wbr, wbi, atab, apr, api, wcr, wci).reshape(T_ALL, D_S5)


def _s5_final_kernel(yf_ref, yb_ref, u_ref, permt_ref, d_ref, gw_ref, gb_ref, onw_ref, o_ref):
    y = _dot3_left(permt_ref[...], yf_ref[...] + yb_ref[...])
    y = jax.nn.gelu(y + d_ref[...] * u_ref[...])
    gate = jax.nn.sigmoid(jnp.dot(y.astype(BF16), gw_ref[...], preferred_element_type=F32) + gb_ref[...])
    o_ref[...] = (_rms(y * gate) * onw_ref[...]).astype(o_ref.dtype)


def _s5_final(yf, yb, z, permt, d_skip, glu_w, glu_b, onw):
    row = lambda i: (i, 0)
    const = lambda i: (0, 0)
    return pl.pallas_call(
        _s5_final_kernel,
        grid=(T_ALL // S5_Q,),
        in_specs=[pl.BlockSpec((S5_Q, D_S5), row),
                  pl.BlockSpec((S5_Q, D_S5), row),
                  pl.BlockSpec((S5_Q, D_S5), lambda i: (i, Z_D // D_S5)),
                  pl.BlockSpec((S5_Q, S5_Q), const),
                  pl.BlockSpec((1, D_S5), const),
                  pl.BlockSpec((D_S5, D_S5), const),
                  pl.BlockSpec((1, D_S5), const),
                  pl.BlockSpec((1, D_S5), const)],
        out_specs=pl.BlockSpec((S5_Q, D_S5), row),
        out_shape=jax.ShapeDtypeStruct((T_ALL, D_S5), BF16),
        compiler_params=_cparams("arbitrary"),
        name="s5_final",
    )(yf, yb, z, permt, d_skip.reshape(1, D_S5), glu_w.astype(BF16), glu_b.reshape(1, D_S5), onw.reshape(1, D_S5))


def _s5_tables(lam_re, lam_im, log_dt, b_re, b_im, c_re, c_im):
    dt = jnp.exp(log_dt)[..., None]
    mag = jnp.exp(lam_re * dt)
    ar = mag * jnp.cos(lam_im * dt)
    ai = mag * jnp.sin(lam_im * dt)
    den = lam_re * lam_re + lam_im * lam_im
    cr_ = ((ar - 1.0) * lam_re + ai * lam_im) / den
    ci_ = (ai * lam_re - (ar - 1.0) * lam_im) / den
    bbr = cr_[..., None] * b_re - ci_[..., None] * b_im
    bbi = cr_[..., None] * b_im + ci_[..., None] * b_re
    gps = S5_SLAB // P_S5
    eye = jnp.eye(gps, dtype=F32)

    def drive_w(bb):
        t = bb.reshape(S5_NSLAB, gps, P_S5, S5_GROUP)
        w = jnp.einsum('mgpc,gh->mgchp', t, eye)
        return w.reshape(S5_NSLAB, gps * S5_GROUP, gps * P_S5).astype(BF16)

    def read_w(cc):
        t = cc.reshape(S5_NSLAB, gps, S5_GROUP, P_S5)
        w = jnp.einsum('mgcp,gh->mgphc', t, eye)
        return w.reshape(S5_NSLAB, gps * P_S5, gps * S5_GROUP).astype(BF16)

    out = []
    for d in range(2):
        a_r = ar[d].reshape(1, S5_LANES)
        a_i = ai[d].reshape(1, S5_LANES)
        pr, pi = [a_r], [a_i]
        for _ in range(S5_SEG - 1):
            pr, pi = pr + [pr[-1] * a_r - pi[-1] * a_i], pi + [pr[-1] * a_i + pi[-1] * a_r]
        order = list(range(S5_SEG))
        if d == 1:
            order = order[::-1]
        apr = jnp.concatenate([jnp.broadcast_to(pr[k], (8, S5_LANES)) for k in order], axis=0)
        api = jnp.concatenate([jnp.broadcast_to(pi[k], (8, S5_LANES)) for k in order], axis=0)
        atab = jnp.concatenate([jnp.broadcast_to(pr[0], (8, S5_LANES)), jnp.broadcast_to(pi[0], (8, S5_LANES)),
                                pr[-1], pi[-1], jnp.zeros((6, S5_LANES), F32)], axis=0)
        out.append((drive_w(bbr[d]), drive_w(bbi[d]), atab, apr, api))
    r = jnp.arange(S5_Q)
    src = (r % 8) * S5_SEG + r // 8
    perm = (src[:, None] == jnp.arange(S5_Q)[None, :]).astype(BF16)
    return out, read_w(c_re), read_w(c_im), perm


def _s5_mixer(z, p, onw):
    dirs, wcr, wci, perm = _s5_tables(p["s5_lam_re"], p["s5_lam_im"], p["s5_log_dt"], p["s5_b_re"], p["s5_b_im"],
                                      p["s5_c_re"], p["s5_c_im"])
    z3 = z.reshape(BATCH, PB, Z_COLS)
    yf = _s5_direction(z3, perm, *dirs[0], wcr, wci, reverse=False)
    yb = _s5_direction(z3, perm, *dirs[1], wcr, wci, reverse=True)
    return _s5_final(yf, yb, z, perm.T, p["s5_d"], p["s5_glu_w"], p["s5_glu_b"], onw)


def _rope_tables():
    pos = jnp.arange(SEQ)
    row = (pos // GRID_W).astype(F32)
    col = (pos % GRID_W).astype(F32)
    inv_freq = ROPE_BASE ** (-jnp.arange(ROPE_AXIS // 2, dtype=F32) * 2.0 / ROPE_AXIS)
    ang_r = row[:, None] * inv_freq
    ang_c = col[:, None] * inv_freq
    zeros = jnp.zeros((SEQ, LANES - D_ROPE), F32)
    cos = jnp.concatenate([jnp.cos(ang_r), jnp.cos(ang_r), jnp.cos(ang_c), jnp.cos(ang_c), zeros], axis=1)
    sin = jnp.concatenate([jnp.sin(ang_r), jnp.sin(ang_r), jnp.sin(ang_c), jnp.sin(ang_c), zeros], axis=1)
    cos_c = jnp.concatenate([jnp.ones((TILE, D_ROPE), F32), jnp.zeros((TILE, LANES - D_ROPE), F32)], axis=1)
    sin_c = jnp.zeros((TILE, LANES), F32)
    return jnp.concatenate([jnp.concatenate([cos, sin], axis=1), jnp.concatenate([cos_c, sin_c], axis=1)], axis=0)


def _layout_mla(w_uq, w_ukv):
    k = w_uq.shape[0]
    qa, qb, wk, wv = [], [], [], []
    for h in range(H_A):
        base = h * (D_NOPE + D_ROPE)
        rope = w_uq[:, base + D_NOPE:base + D_NOPE + D_ROPE]
        qa += [w_uq[:, base:base + D_NOPE], rope, jnp.zeros((k, LANES - D_ROPE), w_uq.dtype)]
        qb += [_rot_cols(rope), jnp.zeros((k, LANES - D_ROPE), w_uq.dtype)]
        kb = h * (D_NOPE + D_V)
        wk.append(w_ukv[:, kb:kb + D_NOPE])
        wv.append(w_ukv[:, kb + D_NOPE:kb + D_NOPE + D_V])
    cat = lambda xs: jnp.concatenate(xs, axis=1).astype(BF16)
    return cat(qa), cat(qb), cat(wk), cat(wv)


def _layer(xall, modtab, p, big, l, cs, final_w, last):
    hx = _prenorm(xall, p["norm1_w"], modtab, 0, 1)
    z = _mm(hx, _layout_w_in(big["w_in"], l), MM_ROWS, Z_COLS // 3)
    onw = p["out_norm_w"]
    wqa, wqb, wk, wv = _layout_mla(p["mla_w_uq"], p["mla_w_ukv"])
    q, k, v = _mla_prep(z, cs, p["mla_q_norm_w"], p["mla_kv_norm_w"], wqa, wqb, wk, wv)
    ya = _attention(q, k, v, onw[:D_MLA].reshape(1, D_MLA))
    yb = _ssd_mixer(z, p, onw[D_MLA:D_MLA + D_SSD])
    yc = _mlstm_mixer(z, p, onw[D_MLA + D_SSD:D_MLA + D_SSD + D_MLSTM])
    yd = _s5_mixer(z, p, onw[D_MLA + D_SSD + D_MLSTM:])
    xall = _outproj([ya, yb, yc, yd], p["w_out"].astype(BF16), xall, modtab, 2)

    w_router = jnp.concatenate([p["moe_w_group"], p["moe_w_expert"],
                                jnp.zeros((D_MODEL, ROUTER_COLS - N_GROUPS - N_EXPERTS), F32)], axis=1)
    h2, route, cnt = _prenorm_router(xall, p["norm2_w"], modtab, w_router, 3, 4, last)
    return _moe(xall, h2, route, cnt, modtab, big["moe_w_gate"], big["moe_w_up"], big["moe_w_down"], l,
                final_w, last)


def kernel(x, c, ctx, c_ctx, mod_w, mod_b, norm1_w, w_in, mla_q_norm_w, mla_kv_norm_w, mla_w_uq, mla_w_ukv,
           ssd_conv_w, ssd_conv_b, ssd_a_log, ssd_dt_bias, ssd_d, mlstm_gate_b, s5_lam_re, s5_lam_im,
           s5_log_dt, s5_b_re, s5_b_im, s5_c_re, s5_c_im, s5_d, s5_glu_w, s5_glu_b, out_norm_w, w_out,
           norm2_w, moe_w_group, moe_w_expert, moe_w_gate, moe_w_up, moe_w_down, final_norm_w):
    stacked = {"norm1_w": norm1_w, "mla_q_norm_w": mla_q_norm_w, "mla_kv_norm_w": mla_kv_norm_w,
               "mla_w_uq": mla_w_uq, "mla_w_ukv": mla_w_ukv, "ssd_conv_w": ssd_conv_w, "ssd_conv_b": ssd_conv_b,
               "ssd_a_log": ssd_a_log, "ssd_dt_bias": ssd_dt_bias, "ssd_d": ssd_d, "mlstm_gate_b": mlstm_gate_b,
               "s5_lam_re": s5_lam_re, "s5_lam_im": s5_lam_im, "s5_log_dt": s5_log_dt, "s5_b_re": s5_b_re,
               "s5_b_im": s5_b_im, "s5_c_re": s5_c_re, "s5_c_im": s5_c_im, "s5_d": s5_d, "s5_glu_w": s5_glu_w,
               "s5_glu_b": s5_glu_b, "out_norm_w": out_norm_w, "w_out": w_out, "norm2_w": norm2_w,
               "moe_w_group": moe_w_group, "moe_w_expert": moe_w_expert}
    big = {"w_in": w_in, "moe_w_gate": moe_w_gate, "moe_w_up": moe_w_up, "moe_w_down": moe_w_down}
    cs = _rope_tables()
    cc = jnp.concatenate([c, c_ctx[None, :], jnp.zeros((8 - BATCH - 1, D_MODEL), F32)], axis=0)
    xall = jnp.concatenate([ctx, x], axis=1).reshape(T_ALL, D_MODEL)
    for l in range(DEPTH):
        p = {name: val[l] for name, val in stacked.items()}
        modtab = _modulation(cc, mod_w, mod_b, l)[:BATCH + 1].reshape(BATCH + 1, 6, D_MODEL)
        xall = _layer(xall, modtab, p, big, l, cs, final_norm_w, l == DEPTH - 1)
    return xall.reshape(BATCH, SEQ, D_MODEL)
```

```python
import functools
import math

import jax
import jax.numpy as jnp
import numpy as np
from jax import lax
from jax.experimental import pallas as pl
from jax.experimental.pallas import tpu as pltpu

F32 = jnp.float32
BF16 = jnp.bfloat16

D_MODEL = 2048
BATCH = 2
SEQ = 4096
DEPTH = 2
GRID_W = 64
CTX_LEN = 256
EPS = 1e-6
NEG_STATE = -1e30
NEG_BIG = -1e30

H_A = 4
D_NOPE = 128
D_ROPE = 64
D_V = 128
Q_RANK = 384
KV_RANK = 128
ROPE_AXIS = D_ROPE // 2
ROPE_BASE = 10000.0
D_MLA = H_A * D_V
D_SSD = 512
P_B = 64
H_B = D_SSD // P_B
G_B = 2
N_B = 128
SSD_CONV = 5
SSD_CHUNK = 128
CONV_CH = D_SSD + 2 * G_B * N_B
D_MLSTM = 512
H_C = 4
DV_C = D_MLSTM // H_C
DQK_C = DV_C // 2
MLSTM_CHUNK = 128
D_S5 = 512
S5_GROUP = 16
G_S5 = D_S5 // S5_GROUP
P_S5 = 64
A_COLS = Q_RANK + KV_RANK + D_ROPE
B_COLS = D_SSD + CONV_CH + 2 * H_B
C_COLS = 2 * H_C * DQK_C + 2 * D_MLSTM + 4 * H_C
N_GROUPS = 4
EXPERTS_PER_GROUP = 8
N_EXPERTS = N_GROUPS * EXPERTS_PER_GROUP
TOP_K = 2
D_EXPERT = 512

PB = CTX_LEN + SEQ
T_X = BATCH * SEQ
T_ALL = BATCH * PB

LANES = 128
VMEM_LIMIT_BYTES = 56 * 1024 * 1024

TILE = 256
TPB = PB // TILE
XT = SEQ // TILE
MM_ROWS = 512

Z_CQ = 0
Z_CKV = 384
Z_KR = 512
Z_KRR = 640
Z_MQ = 768
Z_XBC = 1024
Z_SZ = 2048
Z_MV = 2560
Z_MO = 3072
Z_D = 3584
Z_MK = 4096
Z_DT = 4352
Z_MG = 4480
Z_COLS = 4608
GATE_LANE0 = 2 * H_B

MOE_ROWS = 256
ROUTER_COLS = 128

_NT = (((1,), (1,)), ((), ()))
_TN = (((0,), (0,)), ((), ()))


def _cparams(*sem):
    return pltpu.CompilerParams(dimension_semantics=sem, vmem_limit_bytes=VMEM_LIMIT_BYTES)


def _tile_mod_row(i):
    return jnp.where(i % TPB == 0, BATCH, i // TPB)


def _token_tile(i, latent_only):
    return (i // XT) * TPB + 1 + i % XT if latent_only else i


def _rms(x, w=None):
    y = x * lax.rsqrt(jnp.mean(x * x, axis=-1, keepdims=True) + EPS)
    return y if w is None else y * w


def _in_turn(chains):
    live = list(chains)
    while live:
        nxt = []
        for chain in live:
            try:
                next(chain)
                nxt.append(chain)
            except StopIteration:
                pass
        live = nxt
        yield


def _round_robin(chains):
    for _ in _in_turn(chains):
        pass


def _split3(x):
    hi = x.astype(BF16)
    r1 = x - hi.astype(F32)
    mid = r1.astype(BF16)
    lo = (r1 - mid.astype(F32)).astype(BF16)
    return hi, mid, lo


def _dot3_left(sel, x):
    hi, mid, lo = _split3(x)
    return (jnp.dot(sel, hi, preferred_element_type=F32) + jnp.dot(sel, mid, preferred_element_type=F32)
            + jnp.dot(sel, lo, preferred_element_type=F32))


def _dot3_right(x, sel):
    hi, mid, lo = _split3(x)
    return (jnp.dot(hi, sel, preferred_element_type=F32) + jnp.dot(mid, sel, preferred_element_type=F32)
            + jnp.dot(lo, sel, preferred_element_type=F32))


def _mod_kernel(a_ref, w_ref, b_ref, o_ref):
    a = a_ref[...]
    a = a * jax.nn.sigmoid(a)
    o_ref[...] = jnp.dot(a.astype(BF16), w_ref[...].astype(BF16), preferred_element_type=F32) + b_ref[...]


def _modulation(cc, mod_w, mod_b, l):
    n = mod_w.shape[2]
    tn = 1024
    return pl.pallas_call(
        _mod_kernel,
        grid=(n // tn,),
        in_specs=[pl.BlockSpec((8, D_MODEL), lambda j: (0, 0)),
                  pl.BlockSpec((None, D_MODEL, tn), lambda j: (l, 0, j)),
                  pl.BlockSpec((None, 1, tn), lambda j: (l, 0, j))],
        out_specs=pl.BlockSpec((8, tn), lambda j: (0, j)),
        out_shape=jax.ShapeDtypeStruct((8, n), F32),
        compiler_params=_cparams("arbitrary"),
        name="modulation",
    )(cc, mod_w, mod_b.reshape(DEPTH, 1, n))


def _prenorm_kernel(x_ref, w_ref, mod_ref, o_ref, *, sh_idx, sc_idx):
    y = _rms(x_ref[...], w_ref[...])
    m = mod_ref[0]
    y = y * (1.0 + m[sc_idx:sc_idx + 1, :]) + m[sh_idx:sh_idx + 1, :]
    o_ref[...] = y.astype(o_ref.dtype)


def _prenorm(x, w, modtab, sh_idx, sc_idx):
    rows = x.shape[0]
    return pl.pallas_call(
        functools.partial(_prenorm_kernel, sh_idx=sh_idx, sc_idx=sc_idx),
        grid=(rows // TILE,),
        in_specs=[pl.BlockSpec((TILE, D_MODEL), lambda i: (i, 0)),
                  pl.BlockSpec((1, D_MODEL), lambda i: (0, 0)),
                  pl.BlockSpec((1, 6, D_MODEL), lambda i: (_tile_mod_row(i), 0, 0))],
        out_specs=pl.BlockSpec((TILE, D_MODEL), lambda i: (i, 0)),
        out_shape=jax.ShapeDtypeStruct((rows, D_MODEL), BF16),
        compiler_params=_cparams("arbitrary"),
        name="prenorm",
    )(x, w.reshape(1, D_MODEL), modtab)


def _mm_kernel(a_ref, w_ref, o_ref):
    o_ref[...] = jnp.dot(a_ref[...], w_ref[...], preferred_element_type=F32).astype(o_ref.dtype)


def _mm(a, w, tm, tn, out_dtype=F32):
    m, k = a.shape
    n = w.shape[1]
    return pl.pallas_call(
        _mm_kernel,
        grid=(n // tn, m // tm),
        in_specs=[pl.BlockSpec((tm, k), lambda j, i: (i, 0)),
                  pl.BlockSpec((k, tn), lambda j, i: (0, j))],
        out_specs=pl.BlockSpec((tm, tn), lambda j, i: (i, j)),
        out_shape=jax.ShapeDtypeStruct((m, n), out_dtype),
        compiler_params=_cparams("arbitrary", "arbitrary"),
        name="in_proj",
    )(a, w)


W_IN_B0 = A_COLS
W_IN_C0 = A_COLS + B_COLS
W_IN_D0 = A_COLS + B_COLS + C_COLS
W_IN_CM = W_IN_C0 + 2 * H_C * DQK_C
W_IN_GB = W_IN_CM + 2 * D_MLSTM
W_IN_WIDE = ((Z_CQ, 0, Q_RANK + KV_RANK),
             (Z_MQ, W_IN_C0, H_C * DQK_C),
             (Z_XBC, W_IN_B0 + D_SSD, CONV_CH),
             (Z_SZ, W_IN_B0, D_SSD),
             (Z_MV, W_IN_CM, 2 * D_MLSTM),
             (Z_D, W_IN_D0, D_S5),
             (Z_MK, W_IN_C0 + H_C * DQK_C, H_C * DQK_C))
W_IN_ROWS = 256


def _rot_cols(w):
    q = ROPE_AXIS // 2
    return jnp.concatenate([-w[:, q:2 * q], w[:, 0:q], -w[:, 3 * q:4 * q], w[:, 2 * q:3 * q]], axis=1)


def _w_in_layout_kernel(w_ref, small_ref, o_ref):
    for dst, src, width in W_IN_WIDE:
        o_ref[:, dst:dst + width] = w_ref[:, src:src + width].astype(BF16)
    o_ref[:, Z_KR:Z_MQ] = small_ref[:, 0:2 * LANES]
    o_ref[:, Z_DT:Z_COLS] = small_ref[:, 2 * LANES:4 * LANES]


def _layout_w_in(w_in, l):
    _, k, n = w_in.shape
    w = w_in[l]
    zpad = lambda c: jnp.zeros((k, c), F32)
    kr = w[:, Q_RANK + KV_RANK:A_COLS]
    gb = W_IN_GB
    small = jnp.concatenate([
        kr, zpad(LANES - D_ROPE), _rot_cols(kr), zpad(LANES - D_ROPE),
        w[:, W_IN_B0 + D_SSD + CONV_CH:W_IN_C0], w[:, gb + H_C:gb + 2 * H_C], w[:, gb + 3 * H_C:gb + 4 * H_C],
        zpad(LANES - GATE_LANE0 - 2 * H_C),
        zpad(GATE_LANE0), w[:, gb:gb + H_C], w[:, gb + 2 * H_C:gb + 3 * H_C],
        zpad(LANES - GATE_LANE0 - 2 * H_C)], axis=1).astype(BF16)
    return pl.pallas_call(
        _w_in_layout_kernel,
        grid=(k // W_IN_ROWS,),
        in_specs=[pl.BlockSpec((None, W_IN_ROWS, n), lambda i: (l, i, 0)),
                  pl.BlockSpec((W_IN_ROWS, 4 * LANES), lambda i: (i, 0))],
        out_specs=pl.BlockSpec((W_IN_ROWS, Z_COLS), lambda i: (i, 0)),
        out_shape=jax.ShapeDtypeStruct((k, Z_COLS), BF16),
        compiler_params=_cparams("arbitrary"),
        name="w_in_layout",
    )(w_in, small)


def _mla_prep_kernel(za_ref, cs_ref, qw_ref, kvw_ref, wqa_ref, wqb_ref, wk_ref, wv_ref, q_ref, k_ref, v_ref):
    za = za_ref[...]
    cos = cs_ref[:, :LANES]
    sin = cs_ref[:, LANES:]
    qn = _rms(za[:, Z_CQ:Z_CQ + Q_RANK], qw_ref[...]).astype(BF16)
    kvn = _rms(za[:, Z_CKV:Z_CKV + KV_RANK], kvw_ref[...]).astype(BF16)
    qa = jnp.dot(qn, wqa_ref[...], preferred_element_type=F32)
    qb = jnp.dot(qn, wqb_ref[...], preferred_element_type=F32)
    kn = jnp.dot(kvn, wk_ref[...], preferred_element_type=F32)
    v = jnp.dot(kvn, wv_ref[...], preferred_element_type=F32)
    kr = (za[:, Z_KR:Z_KR + LANES] * cos + za[:, Z_KRR:Z_KRR + LANES] * sin).astype(BF16)
    for h in range(H_A):
        c0 = h * 2 * LANES
        q_ref[:, c0:c0 + LANES] = qa[:, c0:c0 + LANES].astype(BF16)
        q_ref[:, c0 + LANES:c0 + 2 * LANES] = (
            qa[:, c0 + LANES:c0 + 2 * LANES] * cos + qb[:, h * LANES:(h + 1) * LANES] * sin).astype(BF16)
        k_ref[:, c0:c0 + LANES] = kn[:, h * LANES:(h + 1) * LANES].astype(BF16)
        k_ref[:, c0 + LANES:c0 + 2 * LANES] = kr
    v_ref[...] = v.astype(BF16)


ATT_W = H_A * 2 * LANES


def _mla_prep(z, cs, qw, kvw, wqa, wqb, wk, wv):
    const = lambda i: (0, 0)
    rope_blk = lambda i: (jnp.where(i % TPB == 0, XT, i % TPB - 1), 0)
    return pl.pallas_call(
        _mla_prep_kernel,
        grid=(T_ALL // TILE,),
        in_specs=[pl.BlockSpec((TILE, Z_MQ), lambda i: (i, 0)),
                  pl.BlockSpec((TILE, 2 * LANES), rope_blk),
                  pl.BlockSpec((1, Q_RANK), const),
                  pl.BlockSpec((1, KV_RANK), const),
                  pl.BlockSpec(wqa.shape, const),
                  pl.BlockSpec(wqb.shape, const),
                  pl.BlockSpec(wk.shape, const),
                  pl.BlockSpec(wv.shape, const)],
        out_specs=[pl.BlockSpec((TILE, ATT_W), lambda i: (i, 0)),
                   pl.BlockSpec((TILE, ATT_W), lambda i: (i, 0)),
                   pl.BlockSpec((TILE, D_MLA), lambda i: (i, 0))],
        out_shape=[jax.ShapeDtypeStruct((T_ALL, ATT_W), BF16),
                   jax.ShapeDtypeStruct((T_ALL, ATT_W), BF16),
                   jax.ShapeDtypeStruct((T_ALL, D_MLA), BF16)],
        compiler_params=_cparams("arbitrary"),
        name="mla_prep",
    )(z, cs, qw.reshape(1, Q_RANK), kvw.reshape(1, KV_RANK), wqa, wqb, wk, wv)


def _attn_tile(q_ref, k_ref, v_ref, w_ref, o_ref, acc_ref, n_keys):
    scale2 = (D_NOPE + D_ROPE) ** -0.5 * math.log2(math.e)
    for h in range(H_A):
        q = q_ref[:, h * 2 * LANES:(h + 1) * 2 * LANES]
        s = lax.dot_general(q, k_ref[0:n_keys, h * 2 * LANES:(h + 1) * 2 * LANES], _NT, preferred_element_type=F32)
        m = jnp.max(s, axis=-1, keepdims=True)
        p = jnp.exp2((s - m) * scale2)
        l = jnp.sum(p, axis=-1, keepdims=True)
        o = jnp.dot(p.astype(BF16), v_ref[0:n_keys, h * D_V:(h + 1) * D_V], preferred_element_type=F32)
        acc_ref[:, h * D_V:(h + 1) * D_V] = o / l
    o_ref[...] = (_rms(acc_ref[...]) * w_ref[...]).astype(o_ref.dtype)


def _attn_kernel(q_ref, k_ref, v_ref, w_ref, o_ref, acc_ref):
    @pl.when(pl.program_id(1) == 0)
    def _():
        _attn_tile(q_ref, k_ref, v_ref, w_ref, o_ref, acc_ref, CTX_LEN)

    @pl.when(pl.program_id(1) != 0)
    def _():
        _attn_tile(q_ref, k_ref, v_ref, w_ref, o_ref, acc_ref, PB)


def _attention(q, k, v, onw):
    return pl.pallas_call(
        _attn_kernel,
        grid=(BATCH, TPB),
        in_specs=[pl.BlockSpec((TILE, ATT_W), lambda b, i: (b * TPB + i, 0)),
                  pl.BlockSpec((None, PB, ATT_W), lambda b, i: (b, 0, 0)),
                  pl.BlockSpec((None, PB, D_MLA), lambda b, i: (b, 0, 0)),
                  pl.BlockSpec((1, D_MLA), lambda b, i: (0, 0))],
        out_specs=pl.BlockSpec((TILE, D_MLA), lambda b, i: (b * TPB + i, 0)),
        out_shape=jax.ShapeDtypeStruct((T_ALL, D_MLA), BF16),
        scratch_shapes=[pltpu.VMEM((TILE, D_MLA), F32)],
        compiler_params=_cparams("arbitrary", "arbitrary"),
        name="attention",
    )(q, k.reshape(BATCH, PB, ATT_W), v.reshape(BATCH, PB, D_MLA), onw)


def _outproj_kernel(a0, a1, a2, a3, w0, w1, w2, w3, x_ref, mod_ref, o_ref, *, g_idx):
    acc = jnp.dot(a0[...], w0[...], preferred_element_type=F32)
    acc += jnp.dot(a1[...], w1[...], preferred_element_type=F32)
    acc += jnp.dot(a2[...], w2[...], preferred_element_type=F32)
    acc += jnp.dot(a3[...], w3[...], preferred_element_type=F32)
    g = mod_ref[0][g_idx:g_idx + 1, :]
    o_ref[...] = x_ref[...] + g * acc


def _outproj(ys, w, x, modtab, g_idx):
    kq = D_MODEL // 4
    a_specs = [pl.BlockSpec((TILE, kq), lambda i: (i, 0)) for _ in range(4)]
    w_specs = [pl.BlockSpec((kq, D_MODEL), functools.partial(lambda i, r: (r, 0), r=r)) for r in range(4)]
    return pl.pallas_call(
        functools.partial(_outproj_kernel, g_idx=g_idx),
        grid=(T_ALL // TILE,),
        in_specs=a_specs + w_specs + [
            pl.BlockSpec((TILE, D_MODEL), lambda i: (i, 0)),
            pl.BlockSpec((1, 6, D_MODEL), lambda i: (_tile_mod_row(i), 0, 0))],
        out_specs=pl.BlockSpec((TILE, D_MODEL), lambda i: (i, 0)),
        out_shape=jax.ShapeDtypeStruct((T_ALL, D_MODEL), F32),
        compiler_params=_cparams("arbitrary"),
        name="out_proj",
    )(*ys, w, w, w, w, x, modtab)


R_E0, R_E1, R_W0, R_W1, R_K0, R_K1 = range(6)
NO_LANE = 2 * LANES


def _prenorm_router_kernel(x_ref, w_ref, mod_ref, wr_ref, tri_ref, h_ref, route_ref, cnt_ref, cnt_s,
                           *, sh_idx, sc_idx):
    @pl.when(pl.program_id(0) == 0)
    def _():
        cnt_s[...] = jnp.zeros_like(cnt_s)

    y = _rms(x_ref[...], w_ref[...])
    m = mod_ref[0]
    y = y * (1.0 + m[sc_idx:sc_idx + 1, :]) + m[sh_idx:sh_idx + 1, :]
    h_ref[...] = y
    y_hi = y.astype(BF16)
    y_lo = (y - y_hi.astype(F32)).astype(BF16)
    lg = (jnp.dot(y_hi, wr_ref[0], preferred_element_type=F32) + jnp.dot(y_lo, wr_ref[0], preferred_element_type=F32)
          + jnp.dot(y_hi, wr_ref[1], preferred_element_type=F32))
    lane = lax.broadcasted_iota(jnp.int32, lg.shape, 1)

    def first_max(v):
        top = jnp.max(v, axis=1, keepdims=True)
        return top, jnp.min(jnp.where(v == top, lane, NO_LANE), axis=1, keepdims=True)

    is_g = lane < N_GROUPS
    g_top, g_idx = first_max(jnp.where(is_g, lg, NEG_BIG))
    g_w = 1.0 / jnp.sum(jnp.where(is_g, jnp.exp(lg - g_top), 0.0), axis=1, keepdims=True)
    lo = N_GROUPS + EXPERTS_PER_GROUP * g_idx
    el = jnp.where(jnp.logical_and(lane >= lo, lane < lo + EXPERTS_PER_GROUP), lg, NEG_BIG)
    v1, i1 = first_max(el)
    v2, i2 = first_max(jnp.where(lane == i1, NEG_BIG, el))
    t = jnp.exp(v2 - v1)
    w0 = g_w / (1.0 + t)
    w1 = g_w * t / (1.0 + t)
    e0 = i1 - N_GROUPS
    e1 = i2 - N_GROUPS
    hit0 = lane == e0
    hit1 = lane == e1
    onehot = jnp.logical_or(hit0, hit1).astype(F32)
    before = cnt_s[0:1, :] + jnp.dot(tri_ref[...], onehot.astype(BF16), preferred_element_type=F32)
    k0 = jnp.sum(jnp.where(hit0, before, 0.0), axis=1, keepdims=True)
    k1 = jnp.sum(jnp.where(hit1, before, 0.0), axis=1, keepdims=True)
    cnt_s[0:1, :] = cnt_s[0:1, :] + jnp.sum(onehot, axis=0, keepdims=True)
    cnt_ref[...] = cnt_s[...]
    rec = jnp.zeros(lg.shape, F32)
    for ln, val in ((R_E0, e0.astype(F32)), (R_E1, e1.astype(F32)), (R_W0, w0), (R_W1, w1), (R_K0, k0), (R_K1, k1)):
        rec = jnp.where(lane == ln, val, rec)
    route_ref[...] = rec


def _prenorm_router(x, w, modtab, w_router, sh_idx, sc_idx, latent_only):
    n_tok = T_X if latent_only else T_ALL
    r = jnp.arange(TILE)
    tri = (r[None, :] < r[:, None]).astype(BF16)
    src = lambda i: _token_tile(i, latent_only)
    return pl.pallas_call(
        functools.partial(_prenorm_router_kernel, sh_idx=sh_idx, sc_idx=sc_idx),
        grid=(n_tok // TILE,),
        in_specs=[pl.BlockSpec((TILE, D_MODEL), lambda i: (src(i), 0)),
                  pl.BlockSpec((1, D_MODEL), lambda i: (0, 0)),
                  pl.BlockSpec((1, 6, D_MODEL), lambda i: (_tile_mod_row(src(i)), 0, 0)),
                  pl.BlockSpec((2, D_MODEL, ROUTER_COLS), lambda i: (0, 0, 0)),
                  pl.BlockSpec((TILE, TILE), lambda i: (0, 0))],
        out_specs=[pl.BlockSpec((TILE, D_MODEL), lambda i: (i, 0)),
                   pl.BlockSpec((TILE, ROUTER_COLS), lambda i: (i, 0)),
                   pl.BlockSpec((8, ROUTER_COLS), lambda i: (0, 0))],
        out_shape=[jax.ShapeDtypeStruct((n_tok, D_MODEL), F32),
                   jax.ShapeDtypeStruct((n_tok, ROUTER_COLS), F32),
                   jax.ShapeDtypeStruct((8, ROUTER_COLS), F32)],
        scratch_shapes=[pltpu.VMEM((8, ROUTER_COLS), F32)],
        compiler_params=_cparams("arbitrary"),
        name="prenorm_router",
    )(x, w.reshape(1, D_MODEL), modtab, w_router, tri)


def _row_copy(src, src_row, dst, dst_row, sem):
    return pltpu.make_async_copy(src.at[pl.ds(src_row, 1)], dst.at[pl.ds(dst_row, 1)], sem)


def _dispatch_kernel(dest_ref, h_ref, xb_ref, sem):
    def issue(r, carry):
        for k in range(TOP_K):
            _row_copy(h_ref, r, xb_ref, dest_ref[0, 0, TOP_K * r + k], sem).start(priority=k)
        return carry

    lax.fori_loop(0, TILE, issue, 0, unroll=8)

    def drain(r, carry):
        for k in range(TOP_K):
            _row_copy(h_ref, 0, xb_ref, 0, sem).wait()
        return carry

    lax.fori_loop(0, TILE, drain, 0, unroll=8)


def _dispatch(h, dest3):
    n_tok = h.shape[0]
    return pl.pallas_call(
        _dispatch_kernel,
        grid=(n_tok // TILE,),
        in_specs=[pl.BlockSpec((1, 1, TOP_K * TILE), lambda i: (i, 0, 0), memory_space=pltpu.SMEM),
                  pl.BlockSpec((TILE, D_MODEL), lambda i: (i, 0))],
        out_specs=pl.BlockSpec(memory_space=pl.ANY),
        out_shape=jax.ShapeDtypeStruct((n_tok * TOP_K, D_MODEL), F32),
        scratch_shapes=[pltpu.SemaphoreType.DMA(())],
        compiler_params=_cparams("arbitrary"),
        name="moe_dispatch",
    )(dest3, h)


def _moe_kernel(tile_ref, exp_ref, lo_ref, hi_ref, flag_ref, x_ref, wg_ref, wu_ref, wd_ref, o_ref,
                wg_s, wu_s, wd_s):
    i = pl.program_id(0)
    flags = flag_ref[i]

    @pl.when(flags % 2 == 1)
    def _():
        wg_s[...] = wg_ref[...].astype(BF16)
        wu_s[...] = wu_ref[...].astype(BF16)
        wd_s[...] = wd_ref[...].astype(BF16)

    @pl.when(flags >= 4)
    def _():
        x = x_ref[...].astype(BF16)
        g = jnp.dot(x, wg_s[...], preferred_element_type=F32)
        u = jnp.dot(x, wu_s[...], preferred_element_type=F32)
        row = lax.broadcasted_iota(jnp.int32, g.shape, 0)
        mine = jnp.logical_and(row >= lo_ref[i], row < hi_ref[i])
        h = jnp.where(mine, g * jax.nn.sigmoid(g) * u, 0.0).astype(BF16)
        res = jnp.dot(h, wd_s[...], preferred_element_type=F32)

        @pl.when((flags // 2) % 2 == 1)
        def _():
            o_ref[...] = res

        @pl.when((flags // 2) % 2 == 0)
        def _():
            o_ref[...] += res


def _moe_experts(xb, meta, wg, wu, wd, l):
    n_items = meta[0].shape[0]
    wmap = lambda i, ti, ex, lo, hi, fl: (l, ex[i], 0, 0)
    xmap = lambda i, ti, ex, lo, hi, fl: (ti[i], 0)
    grid_spec = pltpu.PrefetchScalarGridSpec(
        num_scalar_prefetch=5,
        grid=(n_items,),
        in_specs=[pl.BlockSpec((MOE_ROWS, D_MODEL), xmap),
                  pl.BlockSpec((None, None, D_MODEL, D_EXPERT), wmap),
                  pl.BlockSpec((None, None, D_MODEL, D_EXPERT), wmap),
                  pl.BlockSpec((None, None, D_EXPERT, D_MODEL), wmap)],
        out_specs=pl.BlockSpec((MOE_ROWS, D_MODEL), xmap),
        scratch_shapes=[pltpu.VMEM((D_MODEL, D_EXPERT), BF16),
                        pltpu.VMEM((D_MODEL, D_EXPERT), BF16),
                        pltpu.VMEM((D_EXPERT, D_MODEL), BF16)])
    return pl.pallas_call(
        _moe_kernel,
        grid_spec=grid_spec,
        out_shape=jax.ShapeDtypeStruct(xb.shape, F32),
        compiler_params=_cparams("arbitrary"),
        name="moe_experts",
    )(*meta, xb, wg, wu, wd)


def _combine_kernel(dest_ref, yb_ref, x_ref, route_ref, mod_ref, nw_ref, nmod_ref, *rest, final):
    if final:
        o_ref, buf, sem = rest
    else:
        o_ref, hx_ref, buf, sem = rest

    def issue(r, carry):
        for k in range(TOP_K):
            _row_copy(yb_ref, dest_ref[0, 0, TOP_K * r + k], buf.at[k], r, sem).start(priority=k)
        return carry

    lax.fori_loop(0, TILE, issue, 0, unroll=8)

    def drain(r, carry):
        for k in range(TOP_K):
            _row_copy(yb_ref, 0, buf.at[k], 0, sem).wait()
        return carry

    lax.fori_loop(0, TILE, drain, 0, unroll=8)
    route = route_ref[...]
    f = buf[0] * route[:, R_W0:R_W0 + 1] + buf[1] * route[:, R_W1:R_W1 + 1]
    y = x_ref[...] + mod_ref[0][5:6, :] * f
    if final:
        o_ref[...] = _rms(y, nw_ref[...])
    else:
        o_ref[...] = y
        nm = nmod_ref[0]
        hx_ref[...] = (_rms(y, nw_ref[...]) * (1.0 + nm[1:2, :]) + nm[0:1, :]).astype(hx_ref.dtype)


def _combine(yb, dest3, x, route, modtab, norm_w, next_modtab, latent_only):
    n_tok = route.shape[0]
    src = lambda i: _token_tile(i, latent_only)
    row = lambda i: (i, 0)
    mod_spec = pl.BlockSpec((1, 6, D_MODEL), lambda i: (_tile_mod_row(src(i)), 0, 0))
    out_specs = [pl.BlockSpec((TILE, D_MODEL), row)]
    out_shape = [jax.ShapeDtypeStruct((n_tok, D_MODEL), F32)]
    if not latent_only:
        out_specs.append(pl.BlockSpec((TILE, D_MODEL), row))
        out_shape.append(jax.ShapeDtypeStruct((n_tok, D_MODEL), BF16))
    return pl.pallas_call(
        functools.partial(_combine_kernel, final=latent_only),
        grid=(n_tok // TILE,),
        in_specs=[pl.BlockSpec((1, 1, TOP_K * TILE), lambda i: (i, 0, 0), memory_space=pltpu.SMEM),
                  pl.BlockSpec(memory_space=pl.ANY),
                  pl.BlockSpec((TILE, D_MODEL), lambda i: (src(i), 0)),
                  pl.BlockSpec((TILE, ROUTER_COLS), row),
                  mod_spec,
                  pl.BlockSpec((1, D_MODEL), lambda i: (0, 0)),
                  mod_spec],
        out_specs=out_specs,
        out_shape=out_shape,
        scratch_shapes=[pltpu.VMEM((TOP_K, TILE, D_MODEL), F32), pltpu.SemaphoreType.DMA(())],
        compiler_params=_cparams("arbitrary"),
        name="moe_combine",
    )(dest3, yb, x, route, modtab, norm_w.reshape(1, D_MODEL), next_modtab)


def _moe_plan(route, cnt):
    t = route.shape[0]
    n_tiles = t * TOP_K // MOE_ROWS
    n_items = n_tiles + N_EXPERTS - 1
    experts = route[:, R_E0:R_E1 + 1].astype(jnp.int32)
    rank = route[:, R_K0:R_K1 + 1].astype(jnp.int32)
    counts = cnt[0, :N_EXPERTS].astype(jnp.int32)
    ends = jnp.cumsum(counts)
    starts = ends - counts
    dest = starts[experts] + rank
    t_first = starts // MOE_ROWS
    per_e = jnp.where(counts > 0, (ends - 1) // MOE_ROWS - t_first + 1, 0)
    item_end = jnp.cumsum(per_e)
    item_start = item_end - per_e
    total = item_end[-1]
    w = jnp.arange(n_items, dtype=jnp.int32)
    valid = w < total
    wc = jnp.minimum(w, total - 1)
    ex = jnp.clip(jnp.searchsorted(item_end, wc, side='right'), 0, N_EXPERTS - 1).astype(jnp.int32)
    tile = (t_first[ex] + wc - item_start[ex]).astype(jnp.int32)
    lo = jnp.where(valid, jnp.maximum(starts[ex], tile * MOE_ROWS) - tile * MOE_ROWS, 0).astype(jnp.int32)
    hi = jnp.where(valid, jnp.minimum(ends[ex], (tile + 1) * MOE_ROWS) - tile * MOE_ROWS, 0).astype(jnp.int32)
    one = jnp.ones((1,), bool)
    new_e = jnp.concatenate([one, ex[1:] != ex[:-1]])
    new_t = jnp.concatenate([one, tile[1:] != tile[:-1]])
    flags = (new_e.astype(jnp.int32) + 2 * new_t.astype(jnp.int32) + 4 * valid.astype(jnp.int32))
    dest3 = dest.reshape(t // TILE, 1, TOP_K * TILE)
    return dest3, (tile, ex, lo, hi, flags)


def _moe(x, h, route, cnt, modtab, wg, wu, wd, l, norm_w, next_modtab, latent_only):
    dest3, meta = _moe_plan(route, cnt)
    xb = _dispatch(h, dest3)
    yb = _moe_experts(xb, meta, wg, wu, wd, l)
    res = _combine(yb, dest3, x, route, modtab, norm_w, next_modtab, latent_only)
    return res[0] if latent_only else (res[0], res[1])


def _scan_chunk(s, rev, chunk):
    n_c = CTX_LEN // chunk
    n_all = PB // chunk
    if not rev:
        return s
    return jnp.where(s < n_c, n_c - 1 - s, n_all + n_c - 1 - s)


HALO = 8


def _ssd_prep_kernel(cur_ref, prev_ref, next_ref, dt_ref, cw_ref, cb_ref, dtb_ref, xo_ref, dto_ref, ext_s):
    j = pl.program_id(0) % TPB
    first = jnp.logical_or(j == 0, j == 1)
    last = jnp.logical_or(j == 0, j == TPB - 1)
    ext_s[0:HALO, :] = jnp.where(first, 0.0, prev_ref[...])
    ext_s[HALO:HALO + TILE, :] = cur_ref[...]
    ext_s[HALO + TILE:, :] = jnp.where(last, 0.0, next_ref[...])
    half = (SSD_CONV - 1) // 2
    acc = cb_ref[...] + cw_ref[0:1, :] * ext_s[HALO - half:HALO - half + TILE, :]
    for k in range(1, SSD_CONV):
        acc = acc + cw_ref[k:k + 1, :] * ext_s[HALO - half + k:HALO - half + k + TILE, :]
    xo_ref[...] = acc * jax.nn.sigmoid(acc)
    lane = lax.broadcasted_iota(jnp.int32, (TILE, LANES), 1)
    dto_ref[...] = jnp.where(lane < 2 * H_B, jax.nn.softplus(dt_ref[...] + dtb_ref[...]), 0.0)


def _ssd_prep(z, conv_w, conv_b, dt_bias):
    n_tiles = T_ALL // TILE
    per = TILE // HALO
    cwp = jnp.concatenate([conv_w, jnp.zeros((8 - SSD_CONV, CONV_CH), F32)], axis=0)
    dtb = jnp.concatenate([dt_bias.reshape(1, 2 * H_B), jnp.zeros((1, LANES - 2 * H_B), F32)], axis=1)
    xc = Z_XBC // CONV_CH
    return pl.pallas_call(
        _ssd_prep_kernel,
        grid=(n_tiles,),
        in_specs=[pl.BlockSpec((TILE, CONV_CH), lambda i: (i, xc)),
                  pl.BlockSpec((HALO, CONV_CH), lambda i: (jnp.maximum(i * per - 1, 0), xc)),
                  pl.BlockSpec((HALO, CONV_CH), lambda i: (jnp.minimum((i + 1) * per, T_ALL // HALO - 1), xc)),
                  pl.BlockSpec((TILE, LANES), lambda i: (i, Z_DT // LANES)),
                  pl.BlockSpec((8, CONV_CH), lambda i: (0, 0)),
                  pl.BlockSpec((1, CONV_CH), lambda i: (0, 0)),
                  pl.BlockSpec((1, LANES), lambda i: (0, 0))],
        out_specs=[pl.BlockSpec((TILE, CONV_CH), lambda i: (i, 0)),
                   pl.BlockSpec((TILE, LANES), lambda i: (i, 0))],
        out_shape=[jax.ShapeDtypeStruct((T_ALL, CONV_CH), F32),
                   jax.ShapeDtypeStruct((T_ALL, LANES), F32)],
        scratch_shapes=[pltpu.VMEM((TILE + 2 * HALO, CONV_CH), F32)],
        compiler_params=_cparams("arbitrary"),
        name="ssd_prep",
    )(z, z, z, z, cwp, conv_b.reshape(1, CONV_CH), dtb)


def _ssd_one_direction(xbc, dtp, arow, tri, expand, h_ref, y_ref, b, d, rev):
    q = SSD_CHUNK
    a = dtp * arow
    acum = _dot3_left(tri, a)
    yield
    acum_t = acum.T
    dt_t = dtp.T
    yield
    edge = 0 if rev else q - 1
    atot = acum[edge:edge + 1, :]
    pieces = jnp.concatenate([jnp.exp(atot - acum) * dtp, jnp.exp(acum),
                              jnp.broadcast_to(jnp.exp(atot), (8, LANES))], axis=0)
    ex = _dot3_right(pieces, expand)
    yield
    wend_x = ex[0:q]
    eacum_x = ex[q:2 * q]
    dec_x = ex[2 * q:2 * q + 1]
    xs = xbc[:, 0:D_SSD]
    xw = (xs * wend_x).astype(BF16)
    xs_b = xs.astype(BF16)
    h_old = h_ref[b]
    h_b = h_old.astype(BF16)
    ri = lax.broadcasted_iota(jnp.int32, (q, q), 0)
    ci = lax.broadcasted_iota(jnp.int32, (q, q), 1)
    mask = (ci >= ri) if rev else (ci <= ri)
    lo_half = lax.broadcasted_iota(jnp.int32, (q, LANES), 1) < P_B
    hpg = H_B // G_B
    gw = hpg * P_B
    yield

    def group(g):
        bg = xbc[:, D_SSD + g * N_B:D_SSD + (g + 1) * N_B].astype(BF16)
        cg = xbc[:, D_SSD + G_B * N_B + g * N_B:D_SSD + G_B * N_B + (g + 1) * N_B].astype(BF16)
        cb = lax.dot_general(cg, bg, _NT, preferred_element_type=F32)
        inter = jnp.dot(cg, h_b[:, g * gw:(g + 1) * gw], preferred_element_type=F32)
        upd = lax.dot_general(bg, xw[:, g * gw:(g + 1) * gw], _TN, preferred_element_type=F32)
        yield

        def pair(j):
            ms = []
            for hh in range(2):
                hc = H_B * d + hpg * g + 2 * j + hh
                seg = acum[:, hc:hc + 1] - acum_t[hc:hc + 1, :]
                dec = jnp.exp(jnp.where(mask, seg, NEG_BIG))
                ms.append((cb * dec * dt_t[hc:hc + 1, :]).astype(BF16))
                yield
            c0 = g * gw + 2 * j * P_B
            xp = xs_b[:, c0:c0 + LANES]
            zero = jnp.zeros_like(xp)
            rhs = jnp.concatenate([jnp.where(lo_half, xp, zero), jnp.where(lo_half, zero, xp)], axis=0)
            y_intra = jnp.dot(jnp.concatenate(ms, axis=1), rhs, preferred_element_type=F32)
            yield
            y_ref[b, :, c0:c0 + LANES] = (
                y_intra + eacum_x[:, c0:c0 + LANES] * inter[:, 2 * j * P_B:2 * j * P_B + LANES])
            yield

        yield from _in_turn([pair(j) for j in range(hpg // 2)])
        h_ref[b, :, g * gw:(g + 1) * gw] = dec_x[:, g * gw:(g + 1) * gw] * h_old[:, g * gw:(g + 1) * gw] + upd
        yield

    yield from _in_turn([group(g) for g in range(G_B)])


def _ssd_scan_kernel(xf_ref, dtf_ref, xb_ref, dtb_ref, arow_ref, trif_ref, trib_ref, ef_ref, eb_ref,
                     yf_ref, yb_ref, hf_s, hb_s):
    @pl.when(pl.program_id(0) == 0)
    def _():
        hf_s[...] = jnp.zeros_like(hf_s)
        hb_s[...] = jnp.zeros_like(hb_s)

    chains = []
    for b in range(BATCH):
        chains.append(_ssd_one_direction(xf_ref[b], dtf_ref[b], arow_ref[...], trif_ref[...], ef_ref[...],
                                         hf_s, yf_ref, b, 0, False))
        chains.append(_ssd_one_direction(xb_ref[b], dtb_ref[b], arow_ref[...], trib_ref[...], eb_ref[...],
                                         hb_s, yb_ref, b, 1, True))
    _round_robin(chains)


def _ssd_scan(xact, dtp, a_log):
    q = SSD_CHUNK
    steps = PB // q
    a_neg = -jnp.exp(a_log)
    arow = jnp.concatenate([a_neg.reshape(1, 2 * H_B), jnp.zeros((1, LANES - 2 * H_B), F32)], axis=1)
    r = jnp.arange(q)
    tri_f = (r[None, :] <= r[:, None]).astype(BF16)
    tri_b = (r[None, :] >= r[:, None]).astype(BF16)
    col_head = jnp.arange(D_SSD) // P_B
    lane = jnp.arange(LANES)
    exp_f = (lane[:, None] == col_head[None, :]).astype(BF16)
    exp_b = (lane[:, None] == col_head[None, :] + H_B).astype(BF16)
    fwd = lambda s: (0, _scan_chunk(s, False, q), 0)
    bwd = lambda s: (0, _scan_chunk(s, True, q), 0)
    const = lambda s: (0, 0)
    x3 = xact.reshape(BATCH, PB, CONV_CH)
    d3 = dtp.reshape(BATCH, PB, LANES)
    yf, yb = pl.pallas_call(
        _ssd_scan_kernel,
        grid=(steps,),
        in_specs=[pl.BlockSpec((BATCH, q, CONV_CH), fwd), pl.BlockSpec((BATCH, q, LANES), fwd),
                  pl.BlockSpec((BATCH, q, CONV_CH), bwd), pl.BlockSpec((BATCH, q, LANES), bwd),
                  pl.BlockSpec((1, LANES), const),
                  pl.BlockSpec((q, q), const), pl.BlockSpec((q, q), const),
                  pl.BlockSpec((LANES, D_SSD), const), pl.BlockSpec((LANES, D_SSD), const)],
        out_specs=[pl.BlockSpec((BATCH, q, D_SSD), fwd), pl.BlockSpec((BATCH, q, D_SSD), bwd)],
        out_shape=[jax.ShapeDtypeStruct((BATCH, PB, D_SSD), F32), jax.ShapeDtypeStruct((BATCH, PB, D_SSD), F32)],
        scratch_shapes=[pltpu.VMEM((BATCH, N_B, D_SSD), F32), pltpu.VMEM((BATCH, N_B, D_SSD), F32)],
        compiler_params=_cparams("arbitrary"),
        name="ssd_scan",
    )(x3, d3, x3, d3, arow, tri_f, tri_b, exp_f, exp_b)
    return yf.reshape(T_ALL, D_SSD), yb.reshape(T_ALL, D_SSD)


def _ssd_final_kernel(yf_ref, yb_ref, xs_ref, gate_ref, d_ref, onw_ref, o_ref):
    gate = gate_ref[...]
    y = (yf_ref[...] + yb_ref[...] + d_ref[...] * xs_ref[...]) * (gate * jax.nn.sigmoid(gate))
    o_ref[...] = (_rms(y) * onw_ref[...]).astype(o_ref.dtype)


def _ssd_final(yf, yb, xact, z, d_skip, onw):
    row = lambda i: (i, 0)
    const = lambda i: (0, 0)
    return pl.pallas_call(
        _ssd_final_kernel,
        grid=(T_ALL // MM_ROWS,),
        in_specs=[pl.BlockSpec((MM_ROWS, D_SSD), row), pl.BlockSpec((MM_ROWS, D_SSD), row),
                  pl.BlockSpec((MM_ROWS, D_SSD), row),
                  pl.BlockSpec((MM_ROWS, D_SSD), lambda i: (i, Z_SZ // D_SSD)),
                  pl.BlockSpec((1, D_SSD), const), pl.BlockSpec((1, D_SSD), const)],
        out_specs=pl.BlockSpec((MM_ROWS, D_SSD), row),
        out_shape=jax.ShapeDtypeStruct((T_ALL, D_SSD), BF16),
        compiler_params=_cparams("arbitrary"),
        name="ssd_final",
    )(yf, yb, xact, z, jnp.repeat(d_skip, P_B).reshape(1, D_SSD), onw.reshape(1, D_SSD))


def _ssd_mixer(z, p, onw):
    xact, dtp = _ssd_prep(z, p["ssd_conv_w"], p["ssd_conv_b"], p["ssd_dt_bias"])
    yf, yb = _ssd_scan(xact, dtp, p["ssd_a_log"])
    return _ssd_final(yf, yb, xact, z, p["ssd_d"], onw)


def _mlstm_one_direction(q, k, v, gi, gf, bi, bf, tri, st_ref, m_ref, h_ref, b, d, rev):
    n = MLSTM_CHUNK
    li = gi + bi
    lf = jax.nn.log_sigmoid(gf + bf)
    yield
    bc = _dot3_left(tri, lf)
    yield
    b_t = bc.T
    li_t = li.T
    yield
    edge = 0 if rev else n - 1
    gtot = bc[edge:edge + 1, :]
    m_old = m_ref[b, 0:1, :]
    w_log = gtot - bc + li
    m_new = jnp.maximum(gtot + m_old, jnp.max(w_log, axis=0, keepdims=True))
    wj = jnp.exp(w_log - m_new)
    dec = jnp.exp(gtot + m_old - m_new)
    inter_log = bc + m_old
    m_ref[b, 0:1, :] = m_new
    yield
    ri = lax.broadcasted_iota(jnp.int32, (n, n), 0)
    ci = lax.broadcasted_iota(jnp.int32, (n, n), 1)
    mask = (ci >= ri) if rev else (ci <= ri)
    lo_half = lax.broadcasted_iota(jnp.int32, (n, LANES), 1) < DQK_C
    row_lo = lax.broadcasted_iota(jnp.int32, (2 * DQK_C, 2 * DV_C), 0) < DQK_C
    ones = jnp.ones((n, DV_C), F32)
    st_old = [st_ref[b, j] for j in range(H_C // 2)]
    upds = {}

    def head(h):
        j, hh = divmod(h, 2)
        gl = GATE_LANE0 + H_C * d + h
        qp = q[:, j * LANES:(j + 1) * LANES] * DQK_C ** -0.5
        kp = k[:, j * LANES:(j + 1) * LANES].astype(BF16)
        qm = jnp.where(lo_half if hh == 0 else jnp.logical_not(lo_half), qp, 0.0).astype(BF16)
        qk = lax.dot_general(qm, kp, _NT, preferred_element_type=F32)
        qs = jnp.dot(qm, st_old[j].astype(BF16), preferred_element_type=F32)
        yield
        dmat = jnp.where(mask, bc[:, gl:gl + 1] - b_t[gl:gl + 1, :] + li_t[gl:gl + 1, :], NEG_BIG)
        il = inter_log[:, gl:gl + 1]
        m_row = jnp.maximum(il, jnp.max(dmat, axis=1, keepdims=True))
        yield
        s = qk * jnp.exp(dmat - m_row)
        w_inter = jnp.exp(il - m_row)
        vh = v[:, h * DV_C:(h + 1) * DV_C]
        yield
        num = jnp.dot(s.astype(BF16), vh.astype(BF16), preferred_element_type=F32) + w_inter * qs[:, :DV_C]
        den = jnp.sum(s, axis=1, keepdims=True) + w_inter * qs[:, DV_C:]
        yield
        h_ref[b, :, h * DV_C:(h + 1) * DV_C] = num / jnp.maximum(jnp.abs(den), jnp.exp(-m_row))
        rhs = (wj[:, gl:gl + 1] * jnp.concatenate([vh, ones], axis=1)).astype(BF16)
        upds[h] = lax.dot_general(kp, rhs, _TN, preferred_element_type=F32)
        yield

    yield from _in_turn([head(h) for h in range(H_C)])
    for j in range(H_C // 2):
        ga = GATE_LANE0 + H_C * d + 2 * j
        decv = jnp.where(row_lo, dec[:, ga:ga + 1], dec[:, ga + 1:ga + 2])
        st_ref[b, j] = decv * st_old[j] + jnp.where(row_lo, upds[2 * j], upds[2 * j + 1])
        yield


def _mlstm_scan_kernel(qf, kf, vf, gif, gff, qb, kb, vb, gib, gfb, bi_ref, bf_ref, trif_ref, trib_ref,
                       hf_ref, hb_ref, stf_s, stb_s, mf_s, mb_s):
    @pl.when(pl.program_id(0) == 0)
    def _():
        stf_s[...] = jnp.zeros_like(stf_s)
        stb_s[...] = jnp.zeros_like(stb_s)
        mf_s[...] = jnp.full_like(mf_s, NEG_STATE)
        mb_s[...] = jnp.full_like(mb_s, NEG_STATE)

    chains = []
    for b in range(BATCH):
        chains.append(_mlstm_one_direction(qf[b], kf[b], vf[b], gif[b], gff[b], bi_ref[...], bf_ref[...],
                                           trif_ref[...], stf_s, mf_s, hf_ref, b, 0, False))
        chains.append(_mlstm_one_direction(qb[b], kb[b], vb[b], gib[b], gfb[b], bi_ref[...], bf_ref[...],
                                           trib_ref[...], stb_s, mb_s, hb_ref, b, 1, True))
    _round_robin(chains)


def _mlstm_scan(z, gate_b):
    n = MLSTM_CHUNK
    steps = PB // n
    qkw = H_C * DQK_C
    pad = lambda t: jnp.concatenate([jnp.zeros((1, GATE_LANE0), F32), t.reshape(1, 2 * H_C),
                                     jnp.zeros((1, LANES - GATE_LANE0 - 2 * H_C), F32)], axis=1)
    bi = pad(gate_b[:, 0, :])
    bf = pad(gate_b[:, 1, :])
    r = jnp.arange(n)
    tri_f = (r[None, :] <= r[:, None]).astype(BF16)
    tri_b = (r[None, :] >= r[:, None]).astype(BF16)
    z3 = z.reshape(BATCH, PB, Z_COLS)

    def specs(rev):
        ch = lambda s: _scan_chunk(s, rev, n)
        return [pl.BlockSpec((BATCH, n, qkw), lambda s: (0, ch(s), Z_MQ // qkw)),
                pl.BlockSpec((BATCH, n, qkw), lambda s: (0, ch(s), Z_MK // qkw)),
                pl.BlockSpec((BATCH, n, D_MLSTM), lambda s: (0, ch(s), Z_MV // D_MLSTM)),
                pl.BlockSpec((BATCH, n, LANES), lambda s: (0, ch(s), Z_MG // LANES)),
                pl.BlockSpec((BATCH, n, LANES), lambda s: (0, ch(s), Z_DT // LANES))]

    const = lambda s: (0, 0)
    n_pairs = H_C // 2
    hf, hb = pl.pallas_call(
        _mlstm_scan_kernel,
        grid=(steps,),
        in_specs=specs(False) + specs(True) + [
            pl.BlockSpec((1, LANES), const), pl.BlockSpec((1, LANES), const),
            pl.BlockSpec((n, n), const), pl.BlockSpec((n, n), const)],
        out_specs=[pl.BlockSpec((BATCH, n, D_MLSTM), lambda s: (0, _scan_chunk(s, False, n), 0)),
                   pl.BlockSpec((BATCH, n, D_MLSTM), lambda s: (0, _scan_chunk(s, True, n), 0))],
        out_shape=[jax.ShapeDtypeStruct((BATCH, PB, D_MLSTM), F32), jax.ShapeDtypeStruct((BATCH, PB, D_MLSTM), F32)],
        scratch_shapes=[pltpu.VMEM((BATCH, n_pairs, 2 * DQK_C, 2 * DV_C), F32),
                        pltpu.VMEM((BATCH, n_pairs, 2 * DQK_C, 2 * DV_C), F32),
                        pltpu.VMEM((BATCH, 8, LANES), F32), pltpu.VMEM((BATCH, 8, LANES), F32)],
        compiler_params=_cparams("arbitrary"),
        name="mlstm_scan",
    )(z3, z3, z3, z3, z3, z3, z3, z3, z3, z3, bi, bf, tri_f, tri_b)
    return hf.reshape(T_ALL, D_MLSTM), hb.reshape(T_ALL, D_MLSTM)


def _mlstm_final_kernel(hf_ref, hb_ref, og_ref, onw_ref, o_ref):
    gate = jax.nn.sigmoid(og_ref[...])
    onw = onw_ref[...]
    for h in range(H_C):
        cs = slice(h * DV_C, (h + 1) * DV_C)
        hn = _rms(hf_ref[:, cs] + hb_ref[:, cs])
        o_ref[:, cs] = (hn * gate[:, cs] * onw[:, cs]).astype(o_ref.dtype)


def _mlstm_final(hf, hb, z, onw):
    row = lambda i: (i, 0)
    return pl.pallas_call(
        _mlstm_final_kernel,
        grid=(T_ALL // MM_ROWS,),
        in_specs=[pl.BlockSpec((MM_ROWS, D_MLSTM), row), pl.BlockSpec((MM_ROWS, D_MLSTM), row),
                  pl.BlockSpec((MM_ROWS, D_MLSTM), lambda i: (i, Z_MO // D_MLSTM)),
                  pl.BlockSpec((1, D_MLSTM), lambda i: (0, 0))],
        out_specs=pl.BlockSpec((MM_ROWS, D_MLSTM), row),
        out_shape=jax.ShapeDtypeStruct((T_ALL, D_MLSTM), BF16),
        compiler_params=_cparams("arbitrary"),
        name="mlstm_final",
    )(hf, hb, z, onw.reshape(1, D_MLSTM))


def _mlstm_mixer(z, p, onw):
    hf, hb = _mlstm_scan(z, p["mlstm_gate_b"])
    return _mlstm_final(hf, hb, z, onw)


S5_Q = 256
S5_SEG = S5_Q // 8
S5_LANES = G_S5 * P_S5
S5_SLAB = 512
S5_NSLAB = S5_LANES // S5_SLAB


def _s5_dir_kernel(u_ref, perm_ref, wbr_ref, wbi_ref, atab_ref, apr_ref, api_ref, wcr_ref, wci_ref, y_ref,
                   ur_s, ui_s, xr_s, xi_s, st_s, car_s, *, reverse):
    @pl.when(pl.program_id(0) == 0)
    def _():
        st_s[...] = jnp.zeros_like(st_s)

    for b in range(BATCH):
        up = jnp.dot(perm_ref[...], u_ref[b].astype(BF16), preferred_element_type=F32).astype(BF16)
        for m in range(S5_NSLAB):
            um = up[:, m * LANES:(m + 1) * LANES]
            ur_s[b, :, m * S5_SLAB:(m + 1) * S5_SLAB] = jnp.dot(um, wbr_ref[m], preferred_element_type=F32)
            ui_s[b, :, m * S5_SLAB:(m + 1) * S5_SLAB] = jnp.dot(um, wbi_ref[m], preferred_element_type=F32)

    per = 4
    for grp in range(S5_LANES // (per * LANES)):
        cols = [grp * per * LANES + j * LANES for j in range(per)]
        a_r = [atab_ref[0:8, c0:c0 + LANES] for c0 in cols]
        a_i = [atab_ref[8:16, c0:c0 + LANES] for c0 in cols]

        def body(i, carry, cols=cols, a_r=a_r, a_i=a_i):
            t = (S5_SEG - 1 - i) if reverse else i
            r0 = pl.multiple_of(t * 8, 8)
            new = []
            for b in range(BATCH):
                for j, c0 in enumerate(cols):
                    xr, xi = carry[2 * (b * per + j)], carry[2 * (b * per + j) + 1]
                    nr = a_r[j] * xr - a_i[j] * xi + ur_s[b, pl.ds(r0, 8), c0:c0 + LANES]
                    ni = a_r[j] * xi + a_i[j] * xr + ui_s[b, pl.ds(r0, 8), c0:c0 + LANES]
                    ur_s[b, pl.ds(r0, 8), c0:c0 + LANES] = nr
                    ui_s[b, pl.ds(r0, 8), c0:c0 + LANES] = ni
                    new += [nr, ni]
            return tuple(new)

        lax.fori_loop(0, S5_SEG, body, tuple(jnp.zeros((8, LANES), F32) for _ in range(2 * per * BATCH)), unroll=2)

    as_r = atab_ref[16:17, :]
    as_i = atab_ref[17:18, :]
    end_row = 0 if reverse else 8 * (S5_SEG - 1)
    for b in range(BATCH):
        cr = st_s[b, 0:1, :]
        ci = st_s[b, 1:2, :]
        for k in (range(7, -1, -1) if reverse else range(8)):
            car_s[b, k:k + 1, :] = cr
            car_s[b, 8 + k:9 + k, :] = ci
            er = ur_s[b, end_row + k:end_row + k + 1, :]
            ei = ui_s[b, end_row + k:end_row + k + 1, :]
            cr, ci = er + as_r * cr - as_i * ci, ei + as_r * ci + as_i * cr
        st_s[b, 0:1, :] = cr
        st_s[b, 1:2, :] = ci

    for b in range(BATCH):
        for m in range(S5_NSLAB):
            cs = slice(m * S5_SLAB, (m + 1) * S5_SLAB)
            c_r = jnp.concatenate([car_s[b, 0:8, cs], car_s[b, 0:8, cs]], axis=0)
            c_i = jnp.concatenate([car_s[b, 8:16, cs], car_s[b, 8:16, cs]], axis=0)

            def fix(i, _, b=b, cs=cs, c_r=c_r, c_i=c_i):
                r0 = pl.multiple_of(i * 16, 16)
                p_r = apr_ref[pl.ds(r0, 16), cs]
                p_i = api_ref[pl.ds(r0, 16), cs]
                xr_s[b, pl.ds(r0, 16), cs] = (ur_s[b, pl.ds(r0, 16), cs] + p_r * c_r - p_i * c_i).astype(BF16)
                xi_s[b, pl.ds(r0, 16), cs] = (ui_s[b, pl.ds(r0, 16), cs] + p_r * c_i + p_i * c_r).astype(BF16)
                return 0

            lax.fori_loop(0, S5_Q // 16, fix, 0, unroll=2)

    for b in range(BATCH):
        for m in range(S5_NSLAB):
            cs = slice(m * S5_SLAB, (m + 1) * S5_SLAB)
            y_ref[b, :, m * LANES:(m + 1) * LANES] = (
                jnp.dot(xr_s[b, :, cs], wcr_ref[m], preferred_element_type=F32)
                - jnp.dot(xi_s[b, :, cs], wci_ref[m], preferred_element_type=F32))


def _s5_direction(z3, perm, wbr, wbi, atab, apr, api, wcr, wci, reverse):
    steps = PB // S5_Q
    chunk = lambda s: _scan_chunk(s, reverse, S5_Q)
    const2 = lambda s: (0, 0)
    const3 = lambda s: (0, 0, 0)
    return pl.pallas_call(
        functools.partial(_s5_dir_kernel, reverse=reverse),
        grid=(steps,),
        in_specs=[pl.BlockSpec((BATCH, S5_Q, D_S5), lambda s: (0, chunk(s), Z_D // D_S5)),
                  pl.BlockSpec((S5_Q, S5_Q), const2),
                  pl.BlockSpec(wbr.shape, const3),
                  pl.BlockSpec(wbi.shape, const3),
                  pl.BlockSpec(atab.shape, const2),
                  pl.BlockSpec(apr.shape, const2),
                  pl.BlockSpec(api.shape, const2),
                  pl.BlockSpec(wcr.shape, const3),
                  pl.BlockSpec(wci.shape, const3)],
        out_specs=pl.BlockSpec((BATCH, S5_Q, D_S5), lambda s: (0, chunk(s), 0)),
        out_shape=jax.ShapeDtypeStruct((BATCH, PB, D_S5), F32),
        scratch_shapes=[pltpu.VMEM((BATCH, S5_Q, S5_LANES), F32), pltpu.VMEM((BATCH, S5_Q, S5_LANES), F32),
                        pltpu.VMEM((BATCH, S5_Q, S5_LANES), BF16), pltpu.VMEM((BATCH, S5_Q, S5_LANES), BF16),
                        pltpu.VMEM((BATCH, 8, S5_LANES), F32), pltpu.VMEM((BATCH, 16, S5_LANES), F32)],
        compiler_params=_cparams("arbitrary"),
        name="s5_bwd" if reverse else "s5_fwd",
    )(z3, perm, wbr, wbi, atab, apr, api, wcr, wci).reshape(T_ALL, D_S5)


def _s5_final_kernel(yf_ref, yb_ref, u_ref, permt_ref, d_ref, gw_ref, gb_ref, onw_ref, o_ref):
    y = _dot3_left(permt_ref[...], yf_ref[...] + yb_ref[...])
    y = jax.nn.gelu(y + d_ref[...] * u_ref[...])
    gate = jax.nn.sigmoid(jnp.dot(y.astype(BF16), gw_ref[...], preferred_element_type=F32) + gb_ref[...])
    o_ref[...] = (_rms(y * gate) * onw_ref[...]).astype(o_ref.dtype)


def _s5_final(yf, yb, z, permt, d_skip, glu_w, glu_b, onw):
    row = lambda i: (i, 0)
    const = lambda i: (0, 0)
    return pl.pallas_call(
        _s5_final_kernel,
        grid=(T_ALL // S5_Q,),
        in_specs=[pl.BlockSpec((S5_Q, D_S5), row),
                  pl.BlockSpec((S5_Q, D_S5), row),
                  pl.BlockSpec((S5_Q, D_S5), lambda i: (i, Z_D // D_S5)),
                  pl.BlockSpec((S5_Q, S5_Q), const),
                  pl.BlockSpec((1, D_S5), const),
                  pl.BlockSpec((D_S5, D_S5), const),
                  pl.BlockSpec((1, D_S5), const),
                  pl.BlockSpec((1, D_S5), const)],
        out_specs=pl.BlockSpec((S5_Q, D_S5), row),
        out_shape=jax.ShapeDtypeStruct((T_ALL, D_S5), BF16),
        compiler_params=_cparams("arbitrary"),
        name="s5_final",
    )(yf, yb, z, permt, d_skip.reshape(1, D_S5), glu_w.astype(BF16), glu_b.reshape(1, D_S5), onw.reshape(1, D_S5))


def _s5_tables(lam_re, lam_im, log_dt, b_re, b_im, c_re, c_im):
    dt = jnp.exp(log_dt)[..., None]
    mag = jnp.exp(lam_re * dt)
    ar = mag * jnp.cos(lam_im * dt)
    ai = mag * jnp.sin(lam_im * dt)
    den = lam_re * lam_re + lam_im * lam_im
    cr_ = ((ar - 1.0) * lam_re + ai * lam_im) / den
    ci_ = (ai * lam_re - (ar - 1.0) * lam_im) / den
    bbr = cr_[..., None] * b_re - ci_[..., None] * b_im
    bbi = cr_[..., None] * b_im + ci_[..., None] * b_re
    gps = S5_SLAB // P_S5
    eye = jnp.eye(gps, dtype=F32)

    def drive_w(bb):
        t = bb.reshape(S5_NSLAB, gps, P_S5, S5_GROUP)
        w = jnp.einsum('mgpc,gh->mgchp', t, eye)
        return w.reshape(S5_NSLAB, gps * S5_GROUP, gps * P_S5).astype(BF16)

    def read_w(cc):
        t = cc.reshape(S5_NSLAB, gps, S5_GROUP, P_S5)
        w = jnp.einsum('mgcp,gh->mgphc', t, eye)
        return w.reshape(S5_NSLAB, gps * P_S5, gps * S5_GROUP).astype(BF16)

    steps = jnp.arange(1, S5_SEG + 1, dtype=F32)[:, None]
    out = []
    for d in range(2):
        decay = (lam_re[d] * dt[d]).reshape(1, S5_LANES)
        angle = (lam_im[d] * dt[d]).reshape(1, S5_LANES)
        pmag = jnp.exp(steps * decay)
        pr = pmag * jnp.cos(steps * angle)
        pi = pmag * jnp.sin(steps * angle)
        if d == 1:
            apr, api = jnp.repeat(pr[::-1], 8, axis=0), jnp.repeat(pi[::-1], 8, axis=0)
        else:
            apr, api = jnp.repeat(pr, 8, axis=0), jnp.repeat(pi, 8, axis=0)
        atab = jnp.concatenate([jnp.broadcast_to(pr[0:1], (8, S5_LANES)), jnp.broadcast_to(pi[0:1], (8, S5_LANES)),
                                pr[S5_SEG - 1:], pi[S5_SEG - 1:], jnp.zeros((6, S5_LANES), F32)], axis=0)
        out.append((drive_w(bbr[d]), drive_w(bbi[d]), atab, apr, api))
    r = jnp.arange(S5_Q)
    src = (r % 8) * S5_SEG + r // 8
    perm = (src[:, None] == jnp.arange(S5_Q)[None, :]).astype(BF16)
    return out, read_w(c_re), read_w(c_im), perm


def _s5_mixer(z, p, onw):
    dirs, wcr, wci, perm = _s5_tables(p["s5_lam_re"], p["s5_lam_im"], p["s5_log_dt"], p["s5_b_re"], p["s5_b_im"],
                                      p["s5_c_re"], p["s5_c_im"])
    z3 = z.reshape(BATCH, PB, Z_COLS)
    yf = _s5_direction(z3, perm, *dirs[0], wcr, wci, reverse=False)
    yb = _s5_direction(z3, perm, *dirs[1], wcr, wci, reverse=True)
    return _s5_final(yf, yb, z, perm.T, p["s5_d"], p["s5_glu_w"], p["s5_glu_b"], onw)


def _rope_tables():
    pos = np.arange(SEQ)
    row = (pos // GRID_W).astype(np.float32)
    col = (pos % GRID_W).astype(np.float32)
    inv_freq = (ROPE_BASE ** (-np.arange(ROPE_AXIS // 2, dtype=np.float32) * 2.0 / ROPE_AXIS)).astype(np.float32)
    ang_r = row[:, None] * inv_freq
    ang_c = col[:, None] * inv_freq
    zeros = np.zeros((SEQ, LANES - D_ROPE), np.float32)
    cos = np.concatenate([np.cos(ang_r), np.cos(ang_r), np.cos(ang_c), np.cos(ang_c), zeros], axis=1)
    sin = np.concatenate([np.sin(ang_r), np.sin(ang_r), np.sin(ang_c), np.sin(ang_c), zeros], axis=1)
    cos_c = np.concatenate([np.ones((TILE, D_ROPE), np.float32), np.zeros((TILE, LANES - D_ROPE), np.float32)], axis=1)
    sin_c = np.zeros((TILE, LANES), np.float32)
    table = np.concatenate([np.concatenate([cos, sin], axis=1), np.concatenate([cos_c, sin_c], axis=1)], axis=0)
    return jnp.asarray(table.astype(np.float32))


def _layout_mla(w_uq, w_ukv):
    k = w_uq.shape[0]
    qa, qb, wk, wv = [], [], [], []
    for h in range(H_A):
        base = h * (D_NOPE + D_ROPE)
        rope = w_uq[:, base + D_NOPE:base + D_NOPE + D_ROPE]
        qa += [w_uq[:, base:base + D_NOPE], rope, jnp.zeros((k, LANES - D_ROPE), w_uq.dtype)]
        qb += [_rot_cols(rope), jnp.zeros((k, LANES - D_ROPE), w_uq.dtype)]
        kb = h * (D_NOPE + D_V)
        wk.append(w_ukv[:, kb:kb + D_NOPE])
        wv.append(w_ukv[:, kb + D_NOPE:kb + D_NOPE + D_V])
    cat = lambda xs: jnp.concatenate(xs, axis=1).astype(BF16)
    return cat(qa), cat(qb), cat(wk), cat(wv)


def _layer(xall, hx, modtab, p, big, l, cs, norm_w, next_modtab, last):
    z = _mm(hx, _layout_w_in(big["w_in"], l), MM_ROWS, Z_COLS // 3)
    onw = p["out_norm_w"]
    wqa, wqb, wk, wv = _layout_mla(p["mla_w_uq"], p["mla_w_ukv"])
    q, k, v = _mla_prep(z, cs, p["mla_q_norm_w"], p["mla_kv_norm_w"], wqa, wqb, wk, wv)
    ya = _attention(q, k, v, onw[:D_MLA].reshape(1, D_MLA))
    yb = _ssd_mixer(z, p, onw[D_MLA:D_MLA + D_SSD])
    yc = _mlstm_mixer(z, p, onw[D_MLA + D_SSD:D_MLA + D_SSD + D_MLSTM])
    yd = _s5_mixer(z, p, onw[D_MLA + D_SSD + D_MLSTM:])
    xall = _outproj([ya, yb, yc, yd], p["w_out"].astype(BF16), xall, modtab, 2)

    w_router = jnp.concatenate([p["moe_w_group"], p["moe_w_expert"],
                                jnp.zeros((D_MODEL, ROUTER_COLS - N_GROUPS - N_EXPERTS), F32)], axis=1)
    wr_hi = w_router.astype(BF16)
    wr_lo = (w_router - wr_hi.astype(F32)).astype(BF16)
    h2, route, cnt = _prenorm_router(xall, p["norm2_w"], modtab, jnp.stack([wr_hi, wr_lo]), 3, 4, last)
    return _moe(xall, h2, route, cnt, modtab, big["moe_w_gate"], big["moe_w_up"], big["moe_w_down"], l,
                norm_w, next_modtab, last)


def kernel(x, c, ctx, c_ctx, mod_w, mod_b, norm1_w, w_in, mla_q_norm_w, mla_kv_norm_w, mla_w_uq, mla_w_ukv,
           ssd_conv_w, ssd_conv_b, ssd_a_log, ssd_dt_bias, ssd_d, mlstm_gate_b, s5_lam_re, s5_lam_im,
           s5_log_dt, s5_b_re, s5_b_im, s5_c_re, s5_c_im, s5_d, s5_glu_w, s5_glu_b, out_norm_w, w_out,
           norm2_w, moe_w_group, moe_w_expert, moe_w_gate, moe_w_up, moe_w_down, final_norm_w):
    stacked = {"norm1_w": norm1_w, "mla_q_norm_w": mla_q_norm_w, "mla_kv_norm_w": mla_kv_norm_w,
               "mla_w_uq": mla_w_uq, "mla_w_ukv": mla_w_ukv, "ssd_conv_w": ssd_conv_w, "ssd_conv_b": ssd_conv_b,
               "ssd_a_log": ssd_a_log, "ssd_dt_bias": ssd_dt_bias, "ssd_d": ssd_d, "mlstm_gate_b": mlstm_gate_b,
               "s5_lam_re": s5_lam_re, "s5_lam_im": s5_lam_im, "s5_log_dt": s5_log_dt, "s5_b_re": s5_b_re,
               "s5_b_im": s5_b_im, "s5_c_re": s5_c_re, "s5_c_im": s5_c_im, "s5_d": s5_d, "s5_glu_w": s5_glu_w,
               "s5_glu_b": s5_glu_b, "out_norm_w": out_norm_w, "w_out": w_out, "norm2_w": norm2_w,
               "moe_w_group": moe_w_group, "moe_w_expert": moe_w_expert}
    big = {"w_in": w_in, "moe_w_gate": moe_w_gate, "moe_w_up": moe_w_up, "moe_w_down": moe_w_down}
    cs = _rope_tables()
    cc = jnp.concatenate([c, c_ctx[None, :], jnp.zeros((8 - BATCH - 1, D_MODEL), F32)], axis=0)
    xall = jnp.concatenate([ctx, x], axis=1).reshape(T_ALL, D_MODEL)
    modtabs = [_modulation(cc, mod_w, mod_b, l)[:BATCH + 1].reshape(BATCH + 1, 6, D_MODEL) for l in range(DEPTH)]
    hx = _prenorm(xall, norm1_w[0], modtabs[0], 0, 1)
    for l in range(DEPTH):
        p = {name: val[l] for name, val in stacked.items()}
        if l == DEPTH - 1:
            out = _layer(xall, hx, modtabs[l], p, big, l, cs, final_norm_w, modtabs[l], True)
        else:
            xall, hx = _layer(xall, hx, modtabs[l], p, big, l, cs, norm1_w[l + 1], modtabs[l + 1], False)
    return out.reshape(BATCH, SEQ, D_MODEL)
```

```python
import functools
import math

import jax
import jax.numpy as jnp
import numpy as np
from jax import lax
from jax.experimental import pallas as pl
from jax.experimental.pallas import tpu as pltpu

F32 = jnp.float32
BF16 = jnp.bfloat16

D_MODEL = 2048
BATCH = 2
SEQ = 4096
DEPTH = 2
GRID_W = 64
CTX_LEN = 256
EPS = 1e-6
NEG_STATE = -1e30
NEG_BIG = -1e30

H_A = 4
D_NOPE = 128
D_ROPE = 64
D_V = 128
Q_RANK = 384
KV_RANK = 128
ROPE_AXIS = D_ROPE // 2
ROPE_BASE = 10000.0
D_MLA = H_A * D_V
D_SSD = 512
P_B = 64
H_B = D_SSD // P_B
G_B = 2
N_B = 128
SSD_CONV = 5
SSD_CHUNK = 128
CONV_CH = D_SSD + 2 * G_B * N_B
D_MLSTM = 512
H_C = 4
DV_C = D_MLSTM // H_C
DQK_C = DV_C // 2
MLSTM_CHUNK = 128
D_S5 = 512
S5_GROUP = 16
G_S5 = D_S5 // S5_GROUP
P_S5 = 64
A_COLS = Q_RANK + KV_RANK + D_ROPE
B_COLS = D_SSD + CONV_CH + 2 * H_B
C_COLS = 2 * H_C * DQK_C + 2 * D_MLSTM + 4 * H_C
N_GROUPS = 4
EXPERTS_PER_GROUP = 8
N_EXPERTS = N_GROUPS * EXPERTS_PER_GROUP
TOP_K = 2
D_EXPERT = 512

PB = CTX_LEN + SEQ
T_X = BATCH * SEQ
T_ALL = BATCH * PB

LANES = 128
VMEM_LIMIT_BYTES = 56 * 1024 * 1024

TILE = 256
TPB = PB // TILE
XT = SEQ // TILE
MM_ROWS = 512

Z_CQ = 0
Z_CKV = 384
Z_KR = 512
Z_KRR = 640
Z_MQ = 768
Z_XBC = 1024
Z_SZ = 2048
Z_MV = 2560
Z_MO = 3072
Z_D = 3584
Z_MK = 4096
Z_DT = 4352
Z_MG = 4480
Z_COLS = 4608
GATE_LANE0 = 2 * H_B

MOE_ROWS = 256
ROUTER_COLS = 128

_NT = (((1,), (1,)), ((), ()))
_TN = (((0,), (0,)), ((), ()))


def _cparams(*sem):
    return pltpu.CompilerParams(dimension_semantics=sem, vmem_limit_bytes=VMEM_LIMIT_BYTES)


def _tile_mod_row(i):
    return jnp.where(i % TPB == 0, BATCH, i // TPB)


def _token_tile(i, latent_only):
    return (i // XT) * TPB + 1 + i % XT if latent_only else i


def _rms(x, w=None):
    y = x * lax.rsqrt(jnp.mean(x * x, axis=-1, keepdims=True) + EPS)
    return y if w is None else y * w


def _in_turn(chains):
    live = list(chains)
    while live:
        nxt = []
        for chain in live:
            try:
                next(chain)
                nxt.append(chain)
            except StopIteration:
                pass
        live = nxt
        yield


def _round_robin(chains):
    for _ in _in_turn(chains):
        pass


def _split3(x):
    hi = x.astype(BF16)
    r1 = x - hi.astype(F32)
    mid = r1.astype(BF16)
    lo = (r1 - mid.astype(F32)).astype(BF16)
    return hi, mid, lo


def _dot3_left(sel, x):
    hi, mid, lo = _split3(x)
    return (jnp.dot(sel, hi, preferred_element_type=F32) + jnp.dot(sel, mid, preferred_element_type=F32)
            + jnp.dot(sel, lo, preferred_element_type=F32))


def _dot3_right(x, sel):
    hi, mid, lo = _split3(x)
    return (jnp.dot(hi, sel, preferred_element_type=F32) + jnp.dot(mid, sel, preferred_element_type=F32)
            + jnp.dot(lo, sel, preferred_element_type=F32))


def _mod_kernel(a_ref, w_ref, b_ref, o_ref):
    a = a_ref[...]
    a = a * jax.nn.sigmoid(a)
    o_ref[...] = jnp.dot(a.astype(BF16), w_ref[...].astype(BF16), preferred_element_type=F32) + b_ref[...]


def _modulation(cc, mod_w, mod_b, l):
    n = mod_w.shape[2]
    tn = 1024
    return pl.pallas_call(
        _mod_kernel,
        grid=(n // tn,),
        in_specs=[pl.BlockSpec((8, D_MODEL), lambda j: (0, 0)),
                  pl.BlockSpec((None, D_MODEL, tn), lambda j: (l, 0, j)),
                  pl.BlockSpec((None, 1, tn), lambda j: (l, 0, j))],
        out_specs=pl.BlockSpec((8, tn), lambda j: (0, j)),
        out_shape=jax.ShapeDtypeStruct((8, n), F32),
        compiler_params=_cparams("arbitrary"),
        name="modulation",
    )(cc, mod_w, mod_b.reshape(DEPTH, 1, n))


def _prenorm_kernel(x_ref, c_ref, w_ref, mod_ref, xall_ref, o_ref, *, sh_idx, sc_idx):
    v = jnp.where(pl.program_id(0) % TPB == 0, c_ref[...], x_ref[...])
    xall_ref[...] = v
    y = _rms(v, w_ref[...])
    m = mod_ref[0]
    y = y * (1.0 + m[sc_idx:sc_idx + 1, :]) + m[sh_idx:sh_idx + 1, :]
    o_ref[...] = y.astype(o_ref.dtype)


def _prenorm(x, ctx, w, modtab, sh_idx, sc_idx):
    row = lambda i: (i, 0)
    return pl.pallas_call(
        functools.partial(_prenorm_kernel, sh_idx=sh_idx, sc_idx=sc_idx),
        grid=(T_ALL // TILE,),
        in_specs=[pl.BlockSpec((TILE, D_MODEL), lambda i: ((i // TPB) * XT + jnp.maximum(i % TPB - 1, 0), 0)),
                  pl.BlockSpec((TILE, D_MODEL), lambda i: (i // TPB, 0)),
                  pl.BlockSpec((1, D_MODEL), lambda i: (0, 0)),
                  pl.BlockSpec((1, 6, D_MODEL), lambda i: (_tile_mod_row(i), 0, 0))],
        out_specs=[pl.BlockSpec((TILE, D_MODEL), row), pl.BlockSpec((TILE, D_MODEL), row)],
        out_shape=[jax.ShapeDtypeStruct((T_ALL, D_MODEL), F32), jax.ShapeDtypeStruct((T_ALL, D_MODEL), BF16)],
        compiler_params=_cparams("arbitrary"),
        name="prenorm",
    )(x.reshape(T_X, D_MODEL), ctx.reshape(BATCH * CTX_LEN, D_MODEL), w.reshape(1, D_MODEL), modtab)


def _mm_kernel(a_ref, w_ref, o_ref):
    o_ref[...] = jnp.dot(a_ref[...], w_ref[...], preferred_element_type=F32).astype(o_ref.dtype)


def _mm(a, w, tm, tn, out_dtype=F32):
    m, k = a.shape
    n = w.shape[1]
    return pl.pallas_call(
        _mm_kernel,
        grid=(n // tn, m // tm),
        in_specs=[pl.BlockSpec((tm, k), lambda j, i: (i, 0)),
                  pl.BlockSpec((k, tn), lambda j, i: (0, j))],
        out_specs=pl.BlockSpec((tm, tn), lambda j, i: (i, j)),
        out_shape=jax.ShapeDtypeStruct((m, n), out_dtype),
        compiler_params=_cparams("arbitrary", "arbitrary"),
        name="in_proj",
    )(a, w)


W_IN_B0 = A_COLS
W_IN_C0 = A_COLS + B_COLS
W_IN_D0 = A_COLS + B_COLS + C_COLS
W_IN_CM = W_IN_C0 + 2 * H_C * DQK_C
W_IN_GB = W_IN_CM + 2 * D_MLSTM
W_IN_WIDE = ((Z_CQ, 0, Q_RANK + KV_RANK),
             (Z_MQ, W_IN_C0, H_C * DQK_C),
             (Z_XBC, W_IN_B0 + D_SSD, CONV_CH),
             (Z_SZ, W_IN_B0, D_SSD),
             (Z_MV, W_IN_CM, 2 * D_MLSTM),
             (Z_D, W_IN_D0, D_S5),
             (Z_MK, W_IN_C0 + H_C * DQK_C, H_C * DQK_C))
W_IN_ROWS = 256


def _rot_cols(w):
    q = ROPE_AXIS // 2
    return jnp.concatenate([-w[:, q:2 * q], w[:, 0:q], -w[:, 3 * q:4 * q], w[:, 2 * q:3 * q]], axis=1)


def _w_in_layout_kernel(w_ref, small_ref, o_ref):
    for dst, src, width in W_IN_WIDE:
        o_ref[:, dst:dst + width] = w_ref[:, src:src + width].astype(BF16)
    o_ref[:, Z_KR:Z_MQ] = small_ref[:, 0:2 * LANES]
    o_ref[:, Z_DT:Z_COLS] = small_ref[:, 2 * LANES:4 * LANES]


def _layout_w_in(w_in, l):
    _, k, n = w_in.shape
    w = w_in[l]
    zpad = lambda c: jnp.zeros((k, c), F32)
    kr = w[:, Q_RANK + KV_RANK:A_COLS]
    gb = W_IN_GB
    small = jnp.concatenate([
        kr, zpad(LANES - D_ROPE), _rot_cols(kr), zpad(LANES - D_ROPE),
        w[:, W_IN_B0 + D_SSD + CONV_CH:W_IN_C0], w[:, gb + H_C:gb + 2 * H_C], w[:, gb + 3 * H_C:gb + 4 * H_C],
        zpad(LANES - GATE_LANE0 - 2 * H_C),
        zpad(GATE_LANE0), w[:, gb:gb + H_C], w[:, gb + 2 * H_C:gb + 3 * H_C],
        zpad(LANES - GATE_LANE0 - 2 * H_C)], axis=1).astype(BF16)
    return pl.pallas_call(
        _w_in_layout_kernel,
        grid=(k // W_IN_ROWS,),
        in_specs=[pl.BlockSpec((None, W_IN_ROWS, n), lambda i: (l, i, 0)),
                  pl.BlockSpec((W_IN_ROWS, 4 * LANES), lambda i: (i, 0))],
        out_specs=pl.BlockSpec((W_IN_ROWS, Z_COLS), lambda i: (i, 0)),
        out_shape=jax.ShapeDtypeStruct((k, Z_COLS), BF16),
        compiler_params=_cparams("arbitrary"),
        name="w_in_layout",
    )(w_in, small)


def _mla_prep_kernel(za_ref, cs_ref, qw_ref, kvw_ref, wqa_ref, wqb_ref, wk_ref, wv_ref, q_ref, k_ref, v_ref):
    za = za_ref[...]
    cos = cs_ref[:, :LANES]
    sin = cs_ref[:, LANES:]
    qn = _rms(za[:, Z_CQ:Z_CQ + Q_RANK], qw_ref[...]).astype(BF16)
    kvn = _rms(za[:, Z_CKV:Z_CKV + KV_RANK], kvw_ref[...]).astype(BF16)
    qa = jnp.dot(qn, wqa_ref[...], preferred_element_type=F32)
    qb = jnp.dot(qn, wqb_ref[...], preferred_element_type=F32)
    kn = jnp.dot(kvn, wk_ref[...], preferred_element_type=F32)
    v = jnp.dot(kvn, wv_ref[...], preferred_element_type=F32)
    kr = (za[:, Z_KR:Z_KR + LANES] * cos + za[:, Z_KRR:Z_KRR + LANES] * sin).astype(BF16)
    for h in range(H_A):
        c0 = h * 2 * LANES
        q_ref[:, c0:c0 + LANES] = qa[:, c0:c0 + LANES].astype(BF16)
        q_ref[:, c0 + LANES:c0 + 2 * LANES] = (
            qa[:, c0 + LANES:c0 + 2 * LANES] * cos + qb[:, h * LANES:(h + 1) * LANES] * sin).astype(BF16)
        k_ref[:, c0:c0 + LANES] = kn[:, h * LANES:(h + 1) * LANES].astype(BF16)
        k_ref[:, c0 + LANES:c0 + 2 * LANES] = kr
    v_ref[...] = v.astype(BF16)


ATT_W = H_A * 2 * LANES


def _mla_prep(z, cs, qw, kvw, wqa, wqb, wk, wv):
    const = lambda i: (0, 0)
    rope_blk = lambda i: (jnp.where(i % TPB == 0, XT, i % TPB - 1), 0)
    return pl.pallas_call(
        _mla_prep_kernel,
        grid=(T_ALL // TILE,),
        in_specs=[pl.BlockSpec((TILE, Z_MQ), lambda i: (i, 0)),
                  pl.BlockSpec((TILE, 2 * LANES), rope_blk),
                  pl.BlockSpec((1, Q_RANK), const),
                  pl.BlockSpec((1, KV_RANK), const),
                  pl.BlockSpec(wqa.shape, const),
                  pl.BlockSpec(wqb.shape, const),
                  pl.BlockSpec(wk.shape, const),
                  pl.BlockSpec(wv.shape, const)],
        out_specs=[pl.BlockSpec((TILE, ATT_W), lambda i: (i, 0)),
                   pl.BlockSpec((TILE, ATT_W), lambda i: (i, 0)),
                   pl.BlockSpec((TILE, D_MLA), lambda i: (i, 0))],
        out_shape=[jax.ShapeDtypeStruct((T_ALL, ATT_W), BF16),
                   jax.ShapeDtypeStruct((T_ALL, ATT_W), BF16),
                   jax.ShapeDtypeStruct((T_ALL, D_MLA), BF16)],
        compiler_params=_cparams("arbitrary"),
        name="mla_prep",
    )(z, cs, qw.reshape(1, Q_RANK), kvw.reshape(1, KV_RANK), wqa, wqb, wk, wv)


def _attn_tile(q_ref, k_ref, v_ref, w_ref, o_ref, acc_ref, n_keys):
    scale2 = (D_NOPE + D_ROPE) ** -0.5 * math.log2(math.e)
    for h in range(H_A):
        q = q_ref[:, h * 2 * LANES:(h + 1) * 2 * LANES]
        s = lax.dot_general(q, k_ref[0:n_keys, h * 2 * LANES:(h + 1) * 2 * LANES], _NT, preferred_element_type=F32)
        m = jnp.max(s, axis=-1, keepdims=True)
        p = jnp.exp2((s - m) * scale2)
        l = jnp.sum(p, axis=-1, keepdims=True)
        o = jnp.dot(p.astype(BF16), v_ref[0:n_keys, h * D_V:(h + 1) * D_V], preferred_element_type=F32)
        acc_ref[:, h * D_V:(h + 1) * D_V] = o / l
    o_ref[...] = (_rms(acc_ref[...]) * w_ref[...]).astype(o_ref.dtype)


def _attn_kernel(q_ref, k_ref, v_ref, w_ref, o_ref, acc_ref):
    @pl.when(pl.program_id(1) == 0)
    def _():
        _attn_tile(q_ref, k_ref, v_ref, w_ref, o_ref, acc_ref, CTX_LEN)

    @pl.when(pl.program_id(1) != 0)
    def _():
        _attn_tile(q_ref, k_ref, v_ref, w_ref, o_ref, acc_ref, PB)


def _attention(q, k, v, onw):
    return pl.pallas_call(
        _attn_kernel,
        grid=(BATCH, TPB),
        in_specs=[pl.BlockSpec((TILE, ATT_W), lambda b, i: (b * TPB + i, 0)),
                  pl.BlockSpec((None, PB, ATT_W), lambda b, i: (b, 0, 0)),
                  pl.BlockSpec((None, PB, D_MLA), lambda b, i: (b, 0, 0)),
                  pl.BlockSpec((1, D_MLA), lambda b, i: (0, 0))],
        out_specs=pl.BlockSpec((TILE, D_MLA), lambda b, i: (b * TPB + i, 0)),
        out_shape=jax.ShapeDtypeStruct((T_ALL, D_MLA), BF16),
        scratch_shapes=[pltpu.VMEM((TILE, D_MLA), F32)],
        compiler_params=_cparams("arbitrary", "arbitrary"),
        name="attention",
    )(q, k.reshape(BATCH, PB, ATT_W), v.reshape(BATCH, PB, D_MLA), onw)


def _outproj_kernel(a0, a1, a2, a3, w0, w1, w2, w3, x_ref, mod_ref, o_ref, wb_s, *, g_idx):
    @pl.when(pl.program_id(0) == 0)
    def _():
        for r, w in enumerate((w0, w1, w2, w3)):
            wb_s[r] = w[...].astype(BF16)

    acc = jnp.dot(a0[...], wb_s[0], preferred_element_type=F32)
    acc += jnp.dot(a1[...], wb_s[1], preferred_element_type=F32)
    acc += jnp.dot(a2[...], wb_s[2], preferred_element_type=F32)
    acc += jnp.dot(a3[...], wb_s[3], preferred_element_type=F32)
    g = mod_ref[0][g_idx:g_idx + 1, :]
    o_ref[...] = x_ref[...] + g * acc


def _outproj(ys, w_out, l, x, modtab, g_idx):
    kq = D_MODEL // 4
    a_specs = [pl.BlockSpec((TILE, kq), lambda i: (i, 0)) for _ in range(4)]
    w_specs = [pl.BlockSpec((None, kq, D_MODEL), functools.partial(lambda i, r: (l, r, 0), r=r)) for r in range(4)]
    return pl.pallas_call(
        functools.partial(_outproj_kernel, g_idx=g_idx),
        grid=(T_ALL // TILE,),
        in_specs=a_specs + w_specs + [
            pl.BlockSpec((TILE, D_MODEL), lambda i: (i, 0)),
            pl.BlockSpec((1, 6, D_MODEL), lambda i: (_tile_mod_row(i), 0, 0))],
        out_specs=pl.BlockSpec((TILE, D_MODEL), lambda i: (i, 0)),
        out_shape=jax.ShapeDtypeStruct((T_ALL, D_MODEL), F32),
        scratch_shapes=[pltpu.VMEM((4, kq, D_MODEL), BF16)],
        compiler_params=_cparams("arbitrary"),
        name="out_proj",
    )(*ys, w_out, w_out, w_out, w_out, x, modtab)


R_E0, R_E1, R_W0, R_W1, R_K0, R_K1 = range(6)
NO_LANE = 2 * LANES


def _prenorm_router_kernel(x_ref, w_ref, mod_ref, wr_ref, tri_ref, h_ref, route_ref, route_t_ref, cnt_ref, cnt_s,
                           *, sh_idx, sc_idx):
    @pl.when(pl.program_id(0) == 0)
    def _():
        cnt_s[...] = jnp.zeros_like(cnt_s)

    y = _rms(x_ref[...], w_ref[...])
    m = mod_ref[0]
    y = y * (1.0 + m[sc_idx:sc_idx + 1, :]) + m[sh_idx:sh_idx + 1, :]
    h_ref[...] = y
    y_hi = y.astype(BF16)
    y_lo = (y - y_hi.astype(F32)).astype(BF16)
    lg = (jnp.dot(y_hi, wr_ref[0], preferred_element_type=F32) + jnp.dot(y_lo, wr_ref[0], preferred_element_type=F32)
          + jnp.dot(y_hi, wr_ref[1], preferred_element_type=F32))
    lane = lax.broadcasted_iota(jnp.int32, lg.shape, 1)

    def first_max(v):
        top = jnp.max(v, axis=1, keepdims=True)
        return top, jnp.min(jnp.where(v == top, lane, NO_LANE), axis=1, keepdims=True)

    is_g = lane < N_GROUPS
    g_top, g_idx = first_max(jnp.where(is_g, lg, NEG_BIG))
    g_w = 1.0 / jnp.sum(jnp.where(is_g, jnp.exp(lg - g_top), 0.0), axis=1, keepdims=True)
    lo = N_GROUPS + EXPERTS_PER_GROUP * g_idx
    el = jnp.where(jnp.logical_and(lane >= lo, lane < lo + EXPERTS_PER_GROUP), lg, NEG_BIG)
    v1, i1 = first_max(el)
    v2, i2 = first_max(jnp.where(lane == i1, NEG_BIG, el))
    t = jnp.exp(v2 - v1)
    w0 = g_w / (1.0 + t)
    w1 = g_w * t / (1.0 + t)
    e0 = i1 - N_GROUPS
    e1 = i2 - N_GROUPS
    hit0 = lane == e0
    hit1 = lane == e1
    onehot = jnp.logical_or(hit0, hit1).astype(F32)
    before = cnt_s[0:1, :] + jnp.dot(tri_ref[...], onehot.astype(BF16), preferred_element_type=F32)
    k0 = jnp.sum(jnp.where(hit0, before, 0.0), axis=1, keepdims=True)
    k1 = jnp.sum(jnp.where(hit1, before, 0.0), axis=1, keepdims=True)
    cnt_s[0:1, :] = cnt_s[0:1, :] + jnp.sum(onehot, axis=0, keepdims=True)
    cnt_ref[...] = cnt_s[...]
    rec = jnp.zeros(lg.shape, F32)
    for ln, val in ((R_E0, e0.astype(F32)), (R_E1, e1.astype(F32)), (R_W0, w0), (R_W1, w1), (R_K0, k0), (R_K1, k1)):
        rec = jnp.where(lane == ln, val, rec)
    route_ref[...] = rec
    route_t_ref[...] = rec.T[0:8, :]


def _prenorm_router(x, w, modtab, w_router, sh_idx, sc_idx, latent_only):
    n_tok = T_X if latent_only else T_ALL
    r = jnp.arange(TILE)
    tri = (r[None, :] < r[:, None]).astype(BF16)
    src = lambda i: _token_tile(i, latent_only)
    return pl.pallas_call(
        functools.partial(_prenorm_router_kernel, sh_idx=sh_idx, sc_idx=sc_idx),
        grid=(n_tok // TILE,),
        in_specs=[pl.BlockSpec((TILE, D_MODEL), lambda i: (src(i), 0)),
                  pl.BlockSpec((1, D_MODEL), lambda i: (0, 0)),
                  pl.BlockSpec((1, 6, D_MODEL), lambda i: (_tile_mod_row(src(i)), 0, 0)),
                  pl.BlockSpec((2, D_MODEL, ROUTER_COLS), lambda i: (0, 0, 0)),
                  pl.BlockSpec((TILE, TILE), lambda i: (0, 0))],
        out_specs=[pl.BlockSpec((TILE, D_MODEL), lambda i: (i, 0)),
                   pl.BlockSpec((TILE, ROUTER_COLS), lambda i: (i, 0)),
                   pl.BlockSpec((8, TILE), lambda i: (0, i)),
                   pl.BlockSpec((8, ROUTER_COLS), lambda i: (0, 0))],
        out_shape=[jax.ShapeDtypeStruct((n_tok, D_MODEL), F32),
                   jax.ShapeDtypeStruct((n_tok, ROUTER_COLS), F32),
                   jax.ShapeDtypeStruct((8, n_tok), F32),
                   jax.ShapeDtypeStruct((8, ROUTER_COLS), F32)],
        scratch_shapes=[pltpu.VMEM((8, ROUTER_COLS), F32)],
        compiler_params=_cparams("arbitrary"),
        name="prenorm_router",
    )(x, w.reshape(1, D_MODEL), modtab, w_router, tri)


def _row_copy(src, src_row, dst, dst_row, sem):
    return pltpu.make_async_copy(src.at[pl.ds(src_row, 1)], dst.at[pl.ds(dst_row, 1)], sem)


def _dispatch_kernel(dest_ref, h_ref, xb_ref, sem):
    def issue(r, carry):
        for k in range(TOP_K):
            _row_copy(h_ref, r, xb_ref, dest_ref[0, k, r], sem).start(priority=k)
        return carry

    lax.fori_loop(0, TILE, issue, 0, unroll=8)

    def drain(r, carry):
        for k in range(TOP_K):
            _row_copy(h_ref, 0, xb_ref, 0, sem).wait()
        return carry

    lax.fori_loop(0, TILE, drain, 0, unroll=8)


def _dispatch(h, dest3):
    n_tok = h.shape[0]
    return pl.pallas_call(
        _dispatch_kernel,
        grid=(n_tok // TILE,),
        in_specs=[pl.BlockSpec((1, TOP_K, TILE), lambda i: (i, 0, 0), memory_space=pltpu.SMEM),
                  pl.BlockSpec((TILE, D_MODEL), lambda i: (i, 0))],
        out_specs=pl.BlockSpec(memory_space=pl.ANY),
        out_shape=jax.ShapeDtypeStruct((n_tok * TOP_K, D_MODEL), F32),
        scratch_shapes=[pltpu.SemaphoreType.DMA(())],
        compiler_params=_cparams("arbitrary"),
        name="moe_dispatch",
    )(dest3, h)


def _moe_kernel(tile_ref, exp_ref, lo_ref, hi_ref, flag_ref, x_ref, wg_ref, wu_ref, wd_ref, o_ref,
                wg_s, wu_s, wd_s):
    i = pl.program_id(0)
    flags = flag_ref[i]

    @pl.when(flags % 2 == 1)
    def _():
        wg_s[...] = wg_ref[...].astype(BF16)
        wu_s[...] = wu_ref[...].astype(BF16)
        wd_s[...] = wd_ref[...].astype(BF16)

    @pl.when(flags >= 4)
    def _():
        x = x_ref[...].astype(BF16)
        g = jnp.dot(x, wg_s[...], preferred_element_type=F32)
        u = jnp.dot(x, wu_s[...], preferred_element_type=F32)
        row = lax.broadcasted_iota(jnp.int32, g.shape, 0)
        mine = jnp.logical_and(row >= lo_ref[i], row < hi_ref[i])
        h = jnp.where(mine, g * jax.nn.sigmoid(g) * u, 0.0).astype(BF16)
        res = jnp.dot(h, wd_s[...], preferred_element_type=F32)

        @pl.when((flags // 2) % 2 == 1)
        def _():
            o_ref[...] = res

        @pl.when((flags // 2) % 2 == 0)
        def _():
            o_ref[...] += res


def _moe_experts(xb, meta, wg, wu, wd, l):
    n_items = meta[0].shape[0]
    wmap = lambda i, ti, ex, lo, hi, fl: (l, ex[i], 0, 0)
    xmap = lambda i, ti, ex, lo, hi, fl: (ti[i], 0)
    grid_spec = pltpu.PrefetchScalarGridSpec(
        num_scalar_prefetch=5,
        grid=(n_items,),
        in_specs=[pl.BlockSpec((MOE_ROWS, D_MODEL), xmap),
                  pl.BlockSpec((None, None, D_MODEL, D_EXPERT), wmap),
                  pl.BlockSpec((None, None, D_MODEL, D_EXPERT), wmap),
                  pl.BlockSpec((None, None, D_EXPERT, D_MODEL), wmap)],
        out_specs=pl.BlockSpec((MOE_ROWS, D_MODEL), xmap),
        scratch_shapes=[pltpu.VMEM((D_MODEL, D_EXPERT), BF16),
                        pltpu.VMEM((D_MODEL, D_EXPERT), BF16),
                        pltpu.VMEM((D_EXPERT, D_MODEL), BF16)])
    return pl.pallas_call(
        _moe_kernel,
        grid_spec=grid_spec,
        out_shape=jax.ShapeDtypeStruct(xb.shape, F32),
        compiler_params=_cparams("arbitrary"),
        name="moe_experts",
    )(*meta, xb, wg, wu, wd)


def _combine_kernel(dest_ref, next_ref, yb_ref, x_ref, route_ref, mod_ref, nw_ref, nmod_ref, *rest, final):
    if final:
        o_ref, buf, sem = rest
    else:
        o_ref, hx_ref, buf, sem = rest
    i = pl.program_id(0)
    slot = i % 2
    n_groups = TILE // COMBINE_ROWS

    def issue(d_ref, to_slot, j):
        for rr in range(COMBINE_ROWS):
            r = j * COMBINE_ROWS + rr
            for k in range(TOP_K):
                _row_copy(yb_ref, d_ref[0, k, r], buf.at[to_slot, k], r, sem.at[to_slot]).start(priority=k)

    @pl.when(i == 0)
    def _():
        def first(j, carry):
            issue(dest_ref, 0, j)
            return carry

        lax.fori_loop(0, n_groups, first, 0)

    def drain(r, carry):
        for k in range(TOP_K):
            _row_copy(yb_ref, 0, buf.at[slot, k], 0, sem.at[slot]).wait()
        return carry

    lax.fori_loop(0, TILE, drain, 0, unroll=8)
    g2 = mod_ref[0][5:6, :]
    nw = nw_ref[...]
    nm = nmod_ref[0]

    def combine(j):
        rows = pl.ds(pl.multiple_of(j * COMBINE_ROWS, COMBINE_ROWS), COMBINE_ROWS)
        rt = route_ref[rows, :]
        f = buf[slot, 0, rows, :] * rt[:, R_W0:R_W0 + 1] + buf[slot, 1, rows, :] * rt[:, R_W1:R_W1 + 1]
        y = x_ref[rows, :] + g2 * f
        if final:
            o_ref[rows, :] = _rms(y, nw)
        else:
            o_ref[rows, :] = y
            hx_ref[rows, :] = (_rms(y, nw) * (1.0 + nm[1:2, :]) + nm[0:1, :]).astype(hx_ref.dtype)

    @pl.when(i + 1 < pl.num_programs(0))
    def _():
        def body(j, carry):
            combine(j)
            issue(next_ref, 1 - slot, j)
            return carry

        lax.fori_loop(0, n_groups, body, 0)

    @pl.when(i + 1 >= pl.num_programs(0))
    def _():
        def body(j, carry):
            combine(j)
            return carry

        lax.fori_loop(0, n_groups, body, 0)


COMBINE_ROWS = 64


def _combine(yb, dest3, x, route, modtab, norm_w, next_modtab, latent_only):
    n_tok = route.shape[0]
    n_tiles = n_tok // TILE
    src = lambda i: _token_tile(i, latent_only)
    row = lambda i: (i, 0)
    mod_spec = pl.BlockSpec((1, 6, D_MODEL), lambda i: (_tile_mod_row(src(i)), 0, 0))
    out_specs = [pl.BlockSpec((TILE, D_MODEL), row)]
    out_shape = [jax.ShapeDtypeStruct((n_tok, D_MODEL), F32)]
    if not latent_only:
        out_specs.append(pl.BlockSpec((TILE, D_MODEL), row))
        out_shape.append(jax.ShapeDtypeStruct((n_tok, D_MODEL), BF16))
    return pl.pallas_call(
        functools.partial(_combine_kernel, final=latent_only),
        grid=(n_tiles,),
        in_specs=[pl.BlockSpec((1, TOP_K, TILE), lambda i: (i, 0, 0), memory_space=pltpu.SMEM),
                  pl.BlockSpec((1, TOP_K, TILE), lambda i: (jnp.minimum(i + 1, n_tiles - 1), 0, 0),
                               memory_space=pltpu.SMEM),
                  pl.BlockSpec(memory_space=pl.ANY),
                  pl.BlockSpec((TILE, D_MODEL), lambda i: (src(i), 0)),
                  pl.BlockSpec((TILE, ROUTER_COLS), row),
                  mod_spec,
                  pl.BlockSpec((1, D_MODEL), lambda i: (0, 0)),
                  mod_spec],
        out_specs=out_specs,
        out_shape=out_shape,
        scratch_shapes=[pltpu.VMEM((2, TOP_K, TILE, D_MODEL), F32), pltpu.SemaphoreType.DMA((2,))],
        compiler_params=_cparams("arbitrary"),
        name="moe_combine",
    )(dest3, dest3, yb, x, route, modtab, norm_w.reshape(1, D_MODEL), next_modtab)


def _moe_plan(route_t, cnt):
    t = route_t.shape[1]
    n_tiles = t * TOP_K // MOE_ROWS
    n_items = n_tiles + N_EXPERTS - 1
    experts = route_t[R_E0:R_E1 + 1].astype(jnp.int32)
    rank = route_t[R_K0:R_K1 + 1].astype(jnp.int32)
    counts = cnt[0, :N_EXPERTS].astype(jnp.int32)
    ends = jnp.cumsum(counts)
    starts = ends - counts
    dest = starts[experts] + rank
    t_first = starts // MOE_ROWS
    per_e = jnp.where(counts > 0, (ends - 1) // MOE_ROWS - t_first + 1, 0)
    item_end = jnp.cumsum(per_e)
    item_start = item_end - per_e
    total = item_end[-1]
    w = jnp.arange(n_items, dtype=jnp.int32)
    valid = w < total
    wc = jnp.minimum(w, total - 1)
    ex = jnp.clip(jnp.searchsorted(item_end, wc, side='right'), 0, N_EXPERTS - 1).astype(jnp.int32)
    tile = (t_first[ex] + wc - item_start[ex]).astype(jnp.int32)
    lo = jnp.where(valid, jnp.maximum(starts[ex], tile * MOE_ROWS) - tile * MOE_ROWS, 0).astype(jnp.int32)
    hi = jnp.where(valid, jnp.minimum(ends[ex], (tile + 1) * MOE_ROWS) - tile * MOE_ROWS, 0).astype(jnp.int32)
    one = jnp.ones((1,), bool)
    new_e = jnp.concatenate([one, ex[1:] != ex[:-1]])
    new_t = jnp.concatenate([one, tile[1:] != tile[:-1]])
    flags = (new_e.astype(jnp.int32) + 2 * new_t.astype(jnp.int32) + 4 * valid.astype(jnp.int32))
    dest3 = dest.reshape(TOP_K, t // TILE, TILE).transpose(1, 0, 2)
    return dest3, (tile, ex, lo, hi, flags)


def _moe(x, h, route, route_t, cnt, modtab, wg, wu, wd, l, norm_w, next_modtab, latent_only):
    dest3, meta = _moe_plan(route_t, cnt)
    xb = _dispatch(h, dest3)
    yb = _moe_experts(xb, meta, wg, wu, wd, l)
    res = _combine(yb, dest3, x, route, modtab, norm_w, next_modtab, latent_only)
    return res[0] if latent_only else (res[0], res[1])


def _scan_chunk(s, rev, chunk):
    n_c = CTX_LEN // chunk
    n_all = PB // chunk
    if not rev:
        return s
    return jnp.where(s < n_c, n_c - 1 - s, n_all + n_c - 1 - s)


HALO = 8


def _ssd_prep_kernel(cur_ref, prev_ref, next_ref, dt_ref, cw_ref, cb_ref, dtb_ref, xo_ref, dto_ref, ext_s):
    j = pl.program_id(0) % TPB
    first = jnp.logical_or(j == 0, j == 1)
    last = jnp.logical_or(j == 0, j == TPB - 1)
    ext_s[0:HALO, :] = jnp.where(first, 0.0, prev_ref[...])
    ext_s[HALO:HALO + TILE, :] = cur_ref[...]
    ext_s[HALO + TILE:, :] = jnp.where(last, 0.0, next_ref[...])
    half = (SSD_CONV - 1) // 2
    acc = cb_ref[...] + cw_ref[0:1, :] * ext_s[HALO - half:HALO - half + TILE, :]
    for k in range(1, SSD_CONV):
        acc = acc + cw_ref[k:k + 1, :] * ext_s[HALO - half + k:HALO - half + k + TILE, :]
    xo_ref[...] = acc * jax.nn.sigmoid(acc)
    lane = lax.broadcasted_iota(jnp.int32, (TILE, LANES), 1)
    dto_ref[...] = jnp.where(lane < 2 * H_B, jax.nn.softplus(dt_ref[...] + dtb_ref[...]), 0.0)


def _ssd_prep(z, conv_w, conv_b, dt_bias):
    n_tiles = T_ALL // TILE
    per = TILE // HALO
    cwp = jnp.concatenate([conv_w, jnp.zeros((8 - SSD_CONV, CONV_CH), F32)], axis=0)
    dtb = jnp.concatenate([dt_bias.reshape(1, 2 * H_B), jnp.zeros((1, LANES - 2 * H_B), F32)], axis=1)
    xc = Z_XBC // CONV_CH
    return pl.pallas_call(
        _ssd_prep_kernel,
        grid=(n_tiles,),
        in_specs=[pl.BlockSpec((TILE, CONV_CH), lambda i: (i, xc)),
                  pl.BlockSpec((HALO, CONV_CH), lambda i: (jnp.maximum(i * per - 1, 0), xc)),
                  pl.BlockSpec((HALO, CONV_CH), lambda i: (jnp.minimum((i + 1) * per, T_ALL // HALO - 1), xc)),
                  pl.BlockSpec((TILE, LANES), lambda i: (i, Z_DT // LANES)),
                  pl.BlockSpec((8, CONV_CH), lambda i: (0, 0)),
                  pl.BlockSpec((1, CONV_CH), lambda i: (0, 0)),
                  pl.BlockSpec((1, LANES), lambda i: (0, 0))],
        out_specs=[pl.BlockSpec((TILE, CONV_CH), lambda i: (i, 0)),
                   pl.BlockSpec((TILE, LANES), lambda i: (i, 0))],
        out_shape=[jax.ShapeDtypeStruct((T_ALL, CONV_CH), F32),
                   jax.ShapeDtypeStruct((T_ALL, LANES), F32)],
        scratch_shapes=[pltpu.VMEM((TILE + 2 * HALO, CONV_CH), F32)],
        compiler_params=_cparams("arbitrary"),
        name="ssd_prep",
    )(z, z, z, z, cwp, conv_b.reshape(1, CONV_CH), dtb)


def _ssd_one_direction(xbc, dtp, arow, tri, expand, h_ref, y_ref, b, d, rev):
    q = SSD_CHUNK
    a = dtp * arow
    acum = _dot3_left(tri, a)
    yield
    acum_t = acum.T
    dt_t = dtp.T
    yield
    edge = 0 if rev else q - 1
    atot = acum[edge:edge + 1, :]
    pieces = jnp.concatenate([jnp.exp(atot - acum) * dtp, jnp.exp(acum),
                              jnp.broadcast_to(jnp.exp(atot), (8, LANES))], axis=0)
    ex = _dot3_right(pieces, expand)
    yield
    wend_x = ex[0:q]
    eacum_x = ex[q:2 * q]
    dec_x = ex[2 * q:2 * q + 1]
    xs = xbc[:, 0:D_SSD]
    xw = (xs * wend_x).astype(BF16)
    xs_b = xs.astype(BF16)
    h_old = h_ref[b]
    h_b = h_old.astype(BF16)
    ri = lax.broadcasted_iota(jnp.int32, (q, q), 0)
    ci = lax.broadcasted_iota(jnp.int32, (q, q), 1)
    mask = (ci >= ri) if rev else (ci <= ri)
    lo_half = lax.broadcasted_iota(jnp.int32, (q, LANES), 1) < P_B
    hpg = H_B // G_B
    gw = hpg * P_B
    yield

    def group(g):
        bg = xbc[:, D_SSD + g * N_B:D_SSD + (g + 1) * N_B].astype(BF16)
        cg = xbc[:, D_SSD + G_B * N_B + g * N_B:D_SSD + G_B * N_B + (g + 1) * N_B].astype(BF16)
        cb = lax.dot_general(cg, bg, _NT, preferred_element_type=F32)
        inter = jnp.dot(cg, h_b[:, g * gw:(g + 1) * gw], preferred_element_type=F32)
        upd = lax.dot_general(bg, xw[:, g * gw:(g + 1) * gw], _TN, preferred_element_type=F32)
        yield

        def pair(j):
            ms = []
            for hh in range(2):
                hc = H_B * d + hpg * g + 2 * j + hh
                seg = acum[:, hc:hc + 1] - acum_t[hc:hc + 1, :]
                dec = jnp.exp(jnp.where(mask, seg, NEG_BIG))
                ms.append((cb * dec * dt_t[hc:hc + 1, :]).astype(BF16))
                yield
            c0 = g * gw + 2 * j * P_B
            xp = xs_b[:, c0:c0 + LANES]
            zero = jnp.zeros_like(xp)
            rhs = jnp.concatenate([jnp.where(lo_half, xp, zero), jnp.where(lo_half, zero, xp)], axis=0)
            y_intra = jnp.dot(jnp.concatenate(ms, axis=1), rhs, preferred_element_type=F32)
            yield
            y_ref[b, :, c0:c0 + LANES] = (
                y_intra + eacum_x[:, c0:c0 + LANES] * inter[:, 2 * j * P_B:2 * j * P_B + LANES])
            yield

        yield from _in_turn([pair(j) for j in range(hpg // 2)])
        h_ref[b, :, g * gw:(g + 1) * gw] = dec_x[:, g * gw:(g + 1) * gw] * h_old[:, g * gw:(g + 1) * gw] + upd
        yield

    yield from _in_turn([group(g) for g in range(G_B)])


def _ssd_scan_kernel(xf_ref, dtf_ref, xb_ref, dtb_ref, arow_ref, trif_ref, trib_ref, ef_ref, eb_ref,
                     yf_ref, yb_ref, hf_s, hb_s):
    @pl.when(pl.program_id(0) == 0)
    def _():
        hf_s[...] = jnp.zeros_like(hf_s)
        hb_s[...] = jnp.zeros_like(hb_s)

    chains = []
    for b in range(BATCH):
        chains.append(_ssd_one_direction(xf_ref[b], dtf_ref[b], arow_ref[...], trif_ref[...], ef_ref[...],
                                         hf_s, yf_ref, b, 0, False))
        chains.append(_ssd_one_direction(xb_ref[b], dtb_ref[b], arow_ref[...], trib_ref[...], eb_ref[...],
                                         hb_s, yb_ref, b, 1, True))
    _round_robin(chains)


def _ssd_scan(xact, dtp, a_log):
    q = SSD_CHUNK
    steps = PB // q
    a_neg = -jnp.exp(a_log)
    arow = jnp.concatenate([a_neg.reshape(1, 2 * H_B), jnp.zeros((1, LANES - 2 * H_B), F32)], axis=1)
    r = jnp.arange(q)
    tri_f = (r[None, :] <= r[:, None]).astype(BF16)
    tri_b = (r[None, :] >= r[:, None]).astype(BF16)
    col_head = jnp.arange(D_SSD) // P_B
    lane = jnp.arange(LANES)
    exp_f = (lane[:, None] == col_head[None, :]).astype(BF16)
    exp_b = (lane[:, None] == col_head[None, :] + H_B).astype(BF16)
    fwd = lambda s: (0, _scan_chunk(s, False, q), 0)
    bwd = lambda s: (0, _scan_chunk(s, True, q), 0)
    const = lambda s: (0, 0)
    x3 = xact.reshape(BATCH, PB, CONV_CH)
    d3 = dtp.reshape(BATCH, PB, LANES)
    yf, yb = pl.pallas_call(
        _ssd_scan_kernel,
        grid=(steps,),
        in_specs=[pl.BlockSpec((BATCH, q, CONV_CH), fwd), pl.BlockSpec((BATCH, q, LANES), fwd),
                  pl.BlockSpec((BATCH, q, CONV_CH), bwd), pl.BlockSpec((BATCH, q, LANES), bwd),
                  pl.BlockSpec((1, LANES), const),
                  pl.BlockSpec((q, q), const), pl.BlockSpec((q, q), const),
                  pl.BlockSpec((LANES, D_SSD), const), pl.BlockSpec((LANES, D_SSD), const)],
        out_specs=[pl.BlockSpec((BATCH, q, D_SSD), fwd), pl.BlockSpec((BATCH, q, D_SSD), bwd)],
        out_shape=[jax.ShapeDtypeStruct((BATCH, PB, D_SSD), F32), jax.ShapeDtypeStruct((BATCH, PB, D_SSD), F32)],
        scratch_shapes=[pltpu.VMEM((BATCH, N_B, D_SSD), F32), pltpu.VMEM((BATCH, N_B, D_SSD), F32)],
        compiler_params=_cparams("arbitrary"),
        name="ssd_scan",
    )(x3, d3, x3, d3, arow, tri_f, tri_b, exp_f, exp_b)
    return yf.reshape(T_ALL, D_SSD), yb.reshape(T_ALL, D_SSD)


def _ssd_final_kernel(yf_ref, yb_ref, xs_ref, gate_ref, d_ref, onw_ref, o_ref):
    gate = gate_ref[...]
    y = (yf_ref[...] + yb_ref[...] + d_ref[...] * xs_ref[...]) * (gate * jax.nn.sigmoid(gate))
    o_ref[...] = (_rms(y) * onw_ref[...]).astype(o_ref.dtype)


def _ssd_final(yf, yb, xact, z, d_skip, onw):
    row = lambda i: (i, 0)
    const = lambda i: (0, 0)
    return pl.pallas_call(
        _ssd_final_kernel,
        grid=(T_ALL // MM_ROWS,),
        in_specs=[pl.BlockSpec((MM_ROWS, D_SSD), row), pl.BlockSpec((MM_ROWS, D_SSD), row),
                  pl.BlockSpec((MM_ROWS, D_SSD), row),
                  pl.BlockSpec((MM_ROWS, D_SSD), lambda i: (i, Z_SZ // D_SSD)),
                  pl.BlockSpec((1, D_SSD), const), pl.BlockSpec((1, D_SSD), const)],
        out_specs=pl.BlockSpec((MM_ROWS, D_SSD), row),
        out_shape=jax.ShapeDtypeStruct((T_ALL, D_SSD), BF16),
        compiler_params=_cparams("arbitrary"),
        name="ssd_final",
    )(yf, yb, xact, z, jnp.repeat(d_skip, P_B).reshape(1, D_SSD), onw.reshape(1, D_SSD))


def _ssd_mixer(z, p, onw):
    xact, dtp = _ssd_prep(z, p["ssd_conv_w"], p["ssd_conv_b"], p["ssd_dt_bias"])
    yf, yb = _ssd_scan(xact, dtp, p["ssd_a_log"])
    return _ssd_final(yf, yb, xact, z, p["ssd_d"], onw)


def _mlstm_one_direction(q, k, v, gi, gf, bi, bf, tri, st_ref, m_ref, h_ref, b, d, rev):
    n = MLSTM_CHUNK
    li = gi + bi
    lf = jax.nn.log_sigmoid(gf + bf)
    yield
    bc = _dot3_left(tri, lf)
    yield
    b_t = bc.T
    li_t = li.T
    yield
    edge = 0 if rev else n - 1
    gtot = bc[edge:edge + 1, :]
    m_old = m_ref[b, 0:1, :]
    w_log = gtot - bc + li
    m_new = jnp.maximum(gtot + m_old, jnp.max(w_log, axis=0, keepdims=True))
    wj = jnp.exp(w_log - m_new)
    dec = jnp.exp(gtot + m_old - m_new)
    inter_log = bc + m_old
    m_ref[b, 0:1, :] = m_new
    yield
    ri = lax.broadcasted_iota(jnp.int32, (n, n), 0)
    ci = lax.broadcasted_iota(jnp.int32, (n, n), 1)
    mask = (ci >= ri) if rev else (ci <= ri)
    lo_half = lax.broadcasted_iota(jnp.int32, (n, LANES), 1) < DQK_C
    row_lo = lax.broadcasted_iota(jnp.int32, (2 * DQK_C, 2 * DV_C), 0) < DQK_C
    ones = jnp.ones((n, DV_C), F32)
    st_old = [st_ref[b, j] for j in range(H_C // 2)]
    upds = {}

    def head(h):
        j, hh = divmod(h, 2)
        gl = GATE_LANE0 + H_C * d + h
        qp = q[:, j * LANES:(j + 1) * LANES] * DQK_C ** -0.5
        kp = k[:, j * LANES:(j + 1) * LANES].astype(BF16)
        qm = jnp.where(lo_half if hh == 0 else jnp.logical_not(lo_half), qp, 0.0).astype(BF16)
        qk = lax.dot_general(qm, kp, _NT, preferred_element_type=F32)
        qs = jnp.dot(qm, st_old[j].astype(BF16), preferred_element_type=F32)
        yield
        dmat = jnp.where(mask, bc[:, gl:gl + 1] - b_t[gl:gl + 1, :] + li_t[gl:gl + 1, :], NEG_BIG)
        il = inter_log[:, gl:gl + 1]
        m_row = jnp.maximum(il, jnp.max(dmat, axis=1, keepdims=True))
        yield
        s = qk * jnp.exp(dmat - m_row)
        w_inter = jnp.exp(il - m_row)
        vh = v[:, h * DV_C:(h + 1) * DV_C]
        yield
        num = jnp.dot(s.astype(BF16), vh.astype(BF16), preferred_element_type=F32) + w_inter * qs[:, :DV_C]
        den = jnp.sum(s, axis=1, keepdims=True) + w_inter * qs[:, DV_C:]
        yield
        h_ref[b, :, h * DV_C:(h + 1) * DV_C] = num / jnp.maximum(jnp.abs(den), jnp.exp(-m_row))
        rhs = (wj[:, gl:gl + 1] * jnp.concatenate([vh, ones], axis=1)).astype(BF16)
        upds[h] = lax.dot_general(kp, rhs, _TN, preferred_element_type=F32)
        yield

    yield from _in_turn([head(h) for h in range(H_C)])
    for j in range(H_C // 2):
        ga = GATE_LANE0 + H_C * d + 2 * j
        decv = jnp.where(row_lo, dec[:, ga:ga + 1], dec[:, ga + 1:ga + 2])
        st_ref[b, j] = decv * st_old[j] + jnp.where(row_lo, upds[2 * j], upds[2 * j + 1])
        yield


def _mlstm_scan_kernel(qf, kf, vf, gif, gff, qb, kb, vb, gib, gfb, bi_ref, bf_ref, trif_ref, trib_ref,
                       hf_ref, hb_ref, stf_s, stb_s, mf_s, mb_s):
    @pl.when(pl.program_id(0) == 0)
    def _():
        stf_s[...] = jnp.zeros_like(stf_s)
        stb_s[...] = jnp.zeros_like(stb_s)
        mf_s[...] = jnp.full_like(mf_s, NEG_STATE)
        mb_s[...] = jnp.full_like(mb_s, NEG_STATE)

    chains = []
    for b in range(BATCH):
        chains.append(_mlstm_one_direction(qf[b], kf[b], vf[b], gif[b], gff[b], bi_ref[...], bf_ref[...],
                                           trif_ref[...], stf_s, mf_s, hf_ref, b, 0, False))
        chains.append(_mlstm_one_direction(qb[b], kb[b], vb[b], gib[b], gfb[b], bi_ref[...], bf_ref[...],
                                           trib_ref[...], stb_s, mb_s, hb_ref, b, 1, True))
    _round_robin(chains)


def _mlstm_scan(z, gate_b):
    n = MLSTM_CHUNK
    steps = PB // n
    qkw = H_C * DQK_C
    pad = lambda t: jnp.concatenate([jnp.zeros((1, GATE_LANE0), F32), t.reshape(1, 2 * H_C),
                                     jnp.zeros((1, LANES - GATE_LANE0 - 2 * H_C), F32)], axis=1)
    bi = pad(gate_b[:, 0, :])
    bf = pad(gate_b[:, 1, :])
    r = jnp.arange(n)
    tri_f = (r[None, :] <= r[:, None]).astype(BF16)
    tri_b = (r[None, :] >= r[:, None]).astype(BF16)
    z3 = z.reshape(BATCH, PB, Z_COLS)

    def specs(rev):
        ch = lambda s: _scan_chunk(s, rev, n)
        return [pl.BlockSpec((BATCH, n, qkw), lambda s: (0, ch(s), Z_MQ // qkw)),
                pl.BlockSpec((BATCH, n, qkw), lambda s: (0, ch(s), Z_MK // qkw)),
                pl.BlockSpec((BATCH, n, D_MLSTM), lambda s: (0, ch(s), Z_MV // D_MLSTM)),
                pl.BlockSpec((BATCH, n, LANES), lambda s: (0, ch(s), Z_MG // LANES)),
                pl.BlockSpec((BATCH, n, LANES), lambda s: (0, ch(s), Z_DT // LANES))]

    const = lambda s: (0, 0)
    n_pairs = H_C // 2
    hf, hb = pl.pallas_call(
        _mlstm_scan_kernel,
        grid=(steps,),
        in_specs=specs(False) + specs(True) + [
            pl.BlockSpec((1, LANES), const), pl.BlockSpec((1, LANES), const),
            pl.BlockSpec((n, n), const), pl.BlockSpec((n, n), const)],
        out_specs=[pl.BlockSpec((BATCH, n, D_MLSTM), lambda s: (0, _scan_chunk(s, False, n), 0)),
                   pl.BlockSpec((BATCH, n, D_MLSTM), lambda s: (0, _scan_chunk(s, True, n), 0))],
        out_shape=[jax.ShapeDtypeStruct((BATCH, PB, D_MLSTM), F32), jax.ShapeDtypeStruct((BATCH, PB, D_MLSTM), F32)],
        scratch_shapes=[pltpu.VMEM((BATCH, n_pairs, 2 * DQK_C, 2 * DV_C), F32),
                        pltpu.VMEM((BATCH, n_pairs, 2 * DQK_C, 2 * DV_C), F32),
                        pltpu.VMEM((BATCH, 8, LANES), F32), pltpu.VMEM((BATCH, 8, LANES), F32)],
        compiler_params=_cparams("arbitrary"),
        name="mlstm_scan",
    )(z3, z3, z3, z3, z3, z3, z3, z3, z3, z3, bi, bf, tri_f, tri_b)
    return hf.reshape(T_ALL, D_MLSTM), hb.reshape(T_ALL, D_MLSTM)


def _mlstm_final_kernel(hf_ref, hb_ref, og_ref, onw_ref, o_ref):
    gate = jax.nn.sigmoid(og_ref[...])
    onw = onw_ref[...]
    for h in range(H_C):
        cs = slice(h * DV_C, (h + 1) * DV_C)
        hn = _rms(hf_ref[:, cs] + hb_ref[:, cs])
        o_ref[:, cs] = (hn * gate[:, cs] * onw[:, cs]).astype(o_ref.dtype)


def _mlstm_final(hf, hb, z, onw):
    row = lambda i: (i, 0)
    return pl.pallas_call(
        _mlstm_final_kernel,
        grid=(T_ALL // MM_ROWS,),
        in_specs=[pl.BlockSpec((MM_ROWS, D_MLSTM), row), pl.BlockSpec((MM_ROWS, D_MLSTM), row),
                  pl.BlockSpec((MM_ROWS, D_MLSTM), lambda i: (i, Z_MO // D_MLSTM)),
                  pl.BlockSpec((1, D_MLSTM), lambda i: (0, 0))],
        out_specs=pl.BlockSpec((MM_ROWS, D_MLSTM), row),
        out_shape=jax.ShapeDtypeStruct((T_ALL, D_MLSTM), BF16),
        compiler_params=_cparams("arbitrary"),
        name="mlstm_final",
    )(hf, hb, z, onw.reshape(1, D_MLSTM))


def _mlstm_mixer(z, p, onw):
    hf, hb = _mlstm_scan(z, p["mlstm_gate_b"])
    return _mlstm_final(hf, hb, z, onw)


S5_Q = 256
S5_SEG = S5_Q // 8
S5_LANES = G_S5 * P_S5
S5_SLAB = 512
S5_NSLAB = S5_LANES // S5_SLAB


def _s5_dir_kernel(u_ref, perm_ref, wbr_ref, wbi_ref, atab_ref, apr_ref, api_ref, wcr_ref, wci_ref, y_ref,
                   ur_s, ui_s, xr_s, xi_s, st_s, car_s, *, reverse):
    @pl.when(pl.program_id(0) == 0)
    def _():
        st_s[...] = jnp.zeros_like(st_s)

    for b in range(BATCH):
        up = jnp.dot(perm_ref[...], u_ref[b].astype(BF16), preferred_element_type=F32).astype(BF16)
        for m in range(S5_NSLAB):
            um = up[:, m * LANES:(m + 1) * LANES]
            ur_s[b, :, m * S5_SLAB:(m + 1) * S5_SLAB] = jnp.dot(um, wbr_ref[m], preferred_element_type=F32)
            ui_s[b, :, m * S5_SLAB:(m + 1) * S5_SLAB] = jnp.dot(um, wbi_ref[m], preferred_element_type=F32)

    per = 4
    for grp in range(S5_LANES // (per * LANES)):
        cols = [grp * per * LANES + j * LANES for j in range(per)]
        a_r = [atab_ref[0:8, c0:c0 + LANES] for c0 in cols]
        a_i = [atab_ref[8:16, c0:c0 + LANES] for c0 in cols]

        def body(i, carry, cols=cols, a_r=a_r, a_i=a_i):
            t = (S5_SEG - 1 - i) if reverse else i
            r0 = pl.multiple_of(t * 8, 8)
            new = []
            for b in range(BATCH):
                for j, c0 in enumerate(cols):
                    xr, xi = carry[2 * (b * per + j)], carry[2 * (b * per + j) + 1]
                    nr = a_r[j] * xr - a_i[j] * xi + ur_s[b, pl.ds(r0, 8), c0:c0 + LANES]
                    ni = a_r[j] * xi + a_i[j] * xr + ui_s[b, pl.ds(r0, 8), c0:c0 + LANES]
                    ur_s[b, pl.ds(r0, 8), c0:c0 + LANES] = nr
                    ui_s[b, pl.ds(r0, 8), c0:c0 + LANES] = ni
                    new += [nr, ni]
            return tuple(new)

        lax.fori_loop(0, S5_SEG, body, tuple(jnp.zeros((8, LANES), F32) for _ in range(2 * per * BATCH)), unroll=2)

    as_r = atab_ref[16:17, :]
    as_i = atab_ref[17:18, :]
    end_row = 0 if reverse else 8 * (S5_SEG - 1)
    for b in range(BATCH):
        cr = st_s[b, 0:1, :]
        ci = st_s[b, 1:2, :]
        for k in (range(7, -1, -1) if reverse else range(8)):
            car_s[b, k:k + 1, :] = cr
            car_s[b, 8 + k:9 + k, :] = ci
            er = ur_s[b, end_row + k:end_row + k + 1, :]
            ei = ui_s[b, end_row + k:end_row + k + 1, :]
            cr, ci = er + as_r * cr - as_i * ci, ei + as_r * ci + as_i * cr
        st_s[b, 0:1, :] = cr
        st_s[b, 1:2, :] = ci

    for b in range(BATCH):
        for m in range(S5_NSLAB):
            cs = slice(m * S5_SLAB, (m + 1) * S5_SLAB)
            c_r = jnp.concatenate([car_s[b, 0:8, cs], car_s[b, 0:8, cs]], axis=0)
            c_i = jnp.concatenate([car_s[b, 8:16, cs], car_s[b, 8:16, cs]], axis=0)

            def fix(i, _, b=b, cs=cs, c_r=c_r, c_i=c_i):
                r0 = pl.multiple_of(i * 16, 16)
                p_r = apr_ref[pl.ds(r0, 16), cs]
                p_i = api_ref[pl.ds(r0, 16), cs]
                xr_s[b, pl.ds(r0, 16), cs] = (ur_s[b, pl.ds(r0, 16), cs] + p_r * c_r - p_i * c_i).astype(BF16)
                xi_s[b, pl.ds(r0, 16), cs] = (ui_s[b, pl.ds(r0, 16), cs] + p_r * c_i + p_i * c_r).astype(BF16)
                return 0

            lax.fori_loop(0, S5_Q // 16, fix, 0, unroll=2)

    for b in range(BATCH):
        for m in range(S5_NSLAB):
            cs = slice(m * S5_SLAB, (m + 1) * S5_SLAB)
            y_ref[b, :, m * LANES:(m + 1) * LANES] = (
                jnp.dot(xr_s[b, :, cs], wcr_ref[m], preferred_element_type=F32)
                - jnp.dot(xi_s[b, :, cs], wci_ref[m], preferred_element_type=F32))


def _s5_direction(z3, perm, wbr, wbi, atab, apr, api, wcr, wci, reverse):
    steps = PB // S5_Q
    chunk = lambda s: _scan_chunk(s, reverse, S5_Q)
    const2 = lambda s: (0, 0)
    const3 = lambda s: (0, 0, 0)
    return pl.pallas_call(
        functools.partial(_s5_dir_kernel, reverse=reverse),
        grid=(steps,),
        in_specs=[pl.BlockSpec((BATCH, S5_Q, D_S5), lambda s: (0, chunk(s), Z_D // D_S5)),
                  pl.BlockSpec((S5_Q, S5_Q), const2),
                  pl.BlockSpec(wbr.shape, const3),
                  pl.BlockSpec(wbi.shape, const3),
                  pl.BlockSpec(atab.shape, const2),
                  pl.BlockSpec(apr.shape, const2),
                  pl.BlockSpec(api.shape, const2),
                  pl.BlockSpec(wcr.shape, const3),
                  pl.BlockSpec(wci.shape, const3)],
        out_specs=pl.BlockSpec((BATCH, S5_Q, D_S5), lambda s: (0, chunk(s), 0)),
        out_shape=jax.ShapeDtypeStruct((BATCH, PB, D_S5), F32),
        scratch_shapes=[pltpu.VMEM((BATCH, S5_Q, S5_LANES), F32), pltpu.VMEM((BATCH, S5_Q, S5_LANES), F32),
                        pltpu.VMEM((BATCH, S5_Q, S5_LANES), BF16), pltpu.VMEM((BATCH, S5_Q, S5_LANES), BF16),
                        pltpu.VMEM((BATCH, 8, S5_LANES), F32), pltpu.VMEM((BATCH, 16, S5_LANES), F32)],
        compiler_params=_cparams("arbitrary"),
        name="s5_bwd" if reverse else "s5_fwd",
    )(z3, perm, wbr, wbi, atab, apr, api, wcr, wci).reshape(T_ALL, D_S5)


def _s5_final_kernel(yf_ref, yb_ref, u_ref, permt_ref, d_ref, gw_ref, gb_ref, onw_ref, o_ref):
    y = _dot3_left(permt_ref[...], yf_ref[...] + yb_ref[...])
    y = jax.nn.gelu(y + d_ref[...] * u_ref[...])
    gate = jax.nn.sigmoid(jnp.dot(y.astype(BF16), gw_ref[...], preferred_element_type=F32) + gb_ref[...])
    o_ref[...] = (_rms(y * gate) * onw_ref[...]).astype(o_ref.dtype)


def _s5_final(yf, yb, z, permt, d_skip, glu_w, glu_b, onw):
    row = lambda i: (i, 0)
    const = lambda i: (0, 0)
    return pl.pallas_call(
        _s5_final_kernel,
        grid=(T_ALL // S5_Q,),
        in_specs=[pl.BlockSpec((S5_Q, D_S5), row),
                  pl.BlockSpec((S5_Q, D_S5), row),
                  pl.BlockSpec((S5_Q, D_S5), lambda i: (i, Z_D // D_S5)),
                  pl.BlockSpec((S5_Q, S5_Q), const),
                  pl.BlockSpec((1, D_S5), const),
                  pl.BlockSpec((D_S5, D_S5), const),
                  pl.BlockSpec((1, D_S5), const),
                  pl.BlockSpec((1, D_S5), const)],
        out_specs=pl.BlockSpec((S5_Q, D_S5), row),
        out_shape=jax.ShapeDtypeStruct((T_ALL, D_S5), BF16),
        compiler_params=_cparams("arbitrary"),
        name="s5_final",
    )(yf, yb, z, permt, d_skip.reshape(1, D_S5), glu_w.astype(BF16), glu_b.reshape(1, D_S5), onw.reshape(1, D_S5))


def _s5_tables(lam_re, lam_im, log_dt, b_re, b_im, c_re, c_im):
    dt = jnp.exp(log_dt)[..., None]
    mag = jnp.exp(lam_re * dt)
    ar = mag * jnp.cos(lam_im * dt)
    ai = mag * jnp.sin(lam_im * dt)
    den = lam_re * lam_re + lam_im * lam_im
    cr_ = ((ar - 1.0) * lam_re + ai * lam_im) / den
    ci_ = (ai * lam_re - (ar - 1.0) * lam_im) / den
    bbr = cr_[..., None] * b_re - ci_[..., None] * b_im
    bbi = cr_[..., None] * b_im + ci_[..., None] * b_re
    gps = S5_SLAB // P_S5
    eye = jnp.eye(gps, dtype=F32)

    def drive_w(bb):
        t = bb.reshape(S5_NSLAB, gps, P_S5, S5_GROUP)
        w = jnp.einsum('mgpc,gh->mgchp', t, eye)
        return w.reshape(S5_NSLAB, gps * S5_GROUP, gps * P_S5).astype(BF16)

    def read_w(cc):
        t = cc.reshape(S5_NSLAB, gps, S5_GROUP, P_S5)
        w = jnp.einsum('mgcp,gh->mgphc', t, eye)
        return w.reshape(S5_NSLAB, gps * P_S5, gps * S5_GROUP).astype(BF16)

    steps = jnp.arange(1, S5_SEG + 1, dtype=F32)[:, None]
    out = []
    for d in range(2):
        decay = (lam_re[d] * dt[d]).reshape(1, S5_LANES)
        angle = (lam_im[d] * dt[d]).reshape(1, S5_LANES)
        pmag = jnp.exp(steps * decay)
        pr = pmag * jnp.cos(steps * angle)
        pi = pmag * jnp.sin(steps * angle)
        if d == 1:
            apr, api = jnp.repeat(pr[::-1], 8, axis=0), jnp.repeat(pi[::-1], 8, axis=0)
        else:
            apr, api = jnp.repeat(pr, 8, axis=0), jnp.repeat(pi, 8, axis=0)
        atab = jnp.concatenate([jnp.broadcast_to(pr[0:1], (8, S5_LANES)), jnp.broadcast_to(pi[0:1], (8, S5_LANES)),
                                pr[S5_SEG - 1:], pi[S5_SEG - 1:], jnp.zeros((6, S5_LANES), F32)], axis=0)
        out.append((drive_w(bbr[d]), drive_w(bbi[d]), atab, apr, api))
    r = jnp.arange(S5_Q)
    src = (r % 8) * S5_SEG + r // 8
    perm = (src[:, None] == jnp.arange(S5_Q)[None, :]).astype(BF16)
    return out, read_w(c_re), read_w(c_im), perm


def _s5_mixer(z, p, onw):
    dirs, wcr, wci, perm = _s5_tables(p["s5_lam_re"], p["s5_lam_im"], p["s5_log_dt"], p["s5_b_re"], p["s5_b_im"],
                                      p["s5_c_re"], p["s5_c_im"])
    z3 = z.reshape(BATCH, PB, Z_COLS)
    yf = _s5_direction(z3, perm, *dirs[0], wcr, wci, reverse=False)
    yb = _s5_direction(z3, perm, *dirs[1], wcr, wci, reverse=True)
    return _s5_final(yf, yb, z, perm.T, p["s5_d"], p["s5_glu_w"], p["s5_glu_b"], onw)


def _rope_tables():
    pos = np.arange(SEQ)
    row = (pos // GRID_W).astype(np.float32)
    col = (pos % GRID_W).astype(np.float32)
    inv_freq = (ROPE_BASE ** (-np.arange(ROPE_AXIS // 2, dtype=np.float32) * 2.0 / ROPE_AXIS)).astype(np.float32)
    ang_r = row[:, None] * inv_freq
    ang_c = col[:, None] * inv_freq
    zeros = np.zeros((SEQ, LANES - D_ROPE), np.float32)
    cos = np.concatenate([np.cos(ang_r), np.cos(ang_r), np.cos(ang_c), np.cos(ang_c), zeros], axis=1)
    sin = np.concatenate([np.sin(ang_r), np.sin(ang_r), np.sin(ang_c), np.sin(ang_c), zeros], axis=1)
    cos_c = np.concatenate([np.ones((TILE, D_ROPE), np.float32), np.zeros((TILE, LANES - D_ROPE), np.float32)], axis=1)
    sin_c = np.zeros((TILE, LANES), np.float32)
    table = np.concatenate([np.concatenate([cos, sin], axis=1), np.concatenate([cos_c, sin_c], axis=1)], axis=0)
    return jnp.asarray(table.astype(np.float32))


def _layout_mla(w_uq, w_ukv):
    k = w_uq.shape[0]
    qa, qb, wk, wv = [], [], [], []
    for h in range(H_A):
        base = h * (D_NOPE + D_ROPE)
        rope = w_uq[:, base + D_NOPE:base + D_NOPE + D_ROPE]
        qa += [w_uq[:, base:base + D_NOPE], rope, jnp.zeros((k, LANES - D_ROPE), w_uq.dtype)]
        qb += [_rot_cols(rope), jnp.zeros((k, LANES - D_ROPE), w_uq.dtype)]
        kb = h * (D_NOPE + D_V)
        wk.append(w_ukv[:, kb:kb + D_NOPE])
        wv.append(w_ukv[:, kb + D_NOPE:kb + D_NOPE + D_V])
    cat = lambda xs: jnp.concatenate(xs, axis=1).astype(BF16)
    return cat(qa), cat(qb), cat(wk), cat(wv)


def _layer(xall, hx, modtab, p, big, l, cs, norm_w, next_modtab, last):
    z = _mm(hx, _layout_w_in(big["w_in"], l), MM_ROWS, Z_COLS // 3)
    onw = p["out_norm_w"]
    wqa, wqb, wk, wv = _layout_mla(p["mla_w_uq"], p["mla_w_ukv"])
    q, k, v = _mla_prep(z, cs, p["mla_q_norm_w"], p["mla_kv_norm_w"], wqa, wqb, wk, wv)
    ya = _attention(q, k, v, onw[:D_MLA].reshape(1, D_MLA))
    yb = _ssd_mixer(z, p, onw[D_MLA:D_MLA + D_SSD])
    yc = _mlstm_mixer(z, p, onw[D_MLA + D_SSD:D_MLA + D_SSD + D_MLSTM])
    yd = _s5_mixer(z, p, onw[D_MLA + D_SSD + D_MLSTM:])
    xall = _outproj([ya, yb, yc, yd], big["w_out"], l, xall, modtab, 2)

    w_router = jnp.concatenate([p["moe_w_group"], p["moe_w_expert"],
                                jnp.zeros((D_MODEL, ROUTER_COLS - N_GROUPS - N_EXPERTS), F32)], axis=1)
    wr_hi = w_router.astype(BF16)
    wr_lo = (w_router - wr_hi.astype(F32)).astype(BF16)
    h2, route, route_t, cnt = _prenorm_router(xall, p["norm2_w"], modtab, jnp.stack([wr_hi, wr_lo]), 3, 4, last)
    return _moe(xall, h2, route, route_t, cnt, modtab, big["moe_w_gate"], big["moe_w_up"], big["moe_w_down"], l,
                norm_w, next_modtab, last)


def kernel(x, c, ctx, c_ctx, mod_w, mod_b, norm1_w, w_in, mla_q_norm_w, mla_kv_norm_w, mla_w_uq, mla_w_ukv,
           ssd_conv_w, ssd_conv_b, ssd_a_log, ssd_dt_bias, ssd_d, mlstm_gate_b, s5_lam_re, s5_lam_im,
           s5_log_dt, s5_b_re, s5_b_im, s5_c_re, s5_c_im, s5_d, s5_glu_w, s5_glu_b, out_norm_w, w_out,
           norm2_w, moe_w_group, moe_w_expert, moe_w_gate, moe_w_up, moe_w_down, final_norm_w):
    stacked = {"norm1_w": norm1_w, "mla_q_norm_w": mla_q_norm_w, "mla_kv_norm_w": mla_kv_norm_w,
               "mla_w_uq": mla_w_uq, "mla_w_ukv": mla_w_ukv, "ssd_conv_w": ssd_conv_w, "ssd_conv_b": ssd_conv_b,
               "ssd_a_log": ssd_a_log, "ssd_dt_bias": ssd_dt_bias, "ssd_d": ssd_d, "mlstm_gate_b": mlstm_gate_b,
               "s5_lam_re": s5_lam_re, "s5_lam_im": s5_lam_im, "s5_log_dt": s5_log_dt, "s5_b_re": s5_b_re,
               "s5_b_im": s5_b_im, "s5_c_re": s5_c_re, "s5_c_im": s5_c_im, "s5_d": s5_d, "s5_glu_w": s5_glu_w,
               "s5_glu_b": s5_glu_b, "out_norm_w": out_norm_w, "norm2_w": norm2_w,
               "moe_w_group": moe_w_group, "moe_w_expert": moe_w_expert}
    big = {"w_in": w_in, "w_out": w_out, "moe_w_gate": moe_w_gate, "moe_w_up": moe_w_up, "moe_w_down": moe_w_down}
    cs = _rope_tables()
    cc = jnp.concatenate([c, c_ctx[None, :], jnp.zeros((8 - BATCH - 1, D_MODEL), F32)], axis=0)
    modtabs = [_modulation(cc, mod_w, mod_b, l)[:BATCH + 1].reshape(BATCH + 1, 6, D_MODEL) for l in range(DEPTH)]
    xall, hx = _prenorm(x, ctx, norm1_w[0], modtabs[0], 0, 1)
    for l in range(DEPTH):
        p = {name: val[l] for name, val in stacked.items()}
        if l == DEPTH - 1:
            out = _layer(xall, hx, modtabs[l], p, big, l, cs, final_norm_w, modtabs[l], True)
        else:
            xall, hx = _layer(xall, hx, modtabs[l], p, big, l, cs, norm1_w[l + 1], modtabs[l + 1], False)
    return out.reshape(BATCH, SEQ, D_MODEL)
```

```python
import functools
import math

import jax
import jax.numpy as jnp
import numpy as np
from jax import lax
from jax.experimental import pallas as pl
from jax.experimental.pallas import tpu as pltpu

F32 = jnp.float32
BF16 = jnp.bfloat16

D_MODEL = 2048
BATCH = 2
SEQ = 4096
DEPTH = 2
GRID_W = 64
CTX_LEN = 256
EPS = 1e-6
NEG_STATE = -1e30
NEG_BIG = -1e30

H_A = 4
D_NOPE = 128
D_ROPE = 64
D_V = 128
Q_RANK = 384
KV_RANK = 128
ROPE_AXIS = D_ROPE // 2
ROPE_BASE = 10000.0
D_MLA = H_A * D_V
D_SSD = 512
P_B = 64
H_B = D_SSD // P_B
G_B = 2
N_B = 128
SSD_CONV = 5
SSD_CHUNK = 128
CONV_CH = D_SSD + 2 * G_B * N_B
D_MLSTM = 512
H_C = 4
DV_C = D_MLSTM // H_C
DQK_C = DV_C // 2
MLSTM_CHUNK = 128
D_S5 = 512
S5_GROUP = 16
G_S5 = D_S5 // S5_GROUP
P_S5 = 64
A_COLS = Q_RANK + KV_RANK + D_ROPE
B_COLS = D_SSD + CONV_CH + 2 * H_B
C_COLS = 2 * H_C * DQK_C + 2 * D_MLSTM + 4 * H_C
N_GROUPS = 4
EXPERTS_PER_GROUP = 8
N_EXPERTS = N_GROUPS * EXPERTS_PER_GROUP
TOP_K = 2
D_EXPERT = 512

PB = CTX_LEN + SEQ
T_X = BATCH * SEQ
T_ALL = BATCH * PB

LANES = 128
VMEM_LIMIT_BYTES = 56 * 1024 * 1024

TILE = 256
TPB = PB // TILE
XT = SEQ // TILE
MM_ROWS = 512

Z_CQ = 0
Z_CKV = 384
Z_KR = 512
Z_KRR = 640
Z_MQ = 768
Z_XBC = 1024
Z_SZ = 2048
Z_MV = 2560
Z_MO = 3072
Z_D = 3584
Z_MK = 4096
Z_DT = 4352
Z_MG = 4480
Z_COLS = 4608
GATE_LANE0 = 2 * H_B

MOE_ROWS = 256
ROUTER_COLS = 128

_NT = (((1,), (1,)), ((), ()))
_TN = (((0,), (0,)), ((), ()))


def _cparams(*sem):
    return pltpu.CompilerParams(dimension_semantics=sem, vmem_limit_bytes=VMEM_LIMIT_BYTES)


def _tile_mod_row(i):
    return jnp.where(i % TPB == 0, BATCH, i // TPB)


def _token_tile(i, latent_only):
    return (i // XT) * TPB + 1 + i % XT if latent_only else i


def _rms(x, w=None):
    y = x * lax.rsqrt(jnp.mean(x * x, axis=-1, keepdims=True) + EPS)
    return y if w is None else y * w


def _in_turn(chains):
    live = list(chains)
    while live:
        nxt = []
        for chain in live:
            try:
                next(chain)
                nxt.append(chain)
            except StopIteration:
                pass
        live = nxt
        yield


def _round_robin(chains):
    for _ in _in_turn(chains):
        pass


def _split3(x):
    hi = x.astype(BF16)
    r1 = x - hi.astype(F32)
    mid = r1.astype(BF16)
    lo = (r1 - mid.astype(F32)).astype(BF16)
    return hi, mid, lo


def _dot3_left(sel, x):
    hi, mid, lo = _split3(x)
    return (jnp.dot(sel, hi, preferred_element_type=F32) + jnp.dot(sel, mid, preferred_element_type=F32)
            + jnp.dot(sel, lo, preferred_element_type=F32))


def _dot3_right(x, sel):
    hi, mid, lo = _split3(x)
    return (jnp.dot(hi, sel, preferred_element_type=F32) + jnp.dot(mid, sel, preferred_element_type=F32)
            + jnp.dot(lo, sel, preferred_element_type=F32))


def _mod_kernel(a_ref, w_ref, b_ref, o_ref):
    a = a_ref[...]
    a = a * jax.nn.sigmoid(a)
    o_ref[...] = jnp.dot(a.astype(BF16), w_ref[...].astype(BF16), preferred_element_type=F32) + b_ref[...]


def _modulation(cc, mod_w, mod_b, l):
    n = mod_w.shape[2]
    tn = 1024
    return pl.pallas_call(
        _mod_kernel,
        grid=(n // tn,),
        in_specs=[pl.BlockSpec((8, D_MODEL), lambda j: (0, 0)),
                  pl.BlockSpec((None, D_MODEL, tn), lambda j: (l, 0, j)),
                  pl.BlockSpec((None, 1, tn), lambda j: (l, 0, j))],
        out_specs=pl.BlockSpec((8, tn), lambda j: (0, j)),
        out_shape=jax.ShapeDtypeStruct((8, n), F32),
        compiler_params=_cparams("arbitrary"),
        name="modulation",
    )(cc, mod_w, mod_b.reshape(DEPTH, 1, n))


def _prenorm_kernel(x_ref, c_ref, w_ref, mod_ref, xall_ref, o_ref, *, sh_idx, sc_idx):
    v = jnp.where(pl.program_id(0) % TPB == 0, c_ref[...], x_ref[...])
    xall_ref[...] = v
    y = _rms(v, w_ref[...])
    m = mod_ref[0]
    y = y * (1.0 + m[sc_idx:sc_idx + 1, :]) + m[sh_idx:sh_idx + 1, :]
    o_ref[...] = y.astype(o_ref.dtype)


def _prenorm(x, ctx, w, modtab, sh_idx, sc_idx):
    row = lambda i: (i, 0)
    return pl.pallas_call(
        functools.partial(_prenorm_kernel, sh_idx=sh_idx, sc_idx=sc_idx),
        grid=(T_ALL // TILE,),
        in_specs=[pl.BlockSpec((TILE, D_MODEL), lambda i: ((i // TPB) * XT + jnp.maximum(i % TPB - 1, 0), 0)),
                  pl.BlockSpec((TILE, D_MODEL), lambda i: (i // TPB, 0)),
                  pl.BlockSpec((1, D_MODEL), lambda i: (0, 0)),
                  pl.BlockSpec((1, 6, D_MODEL), lambda i: (_tile_mod_row(i), 0, 0))],
        out_specs=[pl.BlockSpec((TILE, D_MODEL), row), pl.BlockSpec((TILE, D_MODEL), row)],
        out_shape=[jax.ShapeDtypeStruct((T_ALL, D_MODEL), F32), jax.ShapeDtypeStruct((T_ALL, D_MODEL), BF16)],
        compiler_params=_cparams("arbitrary"),
        name="prenorm",
    )(x.reshape(T_X, D_MODEL), ctx.reshape(BATCH * CTX_LEN, D_MODEL), w.reshape(1, D_MODEL), modtab)


def _mm_kernel(a_ref, w_ref, o_ref):
    o_ref[...] = jnp.dot(a_ref[...], w_ref[...], preferred_element_type=F32).astype(o_ref.dtype)


def _mm(a, w, tm, tn, out_dtype=F32):
    m, k = a.shape
    n = w.shape[1]
    return pl.pallas_call(
        _mm_kernel,
        grid=(n // tn, m // tm),
        in_specs=[pl.BlockSpec((tm, k), lambda j, i: (i, 0)),
                  pl.BlockSpec((k, tn), lambda j, i: (0, j))],
        out_specs=pl.BlockSpec((tm, tn), lambda j, i: (i, j)),
        out_shape=jax.ShapeDtypeStruct((m, n), out_dtype),
        compiler_params=_cparams("arbitrary", "arbitrary"),
        name="in_proj",
    )(a, w)


W_IN_B0 = A_COLS
W_IN_C0 = A_COLS + B_COLS
W_IN_D0 = A_COLS + B_COLS + C_COLS
W_IN_CM = W_IN_C0 + 2 * H_C * DQK_C
W_IN_GB = W_IN_CM + 2 * D_MLSTM
W_IN_WIDE = ((Z_CQ, 0, Q_RANK + KV_RANK),
             (Z_MQ, W_IN_C0, H_C * DQK_C),
             (Z_XBC, W_IN_B0 + D_SSD, CONV_CH),
             (Z_SZ, W_IN_B0, D_SSD),
             (Z_MV, W_IN_CM, 2 * D_MLSTM),
             (Z_D, W_IN_D0, D_S5),
             (Z_MK, W_IN_C0 + H_C * DQK_C, H_C * DQK_C))
W_IN_ROWS = 256


def _rot_cols(w):
    q = ROPE_AXIS // 2
    return jnp.concatenate([-w[:, q:2 * q], w[:, 0:q], -w[:, 3 * q:4 * q], w[:, 2 * q:3 * q]], axis=1)


W_IN_KR0 = Q_RANK + KV_RANK
W_IN_DT0 = W_IN_B0 + D_SSD + CONV_CH
W_IN_WINDOWS = tuple(c // LANES * LANES for c in (W_IN_KR0, W_IN_DT0, W_IN_GB))


def _narrow_selector():
    sel = np.zeros((3 * LANES, 4 * LANES), np.float32)

    def put(window, src_col, dst_col, width, sign=1.0):
        base = window * LANES + src_col - W_IN_WINDOWS[window]
        for c in range(width):
            sel[base + c, dst_col + c] = sign

    q = ROPE_AXIS // 2
    put(0, W_IN_KR0, 0, D_ROPE)
    for dst, src, sign in ((0, q, -1.0), (q, 0, 1.0), (2 * q, 3 * q, -1.0), (3 * q, 2 * q, 1.0)):
        put(0, W_IN_KR0 + src, LANES + dst, q, sign)
    put(1, W_IN_DT0, 2 * LANES, 2 * H_B)
    for d in range(2):
        put(2, W_IN_GB + (2 * d + 1) * H_C, 2 * LANES + GATE_LANE0 + d * H_C, H_C)
        put(2, W_IN_GB + 2 * d * H_C, 3 * LANES + GATE_LANE0 + d * H_C, H_C)
    return jnp.asarray(sel, dtype=BF16)


def _w_in_layout_kernel(w_ref, sel_ref, o_ref):
    for dst, src, width in W_IN_WIDE:
        o_ref[:, dst:dst + width] = w_ref[:, src:src + width].astype(BF16)
    windows = jnp.concatenate([w_ref[:, c:c + LANES] for c in W_IN_WINDOWS], axis=1).astype(BF16)
    small = jnp.dot(windows, sel_ref[...], preferred_element_type=F32).astype(BF16)
    o_ref[:, Z_KR:Z_MQ] = small[:, 0:2 * LANES]
    o_ref[:, Z_DT:Z_COLS] = small[:, 2 * LANES:4 * LANES]


def _layout_w_in(w_in, l):
    _, k, n = w_in.shape
    sel = _narrow_selector()
    return pl.pallas_call(
        _w_in_layout_kernel,
        grid=(k // W_IN_ROWS,),
        in_specs=[pl.BlockSpec((None, W_IN_ROWS, n), lambda i: (l, i, 0)),
                  pl.BlockSpec(sel.shape, lambda i: (0, 0))],
        out_specs=pl.BlockSpec((W_IN_ROWS, Z_COLS), lambda i: (i, 0)),
        out_shape=jax.ShapeDtypeStruct((k, Z_COLS), BF16),
        compiler_params=_cparams("arbitrary"),
        name="w_in_layout",
    )(w_in, sel)


def _mla_prep_kernel(za_ref, cs_ref, qw_ref, kvw_ref, wqa_ref, wqb_ref, wk_ref, wv_ref, q_ref, k_ref, v_ref):
    za = za_ref[...]
    cos = cs_ref[:, :LANES]
    sin = cs_ref[:, LANES:]
    qn = _rms(za[:, Z_CQ:Z_CQ + Q_RANK], qw_ref[...]).astype(BF16)
    kvn = _rms(za[:, Z_CKV:Z_CKV + KV_RANK], kvw_ref[...]).astype(BF16)
    qa = jnp.dot(qn, wqa_ref[...], preferred_element_type=F32)
    qb = jnp.dot(qn, wqb_ref[...], preferred_element_type=F32)
    kn = jnp.dot(kvn, wk_ref[...], preferred_element_type=F32)
    v = jnp.dot(kvn, wv_ref[...], preferred_element_type=F32)
    kr = (za[:, Z_KR:Z_KR + LANES] * cos + za[:, Z_KRR:Z_KRR + LANES] * sin).astype(BF16)
    for h in range(H_A):
        c0 = h * 2 * LANES
        q_ref[:, c0:c0 + LANES] = qa[:, c0:c0 + LANES].astype(BF16)
        q_ref[:, c0 + LANES:c0 + 2 * LANES] = (
            qa[:, c0 + LANES:c0 + 2 * LANES] * cos + qb[:, h * LANES:(h + 1) * LANES] * sin).astype(BF16)
        k_ref[:, c0:c0 + LANES] = kn[:, h * LANES:(h + 1) * LANES].astype(BF16)
        k_ref[:, c0 + LANES:c0 + 2 * LANES] = kr
    v_ref[...] = v.astype(BF16)


ATT_W = H_A * 2 * LANES


def _mla_prep(z, cs, qw, kvw, wqa, wqb, wk, wv):
    const = lambda i: (0, 0)
    rope_blk = lambda i: (jnp.where(i % TPB == 0, XT, i % TPB - 1), 0)
    return pl.pallas_call(
        _mla_prep_kernel,
        grid=(T_ALL // TILE,),
        in_specs=[pl.BlockSpec((TILE, Z_MQ), lambda i: (i, 0)),
                  pl.BlockSpec((TILE, 2 * LANES), rope_blk),
                  pl.BlockSpec((1, Q_RANK), const),
                  pl.BlockSpec((1, KV_RANK), const),
                  pl.BlockSpec(wqa.shape, const),
                  pl.BlockSpec(wqb.shape, const),
                  pl.BlockSpec(wk.shape, const),
                  pl.BlockSpec(wv.shape, const)],
        out_specs=[pl.BlockSpec((TILE, ATT_W), lambda i: (i, 0)),
                   pl.BlockSpec((TILE, ATT_W), lambda i: (i, 0)),
                   pl.BlockSpec((TILE, D_MLA), lambda i: (i, 0))],
        out_shape=[jax.ShapeDtypeStruct((T_ALL, ATT_W), BF16),
                   jax.ShapeDtypeStruct((T_ALL, ATT_W), BF16),
                   jax.ShapeDtypeStruct((T_ALL, D_MLA), BF16)],
        compiler_params=_cparams("arbitrary"),
        name="mla_prep",
    )(z, cs, qw.reshape(1, Q_RANK), kvw.reshape(1, KV_RANK), wqa, wqb, wk, wv)


def _attn_tile(q_ref, k_ref, v_ref, w_ref, o_ref, acc_ref, n_keys):
    scale2 = (D_NOPE + D_ROPE) ** -0.5 * math.log2(math.e)
    for h in range(H_A):
        q = q_ref[:, h * 2 * LANES:(h + 1) * 2 * LANES]
        s = lax.dot_general(q, k_ref[0:n_keys, h * 2 * LANES:(h + 1) * 2 * LANES], _NT, preferred_element_type=F32)
        m = jnp.max(s, axis=-1, keepdims=True)
        p = jnp.exp2((s - m) * scale2)
        l = jnp.sum(p, axis=-1, keepdims=True)
        o = jnp.dot(p.astype(BF16), v_ref[0:n_keys, h * D_V:(h + 1) * D_V], preferred_element_type=F32)
        acc_ref[:, h * D_V:(h + 1) * D_V] = o / l
    o_ref[...] = (_rms(acc_ref[...]) * w_ref[...]).astype(o_ref.dtype)


def _attn_kernel(q_ref, k_ref, v_ref, w_ref, o_ref, acc_ref):
    @pl.when(pl.program_id(1) == 0)
    def _():
        _attn_tile(q_ref, k_ref, v_ref, w_ref, o_ref, acc_ref, CTX_LEN)

    @pl.when(pl.program_id(1) != 0)
    def _():
        _attn_tile(q_ref, k_ref, v_ref, w_ref, o_ref, acc_ref, PB)


def _attention(q, k, v, onw):
    return pl.pallas_call(
        _attn_kernel,
        grid=(BATCH, TPB),
        in_specs=[pl.BlockSpec((TILE, ATT_W), lambda b, i: (b * TPB + i, 0)),
                  pl.BlockSpec((None, PB, ATT_W), lambda b, i: (b, 0, 0)),
                  pl.BlockSpec((None, PB, D_MLA), lambda b, i: (b, 0, 0)),
                  pl.BlockSpec((1, D_MLA), lambda b, i: (0, 0))],
        out_specs=pl.BlockSpec((TILE, D_MLA), lambda b, i: (b * TPB + i, 0)),
        out_shape=jax.ShapeDtypeStruct((T_ALL, D_MLA), BF16),
        scratch_shapes=[pltpu.VMEM((TILE, D_MLA), F32)],
        compiler_params=_cparams("arbitrary", "arbitrary"),
        name="attention",
    )(q, k.reshape(BATCH, PB, ATT_W), v.reshape(BATCH, PB, D_MLA), onw)


def _outproj_kernel(a0, a1, a2, a3, w0, w1, w2, w3, x_ref, mod_ref, o_ref, wb_s, *, g_idx):
    @pl.when(pl.program_id(0) == 0)
    def _():
        for r, w in enumerate((w0, w1, w2, w3)):
            wb_s[r] = w[...].astype(BF16)

    acc = jnp.dot(a0[...], wb_s[0], preferred_element_type=F32)
    acc += jnp.dot(a1[...], wb_s[1], preferred_element_type=F32)
    acc += jnp.dot(a2[...], wb_s[2], preferred_element_type=F32)
    acc += jnp.dot(a3[...], wb_s[3], preferred_element_type=F32)
    g = mod_ref[0][g_idx:g_idx + 1, :]
    o_ref[...] = x_ref[...] + g * acc


def _outproj(ys, w_out, l, x, modtab, g_idx):
    kq = D_MODEL // 4
    a_specs = [pl.BlockSpec((TILE, kq), lambda i: (i, 0)) for _ in range(4)]
    w_specs = [pl.BlockSpec((None, kq, D_MODEL), functools.partial(lambda i, r: (l, r, 0), r=r)) for r in range(4)]
    return pl.pallas_call(
        functools.partial(_outproj_kernel, g_idx=g_idx),
        grid=(T_ALL // TILE,),
        in_specs=a_specs + w_specs + [
            pl.BlockSpec((TILE, D_MODEL), lambda i: (i, 0)),
            pl.BlockSpec((1, 6, D_MODEL), lambda i: (_tile_mod_row(i), 0, 0))],
        out_specs=pl.BlockSpec((TILE, D_MODEL), lambda i: (i, 0)),
        out_shape=jax.ShapeDtypeStruct((T_ALL, D_MODEL), F32),
        scratch_shapes=[pltpu.VMEM((4, kq, D_MODEL), BF16)],
        compiler_params=_cparams("arbitrary"),
        name="out_proj",
    )(*ys, w_out, w_out, w_out, w_out, x, modtab)


R_E0, R_E1, R_W0, R_W1, R_K0, R_K1 = range(6)
NO_LANE = 2 * LANES


def _prenorm_router_kernel(x_ref, w_ref, mod_ref, wr_ref, tri_ref, h_ref, route_ref, route_t_ref, cnt_ref, cnt_s,
                           *, sh_idx, sc_idx):
    @pl.when(pl.program_id(0) == 0)
    def _():
        cnt_s[...] = jnp.zeros_like(cnt_s)

    y = _rms(x_ref[...], w_ref[...])
    m = mod_ref[0]
    y = y * (1.0 + m[sc_idx:sc_idx + 1, :]) + m[sh_idx:sh_idx + 1, :]
    h_ref[...] = y
    y_hi = y.astype(BF16)
    y_lo = (y - y_hi.astype(F32)).astype(BF16)
    lg = (jnp.dot(y_hi, wr_ref[0], preferred_element_type=F32) + jnp.dot(y_lo, wr_ref[0], preferred_element_type=F32)
          + jnp.dot(y_hi, wr_ref[1], preferred_element_type=F32))
    lane = lax.broadcasted_iota(jnp.int32, lg.shape, 1)

    def first_max(v):
        top = jnp.max(v, axis=1, keepdims=True)
        return top, jnp.min(jnp.where(v == top, lane, NO_LANE), axis=1, keepdims=True)

    is_g = lane < N_GROUPS
    g_top, g_idx = first_max(jnp.where(is_g, lg, NEG_BIG))
    g_w = 1.0 / jnp.sum(jnp.where(is_g, jnp.exp(lg - g_top), 0.0), axis=1, keepdims=True)
    lo = N_GROUPS + EXPERTS_PER_GROUP * g_idx
    el = jnp.where(jnp.logical_and(lane >= lo, lane < lo + EXPERTS_PER_GROUP), lg, NEG_BIG)
    v1, i1 = first_max(el)
    v2, i2 = first_max(jnp.where(lane == i1, NEG_BIG, el))
    t = jnp.exp(v2 - v1)
    w0 = g_w / (1.0 + t)
    w1 = g_w * t / (1.0 + t)
    e0 = i1 - N_GROUPS
    e1 = i2 - N_GROUPS
    hit0 = lane == e0
    hit1 = lane == e1
    onehot = jnp.logical_or(hit0, hit1).astype(F32)
    before = cnt_s[0:1, :] + jnp.dot(tri_ref[...], onehot.astype(BF16), preferred_element_type=F32)
    k0 = jnp.sum(jnp.where(hit0, before, 0.0), axis=1, keepdims=True)
    k1 = jnp.sum(jnp.where(hit1, before, 0.0), axis=1, keepdims=True)
    cnt_s[0:1, :] = cnt_s[0:1, :] + jnp.sum(onehot, axis=0, keepdims=True)
    cnt_ref[...] = cnt_s[...]
    rec = jnp.zeros(lg.shape, F32)
    for ln, val in ((R_E0, e0.astype(F32)), (R_E1, e1.astype(F32)), (R_W0, w0), (R_W1, w1), (R_K0, k0), (R_K1, k1)):
        rec = jnp.where(lane == ln, val, rec)
    route_ref[...] = rec
    route_t_ref[...] = rec.T[0:8, :]


def _prenorm_router(x, w, modtab, w_router, sh_idx, sc_idx, latent_only):
    n_tok = T_X if latent_only else T_ALL
    r = jnp.arange(TILE)
    tri = (r[None, :] < r[:, None]).astype(BF16)
    src = lambda i: _token_tile(i, latent_only)
    return pl.pallas_call(
        functools.partial(_prenorm_router_kernel, sh_idx=sh_idx, sc_idx=sc_idx),
        grid=(n_tok // TILE,),
        in_specs=[pl.BlockSpec((TILE, D_MODEL), lambda i: (src(i), 0)),
                  pl.BlockSpec((1, D_MODEL), lambda i: (0, 0)),
                  pl.BlockSpec((1, 6, D_MODEL), lambda i: (_tile_mod_row(src(i)), 0, 0)),
                  pl.BlockSpec((2, D_MODEL, ROUTER_COLS), lambda i: (0, 0, 0)),
                  pl.BlockSpec((TILE, TILE), lambda i: (0, 0))],
        out_specs=[pl.BlockSpec((TILE, D_MODEL), lambda i: (i, 0)),
                   pl.BlockSpec((TILE, ROUTER_COLS), lambda i: (i, 0)),
                   pl.BlockSpec((8, TILE), lambda i: (0, i)),
                   pl.BlockSpec((8, ROUTER_COLS), lambda i: (0, 0))],
        out_shape=[jax.ShapeDtypeStruct((n_tok, D_MODEL), F32),
                   jax.ShapeDtypeStruct((n_tok, ROUTER_COLS), F32),
                   jax.ShapeDtypeStruct((8, n_tok), F32),
                   jax.ShapeDtypeStruct((8, ROUTER_COLS), F32)],
        scratch_shapes=[pltpu.VMEM((8, ROUTER_COLS), F32)],
        compiler_params=_cparams("arbitrary"),
        name="prenorm_router",
    )(x, w.reshape(1, D_MODEL), modtab, w_router, tri)


def _row_copy(src, src_row, dst, dst_row, sem):
    return pltpu.make_async_copy(src.at[pl.ds(src_row, 1)], dst.at[pl.ds(dst_row, 1)], sem)


def _dispatch_kernel(dest_ref, h_ref, xb_ref, sem):
    def issue(r, carry):
        for k in range(TOP_K):
            _row_copy(h_ref, r, xb_ref, dest_ref[0, k, r], sem).start(priority=k)
        return carry

    lax.fori_loop(0, TILE, issue, 0, unroll=8)

    def drain(r, carry):
        for k in range(TOP_K):
            _row_copy(h_ref, 0, xb_ref, 0, sem).wait()
        return carry

    lax.fori_loop(0, TILE, drain, 0, unroll=8)


def _dispatch(h, dest3):
    n_tok = h.shape[0]
    return pl.pallas_call(
        _dispatch_kernel,
        grid=(n_tok // TILE,),
        in_specs=[pl.BlockSpec((1, TOP_K, TILE), lambda i: (i, 0, 0), memory_space=pltpu.SMEM),
                  pl.BlockSpec((TILE, D_MODEL), lambda i: (i, 0))],
        out_specs=pl.BlockSpec(memory_space=pl.ANY),
        out_shape=jax.ShapeDtypeStruct((n_tok * TOP_K, D_MODEL), F32),
        scratch_shapes=[pltpu.SemaphoreType.DMA(())],
        compiler_params=_cparams("arbitrary"),
        name="moe_dispatch",
    )(dest3, h)


def _moe_kernel(tile_ref, exp_ref, lo_ref, hi_ref, flag_ref, x_ref, wg_ref, wu_ref, wd_ref, o_ref,
                wg_s, wu_s, wd_s):
    i = pl.program_id(0)
    flags = flag_ref[i]

    @pl.when(flags % 2 == 1)
    def _():
        wg_s[...] = wg_ref[...].astype(BF16)
        wu_s[...] = wu_ref[...].astype(BF16)
        wd_s[...] = wd_ref[...].astype(BF16)

    @pl.when(flags >= 4)
    def _():
        x = x_ref[...].astype(BF16)
        g = jnp.dot(x, wg_s[...], preferred_element_type=F32)
        u = jnp.dot(x, wu_s[...], preferred_element_type=F32)
        row = lax.broadcasted_iota(jnp.int32, g.shape, 0)
        mine = jnp.logical_and(row >= lo_ref[i], row < hi_ref[i])
        h = jnp.where(mine, g * jax.nn.sigmoid(g) * u, 0.0).astype(BF16)
        res = jnp.dot(h, wd_s[...], preferred_element_type=F32)

        @pl.when((flags // 2) % 2 == 1)
        def _():
            o_ref[...] = res

        @pl.when((flags // 2) % 2 == 0)
        def _():
            o_ref[...] += res


def _moe_experts(xb, meta, wg, wu, wd, l):
    n_items = meta[0].shape[0]
    wmap = lambda i, ti, ex, lo, hi, fl: (l, ex[i], 0, 0)
    xmap = lambda i, ti, ex, lo, hi, fl: (ti[i], 0)
    grid_spec = pltpu.PrefetchScalarGridSpec(
        num_scalar_prefetch=5,
        grid=(n_items,),
        in_specs=[pl.BlockSpec((MOE_ROWS, D_MODEL), xmap),
                  pl.BlockSpec((None, None, D_MODEL, D_EXPERT), wmap),
                  pl.BlockSpec((None, None, D_MODEL, D_EXPERT), wmap),
                  pl.BlockSpec((None, None, D_EXPERT, D_MODEL), wmap)],
        out_specs=pl.BlockSpec((MOE_ROWS, D_MODEL), xmap),
        scratch_shapes=[pltpu.VMEM((D_MODEL, D_EXPERT), BF16),
                        pltpu.VMEM((D_MODEL, D_EXPERT), BF16),
                        pltpu.VMEM((D_EXPERT, D_MODEL), BF16)])
    return pl.pallas_call(
        _moe_kernel,
        grid_spec=grid_spec,
        out_shape=jax.ShapeDtypeStruct(xb.shape, F32),
        compiler_params=_cparams("arbitrary"),
        name="moe_experts",
    )(*meta, xb, wg, wu, wd)


def _combine_kernel(dest_ref, next_ref, yb_ref, x_ref, route_ref, mod_ref, nw_ref, nmod_ref, *rest, final):
    if final:
        o_ref, buf, sem = rest
    else:
        o_ref, hx_ref, buf, sem = rest
    i = pl.program_id(0)
    slot = i % 2
    n_groups = TILE // COMBINE_ROWS

    def issue(d_ref, to_slot, j):
        for rr in range(COMBINE_ROWS):
            r = j * COMBINE_ROWS + rr
            for k in range(TOP_K):
                _row_copy(yb_ref, d_ref[0, k, r], buf.at[to_slot, k], r, sem.at[to_slot]).start(priority=k)

    @pl.when(i == 0)
    def _():
        def first(j, carry):
            issue(dest_ref, 0, j)
            return carry

        lax.fori_loop(0, n_groups, first, 0)

    def drain(r, carry):
        for k in range(TOP_K):
            _row_copy(yb_ref, 0, buf.at[slot, k], 0, sem.at[slot]).wait()
        return carry

    lax.fori_loop(0, TILE, drain, 0, unroll=8)
    g2 = mod_ref[0][5:6, :]
    nw = nw_ref[...]
    nm = nmod_ref[0]

    def combine(j):
        rows = pl.ds(pl.multiple_of(j * COMBINE_ROWS, COMBINE_ROWS), COMBINE_ROWS)
        rt = route_ref[rows, :]
        f = buf[slot, 0, rows, :] * rt[:, R_W0:R_W0 + 1] + buf[slot, 1, rows, :] * rt[:, R_W1:R_W1 + 1]
        y = x_ref[rows, :] + g2 * f
        if final:
            o_ref[rows, :] = _rms(y, nw)
        else:
            o_ref[rows, :] = y
            hx_ref[rows, :] = (_rms(y, nw) * (1.0 + nm[1:2, :]) + nm[0:1, :]).astype(hx_ref.dtype)

    @pl.when(i + 1 < pl.num_programs(0))
    def _():
        def body(j, carry):
            combine(j)
            issue(next_ref, 1 - slot, j)
            return carry

        lax.fori_loop(0, n_groups, body, 0)

    @pl.when(i + 1 >= pl.num_programs(0))
    def _():
        def body(j, carry):
            combine(j)
            return carry

        lax.fori_loop(0, n_groups, body, 0)


COMBINE_ROWS = 64


def _combine(yb, dest3, x, route, modtab, norm_w, next_modtab, latent_only):
    n_tok = route.shape[0]
    n_tiles = n_tok // TILE
    src = lambda i: _token_tile(i, latent_only)
    row = lambda i: (i, 0)
    mod_spec = pl.BlockSpec((1, 6, D_MODEL), lambda i: (_tile_mod_row(src(i)), 0, 0))
    out_specs = [pl.BlockSpec((TILE, D_MODEL), row)]
    out_shape = [jax.ShapeDtypeStruct((n_tok, D_MODEL), F32)]
    if not latent_only:
        out_specs.append(pl.BlockSpec((TILE, D_MODEL), row))
        out_shape.append(jax.ShapeDtypeStruct((n_tok, D_MODEL), BF16))
    return pl.pallas_call(
        functools.partial(_combine_kernel, final=latent_only),
        grid=(n_tiles,),
        in_specs=[pl.BlockSpec((1, TOP_K, TILE), lambda i: (i, 0, 0), memory_space=pltpu.SMEM),
                  pl.BlockSpec((1, TOP_K, TILE), lambda i: (jnp.minimum(i + 1, n_tiles - 1), 0, 0),
                               memory_space=pltpu.SMEM),
                  pl.BlockSpec(memory_space=pl.ANY),
                  pl.BlockSpec((TILE, D_MODEL), lambda i: (src(i), 0)),
                  pl.BlockSpec((TILE, ROUTER_COLS), row),
                  mod_spec,
                  pl.BlockSpec((1, D_MODEL), lambda i: (0, 0)),
                  mod_spec],
        out_specs=out_specs,
        out_shape=out_shape,
        scratch_shapes=[pltpu.VMEM((2, TOP_K, TILE, D_MODEL), F32), pltpu.SemaphoreType.DMA((2,))],
        compiler_params=_cparams("arbitrary"),
        name="moe_combine",
    )(dest3, dest3, yb, x, route, modtab, norm_w.reshape(1, D_MODEL), next_modtab)


def _moe_plan(route_t, cnt):
    t = route_t.shape[1]
    n_tiles = t * TOP_K // MOE_ROWS
    n_items = n_tiles + N_EXPERTS - 1
    experts = route_t[R_E0:R_E1 + 1].astype(jnp.int32)
    rank = route_t[R_K0:R_K1 + 1].astype(jnp.int32)
    counts = cnt[0, :N_EXPERTS].astype(jnp.int32)
    ends = jnp.cumsum(counts)
    starts = ends - counts
    e_axis = jnp.arange(N_EXPERTS, dtype=jnp.int32)[:, None, None]
    dest = jnp.sum(jnp.where(experts[None] == e_axis, starts[:, None, None], 0), axis=0) + rank
    t_first = starts // MOE_ROWS
    per_e = jnp.where(counts > 0, (ends - 1) // MOE_ROWS - t_first + 1, 0)
    item_end = jnp.cumsum(per_e)
    item_start = item_end - per_e
    total = item_end[-1]
    w = jnp.arange(n_items, dtype=jnp.int32)
    valid = w < total
    wc = jnp.minimum(w, total - 1)
    ex = jnp.clip(jnp.searchsorted(item_end, wc, side='right'), 0, N_EXPERTS - 1).astype(jnp.int32)
    tile = (t_first[ex] + wc - item_start[ex]).astype(jnp.int32)
    lo = jnp.where(valid, jnp.maximum(starts[ex], tile * MOE_ROWS) - tile * MOE_ROWS, 0).astype(jnp.int32)
    hi = jnp.where(valid, jnp.minimum(ends[ex], (tile + 1) * MOE_ROWS) - tile * MOE_ROWS, 0).astype(jnp.int32)
    one = jnp.ones((1,), bool)
    new_e = jnp.concatenate([one, ex[1:] != ex[:-1]])
    new_t = jnp.concatenate([one, tile[1:] != tile[:-1]])
    flags = (new_e.astype(jnp.int32) + 2 * new_t.astype(jnp.int32) + 4 * valid.astype(jnp.int32))
    dest3 = dest.reshape(TOP_K, t // TILE, TILE).transpose(1, 0, 2)
    return dest3, (tile, ex, lo, hi, flags)


def _moe(x, h, route, route_t, cnt, modtab, wg, wu, wd, l, norm_w, next_modtab, latent_only):
    dest3, meta = _moe_plan(route_t, cnt)
    xb = _dispatch(h, dest3)
    yb = _moe_experts(xb, meta, wg, wu, wd, l)
    res = _combine(yb, dest3, x, route, modtab, norm_w, next_modtab, latent_only)
    return res[0] if latent_only else (res[0], res[1])


def _scan_chunk(s, rev, chunk):
    n_c = CTX_LEN // chunk
    n_all = PB // chunk
    if not rev:
        return s
    return jnp.where(s < n_c, n_c - 1 - s, n_all + n_c - 1 - s)


HALO = 8


def _ssd_prep_kernel(cur_ref, prev_ref, next_ref, dt_ref, cw_ref, cb_ref, dtb_ref, xo_ref, dto_ref, ext_s):
    j = pl.program_id(0) % TPB
    first = jnp.logical_or(j == 0, j == 1)
    last = jnp.logical_or(j == 0, j == TPB - 1)
    ext_s[0:HALO, :] = jnp.where(first, 0.0, prev_ref[...])
    ext_s[HALO:HALO + TILE, :] = cur_ref[...]
    ext_s[HALO + TILE:, :] = jnp.where(last, 0.0, next_ref[...])
    half = (SSD_CONV - 1) // 2
    acc = cb_ref[...] + cw_ref[0:1, :] * ext_s[HALO - half:HALO - half + TILE, :]
    for k in range(1, SSD_CONV):
        acc = acc + cw_ref[k:k + 1, :] * ext_s[HALO - half + k:HALO - half + k + TILE, :]
    xo_ref[...] = acc * jax.nn.sigmoid(acc)
    lane = lax.broadcasted_iota(jnp.int32, (TILE, LANES), 1)
    dto_ref[...] = jnp.where(lane < 2 * H_B, jax.nn.softplus(dt_ref[...] + dtb_ref[...]), 0.0)


def _ssd_prep(z, conv_w, conv_b, dt_bias):
    n_tiles = T_ALL // TILE
    per = TILE // HALO
    cwp = jnp.concatenate([conv_w, jnp.zeros((8 - SSD_CONV, CONV_CH), F32)], axis=0)
    dtb = jnp.concatenate([dt_bias.reshape(1, 2 * H_B), jnp.zeros((1, LANES - 2 * H_B), F32)], axis=1)
    xc = Z_XBC // CONV_CH
    return pl.pallas_call(
        _ssd_prep_kernel,
        grid=(n_tiles,),
        in_specs=[pl.BlockSpec((TILE, CONV_CH), lambda i: (i, xc)),
                  pl.BlockSpec((HALO, CONV_CH), lambda i: (jnp.maximum(i * per - 1, 0), xc)),
                  pl.BlockSpec((HALO, CONV_CH), lambda i: (jnp.minimum((i + 1) * per, T_ALL // HALO - 1), xc)),
                  pl.BlockSpec((TILE, LANES), lambda i: (i, Z_DT // LANES)),
                  pl.BlockSpec((8, CONV_CH), lambda i: (0, 0)),
                  pl.BlockSpec((1, CONV_CH), lambda i: (0, 0)),
                  pl.BlockSpec((1, LANES), lambda i: (0, 0))],
        out_specs=[pl.BlockSpec((TILE, CONV_CH), lambda i: (i, 0)),
                   pl.BlockSpec((TILE, LANES), lambda i: (i, 0))],
        out_shape=[jax.ShapeDtypeStruct((T_ALL, CONV_CH), F32),
                   jax.ShapeDtypeStruct((T_ALL, LANES), F32)],
        scratch_shapes=[pltpu.VMEM((TILE + 2 * HALO, CONV_CH), F32)],
        compiler_params=_cparams("arbitrary"),
        name="ssd_prep",
    )(z, z, z, z, cwp, conv_b.reshape(1, CONV_CH), dtb)


def _ssd_one_direction(xbc, dtp, arow, tri, expand, h_ref, y_ref, b, d, rev):
    q = SSD_CHUNK
    a = dtp * arow
    acum = _dot3_left(tri, a)
    yield
    acum_t = acum.T
    dt_t = dtp.T
    yield
    edge = 0 if rev else q - 1
    atot = acum[edge:edge + 1, :]
    pieces = jnp.concatenate([jnp.exp(atot - acum) * dtp, jnp.exp(acum),
                              jnp.broadcast_to(jnp.exp(atot), (8, LANES))], axis=0)
    ex = _dot3_right(pieces, expand)
    yield
    wend_x = ex[0:q]
    eacum_x = ex[q:2 * q]
    dec_x = ex[2 * q:2 * q + 1]
    xs = xbc[:, 0:D_SSD]
    xw = (xs * wend_x).astype(BF16)
    xs_b = xs.astype(BF16)
    h_old = h_ref[b]
    h_b = h_old.astype(BF16)
    ri = lax.broadcasted_iota(jnp.int32, (q, q), 0)
    ci = lax.broadcasted_iota(jnp.int32, (q, q), 1)
    mask = (ci >= ri) if rev else (ci <= ri)
    lo_half = lax.broadcasted_iota(jnp.int32, (q, LANES), 1) < P_B
    hpg = H_B // G_B
    gw = hpg * P_B
    yield

    def group(g):
        bg = xbc[:, D_SSD + g * N_B:D_SSD + (g + 1) * N_B].astype(BF16)
        cg = xbc[:, D_SSD + G_B * N_B + g * N_B:D_SSD + G_B * N_B + (g + 1) * N_B].astype(BF16)
        cb = lax.dot_general(cg, bg, _NT, preferred_element_type=F32)
        inter = jnp.dot(cg, h_b[:, g * gw:(g + 1) * gw], preferred_element_type=F32)
        upd = lax.dot_general(bg, xw[:, g * gw:(g + 1) * gw], _TN, preferred_element_type=F32)
        yield

        def pair(j):
            ms = []
            for hh in range(2):
                hc = H_B * d + hpg * g + 2 * j + hh
                seg = acum[:, hc:hc + 1] - acum_t[hc:hc + 1, :]
                dec = jnp.exp(jnp.where(mask, seg, NEG_BIG))
                ms.append((cb * dec * dt_t[hc:hc + 1, :]).astype(BF16))
                yield
            c0 = g * gw + 2 * j * P_B
            xp = xs_b[:, c0:c0 + LANES]
            zero = jnp.zeros_like(xp)
            rhs = jnp.concatenate([jnp.where(lo_half, xp, zero), jnp.where(lo_half, zero, xp)], axis=0)
            y_intra = jnp.dot(jnp.concatenate(ms, axis=1), rhs, preferred_element_type=F32)
            yield
            y_ref[b, :, c0:c0 + LANES] = (
                y_intra + eacum_x[:, c0:c0 + LANES] * inter[:, 2 * j * P_B:2 * j * P_B + LANES])
            yield

        yield from _in_turn([pair(j) for j in range(hpg // 2)])
        h_ref[b, :, g * gw:(g + 1) * gw] = dec_x[:, g * gw:(g + 1) * gw] * h_old[:, g * gw:(g + 1) * gw] + upd
        yield

    yield from _in_turn([group(g) for g in range(G_B)])


def _ssd_scan_kernel(xf_ref, dtf_ref, xb_ref, dtb_ref, arow_ref, trif_ref, trib_ref, ef_ref, eb_ref,
                     yf_ref, yb_ref, hf_s, hb_s):
    @pl.when(pl.program_id(0) == 0)
    def _():
        hf_s[...] = jnp.zeros_like(hf_s)
        hb_s[...] = jnp.zeros_like(hb_s)

    chains = []
    for b in range(BATCH):
        chains.append(_ssd_one_direction(xf_ref[b], dtf_ref[b], arow_ref[...], trif_ref[...], ef_ref[...],
                                         hf_s, yf_ref, b, 0, False))
        chains.append(_ssd_one_direction(xb_ref[b], dtb_ref[b], arow_ref[...], trib_ref[...], eb_ref[...],
                                         hb_s, yb_ref, b, 1, True))
    _round_robin(chains)


def _ssd_scan(xact, dtp, a_log):
    q = SSD_CHUNK
    steps = PB // q
    a_neg = -jnp.exp(a_log)
    arow = jnp.concatenate([a_neg.reshape(1, 2 * H_B), jnp.zeros((1, LANES - 2 * H_B), F32)], axis=1)
    r = jnp.arange(q)
    tri_f = (r[None, :] <= r[:, None]).astype(BF16)
    tri_b = (r[None, :] >= r[:, None]).astype(BF16)
    col_head = jnp.arange(D_SSD) // P_B
    lane = jnp.arange(LANES)
    exp_f = (lane[:, None] == col_head[None, :]).astype(BF16)
    exp_b = (lane[:, None] == col_head[None, :] + H_B).astype(BF16)
    fwd = lambda s: (0, _scan_chunk(s, False, q), 0)
    bwd = lambda s: (0, _scan_chunk(s, True, q), 0)
    const = lambda s: (0, 0)
    x3 = xact.reshape(BATCH, PB, CONV_CH)
    d3 = dtp.reshape(BATCH, PB, LANES)
    yf, yb = pl.pallas_call(
        _ssd_scan_kernel,
        grid=(steps,),
        in_specs=[pl.BlockSpec((BATCH, q, CONV_CH), fwd), pl.BlockSpec((BATCH, q, LANES), fwd),
                  pl.BlockSpec((BATCH, q, CONV_CH), bwd), pl.BlockSpec((BATCH, q, LANES), bwd),
                  pl.BlockSpec((1, LANES), const),
                  pl.BlockSpec((q, q), const), pl.BlockSpec((q, q), const),
                  pl.BlockSpec((LANES, D_SSD), const), pl.BlockSpec((LANES, D_SSD), const)],
        out_specs=[pl.BlockSpec((BATCH, q, D_SSD), fwd), pl.BlockSpec((BATCH, q, D_SSD), bwd)],
        out_shape=[jax.ShapeDtypeStruct((BATCH, PB, D_SSD), F32), jax.ShapeDtypeStruct((BATCH, PB, D_SSD), F32)],
        scratch_shapes=[pltpu.VMEM((BATCH, N_B, D_SSD), F32), pltpu.VMEM((BATCH, N_B, D_SSD), F32)],
        compiler_params=_cparams("arbitrary"),
        name="ssd_scan",
    )(x3, d3, x3, d3, arow, tri_f, tri_b, exp_f, exp_b)
    return yf.reshape(T_ALL, D_SSD), yb.reshape(T_ALL, D_SSD)


def _ssd_final_kernel(yf_ref, yb_ref, xs_ref, gate_ref, d_ref, onw_ref, o_ref):
    gate = gate_ref[...]
    y = (yf_ref[...] + yb_ref[...] + d_ref[...] * xs_ref[...]) * (gate * jax.nn.sigmoid(gate))
    o_ref[...] = (_rms(y) * onw_ref[...]).astype(o_ref.dtype)


def _ssd_final(yf, yb, xact, z, d_skip, onw):
    row = lambda i: (i, 0)
    const = lambda i: (0, 0)
    return pl.pallas_call(
        _ssd_final_kernel,
        grid=(T_ALL // MM_ROWS,),
        in_specs=[pl.BlockSpec((MM_ROWS, D_SSD), row), pl.BlockSpec((MM_ROWS, D_SSD), row),
                  pl.BlockSpec((MM_ROWS, D_SSD), row),
                  pl.BlockSpec((MM_ROWS, D_SSD), lambda i: (i, Z_SZ // D_SSD)),
                  pl.BlockSpec((1, D_SSD), const), pl.BlockSpec((1, D_SSD), const)],
        out_specs=pl.BlockSpec((MM_ROWS, D_SSD), row),
        out_shape=jax.ShapeDtypeStruct((T_ALL, D_SSD), BF16),
        compiler_params=_cparams("arbitrary"),
        name="ssd_final",
    )(yf, yb, xact, z, jnp.repeat(d_skip, P_B).reshape(1, D_SSD), onw.reshape(1, D_SSD))


def _ssd_mixer(z, p, onw):
    xact, dtp = _ssd_prep(z, p["ssd_conv_w"], p["ssd_conv_b"], p["ssd_dt_bias"])
    yf, yb = _ssd_scan(xact, dtp, p["ssd_a_log"])
    return _ssd_final(yf, yb, xact, z, p["ssd_d"], onw)


def _mlstm_one_direction(q, k, v, gi, gf, bi, bf, tri, st_ref, m_ref, h_ref, b, d, rev):
    n = MLSTM_CHUNK
    li = gi + bi
    lf = jax.nn.log_sigmoid(gf + bf)
    yield
    bc = _dot3_left(tri, lf)
    yield
    b_t = bc.T
    li_t = li.T
    yield
    edge = 0 if rev else n - 1
    gtot = bc[edge:edge + 1, :]
    m_old = m_ref[b, 0:1, :]
    w_log = gtot - bc + li
    m_new = jnp.maximum(gtot + m_old, jnp.max(w_log, axis=0, keepdims=True))
    wj = jnp.exp(w_log - m_new)
    dec = jnp.exp(gtot + m_old - m_new)
    inter_log = bc + m_old
    m_ref[b, 0:1, :] = m_new
    yield
    ri = lax.broadcasted_iota(jnp.int32, (n, n), 0)
    ci = lax.broadcasted_iota(jnp.int32, (n, n), 1)
    mask = (ci >= ri) if rev else (ci <= ri)
    lo_half = lax.broadcasted_iota(jnp.int32, (n, LANES), 1) < DQK_C
    row_lo = lax.broadcasted_iota(jnp.int32, (2 * DQK_C, 2 * DV_C), 0) < DQK_C
    ones = jnp.ones((n, DV_C), F32)
    st_old = [st_ref[b, j] for j in range(H_C // 2)]
    upds = {}

    def head(h):
        j, hh = divmod(h, 2)
        gl = GATE_LANE0 + H_C * d + h
        qp = q[:, j * LANES:(j + 1) * LANES] * DQK_C ** -0.5
        kp = k[:, j * LANES:(j + 1) * LANES].astype(BF16)
        qm = jnp.where(lo_half if hh == 0 else jnp.logical_not(lo_half), qp, 0.0).astype(BF16)
        qk = lax.dot_general(qm, kp, _NT, preferred_element_type=F32)
        qs = jnp.dot(qm, st_old[j].astype(BF16), preferred_element_type=F32)
        yield
        dmat = jnp.where(mask, bc[:, gl:gl + 1] - b_t[gl:gl + 1, :] + li_t[gl:gl + 1, :], NEG_BIG)
        il = inter_log[:, gl:gl + 1]
        m_row = jnp.maximum(il, jnp.max(dmat, axis=1, keepdims=True))
        yield
        s = qk * jnp.exp(dmat - m_row)
        w_inter = jnp.exp(il - m_row)
        vh = v[:, h * DV_C:(h + 1) * DV_C]
        yield
        num = jnp.dot(s.astype(BF16), vh.astype(BF16), preferred_element_type=F32) + w_inter * qs[:, :DV_C]
        den = jnp.sum(s, axis=1, keepdims=True) + w_inter * qs[:, DV_C:]
        yield
        h_ref[b, :, h * DV_C:(h + 1) * DV_C] = num / jnp.maximum(jnp.abs(den), jnp.exp(-m_row))
        rhs = (wj[:, gl:gl + 1] * jnp.concatenate([vh, ones], axis=1)).astype(BF16)
        upds[h] = lax.dot_general(kp, rhs, _TN, preferred_element_type=F32)
        yield

    yield from _in_turn([head(h) for h in range(H_C)])
    for j in range(H_C // 2):
        ga = GATE_LANE0 + H_C * d + 2 * j
        decv = jnp.where(row_lo, dec[:, ga:ga + 1], dec[:, ga + 1:ga + 2])
        st_ref[b, j] = decv * st_old[j] + jnp.where(row_lo, upds[2 * j], upds[2 * j + 1])
        yield


def _mlstm_scan_kernel(qf, kf, vf, gif, gff, qb, kb, vb, gib, gfb, bi_ref, bf_ref, trif_ref, trib_ref,
                       hf_ref, hb_ref, stf_s, stb_s, mf_s, mb_s):
    @pl.when(pl.program_id(0) == 0)
    def _():
        stf_s[...] = jnp.zeros_like(stf_s)
        stb_s[...] = jnp.zeros_like(stb_s)
        mf_s[...] = jnp.full_like(mf_s, NEG_STATE)
        mb_s[...] = jnp.full_like(mb_s, NEG_STATE)

    chains = []
    for b in range(BATCH):
        chains.append(_mlstm_one_direction(qf[b], kf[b], vf[b], gif[b], gff[b], bi_ref[...], bf_ref[...],
                                           trif_ref[...], stf_s, mf_s, hf_ref, b, 0, False))
        chains.append(_mlstm_one_direction(qb[b], kb[b], vb[b], gib[b], gfb[b], bi_ref[...], bf_ref[...],
                                           trib_ref[...], stb_s, mb_s, hb_ref, b, 1, True))
    _round_robin(chains)


def _mlstm_scan(z, gate_b):
    n = MLSTM_CHUNK
    steps = PB // n
    qkw = H_C * DQK_C
    pad = lambda t: jnp.concatenate([jnp.zeros((1, GATE_LANE0), F32), t.reshape(1, 2 * H_C),
                                     jnp.zeros((1, LANES - GATE_LANE0 - 2 * H_C), F32)], axis=1)
    bi = pad(gate_b[:, 0, :])
    bf = pad(gate_b[:, 1, :])
    r = jnp.arange(n)
    tri_f = (r[None, :] <= r[:, None]).astype(BF16)
    tri_b = (r[None, :] >= r[:, None]).astype(BF16)
    z3 = z.reshape(BATCH, PB, Z_COLS)

    def specs(rev):
        ch = lambda s: _scan_chunk(s, rev, n)
        return [pl.BlockSpec((BATCH, n, qkw), lambda s: (0, ch(s), Z_MQ // qkw)),
                pl.BlockSpec((BATCH, n, qkw), lambda s: (0, ch(s), Z_MK // qkw)),
                pl.BlockSpec((BATCH, n, D_MLSTM), lambda s: (0, ch(s), Z_MV // D_MLSTM)),
                pl.BlockSpec((BATCH, n, LANES), lambda s: (0, ch(s), Z_MG // LANES)),
                pl.BlockSpec((BATCH, n, LANES), lambda s: (0, ch(s), Z_DT // LANES))]

    const = lambda s: (0, 0)
    n_pairs = H_C // 2
    hf, hb = pl.pallas_call(
        _mlstm_scan_kernel,
        grid=(steps,),
        in_specs=specs(False) + specs(True) + [
            pl.BlockSpec((1, LANES), const), pl.BlockSpec((1, LANES), const),
            pl.BlockSpec((n, n), const), pl.BlockSpec((n, n), const)],
        out_specs=[pl.BlockSpec((BATCH, n, D_MLSTM), lambda s: (0, _scan_chunk(s, False, n), 0)),
                   pl.BlockSpec((BATCH, n, D_MLSTM), lambda s: (0, _scan_chunk(s, True, n), 0))],
        out_shape=[jax.ShapeDtypeStruct((BATCH, PB, D_MLSTM), F32), jax.ShapeDtypeStruct((BATCH, PB, D_MLSTM), F32)],
        scratch_shapes=[pltpu.VMEM((BATCH, n_pairs, 2 * DQK_C, 2 * DV_C), F32),
                        pltpu.VMEM((BATCH, n_pairs, 2 * DQK_C, 2 * DV_C), F32),
                        pltpu.VMEM((BATCH, 8, LANES), F32), pltpu.VMEM((BATCH, 8, LANES), F32)],
        compiler_params=_cparams("arbitrary"),
        name="mlstm_scan",
    )(z3, z3, z3, z3, z3, z3, z3, z3, z3, z3, bi, bf, tri_f, tri_b)
    return hf.reshape(T_ALL, D_MLSTM), hb.reshape(T_ALL, D_MLSTM)


def _mlstm_final_kernel(hf_ref, hb_ref, og_ref, onw_ref, o_ref):
    gate = jax.nn.sigmoid(og_ref[...])
    onw = onw_ref[...]
    for h in range(H_C):
        cs = slice(h * DV_C, (h + 1) * DV_C)
        hn = _rms(hf_ref[:, cs] + hb_ref[:, cs])
        o_ref[:, cs] = (hn * gate[:, cs] * onw[:, cs]).astype(o_ref.dtype)


def _mlstm_final(hf, hb, z, onw):
    row = lambda i: (i, 0)
    return pl.pallas_call(
        _mlstm_final_kernel,
        grid=(T_ALL // MM_ROWS,),
        in_specs=[pl.BlockSpec((MM_ROWS, D_MLSTM), row), pl.BlockSpec((MM_ROWS, D_MLSTM), row),
                  pl.BlockSpec((MM_ROWS, D_MLSTM), lambda i: (i, Z_MO // D_MLSTM)),
                  pl.BlockSpec((1, D_MLSTM), lambda i: (0, 0))],
        out_specs=pl.BlockSpec((MM_ROWS, D_MLSTM), row),
        out_shape=jax.ShapeDtypeStruct((T_ALL, D_MLSTM), BF16),
        compiler_params=_cparams("arbitrary"),
        name="mlstm_final",
    )(hf, hb, z, onw.reshape(1, D_MLSTM))


def _mlstm_mixer(z, p, onw):
    hf, hb = _mlstm_scan(z, p["mlstm_gate_b"])
    return _mlstm_final(hf, hb, z, onw)


S5_Q = 256
S5_SEG = S5_Q // 8
S5_LANES = G_S5 * P_S5
S5_SLAB = 512
S5_NSLAB = S5_LANES // S5_SLAB


def _s5_dir_kernel(u_ref, perm_ref, wbr_ref, wbi_ref, atab_ref, apr_ref, api_ref, wcr_ref, wci_ref, y_ref,
                   ur_s, ui_s, xr_s, xi_s, st_s, car_s, *, reverse):
    @pl.when(pl.program_id(0) == 0)
    def _():
        st_s[...] = jnp.zeros_like(st_s)

    for b in range(BATCH):
        up = jnp.dot(perm_ref[...], u_ref[b].astype(BF16), preferred_element_type=F32).astype(BF16)
        for m in range(S5_NSLAB):
            um = up[:, m * LANES:(m + 1) * LANES]
            ur_s[b, :, m * S5_SLAB:(m + 1) * S5_SLAB] = jnp.dot(um, wbr_ref[m], preferred_element_type=F32)
            ui_s[b, :, m * S5_SLAB:(m + 1) * S5_SLAB] = jnp.dot(um, wbi_ref[m], preferred_element_type=F32)

    per = 4
    for grp in range(S5_LANES // (per * LANES)):
        cols = [grp * per * LANES + j * LANES for j in range(per)]
        a_r = [atab_ref[0:8, c0:c0 + LANES] for c0 in cols]
        a_i = [atab_ref[8:16, c0:c0 + LANES] for c0 in cols]

        def body(i, carry, cols=cols, a_r=a_r, a_i=a_i):
            t = (S5_SEG - 1 - i) if reverse else i
            r0 = pl.multiple_of(t * 8, 8)
            new = []
            for b in range(BATCH):
                for j, c0 in enumerate(cols):
                    xr, xi = carry[2 * (b * per + j)], carry[2 * (b * per + j) + 1]
                    nr = a_r[j] * xr - a_i[j] * xi + ur_s[b, pl.ds(r0, 8), c0:c0 + LANES]
                    ni = a_r[j] * xi + a_i[j] * xr + ui_s[b, pl.ds(r0, 8), c0:c0 + LANES]
                    ur_s[b, pl.ds(r0, 8), c0:c0 + LANES] = nr
                    ui_s[b, pl.ds(r0, 8), c0:c0 + LANES] = ni
                    new += [nr, ni]
            return tuple(new)

        lax.fori_loop(0, S5_SEG, body, tuple(jnp.zeros((8, LANES), F32) for _ in range(2 * per * BATCH)), unroll=2)

    as_r = atab_ref[16:17, :]
    as_i = atab_ref[17:18, :]
    end_row = 0 if reverse else 8 * (S5_SEG - 1)
    for b in range(BATCH):
        cr = st_s[b, 0:1, :]
        ci = st_s[b, 1:2, :]
        for k in (range(7, -1, -1) if reverse else range(8)):
            car_s[b, k:k + 1, :] = cr
            car_s[b, 8 + k:9 + k, :] = ci
            er = ur_s[b, end_row + k:end_row + k + 1, :]
            ei = ui_s[b, end_row + k:end_row + k + 1, :]
            cr, ci = er + as_r * cr - as_i * ci, ei + as_r * ci + as_i * cr
        st_s[b, 0:1, :] = cr
        st_s[b, 1:2, :] = ci

    for b in range(BATCH):
        for m in range(S5_NSLAB):
            cs = slice(m * S5_SLAB, (m + 1) * S5_SLAB)
            c_r = jnp.concatenate([car_s[b, 0:8, cs], car_s[b, 0:8, cs]], axis=0)
            c_i = jnp.concatenate([car_s[b, 8:16, cs], car_s[b, 8:16, cs]], axis=0)

            def fix(i, _, b=b, cs=cs, c_r=c_r, c_i=c_i):
                r0 = pl.multiple_of(i * 16, 16)
                p_r = apr_ref[pl.ds(r0, 16), cs]
                p_i = api_ref[pl.ds(r0, 16), cs]
                xr_s[b, pl.ds(r0, 16), cs] = (ur_s[b, pl.ds(r0, 16), cs] + p_r * c_r - p_i * c_i).astype(BF16)
                xi_s[b, pl.ds(r0, 16), cs] = (ui_s[b, pl.ds(r0, 16), cs] + p_r * c_i + p_i * c_r).astype(BF16)
                return 0

            lax.fori_loop(0, S5_Q // 16, fix, 0, unroll=2)

    for b in range(BATCH):
        for m in range(S5_NSLAB):
            cs = slice(m * S5_SLAB, (m + 1) * S5_SLAB)
            y_ref[b, :, m * LANES:(m + 1) * LANES] = (
                jnp.dot(xr_s[b, :, cs], wcr_ref[m], preferred_element_type=F32)
                - jnp.dot(xi_s[b, :, cs], wci_ref[m], preferred_element_type=F32))


def _s5_direction(z3, perm, wbr, wbi, atab, apr, api, wcr, wci, reverse):
    steps = PB // S5_Q
    chunk = lambda s: _scan_chunk(s, reverse, S5_Q)
    const2 = lambda s: (0, 0)
    const3 = lambda s: (0, 0, 0)
    return pl.pallas_call(
        functools.partial(_s5_dir_kernel, reverse=reverse),
        grid=(steps,),
        in_specs=[pl.BlockSpec((BATCH, S5_Q, D_S5), lambda s: (0, chunk(s), Z_D // D_S5)),
                  pl.BlockSpec((S5_Q, S5_Q), const2),
                  pl.BlockSpec(wbr.shape, const3),
                  pl.BlockSpec(wbi.shape, const3),
                  pl.BlockSpec(atab.shape, const2),
                  pl.BlockSpec(apr.shape, const2),
                  pl.BlockSpec(api.shape, const2),
                  pl.BlockSpec(wcr.shape, const3),
                  pl.BlockSpec(wci.shape, const3)],
        out_specs=pl.BlockSpec((BATCH, S5_Q, D_S5), lambda s: (0, chunk(s), 0)),
        out_shape=jax.ShapeDtypeStruct((BATCH, PB, D_S5), F32),
        scratch_shapes=[pltpu.VMEM((BATCH, S5_Q, S5_LANES), F32), pltpu.VMEM((BATCH, S5_Q, S5_LANES), F32),
                        pltpu.VMEM((BATCH, S5_Q, S5_LANES), BF16), pltpu.VMEM((BATCH, S5_Q, S5_LANES), BF16),
                        pltpu.VMEM((BATCH, 8, S5_LANES), F32), pltpu.VMEM((BATCH, 16, S5_LANES), F32)],
        compiler_params=_cparams("arbitrary"),
        name="s5_bwd" if reverse else "s5_fwd",
    )(z3, perm, wbr, wbi, atab, apr, api, wcr, wci).reshape(T_ALL, D_S5)


def _s5_final_kernel(yf_ref, yb_ref, u_ref, permt_ref, d_ref, gw_ref, gb_ref, onw_ref, o_ref):
    y = _dot3_left(permt_ref[...], yf_ref[...] + yb_ref[...])
    y = jax.nn.gelu(y + d_ref[...] * u_ref[...])
    gate = jax.nn.sigmoid(jnp.dot(y.astype(BF16), gw_ref[...], preferred_element_type=F32) + gb_ref[...])
    o_ref[...] = (_rms(y * gate) * onw_ref[...]).astype(o_ref.dtype)


def _s5_final(yf, yb, z, permt, d_skip, glu_w, glu_b, onw):
    row = lambda i: (i, 0)
    const = lambda i: (0, 0)
    return pl.pallas_call(
        _s5_final_kernel,
        grid=(T_ALL // S5_Q,),
        in_specs=[pl.BlockSpec((S5_Q, D_S5), row),
                  pl.BlockSpec((S5_Q, D_S5), row),
                  pl.BlockSpec((S5_Q, D_S5), lambda i: (i, Z_D // D_S5)),
                  pl.BlockSpec((S5_Q, S5_Q), const),
                  pl.BlockSpec((1, D_S5), const),
                  pl.BlockSpec((D_S5, D_S5), const),
                  pl.BlockSpec((1, D_S5), const),
                  pl.BlockSpec((1, D_S5), const)],
        out_specs=pl.BlockSpec((S5_Q, D_S5), row),
        out_shape=jax.ShapeDtypeStruct((T_ALL, D_S5), BF16),
        compiler_params=_cparams("arbitrary"),
        name="s5_final",
    )(yf, yb, z, permt, d_skip.reshape(1, D_S5), glu_w.astype(BF16), glu_b.reshape(1, D_S5), onw.reshape(1, D_S5))


def _s5_tables(lam_re, lam_im, log_dt, b_re, b_im, c_re, c_im):
    dt = jnp.exp(log_dt)[..., None]
    mag = jnp.exp(lam_re * dt)
    ar = mag * jnp.cos(lam_im * dt)
    ai = mag * jnp.sin(lam_im * dt)
    den = lam_re * lam_re + lam_im * lam_im
    cr_ = ((ar - 1.0) * lam_re + ai * lam_im) / den
    ci_ = (ai * lam_re - (ar - 1.0) * lam_im) / den
    bbr = cr_[..., None] * b_re - ci_[..., None] * b_im
    bbi = cr_[..., None] * b_im + ci_[..., None] * b_re
    gps = S5_SLAB // P_S5
    eye = jnp.eye(gps, dtype=F32)

    def drive_w(bb):
        t = bb.reshape(S5_NSLAB, gps, P_S5, S5_GROUP)
        w = jnp.einsum('mgpc,gh->mgchp', t, eye)
        return w.reshape(S5_NSLAB, gps * S5_GROUP, gps * P_S5).astype(BF16)

    def read_w(cc):
        t = cc.reshape(S5_NSLAB, gps, S5_GROUP, P_S5)
        w = jnp.einsum('mgcp,gh->mgphc', t, eye)
        return w.reshape(S5_NSLAB, gps * P_S5, gps * S5_GROUP).astype(BF16)

    steps = jnp.arange(1, S5_SEG + 1, dtype=F32)[:, None]
    out = []
    for d in range(2):
        decay = (lam_re[d] * dt[d]).reshape(1, S5_LANES)
        angle = (lam_im[d] * dt[d]).reshape(1, S5_LANES)
        pmag = jnp.exp(steps * decay)
        pr = pmag * jnp.cos(steps * angle)
        pi = pmag * jnp.sin(steps * angle)
        if d == 1:
            apr, api = jnp.repeat(pr[::-1], 8, axis=0), jnp.repeat(pi[::-1], 8, axis=0)
        else:
            apr, api = jnp.repeat(pr, 8, axis=0), jnp.repeat(pi, 8, axis=0)
        atab = jnp.concatenate([jnp.broadcast_to(pr[0:1], (8, S5_LANES)), jnp.broadcast_to(pi[0:1], (8, S5_LANES)),
                                pr[S5_SEG - 1:], pi[S5_SEG - 1:], jnp.zeros((6, S5_LANES), F32)], axis=0)
        out.append((drive_w(bbr[d]), drive_w(bbi[d]), atab, apr, api))
    r = jnp.arange(S5_Q)
    src = (r % 8) * S5_SEG + r // 8
    perm = (src[:, None] == jnp.arange(S5_Q)[None, :]).astype(BF16)
    return out, read_w(c_re), read_w(c_im), perm


def _s5_mixer(z, p, onw):
    dirs, wcr, wci, perm = _s5_tables(p["s5_lam_re"], p["s5_lam_im"], p["s5_log_dt"], p["s5_b_re"], p["s5_b_im"],
                                      p["s5_c_re"], p["s5_c_im"])
    z3 = z.reshape(BATCH, PB, Z_COLS)
    yf = _s5_direction(z3, perm, *dirs[0], wcr, wci, reverse=False)
    yb = _s5_direction(z3, perm, *dirs[1], wcr, wci, reverse=True)
    return _s5_final(yf, yb, z, perm.T, p["s5_d"], p["s5_glu_w"], p["s5_glu_b"], onw)


def _rope_tables():
    pos = np.arange(SEQ)
    row = (pos // GRID_W).astype(np.float32)
    col = (pos % GRID_W).astype(np.float32)
    inv_freq = (ROPE_BASE ** (-np.arange(ROPE_AXIS // 2, dtype=np.float32) * 2.0 / ROPE_AXIS)).astype(np.float32)
    ang_r = row[:, None] * inv_freq
    ang_c = col[:, None] * inv_freq
    zeros = np.zeros((SEQ, LANES - D_ROPE), np.float32)
    cos = np.concatenate([np.cos(ang_r), np.cos(ang_r), np.cos(ang_c), np.cos(ang_c), zeros], axis=1)
    sin = np.concatenate([np.sin(ang_r), np.sin(ang_r), np.sin(ang_c), np.sin(ang_c), zeros], axis=1)
    cos_c = np.concatenate([np.ones((TILE, D_ROPE), np.float32), np.zeros((TILE, LANES - D_ROPE), np.float32)], axis=1)
    sin_c = np.zeros((TILE, LANES), np.float32)
    table = np.concatenate([np.concatenate([cos, sin], axis=1), np.concatenate([cos_c, sin_c], axis=1)], axis=0)
    return jnp.asarray(table.astype(np.float32))


def _layout_mla(w_uq, w_ukv):
    k = w_uq.shape[0]
    qa, qb, wk, wv = [], [], [], []
    for h in range(H_A):
        base = h * (D_NOPE + D_ROPE)
        rope = w_uq[:, base + D_NOPE:base + D_NOPE + D_ROPE]
        qa += [w_uq[:, base:base + D_NOPE], rope, jnp.zeros((k, LANES - D_ROPE), w_uq.dtype)]
        qb += [_rot_cols(rope), jnp.zeros((k, LANES - D_ROPE), w_uq.dtype)]
        kb = h * (D_NOPE + D_V)
        wk.append(w_ukv[:, kb:kb + D_NOPE])
        wv.append(w_ukv[:, kb + D_NOPE:kb + D_NOPE + D_V])
    cat = lambda xs: jnp.concatenate(xs, axis=1).astype(BF16)
    return cat(qa), cat(qb), cat(wk), cat(wv)


def _layer(xall, hx, modtab, p, big, l, cs, norm_w, next_modtab, last):
    z = _mm(hx, _layout_w_in(big["w_in"], l), MM_ROWS, Z_COLS // 3)
    onw = p["out_norm_w"]
    wqa, wqb, wk, wv = _layout_mla(p["mla_w_uq"], p["mla_w_ukv"])
    q, k, v = _mla_prep(z, cs, p["mla_q_norm_w"], p["mla_kv_norm_w"], wqa, wqb, wk, wv)
    ya = _attention(q, k, v, onw[:D_MLA].reshape(1, D_MLA))
    yb = _ssd_mixer(z, p, onw[D_MLA:D_MLA + D_SSD])
    yc = _mlstm_mixer(z, p, onw[D_MLA + D_SSD:D_MLA + D_SSD + D_MLSTM])
    yd = _s5_mixer(z, p, onw[D_MLA + D_SSD + D_MLSTM:])
    xall = _outproj([ya, yb, yc, yd], big["w_out"], l, xall, modtab, 2)

    w_router = jnp.concatenate([p["moe_w_group"], p["moe_w_expert"],
                                jnp.zeros((D_MODEL, ROUTER_COLS - N_GROUPS - N_EXPERTS), F32)], axis=1)
    wr_hi = w_router.astype(BF16)
    wr_lo = (w_router - wr_hi.astype(F32)).astype(BF16)
    h2, route, route_t, cnt = _prenorm_router(xall, p["norm2_w"], modtab, jnp.stack([wr_hi, wr_lo]), 3, 4, last)
    return _moe(xall, h2, route, route_t, cnt, modtab, big["moe_w_gate"], big["moe_w_up"], big["moe_w_down"], l,
                norm_w, next_modtab, last)


def kernel(x, c, ctx, c_ctx, mod_w, mod_b, norm1_w, w_in, mla_q_norm_w, mla_kv_norm_w, mla_w_uq, mla_w_ukv,
           ssd_conv_w, ssd_conv_b, ssd_a_log, ssd_dt_bias, ssd_d, mlstm_gate_b, s5_lam_re, s5_lam_im,
           s5_log_dt, s5_b_re, s5_b_im, s5_c_re, s5_c_im, s5_d, s5_glu_w, s5_glu_b, out_norm_w, w_out,
           norm2_w, moe_w_group, moe_w_expert, moe_w_gate, moe_w_up, moe_w_down, final_norm_w):
    stacked = {"norm1_w": norm1_w, "mla_q_norm_w": mla_q_norm_w, "mla_kv_norm_w": mla_kv_norm_w,
               "mla_w_uq": mla_w_uq, "mla_w_ukv": mla_w_ukv, "ssd_conv_w": ssd_conv_w, "ssd_conv_b": ssd_conv_b,
               "ssd_a_log": ssd_a_log, "ssd_dt_bias": ssd_dt_bias, "ssd_d": ssd_d, "mlstm_gate_b": mlstm_gate_b,
               "s5_lam_re": s5_lam_re, "s5_lam_im": s5_lam_im, "s5_log_dt": s5_log_dt, "s5_b_re": s5_b_re,
               "s5_b_im": s5_b_im, "s5_c_re": s5_c_re, "s5_c_im": s5_c_im, "s5_d": s5_d, "s5_glu_w": s5_glu_w,
               "s5_glu_b": s5_glu_b, "out_norm_w": out_norm_w, "norm2_w": norm2_w,
               "moe_w_group": moe_w_group, "moe_w_expert": moe_w_expert}
    big = {"w_in": w_in, "w_out": w_out, "moe_w_gate": moe_w_gate, "moe_w_up": moe_w_up, "moe_w_down": moe_w_down}
    cs = _rope_tables()
    cc = jnp.concatenate([c, c_ctx[None, :], jnp.zeros((8 - BATCH - 1, D_MODEL), F32)], axis=0)
    modtabs = [_modulation(cc, mod_w, mod_b, l)[:BATCH + 1].reshape(BATCH + 1, 6, D_MODEL) for l in range(DEPTH)]
    xall, hx = _prenorm(x, ctx, norm1_w[0], modtabs[0], 0, 1)
    for l in range(DEPTH):
        p = {name: val[l] for name, val in stacked.items()}
        if l == DEPTH - 1:
            out = _layer(xall, hx, modtabs[l], p, big, l, cs, final_norm_w, modtabs[l], True)
        else:
            xall, hx = _layer(xall, hx, modtabs[l], p, big, l, cs, norm1_w[l + 1], modtabs[l + 1], False)
    return out.reshape(BATCH, SEQ, D_MODEL)
```

```python
import functools
import math

import jax
import jax.numpy as jnp
import numpy as np
from jax import lax
from jax.experimental import pallas as pl
from jax.experimental.pallas import tpu as pltpu

F32 = jnp.float32
BF16 = jnp.bfloat16

D_MODEL = 2048
BATCH = 2
SEQ = 4096
DEPTH = 2
GRID_W = 64
CTX_LEN = 256
EPS = 1e-6
NEG_STATE = -1e30
NEG_BIG = -1e30

H_A = 4
D_NOPE = 128
D_ROPE = 64
D_V = 128
Q_RANK = 384
KV_RANK = 128
ROPE_AXIS = D_ROPE // 2
ROPE_BASE = 10000.0
D_MLA = H_A * D_V
D_SSD = 512
P_B = 64
H_B = D_SSD // P_B
G_B = 2
N_B = 128
SSD_CONV = 5
SSD_CHUNK = 128
CONV_CH = D_SSD + 2 * G_B * N_B
D_MLSTM = 512
H_C = 4
DV_C = D_MLSTM // H_C
DQK_C = DV_C // 2
MLSTM_CHUNK = 128
D_S5 = 512
S5_GROUP = 16
G_S5 = D_S5 // S5_GROUP
P_S5 = 64
A_COLS = Q_RANK + KV_RANK + D_ROPE
B_COLS = D_SSD + CONV_CH + 2 * H_B
C_COLS = 2 * H_C * DQK_C + 2 * D_MLSTM + 4 * H_C
N_GROUPS = 4
EXPERTS_PER_GROUP = 8
N_EXPERTS = N_GROUPS * EXPERTS_PER_GROUP
TOP_K = 2
D_EXPERT = 512

PB = CTX_LEN + SEQ
T_X = BATCH * SEQ
T_ALL = BATCH * PB

LANES = 128
VMEM_LIMIT_BYTES = 56 * 1024 * 1024

TILE = 256
TPB = PB // TILE
XT = SEQ // TILE
MM_ROWS = 512

Z_CQ = 0
Z_CKV = 384
Z_KR = 512
Z_KRR = 640
Z_MQ = 768
Z_XBC = 1024
Z_SZ = 2048
Z_MV = 2560
Z_MO = 3072
Z_D = 3584
Z_MK = 4096
Z_DT = 4352
Z_MG = 4480
Z_COLS = 4608
GATE_LANE0 = 2 * H_B

MOE_ROWS = 256
ROUTER_COLS = 128

_NT = (((1,), (1,)), ((), ()))
_TN = (((0,), (0,)), ((), ()))


def _cparams(*sem):
    return pltpu.CompilerParams(dimension_semantics=sem, vmem_limit_bytes=VMEM_LIMIT_BYTES)


def _tile_mod_row(i):
    return jnp.where(i % TPB == 0, BATCH, i // TPB)


def _token_tile(i, latent_only):
    return (i // XT) * TPB + 1 + i % XT if latent_only else i


def _rms(x, w=None):
    y = x * lax.rsqrt(jnp.mean(x * x, axis=-1, keepdims=True) + EPS)
    return y if w is None else y * w


def _in_turn(chains):
    live = list(chains)
    while live:
        nxt = []
        for chain in live:
            try:
                next(chain)
                nxt.append(chain)
            except StopIteration:
                pass
        live = nxt
        yield


def _round_robin(chains):
    for _ in _in_turn(chains):
        pass


def _split3(x):
    hi = x.astype(BF16)
    r1 = x - hi.astype(F32)
    mid = r1.astype(BF16)
    lo = (r1 - mid.astype(F32)).astype(BF16)
    return hi, mid, lo


def _dot3_left(sel, x):
    hi, mid, lo = _split3(x)
    return (jnp.dot(sel, hi, preferred_element_type=F32) + jnp.dot(sel, mid, preferred_element_type=F32)
            + jnp.dot(sel, lo, preferred_element_type=F32))


def _dot3_right(x, sel):
    hi, mid, lo = _split3(x)
    return (jnp.dot(hi, sel, preferred_element_type=F32) + jnp.dot(mid, sel, preferred_element_type=F32)
            + jnp.dot(lo, sel, preferred_element_type=F32))


def _mod_kernel(a_ref, w_ref, b_ref, o_ref):
    a = a_ref[...]
    a = a * jax.nn.sigmoid(a)
    o_ref[...] = jnp.dot(a.astype(BF16), w_ref[...].astype(BF16), preferred_element_type=F32) + b_ref[...]


def _modulation(cc, mod_w, mod_b, l):
    n = mod_w.shape[2]
    tn = 1024
    return pl.pallas_call(
        _mod_kernel,
        grid=(n // tn,),
        in_specs=[pl.BlockSpec((8, D_MODEL), lambda j: (0, 0)),
                  pl.BlockSpec((None, D_MODEL, tn), lambda j: (l, 0, j)),
                  pl.BlockSpec((None, 1, tn), lambda j: (l, 0, j))],
        out_specs=pl.BlockSpec((8, tn), lambda j: (0, j)),
        out_shape=jax.ShapeDtypeStruct((8, n), F32),
        compiler_params=_cparams("arbitrary"),
        name="modulation",
    )(cc, mod_w, mod_b.reshape(DEPTH, 1, n))


def _prenorm_kernel(x_ref, c_ref, w_ref, mod_ref, xall_ref, o_ref, *, sh_idx, sc_idx):
    v = jnp.where(pl.program_id(0) % TPB == 0, c_ref[...], x_ref[...])
    xall_ref[...] = v
    y = _rms(v, w_ref[...])
    m = mod_ref[0]
    y = y * (1.0 + m[sc_idx:sc_idx + 1, :]) + m[sh_idx:sh_idx + 1, :]
    o_ref[...] = y.astype(o_ref.dtype)


def _prenorm(x, ctx, w, modtab, sh_idx, sc_idx):
    row = lambda i: (i, 0)
    return pl.pallas_call(
        functools.partial(_prenorm_kernel, sh_idx=sh_idx, sc_idx=sc_idx),
        grid=(T_ALL // TILE,),
        in_specs=[pl.BlockSpec((TILE, D_MODEL), lambda i: ((i // TPB) * XT + jnp.maximum(i % TPB - 1, 0), 0)),
                  pl.BlockSpec((TILE, D_MODEL), lambda i: (i // TPB, 0)),
                  pl.BlockSpec((1, D_MODEL), lambda i: (0, 0)),
                  pl.BlockSpec((1, 6, D_MODEL), lambda i: (_tile_mod_row(i), 0, 0))],
        out_specs=[pl.BlockSpec((TILE, D_MODEL), row), pl.BlockSpec((TILE, D_MODEL), row)],
        out_shape=[jax.ShapeDtypeStruct((T_ALL, D_MODEL), F32), jax.ShapeDtypeStruct((T_ALL, D_MODEL), BF16)],
        compiler_params=_cparams("arbitrary"),
        name="prenorm",
    )(x.reshape(T_X, D_MODEL), ctx.reshape(BATCH * CTX_LEN, D_MODEL), w.reshape(1, D_MODEL), modtab)


def _mm_kernel(a_ref, w_ref, o_ref):
    o_ref[...] = jnp.dot(a_ref[...], w_ref[...], preferred_element_type=F32).astype(o_ref.dtype)


def _mm(a, w, tm, tn, out_dtype=F32):
    m, k = a.shape
    n = w.shape[1]
    return pl.pallas_call(
        _mm_kernel,
        grid=(n // tn, m // tm),
        in_specs=[pl.BlockSpec((tm, k), lambda j, i: (i, 0)),
                  pl.BlockSpec((k, tn), lambda j, i: (0, j))],
        out_specs=pl.BlockSpec((tm, tn), lambda j, i: (i, j)),
        out_shape=jax.ShapeDtypeStruct((m, n), out_dtype),
        compiler_params=_cparams("arbitrary", "arbitrary"),
        name="in_proj",
    )(a, w)


W_IN_B0 = A_COLS
W_IN_C0 = A_COLS + B_COLS
W_IN_D0 = A_COLS + B_COLS + C_COLS
W_IN_CM = W_IN_C0 + 2 * H_C * DQK_C
W_IN_GB = W_IN_CM + 2 * D_MLSTM
W_IN_WIDE = ((Z_CQ, 0, Q_RANK + KV_RANK),
             (Z_MQ, W_IN_C0, H_C * DQK_C),
             (Z_XBC, W_IN_B0 + D_SSD, CONV_CH),
             (Z_SZ, W_IN_B0, D_SSD),
             (Z_MV, W_IN_CM, 2 * D_MLSTM),
             (Z_D, W_IN_D0, D_S5),
             (Z_MK, W_IN_C0 + H_C * DQK_C, H_C * DQK_C))
W_IN_ROWS = 256


def _rot_cols(w):
    q = ROPE_AXIS // 2
    return jnp.concatenate([-w[:, q:2 * q], w[:, 0:q], -w[:, 3 * q:4 * q], w[:, 2 * q:3 * q]], axis=1)


W_IN_KR0 = Q_RANK + KV_RANK
W_IN_DT0 = W_IN_B0 + D_SSD + CONV_CH
W_IN_WINDOWS = tuple(c // LANES * LANES for c in (W_IN_KR0, W_IN_DT0, W_IN_GB))


def _narrow_selector():
    sel = np.zeros((3 * LANES, 4 * LANES), np.float32)

    def put(window, src_col, dst_col, width, sign=1.0):
        base = window * LANES + src_col - W_IN_WINDOWS[window]
        for c in range(width):
            sel[base + c, dst_col + c] = sign

    q = ROPE_AXIS // 2
    put(0, W_IN_KR0, 0, D_ROPE)
    for dst, src, sign in ((0, q, -1.0), (q, 0, 1.0), (2 * q, 3 * q, -1.0), (3 * q, 2 * q, 1.0)):
        put(0, W_IN_KR0 + src, LANES + dst, q, sign)
    put(1, W_IN_DT0, 2 * LANES, 2 * H_B)
    for d in range(2):
        put(2, W_IN_GB + (2 * d + 1) * H_C, 2 * LANES + GATE_LANE0 + d * H_C, H_C)
        put(2, W_IN_GB + 2 * d * H_C, 3 * LANES + GATE_LANE0 + d * H_C, H_C)
    return jnp.asarray(sel, dtype=BF16)


def _w_in_layout_kernel(w_ref, sel_ref, o_ref):
    for dst, src, width in W_IN_WIDE:
        o_ref[:, dst:dst + width] = w_ref[:, src:src + width].astype(BF16)
    windows = jnp.concatenate([w_ref[:, c:c + LANES] for c in W_IN_WINDOWS], axis=1).astype(BF16)
    small = jnp.dot(windows, sel_ref[...], preferred_element_type=F32).astype(BF16)
    o_ref[:, Z_KR:Z_MQ] = small[:, 0:2 * LANES]
    o_ref[:, Z_DT:Z_COLS] = small[:, 2 * LANES:4 * LANES]


def _layout_w_in(w_in, l):
    _, k, n = w_in.shape
    sel = _narrow_selector()
    return pl.pallas_call(
        _w_in_layout_kernel,
        grid=(k // W_IN_ROWS,),
        in_specs=[pl.BlockSpec((None, W_IN_ROWS, n), lambda i: (l, i, 0)),
                  pl.BlockSpec(sel.shape, lambda i: (0, 0))],
        out_specs=pl.BlockSpec((W_IN_ROWS, Z_COLS), lambda i: (i, 0)),
        out_shape=jax.ShapeDtypeStruct((k, Z_COLS), BF16),
        compiler_params=_cparams("arbitrary"),
        name="w_in_layout",
    )(w_in, sel)


def _mla_prep_kernel(za_ref, cs_ref, qw_ref, kvw_ref, wqa_ref, wqb_ref, wk_ref, wv_ref, q_ref, k_ref, v_ref):
    za = za_ref[...]
    cos = cs_ref[:, :LANES]
    sin = cs_ref[:, LANES:]
    qn = _rms(za[:, Z_CQ:Z_CQ + Q_RANK], qw_ref[...]).astype(BF16)
    kvn = _rms(za[:, Z_CKV:Z_CKV + KV_RANK], kvw_ref[...]).astype(BF16)
    qa = jnp.dot(qn, wqa_ref[...], preferred_element_type=F32)
    qb = jnp.dot(qn, wqb_ref[...], preferred_element_type=F32)
    kn = jnp.dot(kvn, wk_ref[...], preferred_element_type=F32)
    v = jnp.dot(kvn, wv_ref[...], preferred_element_type=F32)
    kr = (za[:, Z_KR:Z_KR + LANES] * cos + za[:, Z_KRR:Z_KRR + LANES] * sin).astype(BF16)
    for h in range(H_A):
        c0 = h * 2 * LANES
        q_ref[:, c0:c0 + LANES] = qa[:, c0:c0 + LANES].astype(BF16)
        q_ref[:, c0 + LANES:c0 + 2 * LANES] = (
            qa[:, c0 + LANES:c0 + 2 * LANES] * cos + qb[:, h * LANES:(h + 1) * LANES] * sin).astype(BF16)
        k_ref[:, c0:c0 + LANES] = kn[:, h * LANES:(h + 1) * LANES].astype(BF16)
        k_ref[:, c0 + LANES:c0 + 2 * LANES] = kr
    v_ref[...] = v.astype(BF16)


ATT_W = H_A * 2 * LANES


def _mla_prep(z, cs, qw, kvw, wqa, wqb, wk, wv):
    const = lambda i: (0, 0)
    rope_blk = lambda i: (jnp.where(i % TPB == 0, XT, i % TPB - 1), 0)
    return pl.pallas_call(
        _mla_prep_kernel,
        grid=(T_ALL // TILE,),
        in_specs=[pl.BlockSpec((TILE, Z_MQ), lambda i: (i, 0)),
                  pl.BlockSpec((TILE, 2 * LANES), rope_blk),
                  pl.BlockSpec((1, Q_RANK), const),
                  pl.BlockSpec((1, KV_RANK), const),
                  pl.BlockSpec(wqa.shape, const),
                  pl.BlockSpec(wqb.shape, const),
                  pl.BlockSpec(wk.shape, const),
                  pl.BlockSpec(wv.shape, const)],
        out_specs=[pl.BlockSpec((TILE, ATT_W), lambda i: (i, 0)),
                   pl.BlockSpec((TILE, ATT_W), lambda i: (i, 0)),
                   pl.BlockSpec((TILE, D_MLA), lambda i: (i, 0))],
        out_shape=[jax.ShapeDtypeStruct((T_ALL, ATT_W), BF16),
                   jax.ShapeDtypeStruct((T_ALL, ATT_W), BF16),
                   jax.ShapeDtypeStruct((T_ALL, D_MLA), BF16)],
        compiler_params=_cparams("arbitrary"),
        name="mla_prep",
    )(z, cs, qw.reshape(1, Q_RANK), kvw.reshape(1, KV_RANK), wqa, wqb, wk, wv)


def _attn_tile(q_ref, k_ref, v_ref, w_ref, o_ref, acc_ref, n_keys):
    scale2 = (D_NOPE + D_ROPE) ** -0.5 * math.log2(math.e)
    for h in range(H_A):
        q = q_ref[:, h * 2 * LANES:(h + 1) * 2 * LANES]
        s = lax.dot_general(q, k_ref[0:n_keys, h * 2 * LANES:(h + 1) * 2 * LANES], _NT, preferred_element_type=F32)
        m = jnp.max(s, axis=-1, keepdims=True)
        p = jnp.exp2((s - m) * scale2)
        l = jnp.sum(p, axis=-1, keepdims=True)
        o = jnp.dot(p.astype(BF16), v_ref[0:n_keys, h * D_V:(h + 1) * D_V], preferred_element_type=F32)
        acc_ref[:, h * D_V:(h + 1) * D_V] = o / l
    o_ref[...] = (_rms(acc_ref[...]) * w_ref[...]).astype(o_ref.dtype)


def _attn_kernel(q_ref, k_ref, v_ref, w_ref, o_ref, acc_ref):
    @pl.when(pl.program_id(1) == 0)
    def _():
        _attn_tile(q_ref, k_ref, v_ref, w_ref, o_ref, acc_ref, CTX_LEN)

    @pl.when(pl.program_id(1) != 0)
    def _():
        _attn_tile(q_ref, k_ref, v_ref, w_ref, o_ref, acc_ref, PB)


def _attention(q, k, v, onw):
    return pl.pallas_call(
        _attn_kernel,
        grid=(BATCH, TPB),
        in_specs=[pl.BlockSpec((TILE, ATT_W), lambda b, i: (b * TPB + i, 0)),
                  pl.BlockSpec((None, PB, ATT_W), lambda b, i: (b, 0, 0)),
                  pl.BlockSpec((None, PB, D_MLA), lambda b, i: (b, 0, 0)),
                  pl.BlockSpec((1, D_MLA), lambda b, i: (0, 0))],
        out_specs=pl.BlockSpec((TILE, D_MLA), lambda b, i: (b * TPB + i, 0)),
        out_shape=jax.ShapeDtypeStruct((T_ALL, D_MLA), BF16),
        scratch_shapes=[pltpu.VMEM((TILE, D_MLA), F32)],
        compiler_params=_cparams("arbitrary", "arbitrary"),
        name="attention",
    )(q, k.reshape(BATCH, PB, ATT_W), v.reshape(BATCH, PB, D_MLA), onw)


R_E0, R_E1, R_W0, R_W1, R_K0, R_K1 = range(6)
NO_LANE = 2 * LANES


def _prenorm_router_kernel(x_ref, w_ref, mod_ref, wr_ref, tri_ref, h_ref, route_ref, route_t_ref, cnt_ref, cnt_s,
                           *, sh_idx, sc_idx):
    @pl.when(pl.program_id(0) == 0)
    def _():
        cnt_s[...] = jnp.zeros_like(cnt_s)

    y = _rms(x_ref[...], w_ref[...])
    m = mod_ref[0]
    y = y * (1.0 + m[sc_idx:sc_idx + 1, :]) + m[sh_idx:sh_idx + 1, :]
    h_ref[...] = y
    y_hi = y.astype(BF16)
    y_lo = (y - y_hi.astype(F32)).astype(BF16)
    lg = (jnp.dot(y_hi, wr_ref[0], preferred_element_type=F32) + jnp.dot(y_lo, wr_ref[0], preferred_element_type=F32)
          + jnp.dot(y_hi, wr_ref[1], preferred_element_type=F32))
    lane = lax.broadcasted_iota(jnp.int32, lg.shape, 1)

    def first_max(v):
        top = jnp.max(v, axis=1, keepdims=True)
        return top, jnp.min(jnp.where(v == top, lane, NO_LANE), axis=1, keepdims=True)

    is_g = lane < N_GROUPS
    g_top, g_idx = first_max(jnp.where(is_g, lg, NEG_BIG))
    g_w = 1.0 / jnp.sum(jnp.where(is_g, jnp.exp(lg - g_top), 0.0), axis=1, keepdims=True)
    lo = N_GROUPS + EXPERTS_PER_GROUP * g_idx
    el = jnp.where(jnp.logical_and(lane >= lo, lane < lo + EXPERTS_PER_GROUP), lg, NEG_BIG)
    v1, i1 = first_max(el)
    v2, i2 = first_max(jnp.where(lane == i1, NEG_BIG, el))
    t = jnp.exp(v2 - v1)
    w0 = g_w / (1.0 + t)
    w1 = g_w * t / (1.0 + t)
    e0 = i1 - N_GROUPS
    e1 = i2 - N_GROUPS
    hit0 = lane == e0
    hit1 = lane == e1
    onehot = jnp.logical_or(hit0, hit1).astype(F32)
    before = cnt_s[0:1, :] + jnp.dot(tri_ref[...], onehot.astype(BF16), preferred_element_type=F32)
    k0 = jnp.sum(jnp.where(hit0, before, 0.0), axis=1, keepdims=True)
    k1 = jnp.sum(jnp.where(hit1, before, 0.0), axis=1, keepdims=True)
    cnt_s[0:1, :] = cnt_s[0:1, :] + jnp.sum(onehot, axis=0, keepdims=True)
    cnt_ref[...] = cnt_s[...]
    rec = jnp.zeros(lg.shape, F32)
    for ln, val in ((R_E0, e0.astype(F32)), (R_E1, e1.astype(F32)), (R_W0, w0), (R_W1, w1), (R_K0, k0), (R_K1, k1)):
        rec = jnp.where(lane == ln, val, rec)
    route_ref[...] = rec
    route_t_ref[...] = rec.T[0:8, :]


def _prenorm_router(x, w, modtab, w_router, sh_idx, sc_idx, latent_only):
    n_tok = T_X if latent_only else T_ALL
    r = jnp.arange(TILE)
    tri = (r[None, :] < r[:, None]).astype(BF16)
    src = lambda i: _token_tile(i, latent_only)
    return pl.pallas_call(
        functools.partial(_prenorm_router_kernel, sh_idx=sh_idx, sc_idx=sc_idx),
        grid=(n_tok // TILE,),
        in_specs=[pl.BlockSpec((TILE, D_MODEL), lambda i: (src(i), 0)),
                  pl.BlockSpec((1, D_MODEL), lambda i: (0, 0)),
                  pl.BlockSpec((1, 6, D_MODEL), lambda i: (_tile_mod_row(src(i)), 0, 0)),
                  pl.BlockSpec((2, D_MODEL, ROUTER_COLS), lambda i: (0, 0, 0)),
                  pl.BlockSpec((TILE, TILE), lambda i: (0, 0))],
        out_specs=[pl.BlockSpec((TILE, D_MODEL), lambda i: (i, 0)),
                   pl.BlockSpec((TILE, ROUTER_COLS), lambda i: (i, 0)),
                   pl.BlockSpec((8, TILE), lambda i: (0, i)),
                   pl.BlockSpec((8, ROUTER_COLS), lambda i: (0, 0))],
        out_shape=[jax.ShapeDtypeStruct((n_tok, D_MODEL), F32),
                   jax.ShapeDtypeStruct((n_tok, ROUTER_COLS), F32),
                   jax.ShapeDtypeStruct((8, n_tok), F32),
                   jax.ShapeDtypeStruct((8, ROUTER_COLS), F32)],
        scratch_shapes=[pltpu.VMEM((8, ROUTER_COLS), F32)],
        compiler_params=_cparams("arbitrary"),
        name="prenorm_router",
    )(x, w.reshape(1, D_MODEL), modtab, w_router, tri)


def _row_copy(src, src_row, dst, dst_row, sem):
    return pltpu.make_async_copy(src.at[pl.ds(src_row, 1)], dst.at[pl.ds(dst_row, 1)], sem)


def _dispatch_kernel(dest_ref, h_ref, xb_ref, sem):
    def issue(r, carry):
        for k in range(TOP_K):
            _row_copy(h_ref, r, xb_ref, dest_ref[0, k, r], sem).start(priority=k)
        return carry

    lax.fori_loop(0, TILE, issue, 0, unroll=8)

    def drain(r, carry):
        for k in range(TOP_K):
            _row_copy(h_ref, 0, xb_ref, 0, sem).wait()
        return carry

    lax.fori_loop(0, TILE, drain, 0, unroll=8)


def _dispatch(h, dest3):
    n_tok = h.shape[0]
    return pl.pallas_call(
        _dispatch_kernel,
        grid=(n_tok // TILE,),
        in_specs=[pl.BlockSpec((1, TOP_K, TILE), lambda i: (i, 0, 0), memory_space=pltpu.SMEM),
                  pl.BlockSpec((TILE, D_MODEL), lambda i: (i, 0))],
        out_specs=pl.BlockSpec(memory_space=pl.ANY),
        out_shape=jax.ShapeDtypeStruct((n_tok * TOP_K, D_MODEL), F32),
        scratch_shapes=[pltpu.SemaphoreType.DMA(())],
        compiler_params=_cparams("arbitrary"),
        name="moe_dispatch",
    )(dest3, h)


def _moe_kernel(tile_ref, exp_ref, lo_ref, hi_ref, flag_ref, x_ref, wg_ref, wu_ref, wd_ref, o_ref,
                wg_s, wu_s, wd_s):
    i = pl.program_id(0)
    flags = flag_ref[i]

    @pl.when(flags % 2 == 1)
    def _():
        wg_s[...] = wg_ref[...].astype(BF16)
        wu_s[...] = wu_ref[...].astype(BF16)
        wd_s[...] = wd_ref[...].astype(BF16)

    @pl.when(flags >= 4)
    def _():
        x = x_ref[...].astype(BF16)
        g = jnp.dot(x, wg_s[...], preferred_element_type=F32)
        u = jnp.dot(x, wu_s[...], preferred_element_type=F32)
        row = lax.broadcasted_iota(jnp.int32, g.shape, 0)
        mine = jnp.logical_and(row >= lo_ref[i], row < hi_ref[i])
        h = jnp.where(mine, g * jax.nn.sigmoid(g) * u, 0.0).astype(BF16)
        res = jnp.dot(h, wd_s[...], preferred_element_type=F32)

        @pl.when((flags // 2) % 2 == 1)
        def _():
            o_ref[...] = res

        @pl.when((flags // 2) % 2 == 0)
        def _():
            o_ref[...] += res


def _moe_experts(xb, meta, wg, wu, wd, l):
    n_items = meta[0].shape[0]
    wmap = lambda i, ti, ex, lo, hi, fl: (l, ex[i], 0, 0)
    xmap = lambda i, ti, ex, lo, hi, fl: (ti[i], 0)
    grid_spec = pltpu.PrefetchScalarGridSpec(
        num_scalar_prefetch=5,
        grid=(n_items,),
        in_specs=[pl.BlockSpec((MOE_ROWS, D_MODEL), xmap),
                  pl.BlockSpec((None, None, D_MODEL, D_EXPERT), wmap),
                  pl.BlockSpec((None, None, D_MODEL, D_EXPERT), wmap),
                  pl.BlockSpec((None, None, D_EXPERT, D_MODEL), wmap)],
        out_specs=pl.BlockSpec((MOE_ROWS, D_MODEL), xmap),
        scratch_shapes=[pltpu.VMEM((D_MODEL, D_EXPERT), BF16),
                        pltpu.VMEM((D_MODEL, D_EXPERT), BF16),
                        pltpu.VMEM((D_EXPERT, D_MODEL), BF16)])
    return pl.pallas_call(
        _moe_kernel,
        grid_spec=grid_spec,
        out_shape=jax.ShapeDtypeStruct(xb.shape, F32),
        compiler_params=_cparams("arbitrary"),
        name="moe_experts",
    )(*meta, xb, wg, wu, wd)


def _combine_kernel(dest_ref, next_ref, yb_ref, x_ref, route_ref, mod_ref, nw_ref, nmod_ref, *rest, final):
    if final:
        o_ref, buf, sem = rest
    else:
        o_ref, hx_ref, buf, sem = rest
    i = pl.program_id(0)
    slot = i % 2
    n_groups = TILE // COMBINE_ROWS

    def issue(d_ref, to_slot, j):
        for rr in range(COMBINE_ROWS):
            r = j * COMBINE_ROWS + rr
            for k in range(TOP_K):
                _row_copy(yb_ref, d_ref[0, k, r], buf.at[to_slot, k], r, sem.at[to_slot]).start(priority=k)

    @pl.when(i == 0)
    def _():
        def first(j, carry):
            issue(dest_ref, 0, j)
            return carry

        lax.fori_loop(0, n_groups, first, 0)

    def drain(r, carry):
        for k in range(TOP_K):
            _row_copy(yb_ref, 0, buf.at[slot, k], 0, sem.at[slot]).wait()
        return carry

    lax.fori_loop(0, TILE, drain, 0, unroll=8)
    g2 = mod_ref[0][5:6, :]
    nw = nw_ref[...]
    nm = nmod_ref[0]

    def combine(j):
        rows = pl.ds(pl.multiple_of(j * COMBINE_ROWS, COMBINE_ROWS), COMBINE_ROWS)
        rt = route_ref[rows, :]
        f = buf[slot, 0, rows, :] * rt[:, R_W0:R_W0 + 1] + buf[slot, 1, rows, :] * rt[:, R_W1:R_W1 + 1]
        y = x_ref[rows, :] + g2 * f
        if final:
            o_ref[rows, :] = _rms(y, nw)
        else:
            o_ref[rows, :] = y
            hx_ref[rows, :] = (_rms(y, nw) * (1.0 + nm[1:2, :]) + nm[0:1, :]).astype(hx_ref.dtype)

    @pl.when(i + 1 < pl.num_programs(0))
    def _():
        def body(j, carry):
            combine(j)
            issue(next_ref, 1 - slot, j)
            return carry

        lax.fori_loop(0, n_groups, body, 0)

    @pl.when(i + 1 >= pl.num_programs(0))
    def _():
        def body(j, carry):
            combine(j)
            return carry

        lax.fori_loop(0, n_groups, body, 0)


COMBINE_ROWS = 64


def _combine(yb, dest3, x, route, modtab, norm_w, next_modtab, latent_only):
    n_tok = route.shape[0]
    n_tiles = n_tok // TILE
    src = lambda i: _token_tile(i, latent_only)
    row = lambda i: (i, 0)
    mod_spec = pl.BlockSpec((1, 6, D_MODEL), lambda i: (_tile_mod_row(src(i)), 0, 0))
    out_specs = [pl.BlockSpec((TILE, D_MODEL), row)]
    out_shape = [jax.ShapeDtypeStruct((n_tok, D_MODEL), F32)]
    if not latent_only:
        out_specs.append(pl.BlockSpec((TILE, D_MODEL), row))
        out_shape.append(jax.ShapeDtypeStruct((n_tok, D_MODEL), BF16))
    return pl.pallas_call(
        functools.partial(_combine_kernel, final=latent_only),
        grid=(n_tiles,),
        in_specs=[pl.BlockSpec((1, TOP_K, TILE), lambda i: (i, 0, 0), memory_space=pltpu.SMEM),
                  pl.BlockSpec((1, TOP_K, TILE), lambda i: (jnp.minimum(i + 1, n_tiles - 1), 0, 0),
                               memory_space=pltpu.SMEM),
                  pl.BlockSpec(memory_space=pl.ANY),
                  pl.BlockSpec((TILE, D_MODEL), lambda i: (src(i), 0)),
                  pl.BlockSpec((TILE, ROUTER_COLS), row),
                  mod_spec,
                  pl.BlockSpec((1, D_MODEL), lambda i: (0, 0)),
                  mod_spec],
        out_specs=out_specs,
        out_shape=out_shape,
        scratch_shapes=[pltpu.VMEM((2, TOP_K, TILE, D_MODEL), F32), pltpu.SemaphoreType.DMA((2,))],
        compiler_params=_cparams("arbitrary"),
        name="moe_combine",
    )(dest3, dest3, yb, x, route, modtab, norm_w.reshape(1, D_MODEL), next_modtab)


def _moe_plan(route_t, cnt):
    t = route_t.shape[1]
    n_tiles = t * TOP_K // MOE_ROWS
    n_items = n_tiles + N_EXPERTS - 1
    experts = route_t[R_E0:R_E1 + 1].astype(jnp.int32)
    rank = route_t[R_K0:R_K1 + 1].astype(jnp.int32)
    counts = cnt[0, :N_EXPERTS].astype(jnp.int32)
    ends = jnp.cumsum(counts)
    starts = ends - counts
    e_axis = jnp.arange(N_EXPERTS, dtype=jnp.int32)[:, None, None]
    dest = jnp.sum(jnp.where(experts[None] == e_axis, starts[:, None, None], 0), axis=0) + rank
    t_first = starts // MOE_ROWS
    per_e = jnp.where(counts > 0, (ends - 1) // MOE_ROWS - t_first + 1, 0)
    item_end = jnp.cumsum(per_e)
    item_start = item_end - per_e
    total = item_end[-1]
    w = jnp.arange(n_items, dtype=jnp.int32)
    valid = w < total
    wc = jnp.minimum(w, total - 1)
    ex = jnp.clip(jnp.searchsorted(item_end, wc, side='right'), 0, N_EXPERTS - 1).astype(jnp.int32)
    tile = (t_first[ex] + wc - item_start[ex]).astype(jnp.int32)
    lo = jnp.where(valid, jnp.maximum(starts[ex], tile * MOE_ROWS) - tile * MOE_ROWS, 0).astype(jnp.int32)
    hi = jnp.where(valid, jnp.minimum(ends[ex], (tile + 1) * MOE_ROWS) - tile * MOE_ROWS, 0).astype(jnp.int32)
    one = jnp.ones((1,), bool)
    new_e = jnp.concatenate([one, ex[1:] != ex[:-1]])
    new_t = jnp.concatenate([one, tile[1:] != tile[:-1]])
    flags = (new_e.astype(jnp.int32) + 2 * new_t.astype(jnp.int32) + 4 * valid.astype(jnp.int32))
    dest3 = dest.reshape(TOP_K, t // TILE, TILE).transpose(1, 0, 2)
    return dest3, (tile, ex, lo, hi, flags)


def _moe(x, h, route, route_t, cnt, modtab, wg, wu, wd, l, norm_w, next_modtab, latent_only):
    dest3, meta = _moe_plan(route_t, cnt)
    xb = _dispatch(h, dest3)
    yb = _moe_experts(xb, meta, wg, wu, wd, l)
    res = _combine(yb, dest3, x, route, modtab, norm_w, next_modtab, latent_only)
    return res[0] if latent_only else (res[0], res[1])


def _scan_chunk(s, rev, chunk):
    n_c = CTX_LEN // chunk
    n_all = PB // chunk
    if not rev:
        return s
    return jnp.where(s < n_c, n_c - 1 - s, n_all + n_c - 1 - s)


HALO = 8


def _ssd_prep_kernel(cur_ref, prev_ref, next_ref, dt_ref, cw_ref, cb_ref, dtb_ref, xo_ref, dto_ref, ext_s):
    j = pl.program_id(0) % TPB
    first = jnp.logical_or(j == 0, j == 1)
    last = jnp.logical_or(j == 0, j == TPB - 1)
    ext_s[0:HALO, :] = jnp.where(first, 0.0, prev_ref[...])
    ext_s[HALO:HALO + TILE, :] = cur_ref[...]
    ext_s[HALO + TILE:, :] = jnp.where(last, 0.0, next_ref[...])
    half = (SSD_CONV - 1) // 2
    acc = cb_ref[...] + cw_ref[0:1, :] * ext_s[HALO - half:HALO - half + TILE, :]
    for k in range(1, SSD_CONV):
        acc = acc + cw_ref[k:k + 1, :] * ext_s[HALO - half + k:HALO - half + k + TILE, :]
    xo_ref[...] = acc * jax.nn.sigmoid(acc)
    lane = lax.broadcasted_iota(jnp.int32, (TILE, LANES), 1)
    dto_ref[...] = jnp.where(lane < 2 * H_B, jax.nn.softplus(dt_ref[...] + dtb_ref[...]), 0.0)


def _ssd_prep(z, conv_w, conv_b, dt_bias):
    n_tiles = T_ALL // TILE
    per = TILE // HALO
    cwp = jnp.concatenate([conv_w, jnp.zeros((8 - SSD_CONV, CONV_CH), F32)], axis=0)
    dtb = jnp.concatenate([dt_bias.reshape(1, 2 * H_B), jnp.zeros((1, LANES - 2 * H_B), F32)], axis=1)
    xc = Z_XBC // CONV_CH
    return pl.pallas_call(
        _ssd_prep_kernel,
        grid=(n_tiles,),
        in_specs=[pl.BlockSpec((TILE, CONV_CH), lambda i: (i, xc)),
                  pl.BlockSpec((HALO, CONV_CH), lambda i: (jnp.maximum(i * per - 1, 0), xc)),
                  pl.BlockSpec((HALO, CONV_CH), lambda i: (jnp.minimum((i + 1) * per, T_ALL // HALO - 1), xc)),
                  pl.BlockSpec((TILE, LANES), lambda i: (i, Z_DT // LANES)),
                  pl.BlockSpec((8, CONV_CH), lambda i: (0, 0)),
                  pl.BlockSpec((1, CONV_CH), lambda i: (0, 0)),
                  pl.BlockSpec((1, LANES), lambda i: (0, 0))],
        out_specs=[pl.BlockSpec((TILE, CONV_CH), lambda i: (i, 0)),
                   pl.BlockSpec((TILE, LANES), lambda i: (i, 0))],
        out_shape=[jax.ShapeDtypeStruct((T_ALL, CONV_CH), F32),
                   jax.ShapeDtypeStruct((T_ALL, LANES), F32)],
        scratch_shapes=[pltpu.VMEM((TILE + 2 * HALO, CONV_CH), F32)],
        compiler_params=_cparams("arbitrary"),
        name="ssd_prep",
    )(z, z, z, z, cwp, conv_b.reshape(1, CONV_CH), dtb)


def _ssd_one_direction(xbc, dtp, arow, tri, expand, h_ref, y_ref, b, d, rev):
    q = SSD_CHUNK
    a = dtp * arow
    acum = _dot3_left(tri, a)
    yield
    acum_t = acum.T
    dt_t = dtp.T
    yield
    edge = 0 if rev else q - 1
    atot = acum[edge:edge + 1, :]
    pieces = jnp.concatenate([jnp.exp(atot - acum) * dtp, jnp.exp(acum),
                              jnp.broadcast_to(jnp.exp(atot), (8, LANES))], axis=0)
    ex = _dot3_right(pieces, expand)
    yield
    wend_x = ex[0:q]
    eacum_x = ex[q:2 * q]
    dec_x = ex[2 * q:2 * q + 1]
    xs = xbc[:, 0:D_SSD]
    xw = (xs * wend_x).astype(BF16)
    xs_b = xs.astype(BF16)
    h_old = h_ref[b]
    h_b = h_old.astype(BF16)
    ri = lax.broadcasted_iota(jnp.int32, (q, q), 0)
    ci = lax.broadcasted_iota(jnp.int32, (q, q), 1)
    mask = (ci >= ri) if rev else (ci <= ri)
    lo_half = lax.broadcasted_iota(jnp.int32, (q, LANES), 1) < P_B
    hpg = H_B // G_B
    gw = hpg * P_B
    yield

    def group(g):
        bg = xbc[:, D_SSD + g * N_B:D_SSD + (g + 1) * N_B].astype(BF16)
        cg = xbc[:, D_SSD + G_B * N_B + g * N_B:D_SSD + G_B * N_B + (g + 1) * N_B].astype(BF16)
        cb = lax.dot_general(cg, bg, _NT, preferred_element_type=F32)
        inter = jnp.dot(cg, h_b[:, g * gw:(g + 1) * gw], preferred_element_type=F32)
        upd = lax.dot_general(bg, xw[:, g * gw:(g + 1) * gw], _TN, preferred_element_type=F32)
        yield

        def pair(j):
            ms = []
            for hh in range(2):
                hc = H_B * d + hpg * g + 2 * j + hh
                seg = acum[:, hc:hc + 1] - acum_t[hc:hc + 1, :]
                dec = jnp.exp(jnp.where(mask, seg, NEG_BIG))
                ms.append((cb * dec * dt_t[hc:hc + 1, :]).astype(BF16))
                yield
            c0 = g * gw + 2 * j * P_B
            xp = xs_b[:, c0:c0 + LANES]
            zero = jnp.zeros_like(xp)
            rhs = jnp.concatenate([jnp.where(lo_half, xp, zero), jnp.where(lo_half, zero, xp)], axis=0)
            y_intra = jnp.dot(jnp.concatenate(ms, axis=1), rhs, preferred_element_type=F32)
            yield
            y_ref[b, :, c0:c0 + LANES] = (
                y_intra + eacum_x[:, c0:c0 + LANES] * inter[:, 2 * j * P_B:2 * j * P_B + LANES])
            yield

        yield from _in_turn([pair(j) for j in range(hpg // 2)])
        h_ref[b, :, g * gw:(g + 1) * gw] = dec_x[:, g * gw:(g + 1) * gw] * h_old[:, g * gw:(g + 1) * gw] + upd
        yield

    yield from _in_turn([group(g) for g in range(G_B)])


def _ssd_scan_kernel(xf_ref, dtf_ref, xb_ref, dtb_ref, arow_ref, trif_ref, trib_ref, ef_ref, eb_ref,
                     yf_ref, yb_ref, hf_s, hb_s):
    @pl.when(pl.program_id(0) == 0)
    def _():
        hf_s[...] = jnp.zeros_like(hf_s)
        hb_s[...] = jnp.zeros_like(hb_s)

    chains = []
    for b in range(BATCH):
        chains.append(_ssd_one_direction(xf_ref[b], dtf_ref[b], arow_ref[...], trif_ref[...], ef_ref[...],
                                         hf_s, yf_ref, b, 0, False))
        chains.append(_ssd_one_direction(xb_ref[b], dtb_ref[b], arow_ref[...], trib_ref[...], eb_ref[...],
                                         hb_s, yb_ref, b, 1, True))
    _round_robin(chains)


def _ssd_scan(xact, dtp, a_log):
    q = SSD_CHUNK
    steps = PB // q
    a_neg = -jnp.exp(a_log)
    arow = jnp.concatenate([a_neg.reshape(1, 2 * H_B), jnp.zeros((1, LANES - 2 * H_B), F32)], axis=1)
    r = jnp.arange(q)
    tri_f = (r[None, :] <= r[:, None]).astype(BF16)
    tri_b = (r[None, :] >= r[:, None]).astype(BF16)
    col_head = jnp.arange(D_SSD) // P_B
    lane = jnp.arange(LANES)
    exp_f = (lane[:, None] == col_head[None, :]).astype(BF16)
    exp_b = (lane[:, None] == col_head[None, :] + H_B).astype(BF16)
    fwd = lambda s: (0, _scan_chunk(s, False, q), 0)
    bwd = lambda s: (0, _scan_chunk(s, True, q), 0)
    const = lambda s: (0, 0)
    x3 = xact.reshape(BATCH, PB, CONV_CH)
    d3 = dtp.reshape(BATCH, PB, LANES)
    yf, yb = pl.pallas_call(
        _ssd_scan_kernel,
        grid=(steps,),
        in_specs=[pl.BlockSpec((BATCH, q, CONV_CH), fwd), pl.BlockSpec((BATCH, q, LANES), fwd),
                  pl.BlockSpec((BATCH, q, CONV_CH), bwd), pl.BlockSpec((BATCH, q, LANES), bwd),
                  pl.BlockSpec((1, LANES), const),
                  pl.BlockSpec((q, q), const), pl.BlockSpec((q, q), const),
                  pl.BlockSpec((LANES, D_SSD), const), pl.BlockSpec((LANES, D_SSD), const)],
        out_specs=[pl.BlockSpec((BATCH, q, D_SSD), fwd), pl.BlockSpec((BATCH, q, D_SSD), bwd)],
        out_shape=[jax.ShapeDtypeStruct((BATCH, PB, D_SSD), F32), jax.ShapeDtypeStruct((BATCH, PB, D_SSD), F32)],
        scratch_shapes=[pltpu.VMEM((BATCH, N_B, D_SSD), F32), pltpu.VMEM((BATCH, N_B, D_SSD), F32)],
        compiler_params=_cparams("arbitrary"),
        name="ssd_scan",
    )(x3, d3, x3, d3, arow, tri_f, tri_b, exp_f, exp_b)
    return yf.reshape(T_ALL, D_SSD), yb.reshape(T_ALL, D_SSD)


def _mlstm_one_direction(q, k, v, gi, gf, bi, bf, tri, st_ref, m_ref, h_ref, b, d, rev):
    n = MLSTM_CHUNK
    li = gi + bi
    lf = jax.nn.log_sigmoid(gf + bf)
    yield
    bc = _dot3_left(tri, lf)
    yield
    b_t = bc.T
    li_t = li.T
    yield
    edge = 0 if rev else n - 1
    gtot = bc[edge:edge + 1, :]
    m_old = m_ref[b, 0:1, :]
    w_log = gtot - bc + li
    m_new = jnp.maximum(gtot + m_old, jnp.max(w_log, axis=0, keepdims=True))
    wj = jnp.exp(w_log - m_new)
    dec = jnp.exp(gtot + m_old - m_new)
    inter_log = bc + m_old
    m_ref[b, 0:1, :] = m_new
    yield
    ri = lax.broadcasted_iota(jnp.int32, (n, n), 0)
    ci = lax.broadcasted_iota(jnp.int32, (n, n), 1)
    mask = (ci >= ri) if rev else (ci <= ri)
    lo_half = lax.broadcasted_iota(jnp.int32, (n, LANES), 1) < DQK_C
    row_lo = lax.broadcasted_iota(jnp.int32, (2 * DQK_C, 2 * DV_C), 0) < DQK_C
    ones = jnp.ones((n, DV_C), F32)
    st_old = [st_ref[b, j] for j in range(H_C // 2)]
    upds = {}

    def head(h):
        j, hh = divmod(h, 2)
        gl = GATE_LANE0 + H_C * d + h
        qp = q[:, j * LANES:(j + 1) * LANES] * DQK_C ** -0.5
        kp = k[:, j * LANES:(j + 1) * LANES].astype(BF16)
        qm = jnp.where(lo_half if hh == 0 else jnp.logical_not(lo_half), qp, 0.0).astype(BF16)
        qk = lax.dot_general(qm, kp, _NT, preferred_element_type=F32)
        qs = jnp.dot(qm, st_old[j].astype(BF16), preferred_element_type=F32)
        yield
        dmat = jnp.where(mask, bc[:, gl:gl + 1] - b_t[gl:gl + 1, :] + li_t[gl:gl + 1, :], NEG_BIG)
        il = inter_log[:, gl:gl + 1]
        m_row = jnp.maximum(il, jnp.max(dmat, axis=1, keepdims=True))
        yield
        s = qk * jnp.exp(dmat - m_row)
        w_inter = jnp.exp(il - m_row)
        vh = v[:, h * DV_C:(h + 1) * DV_C]
        yield
        num = jnp.dot(s.astype(BF16), vh.astype(BF16), preferred_element_type=F32) + w_inter * qs[:, :DV_C]
        den = jnp.sum(s, axis=1, keepdims=True) + w_inter * qs[:, DV_C:]
        yield
        h_ref[b, :, h * DV_C:(h + 1) * DV_C] = num / jnp.maximum(jnp.abs(den), jnp.exp(-m_row))
        rhs = (wj[:, gl:gl + 1] * jnp.concatenate([vh, ones], axis=1)).astype(BF16)
        upds[h] = lax.dot_general(kp, rhs, _TN, preferred_element_type=F32)
        yield

    yield from _in_turn([head(h) for h in range(H_C)])
    for j in range(H_C // 2):
        ga = GATE_LANE0 + H_C * d + 2 * j
        decv = jnp.where(row_lo, dec[:, ga:ga + 1], dec[:, ga + 1:ga + 2])
        st_ref[b, j] = decv * st_old[j] + jnp.where(row_lo, upds[2 * j], upds[2 * j + 1])
        yield


def _mlstm_scan_kernel(qf, kf, vf, gif, gff, qb, kb, vb, gib, gfb, bi_ref, bf_ref, trif_ref, trib_ref,
                       hf_ref, hb_ref, stf_s, stb_s, mf_s, mb_s):
    @pl.when(pl.program_id(0) == 0)
    def _():
        stf_s[...] = jnp.zeros_like(stf_s)
        stb_s[...] = jnp.zeros_like(stb_s)
        mf_s[...] = jnp.full_like(mf_s, NEG_STATE)
        mb_s[...] = jnp.full_like(mb_s, NEG_STATE)

    chains = []
    for b in range(BATCH):
        chains.append(_mlstm_one_direction(qf[b], kf[b], vf[b], gif[b], gff[b], bi_ref[...], bf_ref[...],
                                           trif_ref[...], stf_s, mf_s, hf_ref, b, 0, False))
        chains.append(_mlstm_one_direction(qb[b], kb[b], vb[b], gib[b], gfb[b], bi_ref[...], bf_ref[...],
                                           trib_ref[...], stb_s, mb_s, hb_ref, b, 1, True))
    _round_robin(chains)


def _mlstm_scan(z, gate_b):
    n = MLSTM_CHUNK
    steps = PB // n
    qkw = H_C * DQK_C
    pad = lambda t: jnp.concatenate([jnp.zeros((1, GATE_LANE0), F32), t.reshape(1, 2 * H_C),
                                     jnp.zeros((1, LANES - GATE_LANE0 - 2 * H_C), F32)], axis=1)
    bi = pad(gate_b[:, 0, :])
    bf = pad(gate_b[:, 1, :])
    r = jnp.arange(n)
    tri_f = (r[None, :] <= r[:, None]).astype(BF16)
    tri_b = (r[None, :] >= r[:, None]).astype(BF16)
    z3 = z.reshape(BATCH, PB, Z_COLS)

    def specs(rev):
        ch = lambda s: _scan_chunk(s, rev, n)
        return [pl.BlockSpec((BATCH, n, qkw), lambda s: (0, ch(s), Z_MQ // qkw)),
                pl.BlockSpec((BATCH, n, qkw), lambda s: (0, ch(s), Z_MK // qkw)),
                pl.BlockSpec((BATCH, n, D_MLSTM), lambda s: (0, ch(s), Z_MV // D_MLSTM)),
                pl.BlockSpec((BATCH, n, LANES), lambda s: (0, ch(s), Z_MG // LANES)),
                pl.BlockSpec((BATCH, n, LANES), lambda s: (0, ch(s), Z_DT // LANES))]

    const = lambda s: (0, 0)
    n_pairs = H_C // 2
    hf, hb = pl.pallas_call(
        _mlstm_scan_kernel,
        grid=(steps,),
        in_specs=specs(False) + specs(True) + [
            pl.BlockSpec((1, LANES), const), pl.BlockSpec((1, LANES), const),
            pl.BlockSpec((n, n), const), pl.BlockSpec((n, n), const)],
        out_specs=[pl.BlockSpec((BATCH, n, D_MLSTM), lambda s: (0, _scan_chunk(s, False, n), 0)),
                   pl.BlockSpec((BATCH, n, D_MLSTM), lambda s: (0, _scan_chunk(s, True, n), 0))],
        out_shape=[jax.ShapeDtypeStruct((BATCH, PB, D_MLSTM), F32), jax.ShapeDtypeStruct((BATCH, PB, D_MLSTM), F32)],
        scratch_shapes=[pltpu.VMEM((BATCH, n_pairs, 2 * DQK_C, 2 * DV_C), F32),
                        pltpu.VMEM((BATCH, n_pairs, 2 * DQK_C, 2 * DV_C), F32),
                        pltpu.VMEM((BATCH, 8, LANES), F32), pltpu.VMEM((BATCH, 8, LANES), F32)],
        compiler_params=_cparams("arbitrary"),
        name="mlstm_scan",
    )(z3, z3, z3, z3, z3, z3, z3, z3, z3, z3, bi, bf, tri_f, tri_b)
    return hf.reshape(T_ALL, D_MLSTM), hb.reshape(T_ALL, D_MLSTM)


S5_Q = 256
S5_SEG = S5_Q // 8
S5_LANES = G_S5 * P_S5
S5_SLAB = 512
S5_NSLAB = S5_LANES // S5_SLAB


def _s5_dir_kernel(u_ref, perm_ref, wbr_ref, wbi_ref, atab_ref, apr_ref, api_ref, wcr_ref, wci_ref, y_ref,
                   ur_s, ui_s, xr_s, xi_s, st_s, car_s, *, reverse):
    @pl.when(pl.program_id(0) == 0)
    def _():
        st_s[...] = jnp.zeros_like(st_s)

    for b in range(BATCH):
        up = jnp.dot(perm_ref[...], u_ref[b].astype(BF16), preferred_element_type=F32).astype(BF16)
        for m in range(S5_NSLAB):
            um = up[:, m * LANES:(m + 1) * LANES]
            ur_s[b, :, m * S5_SLAB:(m + 1) * S5_SLAB] = jnp.dot(um, wbr_ref[m], preferred_element_type=F32)
            ui_s[b, :, m * S5_SLAB:(m + 1) * S5_SLAB] = jnp.dot(um, wbi_ref[m], preferred_element_type=F32)

    per = 4
    for grp in range(S5_LANES // (per * LANES)):
        cols = [grp * per * LANES + j * LANES for j in range(per)]
        a_r = [atab_ref[0:8, c0:c0 + LANES] for c0 in cols]
        a_i = [atab_ref[8:16, c0:c0 + LANES] for c0 in cols]

        def body(i, carry, cols=cols, a_r=a_r, a_i=a_i):
            t = (S5_SEG - 1 - i) if reverse else i
            r0 = pl.multiple_of(t * 8, 8)
            new = []
            for b in range(BATCH):
                for j, c0 in enumerate(cols):
                    xr, xi = carry[2 * (b * per + j)], carry[2 * (b * per + j) + 1]
                    nr = a_r[j] * xr - a_i[j] * xi + ur_s[b, pl.ds(r0, 8), c0:c0 + LANES]
                    ni = a_r[j] * xi + a_i[j] * xr + ui_s[b, pl.ds(r0, 8), c0:c0 + LANES]
                    ur_s[b, pl.ds(r0, 8), c0:c0 + LANES] = nr
                    ui_s[b, pl.ds(r0, 8), c0:c0 + LANES] = ni
                    new += [nr, ni]
            return tuple(new)

        lax.fori_loop(0, S5_SEG, body, tuple(jnp.zeros((8, LANES), F32) for _ in range(2 * per * BATCH)), unroll=2)

    as_r = atab_ref[16:17, :]
    as_i = atab_ref[17:18, :]
    end_row = 0 if reverse else 8 * (S5_SEG - 1)
    for b in range(BATCH):
        cr = st_s[b, 0:1, :]
        ci = st_s[b, 1:2, :]
        for k in (range(7, -1, -1) if reverse else range(8)):
            car_s[b, k:k + 1, :] = cr
            car_s[b, 8 + k:9 + k, :] = ci
            er = ur_s[b, end_row + k:end_row + k + 1, :]
            ei = ui_s[b, end_row + k:end_row + k + 1, :]
            cr, ci = er + as_r * cr - as_i * ci, ei + as_r * ci + as_i * cr
        st_s[b, 0:1, :] = cr
        st_s[b, 1:2, :] = ci

    for b in range(BATCH):
        for m in range(S5_NSLAB):
            cs = slice(m * S5_SLAB, (m + 1) * S5_SLAB)
            c_r = jnp.concatenate([car_s[b, 0:8, cs], car_s[b, 0:8, cs]], axis=0)
            c_i = jnp.concatenate([car_s[b, 8:16, cs], car_s[b, 8:16, cs]], axis=0)

            def fix(i, _, b=b, cs=cs, c_r=c_r, c_i=c_i):
                r0 = pl.multiple_of(i * 16, 16)
                p_r = apr_ref[pl.ds(r0, 16), cs]
                p_i = api_ref[pl.ds(r0, 16), cs]
                xr_s[b, pl.ds(r0, 16), cs] = (ur_s[b, pl.ds(r0, 16), cs] + p_r * c_r - p_i * c_i).astype(BF16)
                xi_s[b, pl.ds(r0, 16), cs] = (ui_s[b, pl.ds(r0, 16), cs] + p_r * c_i + p_i * c_r).astype(BF16)
                return 0

            lax.fori_loop(0, S5_Q // 16, fix, 0, unroll=2)

    for b in range(BATCH):
        for m in range(S5_NSLAB):
            cs = slice(m * S5_SLAB, (m + 1) * S5_SLAB)
            y_ref[b, :, m * LANES:(m + 1) * LANES] = (
                jnp.dot(xr_s[b, :, cs], wcr_ref[m], preferred_element_type=F32)
                - jnp.dot(xi_s[b, :, cs], wci_ref[m], preferred_element_type=F32))


def _s5_direction(z3, perm, wbr, wbi, atab, apr, api, wcr, wci, reverse):
    steps = PB // S5_Q
    chunk = lambda s: _scan_chunk(s, reverse, S5_Q)
    const2 = lambda s: (0, 0)
    const3 = lambda s: (0, 0, 0)
    return pl.pallas_call(
        functools.partial(_s5_dir_kernel, reverse=reverse),
        grid=(steps,),
        in_specs=[pl.BlockSpec((BATCH, S5_Q, D_S5), lambda s: (0, chunk(s), Z_D // D_S5)),
                  pl.BlockSpec((S5_Q, S5_Q), const2),
                  pl.BlockSpec(wbr.shape, const3),
                  pl.BlockSpec(wbi.shape, const3),
                  pl.BlockSpec(atab.shape, const2),
                  pl.BlockSpec(apr.shape, const2),
                  pl.BlockSpec(api.shape, const2),
                  pl.BlockSpec(wcr.shape, const3),
                  pl.BlockSpec(wci.shape, const3)],
        out_specs=pl.BlockSpec((BATCH, S5_Q, D_S5), lambda s: (0, chunk(s), 0)),
        out_shape=jax.ShapeDtypeStruct((BATCH, PB, D_S5), F32),
        scratch_shapes=[pltpu.VMEM((BATCH, S5_Q, S5_LANES), F32), pltpu.VMEM((BATCH, S5_Q, S5_LANES), F32),
                        pltpu.VMEM((BATCH, S5_Q, S5_LANES), BF16), pltpu.VMEM((BATCH, S5_Q, S5_LANES), BF16),
                        pltpu.VMEM((BATCH, 8, S5_LANES), F32), pltpu.VMEM((BATCH, 16, S5_LANES), F32)],
        compiler_params=_cparams("arbitrary"),
        name="s5_bwd" if reverse else "s5_fwd",
    )(z3, perm, wbr, wbi, atab, apr, api, wcr, wci).reshape(T_ALL, D_S5)


def _s5_tables(lam_re, lam_im, log_dt, b_re, b_im, c_re, c_im):
    dt = jnp.exp(log_dt)[..., None]
    mag = jnp.exp(lam_re * dt)
    ar = mag * jnp.cos(lam_im * dt)
    ai = mag * jnp.sin(lam_im * dt)
    den = lam_re * lam_re + lam_im * lam_im
    cr_ = ((ar - 1.0) * lam_re + ai * lam_im) / den
    ci_ = (ai * lam_re - (ar - 1.0) * lam_im) / den
    bbr = cr_[..., None] * b_re - ci_[..., None] * b_im
    bbi = cr_[..., None] * b_im + ci_[..., None] * b_re
    gps = S5_SLAB // P_S5
    eye = jnp.eye(gps, dtype=F32)

    def drive_w(bb):
        t = bb.reshape(S5_NSLAB, gps, P_S5, S5_GROUP)
        w = jnp.einsum('mgpc,gh->mgchp', t, eye)
        return w.reshape(S5_NSLAB, gps * S5_GROUP, gps * P_S5).astype(BF16)

    def read_w(cc):
        t = cc.reshape(S5_NSLAB, gps, S5_GROUP, P_S5)
        w = jnp.einsum('mgcp,gh->mgphc', t, eye)
        return w.reshape(S5_NSLAB, gps * P_S5, gps * S5_GROUP).astype(BF16)

    steps = jnp.arange(1, S5_SEG + 1, dtype=F32)[:, None]
    out = []
    for d in range(2):
        decay = (lam_re[d] * dt[d]).reshape(1, S5_LANES)
        angle = (lam_im[d] * dt[d]).reshape(1, S5_LANES)
        pmag = jnp.exp(steps * decay)
        pr = pmag * jnp.cos(steps * angle)
        pi = pmag * jnp.sin(steps * angle)
        if d == 1:
            apr, api = jnp.repeat(pr[::-1], 8, axis=0), jnp.repeat(pi[::-1], 8, axis=0)
        else:
            apr, api = jnp.repeat(pr, 8, axis=0), jnp.repeat(pi, 8, axis=0)
        atab = jnp.concatenate([jnp.broadcast_to(pr[0:1], (8, S5_LANES)), jnp.broadcast_to(pi[0:1], (8, S5_LANES)),
                                pr[S5_SEG - 1:], pi[S5_SEG - 1:], jnp.zeros((6, S5_LANES), F32)], axis=0)
        out.append((drive_w(bbr[d]), drive_w(bbi[d]), atab, apr, api))
    r = jnp.arange(S5_Q)
    src = (r % 8) * S5_SEG + r // 8
    perm = (src[:, None] == jnp.arange(S5_Q)[None, :]).astype(BF16)
    return out, read_w(c_re), read_w(c_im), perm


def _s5_scans(z, p):
    dirs, wcr, wci, perm = _s5_tables(p["s5_lam_re"], p["s5_lam_im"], p["s5_log_dt"], p["s5_b_re"], p["s5_b_im"],
                                      p["s5_c_re"], p["s5_c_im"])
    z3 = z.reshape(BATCH, PB, Z_COLS)
    yf = _s5_direction(z3, perm, *dirs[0], wcr, wci, reverse=False)
    yb = _s5_direction(z3, perm, *dirs[1], wcr, wci, reverse=True)
    return yf, yb, perm.T


def _merge_kernel(ya_ref, sf_ref, sb_ref, sx_ref, sg_ref, mf_ref, mb_ref, mo_ref, df_ref, db_ref, du_ref,
                  permt_ref, sd_ref, dd_ref, gw_ref, gb_ref, onw_ref, w0, w1, w2, w3, x_ref, mod_ref,
                  o_ref, wb_s, *, g_idx):
    @pl.when(pl.program_id(0) == 0)
    def _():
        for r, w in enumerate((w0, w1, w2, w3)):
            wb_s[r] = w[...].astype(BF16)

    onw = onw_ref[...]
    out = {}

    def attn():
        out[0] = jnp.dot(ya_ref[...], wb_s[0], preferred_element_type=F32)
        yield

    def ssd():
        gate = sg_ref[...]
        y = (sf_ref[...] + sb_ref[...] + sd_ref[...] * sx_ref[...]) * (gate * jax.nn.sigmoid(gate))
        yield
        y = (_rms(y) * onw[:, D_MLA:D_MLA + D_SSD]).astype(BF16)
        yield
        out[1] = jnp.dot(y, wb_s[1], preferred_element_type=F32)
        yield

    def mlstm():
        gate = jax.nn.sigmoid(mo_ref[...])
        c0 = D_MLA + D_SSD
        parts = []
        for h in range(H_C):
            cs = slice(h * DV_C, (h + 1) * DV_C)
            hn = _rms(mf_ref[:, cs] + mb_ref[:, cs])
            parts.append((hn * gate[:, cs] * onw[:, c0 + h * DV_C:c0 + (h + 1) * DV_C]).astype(BF16))
            yield
        out[2] = jnp.dot(jnp.concatenate(parts, axis=1), wb_s[2], preferred_element_type=F32)
        yield

    def s5():
        y = _dot3_left(permt_ref[...], df_ref[...] + db_ref[...])
        yield
        y = jax.nn.gelu(y + dd_ref[...] * du_ref[...])
        yield
        gate = jax.nn.sigmoid(jnp.dot(y.astype(BF16), gw_ref[...], preferred_element_type=F32) + gb_ref[...])
        yield
        y = (_rms(y * gate) * onw[:, D_MLA + D_SSD + D_MLSTM:]).astype(BF16)
        yield
        out[3] = jnp.dot(y, wb_s[3], preferred_element_type=F32)
        yield

    _round_robin([attn(), ssd(), mlstm(), s5()])
    g = mod_ref[0][g_idx:g_idx + 1, :]
    o_ref[...] = x_ref[...] + g * (out[0] + out[1] + out[2] + out[3])


def _merge_proj(ya, ssd, mlstm, s5, z, p, w_out, l, x, modtab, g_idx):
    sf, sb, xact = ssd
    mf, mb = mlstm
    df, db, permt = s5
    kq = D_MODEL // 4
    row = lambda i: (i, 0)
    const = lambda i: (0, 0)
    grp = lambda col: pl.BlockSpec((TILE, kq), lambda i: (i, col // kq))
    w_specs = [pl.BlockSpec((None, kq, D_MODEL), functools.partial(lambda i, r: (l, r, 0), r=r),
                            pipeline_mode=pl.Buffered(1)) for r in range(4)]
    return pl.pallas_call(
        functools.partial(_merge_kernel, g_idx=g_idx),
        grid=(T_ALL // TILE,),
        in_specs=[pl.BlockSpec((TILE, kq), row),
                  pl.BlockSpec((TILE, kq), row), pl.BlockSpec((TILE, kq), row), grp(0), grp(Z_SZ),
                  pl.BlockSpec((TILE, kq), row), pl.BlockSpec((TILE, kq), row), grp(Z_MO),
                  pl.BlockSpec((TILE, kq), row), pl.BlockSpec((TILE, kq), row), grp(Z_D),
                  pl.BlockSpec((TILE, TILE), const),
                  pl.BlockSpec((1, kq), const), pl.BlockSpec((1, kq), const),
                  pl.BlockSpec((kq, kq), const), pl.BlockSpec((1, kq), const),
                  pl.BlockSpec((1, D_MODEL), const)] + w_specs + [
            pl.BlockSpec((TILE, D_MODEL), row),
            pl.BlockSpec((1, 6, D_MODEL), lambda i: (_tile_mod_row(i), 0, 0))],
        out_specs=pl.BlockSpec((TILE, D_MODEL), row),
        out_shape=jax.ShapeDtypeStruct((T_ALL, D_MODEL), F32),
        scratch_shapes=[pltpu.VMEM((4, kq, D_MODEL), BF16)],
        compiler_params=_cparams("arbitrary"),
        name="merge_proj",
    )(ya, sf, sb, xact, z, mf, mb, z, df, db, z, permt,
      jnp.repeat(p["ssd_d"], P_B).reshape(1, kq), p["s5_d"].reshape(1, kq),
      p["s5_glu_w"].astype(BF16), p["s5_glu_b"].reshape(1, kq), p["out_norm_w"].reshape(1, D_MODEL),
      w_out, w_out, w_out, w_out, x, modtab)


def _rope_tables():
    pos = np.arange(SEQ)
    row = (pos // GRID_W).astype(np.float32)
    col = (pos % GRID_W).astype(np.float32)
    inv_freq = (ROPE_BASE ** (-np.arange(ROPE_AXIS // 2, dtype=np.float32) * 2.0 / ROPE_AXIS)).astype(np.float32)
    ang_r = row[:, None] * inv_freq
    ang_c = col[:, None] * inv_freq
    zeros = np.zeros((SEQ, LANES - D_ROPE), np.float32)
    cos = np.concatenate([np.cos(ang_r), np.cos(ang_r), np.cos(ang_c), np.cos(ang_c), zeros], axis=1)
    sin = np.concatenate([np.sin(ang_r), np.sin(ang_r), np.sin(ang_c), np.sin(ang_c), zeros], axis=1)
    cos_c = np.concatenate([np.ones((TILE, D_ROPE), np.float32), np.zeros((TILE, LANES - D_ROPE), np.float32)], axis=1)
    sin_c = np.zeros((TILE, LANES), np.float32)
    table = np.concatenate([np.concatenate([cos, sin], axis=1), np.concatenate([cos_c, sin_c], axis=1)], axis=0)
    return jnp.asarray(table.astype(np.float32))


def _layout_mla(w_uq, w_ukv):
    k = w_uq.shape[0]
    qa, qb, wk, wv = [], [], [], []
    for h in range(H_A):
        base = h * (D_NOPE + D_ROPE)
        rope = w_uq[:, base + D_NOPE:base + D_NOPE + D_ROPE]
        qa += [w_uq[:, base:base + D_NOPE], rope, jnp.zeros((k, LANES - D_ROPE), w_uq.dtype)]
        qb += [_rot_cols(rope), jnp.zeros((k, LANES - D_ROPE), w_uq.dtype)]
        kb = h * (D_NOPE + D_V)
        wk.append(w_ukv[:, kb:kb + D_NOPE])
        wv.append(w_ukv[:, kb + D_NOPE:kb + D_NOPE + D_V])
    cat = lambda xs: jnp.concatenate(xs, axis=1).astype(BF16)
    return cat(qa), cat(qb), cat(wk), cat(wv)


def _layer(xall, hx, modtab, p, big, l, cs, norm_w, next_modtab, last):
    z = _mm(hx, _layout_w_in(big["w_in"], l), MM_ROWS, Z_COLS // 3)
    onw = p["out_norm_w"]
    wqa, wqb, wk, wv = _layout_mla(p["mla_w_uq"], p["mla_w_ukv"])
    q, k, v = _mla_prep(z, cs, p["mla_q_norm_w"], p["mla_kv_norm_w"], wqa, wqb, wk, wv)
    ya = _attention(q, k, v, onw[:D_MLA].reshape(1, D_MLA))
    xact, dtp = _ssd_prep(z, p["ssd_conv_w"], p["ssd_conv_b"], p["ssd_dt_bias"])
    ssd = _ssd_scan(xact, dtp, p["ssd_a_log"]) + (xact,)
    mlstm = _mlstm_scan(z, p["mlstm_gate_b"])
    s5 = _s5_scans(z, p)
    xall = _merge_proj(ya, ssd, mlstm, s5, z, p, big["w_out"], l, xall, modtab, 2)

    w_router = jnp.concatenate([p["moe_w_group"], p["moe_w_expert"],
                                jnp.zeros((D_MODEL, ROUTER_COLS - N_GROUPS - N_EXPERTS), F32)], axis=1)
    wr_hi = w_router.astype(BF16)
    wr_lo = (w_router - wr_hi.astype(F32)).astype(BF16)
    h2, route, route_t, cnt = _prenorm_router(xall, p["norm2_w"], modtab, jnp.stack([wr_hi, wr_lo]), 3, 4, last)
    return _moe(xall, h2, route, route_t, cnt, modtab, big["moe_w_gate"], big["moe_w_up"], big["moe_w_down"], l,
                norm_w, next_modtab, last)


def kernel(x, c, ctx, c_ctx, mod_w, mod_b, norm1_w, w_in, mla_q_norm_w, mla_kv_norm_w, mla_w_uq, mla_w_ukv,
           ssd_conv_w, ssd_conv_b, ssd_a_log, ssd_dt_bias, ssd_d, mlstm_gate_b, s5_lam_re, s5_lam_im,
           s5_log_dt, s5_b_re, s5_b_im, s5_c_re, s5_c_im, s5_d, s5_glu_w, s5_glu_b, out_norm_w, w_out,
           norm2_w, moe_w_group, moe_w_expert, moe_w_gate, moe_w_up, moe_w_down, final_norm_w):
    stacked = {"norm1_w": norm1_w, "mla_q_norm_w": mla_q_norm_w, "mla_kv_norm_w": mla_kv_norm_w,
               "mla_w_uq": mla_w_uq, "mla_w_ukv": mla_w_ukv, "ssd_conv_w": ssd_conv_w, "ssd_conv_b": ssd_conv_b,
               "ssd_a_log": ssd_a_log, "ssd_dt_bias": ssd_dt_bias, "ssd_d": ssd_d, "mlstm_gate_b": mlstm_gate_b,
               "s5_lam_re": s5_lam_re, "s5_lam_im": s5_lam_im, "s5_log_dt": s5_log_dt, "s5_b_re": s5_b_re,
               "s5_b_im": s5_b_im, "s5_c_re": s5_c_re, "s5_c_im": s5_c_im, "s5_d": s5_d, "s5_glu_w": s5_glu_w,
               "s5_glu_b": s5_glu_b, "out_norm_w": out_norm_w, "norm2_w": norm2_w,
               "moe_w_group": moe_w_group, "moe_w_expert": moe_w_expert}
    big = {"w_in": w_in, "w_out": w_out, "moe_w_gate": moe_w_gate, "moe_w_up": moe_w_up, "moe_w_down": moe_w_down}
    cs = _rope_tables()
    cc = jnp.concatenate([c, c_ctx[None, :], jnp.zeros((8 - BATCH - 1, D_MODEL), F32)], axis=0)
    modtabs = [_modulation(cc, mod_w, mod_b, l)[:BATCH + 1].reshape(BATCH + 1, 6, D_MODEL) for l in range(DEPTH)]
    xall, hx = _prenorm(x, ctx, norm1_w[0], modtabs[0], 0, 1)
    for l in range(DEPTH):
        p = {name: val[l] for name, val in stacked.items()}
        if l == DEPTH - 1:
            out = _layer(xall, hx, modtabs[l], p, big, l, cs, final_norm_w, modtabs[l], True)
        else:
            xall, hx = _layer(xall, hx, modtabs[l], p, big, l, cs, norm1_w[l + 1], modtabs[l + 1], False)
    return out.reshape(BATCH, SEQ, D_MODEL)
```

```python
import functools
import math

import jax
import jax.numpy as jnp
import numpy as np
from jax import lax
from jax.experimental import pallas as pl
from jax.experimental.pallas import tpu as pltpu

F32 = jnp.float32
BF16 = jnp.bfloat16

D_MODEL = 2048
BATCH = 2
SEQ = 4096
DEPTH = 2
GRID_W = 64
CTX_LEN = 256
EPS = 1e-6
NEG_STATE = -1e30
NEG_BIG = -1e30

H_A = 4
D_NOPE = 128
D_ROPE = 64
D_V = 128
Q_RANK = 384
KV_RANK = 128
ROPE_AXIS = D_ROPE // 2
ROPE_BASE = 10000.0
D_MLA = H_A * D_V
D_SSD = 512
P_B = 64
H_B = D_SSD // P_B
G_B = 2
N_B = 128
SSD_CONV = 5
SSD_CHUNK = 128
CONV_CH = D_SSD + 2 * G_B * N_B
D_MLSTM = 512
H_C = 4
DV_C = D_MLSTM // H_C
DQK_C = DV_C // 2
MLSTM_CHUNK = 128
D_S5 = 512
S5_GROUP = 16
G_S5 = D_S5 // S5_GROUP
P_S5 = 64
A_COLS = Q_RANK + KV_RANK + D_ROPE
B_COLS = D_SSD + CONV_CH + 2 * H_B
C_COLS = 2 * H_C * DQK_C + 2 * D_MLSTM + 4 * H_C
N_GROUPS = 4
EXPERTS_PER_GROUP = 8
N_EXPERTS = N_GROUPS * EXPERTS_PER_GROUP
TOP_K = 2
D_EXPERT = 512

PB = CTX_LEN + SEQ
T_X = BATCH * SEQ
T_ALL = BATCH * PB

LANES = 128
VMEM_LIMIT_BYTES = 56 * 1024 * 1024

TILE = 256
TPB = PB // TILE
XT = SEQ // TILE
MM_ROWS = 512

Z_CQ = 0
Z_CKV = 384
Z_KR = 512
Z_KRR = 640
Z_MQ = 768
Z_XBC = 1024
Z_SZ = 2048
Z_MV = 2560
Z_MO = 3072
Z_D = 3584
Z_MK = 4096
Z_DT = 4352
Z_MG = 4480
Z_COLS = 4608
GATE_LANE0 = 2 * H_B

MOE_ROWS = 256
ROUTER_COLS = 128

_NT = (((1,), (1,)), ((), ()))
_TN = (((0,), (0,)), ((), ()))


def _cparams(*sem):
    return pltpu.CompilerParams(dimension_semantics=sem, vmem_limit_bytes=VMEM_LIMIT_BYTES)


def _tile_mod_row(i):
    return jnp.where(i % TPB == 0, BATCH, i // TPB)


def _token_tile(i, latent_only):
    return (i // XT) * TPB + 1 + i % XT if latent_only else i


def _rms(x, w=None):
    y = x * lax.rsqrt(jnp.mean(x * x, axis=-1, keepdims=True) + EPS)
    return y if w is None else y * w


def _in_turn(chains):
    live = list(chains)
    while live:
        nxt = []
        for chain in live:
            try:
                next(chain)
                nxt.append(chain)
            except StopIteration:
                pass
        live = nxt
        yield


def _round_robin(chains):
    for _ in _in_turn(chains):
        pass


def _split3(x):
    hi = x.astype(BF16)
    r1 = x - hi.astype(F32)
    mid = r1.astype(BF16)
    lo = (r1 - mid.astype(F32)).astype(BF16)
    return hi, mid, lo


def _dot3_left(sel, x):
    hi, mid, lo = _split3(x)
    return (jnp.dot(sel, hi, preferred_element_type=F32) + jnp.dot(sel, mid, preferred_element_type=F32)
            + jnp.dot(sel, lo, preferred_element_type=F32))


def _dot3_right(x, sel):
    hi, mid, lo = _split3(x)
    return (jnp.dot(hi, sel, preferred_element_type=F32) + jnp.dot(mid, sel, preferred_element_type=F32)
            + jnp.dot(lo, sel, preferred_element_type=F32))


def _mod_kernel(a_ref, w_ref, b_ref, o_ref):
    a = a_ref[...]
    a = a * jax.nn.sigmoid(a)
    o_ref[...] = jnp.dot(a.astype(BF16), w_ref[...].astype(BF16), preferred_element_type=F32) + b_ref[...]


def _modulation(cc, mod_w, mod_b, l):
    n = mod_w.shape[2]
    tn = 1024
    return pl.pallas_call(
        _mod_kernel,
        grid=(n // tn,),
        in_specs=[pl.BlockSpec((8, D_MODEL), lambda j: (0, 0)),
                  pl.BlockSpec((None, D_MODEL, tn), lambda j: (l, 0, j)),
                  pl.BlockSpec((None, 1, tn), lambda j: (l, 0, j))],
        out_specs=pl.BlockSpec((8, tn), lambda j: (0, j)),
        out_shape=jax.ShapeDtypeStruct((8, n), F32),
        compiler_params=_cparams("arbitrary"),
        name="modulation",
    )(cc, mod_w, mod_b.reshape(DEPTH, 1, n))


def _prenorm_kernel(x_ref, c_ref, w_ref, mod_ref, xall_ref, o_ref, *, sh_idx, sc_idx):
    v = jnp.where(pl.program_id(0) % TPB == 0, c_ref[...], x_ref[...])
    xall_ref[...] = v
    y = _rms(v, w_ref[...])
    m = mod_ref[0]
    y = y * (1.0 + m[sc_idx:sc_idx + 1, :]) + m[sh_idx:sh_idx + 1, :]
    o_ref[...] = y.astype(o_ref.dtype)


def _prenorm(x, ctx, w, modtab, sh_idx, sc_idx):
    row = lambda i: (i, 0)
    return pl.pallas_call(
        functools.partial(_prenorm_kernel, sh_idx=sh_idx, sc_idx=sc_idx),
        grid=(T_ALL // TILE,),
        in_specs=[pl.BlockSpec((TILE, D_MODEL), lambda i: ((i // TPB) * XT + jnp.maximum(i % TPB - 1, 0), 0)),
                  pl.BlockSpec((TILE, D_MODEL), lambda i: (i // TPB, 0)),
                  pl.BlockSpec((1, D_MODEL), lambda i: (0, 0)),
                  pl.BlockSpec((1, 6, D_MODEL), lambda i: (_tile_mod_row(i), 0, 0))],
        out_specs=[pl.BlockSpec((TILE, D_MODEL), row), pl.BlockSpec((TILE, D_MODEL), row)],
        out_shape=[jax.ShapeDtypeStruct((T_ALL, D_MODEL), F32), jax.ShapeDtypeStruct((T_ALL, D_MODEL), BF16)],
        compiler_params=_cparams("arbitrary"),
        name="prenorm",
    )(x.reshape(T_X, D_MODEL), ctx.reshape(BATCH * CTX_LEN, D_MODEL), w.reshape(1, D_MODEL), modtab)


def _mm_kernel(a_ref, w_ref, o_ref):
    o_ref[...] = jnp.dot(a_ref[...], w_ref[...], preferred_element_type=F32).astype(o_ref.dtype)


def _mm(a, w, tm, tn, out_dtype=F32):
    m, k = a.shape
    n = w.shape[1]
    return pl.pallas_call(
        _mm_kernel,
        grid=(n // tn, m // tm),
        in_specs=[pl.BlockSpec((tm, k), lambda j, i: (i, 0)),
                  pl.BlockSpec((k, tn), lambda j, i: (0, j))],
        out_specs=pl.BlockSpec((tm, tn), lambda j, i: (i, j)),
        out_shape=jax.ShapeDtypeStruct((m, n), out_dtype),
        compiler_params=_cparams("arbitrary", "arbitrary"),
        name="in_proj",
    )(a, w)


W_IN_B0 = A_COLS
W_IN_C0 = A_COLS + B_COLS
W_IN_D0 = A_COLS + B_COLS + C_COLS
W_IN_CM = W_IN_C0 + 2 * H_C * DQK_C
W_IN_GB = W_IN_CM + 2 * D_MLSTM
W_IN_WIDE = ((Z_CQ, 0, Q_RANK + KV_RANK),
             (Z_MQ, W_IN_C0, H_C * DQK_C),
             (Z_XBC, W_IN_B0 + D_SSD, CONV_CH),
             (Z_SZ, W_IN_B0, D_SSD),
             (Z_MV, W_IN_CM, 2 * D_MLSTM),
             (Z_D, W_IN_D0, D_S5),
             (Z_MK, W_IN_C0 + H_C * DQK_C, H_C * DQK_C))
W_IN_ROWS = 256


def _rot_cols(w):
    q = ROPE_AXIS // 2
    return jnp.concatenate([-w[:, q:2 * q], w[:, 0:q], -w[:, 3 * q:4 * q], w[:, 2 * q:3 * q]], axis=1)


W_IN_KR0 = Q_RANK + KV_RANK
W_IN_DT0 = W_IN_B0 + D_SSD + CONV_CH
W_IN_WINDOWS = tuple(c // LANES * LANES for c in (W_IN_KR0, W_IN_DT0, W_IN_GB))


def _narrow_selector():
    sel = np.zeros((3 * LANES, 4 * LANES), np.float32)

    def put(window, src_col, dst_col, width, sign=1.0):
        base = window * LANES + src_col - W_IN_WINDOWS[window]
        for c in range(width):
            sel[base + c, dst_col + c] = sign

    q = ROPE_AXIS // 2
    put(0, W_IN_KR0, 0, D_ROPE)
    for dst, src, sign in ((0, q, -1.0), (q, 0, 1.0), (2 * q, 3 * q, -1.0), (3 * q, 2 * q, 1.0)):
        put(0, W_IN_KR0 + src, LANES + dst, q, sign)
    put(1, W_IN_DT0, 2 * LANES, 2 * H_B)
    for d in range(2):
        put(2, W_IN_GB + (2 * d + 1) * H_C, 2 * LANES + GATE_LANE0 + d * H_C, H_C)
        put(2, W_IN_GB + 2 * d * H_C, 3 * LANES + GATE_LANE0 + d * H_C, H_C)
    return jnp.asarray(sel, dtype=BF16)


def _w_in_layout_kernel(w_ref, sel_ref, o_ref):
    for dst, src, width in W_IN_WIDE:
        o_ref[:, dst:dst + width] = w_ref[:, src:src + width].astype(BF16)
    windows = jnp.concatenate([w_ref[:, c:c + LANES] for c in W_IN_WINDOWS], axis=1).astype(BF16)
    small = jnp.dot(windows, sel_ref[...], preferred_element_type=F32).astype(BF16)
    o_ref[:, Z_KR:Z_MQ] = small[:, 0:2 * LANES]
    o_ref[:, Z_DT:Z_COLS] = small[:, 2 * LANES:4 * LANES]


def _layout_w_in(w_in, l):
    _, k, n = w_in.shape
    sel = _narrow_selector()
    return pl.pallas_call(
        _w_in_layout_kernel,
        grid=(k // W_IN_ROWS,),
        in_specs=[pl.BlockSpec((None, W_IN_ROWS, n), lambda i: (l, i, 0)),
                  pl.BlockSpec(sel.shape, lambda i: (0, 0))],
        out_specs=pl.BlockSpec((W_IN_ROWS, Z_COLS), lambda i: (i, 0)),
        out_shape=jax.ShapeDtypeStruct((k, Z_COLS), BF16),
        compiler_params=_cparams("arbitrary"),
        name="w_in_layout",
    )(w_in, sel)


def _mla_prep_kernel(za_ref, cs_ref, qw_ref, kvw_ref, wqa_ref, wqb_ref, wk_ref, wv_ref, q_ref, k_ref, v_ref):
    za = za_ref[...]
    cos = cs_ref[:, :LANES]
    sin = cs_ref[:, LANES:]
    qn = _rms(za[:, Z_CQ:Z_CQ + Q_RANK], qw_ref[...]).astype(BF16)
    kvn = _rms(za[:, Z_CKV:Z_CKV + KV_RANK], kvw_ref[...]).astype(BF16)
    qa = jnp.dot(qn, wqa_ref[...], preferred_element_type=F32)
    qb = jnp.dot(qn, wqb_ref[...], preferred_element_type=F32)
    kn = jnp.dot(kvn, wk_ref[...], preferred_element_type=F32)
    v = jnp.dot(kvn, wv_ref[...], preferred_element_type=F32)
    kr = (za[:, Z_KR:Z_KR + LANES] * cos + za[:, Z_KRR:Z_KRR + LANES] * sin).astype(BF16)
    for h in range(H_A):
        c0 = h * 2 * LANES
        q_ref[:, c0:c0 + LANES] = qa[:, c0:c0 + LANES].astype(BF16)
        q_ref[:, c0 + LANES:c0 + 2 * LANES] = (
            qa[:, c0 + LANES:c0 + 2 * LANES] * cos + qb[:, h * LANES:(h + 1) * LANES] * sin).astype(BF16)
        k_ref[:, c0:c0 + LANES] = kn[:, h * LANES:(h + 1) * LANES].astype(BF16)
        k_ref[:, c0 + LANES:c0 + 2 * LANES] = kr
    v_ref[...] = v.astype(BF16)


ATT_W = H_A * 2 * LANES


def _mla_prep(z, cs, qw, kvw, wqa, wqb, wk, wv):
    const = lambda i: (0, 0)
    rope_blk = lambda i: (jnp.where(i % TPB == 0, XT, i % TPB - 1), 0)
    return pl.pallas_call(
        _mla_prep_kernel,
        grid=(T_ALL // TILE,),
        in_specs=[pl.BlockSpec((TILE, Z_MQ), lambda i: (i, 0)),
                  pl.BlockSpec((TILE, 2 * LANES), rope_blk),
                  pl.BlockSpec((1, Q_RANK), const),
                  pl.BlockSpec((1, KV_RANK), const),
                  pl.BlockSpec(wqa.shape, const),
                  pl.BlockSpec(wqb.shape, const),
                  pl.BlockSpec(wk.shape, const),
                  pl.BlockSpec(wv.shape, const)],
        out_specs=[pl.BlockSpec((TILE, ATT_W), lambda i: (i, 0)),
                   pl.BlockSpec((TILE, ATT_W), lambda i: (i, 0)),
                   pl.BlockSpec((TILE, D_MLA), lambda i: (i, 0))],
        out_shape=[jax.ShapeDtypeStruct((T_ALL, ATT_W), BF16),
                   jax.ShapeDtypeStruct((T_ALL, ATT_W), BF16),
                   jax.ShapeDtypeStruct((T_ALL, D_MLA), BF16)],
        compiler_params=_cparams("arbitrary"),
        name="mla_prep",
    )(z, cs, qw.reshape(1, Q_RANK), kvw.reshape(1, KV_RANK), wqa, wqb, wk, wv)


def _attn_tile(q_ref, k_ref, v_ref, w_ref, o_ref, acc_ref, n_keys):
    scale2 = (D_NOPE + D_ROPE) ** -0.5 * math.log2(math.e)
    for h in range(H_A):
        q = q_ref[:, h * 2 * LANES:(h + 1) * 2 * LANES]
        s = lax.dot_general(q, k_ref[0:n_keys, h * 2 * LANES:(h + 1) * 2 * LANES], _NT, preferred_element_type=F32)
        m = jnp.max(s, axis=-1, keepdims=True)
        p = jnp.exp2((s - m) * scale2)
        l = jnp.sum(p, axis=-1, keepdims=True)
        o = jnp.dot(p.astype(BF16), v_ref[0:n_keys, h * D_V:(h + 1) * D_V], preferred_element_type=F32)
        acc_ref[:, h * D_V:(h + 1) * D_V] = o / l
    o_ref[...] = (_rms(acc_ref[...]) * w_ref[...]).astype(o_ref.dtype)


def _attn_kernel(q_ref, k_ref, v_ref, w_ref, o_ref, acc_ref):
    @pl.when(pl.program_id(1) == 0)
    def _():
        _attn_tile(q_ref, k_ref, v_ref, w_ref, o_ref, acc_ref, CTX_LEN)

    @pl.when(pl.program_id(1) != 0)
    def _():
        _attn_tile(q_ref, k_ref, v_ref, w_ref, o_ref, acc_ref, PB)


def _attention(q, k, v, onw):
    return pl.pallas_call(
        _attn_kernel,
        grid=(BATCH, TPB),
        in_specs=[pl.BlockSpec((TILE, ATT_W), lambda b, i: (b * TPB + i, 0)),
                  pl.BlockSpec((None, PB, ATT_W), lambda b, i: (b, 0, 0)),
                  pl.BlockSpec((None, PB, D_MLA), lambda b, i: (b, 0, 0)),
                  pl.BlockSpec((1, D_MLA), lambda b, i: (0, 0))],
        out_specs=pl.BlockSpec((TILE, D_MLA), lambda b, i: (b * TPB + i, 0)),
        out_shape=jax.ShapeDtypeStruct((T_ALL, D_MLA), BF16),
        scratch_shapes=[pltpu.VMEM((TILE, D_MLA), F32)],
        compiler_params=_cparams("arbitrary", "arbitrary"),
        name="attention",
    )(q, k.reshape(BATCH, PB, ATT_W), v.reshape(BATCH, PB, D_MLA), onw)


R_E0, R_E1, R_W0, R_W1, R_K0, R_K1 = range(6)
NO_LANE = 2 * LANES


def _prenorm_router_kernel(x_ref, w_ref, mod_ref, wr_ref, tri_ref, h_ref, route_ref, route_t_ref, cnt_ref, cnt_s,
                           *, sh_idx, sc_idx):
    @pl.when(pl.program_id(0) == 0)
    def _():
        cnt_s[...] = jnp.zeros_like(cnt_s)

    y = _rms(x_ref[...], w_ref[...])
    m = mod_ref[0]
    y = y * (1.0 + m[sc_idx:sc_idx + 1, :]) + m[sh_idx:sh_idx + 1, :]
    h_ref[...] = y
    y_hi = y.astype(BF16)
    y_lo = (y - y_hi.astype(F32)).astype(BF16)
    lg = (jnp.dot(y_hi, wr_ref[0], preferred_element_type=F32) + jnp.dot(y_lo, wr_ref[0], preferred_element_type=F32)
          + jnp.dot(y_hi, wr_ref[1], preferred_element_type=F32))
    lane = lax.broadcasted_iota(jnp.int32, lg.shape, 1)

    def first_max(v):
        top = jnp.max(v, axis=1, keepdims=True)
        return top, jnp.min(jnp.where(v == top, lane, NO_LANE), axis=1, keepdims=True)

    is_g = lane < N_GROUPS
    g_top, g_idx = first_max(jnp.where(is_g, lg, NEG_BIG))
    g_w = 1.0 / jnp.sum(jnp.where(is_g, jnp.exp(lg - g_top), 0.0), axis=1, keepdims=True)
    lo = N_GROUPS + EXPERTS_PER_GROUP * g_idx
    el = jnp.where(jnp.logical_and(lane >= lo, lane < lo + EXPERTS_PER_GROUP), lg, NEG_BIG)
    v1, i1 = first_max(el)
    v2, i2 = first_max(jnp.where(lane == i1, NEG_BIG, el))
    t = jnp.exp(v2 - v1)
    w0 = g_w / (1.0 + t)
    w1 = g_w * t / (1.0 + t)
    e0 = i1 - N_GROUPS
    e1 = i2 - N_GROUPS
    hit0 = lane == e0
    hit1 = lane == e1
    onehot = jnp.logical_or(hit0, hit1).astype(F32)
    before = cnt_s[0:1, :] + jnp.dot(tri_ref[...], onehot.astype(BF16), preferred_element_type=F32)
    k0 = jnp.sum(jnp.where(hit0, before, 0.0), axis=1, keepdims=True)
    k1 = jnp.sum(jnp.where(hit1, before, 0.0), axis=1, keepdims=True)
    cnt_s[0:1, :] = cnt_s[0:1, :] + jnp.sum(onehot, axis=0, keepdims=True)
    cnt_ref[...] = cnt_s[...]
    rec = jnp.zeros(lg.shape, F32)
    for ln, val in ((R_E0, e0.astype(F32)), (R_E1, e1.astype(F32)), (R_W0, w0), (R_W1, w1), (R_K0, k0), (R_K1, k1)):
        rec = jnp.where(lane == ln, val, rec)
    route_ref[...] = rec
    route_t_ref[...] = rec.T[0:8, :]


def _prenorm_router(x, w, modtab, w_router, sh_idx, sc_idx, latent_only):
    n_tok = T_X if latent_only else T_ALL
    r = jnp.arange(TILE)
    tri = (r[None, :] < r[:, None]).astype(BF16)
    src = lambda i: _token_tile(i, latent_only)
    return pl.pallas_call(
        functools.partial(_prenorm_router_kernel, sh_idx=sh_idx, sc_idx=sc_idx),
        grid=(n_tok // TILE,),
        in_specs=[pl.BlockSpec((TILE, D_MODEL), lambda i: (src(i), 0)),
                  pl.BlockSpec((1, D_MODEL), lambda i: (0, 0)),
                  pl.BlockSpec((1, 6, D_MODEL), lambda i: (_tile_mod_row(src(i)), 0, 0)),
                  pl.BlockSpec((2, D_MODEL, ROUTER_COLS), lambda i: (0, 0, 0)),
                  pl.BlockSpec((TILE, TILE), lambda i: (0, 0))],
        out_specs=[pl.BlockSpec((TILE, D_MODEL), lambda i: (i, 0)),
                   pl.BlockSpec((TILE, ROUTER_COLS), lambda i: (i, 0)),
                   pl.BlockSpec((8, TILE), lambda i: (0, i)),
                   pl.BlockSpec((8, ROUTER_COLS), lambda i: (0, 0))],
        out_shape=[jax.ShapeDtypeStruct((n_tok, D_MODEL), F32),
                   jax.ShapeDtypeStruct((n_tok, ROUTER_COLS), F32),
                   jax.ShapeDtypeStruct((8, n_tok), F32),
                   jax.ShapeDtypeStruct((8, ROUTER_COLS), F32)],
        scratch_shapes=[pltpu.VMEM((8, ROUTER_COLS), F32)],
        compiler_params=_cparams("arbitrary"),
        name="prenorm_router",
    )(x, w.reshape(1, D_MODEL), modtab, w_router, tri)


def _row_copy(src, src_row, dst, dst_row, sem):
    return pltpu.make_async_copy(src.at[pl.ds(src_row, 1)], dst.at[pl.ds(dst_row, 1)], sem)


def _dispatch_kernel(dest_ref, h_ref, xb_ref, sem):
    def issue(r, carry):
        for k in range(TOP_K):
            _row_copy(h_ref, r, xb_ref, dest_ref[0, k, r], sem).start(priority=k)
        return carry

    lax.fori_loop(0, TILE, issue, 0, unroll=8)

    def drain(r, carry):
        for k in range(TOP_K):
            _row_copy(h_ref, 0, xb_ref, 0, sem).wait()
        return carry

    lax.fori_loop(0, TILE, drain, 0, unroll=8)


def _dispatch(h, dest3):
    n_tok = h.shape[0]
    return pl.pallas_call(
        _dispatch_kernel,
        grid=(n_tok // TILE,),
        in_specs=[pl.BlockSpec((1, TOP_K, TILE), lambda i: (i, 0, 0), memory_space=pltpu.SMEM),
                  pl.BlockSpec((TILE, D_MODEL), lambda i: (i, 0))],
        out_specs=pl.BlockSpec(memory_space=pl.ANY),
        out_shape=jax.ShapeDtypeStruct((n_tok * TOP_K, D_MODEL), F32),
        scratch_shapes=[pltpu.SemaphoreType.DMA(())],
        compiler_params=_cparams("arbitrary"),
        name="moe_dispatch",
    )(dest3, h)


def _moe_kernel(tile_ref, exp_ref, lo_ref, hi_ref, flag_ref, next_ref, x_ref, wg_hbm, wu_hbm, wd_hbm, o_ref,
                stage_g, stage_u, stage_d, wg_s, wu_s, wd_s, sem, *, layer):
    i = pl.program_id(0)
    flags = flag_ref[i]

    def fetch(e):
        return (pltpu.make_async_copy(wg_hbm.at[layer, e], stage_g, sem.at[0]),
                pltpu.make_async_copy(wu_hbm.at[layer, e], stage_u, sem.at[1]),
                pltpu.make_async_copy(wd_hbm.at[layer, e], stage_d, sem.at[2]))

    @pl.when(i == 0)
    def _():
        for copy in fetch(exp_ref[0]):
            copy.start()

    @pl.when(flags % 2 == 1)
    def _():
        for copy in fetch(exp_ref[i]):
            copy.wait()
        wg_s[...] = stage_g[...].astype(BF16)
        wu_s[...] = stage_u[...].astype(BF16)
        wd_s[...] = stage_d[...].astype(BF16)

        @pl.when(next_ref[i] >= 0)
        def _():
            for copy in fetch(next_ref[i]):
                copy.start()

    @pl.when(flags >= 4)
    def _():
        x = x_ref[...].astype(BF16)
        g = jnp.dot(x, wg_s[...], preferred_element_type=F32)
        u = jnp.dot(x, wu_s[...], preferred_element_type=F32)
        row = lax.broadcasted_iota(jnp.int32, g.shape, 0)
        mine = jnp.logical_and(row >= lo_ref[i], row < hi_ref[i])
        h = jnp.where(mine, g * jax.nn.sigmoid(g) * u, 0.0).astype(BF16)
        res = jnp.dot(h, wd_s[...], preferred_element_type=F32)

        @pl.when((flags // 2) % 2 == 1)
        def _():
            o_ref[...] = res

        @pl.when((flags // 2) % 2 == 0)
        def _():
            o_ref[...] += res


def _moe_experts(xb, meta, wg, wu, wd, l):
    n_items = meta[0].shape[0]
    xmap = lambda i, ti, ex, lo, hi, fl, nx: (ti[i], 0)
    hbm = pl.BlockSpec(memory_space=pl.ANY)
    grid_spec = pltpu.PrefetchScalarGridSpec(
        num_scalar_prefetch=6,
        grid=(n_items,),
        in_specs=[pl.BlockSpec((MOE_ROWS, D_MODEL), xmap), hbm, hbm, hbm],
        out_specs=pl.BlockSpec((MOE_ROWS, D_MODEL), xmap),
        scratch_shapes=[pltpu.VMEM((D_MODEL, D_EXPERT), F32),
                        pltpu.VMEM((D_MODEL, D_EXPERT), F32),
                        pltpu.VMEM((D_EXPERT, D_MODEL), F32),
                        pltpu.VMEM((D_MODEL, D_EXPERT), BF16),
                        pltpu.VMEM((D_MODEL, D_EXPERT), BF16),
                        pltpu.VMEM((D_EXPERT, D_MODEL), BF16),
                        pltpu.SemaphoreType.DMA((3,))])
    return pl.pallas_call(
        functools.partial(_moe_kernel, layer=l),
        grid_spec=grid_spec,
        out_shape=jax.ShapeDtypeStruct(xb.shape, F32),
        compiler_params=_cparams("arbitrary"),
        name="moe_experts",
    )(*meta, xb, wg, wu, wd)


def _combine_kernel(dest_ref, next_ref, yb_ref, x_ref, route_ref, mod_ref, nw_ref, nmod_ref, *rest, final):
    if final:
        o_ref, buf, sem = rest
    else:
        o_ref, hx_ref, buf, sem = rest
    i = pl.program_id(0)
    slot = i % 2
    n_groups = TILE // COMBINE_ROWS

    def issue(d_ref, to_slot, j):
        for rr in range(COMBINE_ROWS):
            r = j * COMBINE_ROWS + rr
            for k in range(TOP_K):
                _row_copy(yb_ref, d_ref[0, k, r], buf.at[to_slot, k], r, sem.at[to_slot]).start(priority=k)

    @pl.when(i == 0)
    def _():
        def first(j, carry):
            issue(dest_ref, 0, j)
            return carry

        lax.fori_loop(0, n_groups, first, 0)

    def drain(r, carry):
        for k in range(TOP_K):
            _row_copy(yb_ref, 0, buf.at[slot, k], 0, sem.at[slot]).wait()
        return carry

    lax.fori_loop(0, TILE, drain, 0, unroll=8)
    g2 = mod_ref[0][5:6, :]
    nw = nw_ref[...]
    nm = nmod_ref[0]

    def combine(j):
        rows = pl.ds(pl.multiple_of(j * COMBINE_ROWS, COMBINE_ROWS), COMBINE_ROWS)
        rt = route_ref[rows, :]
        f = buf[slot, 0, rows, :] * rt[:, R_W0:R_W0 + 1] + buf[slot, 1, rows, :] * rt[:, R_W1:R_W1 + 1]
        y = x_ref[rows, :] + g2 * f
        if final:
            o_ref[rows, :] = _rms(y, nw)
        else:
            o_ref[rows, :] = y
            hx_ref[rows, :] = (_rms(y, nw) * (1.0 + nm[1:2, :]) + nm[0:1, :]).astype(hx_ref.dtype)

    @pl.when(i + 1 < pl.num_programs(0))
    def _():
        def body(j, carry):
            combine(j)
            issue(next_ref, 1 - slot, j)
            return carry

        lax.fori_loop(0, n_groups, body, 0)

    @pl.when(i + 1 >= pl.num_programs(0))
    def _():
        def body(j, carry):
            combine(j)
            return carry

        lax.fori_loop(0, n_groups, body, 0)


COMBINE_ROWS = 64


def _combine(yb, dest3, x, route, modtab, norm_w, next_modtab, latent_only):
    n_tok = route.shape[0]
    n_tiles = n_tok // TILE
    src = lambda i: _token_tile(i, latent_only)
    row = lambda i: (i, 0)
    mod_spec = pl.BlockSpec((1, 6, D_MODEL), lambda i: (_tile_mod_row(src(i)), 0, 0))
    out_specs = [pl.BlockSpec((TILE, D_MODEL), row)]
    out_shape = [jax.ShapeDtypeStruct((n_tok, D_MODEL), F32)]
    if not latent_only:
        out_specs.append(pl.BlockSpec((TILE, D_MODEL), row))
        out_shape.append(jax.ShapeDtypeStruct((n_tok, D_MODEL), BF16))
    return pl.pallas_call(
        functools.partial(_combine_kernel, final=latent_only),
        grid=(n_tiles,),
        in_specs=[pl.BlockSpec((1, TOP_K, TILE), lambda i: (i, 0, 0), memory_space=pltpu.SMEM),
                  pl.BlockSpec((1, TOP_K, TILE), lambda i: (jnp.minimum(i + 1, n_tiles - 1), 0, 0),
                               memory_space=pltpu.SMEM),
                  pl.BlockSpec(memory_space=pl.ANY),
                  pl.BlockSpec((TILE, D_MODEL), lambda i: (src(i), 0)),
                  pl.BlockSpec((TILE, ROUTER_COLS), row),
                  mod_spec,
                  pl.BlockSpec((1, D_MODEL), lambda i: (0, 0)),
                  mod_spec],
        out_specs=out_specs,
        out_shape=out_shape,
        scratch_shapes=[pltpu.VMEM((2, TOP_K, TILE, D_MODEL), F32), pltpu.SemaphoreType.DMA((2,))],
        compiler_params=_cparams("arbitrary"),
        name="moe_combine",
    )(dest3, dest3, yb, x, route, modtab, norm_w.reshape(1, D_MODEL), next_modtab)


def _moe_plan(route_t, cnt):
    t = route_t.shape[1]
    n_tiles = t * TOP_K // MOE_ROWS
    n_items = n_tiles + N_EXPERTS - 1
    experts = route_t[R_E0:R_E1 + 1].astype(jnp.int32)
    rank = route_t[R_K0:R_K1 + 1].astype(jnp.int32)
    counts = cnt[0, :N_EXPERTS].astype(jnp.int32)
    ends = jnp.cumsum(counts)
    starts = ends - counts
    e_axis = jnp.arange(N_EXPERTS, dtype=jnp.int32)[:, None, None]
    dest = jnp.sum(jnp.where(experts[None] == e_axis, starts[:, None, None], 0), axis=0) + rank
    t_first = starts // MOE_ROWS
    per_e = jnp.where(counts > 0, (ends - 1) // MOE_ROWS - t_first + 1, 0)
    item_end = jnp.cumsum(per_e)
    item_start = item_end - per_e
    total = item_end[-1]
    w = jnp.arange(n_items, dtype=jnp.int32)
    valid = w < total
    wc = jnp.minimum(w, total - 1)
    ex = jnp.clip(jnp.searchsorted(item_end, wc, side='right'), 0, N_EXPERTS - 1).astype(jnp.int32)
    tile = (t_first[ex] + wc - item_start[ex]).astype(jnp.int32)
    lo = jnp.where(valid, jnp.maximum(starts[ex], tile * MOE_ROWS) - tile * MOE_ROWS, 0).astype(jnp.int32)
    hi = jnp.where(valid, jnp.minimum(ends[ex], (tile + 1) * MOE_ROWS) - tile * MOE_ROWS, 0).astype(jnp.int32)
    one = jnp.ones((1,), bool)
    new_e = jnp.concatenate([one, ex[1:] != ex[:-1]])
    new_t = jnp.concatenate([one, tile[1:] != tile[:-1]])
    flags = (new_e.astype(jnp.int32) + 2 * new_t.astype(jnp.int32) + 4 * valid.astype(jnp.int32))
    e_ids = jnp.arange(N_EXPERTS, dtype=jnp.int32)
    later = jnp.where(jnp.logical_and(counts[None, :] > 0, e_ids[None, :] > e_ids[:, None]), e_ids[None, :], N_EXPERTS)
    next_e = jnp.min(later, axis=1)
    nxt = jnp.where(next_e[ex] < N_EXPERTS, next_e[ex], -1).astype(jnp.int32)
    dest3 = dest.reshape(TOP_K, t // TILE, TILE).transpose(1, 0, 2)
    return dest3, (tile, ex, lo, hi, flags, nxt)


def _moe(x, h, route, route_t, cnt, modtab, wg, wu, wd, l, norm_w, next_modtab, latent_only):
    dest3, meta = _moe_plan(route_t, cnt)
    xb = _dispatch(h, dest3)
    yb = _moe_experts(xb, meta, wg, wu, wd, l)
    res = _combine(yb, dest3, x, route, modtab, norm_w, next_modtab, latent_only)
    return res[0] if latent_only else (res[0], res[1])


def _scan_chunk(s, rev, chunk):
    n_c = CTX_LEN // chunk
    n_all = PB // chunk
    if not rev:
        return s
    return jnp.where(s < n_c, n_c - 1 - s, n_all + n_c - 1 - s)


HALO = 8


def _ssd_prep_kernel(cur_ref, prev_ref, next_ref, dt_ref, cw_ref, cb_ref, dtb_ref, xo_ref, dto_ref, ext_s):
    j = pl.program_id(0) % TPB
    first = jnp.logical_or(j == 0, j == 1)
    last = jnp.logical_or(j == 0, j == TPB - 1)
    ext_s[0:HALO, :] = jnp.where(first, 0.0, prev_ref[...])
    ext_s[HALO:HALO + TILE, :] = cur_ref[...]
    ext_s[HALO + TILE:, :] = jnp.where(last, 0.0, next_ref[...])
    half = (SSD_CONV - 1) // 2
    acc = cb_ref[...] + cw_ref[0:1, :] * ext_s[HALO - half:HALO - half + TILE, :]
    for k in range(1, SSD_CONV):
        acc = acc + cw_ref[k:k + 1, :] * ext_s[HALO - half + k:HALO - half + k + TILE, :]
    xo_ref[...] = acc * jax.nn.sigmoid(acc)
    lane = lax.broadcasted_iota(jnp.int32, (TILE, LANES), 1)
    dto_ref[...] = jnp.where(lane < 2 * H_B, jax.nn.softplus(dt_ref[...] + dtb_ref[...]), 0.0)


def _ssd_prep(z, conv_w, conv_b, dt_bias):
    n_tiles = T_ALL // TILE
    per = TILE // HALO
    cwp = jnp.concatenate([conv_w, jnp.zeros((8 - SSD_CONV, CONV_CH), F32)], axis=0)
    dtb = jnp.concatenate([dt_bias.reshape(1, 2 * H_B), jnp.zeros((1, LANES - 2 * H_B), F32)], axis=1)
    xc = Z_XBC // CONV_CH
    return pl.pallas_call(
        _ssd_prep_kernel,
        grid=(n_tiles,),
        in_specs=[pl.BlockSpec((TILE, CONV_CH), lambda i: (i, xc)),
                  pl.BlockSpec((HALO, CONV_CH), lambda i: (jnp.maximum(i * per - 1, 0), xc)),
                  pl.BlockSpec((HALO, CONV_CH), lambda i: (jnp.minimum((i + 1) * per, T_ALL // HALO - 1), xc)),
                  pl.BlockSpec((TILE, LANES), lambda i: (i, Z_DT // LANES)),
                  pl.BlockSpec((8, CONV_CH), lambda i: (0, 0)),
                  pl.BlockSpec((1, CONV_CH), lambda i: (0, 0)),
                  pl.BlockSpec((1, LANES), lambda i: (0, 0))],
        out_specs=[pl.BlockSpec((TILE, CONV_CH), lambda i: (i, 0)),
                   pl.BlockSpec((TILE, LANES), lambda i: (i, 0))],
        out_shape=[jax.ShapeDtypeStruct((T_ALL, CONV_CH), F32),
                   jax.ShapeDtypeStruct((T_ALL, LANES), F32)],
        scratch_shapes=[pltpu.VMEM((TILE + 2 * HALO, CONV_CH), F32)],
        compiler_params=_cparams("arbitrary"),
        name="ssd_prep",
    )(z, z, z, z, cwp, conv_b.reshape(1, CONV_CH), dtb)


def _ssd_one_direction(xbc, dtp, arow, tri, expand, h_ref, y_ref, b, d, rev):
    q = SSD_CHUNK
    a = dtp * arow
    acum = _dot3_left(tri, a)
    yield
    acum_t = acum.T
    dt_t = dtp.T
    yield
    edge = 0 if rev else q - 1
    atot = acum[edge:edge + 1, :]
    pieces = jnp.concatenate([jnp.exp(atot - acum) * dtp, jnp.exp(acum),
                              jnp.broadcast_to(jnp.exp(atot), (8, LANES))], axis=0)
    ex = _dot3_right(pieces, expand)
    yield
    wend_x = ex[0:q]
    eacum_x = ex[q:2 * q]
    dec_x = ex[2 * q:2 * q + 1]
    xs = xbc[:, 0:D_SSD]
    xw = (xs * wend_x).astype(BF16)
    xs_b = xs.astype(BF16)
    h_old = h_ref[b]
    h_b = h_old.astype(BF16)
    ri = lax.broadcasted_iota(jnp.int32, (q, q), 0)
    ci = lax.broadcasted_iota(jnp.int32, (q, q), 1)
    mask = (ci >= ri) if rev else (ci <= ri)
    lo_half = lax.broadcasted_iota(jnp.int32, (q, LANES), 1) < P_B
    hpg = H_B // G_B
    gw = hpg * P_B
    yield

    def group(g):
        bg = xbc[:, D_SSD + g * N_B:D_SSD + (g + 1) * N_B].astype(BF16)
        cg = xbc[:, D_SSD + G_B * N_B + g * N_B:D_SSD + G_B * N_B + (g + 1) * N_B].astype(BF16)
        cb = lax.dot_general(cg, bg, _NT, preferred_element_type=F32)
        inter = jnp.dot(cg, h_b[:, g * gw:(g + 1) * gw], preferred_element_type=F32)
        upd = lax.dot_general(bg, xw[:, g * gw:(g + 1) * gw], _TN, preferred_element_type=F32)
        yield

        def pair(j):
            ms = []
            for hh in range(2):
                hc = H_B * d + hpg * g + 2 * j + hh
                seg = acum[:, hc:hc + 1] - acum_t[hc:hc + 1, :]
                dec = jnp.exp(jnp.where(mask, seg, NEG_BIG))
                ms.append((cb * dec * dt_t[hc:hc + 1, :]).astype(BF16))
                yield
            c0 = g * gw + 2 * j * P_B
            xp = xs_b[:, c0:c0 + LANES]
            zero = jnp.zeros_like(xp)
            rhs = jnp.concatenate([jnp.where(lo_half, xp, zero), jnp.where(lo_half, zero, xp)], axis=0)
            y_intra = jnp.dot(jnp.concatenate(ms, axis=1), rhs, preferred_element_type=F32)
            yield
            y_ref[b, :, c0:c0 + LANES] = (
                y_intra + eacum_x[:, c0:c0 + LANES] * inter[:, 2 * j * P_B:2 * j * P_B + LANES])
            yield

        yield from _in_turn([pair(j) for j in range(hpg // 2)])
        h_ref[b, :, g * gw:(g + 1) * gw] = dec_x[:, g * gw:(g + 1) * gw] * h_old[:, g * gw:(g + 1) * gw] + upd
        yield

    yield from _in_turn([group(g) for g in range(G_B)])


def _ssd_scan_kernel(xf_ref, dtf_ref, xb_ref, dtb_ref, arow_ref, trif_ref, trib_ref, ef_ref, eb_ref,
                     yf_ref, yb_ref, hf_s, hb_s):
    @pl.when(pl.program_id(0) == 0)
    def _():
        hf_s[...] = jnp.zeros_like(hf_s)
        hb_s[...] = jnp.zeros_like(hb_s)

    chains = []
    for b in range(BATCH):
        chains.append(_ssd_one_direction(xf_ref[b], dtf_ref[b], arow_ref[...], trif_ref[...], ef_ref[...],
                                         hf_s, yf_ref, b, 0, False))
        chains.append(_ssd_one_direction(xb_ref[b], dtb_ref[b], arow_ref[...], trib_ref[...], eb_ref[...],
                                         hb_s, yb_ref, b, 1, True))
    _round_robin(chains)


def _ssd_scan(xact, dtp, a_log):
    q = SSD_CHUNK
    steps = PB // q
    a_neg = -jnp.exp(a_log)
    arow = jnp.concatenate([a_neg.reshape(1, 2 * H_B), jnp.zeros((1, LANES - 2 * H_B), F32)], axis=1)
    r = jnp.arange(q)
    tri_f = (r[None, :] <= r[:, None]).astype(BF16)
    tri_b = (r[None, :] >= r[:, None]).astype(BF16)
    col_head = jnp.arange(D_SSD) // P_B
    lane = jnp.arange(LANES)
    exp_f = (lane[:, None] == col_head[None, :]).astype(BF16)
    exp_b = (lane[:, None] == col_head[None, :] + H_B).astype(BF16)
    fwd = lambda s: (0, _scan_chunk(s, False, q), 0)
    bwd = lambda s: (0, _scan_chunk(s, True, q), 0)
    const = lambda s: (0, 0)
    x3 = xact.reshape(BATCH, PB, CONV_CH)
    d3 = dtp.reshape(BATCH, PB, LANES)
    yf, yb = pl.pallas_call(
        _ssd_scan_kernel,
        grid=(steps,),
        in_specs=[pl.BlockSpec((BATCH, q, CONV_CH), fwd), pl.BlockSpec((BATCH, q, LANES), fwd),
                  pl.BlockSpec((BATCH, q, CONV_CH), bwd), pl.BlockSpec((BATCH, q, LANES), bwd),
                  pl.BlockSpec((1, LANES), const),
                  pl.BlockSpec((q, q), const), pl.BlockSpec((q, q), const),
                  pl.BlockSpec((LANES, D_SSD), const), pl.BlockSpec((LANES, D_SSD), const)],
        out_specs=[pl.BlockSpec((BATCH, q, D_SSD), fwd), pl.BlockSpec((BATCH, q, D_SSD), bwd)],
        out_shape=[jax.ShapeDtypeStruct((BATCH, PB, D_SSD), F32), jax.ShapeDtypeStruct((BATCH, PB, D_SSD), F32)],
        scratch_shapes=[pltpu.VMEM((BATCH, N_B, D_SSD), F32), pltpu.VMEM((BATCH, N_B, D_SSD), F32)],
        compiler_params=_cparams("arbitrary"),
        name="ssd_scan",
    )(x3, d3, x3, d3, arow, tri_f, tri_b, exp_f, exp_b)
    return yf.reshape(T_ALL, D_SSD), yb.reshape(T_ALL, D_SSD)


def _mlstm_one_direction(q, k, v, gi, gf, bi, bf, tri, st_ref, m_ref, h_ref, b, d, rev):
    n = MLSTM_CHUNK
    li = gi + bi
    lf = jax.nn.log_sigmoid(gf + bf)
    yield
    bc = _dot3_left(tri, lf)
    yield
    b_t = bc.T
    li_t = li.T
    yield
    edge = 0 if rev else n - 1
    gtot = bc[edge:edge + 1, :]
    m_old = m_ref[b, 0:1, :]
    w_log = gtot - bc + li
    m_new = jnp.maximum(gtot + m_old, jnp.max(w_log, axis=0, keepdims=True))
    wj = jnp.exp(w_log - m_new)
    dec = jnp.exp(gtot + m_old - m_new)
    inter_log = bc + m_old
    m_ref[b, 0:1, :] = m_new
    yield
    ri = lax.broadcasted_iota(jnp.int32, (n, n), 0)
    ci = lax.broadcasted_iota(jnp.int32, (n, n), 1)
    mask = (ci >= ri) if rev else (ci <= ri)
    lo_half = lax.broadcasted_iota(jnp.int32, (n, LANES), 1) < DQK_C
    row_lo = lax.broadcasted_iota(jnp.int32, (2 * DQK_C, 2 * DV_C), 0) < DQK_C
    ones = jnp.ones((n, DV_C), F32)
    st_old = [st_ref[b, j] for j in range(H_C // 2)]
    upds = {}

    def head(h):
        j, hh = divmod(h, 2)
        gl = GATE_LANE0 + H_C * d + h
        qp = q[:, j * LANES:(j + 1) * LANES] * DQK_C ** -0.5
        kp = k[:, j * LANES:(j + 1) * LANES].astype(BF16)
        qm = jnp.where(lo_half if hh == 0 else jnp.logical_not(lo_half), qp, 0.0).astype(BF16)
        qk = lax.dot_general(qm, kp, _NT, preferred_element_type=F32)
        qs = jnp.dot(qm, st_old[j].astype(BF16), preferred_element_type=F32)
        yield
        dmat = jnp.where(mask, bc[:, gl:gl + 1] - b_t[gl:gl + 1, :] + li_t[gl:gl + 1, :], NEG_BIG)
        il = inter_log[:, gl:gl + 1]
        m_row = jnp.maximum(il, jnp.max(dmat, axis=1, keepdims=True))
        yield
        s = qk * jnp.exp(dmat - m_row)
        w_inter = jnp.exp(il - m_row)
        vh = v[:, h * DV_C:(h + 1) * DV_C]
        yield
        num = jnp.dot(s.astype(BF16), vh.astype(BF16), preferred_element_type=F32) + w_inter * qs[:, :DV_C]
        den = jnp.sum(s, axis=1, keepdims=True) + w_inter * qs[:, DV_C:]
        yield
        h_ref[b, :, h * DV_C:(h + 1) * DV_C] = num / jnp.maximum(jnp.abs(den), jnp.exp(-m_row))
        rhs = (wj[:, gl:gl + 1] * jnp.concatenate([vh, ones], axis=1)).astype(BF16)
        upds[h] = lax.dot_general(kp, rhs, _TN, preferred_element_type=F32)
        yield

    yield from _in_turn([head(h) for h in range(H_C)])
    for j in range(H_C // 2):
        ga = GATE_LANE0 + H_C * d + 2 * j
        decv = jnp.where(row_lo, dec[:, ga:ga + 1], dec[:, ga + 1:ga + 2])
        st_ref[b, j] = decv * st_old[j] + jnp.where(row_lo, upds[2 * j], upds[2 * j + 1])
        yield


def _mlstm_scan_kernel(qf, kf, vf, gif, gff, qb, kb, vb, gib, gfb, bi_ref, bf_ref, trif_ref, trib_ref,
                       hf_ref, hb_ref, stf_s, stb_s, mf_s, mb_s):
    @pl.when(pl.program_id(0) == 0)
    def _():
        stf_s[...] = jnp.zeros_like(stf_s)
        stb_s[...] = jnp.zeros_like(stb_s)
        mf_s[...] = jnp.full_like(mf_s, NEG_STATE)
        mb_s[...] = jnp.full_like(mb_s, NEG_STATE)

    chains = []
    for b in range(BATCH):
        chains.append(_mlstm_one_direction(qf[b], kf[b], vf[b], gif[b], gff[b], bi_ref[...], bf_ref[...],
                                           trif_ref[...], stf_s, mf_s, hf_ref, b, 0, False))
        chains.append(_mlstm_one_direction(qb[b], kb[b], vb[b], gib[b], gfb[b], bi_ref[...], bf_ref[...],
                                           trib_ref[...], stb_s, mb_s, hb_ref, b, 1, True))
    _round_robin(chains)


def _mlstm_scan(z, gate_b):
    n = MLSTM_CHUNK
    steps = PB // n
    qkw = H_C * DQK_C
    pad = lambda t: jnp.concatenate([jnp.zeros((1, GATE_LANE0), F32), t.reshape(1, 2 * H_C),
                                     jnp.zeros((1, LANES - GATE_LANE0 - 2 * H_C), F32)], axis=1)
    bi = pad(gate_b[:, 0, :])
    bf = pad(gate_b[:, 1, :])
    r = jnp.arange(n)
    tri_f = (r[None, :] <= r[:, None]).astype(BF16)
    tri_b = (r[None, :] >= r[:, None]).astype(BF16)
    z3 = z.reshape(BATCH, PB, Z_COLS)

    def specs(rev):
        ch = lambda s: _scan_chunk(s, rev, n)
        return [pl.BlockSpec((BATCH, n, qkw), lambda s: (0, ch(s), Z_MQ // qkw)),
                pl.BlockSpec((BATCH, n, qkw), lambda s: (0, ch(s), Z_MK // qkw)),
                pl.BlockSpec((BATCH, n, D_MLSTM), lambda s: (0, ch(s), Z_MV // D_MLSTM)),
                pl.BlockSpec((BATCH, n, LANES), lambda s: (0, ch(s), Z_MG // LANES)),
                pl.BlockSpec((BATCH, n, LANES), lambda s: (0, ch(s), Z_DT // LANES))]

    const = lambda s: (0, 0)
    n_pairs = H_C // 2
    hf, hb = pl.pallas_call(
        _mlstm_scan_kernel,
        grid=(steps,),
        in_specs=specs(False) + specs(True) + [
            pl.BlockSpec((1, LANES), const), pl.BlockSpec((1, LANES), const),
            pl.BlockSpec((n, n), const), pl.BlockSpec((n, n), const)],
        out_specs=[pl.BlockSpec((BATCH, n, D_MLSTM), lambda s: (0, _scan_chunk(s, False, n), 0)),
                   pl.BlockSpec((BATCH, n, D_MLSTM), lambda s: (0, _scan_chunk(s, True, n), 0))],
        out_shape=[jax.ShapeDtypeStruct((BATCH, PB, D_MLSTM), F32), jax.ShapeDtypeStruct((BATCH, PB, D_MLSTM), F32)],
        scratch_shapes=[pltpu.VMEM((BATCH, n_pairs, 2 * DQK_C, 2 * DV_C), F32),
                        pltpu.VMEM((BATCH, n_pairs, 2 * DQK_C, 2 * DV_C), F32),
                        pltpu.VMEM((BATCH, 8, LANES), F32), pltpu.VMEM((BATCH, 8, LANES), F32)],
        compiler_params=_cparams("arbitrary"),
        name="mlstm_scan",
    )(z3, z3, z3, z3, z3, z3, z3, z3, z3, z3, bi, bf, tri_f, tri_b)
    return hf.reshape(T_ALL, D_MLSTM), hb.reshape(T_ALL, D_MLSTM)


S5_Q = 256
S5_SEG = S5_Q // 8
S5_LANES = G_S5 * P_S5
S5_SLAB = 512
S5_NSLAB = S5_LANES // S5_SLAB


def _s5_dir_kernel(u_ref, perm_ref, wbr_ref, wbi_ref, atab_ref, apr_ref, api_ref, wcr_ref, wci_ref, y_ref,
                   ur_s, ui_s, xr_s, xi_s, st_s, car_s, *, reverse):
    @pl.when(pl.program_id(0) == 0)
    def _():
        st_s[...] = jnp.zeros_like(st_s)

    for b in range(BATCH):
        up = jnp.dot(perm_ref[...], u_ref[b].astype(BF16), preferred_element_type=F32).astype(BF16)
        for m in range(S5_NSLAB):
            um = up[:, m * LANES:(m + 1) * LANES]
            ur_s[b, :, m * S5_SLAB:(m + 1) * S5_SLAB] = jnp.dot(um, wbr_ref[m], preferred_element_type=F32)
            ui_s[b, :, m * S5_SLAB:(m + 1) * S5_SLAB] = jnp.dot(um, wbi_ref[m], preferred_element_type=F32)

    per = 4
    for grp in range(S5_LANES // (per * LANES)):
        cols = [grp * per * LANES + j * LANES for j in range(per)]
        a_r = [atab_ref[0:8, c0:c0 + LANES] for c0 in cols]
        a_i = [atab_ref[8:16, c0:c0 + LANES] for c0 in cols]

        def body(i, carry, cols=cols, a_r=a_r, a_i=a_i):
            t = (S5_SEG - 1 - i) if reverse else i
            r0 = pl.multiple_of(t * 8, 8)
            new = []
            for b in range(BATCH):
                for j, c0 in enumerate(cols):
                    xr, xi = carry[2 * (b * per + j)], carry[2 * (b * per + j) + 1]
                    nr = a_r[j] * xr - a_i[j] * xi + ur_s[b, pl.ds(r0, 8), c0:c0 + LANES]
                    ni = a_r[j] * xi + a_i[j] * xr + ui_s[b, pl.ds(r0, 8), c0:c0 + LANES]
                    ur_s[b, pl.ds(r0, 8), c0:c0 + LANES] = nr
                    ui_s[b, pl.ds(r0, 8), c0:c0 + LANES] = ni
                    new += [nr, ni]
            return tuple(new)

        lax.fori_loop(0, S5_SEG, body, tuple(jnp.zeros((8, LANES), F32) for _ in range(2 * per * BATCH)), unroll=2)

    as_r = atab_ref[16:17, :]
    as_i = atab_ref[17:18, :]
    end_row = 0 if reverse else 8 * (S5_SEG - 1)
    for b in range(BATCH):
        cr = st_s[b, 0:1, :]
        ci = st_s[b, 1:2, :]
        for k in (range(7, -1, -1) if reverse else range(8)):
            car_s[b, k:k + 1, :] = cr
            car_s[b, 8 + k:9 + k, :] = ci
            er = ur_s[b, end_row + k:end_row + k + 1, :]
            ei = ui_s[b, end_row + k:end_row + k + 1, :]
            cr, ci = er + as_r * cr - as_i * ci, ei + as_r * ci + as_i * cr
        st_s[b, 0:1, :] = cr
        st_s[b, 1:2, :] = ci

    for b in range(BATCH):
        for m in range(S5_NSLAB):
            cs = slice(m * S5_SLAB, (m + 1) * S5_SLAB)
            c_r = jnp.concatenate([car_s[b, 0:8, cs], car_s[b, 0:8, cs]], axis=0)
            c_i = jnp.concatenate([car_s[b, 8:16, cs], car_s[b, 8:16, cs]], axis=0)

            def fix(i, _, b=b, cs=cs, c_r=c_r, c_i=c_i):
                r0 = pl.multiple_of(i * 16, 16)
                p_r = apr_ref[pl.ds(r0, 16), cs]
                p_i = api_ref[pl.ds(r0, 16), cs]
                xr_s[b, pl.ds(r0, 16), cs] = (ur_s[b, pl.ds(r0, 16), cs] + p_r * c_r - p_i * c_i).astype(BF16)
                xi_s[b, pl.ds(r0, 16), cs] = (ui_s[b, pl.ds(r0, 16), cs] + p_r * c_i + p_i * c_r).astype(BF16)
                return 0

            lax.fori_loop(0, S5_Q // 16, fix, 0, unroll=2)

    for b in range(BATCH):
        for m in range(S5_NSLAB):
            cs = slice(m * S5_SLAB, (m + 1) * S5_SLAB)
            y_ref[b, :, m * LANES:(m + 1) * LANES] = (
                jnp.dot(xr_s[b, :, cs], wcr_ref[m], preferred_element_type=F32)
                - jnp.dot(xi_s[b, :, cs], wci_ref[m], preferred_element_type=F32))


def _s5_direction(z3, perm, wbr, wbi, atab, apr, api, wcr, wci, reverse):
    steps = PB // S5_Q
    chunk = lambda s: _scan_chunk(s, reverse, S5_Q)
    const2 = lambda s: (0, 0)
    const3 = lambda s: (0, 0, 0)
    return pl.pallas_call(
        functools.partial(_s5_dir_kernel, reverse=reverse),
        grid=(steps,),
        in_specs=[pl.BlockSpec((BATCH, S5_Q, D_S5), lambda s: (0, chunk(s), Z_D // D_S5)),
                  pl.BlockSpec((S5_Q, S5_Q), const2),
                  pl.BlockSpec(wbr.shape, const3),
                  pl.BlockSpec(wbi.shape, const3),
                  pl.BlockSpec(atab.shape, const2),
                  pl.BlockSpec(apr.shape, const2),
                  pl.BlockSpec(api.shape, const2),
                  pl.BlockSpec(wcr.shape, const3),
                  pl.BlockSpec(wci.shape, const3)],
        out_specs=pl.BlockSpec((BATCH, S5_Q, D_S5), lambda s: (0, chunk(s), 0)),
        out_shape=jax.ShapeDtypeStruct((BATCH, PB, D_S5), F32),
        scratch_shapes=[pltpu.VMEM((BATCH, S5_Q, S5_LANES), F32), pltpu.VMEM((BATCH, S5_Q, S5_LANES), F32),
                        pltpu.VMEM((BATCH, S5_Q, S5_LANES), BF16), pltpu.VMEM((BATCH, S5_Q, S5_LANES), BF16),
                        pltpu.VMEM((BATCH, 8, S5_LANES), F32), pltpu.VMEM((BATCH, 16, S5_LANES), F32)],
        compiler_params=_cparams("arbitrary"),
        name="s5_bwd" if reverse else "s5_fwd",
    )(z3, perm, wbr, wbi, atab, apr, api, wcr, wci).reshape(T_ALL, D_S5)


def _s5_tables(lam_re, lam_im, log_dt, b_re, b_im, c_re, c_im):
    dt = jnp.exp(log_dt)[..., None]
    mag = jnp.exp(lam_re * dt)
    ar = mag * jnp.cos(lam_im * dt)
    ai = mag * jnp.sin(lam_im * dt)
    den = lam_re * lam_re + lam_im * lam_im
    cr_ = ((ar - 1.0) * lam_re + ai * lam_im) / den
    ci_ = (ai * lam_re - (ar - 1.0) * lam_im) / den
    bbr = cr_[..., None] * b_re - ci_[..., None] * b_im
    bbi = cr_[..., None] * b_im + ci_[..., None] * b_re
    gps = S5_SLAB // P_S5
    eye = jnp.eye(gps, dtype=F32)

    def drive_w(bb):
        t = bb.reshape(S5_NSLAB, gps, P_S5, S5_GROUP)
        w = jnp.einsum('mgpc,gh->mgchp', t, eye)
        return w.reshape(S5_NSLAB, gps * S5_GROUP, gps * P_S5).astype(BF16)

    def read_w(cc):
        t = cc.reshape(S5_NSLAB, gps, S5_GROUP, P_S5)
        w = jnp.einsum('mgcp,gh->mgphc', t, eye)
        return w.reshape(S5_NSLAB, gps * P_S5, gps * S5_GROUP).astype(BF16)

    steps = jnp.arange(1, S5_SEG + 1, dtype=F32)[:, None]
    out = []
    for d in range(2):
        decay = (lam_re[d] * dt[d]).reshape(1, S5_LANES)
        angle = (lam_im[d] * dt[d]).reshape(1, S5_LANES)
        pmag = jnp.exp(steps * decay)
        pr = pmag * jnp.cos(steps * angle)
        pi = pmag * jnp.sin(steps * angle)
        if d == 1:
            apr, api = jnp.repeat(pr[::-1], 8, axis=0), jnp.repeat(pi[::-1], 8, axis=0)
        else:
            apr, api = jnp.repeat(pr, 8, axis=0), jnp.repeat(pi, 8, axis=0)
        atab = jnp.concatenate([jnp.broadcast_to(pr[0:1], (8, S5_LANES)), jnp.broadcast_to(pi[0:1], (8, S5_LANES)),
                                pr[S5_SEG - 1:], pi[S5_SEG - 1:], jnp.zeros((6, S5_LANES), F32)], axis=0)
        out.append((drive_w(bbr[d]), drive_w(bbi[d]), atab, apr, api))
    r = jnp.arange(S5_Q)
    src = (r % 8) * S5_SEG + r // 8
    perm = (src[:, None] == jnp.arange(S5_Q)[None, :]).astype(BF16)
    return out, read_w(c_re), read_w(c_im), perm


def _s5_scans(z, p):
    dirs, wcr, wci, perm = _s5_tables(p["s5_lam_re"], p["s5_lam_im"], p["s5_log_dt"], p["s5_b_re"], p["s5_b_im"],
                                      p["s5_c_re"], p["s5_c_im"])
    z3 = z.reshape(BATCH, PB, Z_COLS)
    yf = _s5_direction(z3, perm, *dirs[0], wcr, wci, reverse=False)
    yb = _s5_direction(z3, perm, *dirs[1], wcr, wci, reverse=True)
    return yf, yb, perm.T


def _merge_kernel(ya_ref, sf_ref, sb_ref, sx_ref, sg_ref, mf_ref, mb_ref, mo_ref, df_ref, db_ref, du_ref,
                  permt_ref, sd_ref, dd_ref, gw_ref, gb_ref, onw_ref, w0, w1, w2, w3, x_ref, mod_ref,
                  o_ref, wb_s, *, g_idx):
    @pl.when(pl.program_id(0) == 0)
    def _():
        for r, w in enumerate((w0, w1, w2, w3)):
            wb_s[r] = w[...].astype(BF16)

    onw = onw_ref[...]
    out = {}

    def attn():
        out[0] = jnp.dot(ya_ref[...], wb_s[0], preferred_element_type=F32)
        yield

    def ssd():
        gate = sg_ref[...]
        y = (sf_ref[...] + sb_ref[...] + sd_ref[...] * sx_ref[...]) * (gate * jax.nn.sigmoid(gate))
        yield
        y = (_rms(y) * onw[:, D_MLA:D_MLA + D_SSD]).astype(BF16)
        yield
        out[1] = jnp.dot(y, wb_s[1], preferred_element_type=F32)
        yield

    def mlstm():
        gate = jax.nn.sigmoid(mo_ref[...])
        c0 = D_MLA + D_SSD
        parts = []
        for h in range(H_C):
            cs = slice(h * DV_C, (h + 1) * DV_C)
            hn = _rms(mf_ref[:, cs] + mb_ref[:, cs])
            parts.append((hn * gate[:, cs] * onw[:, c0 + h * DV_C:c0 + (h + 1) * DV_C]).astype(BF16))
            yield
        out[2] = jnp.dot(jnp.concatenate(parts, axis=1), wb_s[2], preferred_element_type=F32)
        yield

    def s5():
        y = _dot3_left(permt_ref[...], df_ref[...] + db_ref[...])
        yield
        y = jax.nn.gelu(y + dd_ref[...] * du_ref[...])
        yield
        gate = jax.nn.sigmoid(jnp.dot(y.astype(BF16), gw_ref[...], preferred_element_type=F32) + gb_ref[...])
        yield
        y = (_rms(y * gate) * onw[:, D_MLA + D_SSD + D_MLSTM:]).astype(BF16)
        yield
        out[3] = jnp.dot(y, wb_s[3], preferred_element_type=F32)
        yield

    _round_robin([attn(), ssd(), mlstm(), s5()])
    g = mod_ref[0][g_idx:g_idx + 1, :]
    o_ref[...] = x_ref[...] + g * (out[0] + out[1] + out[2] + out[3])


def _merge_proj(ya, ssd, mlstm, s5, z, p, w_out, l, x, modtab, g_idx):
    sf, sb, xact = ssd
    mf, mb = mlstm
    df, db, permt = s5
    kq = D_MODEL // 4
    row = lambda i: (i, 0)
    const = lambda i: (0, 0)
    grp = lambda col: pl.BlockSpec((TILE, kq), lambda i: (i, col // kq))
    w_specs = [pl.BlockSpec((None, kq, D_MODEL), functools.partial(lambda i, r: (l, r, 0), r=r),
                            pipeline_mode=pl.Buffered(1)) for r in range(4)]
    return pl.pallas_call(
        functools.partial(_merge_kernel, g_idx=g_idx),
        grid=(T_ALL // TILE,),
        in_specs=[pl.BlockSpec((TILE, kq), row),
                  pl.BlockSpec((TILE, kq), row), pl.BlockSpec((TILE, kq), row), grp(0), grp(Z_SZ),
                  pl.BlockSpec((TILE, kq), row), pl.BlockSpec((TILE, kq), row), grp(Z_MO),
                  pl.BlockSpec((TILE, kq), row), pl.BlockSpec((TILE, kq), row), grp(Z_D),
                  pl.BlockSpec((TILE, TILE), const),
                  pl.BlockSpec((1, kq), const), pl.BlockSpec((1, kq), const),
                  pl.BlockSpec((kq, kq), const), pl.BlockSpec((1, kq), const),
                  pl.BlockSpec((1, D_MODEL), const)] + w_specs + [
            pl.BlockSpec((TILE, D_MODEL), row),
            pl.BlockSpec((1, 6, D_MODEL), lambda i: (_tile_mod_row(i), 0, 0))],
        out_specs=pl.BlockSpec((TILE, D_MODEL), row),
        out_shape=jax.ShapeDtypeStruct((T_ALL, D_MODEL), F32),
        scratch_shapes=[pltpu.VMEM((4, kq, D_MODEL), BF16)],
        compiler_params=_cparams("arbitrary"),
        name="merge_proj",
    )(ya, sf, sb, xact, z, mf, mb, z, df, db, z, permt,
      jnp.repeat(p["ssd_d"], P_B).reshape(1, kq), p["s5_d"].reshape(1, kq),
      p["s5_glu_w"].astype(BF16), p["s5_glu_b"].reshape(1, kq), p["out_norm_w"].reshape(1, D_MODEL),
      w_out, w_out, w_out, w_out, x, modtab)


def _rope_tables():
    pos = np.arange(SEQ)
    row = (pos // GRID_W).astype(np.float32)
    col = (pos % GRID_W).astype(np.float32)
    inv_freq = (ROPE_BASE ** (-np.arange(ROPE_AXIS // 2, dtype=np.float32) * 2.0 / ROPE_AXIS)).astype(np.float32)
    ang_r = row[:, None] * inv_freq
    ang_c = col[:, None] * inv_freq
    zeros = np.zeros((SEQ, LANES - D_ROPE), np.float32)
    cos = np.concatenate([np.cos(ang_r), np.cos(ang_r), np.cos(ang_c), np.cos(ang_c), zeros], axis=1)
    sin = np.concatenate([np.sin(ang_r), np.sin(ang_r), np.sin(ang_c), np.sin(ang_c), zeros], axis=1)
    cos_c = np.concatenate([np.ones((TILE, D_ROPE), np.float32), np.zeros((TILE, LANES - D_ROPE), np.float32)], axis=1)
    sin_c = np.zeros((TILE, LANES), np.float32)
    table = np.concatenate([np.concatenate([cos, sin], axis=1), np.concatenate([cos_c, sin_c], axis=1)], axis=0)
    return jnp.asarray(table.astype(np.float32))


def _layout_mla(w_uq, w_ukv):
    k = w_uq.shape[0]
    qa, qb, wk, wv = [], [], [], []
    for h in range(H_A):
        base = h * (D_NOPE + D_ROPE)
        rope = w_uq[:, base + D_NOPE:base + D_NOPE + D_ROPE]
        qa += [w_uq[:, base:base + D_NOPE], rope, jnp.zeros((k, LANES - D_ROPE), w_uq.dtype)]
        qb += [_rot_cols(rope), jnp.zeros((k, LANES - D_ROPE), w_uq.dtype)]
        kb = h * (D_NOPE + D_V)
        wk.append(w_ukv[:, kb:kb + D_NOPE])
        wv.append(w_ukv[:, kb + D_NOPE:kb + D_NOPE + D_V])
    cat = lambda xs: jnp.concatenate(xs, axis=1).astype(BF16)
    return cat(qa), cat(qb), cat(wk), cat(wv)


def _layer(xall, hx, modtab, p, big, l, cs, norm_w, next_modtab, last):
    z = _mm(hx, _layout_w_in(big["w_in"], l), MM_ROWS, Z_COLS // 3)
    onw = p["out_norm_w"]
    wqa, wqb, wk, wv = _layout_mla(p["mla_w_uq"], p["mla_w_ukv"])
    q, k, v = _mla_prep(z, cs, p["mla_q_norm_w"], p["mla_kv_norm_w"], wqa, wqb, wk, wv)
    ya = _attention(q, k, v, onw[:D_MLA].reshape(1, D_MLA))
    xact, dtp = _ssd_prep(z, p["ssd_conv_w"], p["ssd_conv_b"], p["ssd_dt_bias"])
    ssd = _ssd_scan(xact, dtp, p["ssd_a_log"]) + (xact,)
    mlstm = _mlstm_scan(z, p["mlstm_gate_b"])
    s5 = _s5_scans(z, p)
    xall = _merge_proj(ya, ssd, mlstm, s5, z, p, big["w_out"], l, xall, modtab, 2)

    w_router = jnp.concatenate([p["moe_w_group"], p["moe_w_expert"],
                                jnp.zeros((D_MODEL, ROUTER_COLS - N_GROUPS - N_EXPERTS), F32)], axis=1)
    wr_hi = w_router.astype(BF16)
    wr_lo = (w_router - wr_hi.astype(F32)).astype(BF16)
    h2, route, route_t, cnt = _prenorm_router(xall, p["norm2_w"], modtab, jnp.stack([wr_hi, wr_lo]), 3, 4, last)
    return _moe(xall, h2, route, route_t, cnt, modtab, big["moe_w_gate"], big["moe_w_up"], big["moe_w_down"], l,
                norm_w, next_modtab, last)


def kernel(x, c, ctx, c_ctx, mod_w, mod_b, norm1_w, w_in, mla_q_norm_w, mla_kv_norm_w, mla_w_uq, mla_w_ukv,
           ssd_conv_w, ssd_conv_b, ssd_a_log, ssd_dt_bias, ssd_d, mlstm_gate_b, s5_lam_re, s5_lam_im,
           s5_log_dt, s5_b_re, s5_b_im, s5_c_re, s5_c_im, s5_d, s5_glu_w, s5_glu_b, out_norm_w, w_out,
           norm2_w, moe_w_group, moe_w_expert, moe_w_gate, moe_w_up, moe_w_down, final_norm_w):
    stacked = {"norm1_w": norm1_w, "mla_q_norm_w": mla_q_norm_w, "mla_kv_norm_w": mla_kv_norm_w,
               "mla_w_uq": mla_w_uq, "mla_w_ukv": mla_w_ukv, "ssd_conv_w": ssd_conv_w, "ssd_conv_b": ssd_conv_b,
               "ssd_a_log": ssd_a_log, "ssd_dt_bias": ssd_dt_bias, "ssd_d": ssd_d, "mlstm_gate_b": mlstm_gate_b,
               "s5_lam_re": s5_lam_re, "s5_lam_im": s5_lam_im, "s5_log_dt": s5_log_dt, "s5_b_re": s5_b_re,
               "s5_b_im": s5_b_im, "s5_c_re": s5_c_re, "s5_c_im": s5_c_im, "s5_d": s5_d, "s5_glu_w": s5_glu_w,
               "s5_glu_b": s5_glu_b, "out_norm_w": out_norm_w, "norm2_w": norm2_w,
               "moe_w_group": moe_w_group, "moe_w_expert": moe_w_expert}
    big = {"w_in": w_in, "w_out": w_out, "moe_w_gate": moe_w_gate, "moe_w_up": moe_w_up, "moe_w_down": moe_w_down}
    cs = _rope_tables()
    cc = jnp.concatenate([c, c_ctx[None, :], jnp.zeros((8 - BATCH - 1, D_MODEL), F32)], axis=0)
    modtabs = [_modulation(cc, mod_w, mod_b, l)[:BATCH + 1].reshape(BATCH + 1, 6, D_MODEL) for l in range(DEPTH)]
    xall, hx = _prenorm(x, ctx, norm1_w[0], modtabs[0], 0, 1)
    for l in range(DEPTH):
        p = {name: val[l] for name, val in stacked.items()}
        if l == DEPTH - 1:
            out = _layer(xall, hx, modtabs[l], p, big, l, cs, final_norm_w, modtabs[l], True)
        else:
            xall, hx = _layer(xall, hx, modtabs[l], p, big, l, cs, norm1_w[l + 1], modtabs[l + 1], False)
    return out.reshape(BATCH, SEQ, D_MODEL)
```

```python
import functools
import math

import jax
import jax.numpy as jnp
import numpy as np
from jax import lax
from jax.experimental import pallas as pl
from jax.experimental.pallas import tpu as pltpu

F32 = jnp.float32
BF16 = jnp.bfloat16

D_MODEL = 2048
BATCH = 2
SEQ = 4096
DEPTH = 2
GRID_W = 64
CTX_LEN = 256
EPS = 1e-6
NEG_STATE = -1e30
NEG_BIG = -1e30

H_A = 4
D_NOPE = 128
D_ROPE = 64
D_V = 128
Q_RANK = 384
KV_RANK = 128
ROPE_AXIS = D_ROPE // 2
ROPE_BASE = 10000.0
D_MLA = H_A * D_V
D_SSD = 512
P_B = 64
H_B = D_SSD // P_B
G_B = 2
N_B = 128
SSD_CONV = 5
SSD_CHUNK = 128
CONV_CH = D_SSD + 2 * G_B * N_B
D_MLSTM = 512
H_C = 4
DV_C = D_MLSTM // H_C
DQK_C = DV_C // 2
MLSTM_CHUNK = 128
D_S5 = 512
S5_GROUP = 16
G_S5 = D_S5 // S5_GROUP
P_S5 = 64
A_COLS = Q_RANK + KV_RANK + D_ROPE
B_COLS = D_SSD + CONV_CH + 2 * H_B
C_COLS = 2 * H_C * DQK_C + 2 * D_MLSTM + 4 * H_C
N_GROUPS = 4
EXPERTS_PER_GROUP = 8
N_EXPERTS = N_GROUPS * EXPERTS_PER_GROUP
TOP_K = 2
D_EXPERT = 512

PB = CTX_LEN + SEQ
T_X = BATCH * SEQ
T_ALL = BATCH * PB

LANES = 128
VMEM_LIMIT_BYTES = 56 * 1024 * 1024

TILE = 256
TPB = PB // TILE
XT = SEQ // TILE
MM_ROWS = 512

Z_CQ = 0
Z_CKV = 384
Z_KR = 512
Z_KRR = 640
Z_MQ = 768
Z_XBC = 1024
Z_SZ = 2048
Z_MV = 2560
Z_MO = 3072
Z_D = 3584
Z_MK = 4096
Z_DT = 4352
Z_MG = 4480
Z_COLS = 4608
GATE_LANE0 = 2 * H_B

MOE_ROWS = 256
ROUTER_COLS = 128

_NT = (((1,), (1,)), ((), ()))
_TN = (((0,), (0,)), ((), ()))


def _cparams(*sem):
    return pltpu.CompilerParams(dimension_semantics=sem, vmem_limit_bytes=VMEM_LIMIT_BYTES)


def _tile_mod_row(i):
    return jnp.where(i % TPB == 0, BATCH, i // TPB)


def _token_tile(i, latent_only):
    return (i // XT) * TPB + 1 + i % XT if latent_only else i


def _rms(x, w=None):
    y = x * lax.rsqrt(jnp.mean(x * x, axis=-1, keepdims=True) + EPS)
    return y if w is None else y * w


def _in_turn(chains):
    live = list(chains)
    while live:
        nxt = []
        for chain in live:
            try:
                next(chain)
                nxt.append(chain)
            except StopIteration:
                pass
        live = nxt
        yield


def _round_robin(chains):
    for _ in _in_turn(chains):
        pass


def _split3(x):
    hi = x.astype(BF16)
    r1 = x - hi.astype(F32)
    mid = r1.astype(BF16)
    lo = (r1 - mid.astype(F32)).astype(BF16)
    return hi, mid, lo


def _dot3_left(sel, x):
    hi, mid, lo = _split3(x)
    return (jnp.dot(sel, hi, preferred_element_type=F32) + jnp.dot(sel, mid, preferred_element_type=F32)
            + jnp.dot(sel, lo, preferred_element_type=F32))


def _dot3_right(x, sel):
    hi, mid, lo = _split3(x)
    return (jnp.dot(hi, sel, preferred_element_type=F32) + jnp.dot(mid, sel, preferred_element_type=F32)
            + jnp.dot(lo, sel, preferred_element_type=F32))


def _mod_kernel(a_ref, w_ref, b_ref, o_ref):
    a = a_ref[...]
    a = a * jax.nn.sigmoid(a)
    o_ref[...] = jnp.dot(a.astype(BF16), w_ref[...].astype(BF16), preferred_element_type=F32) + b_ref[...]


def _modulation(cc, mod_w, mod_b, l):
    n = mod_w.shape[2]
    tn = 1024
    return pl.pallas_call(
        _mod_kernel,
        grid=(n // tn,),
        in_specs=[pl.BlockSpec((8, D_MODEL), lambda j: (0, 0)),
                  pl.BlockSpec((None, D_MODEL, tn), lambda j: (l, 0, j)),
                  pl.BlockSpec((None, 1, tn), lambda j: (l, 0, j))],
        out_specs=pl.BlockSpec((8, tn), lambda j: (0, j)),
        out_shape=jax.ShapeDtypeStruct((8, n), F32),
        compiler_params=_cparams("arbitrary"),
        name="modulation",
    )(cc, mod_w, mod_b.reshape(DEPTH, 1, n))


def _prenorm_kernel(x_ref, c_ref, w_ref, mod_ref, xall_ref, o_ref, *, sh_idx, sc_idx):
    v = jnp.where(pl.program_id(0) % TPB == 0, c_ref[...], x_ref[...])
    xall_ref[...] = v
    y = _rms(v, w_ref[...])
    m = mod_ref[0]
    y = y * (1.0 + m[sc_idx:sc_idx + 1, :]) + m[sh_idx:sh_idx + 1, :]
    o_ref[...] = y.astype(o_ref.dtype)


def _prenorm(x, ctx, w, modtab, sh_idx, sc_idx):
    row = lambda i: (i, 0)
    return pl.pallas_call(
        functools.partial(_prenorm_kernel, sh_idx=sh_idx, sc_idx=sc_idx),
        grid=(T_ALL // TILE,),
        in_specs=[pl.BlockSpec((TILE, D_MODEL), lambda i: ((i // TPB) * XT + jnp.maximum(i % TPB - 1, 0), 0)),
                  pl.BlockSpec((TILE, D_MODEL), lambda i: (i // TPB, 0)),
                  pl.BlockSpec((1, D_MODEL), lambda i: (0, 0)),
                  pl.BlockSpec((1, 6, D_MODEL), lambda i: (_tile_mod_row(i), 0, 0))],
        out_specs=[pl.BlockSpec((TILE, D_MODEL), row), pl.BlockSpec((TILE, D_MODEL), row)],
        out_shape=[jax.ShapeDtypeStruct((T_ALL, D_MODEL), F32), jax.ShapeDtypeStruct((T_ALL, D_MODEL), BF16)],
        compiler_params=_cparams("arbitrary"),
        name="prenorm",
    )(x.reshape(T_X, D_MODEL), ctx.reshape(BATCH * CTX_LEN, D_MODEL), w.reshape(1, D_MODEL), modtab)


def _mm_kernel(a_ref, w_ref, o_ref):
    o_ref[...] = jnp.dot(a_ref[...], w_ref[...], preferred_element_type=F32).astype(o_ref.dtype)


def _mm(a, w, tm, tn, out_dtype=F32):
    m, k = a.shape
    n = w.shape[1]
    return pl.pallas_call(
        _mm_kernel,
        grid=(n // tn, m // tm),
        in_specs=[pl.BlockSpec((tm, k), lambda j, i: (i, 0)),
                  pl.BlockSpec((k, tn), lambda j, i: (0, j))],
        out_specs=pl.BlockSpec((tm, tn), lambda j, i: (i, j)),
        out_shape=jax.ShapeDtypeStruct((m, n), out_dtype),
        compiler_params=_cparams("arbitrary", "arbitrary"),
        name="in_proj",
    )(a, w)


W_IN_B0 = A_COLS
W_IN_C0 = A_COLS + B_COLS
W_IN_D0 = A_COLS + B_COLS + C_COLS
W_IN_CM = W_IN_C0 + 2 * H_C * DQK_C
W_IN_GB = W_IN_CM + 2 * D_MLSTM
W_IN_WIDE = ((Z_CQ, 0, Q_RANK + KV_RANK),
             (Z_MQ, W_IN_C0, H_C * DQK_C),
             (Z_XBC, W_IN_B0 + D_SSD, CONV_CH),
             (Z_SZ, W_IN_B0, D_SSD),
             (Z_MV, W_IN_CM, 2 * D_MLSTM),
             (Z_D, W_IN_D0, D_S5),
             (Z_MK, W_IN_C0 + H_C * DQK_C, H_C * DQK_C))
W_IN_ROWS = 256


def _rot_cols(w):
    q = ROPE_AXIS // 2
    return jnp.concatenate([-w[:, q:2 * q], w[:, 0:q], -w[:, 3 * q:4 * q], w[:, 2 * q:3 * q]], axis=1)


W_IN_KR0 = Q_RANK + KV_RANK
W_IN_DT0 = W_IN_B0 + D_SSD + CONV_CH
W_IN_WINDOWS = tuple(c // LANES * LANES for c in (W_IN_KR0, W_IN_DT0, W_IN_GB))


def _narrow_selector():
    sel = np.zeros((3 * LANES, 4 * LANES), np.float32)

    def put(window, src_col, dst_col, width, sign=1.0):
        base = window * LANES + src_col - W_IN_WINDOWS[window]
        for c in range(width):
            sel[base + c, dst_col + c] = sign

    q = ROPE_AXIS // 2
    put(0, W_IN_KR0, 0, D_ROPE)
    for dst, src, sign in ((0, q, -1.0), (q, 0, 1.0), (2 * q, 3 * q, -1.0), (3 * q, 2 * q, 1.0)):
        put(0, W_IN_KR0 + src, LANES + dst, q, sign)
    put(1, W_IN_DT0, 2 * LANES, 2 * H_B)
    for d in range(2):
        put(2, W_IN_GB + (2 * d + 1) * H_C, 2 * LANES + GATE_LANE0 + d * H_C, H_C)
        put(2, W_IN_GB + 2 * d * H_C, 3 * LANES + GATE_LANE0 + d * H_C, H_C)
    return jnp.asarray(sel, dtype=BF16)


def _w_in_layout_kernel(w_ref, sel_ref, o_ref):
    for dst, src, width in W_IN_WIDE:
        o_ref[:, dst:dst + width] = w_ref[:, src:src + width].astype(BF16)
    windows = jnp.concatenate([w_ref[:, c:c + LANES] for c in W_IN_WINDOWS], axis=1).astype(BF16)
    small = jnp.dot(windows, sel_ref[...], preferred_element_type=F32).astype(BF16)
    o_ref[:, Z_KR:Z_MQ] = small[:, 0:2 * LANES]
    o_ref[:, Z_DT:Z_COLS] = small[:, 2 * LANES:4 * LANES]


def _layout_w_in(w_in, l):
    _, k, n = w_in.shape
    sel = _narrow_selector()
    return pl.pallas_call(
        _w_in_layout_kernel,
        grid=(k // W_IN_ROWS,),
        in_specs=[pl.BlockSpec((None, W_IN_ROWS, n), lambda i: (l, i, 0)),
                  pl.BlockSpec(sel.shape, lambda i: (0, 0))],
        out_specs=pl.BlockSpec((W_IN_ROWS, Z_COLS), lambda i: (i, 0)),
        out_shape=jax.ShapeDtypeStruct((k, Z_COLS), BF16),
        compiler_params=_cparams("arbitrary"),
        name="w_in_layout",
    )(w_in, sel)


def _mla_prep_kernel(za_ref, cs_ref, qw_ref, kvw_ref, wqa_ref, wqb_ref, wk_ref, wv_ref, q_ref, k_ref, v_ref):
    za = za_ref[...]
    cos = cs_ref[:, :LANES]
    sin = cs_ref[:, LANES:]
    qn = _rms(za[:, Z_CQ:Z_CQ + Q_RANK], qw_ref[...]).astype(BF16)
    kvn = _rms(za[:, Z_CKV:Z_CKV + KV_RANK], kvw_ref[...]).astype(BF16)
    qa = jnp.dot(qn, wqa_ref[...], preferred_element_type=F32)
    qb = jnp.dot(qn, wqb_ref[...], preferred_element_type=F32)
    kn = jnp.dot(kvn, wk_ref[...], preferred_element_type=F32)
    v = jnp.dot(kvn, wv_ref[...], preferred_element_type=F32)
    kr = (za[:, Z_KR:Z_KR + LANES] * cos + za[:, Z_KRR:Z_KRR + LANES] * sin).astype(BF16)
    for h in range(H_A):
        c0 = h * 2 * LANES
        q_ref[:, c0:c0 + LANES] = qa[:, c0:c0 + LANES].astype(BF16)
        q_ref[:, c0 + LANES:c0 + 2 * LANES] = (
            qa[:, c0 + LANES:c0 + 2 * LANES] * cos + qb[:, h * LANES:(h + 1) * LANES] * sin).astype(BF16)
        k_ref[:, c0:c0 + LANES] = kn[:, h * LANES:(h + 1) * LANES].astype(BF16)
        k_ref[:, c0 + LANES:c0 + 2 * LANES] = kr
    v_ref[...] = v.astype(BF16)


ATT_W = H_A * 2 * LANES


def _mla_prep(z, cs, qw, kvw, wqa, wqb, wk, wv):
    const = lambda i: (0, 0)
    rope_blk = lambda i: (jnp.where(i % TPB == 0, XT, i % TPB - 1), 0)
    return pl.pallas_call(
        _mla_prep_kernel,
        grid=(T_ALL // TILE,),
        in_specs=[pl.BlockSpec((TILE, Z_MQ), lambda i: (i, 0)),
                  pl.BlockSpec((TILE, 2 * LANES), rope_blk),
                  pl.BlockSpec((1, Q_RANK), const),
                  pl.BlockSpec((1, KV_RANK), const),
                  pl.BlockSpec(wqa.shape, const),
                  pl.BlockSpec(wqb.shape, const),
                  pl.BlockSpec(wk.shape, const),
                  pl.BlockSpec(wv.shape, const)],
        out_specs=[pl.BlockSpec((TILE, ATT_W), lambda i: (i, 0)),
                   pl.BlockSpec((TILE, ATT_W), lambda i: (i, 0)),
                   pl.BlockSpec((TILE, D_MLA), lambda i: (i, 0))],
        out_shape=[jax.ShapeDtypeStruct((T_ALL, ATT_W), BF16),
                   jax.ShapeDtypeStruct((T_ALL, ATT_W), BF16),
                   jax.ShapeDtypeStruct((T_ALL, D_MLA), BF16)],
        compiler_params=_cparams("arbitrary"),
        name="mla_prep",
    )(z, cs, qw.reshape(1, Q_RANK), kvw.reshape(1, KV_RANK), wqa, wqb, wk, wv)


def _attn_tile(q_ref, k_ref, v_ref, w_ref, o_ref, acc_ref, n_keys):
    scale2 = (D_NOPE + D_ROPE) ** -0.5 * math.log2(math.e)
    for h in range(H_A):
        q = q_ref[:, h * 2 * LANES:(h + 1) * 2 * LANES]
        s = lax.dot_general(q, k_ref[0:n_keys, h * 2 * LANES:(h + 1) * 2 * LANES], _NT, preferred_element_type=F32)
        m = jnp.max(s, axis=-1, keepdims=True)
        p = jnp.exp2((s - m) * scale2)
        l = jnp.sum(p, axis=-1, keepdims=True)
        o = jnp.dot(p.astype(BF16), v_ref[0:n_keys, h * D_V:(h + 1) * D_V], preferred_element_type=F32)
        acc_ref[:, h * D_V:(h + 1) * D_V] = o / l
    o_ref[...] = (_rms(acc_ref[...]) * w_ref[...]).astype(o_ref.dtype)


def _attn_kernel(q_ref, k_ref, v_ref, w_ref, o_ref, acc_ref):
    @pl.when(pl.program_id(1) == 0)
    def _():
        _attn_tile(q_ref, k_ref, v_ref, w_ref, o_ref, acc_ref, CTX_LEN)

    @pl.when(pl.program_id(1) != 0)
    def _():
        _attn_tile(q_ref, k_ref, v_ref, w_ref, o_ref, acc_ref, PB)


def _attention(q, k, v, onw):
    return pl.pallas_call(
        _attn_kernel,
        grid=(BATCH, TPB),
        in_specs=[pl.BlockSpec((TILE, ATT_W), lambda b, i: (b * TPB + i, 0)),
                  pl.BlockSpec((None, PB, ATT_W), lambda b, i: (b, 0, 0)),
                  pl.BlockSpec((None, PB, D_MLA), lambda b, i: (b, 0, 0)),
                  pl.BlockSpec((1, D_MLA), lambda b, i: (0, 0))],
        out_specs=pl.BlockSpec((TILE, D_MLA), lambda b, i: (b * TPB + i, 0)),
        out_shape=jax.ShapeDtypeStruct((T_ALL, D_MLA), BF16),
        scratch_shapes=[pltpu.VMEM((TILE, D_MLA), F32)],
        compiler_params=_cparams("arbitrary", "arbitrary"),
        name="attention",
    )(q, k.reshape(BATCH, PB, ATT_W), v.reshape(BATCH, PB, D_MLA), onw)


R_E0, R_E1, R_W0, R_W1, R_K0, R_K1 = range(6)
NO_LANE = 2 * LANES


def _prenorm_router_kernel(x_ref, w_ref, mod_ref, wr_ref, tri_ref, h_ref, route_ref, route_t_ref, cnt_ref, cnt_s,
                           *, sh_idx, sc_idx):
    @pl.when(pl.program_id(0) == 0)
    def _():
        cnt_s[...] = jnp.zeros_like(cnt_s)

    y = _rms(x_ref[...], w_ref[...])
    m = mod_ref[0]
    y = y * (1.0 + m[sc_idx:sc_idx + 1, :]) + m[sh_idx:sh_idx + 1, :]
    h_ref[...] = y
    y_hi = y.astype(BF16)
    y_lo = (y - y_hi.astype(F32)).astype(BF16)
    lg = (jnp.dot(y_hi, wr_ref[0], preferred_element_type=F32) + jnp.dot(y_lo, wr_ref[0], preferred_element_type=F32)
          + jnp.dot(y_hi, wr_ref[1], preferred_element_type=F32))
    lane = lax.broadcasted_iota(jnp.int32, lg.shape, 1)

    def first_max(v):
        top = jnp.max(v, axis=1, keepdims=True)
        return top, jnp.min(jnp.where(v == top, lane, NO_LANE), axis=1, keepdims=True)

    is_g = lane < N_GROUPS
    g_top, g_idx = first_max(jnp.where(is_g, lg, NEG_BIG))
    g_w = 1.0 / jnp.sum(jnp.where(is_g, jnp.exp(lg - g_top), 0.0), axis=1, keepdims=True)
    lo = N_GROUPS + EXPERTS_PER_GROUP * g_idx
    el = jnp.where(jnp.logical_and(lane >= lo, lane < lo + EXPERTS_PER_GROUP), lg, NEG_BIG)
    v1, i1 = first_max(el)
    v2, i2 = first_max(jnp.where(lane == i1, NEG_BIG, el))
    t = jnp.exp(v2 - v1)
    w0 = g_w / (1.0 + t)
    w1 = g_w * t / (1.0 + t)
    e0 = i1 - N_GROUPS
    e1 = i2 - N_GROUPS
    hit0 = lane == e0
    hit1 = lane == e1
    onehot = jnp.logical_or(hit0, hit1).astype(F32)
    before = cnt_s[0:1, :] + jnp.dot(tri_ref[...], onehot.astype(BF16), preferred_element_type=F32)
    k0 = jnp.sum(jnp.where(hit0, before, 0.0), axis=1, keepdims=True)
    k1 = jnp.sum(jnp.where(hit1, before, 0.0), axis=1, keepdims=True)
    cnt_s[0:1, :] = cnt_s[0:1, :] + jnp.sum(onehot, axis=0, keepdims=True)
    cnt_ref[...] = cnt_s[...]
    rec = jnp.zeros(lg.shape, F32)
    for ln, val in ((R_E0, e0.astype(F32)), (R_E1, e1.astype(F32)), (R_W0, w0), (R_W1, w1), (R_K0, k0), (R_K1, k1)):
        rec = jnp.where(lane == ln, val, rec)
    route_ref[...] = rec
    route_t_ref[...] = rec.T[0:8, :]


def _prenorm_router(x, w, modtab, w_router, sh_idx, sc_idx, latent_only):
    n_tok = T_X if latent_only else T_ALL
    r = jnp.arange(TILE)
    tri = (r[None, :] < r[:, None]).astype(BF16)
    src = lambda i: _token_tile(i, latent_only)
    return pl.pallas_call(
        functools.partial(_prenorm_router_kernel, sh_idx=sh_idx, sc_idx=sc_idx),
        grid=(n_tok // TILE,),
        in_specs=[pl.BlockSpec((TILE, D_MODEL), lambda i: (src(i), 0)),
                  pl.BlockSpec((1, D_MODEL), lambda i: (0, 0)),
                  pl.BlockSpec((1, 6, D_MODEL), lambda i: (_tile_mod_row(src(i)), 0, 0)),
                  pl.BlockSpec((2, D_MODEL, ROUTER_COLS), lambda i: (0, 0, 0)),
                  pl.BlockSpec((TILE, TILE), lambda i: (0, 0))],
        out_specs=[pl.BlockSpec((TILE, D_MODEL), lambda i: (i, 0)),
                   pl.BlockSpec((TILE, ROUTER_COLS), lambda i: (i, 0)),
                   pl.BlockSpec((8, TILE), lambda i: (0, i)),
                   pl.BlockSpec((8, ROUTER_COLS), lambda i: (0, 0))],
        out_shape=[jax.ShapeDtypeStruct((n_tok, D_MODEL), F32),
                   jax.ShapeDtypeStruct((n_tok, ROUTER_COLS), F32),
                   jax.ShapeDtypeStruct((8, n_tok), F32),
                   jax.ShapeDtypeStruct((8, ROUTER_COLS), F32)],
        scratch_shapes=[pltpu.VMEM((8, ROUTER_COLS), F32)],
        compiler_params=_cparams("arbitrary"),
        name="prenorm_router",
    )(x, w.reshape(1, D_MODEL), modtab, w_router, tri)


def _row_copy(src, src_row, dst, dst_row, sem):
    return pltpu.make_async_copy(src.at[pl.ds(src_row, 1)], dst.at[pl.ds(dst_row, 1)], sem)


def _dispatch_kernel(dest_ref, h_ref, xb_ref, sem):
    def issue(r, carry):
        for k in range(TOP_K):
            _row_copy(h_ref, r, xb_ref, dest_ref[0, k, r], sem).start(priority=k)
        return carry

    lax.fori_loop(0, TILE, issue, 0, unroll=8)

    def drain(r, carry):
        for k in range(TOP_K):
            _row_copy(h_ref, 0, xb_ref, 0, sem).wait()
        return carry

    lax.fori_loop(0, TILE, drain, 0, unroll=8)


def _dispatch(h, dest3):
    n_tok = h.shape[0]
    return pl.pallas_call(
        _dispatch_kernel,
        grid=(n_tok // TILE,),
        in_specs=[pl.BlockSpec((1, TOP_K, TILE), lambda i: (i, 0, 0), memory_space=pltpu.SMEM),
                  pl.BlockSpec((TILE, D_MODEL), lambda i: (i, 0))],
        out_specs=pl.BlockSpec(memory_space=pl.ANY),
        out_shape=jax.ShapeDtypeStruct((n_tok * TOP_K, D_MODEL), F32),
        scratch_shapes=[pltpu.SemaphoreType.DMA(())],
        compiler_params=_cparams("arbitrary"),
        name="moe_dispatch",
    )(dest3, h)


def _moe_kernel(tile_ref, exp_ref, lo_ref, hi_ref, flag_ref, next_ref, x_ref, wg_hbm, wu_hbm, wd_hbm, o_ref,
                stage_g, stage_u, stage_d, wg_s, wu_s, wd_s, sem, *, layer):
    i = pl.program_id(0)
    flags = flag_ref[i]

    def fetch(e):
        return (pltpu.make_async_copy(wg_hbm.at[layer, e], stage_g, sem.at[0]),
                pltpu.make_async_copy(wu_hbm.at[layer, e], stage_u, sem.at[1]),
                pltpu.make_async_copy(wd_hbm.at[layer, e], stage_d, sem.at[2]))

    @pl.when(i == 0)
    def _():
        for copy in fetch(exp_ref[0]):
            copy.start()

    @pl.when(flags % 2 == 1)
    def _():
        for copy in fetch(exp_ref[i]):
            copy.wait()
        wg_s[...] = stage_g[...].astype(BF16)
        wu_s[...] = stage_u[...].astype(BF16)
        wd_s[...] = stage_d[...].astype(BF16)

        @pl.when(next_ref[i] >= 0)
        def _():
            for copy in fetch(next_ref[i]):
                copy.start()

    @pl.when(flags >= 4)
    def _():
        x = x_ref[...].astype(BF16)
        g = jnp.dot(x, wg_s[...], preferred_element_type=F32)
        u = jnp.dot(x, wu_s[...], preferred_element_type=F32)
        row = lax.broadcasted_iota(jnp.int32, g.shape, 0)
        mine = jnp.logical_and(row >= lo_ref[i], row < hi_ref[i])
        h = jnp.where(mine, g * jax.nn.sigmoid(g) * u, 0.0).astype(BF16)
        res = jnp.dot(h, wd_s[...], preferred_element_type=F32)

        @pl.when((flags // 2) % 2 == 1)
        def _():
            o_ref[...] = res

        @pl.when((flags // 2) % 2 == 0)
        def _():
            o_ref[...] += res


def _moe_experts(xb, meta, wg, wu, wd, l):
    n_items = meta[0].shape[0]
    xmap = lambda i, ti, ex, lo, hi, fl, nx: (ti[i], 0)
    hbm = pl.BlockSpec(memory_space=pl.ANY)
    grid_spec = pltpu.PrefetchScalarGridSpec(
        num_scalar_prefetch=6,
        grid=(n_items,),
        in_specs=[pl.BlockSpec((MOE_ROWS, D_MODEL), xmap), hbm, hbm, hbm],
        out_specs=pl.BlockSpec((MOE_ROWS, D_MODEL), xmap),
        scratch_shapes=[pltpu.VMEM((D_MODEL, D_EXPERT), F32),
                        pltpu.VMEM((D_MODEL, D_EXPERT), F32),
                        pltpu.VMEM((D_EXPERT, D_MODEL), F32),
                        pltpu.VMEM((D_MODEL, D_EXPERT), BF16),
                        pltpu.VMEM((D_MODEL, D_EXPERT), BF16),
                        pltpu.VMEM((D_EXPERT, D_MODEL), BF16),
                        pltpu.SemaphoreType.DMA((3,))])
    return pl.pallas_call(
        functools.partial(_moe_kernel, layer=l),
        grid_spec=grid_spec,
        out_shape=jax.ShapeDtypeStruct(xb.shape, F32),
        compiler_params=_cparams("arbitrary"),
        name="moe_experts",
    )(*meta, xb, wg, wu, wd)


def _combine_kernel(dest_ref, next_ref, yb_ref, x_ref, route_ref, mod_ref, nw_ref, nmod_ref, *rest, final):
    if final:
        o_ref, buf, sem = rest
    else:
        o_ref, hx_ref, buf, sem = rest
    i = pl.program_id(0)
    slot = i % 2
    n_groups = TILE // COMBINE_ROWS

    def issue(d_ref, to_slot, j):
        for rr in range(COMBINE_ROWS):
            r = j * COMBINE_ROWS + rr
            for k in range(TOP_K):
                _row_copy(yb_ref, d_ref[0, k, r], buf.at[to_slot, k], r, sem.at[to_slot]).start(priority=k)

    @pl.when(i == 0)
    def _():
        def first(j, carry):
            issue(dest_ref, 0, j)
            return carry

        lax.fori_loop(0, n_groups, first, 0)

    def drain(r, carry):
        for k in range(TOP_K):
            _row_copy(yb_ref, 0, buf.at[slot, k], 0, sem.at[slot]).wait()
        return carry

    lax.fori_loop(0, TILE, drain, 0, unroll=8)
    g2 = mod_ref[0][5:6, :]
    nw = nw_ref[...]
    nm = nmod_ref[0]

    def combine(j):
        rows = pl.ds(pl.multiple_of(j * COMBINE_ROWS, COMBINE_ROWS), COMBINE_ROWS)
        rt = route_ref[rows, :]
        f = buf[slot, 0, rows, :] * rt[:, R_W0:R_W0 + 1] + buf[slot, 1, rows, :] * rt[:, R_W1:R_W1 + 1]
        y = x_ref[rows, :] + g2 * f
        if final:
            o_ref[rows, :] = _rms(y, nw)
        else:
            o_ref[rows, :] = y
            hx_ref[rows, :] = (_rms(y, nw) * (1.0 + nm[1:2, :]) + nm[0:1, :]).astype(hx_ref.dtype)

    @pl.when(i + 1 < pl.num_programs(0))
    def _():
        def body(j, carry):
            combine(j)
            issue(next_ref, 1 - slot, j)
            return carry

        lax.fori_loop(0, n_groups, body, 0)

    @pl.when(i + 1 >= pl.num_programs(0))
    def _():
        def body(j, carry):
            combine(j)
            return carry

        lax.fori_loop(0, n_groups, body, 0)


COMBINE_ROWS = 64


def _combine(yb, dest3, x, route, modtab, norm_w, next_modtab, latent_only):
    n_tok = route.shape[0]
    n_tiles = n_tok // TILE
    src = lambda i: _token_tile(i, latent_only)
    row = lambda i: (i, 0)
    mod_spec = pl.BlockSpec((1, 6, D_MODEL), lambda i: (_tile_mod_row(src(i)), 0, 0))
    out_specs = [pl.BlockSpec((TILE, D_MODEL), row)]
    out_shape = [jax.ShapeDtypeStruct((n_tok, D_MODEL), F32)]
    if not latent_only:
        out_specs.append(pl.BlockSpec((TILE, D_MODEL), row))
        out_shape.append(jax.ShapeDtypeStruct((n_tok, D_MODEL), BF16))
    return pl.pallas_call(
        functools.partial(_combine_kernel, final=latent_only),
        grid=(n_tiles,),
        in_specs=[pl.BlockSpec((1, TOP_K, TILE), lambda i: (i, 0, 0), memory_space=pltpu.SMEM),
                  pl.BlockSpec((1, TOP_K, TILE), lambda i: (jnp.minimum(i + 1, n_tiles - 1), 0, 0),
                               memory_space=pltpu.SMEM),
                  pl.BlockSpec(memory_space=pl.ANY),
                  pl.BlockSpec((TILE, D_MODEL), lambda i: (src(i), 0)),
                  pl.BlockSpec((TILE, ROUTER_COLS), row),
                  mod_spec,
                  pl.BlockSpec((1, D_MODEL), lambda i: (0, 0)),
                  mod_spec],
        out_specs=out_specs,
        out_shape=out_shape,
        scratch_shapes=[pltpu.VMEM((2, TOP_K, TILE, D_MODEL), F32), pltpu.SemaphoreType.DMA((2,))],
        compiler_params=_cparams("arbitrary"),
        name="moe_combine",
    )(dest3, dest3, yb, x, route, modtab, norm_w.reshape(1, D_MODEL), next_modtab)


def _moe_plan(route_t, cnt):
    t = route_t.shape[1]
    n_tiles = t * TOP_K // MOE_ROWS
    n_items = n_tiles + N_EXPERTS - 1
    experts = route_t[R_E0:R_E1 + 1].astype(jnp.int32)
    rank = route_t[R_K0:R_K1 + 1].astype(jnp.int32)
    counts = cnt[0, :N_EXPERTS].astype(jnp.int32)
    ends = jnp.cumsum(counts)
    starts = ends - counts
    e_axis = jnp.arange(N_EXPERTS, dtype=jnp.int32)[:, None, None]
    dest = jnp.sum(jnp.where(experts[None] == e_axis, starts[:, None, None], 0), axis=0) + rank
    t_first = starts // MOE_ROWS
    per_e = jnp.where(counts > 0, (ends - 1) // MOE_ROWS - t_first + 1, 0)
    item_end = jnp.cumsum(per_e)
    item_start = item_end - per_e
    total = item_end[-1]
    w = jnp.arange(n_items, dtype=jnp.int32)
    valid = w < total
    wc = jnp.minimum(w, total - 1)
    ex = jnp.minimum(jnp.sum((item_end[None, :] <= wc[:, None]).astype(jnp.int32), axis=1), N_EXPERTS - 1)
    e_ids = jnp.arange(N_EXPERTS, dtype=jnp.int32)
    onehot = ex[:, None] == e_ids[None, :]
    pick = lambda table: jnp.sum(jnp.where(onehot, table[None, :], 0), axis=1)
    tile = (pick(t_first) + wc - pick(item_start)).astype(jnp.int32)
    lo = jnp.where(valid, jnp.maximum(pick(starts), tile * MOE_ROWS) - tile * MOE_ROWS, 0).astype(jnp.int32)
    hi = jnp.where(valid, jnp.minimum(pick(ends), (tile + 1) * MOE_ROWS) - tile * MOE_ROWS, 0).astype(jnp.int32)
    one = jnp.ones((1,), bool)
    new_e = jnp.concatenate([one, ex[1:] != ex[:-1]])
    new_t = jnp.concatenate([one, tile[1:] != tile[:-1]])
    flags = (new_e.astype(jnp.int32) + 2 * new_t.astype(jnp.int32) + 4 * valid.astype(jnp.int32))
    later = jnp.where(jnp.logical_and(counts[None, :] > 0, e_ids[None, :] > e_ids[:, None]), e_ids[None, :], N_EXPERTS)
    next_of = pick(jnp.min(later, axis=1))
    nxt = jnp.where(next_of < N_EXPERTS, next_of, -1).astype(jnp.int32)
    dest3 = dest.reshape(TOP_K, t // TILE, TILE).transpose(1, 0, 2)
    return dest3, (tile, ex, lo, hi, flags, nxt)


def _moe(x, h, route, route_t, cnt, modtab, wg, wu, wd, l, norm_w, next_modtab, latent_only):
    dest3, meta = _moe_plan(route_t, cnt)
    xb = _dispatch(h, dest3)
    yb = _moe_experts(xb, meta, wg, wu, wd, l)
    res = _combine(yb, dest3, x, route, modtab, norm_w, next_modtab, latent_only)
    return res[0] if latent_only else (res[0], res[1])


def _scan_chunk(s, rev, chunk):
    n_c = CTX_LEN // chunk
    n_all = PB // chunk
    if not rev:
        return s
    return jnp.where(s < n_c, n_c - 1 - s, n_all + n_c - 1 - s)


HALO = 8


def _ssd_prep_kernel(cur_ref, prev_ref, next_ref, dt_ref, cw_ref, cb_ref, dtb_ref, xo_ref, dto_ref, ext_s):
    j = pl.program_id(0) % TPB
    first = jnp.logical_or(j == 0, j == 1)
    last = jnp.logical_or(j == 0, j == TPB - 1)
    ext_s[0:HALO, :] = jnp.where(first, 0.0, prev_ref[...])
    ext_s[HALO:HALO + TILE, :] = cur_ref[...]
    ext_s[HALO + TILE:, :] = jnp.where(last, 0.0, next_ref[...])
    half = (SSD_CONV - 1) // 2
    acc = cb_ref[...] + cw_ref[0:1, :] * ext_s[HALO - half:HALO - half + TILE, :]
    for k in range(1, SSD_CONV):
        acc = acc + cw_ref[k:k + 1, :] * ext_s[HALO - half + k:HALO - half + k + TILE, :]
    xo_ref[...] = acc * jax.nn.sigmoid(acc)
    lane = lax.broadcasted_iota(jnp.int32, (TILE, LANES), 1)
    dto_ref[...] = jnp.where(lane < 2 * H_B, jax.nn.softplus(dt_ref[...] + dtb_ref[...]), 0.0)


def _ssd_prep(z, conv_w, conv_b, dt_bias):
    n_tiles = T_ALL // TILE
    per = TILE // HALO
    cwp = jnp.concatenate([conv_w, jnp.zeros((8 - SSD_CONV, CONV_CH), F32)], axis=0)
    dtb = jnp.concatenate([dt_bias.reshape(1, 2 * H_B), jnp.zeros((1, LANES - 2 * H_B), F32)], axis=1)
    xc = Z_XBC // CONV_CH
    return pl.pallas_call(
        _ssd_prep_kernel,
        grid=(n_tiles,),
        in_specs=[pl.BlockSpec((TILE, CONV_CH), lambda i: (i, xc)),
                  pl.BlockSpec((HALO, CONV_CH), lambda i: (jnp.maximum(i * per - 1, 0), xc)),
                  pl.BlockSpec((HALO, CONV_CH), lambda i: (jnp.minimum((i + 1) * per, T_ALL // HALO - 1), xc)),
                  pl.BlockSpec((TILE, LANES), lambda i: (i, Z_DT // LANES)),
                  pl.BlockSpec((8, CONV_CH), lambda i: (0, 0)),
                  pl.BlockSpec((1, CONV_CH), lambda i: (0, 0)),
                  pl.BlockSpec((1, LANES), lambda i: (0, 0))],
        out_specs=[pl.BlockSpec((TILE, CONV_CH), lambda i: (i, 0)),
                   pl.BlockSpec((TILE, LANES), lambda i: (i, 0))],
        out_shape=[jax.ShapeDtypeStruct((T_ALL, CONV_CH), F32),
                   jax.ShapeDtypeStruct((T_ALL, LANES), F32)],
        scratch_shapes=[pltpu.VMEM((TILE + 2 * HALO, CONV_CH), F32)],
        compiler_params=_cparams("arbitrary"),
        name="ssd_prep",
    )(z, z, z, z, cwp, conv_b.reshape(1, CONV_CH), dtb)


def _ssd_one_direction(xbc, dtp, arow, tri, expand, h_ref, y_ref, b, d, rev):
    q = SSD_CHUNK
    a = dtp * arow
    acum = _dot3_left(tri, a)
    yield
    acum_t = acum.T
    dt_t = dtp.T
    yield
    edge = 0 if rev else q - 1
    atot = acum[edge:edge + 1, :]
    pieces = jnp.concatenate([jnp.exp(atot - acum) * dtp, jnp.exp(acum),
                              jnp.broadcast_to(jnp.exp(atot), (8, LANES))], axis=0)
    ex = _dot3_right(pieces, expand)
    yield
    wend_x = ex[0:q]
    eacum_x = ex[q:2 * q]
    dec_x = ex[2 * q:2 * q + 1]
    xs = xbc[:, 0:D_SSD]
    xw = (xs * wend_x).astype(BF16)
    xs_b = xs.astype(BF16)
    h_old = h_ref[b]
    h_b = h_old.astype(BF16)
    ri = lax.broadcasted_iota(jnp.int32, (q, q), 0)
    ci = lax.broadcasted_iota(jnp.int32, (q, q), 1)
    mask = (ci >= ri) if rev else (ci <= ri)
    lo_half = lax.broadcasted_iota(jnp.int32, (q, LANES), 1) < P_B
    hpg = H_B // G_B
    gw = hpg * P_B
    yield

    def group(g):
        bg = xbc[:, D_SSD + g * N_B:D_SSD + (g + 1) * N_B].astype(BF16)
        cg = xbc[:, D_SSD + G_B * N_B + g * N_B:D_SSD + G_B * N_B + (g + 1) * N_B].astype(BF16)
        cb = lax.dot_general(cg, bg, _NT, preferred_element_type=F32)
        inter = jnp.dot(cg, h_b[:, g * gw:(g + 1) * gw], preferred_element_type=F32)
        upd = lax.dot_general(bg, xw[:, g * gw:(g + 1) * gw], _TN, preferred_element_type=F32)
        yield

        def pair(j):
            ms = []
            for hh in range(2):
                hc = H_B * d + hpg * g + 2 * j + hh
                seg = acum[:, hc:hc + 1] - acum_t[hc:hc + 1, :]
                dec = jnp.exp(jnp.where(mask, seg, NEG_BIG))
                ms.append((cb * dec * dt_t[hc:hc + 1, :]).astype(BF16))
                yield
            c0 = g * gw + 2 * j * P_B
            xp = xs_b[:, c0:c0 + LANES]
            zero = jnp.zeros_like(xp)
            rhs = jnp.concatenate([jnp.where(lo_half, xp, zero), jnp.where(lo_half, zero, xp)], axis=0)
            y_intra = jnp.dot(jnp.concatenate(ms, axis=1), rhs, preferred_element_type=F32)
            yield
            y_ref[b, :, c0:c0 + LANES] = (
                y_intra + eacum_x[:, c0:c0 + LANES] * inter[:, 2 * j * P_B:2 * j * P_B + LANES])
            yield

        yield from _in_turn([pair(j) for j in range(hpg // 2)])
        h_ref[b, :, g * gw:(g + 1) * gw] = dec_x[:, g * gw:(g + 1) * gw] * h_old[:, g * gw:(g + 1) * gw] + upd
        yield

    yield from _in_turn([group(g) for g in range(G_B)])


def _ssd_scan_kernel(xf_ref, dtf_ref, xb_ref, dtb_ref, arow_ref, trif_ref, trib_ref, ef_ref, eb_ref,
                     yf_ref, yb_ref, hf_s, hb_s):
    @pl.when(pl.program_id(0) == 0)
    def _():
        hf_s[...] = jnp.zeros_like(hf_s)
        hb_s[...] = jnp.zeros_like(hb_s)

    chains = []
    for b in range(BATCH):
        chains.append(_ssd_one_direction(xf_ref[b], dtf_ref[b], arow_ref[...], trif_ref[...], ef_ref[...],
                                         hf_s, yf_ref, b, 0, False))
        chains.append(_ssd_one_direction(xb_ref[b], dtb_ref[b], arow_ref[...], trib_ref[...], eb_ref[...],
                                         hb_s, yb_ref, b, 1, True))
    _round_robin(chains)


def _ssd_scan(xact, dtp, a_log):
    q = SSD_CHUNK
    steps = PB // q
    a_neg = -jnp.exp(a_log)
    arow = jnp.concatenate([a_neg.reshape(1, 2 * H_B), jnp.zeros((1, LANES - 2 * H_B), F32)], axis=1)
    r = jnp.arange(q)
    tri_f = (r[None, :] <= r[:, None]).astype(BF16)
    tri_b = (r[None, :] >= r[:, None]).astype(BF16)
    col_head = jnp.arange(D_SSD) // P_B
    lane = jnp.arange(LANES)
    exp_f = (lane[:, None] == col_head[None, :]).astype(BF16)
    exp_b = (lane[:, None] == col_head[None, :] + H_B).astype(BF16)
    fwd = lambda s: (0, _scan_chunk(s, False, q), 0)
    bwd = lambda s: (0, _scan_chunk(s, True, q), 0)
    const = lambda s: (0, 0)
    x3 = xact.reshape(BATCH, PB, CONV_CH)
    d3 = dtp.reshape(BATCH, PB, LANES)
    yf, yb = pl.pallas_call(
        _ssd_scan_kernel,
        grid=(steps,),
        in_specs=[pl.BlockSpec((BATCH, q, CONV_CH), fwd), pl.BlockSpec((BATCH, q, LANES), fwd),
                  pl.BlockSpec((BATCH, q, CONV_CH), bwd), pl.BlockSpec((BATCH, q, LANES), bwd),
                  pl.BlockSpec((1, LANES), const),
                  pl.BlockSpec((q, q), const), pl.BlockSpec((q, q), const),
                  pl.BlockSpec((LANES, D_SSD), const), pl.BlockSpec((LANES, D_SSD), const)],
        out_specs=[pl.BlockSpec((BATCH, q, D_SSD), fwd), pl.BlockSpec((BATCH, q, D_SSD), bwd)],
        out_shape=[jax.ShapeDtypeStruct((BATCH, PB, D_SSD), F32), jax.ShapeDtypeStruct((BATCH, PB, D_SSD), F32)],
        scratch_shapes=[pltpu.VMEM((BATCH, N_B, D_SSD), F32), pltpu.VMEM((BATCH, N_B, D_SSD), F32)],
        compiler_params=_cparams("arbitrary"),
        name="ssd_scan",
    )(x3, d3, x3, d3, arow, tri_f, tri_b, exp_f, exp_b)
    return yf.reshape(T_ALL, D_SSD), yb.reshape(T_ALL, D_SSD)


def _mlstm_one_direction(q, k, v, gi, gf, bi, bf, tri, st_ref, m_ref, h_ref, b, d, rev):
    n = MLSTM_CHUNK
    li = gi + bi
    lf = jax.nn.log_sigmoid(gf + bf)
    yield
    bc = _dot3_left(tri, lf)
    yield
    b_t = bc.T
    li_t = li.T
    yield
    edge = 0 if rev else n - 1
    gtot = bc[edge:edge + 1, :]
    m_old = m_ref[b, 0:1, :]
    w_log = gtot - bc + li
    m_new = jnp.maximum(gtot + m_old, jnp.max(w_log, axis=0, keepdims=True))
    wj = jnp.exp(w_log - m_new)
    dec = jnp.exp(gtot + m_old - m_new)
    inter_log = bc + m_old
    m_ref[b, 0:1, :] = m_new
    yield
    ri = lax.broadcasted_iota(jnp.int32, (n, n), 0)
    ci = lax.broadcasted_iota(jnp.int32, (n, n), 1)
    mask = (ci >= ri) if rev else (ci <= ri)
    lo_half = lax.broadcasted_iota(jnp.int32, (n, LANES), 1) < DQK_C
    row_lo = lax.broadcasted_iota(jnp.int32, (2 * DQK_C, 2 * DV_C), 0) < DQK_C
    ones = jnp.ones((n, DV_C), F32)
    st_old = [st_ref[b, j] for j in range(H_C // 2)]
    upds = {}

    def head(h):
        j, hh = divmod(h, 2)
        gl = GATE_LANE0 + H_C * d + h
        qp = q[:, j * LANES:(j + 1) * LANES] * DQK_C ** -0.5
        kp = k[:, j * LANES:(j + 1) * LANES].astype(BF16)
        qm = jnp.where(lo_half if hh == 0 else jnp.logical_not(lo_half), qp, 0.0).astype(BF16)
        qk = lax.dot_general(qm, kp, _NT, preferred_element_type=F32)
        qs = jnp.dot(qm, st_old[j].astype(BF16), preferred_element_type=F32)
        yield
        dmat = jnp.where(mask, bc[:, gl:gl + 1] - b_t[gl:gl + 1, :] + li_t[gl:gl + 1, :], NEG_BIG)
        il = inter_log[:, gl:gl + 1]
        m_row = jnp.maximum(il, jnp.max(dmat, axis=1, keepdims=True))
        yield
        s = qk * jnp.exp(dmat - m_row)
        w_inter = jnp.exp(il - m_row)
        vh = v[:, h * DV_C:(h + 1) * DV_C]
        yield
        num = jnp.dot(s.astype(BF16), vh.astype(BF16), preferred_element_type=F32) + w_inter * qs[:, :DV_C]
        den = jnp.sum(s, axis=1, keepdims=True) + w_inter * qs[:, DV_C:]
        yield
        h_ref[b, :, h * DV_C:(h + 1) * DV_C] = num / jnp.maximum(jnp.abs(den), jnp.exp(-m_row))
        rhs = (wj[:, gl:gl + 1] * jnp.concatenate([vh, ones], axis=1)).astype(BF16)
        upds[h] = lax.dot_general(kp, rhs, _TN, preferred_element_type=F32)
        yield

    yield from _in_turn([head(h) for h in range(H_C)])
    for j in range(H_C // 2):
        ga = GATE_LANE0 + H_C * d + 2 * j
        decv = jnp.where(row_lo, dec[:, ga:ga + 1], dec[:, ga + 1:ga + 2])
        st_ref[b, j] = decv * st_old[j] + jnp.where(row_lo, upds[2 * j], upds[2 * j + 1])
        yield


def _mlstm_scan_kernel(qf, kf, vf, gif, gff, qb, kb, vb, gib, gfb, bi_ref, bf_ref, trif_ref, trib_ref,
                       hf_ref, hb_ref, stf_s, stb_s, mf_s, mb_s):
    @pl.when(pl.program_id(0) == 0)
    def _():
        stf_s[...] = jnp.zeros_like(stf_s)
        stb_s[...] = jnp.zeros_like(stb_s)
        mf_s[...] = jnp.full_like(mf_s, NEG_STATE)
        mb_s[...] = jnp.full_like(mb_s, NEG_STATE)

    chains = []
    for b in range(BATCH):
        chains.append(_mlstm_one_direction(qf[b], kf[b], vf[b], gif[b], gff[b], bi_ref[...], bf_ref[...],
                                           trif_ref[...], stf_s, mf_s, hf_ref, b, 0, False))
        chains.append(_mlstm_one_direction(qb[b], kb[b], vb[b], gib[b], gfb[b], bi_ref[...], bf_ref[...],
                                           trib_ref[...], stb_s, mb_s, hb_ref, b, 1, True))
    _round_robin(chains)


def _mlstm_scan(z, gate_b):
    n = MLSTM_CHUNK
    steps = PB // n
    qkw = H_C * DQK_C
    pad = lambda t: jnp.concatenate([jnp.zeros((1, GATE_LANE0), F32), t.reshape(1, 2 * H_C),
                                     jnp.zeros((1, LANES - GATE_LANE0 - 2 * H_C), F32)], axis=1)
    bi = pad(gate_b[:, 0, :])
    bf = pad(gate_b[:, 1, :])
    r = jnp.arange(n)
    tri_f = (r[None, :] <= r[:, None]).astype(BF16)
    tri_b = (r[None, :] >= r[:, None]).astype(BF16)
    z3 = z.reshape(BATCH, PB, Z_COLS)

    def specs(rev):
        ch = lambda s: _scan_chunk(s, rev, n)
        return [pl.BlockSpec((BATCH, n, qkw), lambda s: (0, ch(s), Z_MQ // qkw)),
                pl.BlockSpec((BATCH, n, qkw), lambda s: (0, ch(s), Z_MK // qkw)),
                pl.BlockSpec((BATCH, n, D_MLSTM), lambda s: (0, ch(s), Z_MV // D_MLSTM)),
                pl.BlockSpec((BATCH, n, LANES), lambda s: (0, ch(s), Z_MG // LANES)),
                pl.BlockSpec((BATCH, n, LANES), lambda s: (0, ch(s), Z_DT // LANES))]

    const = lambda s: (0, 0)
    n_pairs = H_C // 2
    hf, hb = pl.pallas_call(
        _mlstm_scan_kernel,
        grid=(steps,),
        in_specs=specs(False) + specs(True) + [
            pl.BlockSpec((1, LANES), const), pl.BlockSpec((1, LANES), const),
            pl.BlockSpec((n, n), const), pl.BlockSpec((n, n), const)],
        out_specs=[pl.BlockSpec((BATCH, n, D_MLSTM), lambda s: (0, _scan_chunk(s, False, n), 0)),
                   pl.BlockSpec((BATCH, n, D_MLSTM), lambda s: (0, _scan_chunk(s, True, n), 0))],
        out_shape=[jax.ShapeDtypeStruct((BATCH, PB, D_MLSTM), F32), jax.ShapeDtypeStruct((BATCH, PB, D_MLSTM), F32)],
        scratch_shapes=[pltpu.VMEM((BATCH, n_pairs, 2 * DQK_C, 2 * DV_C), F32),
                        pltpu.VMEM((BATCH, n_pairs, 2 * DQK_C, 2 * DV_C), F32),
                        pltpu.VMEM((BATCH, 8, LANES), F32), pltpu.VMEM((BATCH, 8, LANES), F32)],
        compiler_params=_cparams("arbitrary"),
        name="mlstm_scan",
    )(z3, z3, z3, z3, z3, z3, z3, z3, z3, z3, bi, bf, tri_f, tri_b)
    return hf.reshape(T_ALL, D_MLSTM), hb.reshape(T_ALL, D_MLSTM)


S5_Q = 256
S5_SEG = S5_Q // 8
S5_LANES = G_S5 * P_S5
S5_SLAB = 512
S5_NSLAB = S5_LANES // S5_SLAB


def _s5_dir_kernel(u_ref, perm_ref, wbr_ref, wbi_ref, atab_ref, apr_ref, api_ref, wcr_ref, wci_ref, y_ref,
                   ur_s, ui_s, xr_s, xi_s, st_s, car_s, *, reverse):
    @pl.when(pl.program_id(0) == 0)
    def _():
        st_s[...] = jnp.zeros_like(st_s)

    for b in range(BATCH):
        up = jnp.dot(perm_ref[...], u_ref[b].astype(BF16), preferred_element_type=F32).astype(BF16)
        for m in range(S5_NSLAB):
            um = up[:, m * LANES:(m + 1) * LANES]
            ur_s[b, :, m * S5_SLAB:(m + 1) * S5_SLAB] = jnp.dot(um, wbr_ref[m], preferred_element_type=F32)
            ui_s[b, :, m * S5_SLAB:(m + 1) * S5_SLAB] = jnp.dot(um, wbi_ref[m], preferred_element_type=F32)

    per = 4
    for grp in range(S5_LANES // (per * LANES)):
        cols = [grp * per * LANES + j * LANES for j in range(per)]
        a_r = [atab_ref[0:8, c0:c0 + LANES] for c0 in cols]
        a_i = [atab_ref[8:16, c0:c0 + LANES] for c0 in cols]

        def body(i, carry, cols=cols, a_r=a_r, a_i=a_i):
            t = (S5_SEG - 1 - i) if reverse else i
            r0 = pl.multiple_of(t * 8, 8)
            new = []
            for b in range(BATCH):
                for j, c0 in enumerate(cols):
                    xr, xi = carry[2 * (b * per + j)], carry[2 * (b * per + j) + 1]
                    nr = a_r[j] * xr - a_i[j] * xi + ur_s[b, pl.ds(r0, 8), c0:c0 + LANES]
                    ni = a_r[j] * xi + a_i[j] * xr + ui_s[b, pl.ds(r0, 8), c0:c0 + LANES]
                    ur_s[b, pl.ds(r0, 8), c0:c0 + LANES] = nr
                    ui_s[b, pl.ds(r0, 8), c0:c0 + LANES] = ni
                    new += [nr, ni]
            return tuple(new)

        lax.fori_loop(0, S5_SEG, body, tuple(jnp.zeros((8, LANES), F32) for _ in range(2 * per * BATCH)), unroll=2)

    as_r = atab_ref[16:17, :]
    as_i = atab_ref[17:18, :]
    end_row = 0 if reverse else 8 * (S5_SEG - 1)
    for b in range(BATCH):
        cr = st_s[b, 0:1, :]
        ci = st_s[b, 1:2, :]
        for k in (range(7, -1, -1) if reverse else range(8)):
            car_s[b, k:k + 1, :] = cr
            car_s[b, 8 + k:9 + k, :] = ci
            er = ur_s[b, end_row + k:end_row + k + 1, :]
            ei = ui_s[b, end_row + k:end_row + k + 1, :]
            cr, ci = er + as_r * cr - as_i * ci, ei + as_r * ci + as_i * cr
        st_s[b, 0:1, :] = cr
        st_s[b, 1:2, :] = ci

    for b in range(BATCH):
        for m in range(S5_NSLAB):
            cs = slice(m * S5_SLAB, (m + 1) * S5_SLAB)
            c_r = jnp.concatenate([car_s[b, 0:8, cs], car_s[b, 0:8, cs]], axis=0)
            c_i = jnp.concatenate([car_s[b, 8:16, cs], car_s[b, 8:16, cs]], axis=0)

            def fix(i, _, b=b, cs=cs, c_r=c_r, c_i=c_i):
                r0 = pl.multiple_of(i * 16, 16)
                p_r = apr_ref[pl.ds(r0, 16), cs]
                p_i = api_ref[pl.ds(r0, 16), cs]
                xr_s[b, pl.ds(r0, 16), cs] = (ur_s[b, pl.ds(r0, 16), cs] + p_r * c_r - p_i * c_i).astype(BF16)
                xi_s[b, pl.ds(r0, 16), cs] = (ui_s[b, pl.ds(r0, 16), cs] + p_r * c_i + p_i * c_r).astype(BF16)
                return 0

            lax.fori_loop(0, S5_Q // 16, fix, 0, unroll=2)

    for b in range(BATCH):
        for m in range(S5_NSLAB):
            cs = slice(m * S5_SLAB, (m + 1) * S5_SLAB)
            y_ref[b, :, m * LANES:(m + 1) * LANES] = (
                jnp.dot(xr_s[b, :, cs], wcr_ref[m], preferred_element_type=F32)
                - jnp.dot(xi_s[b, :, cs], wci_ref[m], preferred_element_type=F32))


def _s5_direction(z3, perm, wbr, wbi, atab, apr, api, wcr, wci, reverse):
    steps = PB // S5_Q
    chunk = lambda s: _scan_chunk(s, reverse, S5_Q)
    const2 = lambda s: (0, 0)
    const3 = lambda s: (0, 0, 0)
    return pl.pallas_call(
        functools.partial(_s5_dir_kernel, reverse=reverse),
        grid=(steps,),
        in_specs=[pl.BlockSpec((BATCH, S5_Q, D_S5), lambda s: (0, chunk(s), Z_D // D_S5)),
                  pl.BlockSpec((S5_Q, S5_Q), const2),
                  pl.BlockSpec(wbr.shape, const3),
                  pl.BlockSpec(wbi.shape, const3),
                  pl.BlockSpec(atab.shape, const2),
                  pl.BlockSpec(apr.shape, const2),
                  pl.BlockSpec(api.shape, const2),
                  pl.BlockSpec(wcr.shape, const3),
                  pl.BlockSpec(wci.shape, const3)],
        out_specs=pl.BlockSpec((BATCH, S5_Q, D_S5), lambda s: (0, chunk(s), 0)),
        out_shape=jax.ShapeDtypeStruct((BATCH, PB, D_S5), F32),
        scratch_shapes=[pltpu.VMEM((BATCH, S5_Q, S5_LANES), F32), pltpu.VMEM((BATCH, S5_Q, S5_LANES), F32),
                        pltpu.VMEM((BATCH, S5_Q, S5_LANES), BF16), pltpu.VMEM((BATCH, S5_Q, S5_LANES), BF16),
                        pltpu.VMEM((BATCH, 8, S5_LANES), F32), pltpu.VMEM((BATCH, 16, S5_LANES), F32)],
        compiler_params=_cparams("arbitrary"),
        name="s5_bwd" if reverse else "s5_fwd",
    )(z3, perm, wbr, wbi, atab, apr, api, wcr, wci).reshape(T_ALL, D_S5)


def _s5_tables(lam_re, lam_im, log_dt, b_re, b_im, c_re, c_im):
    dt = jnp.exp(log_dt)[..., None]
    mag = jnp.exp(lam_re * dt)
    ar = mag * jnp.cos(lam_im * dt)
    ai = mag * jnp.sin(lam_im * dt)
    den = lam_re * lam_re + lam_im * lam_im
    cr_ = ((ar - 1.0) * lam_re + ai * lam_im) / den
    ci_ = (ai * lam_re - (ar - 1.0) * lam_im) / den
    bbr = cr_[..., None] * b_re - ci_[..., None] * b_im
    bbi = cr_[..., None] * b_im + ci_[..., None] * b_re
    gps = S5_SLAB // P_S5
    eye = jnp.eye(gps, dtype=F32)

    def drive_w(bb):
        t = bb.reshape(S5_NSLAB, gps, P_S5, S5_GROUP)
        w = jnp.einsum('mgpc,gh->mgchp', t, eye)
        return w.reshape(S5_NSLAB, gps * S5_GROUP, gps * P_S5).astype(BF16)

    def read_w(cc):
        t = cc.reshape(S5_NSLAB, gps, S5_GROUP, P_S5)
        w = jnp.einsum('mgcp,gh->mgphc', t, eye)
        return w.reshape(S5_NSLAB, gps * P_S5, gps * S5_GROUP).astype(BF16)

    steps = jnp.arange(1, S5_SEG + 1, dtype=F32)[:, None]
    out = []
    for d in range(2):
        decay = (lam_re[d] * dt[d]).reshape(1, S5_LANES)
        angle = (lam_im[d] * dt[d]).reshape(1, S5_LANES)
        pmag = jnp.exp(steps * decay)
        pr = pmag * jnp.cos(steps * angle)
        pi = pmag * jnp.sin(steps * angle)
        if d == 1:
            apr, api = jnp.repeat(pr[::-1], 8, axis=0), jnp.repeat(pi[::-1], 8, axis=0)
        else:
            apr, api = jnp.repeat(pr, 8, axis=0), jnp.repeat(pi, 8, axis=0)
        atab = jnp.concatenate([jnp.broadcast_to(pr[0:1], (8, S5_LANES)), jnp.broadcast_to(pi[0:1], (8, S5_LANES)),
                                pr[S5_SEG - 1:], pi[S5_SEG - 1:], jnp.zeros((6, S5_LANES), F32)], axis=0)
        out.append((drive_w(bbr[d]), drive_w(bbi[d]), atab, apr, api))
    r = jnp.arange(S5_Q)
    src = (r % 8) * S5_SEG + r // 8
    perm = (src[:, None] == jnp.arange(S5_Q)[None, :]).astype(BF16)
    return out, read_w(c_re), read_w(c_im), perm


def _s5_scans(z, p):
    dirs, wcr, wci, perm = _s5_tables(p["s5_lam_re"], p["s5_lam_im"], p["s5_log_dt"], p["s5_b_re"], p["s5_b_im"],
                                      p["s5_c_re"], p["s5_c_im"])
    z3 = z.reshape(BATCH, PB, Z_COLS)
    yf = _s5_direction(z3, perm, *dirs[0], wcr, wci, reverse=False)
    yb = _s5_direction(z3, perm, *dirs[1], wcr, wci, reverse=True)
    return yf, yb, perm.T


def _merge_kernel(ya_ref, sf_ref, sb_ref, sx_ref, sg_ref, mf_ref, mb_ref, mo_ref, df_ref, db_ref, du_ref,
                  permt_ref, sd_ref, dd_ref, gw_ref, gb_ref, onw_ref, w0, w1, w2, w3, x_ref, mod_ref,
                  o_ref, wb_s, *, g_idx):
    @pl.when(pl.program_id(0) == 0)
    def _():
        for r, w in enumerate((w0, w1, w2, w3)):
            wb_s[r] = w[...].astype(BF16)

    onw = onw_ref[...]
    out = {}

    def attn():
        out[0] = jnp.dot(ya_ref[...], wb_s[0], preferred_element_type=F32)
        yield

    def ssd():
        gate = sg_ref[...]
        y = (sf_ref[...] + sb_ref[...] + sd_ref[...] * sx_ref[...]) * (gate * jax.nn.sigmoid(gate))
        yield
        y = (_rms(y) * onw[:, D_MLA:D_MLA + D_SSD]).astype(BF16)
        yield
        out[1] = jnp.dot(y, wb_s[1], preferred_element_type=F32)
        yield

    def mlstm():
        gate = jax.nn.sigmoid(mo_ref[...])
        c0 = D_MLA + D_SSD
        parts = []
        for h in range(H_C):
            cs = slice(h * DV_C, (h + 1) * DV_C)
            hn = _rms(mf_ref[:, cs] + mb_ref[:, cs])
            parts.append((hn * gate[:, cs] * onw[:, c0 + h * DV_C:c0 + (h + 1) * DV_C]).astype(BF16))
            yield
        out[2] = jnp.dot(jnp.concatenate(parts, axis=1), wb_s[2], preferred_element_type=F32)
        yield

    def s5():
        y = _dot3_left(permt_ref[...], df_ref[...] + db_ref[...])
        yield
        y = jax.nn.gelu(y + dd_ref[...] * du_ref[...])
        yield
        gate = jax.nn.sigmoid(jnp.dot(y.astype(BF16), gw_ref[...], preferred_element_type=F32) + gb_ref[...])
        yield
        y = (_rms(y * gate) * onw[:, D_MLA + D_SSD + D_MLSTM:]).astype(BF16)
        yield
        out[3] = jnp.dot(y, wb_s[3], preferred_element_type=F32)
        yield

    _round_robin([attn(), ssd(), mlstm(), s5()])
    g = mod_ref[0][g_idx:g_idx + 1, :]
    o_ref[...] = x_ref[...] + g * (out[0] + out[1] + out[2] + out[3])


def _merge_proj(ya, ssd, mlstm, s5, z, p, w_out, l, x, modtab, g_idx):
    sf, sb, xact = ssd
    mf, mb = mlstm
    df, db, permt = s5
    kq = D_MODEL // 4
    row = lambda i: (i, 0)
    const = lambda i: (0, 0)
    grp = lambda col: pl.BlockSpec((TILE, kq), lambda i: (i, col // kq))
    w_specs = [pl.BlockSpec((None, kq, D_MODEL), functools.partial(lambda i, r: (l, r, 0), r=r),
                            pipeline_mode=pl.Buffered(1)) for r in range(4)]
    return pl.pallas_call(
        functools.partial(_merge_kernel, g_idx=g_idx),
        grid=(T_ALL // TILE,),
        in_specs=[pl.BlockSpec((TILE, kq), row),
                  pl.BlockSpec((TILE, kq), row), pl.BlockSpec((TILE, kq), row), grp(0), grp(Z_SZ),
                  pl.BlockSpec((TILE, kq), row), pl.BlockSpec((TILE, kq), row), grp(Z_MO),
                  pl.BlockSpec((TILE, kq), row), pl.BlockSpec((TILE, kq), row), grp(Z_D),
                  pl.BlockSpec((TILE, TILE), const),
                  pl.BlockSpec((1, kq), const), pl.BlockSpec((1, kq), const),
                  pl.BlockSpec((kq, kq), const), pl.BlockSpec((1, kq), const),
                  pl.BlockSpec((1, D_MODEL), const)] + w_specs + [
            pl.BlockSpec((TILE, D_MODEL), row),
            pl.BlockSpec((1, 6, D_MODEL), lambda i: (_tile_mod_row(i), 0, 0))],
        out_specs=pl.BlockSpec((TILE, D_MODEL), row),
        out_shape=jax.ShapeDtypeStruct((T_ALL, D_MODEL), F32),
        scratch_shapes=[pltpu.VMEM((4, kq, D_MODEL), BF16)],
        compiler_params=_cparams("arbitrary"),
        name="merge_proj",
    )(ya, sf, sb, xact, z, mf, mb, z, df, db, z, permt,
      jnp.repeat(p["ssd_d"], P_B).reshape(1, kq), p["s5_d"].reshape(1, kq),
      p["s5_glu_w"].astype(BF16), p["s5_glu_b"].reshape(1, kq), p["out_norm_w"].reshape(1, D_MODEL),
      w_out, w_out, w_out, w_out, x, modtab)


def _rope_tables():
    pos = np.arange(SEQ)
    row = (pos // GRID_W).astype(np.float32)
    col = (pos % GRID_W).astype(np.float32)
    inv_freq = (ROPE_BASE ** (-np.arange(ROPE_AXIS // 2, dtype=np.float32) * 2.0 / ROPE_AXIS)).astype(np.float32)
    ang_r = row[:, None] * inv_freq
    ang_c = col[:, None] * inv_freq
    zeros = np.zeros((SEQ, LANES - D_ROPE), np.float32)
    cos = np.concatenate([np.cos(ang_r), np.cos(ang_r), np.cos(ang_c), np.cos(ang_c), zeros], axis=1)
    sin = np.concatenate([np.sin(ang_r), np.sin(ang_r), np.sin(ang_c), np.sin(ang_c), zeros], axis=1)
    cos_c = np.concatenate([np.ones((TILE, D_ROPE), np.float32), np.zeros((TILE, LANES - D_ROPE), np.float32)], axis=1)
    sin_c = np.zeros((TILE, LANES), np.float32)
    table = np.concatenate([np.concatenate([cos, sin], axis=1), np.concatenate([cos_c, sin_c], axis=1)], axis=0)
    return jnp.asarray(table.astype(np.float32))


def _layout_mla(w_uq, w_ukv):
    k = w_uq.shape[0]
    qa, qb, wk, wv = [], [], [], []
    for h in range(H_A):
        base = h * (D_NOPE + D_ROPE)
        rope = w_uq[:, base + D_NOPE:base + D_NOPE + D_ROPE]
        qa += [w_uq[:, base:base + D_NOPE], rope, jnp.zeros((k, LANES - D_ROPE), w_uq.dtype)]
        qb += [_rot_cols(rope), jnp.zeros((k, LANES - D_ROPE), w_uq.dtype)]
        kb = h * (D_NOPE + D_V)
        wk.append(w_ukv[:, kb:kb + D_NOPE])
        wv.append(w_ukv[:, kb + D_NOPE:kb + D_NOPE + D_V])
    cat = lambda xs: jnp.concatenate(xs, axis=1).astype(BF16)
    return cat(qa), cat(qb), cat(wk), cat(wv)


def _layer(xall, hx, modtab, p, big, l, cs, norm_w, next_modtab, last):
    z = _mm(hx, _layout_w_in(big["w_in"], l), MM_ROWS, Z_COLS // 3)
    onw = p["out_norm_w"]
    wqa, wqb, wk, wv = _layout_mla(p["mla_w_uq"], p["mla_w_ukv"])
    q, k, v = _mla_prep(z, cs, p["mla_q_norm_w"], p["mla_kv_norm_w"], wqa, wqb, wk, wv)
    ya = _attention(q, k, v, onw[:D_MLA].reshape(1, D_MLA))
    xact, dtp = _ssd_prep(z, p["ssd_conv_w"], p["ssd_conv_b"], p["ssd_dt_bias"])
    ssd = _ssd_scan(xact, dtp, p["ssd_a_log"]) + (xact,)
    mlstm = _mlstm_scan(z, p["mlstm_gate_b"])
    s5 = _s5_scans(z, p)
    xall = _merge_proj(ya, ssd, mlstm, s5, z, p, big["w_out"], l, xall, modtab, 2)

    w_router = jnp.concatenate([p["moe_w_group"], p["moe_w_expert"],
                                jnp.zeros((D_MODEL, ROUTER_COLS - N_GROUPS - N_EXPERTS), F32)], axis=1)
    wr_hi = w_router.astype(BF16)
    wr_lo = (w_router - wr_hi.astype(F32)).astype(BF16)
    h2, route, route_t, cnt = _prenorm_router(xall, p["norm2_w"], modtab, jnp.stack([wr_hi, wr_lo]), 3, 4, last)
    return _moe(xall, h2, route, route_t, cnt, modtab, big["moe_w_gate"], big["moe_w_up"], big["moe_w_down"], l,
                norm_w, next_modtab, last)


def kernel(x, c, ctx, c_ctx, mod_w, mod_b, norm1_w, w_in, mla_q_norm_w, mla_kv_norm_w, mla_w_uq, mla_w_ukv,
           ssd_conv_w, ssd_conv_b, ssd_a_log, ssd_dt_bias, ssd_d, mlstm_gate_b, s5_lam_re, s5_lam_im,
           s5_log_dt, s5_b_re, s5_b_im, s5_c_re, s5_c_im, s5_d, s5_glu_w, s5_glu_b, out_norm_w, w_out,
           norm2_w, moe_w_group, moe_w_expert, moe_w_gate, moe_w_up, moe_w_down, final_norm_w):
    stacked = {"norm1_w": norm1_w, "mla_q_norm_w": mla_q_norm_w, "mla_kv_norm_w": mla_kv_norm_w,
               "mla_w_uq": mla_w_uq, "mla_w_ukv": mla_w_ukv, "ssd_conv_w": ssd_conv_w, "ssd_conv_b": ssd_conv_b,
               "ssd_a_log": ssd_a_log, "ssd_dt_bias": ssd_dt_bias, "ssd_d": ssd_d, "mlstm_gate_b": mlstm_gate_b,
               "s5_lam_re": s5_lam_re, "s5_lam_im": s5_lam_im, "s5_log_dt": s5_log_dt, "s5_b_re": s5_b_re,
               "s5_b_im": s5_b_im, "s5_c_re": s5_c_re, "s5_c_im": s5_c_im, "s5_d": s5_d, "s5_glu_w": s5_glu_w,
               "s5_glu_b": s5_glu_b, "out_norm_w": out_norm_w, "norm2_w": norm2_w,
               "moe_w_group": moe_w_group, "moe_w_expert": moe_w_expert}
    big = {"w_in": w_in, "w_out": w_out, "moe_w_gate": moe_w_gate, "moe_w_up": moe_w_up, "moe_w_down": moe_w_down}
    cs = _rope_tables()
    cc = jnp.concatenate([c, c_ctx[None, :], jnp.zeros((8 - BATCH - 1, D_MODEL), F32)], axis=0)
    modtabs = [_modulation(cc, mod_w, mod_b, l)[:BATCH + 1].reshape(BATCH + 1, 6, D_MODEL) for l in range(DEPTH)]
    xall, hx = _prenorm(x, ctx, norm1_w[0], modtabs[0], 0, 1)
    for l in range(DEPTH):
        p = {name: val[l] for name, val in stacked.items()}
        if l == DEPTH - 1:
            out = _layer(xall, hx, modtabs[l], p, big, l, cs, final_norm_w, modtabs[l], True)
        else:
            xall, hx = _layer(xall, hx, modtabs[l], p, big, l, cs, norm1_w[l + 1], modtabs[l + 1], False)
    return out.reshape(BATCH, SEQ, D_MODEL)
```

```python
import functools
import math

import jax
import jax.numpy as jnp
import numpy as np
from jax import lax
from jax.experimental import pallas as pl
from jax.experimental.pallas import tpu as pltpu

F32 = jnp.float32
BF16 = jnp.bfloat16

D_MODEL = 2048
BATCH = 2
SEQ = 4096
DEPTH = 2
GRID_W = 64
CTX_LEN = 256
EPS = 1e-6
NEG_STATE = -1e30
NEG_BIG = -1e30

H_A = 4
D_NOPE = 128
D_ROPE = 64
D_V = 128
Q_RANK = 384
KV_RANK = 128
ROPE_AXIS = D_ROPE // 2
ROPE_BASE = 10000.0
D_MLA = H_A * D_V
D_SSD = 512
P_B = 64
H_B = D_SSD // P_B
G_B = 2
N_B = 128
SSD_CONV = 5
SSD_CHUNK = 128
CONV_CH = D_SSD + 2 * G_B * N_B
D_MLSTM = 512
H_C = 4
DV_C = D_MLSTM // H_C
DQK_C = DV_C // 2
MLSTM_CHUNK = 128
D_S5 = 512
S5_GROUP = 16
G_S5 = D_S5 // S5_GROUP
P_S5 = 64
A_COLS = Q_RANK + KV_RANK + D_ROPE
B_COLS = D_SSD + CONV_CH + 2 * H_B
C_COLS = 2 * H_C * DQK_C + 2 * D_MLSTM + 4 * H_C
N_GROUPS = 4
EXPERTS_PER_GROUP = 8
N_EXPERTS = N_GROUPS * EXPERTS_PER_GROUP
TOP_K = 2
D_EXPERT = 512

PB = CTX_LEN + SEQ
T_X = BATCH * SEQ
T_ALL = BATCH * PB

LANES = 128
VMEM_LIMIT_BYTES = 56 * 1024 * 1024

TILE = 256
TPB = PB // TILE
XT = SEQ // TILE
MM_ROWS = T_ALL // 8
MM_COLS = 2304

Z_CQ = 0
Z_CKV = 384
Z_KR = 512
Z_KRR = 640
Z_MQ = 768
Z_XBC = 1024
Z_SZ = 2048
Z_MV = 2560
Z_MO = 3072
Z_D = 3584
Z_MK = 4096
Z_DT = 4352
Z_MG = 4480
Z_COLS = 4608
GATE_LANE0 = 2 * H_B

MOE_ROWS = 256
ROUTER_COLS = 128

_NT = (((1,), (1,)), ((), ()))
_TN = (((0,), (0,)), ((), ()))


def _cparams(*sem):
    return pltpu.CompilerParams(dimension_semantics=sem, vmem_limit_bytes=VMEM_LIMIT_BYTES)


def _tile_mod_row(i):
    return jnp.where(i % TPB == 0, BATCH, i // TPB)


def _token_tile(i, latent_only):
    return (i // XT) * TPB + 1 + i % XT if latent_only else i


def _rms(x, w=None):
    y = x * lax.rsqrt(jnp.mean(x * x, axis=-1, keepdims=True) + EPS)
    return y if w is None else y * w


def _in_turn(chains):
    live = list(chains)
    while live:
        nxt = []
        for chain in live:
            try:
                next(chain)
                nxt.append(chain)
            except StopIteration:
                pass
        live = nxt
        yield


def _round_robin(chains):
    for _ in _in_turn(chains):
        pass


def _split3(x):
    hi = x.astype(BF16)
    r1 = x - hi.astype(F32)
    mid = r1.astype(BF16)
    lo = (r1 - mid.astype(F32)).astype(BF16)
    return hi, mid, lo


def _dot3_left(sel, x):
    hi, mid, lo = _split3(x)
    return (jnp.dot(sel, hi, preferred_element_type=F32) + jnp.dot(sel, mid, preferred_element_type=F32)
            + jnp.dot(sel, lo, preferred_element_type=F32))


def _dot3_right(x, sel):
    hi, mid, lo = _split3(x)
    return (jnp.dot(hi, sel, preferred_element_type=F32) + jnp.dot(mid, sel, preferred_element_type=F32)
            + jnp.dot(lo, sel, preferred_element_type=F32))


def _mod_kernel(a_ref, w_ref, b_ref, o_ref):
    a = a_ref[...]
    a = a * jax.nn.sigmoid(a)
    o_ref[...] = jnp.dot(a.astype(BF16), w_ref[...].astype(BF16), preferred_element_type=F32) + b_ref[...]


def _modulation(cc, mod_w, mod_b, l):
    n = mod_w.shape[2]
    tn = 1024
    return pl.pallas_call(
        _mod_kernel,
        grid=(n // tn,),
        in_specs=[pl.BlockSpec((8, D_MODEL), lambda j: (0, 0)),
                  pl.BlockSpec((None, D_MODEL, tn), lambda j: (l, 0, j)),
                  pl.BlockSpec((None, 1, tn), lambda j: (l, 0, j))],
        out_specs=pl.BlockSpec((8, tn), lambda j: (0, j)),
        out_shape=jax.ShapeDtypeStruct((8, n), F32),
        compiler_params=_cparams("arbitrary"),
        name="modulation",
    )(cc, mod_w, mod_b.reshape(DEPTH, 1, n))


def _prenorm_kernel(x_ref, c_ref, w_ref, mod_ref, xall_ref, o_ref, *, sh_idx, sc_idx):
    v = jnp.where(pl.program_id(0) % TPB == 0, c_ref[...], x_ref[...])
    xall_ref[...] = v
    y = _rms(v, w_ref[...])
    m = mod_ref[0]
    y = y * (1.0 + m[sc_idx:sc_idx + 1, :]) + m[sh_idx:sh_idx + 1, :]
    o_ref[...] = y.astype(o_ref.dtype)


def _prenorm(x, ctx, w, modtab, sh_idx, sc_idx):
    row = lambda i: (i, 0)
    return pl.pallas_call(
        functools.partial(_prenorm_kernel, sh_idx=sh_idx, sc_idx=sc_idx),
        grid=(T_ALL // TILE,),
        in_specs=[pl.BlockSpec((TILE, D_MODEL), lambda i: ((i // TPB) * XT + jnp.maximum(i % TPB - 1, 0), 0)),
                  pl.BlockSpec((TILE, D_MODEL), lambda i: (i // TPB, 0)),
                  pl.BlockSpec((1, D_MODEL), lambda i: (0, 0)),
                  pl.BlockSpec((1, 6, D_MODEL), lambda i: (_tile_mod_row(i), 0, 0))],
        out_specs=[pl.BlockSpec((TILE, D_MODEL), row), pl.BlockSpec((TILE, D_MODEL), row)],
        out_shape=[jax.ShapeDtypeStruct((T_ALL, D_MODEL), F32), jax.ShapeDtypeStruct((T_ALL, D_MODEL), BF16)],
        compiler_params=_cparams("arbitrary"),
        name="prenorm",
    )(x.reshape(T_X, D_MODEL), ctx.reshape(BATCH * CTX_LEN, D_MODEL), w.reshape(1, D_MODEL), modtab)


def _mm_kernel(a_ref, w_ref, o_ref):
    o_ref[...] = jnp.dot(a_ref[...], w_ref[...], preferred_element_type=F32).astype(o_ref.dtype)


def _mm(a, w, tm, tn, out_dtype=F32):
    m, k = a.shape
    n = w.shape[1]
    return pl.pallas_call(
        _mm_kernel,
        grid=(n // tn, m // tm),
        in_specs=[pl.BlockSpec((tm, k), lambda j, i: (i, 0)),
                  pl.BlockSpec((k, tn), lambda j, i: (0, j))],
        out_specs=pl.BlockSpec((tm, tn), lambda j, i: (i, j)),
        out_shape=jax.ShapeDtypeStruct((m, n), out_dtype),
        compiler_params=_cparams("arbitrary", "arbitrary"),
        name="in_proj",
    )(a, w)


W_IN_B0 = A_COLS
W_IN_C0 = A_COLS + B_COLS
W_IN_D0 = A_COLS + B_COLS + C_COLS
W_IN_CM = W_IN_C0 + 2 * H_C * DQK_C
W_IN_GB = W_IN_CM + 2 * D_MLSTM
W_IN_WIDE = ((Z_CQ, 0, Q_RANK + KV_RANK),
             (Z_MQ, W_IN_C0, H_C * DQK_C),
             (Z_XBC, W_IN_B0 + D_SSD, CONV_CH),
             (Z_SZ, W_IN_B0, D_SSD),
             (Z_MV, W_IN_CM, 2 * D_MLSTM),
             (Z_D, W_IN_D0, D_S5),
             (Z_MK, W_IN_C0 + H_C * DQK_C, H_C * DQK_C))
W_IN_ROWS = 256


def _rot_cols(w):
    q = ROPE_AXIS // 2
    return jnp.concatenate([-w[:, q:2 * q], w[:, 0:q], -w[:, 3 * q:4 * q], w[:, 2 * q:3 * q]], axis=1)


W_IN_KR0 = Q_RANK + KV_RANK
W_IN_DT0 = W_IN_B0 + D_SSD + CONV_CH
W_IN_WINDOWS = tuple(c // LANES * LANES for c in (W_IN_KR0, W_IN_DT0, W_IN_GB))


def _narrow_selector():
    sel = np.zeros((3 * LANES, 4 * LANES), np.float32)

    def put(window, src_col, dst_col, width, sign=1.0):
        base = window * LANES + src_col - W_IN_WINDOWS[window]
        for c in range(width):
            sel[base + c, dst_col + c] = sign

    q = ROPE_AXIS // 2
    put(0, W_IN_KR0, 0, D_ROPE)
    for dst, src, sign in ((0, q, -1.0), (q, 0, 1.0), (2 * q, 3 * q, -1.0), (3 * q, 2 * q, 1.0)):
        put(0, W_IN_KR0 + src, LANES + dst, q, sign)
    put(1, W_IN_DT0, 2 * LANES, 2 * H_B)
    for d in range(2):
        put(2, W_IN_GB + (2 * d + 1) * H_C, 2 * LANES + GATE_LANE0 + d * H_C, H_C)
        put(2, W_IN_GB + 2 * d * H_C, 3 * LANES + GATE_LANE0 + d * H_C, H_C)
    return jnp.asarray(sel, dtype=BF16)


def _w_in_layout_kernel(w_ref, sel_ref, o_ref):
    for dst, src, width in W_IN_WIDE:
        o_ref[:, dst:dst + width] = w_ref[:, src:src + width].astype(BF16)
    windows = jnp.concatenate([w_ref[:, c:c + LANES] for c in W_IN_WINDOWS], axis=1).astype(BF16)
    small = jnp.dot(windows, sel_ref[...], preferred_element_type=F32).astype(BF16)
    o_ref[:, Z_KR:Z_MQ] = small[:, 0:2 * LANES]
    o_ref[:, Z_DT:Z_COLS] = small[:, 2 * LANES:4 * LANES]


def _layout_w_in(w_in, l):
    _, k, n = w_in.shape
    sel = _narrow_selector()
    return pl.pallas_call(
        _w_in_layout_kernel,
        grid=(k // W_IN_ROWS,),
        in_specs=[pl.BlockSpec((None, W_IN_ROWS, n), lambda i: (l, i, 0)),
                  pl.BlockSpec(sel.shape, lambda i: (0, 0))],
        out_specs=pl.BlockSpec((W_IN_ROWS, Z_COLS), lambda i: (i, 0)),
        out_shape=jax.ShapeDtypeStruct((k, Z_COLS), BF16),
        compiler_params=_cparams("arbitrary"),
        name="w_in_layout",
    )(w_in, sel)


def _mla_prep_kernel(za_ref, cs_ref, qw_ref, kvw_ref, wqa_ref, wqb_ref, wk_ref, wv_ref, q_ref, k_ref, v_ref):
    za = za_ref[...]
    cos = cs_ref[:, :LANES]
    sin = cs_ref[:, LANES:]
    qn = _rms(za[:, Z_CQ:Z_CQ + Q_RANK], qw_ref[...]).astype(BF16)
    kvn = _rms(za[:, Z_CKV:Z_CKV + KV_RANK], kvw_ref[...]).astype(BF16)
    qa = jnp.dot(qn, wqa_ref[...], preferred_element_type=F32)
    qb = jnp.dot(qn, wqb_ref[...], preferred_element_type=F32)
    kn = jnp.dot(kvn, wk_ref[...], preferred_element_type=F32)
    v = jnp.dot(kvn, wv_ref[...], preferred_element_type=F32)
    kr = (za[:, Z_KR:Z_KR + LANES] * cos + za[:, Z_KRR:Z_KRR + LANES] * sin).astype(BF16)
    for h in range(H_A):
        c0 = h * 2 * LANES
        q_ref[:, c0:c0 + LANES] = qa[:, c0:c0 + LANES].astype(BF16)
        q_ref[:, c0 + LANES:c0 + 2 * LANES] = (
            qa[:, c0 + LANES:c0 + 2 * LANES] * cos + qb[:, h * LANES:(h + 1) * LANES] * sin).astype(BF16)
        k_ref[:, c0:c0 + LANES] = kn[:, h * LANES:(h + 1) * LANES].astype(BF16)
        k_ref[:, c0 + LANES:c0 + 2 * LANES] = kr
    v_ref[...] = v.astype(BF16)


ATT_W = H_A * 2 * LANES


def _mla_prep(z, cs, qw, kvw, wqa, wqb, wk, wv):
    const = lambda i: (0, 0)
    rope_blk = lambda i: (jnp.where(i % TPB == 0, XT, i % TPB - 1), 0)
    return pl.pallas_call(
        _mla_prep_kernel,
        grid=(T_ALL // TILE,),
        in_specs=[pl.BlockSpec((TILE, Z_MQ), lambda i: (i, 0)),
                  pl.BlockSpec((TILE, 2 * LANES), rope_blk),
                  pl.BlockSpec((1, Q_RANK), const),
                  pl.BlockSpec((1, KV_RANK), const),
                  pl.BlockSpec(wqa.shape, const),
                  pl.BlockSpec(wqb.shape, const),
                  pl.BlockSpec(wk.shape, const),
                  pl.BlockSpec(wv.shape, const)],
        out_specs=[pl.BlockSpec((TILE, ATT_W), lambda i: (i, 0)),
                   pl.BlockSpec((TILE, ATT_W), lambda i: (i, 0)),
                   pl.BlockSpec((TILE, D_MLA), lambda i: (i, 0))],
        out_shape=[jax.ShapeDtypeStruct((T_ALL, ATT_W), BF16),
                   jax.ShapeDtypeStruct((T_ALL, ATT_W), BF16),
                   jax.ShapeDtypeStruct((T_ALL, D_MLA), BF16)],
        compiler_params=_cparams("arbitrary"),
        name="mla_prep",
    )(z, cs, qw.reshape(1, Q_RANK), kvw.reshape(1, KV_RANK), wqa, wqb, wk, wv)


def _attn_tile(q_ref, k_ref, v_ref, w_ref, o_ref, acc_ref, n_keys):
    scale2 = (D_NOPE + D_ROPE) ** -0.5 * math.log2(math.e)
    for h in range(H_A):
        q = q_ref[:, h * 2 * LANES:(h + 1) * 2 * LANES]
        s = lax.dot_general(q, k_ref[0:n_keys, h * 2 * LANES:(h + 1) * 2 * LANES], _NT, preferred_element_type=F32)
        m = jnp.max(s, axis=-1, keepdims=True)
        p = jnp.exp2((s - m) * scale2)
        l = jnp.sum(p, axis=-1, keepdims=True)
        o = jnp.dot(p.astype(BF16), v_ref[0:n_keys, h * D_V:(h + 1) * D_V], preferred_element_type=F32)
        acc_ref[:, h * D_V:(h + 1) * D_V] = o / l
    o_ref[...] = (_rms(acc_ref[...]) * w_ref[...]).astype(o_ref.dtype)


def _attn_kernel(q_ref, k_ref, v_ref, w_ref, o_ref, acc_ref):
    @pl.when(pl.program_id(1) == 0)
    def _():
        _attn_tile(q_ref, k_ref, v_ref, w_ref, o_ref, acc_ref, CTX_LEN)

    @pl.when(pl.program_id(1) != 0)
    def _():
        _attn_tile(q_ref, k_ref, v_ref, w_ref, o_ref, acc_ref, PB)


def _attention(q, k, v, onw):
    return pl.pallas_call(
        _attn_kernel,
        grid=(BATCH, TPB),
        in_specs=[pl.BlockSpec((TILE, ATT_W), lambda b, i: (b * TPB + i, 0)),
                  pl.BlockSpec((None, PB, ATT_W), lambda b, i: (b, 0, 0)),
                  pl.BlockSpec((None, PB, D_MLA), lambda b, i: (b, 0, 0)),
                  pl.BlockSpec((1, D_MLA), lambda b, i: (0, 0))],
        out_specs=pl.BlockSpec((TILE, D_MLA), lambda b, i: (b * TPB + i, 0)),
        out_shape=jax.ShapeDtypeStruct((T_ALL, D_MLA), BF16),
        scratch_shapes=[pltpu.VMEM((TILE, D_MLA), F32)],
        compiler_params=_cparams("arbitrary", "arbitrary"),
        name="attention",
    )(q, k.reshape(BATCH, PB, ATT_W), v.reshape(BATCH, PB, D_MLA), onw)


R_E0, R_E1, R_W0, R_W1, R_K0, R_K1 = range(6)
NO_LANE = 2 * LANES


def _prenorm_router_kernel(x_ref, w_ref, mod_ref, wr_ref, tri_ref, h_ref, route_ref, route_t_ref, cnt_ref, cnt_s,
                           *, sh_idx, sc_idx):
    @pl.when(pl.program_id(0) == 0)
    def _():
        cnt_s[...] = jnp.zeros_like(cnt_s)

    y = _rms(x_ref[...], w_ref[...])
    m = mod_ref[0]
    y = y * (1.0 + m[sc_idx:sc_idx + 1, :]) + m[sh_idx:sh_idx + 1, :]
    h_ref[...] = y
    y_hi = y.astype(BF16)
    y_lo = (y - y_hi.astype(F32)).astype(BF16)
    lg = (jnp.dot(y_hi, wr_ref[0], preferred_element_type=F32) + jnp.dot(y_lo, wr_ref[0], preferred_element_type=F32)
          + jnp.dot(y_hi, wr_ref[1], preferred_element_type=F32))
    lane = lax.broadcasted_iota(jnp.int32, lg.shape, 1)

    def first_max(v):
        top = jnp.max(v, axis=1, keepdims=True)
        return top, jnp.min(jnp.where(v == top, lane, NO_LANE), axis=1, keepdims=True)

    is_g = lane < N_GROUPS
    g_top, g_idx = first_max(jnp.where(is_g, lg, NEG_BIG))
    g_w = 1.0 / jnp.sum(jnp.where(is_g, jnp.exp(lg - g_top), 0.0), axis=1, keepdims=True)
    lo = N_GROUPS + EXPERTS_PER_GROUP * g_idx
    el = jnp.where(jnp.logical_and(lane >= lo, lane < lo + EXPERTS_PER_GROUP), lg, NEG_BIG)
    v1, i1 = first_max(el)
    v2, i2 = first_max(jnp.where(lane == i1, NEG_BIG, el))
    t = jnp.exp(v2 - v1)
    w0 = g_w / (1.0 + t)
    w1 = g_w * t / (1.0 + t)
    e0 = i1 - N_GROUPS
    e1 = i2 - N_GROUPS
    hit0 = lane == e0
    hit1 = lane == e1
    onehot = jnp.logical_or(hit0, hit1).astype(F32)
    before = cnt_s[0:1, :] + jnp.dot(tri_ref[...], onehot.astype(BF16), preferred_element_type=F32)
    k0 = jnp.sum(jnp.where(hit0, before, 0.0), axis=1, keepdims=True)
    k1 = jnp.sum(jnp.where(hit1, before, 0.0), axis=1, keepdims=True)
    cnt_s[0:1, :] = cnt_s[0:1, :] + jnp.sum(onehot, axis=0, keepdims=True)
    cnt_ref[...] = cnt_s[...]
    rec = jnp.zeros(lg.shape, F32)
    for ln, val in ((R_E0, e0.astype(F32)), (R_E1, e1.astype(F32)), (R_W0, w0), (R_W1, w1), (R_K0, k0), (R_K1, k1)):
        rec = jnp.where(lane == ln, val, rec)
    route_ref[...] = rec
    route_t_ref[...] = rec.T[0:8, :]


def _prenorm_router(x, w, modtab, w_router, sh_idx, sc_idx, latent_only):
    n_tok = T_X if latent_only else T_ALL
    r = jnp.arange(TILE)
    tri = (r[None, :] < r[:, None]).astype(BF16)
    src = lambda i: _token_tile(i, latent_only)
    return pl.pallas_call(
        functools.partial(_prenorm_router_kernel, sh_idx=sh_idx, sc_idx=sc_idx),
        grid=(n_tok // TILE,),
        in_specs=[pl.BlockSpec((TILE, D_MODEL), lambda i: (src(i), 0)),
                  pl.BlockSpec((1, D_MODEL), lambda i: (0, 0)),
                  pl.BlockSpec((1, 6, D_MODEL), lambda i: (_tile_mod_row(src(i)), 0, 0)),
                  pl.BlockSpec((2, D_MODEL, ROUTER_COLS), lambda i: (0, 0, 0)),
                  pl.BlockSpec((TILE, TILE), lambda i: (0, 0))],
        out_specs=[pl.BlockSpec((TILE, D_MODEL), lambda i: (i, 0)),
                   pl.BlockSpec((TILE, ROUTER_COLS), lambda i: (i, 0)),
                   pl.BlockSpec((8, TILE), lambda i: (0, i)),
                   pl.BlockSpec((8, ROUTER_COLS), lambda i: (0, 0))],
        out_shape=[jax.ShapeDtypeStruct((n_tok, D_MODEL), F32),
                   jax.ShapeDtypeStruct((n_tok, ROUTER_COLS), F32),
                   jax.ShapeDtypeStruct((8, n_tok), F32),
                   jax.ShapeDtypeStruct((8, ROUTER_COLS), F32)],
        scratch_shapes=[pltpu.VMEM((8, ROUTER_COLS), F32)],
        compiler_params=_cparams("arbitrary"),
        name="prenorm_router",
    )(x, w.reshape(1, D_MODEL), modtab, w_router, tri)


def _row_copy(src, src_row, dst, dst_row, sem):
    return pltpu.make_async_copy(src.at[pl.ds(src_row, 1)], dst.at[pl.ds(dst_row, 1)], sem)


def _dispatch_kernel(dest_ref, h_ref, xb_ref, sem):
    def issue(r, carry):
        for k in range(TOP_K):
            _row_copy(h_ref, r, xb_ref, dest_ref[0, k, r], sem).start(priority=k)
        return carry

    lax.fori_loop(0, TILE, issue, 0, unroll=8)

    def drain(r, carry):
        for k in range(TOP_K):
            _row_copy(h_ref, 0, xb_ref, 0, sem).wait()
        return carry

    lax.fori_loop(0, TILE, drain, 0, unroll=8)


def _dispatch(h, dest3):
    n_tok = h.shape[0]
    return pl.pallas_call(
        _dispatch_kernel,
        grid=(n_tok // TILE,),
        in_specs=[pl.BlockSpec((1, TOP_K, TILE), lambda i: (i, 0, 0), memory_space=pltpu.SMEM),
                  pl.BlockSpec((TILE, D_MODEL), lambda i: (i, 0))],
        out_specs=pl.BlockSpec(memory_space=pl.ANY),
        out_shape=jax.ShapeDtypeStruct((n_tok * TOP_K, D_MODEL), F32),
        scratch_shapes=[pltpu.SemaphoreType.DMA(())],
        compiler_params=_cparams("arbitrary"),
        name="moe_dispatch",
    )(dest3, h)


def _moe_kernel(tile_ref, exp_ref, lo_ref, hi_ref, flag_ref, next_ref, x_ref, wg_hbm, wu_hbm, wd_hbm, o_ref,
                stage_g, stage_u, stage_d, wg_s, wu_s, wd_s, sem, *, layer):
    i = pl.program_id(0)
    flags = flag_ref[i]

    def fetch(e):
        return (pltpu.make_async_copy(wg_hbm.at[layer, e], stage_g, sem.at[0]),
                pltpu.make_async_copy(wu_hbm.at[layer, e], stage_u, sem.at[1]),
                pltpu.make_async_copy(wd_hbm.at[layer, e], stage_d, sem.at[2]))

    @pl.when(i == 0)
    def _():
        for copy in fetch(exp_ref[0]):
            copy.start()

    @pl.when(flags % 2 == 1)
    def _():
        for copy in fetch(exp_ref[i]):
            copy.wait()
        wg_s[...] = stage_g[...].astype(BF16)
        wu_s[...] = stage_u[...].astype(BF16)
        wd_s[...] = stage_d[...].astype(BF16)

        @pl.when(next_ref[i] >= 0)
        def _():
            for copy in fetch(next_ref[i]):
                copy.start()

    @pl.when(flags >= 4)
    def _():
        x = x_ref[...].astype(BF16)
        g = jnp.dot(x, wg_s[...], preferred_element_type=F32)
        u = jnp.dot(x, wu_s[...], preferred_element_type=F32)
        row = lax.broadcasted_iota(jnp.int32, g.shape, 0)
        mine = jnp.logical_and(row >= lo_ref[i], row < hi_ref[i])
        h = jnp.where(mine, g * jax.nn.sigmoid(g) * u, 0.0).astype(BF16)
        res = jnp.dot(h, wd_s[...], preferred_element_type=F32)

        @pl.when((flags // 2) % 2 == 1)
        def _():
            o_ref[...] = res

        @pl.when((flags // 2) % 2 == 0)
        def _():
            o_ref[...] += res


def _moe_experts(xb, meta, wg, wu, wd, l):
    n_items = meta[0].shape[0]
    xmap = lambda i, ti, ex, lo, hi, fl, nx: (ti[i], 0)
    hbm = pl.BlockSpec(memory_space=pl.ANY)
    grid_spec = pltpu.PrefetchScalarGridSpec(
        num_scalar_prefetch=6,
        grid=(n_items,),
        in_specs=[pl.BlockSpec((MOE_ROWS, D_MODEL), xmap), hbm, hbm, hbm],
        out_specs=pl.BlockSpec((MOE_ROWS, D_MODEL), xmap),
        scratch_shapes=[pltpu.VMEM((D_MODEL, D_EXPERT), F32),
                        pltpu.VMEM((D_MODEL, D_EXPERT), F32),
                        pltpu.VMEM((D_EXPERT, D_MODEL), F32),
                        pltpu.VMEM((D_MODEL, D_EXPERT), BF16),
                        pltpu.VMEM((D_MODEL, D_EXPERT), BF16),
                        pltpu.VMEM((D_EXPERT, D_MODEL), BF16),
                        pltpu.SemaphoreType.DMA((3,))])
    return pl.pallas_call(
        functools.partial(_moe_kernel, layer=l),
        grid_spec=grid_spec,
        out_shape=jax.ShapeDtypeStruct(xb.shape, F32),
        compiler_params=_cparams("arbitrary"),
        name="moe_experts",
    )(*meta, xb, wg, wu, wd)


def _combine_kernel(dest_ref, next_ref, yb_ref, x_ref, route_ref, mod_ref, nw_ref, nmod_ref, *rest, final):
    if final:
        o_ref, buf, sem = rest
    else:
        o_ref, hx_ref, buf, sem = rest
    i = pl.program_id(0)
    slot = i % 2
    n_groups = TILE // COMBINE_ROWS

    def issue(d_ref, to_slot, j):
        for rr in range(COMBINE_ROWS):
            r = j * COMBINE_ROWS + rr
            for k in range(TOP_K):
                _row_copy(yb_ref, d_ref[0, k, r], buf.at[to_slot, k], r, sem.at[to_slot]).start(priority=k)

    @pl.when(i == 0)
    def _():
        def first(j, carry):
            issue(dest_ref, 0, j)
            return carry

        lax.fori_loop(0, n_groups, first, 0)

    def drain(r, carry):
        for k in range(TOP_K):
            _row_copy(yb_ref, 0, buf.at[slot, k], 0, sem.at[slot]).wait()
        return carry

    lax.fori_loop(0, TILE, drain, 0, unroll=8)
    g2 = mod_ref[0][5:6, :]
    nw = nw_ref[...]
    nm = nmod_ref[0]

    def combine(j):
        rows = pl.ds(pl.multiple_of(j * COMBINE_ROWS, COMBINE_ROWS), COMBINE_ROWS)
        rt = route_ref[rows, :]
        f = buf[slot, 0, rows, :] * rt[:, R_W0:R_W0 + 1] + buf[slot, 1, rows, :] * rt[:, R_W1:R_W1 + 1]
        y = x_ref[rows, :] + g2 * f
        if final:
            o_ref[rows, :] = _rms(y, nw)
        else:
            o_ref[rows, :] = y
            hx_ref[rows, :] = (_rms(y, nw) * (1.0 + nm[1:2, :]) + nm[0:1, :]).astype(hx_ref.dtype)

    @pl.when(i + 1 < pl.num_programs(0))
    def _():
        def body(j, carry):
            combine(j)
            issue(next_ref, 1 - slot, j)
            return carry

        lax.fori_loop(0, n_groups, body, 0)

    @pl.when(i + 1 >= pl.num_programs(0))
    def _():
        def body(j, carry):
            combine(j)
            return carry

        lax.fori_loop(0, n_groups, body, 0)


COMBINE_ROWS = 64


def _combine(yb, dest3, x, route, modtab, norm_w, next_modtab, latent_only):
    n_tok = route.shape[0]
    n_tiles = n_tok // TILE
    src = lambda i: _token_tile(i, latent_only)
    row = lambda i: (i, 0)
    mod_spec = pl.BlockSpec((1, 6, D_MODEL), lambda i: (_tile_mod_row(src(i)), 0, 0))
    out_specs = [pl.BlockSpec((TILE, D_MODEL), row)]
    out_shape = [jax.ShapeDtypeStruct((n_tok, D_MODEL), F32)]
    if not latent_only:
        out_specs.append(pl.BlockSpec((TILE, D_MODEL), row))
        out_shape.append(jax.ShapeDtypeStruct((n_tok, D_MODEL), BF16))
    return pl.pallas_call(
        functools.partial(_combine_kernel, final=latent_only),
        grid=(n_tiles,),
        in_specs=[pl.BlockSpec((1, TOP_K, TILE), lambda i: (i, 0, 0), memory_space=pltpu.SMEM),
                  pl.BlockSpec((1, TOP_K, TILE), lambda i: (jnp.minimum(i + 1, n_tiles - 1), 0, 0),
                               memory_space=pltpu.SMEM),
                  pl.BlockSpec(memory_space=pl.ANY),
                  pl.BlockSpec((TILE, D_MODEL), lambda i: (src(i), 0)),
                  pl.BlockSpec((TILE, ROUTER_COLS), row),
                  mod_spec,
                  pl.BlockSpec((1, D_MODEL), lambda i: (0, 0)),
                  mod_spec],
        out_specs=out_specs,
        out_shape=out_shape,
        scratch_shapes=[pltpu.VMEM((2, TOP_K, TILE, D_MODEL), F32), pltpu.SemaphoreType.DMA((2,))],
        compiler_params=_cparams("arbitrary"),
        name="moe_combine",
    )(dest3, dest3, yb, x, route, modtab, norm_w.reshape(1, D_MODEL), next_modtab)


def _moe_plan(route_t, cnt):
    t = route_t.shape[1]
    n_tiles = t * TOP_K // MOE_ROWS
    n_items = n_tiles + N_EXPERTS - 1
    experts = route_t[R_E0:R_E1 + 1].astype(jnp.int32)
    rank = route_t[R_K0:R_K1 + 1].astype(jnp.int32)
    counts = cnt[0, :N_EXPERTS].astype(jnp.int32)
    ends = jnp.cumsum(counts)
    starts = ends - counts
    e_axis = jnp.arange(N_EXPERTS, dtype=jnp.int32)[:, None, None]
    dest = jnp.sum(jnp.where(experts[None] == e_axis, starts[:, None, None], 0), axis=0) + rank
    t_first = starts // MOE_ROWS
    per_e = jnp.where(counts > 0, (ends - 1) // MOE_ROWS - t_first + 1, 0)
    item_end = jnp.cumsum(per_e)
    item_start = item_end - per_e
    total = item_end[-1]
    w = jnp.arange(n_items, dtype=jnp.int32)
    valid = w < total
    wc = jnp.minimum(w, total - 1)
    ex = jnp.minimum(jnp.sum((item_end[None, :] <= wc[:, None]).astype(jnp.int32), axis=1), N_EXPERTS - 1)
    e_ids = jnp.arange(N_EXPERTS, dtype=jnp.int32)
    onehot = ex[:, None] == e_ids[None, :]
    pick = lambda table: jnp.sum(jnp.where(onehot, table[None, :], 0), axis=1)
    tile = (pick(t_first) + wc - pick(item_start)).astype(jnp.int32)
    lo = jnp.where(valid, jnp.maximum(pick(starts), tile * MOE_ROWS) - tile * MOE_ROWS, 0).astype(jnp.int32)
    hi = jnp.where(valid, jnp.minimum(pick(ends), (tile + 1) * MOE_ROWS) - tile * MOE_ROWS, 0).astype(jnp.int32)
    one = jnp.ones((1,), bool)
    new_e = jnp.concatenate([one, ex[1:] != ex[:-1]])
    new_t = jnp.concatenate([one, tile[1:] != tile[:-1]])
    flags = (new_e.astype(jnp.int32) + 2 * new_t.astype(jnp.int32) + 4 * valid.astype(jnp.int32))
    later = jnp.where(jnp.logical_and(counts[None, :] > 0, e_ids[None, :] > e_ids[:, None]), e_ids[None, :], N_EXPERTS)
    next_of = pick(jnp.min(later, axis=1))
    nxt = jnp.where(next_of < N_EXPERTS, next_of, -1).astype(jnp.int32)
    dest3 = dest.reshape(TOP_K, t // TILE, TILE).transpose(1, 0, 2)
    return dest3, (tile, ex, lo, hi, flags, nxt)


def _moe(x, h, route, route_t, cnt, modtab, wg, wu, wd, l, norm_w, next_modtab, latent_only):
    dest3, meta = _moe_plan(route_t, cnt)
    xb = _dispatch(h, dest3)
    yb = _moe_experts(xb, meta, wg, wu, wd, l)
    res = _combine(yb, dest3, x, route, modtab, norm_w, next_modtab, latent_only)
    return res[0] if latent_only else (res[0], res[1])


def _scan_chunk(s, rev, chunk):
    n_c = CTX_LEN // chunk
    n_all = PB // chunk
    if not rev:
        return s
    return jnp.where(s < n_c, n_c - 1 - s, n_all + n_c - 1 - s)


HALO = 8


def _ssd_prep_kernel(cur_ref, prev_ref, next_ref, dt_ref, cw_ref, cb_ref, dtb_ref, xo_ref, dto_ref, ext_s):
    j = pl.program_id(0) % TPB
    first = jnp.logical_or(j == 0, j == 1)
    last = jnp.logical_or(j == 0, j == TPB - 1)
    ext_s[0:HALO, :] = jnp.where(first, 0.0, prev_ref[...])
    ext_s[HALO:HALO + TILE, :] = cur_ref[...]
    ext_s[HALO + TILE:, :] = jnp.where(last, 0.0, next_ref[...])
    half = (SSD_CONV - 1) // 2
    acc = cb_ref[...] + cw_ref[0:1, :] * ext_s[HALO - half:HALO - half + TILE, :]
    for k in range(1, SSD_CONV):
        acc = acc + cw_ref[k:k + 1, :] * ext_s[HALO - half + k:HALO - half + k + TILE, :]
    xo_ref[...] = acc * jax.nn.sigmoid(acc)
    lane = lax.broadcasted_iota(jnp.int32, (TILE, LANES), 1)
    dto_ref[...] = jnp.where(lane < 2 * H_B, jax.nn.softplus(dt_ref[...] + dtb_ref[...]), 0.0)


def _ssd_prep(z, conv_w, conv_b, dt_bias):
    n_tiles = T_ALL // TILE
    per = TILE // HALO
    cwp = jnp.concatenate([conv_w, jnp.zeros((8 - SSD_CONV, CONV_CH), F32)], axis=0)
    dtb = jnp.concatenate([dt_bias.reshape(1, 2 * H_B), jnp.zeros((1, LANES - 2 * H_B), F32)], axis=1)
    xc = Z_XBC // CONV_CH
    return pl.pallas_call(
        _ssd_prep_kernel,
        grid=(n_tiles,),
        in_specs=[pl.BlockSpec((TILE, CONV_CH), lambda i: (i, xc)),
                  pl.BlockSpec((HALO, CONV_CH), lambda i: (jnp.maximum(i * per - 1, 0), xc)),
                  pl.BlockSpec((HALO, CONV_CH), lambda i: (jnp.minimum((i + 1) * per, T_ALL // HALO - 1), xc)),
                  pl.BlockSpec((TILE, LANES), lambda i: (i, Z_DT // LANES)),
                  pl.BlockSpec((8, CONV_CH), lambda i: (0, 0)),
                  pl.BlockSpec((1, CONV_CH), lambda i: (0, 0)),
                  pl.BlockSpec((1, LANES), lambda i: (0, 0))],
        out_specs=[pl.BlockSpec((TILE, CONV_CH), lambda i: (i, 0)),
                   pl.BlockSpec((TILE, LANES), lambda i: (i, 0))],
        out_shape=[jax.ShapeDtypeStruct((T_ALL, CONV_CH), F32),
                   jax.ShapeDtypeStruct((T_ALL, LANES), F32)],
        scratch_shapes=[pltpu.VMEM((TILE + 2 * HALO, CONV_CH), F32)],
        compiler_params=_cparams("arbitrary"),
        name="ssd_prep",
    )(z, z, z, z, cwp, conv_b.reshape(1, CONV_CH), dtb)


def _ssd_one_direction(xbc, dtp, arow, tri, expand, h_ref, y_ref, b, d, rev):
    q = SSD_CHUNK
    a = dtp * arow
    acum = _dot3_left(tri, a)
    yield
    acum_t = acum.T
    dt_t = dtp.T
    yield
    edge = 0 if rev else q - 1
    atot = acum[edge:edge + 1, :]
    pieces = jnp.concatenate([jnp.exp(atot - acum) * dtp, jnp.exp(acum),
                              jnp.broadcast_to(jnp.exp(atot), (8, LANES))], axis=0)
    ex = _dot3_right(pieces, expand)
    yield
    wend_x = ex[0:q]
    eacum_x = ex[q:2 * q]
    dec_x = ex[2 * q:2 * q + 1]
    xs = xbc[:, 0:D_SSD]
    xw = (xs * wend_x).astype(BF16)
    xs_b = xs.astype(BF16)
    h_old = h_ref[b]
    h_b = h_old.astype(BF16)
    ri = lax.broadcasted_iota(jnp.int32, (q, q), 0)
    ci = lax.broadcasted_iota(jnp.int32, (q, q), 1)
    mask = (ci >= ri) if rev else (ci <= ri)
    lo_half = lax.broadcasted_iota(jnp.int32, (q, LANES), 1) < P_B
    hpg = H_B // G_B
    gw = hpg * P_B
    yield

    def group(g):
        bg = xbc[:, D_SSD + g * N_B:D_SSD + (g + 1) * N_B].astype(BF16)
        cg = xbc[:, D_SSD + G_B * N_B + g * N_B:D_SSD + G_B * N_B + (g + 1) * N_B].astype(BF16)
        cb = lax.dot_general(cg, bg, _NT, preferred_element_type=F32)
        inter = jnp.dot(cg, h_b[:, g * gw:(g + 1) * gw], preferred_element_type=F32)
        upd = lax.dot_general(bg, xw[:, g * gw:(g + 1) * gw], _TN, preferred_element_type=F32)
        yield

        def pair(j):
            ms = []
            for hh in range(2):
                hc = H_B * d + hpg * g + 2 * j + hh
                seg = acum[:, hc:hc + 1] - acum_t[hc:hc + 1, :]
                dec = jnp.exp(jnp.where(mask, seg, NEG_BIG))
                ms.append((cb * dec * dt_t[hc:hc + 1, :]).astype(BF16))
                yield
            c0 = g * gw + 2 * j * P_B
            xp = xs_b[:, c0:c0 + LANES]
            zero = jnp.zeros_like(xp)
            rhs = jnp.concatenate([jnp.where(lo_half, xp, zero), jnp.where(lo_half, zero, xp)], axis=0)
            y_intra = jnp.dot(jnp.concatenate(ms, axis=1), rhs, preferred_element_type=F32)
            yield
            y_ref[b, :, c0:c0 + LANES] = (
                y_intra + eacum_x[:, c0:c0 + LANES] * inter[:, 2 * j * P_B:2 * j * P_B + LANES])
            yield

        yield from _in_turn([pair(j) for j in range(hpg // 2)])
        h_ref[b, :, g * gw:(g + 1) * gw] = dec_x[:, g * gw:(g + 1) * gw] * h_old[:, g * gw:(g + 1) * gw] + upd
        yield

    yield from _in_turn([group(g) for g in range(G_B)])


def _ssd_scan_kernel(xf_ref, dtf_ref, xb_ref, dtb_ref, arow_ref, trif_ref, trib_ref, ef_ref, eb_ref,
                     yf_ref, yb_ref, hf_s, hb_s):
    @pl.when(pl.program_id(0) == 0)
    def _():
        hf_s[...] = jnp.zeros_like(hf_s)
        hb_s[...] = jnp.zeros_like(hb_s)

    chains = []
    for b in range(BATCH):
        chains.append(_ssd_one_direction(xf_ref[b], dtf_ref[b], arow_ref[...], trif_ref[...], ef_ref[...],
                                         hf_s, yf_ref, b, 0, False))
        chains.append(_ssd_one_direction(xb_ref[b], dtb_ref[b], arow_ref[...], trib_ref[...], eb_ref[...],
                                         hb_s, yb_ref, b, 1, True))
    _round_robin(chains)


def _ssd_scan(xact, dtp, a_log):
    q = SSD_CHUNK
    steps = PB // q
    a_neg = -jnp.exp(a_log)
    arow = jnp.concatenate([a_neg.reshape(1, 2 * H_B), jnp.zeros((1, LANES - 2 * H_B), F32)], axis=1)
    r = jnp.arange(q)
    tri_f = (r[None, :] <= r[:, None]).astype(BF16)
    tri_b = (r[None, :] >= r[:, None]).astype(BF16)
    col_head = jnp.arange(D_SSD) // P_B
    lane = jnp.arange(LANES)
    exp_f = (lane[:, None] == col_head[None, :]).astype(BF16)
    exp_b = (lane[:, None] == col_head[None, :] + H_B).astype(BF16)
    fwd = lambda s: (0, _scan_chunk(s, False, q), 0)
    bwd = lambda s: (0, _scan_chunk(s, True, q), 0)
    const = lambda s: (0, 0)
    x3 = xact.reshape(BATCH, PB, CONV_CH)
    d3 = dtp.reshape(BATCH, PB, LANES)
    yf, yb = pl.pallas_call(
        _ssd_scan_kernel,
        grid=(steps,),
        in_specs=[pl.BlockSpec((BATCH, q, CONV_CH), fwd), pl.BlockSpec((BATCH, q, LANES), fwd),
                  pl.BlockSpec((BATCH, q, CONV_CH), bwd), pl.BlockSpec((BATCH, q, LANES), bwd),
                  pl.BlockSpec((1, LANES), const),
                  pl.BlockSpec((q, q), const), pl.BlockSpec((q, q), const),
                  pl.BlockSpec((LANES, D_SSD), const), pl.BlockSpec((LANES, D_SSD), const)],
        out_specs=[pl.BlockSpec((BATCH, q, D_SSD), fwd), pl.BlockSpec((BATCH, q, D_SSD), bwd)],
        out_shape=[jax.ShapeDtypeStruct((BATCH, PB, D_SSD), F32), jax.ShapeDtypeStruct((BATCH, PB, D_SSD), F32)],
        scratch_shapes=[pltpu.VMEM((BATCH, N_B, D_SSD), F32), pltpu.VMEM((BATCH, N_B, D_SSD), F32)],
        compiler_params=_cparams("arbitrary"),
        name="ssd_scan",
    )(x3, d3, x3, d3, arow, tri_f, tri_b, exp_f, exp_b)
    return yf.reshape(T_ALL, D_SSD), yb.reshape(T_ALL, D_SSD)


def _mlstm_one_direction(q, k, v, gi, gf, bi, bf, tri, st_ref, m_ref, h_ref, b, d, rev):
    n = MLSTM_CHUNK
    li = gi + bi
    lf = jax.nn.log_sigmoid(gf + bf)
    yield
    bc = _dot3_left(tri, lf)
    yield
    b_t = bc.T
    li_t = li.T
    yield
    edge = 0 if rev else n - 1
    gtot = bc[edge:edge + 1, :]
    m_old = m_ref[b, 0:1, :]
    w_log = gtot - bc + li
    m_new = jnp.maximum(gtot + m_old, jnp.max(w_log, axis=0, keepdims=True))
    wj = jnp.exp(w_log - m_new)
    dec = jnp.exp(gtot + m_old - m_new)
    inter_log = bc + m_old
    m_ref[b, 0:1, :] = m_new
    yield
    ri = lax.broadcasted_iota(jnp.int32, (n, n), 0)
    ci = lax.broadcasted_iota(jnp.int32, (n, n), 1)
    mask = (ci >= ri) if rev else (ci <= ri)
    lo_half = lax.broadcasted_iota(jnp.int32, (n, LANES), 1) < DQK_C
    row_lo = lax.broadcasted_iota(jnp.int32, (2 * DQK_C, 2 * DV_C), 0) < DQK_C
    ones = jnp.ones((n, DV_C), F32)
    st_old = [st_ref[b, j] for j in range(H_C // 2)]
    upds = {}

    def head(h):
        j, hh = divmod(h, 2)
        gl = GATE_LANE0 + H_C * d + h
        qp = q[:, j * LANES:(j + 1) * LANES] * DQK_C ** -0.5
        kp = k[:, j * LANES:(j + 1) * LANES].astype(BF16)
        qm = jnp.where(lo_half if hh == 0 else jnp.logical_not(lo_half), qp, 0.0).astype(BF16)
        qk = lax.dot_general(qm, kp, _NT, preferred_element_type=F32)
        qs = jnp.dot(qm, st_old[j].astype(BF16), preferred_element_type=F32)
        yield
        dmat = jnp.where(mask, bc[:, gl:gl + 1] - b_t[gl:gl + 1, :] + li_t[gl:gl + 1, :], NEG_BIG)
        il = inter_log[:, gl:gl + 1]
        m_row = jnp.maximum(il, jnp.max(dmat, axis=1, keepdims=True))
        yield
        s = qk * jnp.exp(dmat - m_row)
        w_inter = jnp.exp(il - m_row)
        vh = v[:, h * DV_C:(h + 1) * DV_C]
        yield
        num = jnp.dot(s.astype(BF16), vh.astype(BF16), preferred_element_type=F32) + w_inter * qs[:, :DV_C]
        den = jnp.sum(s, axis=1, keepdims=True) + w_inter * qs[:, DV_C:]
        yield
        h_ref[b, :, h * DV_C:(h + 1) * DV_C] = num / jnp.maximum(jnp.abs(den), jnp.exp(-m_row))
        rhs = (wj[:, gl:gl + 1] * jnp.concatenate([vh, ones], axis=1)).astype(BF16)
        upds[h] = lax.dot_general(kp, rhs, _TN, preferred_element_type=F32)
        yield

    yield from _in_turn([head(h) for h in range(H_C)])
    for j in range(H_C // 2):
        ga = GATE_LANE0 + H_C * d + 2 * j
        decv = jnp.where(row_lo, dec[:, ga:ga + 1], dec[:, ga + 1:ga + 2])
        st_ref[b, j] = decv * st_old[j] + jnp.where(row_lo, upds[2 * j], upds[2 * j + 1])
        yield


def _mlstm_scan_kernel(qf, kf, vf, gif, gff, qb, kb, vb, gib, gfb, bi_ref, bf_ref, trif_ref, trib_ref,
                       hf_ref, hb_ref, stf_s, stb_s, mf_s, mb_s):
    @pl.when(pl.program_id(0) == 0)
    def _():
        stf_s[...] = jnp.zeros_like(stf_s)
        stb_s[...] = jnp.zeros_like(stb_s)
        mf_s[...] = jnp.full_like(mf_s, NEG_STATE)
        mb_s[...] = jnp.full_like(mb_s, NEG_STATE)

    chains = []
    for b in range(BATCH):
        chains.append(_mlstm_one_direction(qf[b], kf[b], vf[b], gif[b], gff[b], bi_ref[...], bf_ref[...],
                                           trif_ref[...], stf_s, mf_s, hf_ref, b, 0, False))
        chains.append(_mlstm_one_direction(qb[b], kb[b], vb[b], gib[b], gfb[b], bi_ref[...], bf_ref[...],
                                           trib_ref[...], stb_s, mb_s, hb_ref, b, 1, True))
    _round_robin(chains)


def _mlstm_scan(z, gate_b):
    n = MLSTM_CHUNK
    steps = PB // n
    qkw = H_C * DQK_C
    pad = lambda t: jnp.concatenate([jnp.zeros((1, GATE_LANE0), F32), t.reshape(1, 2 * H_C),
                                     jnp.zeros((1, LANES - GATE_LANE0 - 2 * H_C), F32)], axis=1)
    bi = pad(gate_b[:, 0, :])
    bf = pad(gate_b[:, 1, :])
    r = jnp.arange(n)
    tri_f = (r[None, :] <= r[:, None]).astype(BF16)
    tri_b = (r[None, :] >= r[:, None]).astype(BF16)
    z3 = z.reshape(BATCH, PB, Z_COLS)

    def specs(rev):
        ch = lambda s: _scan_chunk(s, rev, n)
        return [pl.BlockSpec((BATCH, n, qkw), lambda s: (0, ch(s), Z_MQ // qkw)),
                pl.BlockSpec((BATCH, n, qkw), lambda s: (0, ch(s), Z_MK // qkw)),
                pl.BlockSpec((BATCH, n, D_MLSTM), lambda s: (0, ch(s), Z_MV // D_MLSTM)),
                pl.BlockSpec((BATCH, n, LANES), lambda s: (0, ch(s), Z_MG // LANES)),
                pl.BlockSpec((BATCH, n, LANES), lambda s: (0, ch(s), Z_DT // LANES))]

    const = lambda s: (0, 0)
    n_pairs = H_C // 2
    hf, hb = pl.pallas_call(
        _mlstm_scan_kernel,
        grid=(steps,),
        in_specs=specs(False) + specs(True) + [
            pl.BlockSpec((1, LANES), const), pl.BlockSpec((1, LANES), const),
            pl.BlockSpec((n, n), const), pl.BlockSpec((n, n), const)],
        out_specs=[pl.BlockSpec((BATCH, n, D_MLSTM), lambda s: (0, _scan_chunk(s, False, n), 0)),
                   pl.BlockSpec((BATCH, n, D_MLSTM), lambda s: (0, _scan_chunk(s, True, n), 0))],
        out_shape=[jax.ShapeDtypeStruct((BATCH, PB, D_MLSTM), F32), jax.ShapeDtypeStruct((BATCH, PB, D_MLSTM), F32)],
        scratch_shapes=[pltpu.VMEM((BATCH, n_pairs, 2 * DQK_C, 2 * DV_C), F32),
                        pltpu.VMEM((BATCH, n_pairs, 2 * DQK_C, 2 * DV_C), F32),
                        pltpu.VMEM((BATCH, 8, LANES), F32), pltpu.VMEM((BATCH, 8, LANES), F32)],
        compiler_params=_cparams("arbitrary"),
        name="mlstm_scan",
    )(z3, z3, z3, z3, z3, z3, z3, z3, z3, z3, bi, bf, tri_f, tri_b)
    return hf.reshape(T_ALL, D_MLSTM), hb.reshape(T_ALL, D_MLSTM)


S5_Q = 256
S5_SEG = S5_Q // 8
S5_LANES = G_S5 * P_S5
S5_SLAB = 512
S5_NSLAB = S5_LANES // S5_SLAB


def _s5_dir_kernel(u_ref, perm_ref, wbr_ref, wbi_ref, atab_ref, apr_ref, api_ref, wcr_ref, wci_ref, y_ref,
                   ur_s, ui_s, xr_s, xi_s, st_s, car_s, *, reverse):
    @pl.when(pl.program_id(0) == 0)
    def _():
        st_s[...] = jnp.zeros_like(st_s)

    for b in range(BATCH):
        up = jnp.dot(perm_ref[...], u_ref[b].astype(BF16), preferred_element_type=F32).astype(BF16)
        for m in range(S5_NSLAB):
            um = up[:, m * LANES:(m + 1) * LANES]
            ur_s[b, :, m * S5_SLAB:(m + 1) * S5_SLAB] = jnp.dot(um, wbr_ref[m], preferred_element_type=F32)
            ui_s[b, :, m * S5_SLAB:(m + 1) * S5_SLAB] = jnp.dot(um, wbi_ref[m], preferred_element_type=F32)

    per = 4
    for grp in range(S5_LANES // (per * LANES)):
        cols = [grp * per * LANES + j * LANES for j in range(per)]
        a_r = [atab_ref[0:8, c0:c0 + LANES] for c0 in cols]
        a_i = [atab_ref[8:16, c0:c0 + LANES] for c0 in cols]

        def body(i, carry, cols=cols, a_r=a_r, a_i=a_i):
            t = (S5_SEG - 1 - i) if reverse else i
            r0 = pl.multiple_of(t * 8, 8)
            new = []
            for b in range(BATCH):
                for j, c0 in enumerate(cols):
                    xr, xi = carry[2 * (b * per + j)], carry[2 * (b * per + j) + 1]
                    nr = a_r[j] * xr - a_i[j] * xi + ur_s[b, pl.ds(r0, 8), c0:c0 + LANES]
                    ni = a_r[j] * xi + a_i[j] * xr + ui_s[b, pl.ds(r0, 8), c0:c0 + LANES]
                    ur_s[b, pl.ds(r0, 8), c0:c0 + LANES] = nr
                    ui_s[b, pl.ds(r0, 8), c0:c0 + LANES] = ni
                    new += [nr, ni]
            return tuple(new)

        lax.fori_loop(0, S5_SEG, body, tuple(jnp.zeros((8, LANES), F32) for _ in range(2 * per * BATCH)), unroll=2)

    as_r = atab_ref[16:17, :]
    as_i = atab_ref[17:18, :]
    end_row = 0 if reverse else 8 * (S5_SEG - 1)
    for b in range(BATCH):
        cr = st_s[b, 0:1, :]
        ci = st_s[b, 1:2, :]
        for k in (range(7, -1, -1) if reverse else range(8)):
            car_s[b, k:k + 1, :] = cr
            car_s[b, 8 + k:9 + k, :] = ci
            er = ur_s[b, end_row + k:end_row + k + 1, :]
            ei = ui_s[b, end_row + k:end_row + k + 1, :]
            cr, ci = er + as_r * cr - as_i * ci, ei + as_r * ci + as_i * cr
        st_s[b, 0:1, :] = cr
        st_s[b, 1:2, :] = ci

    for b in range(BATCH):
        for m in range(S5_NSLAB):
            cs = slice(m * S5_SLAB, (m + 1) * S5_SLAB)
            c_r = jnp.concatenate([car_s[b, 0:8, cs], car_s[b, 0:8, cs]], axis=0)
            c_i = jnp.concatenate([car_s[b, 8:16, cs], car_s[b, 8:16, cs]], axis=0)

            def fix(i, _, b=b, cs=cs, c_r=c_r, c_i=c_i):
                r0 = pl.multiple_of(i * 16, 16)
                p_r = apr_ref[pl.ds(r0, 16), cs]
                p_i = api_ref[pl.ds(r0, 16), cs]
                xr_s[b, pl.ds(r0, 16), cs] = (ur_s[b, pl.ds(r0, 16), cs] + p_r * c_r - p_i * c_i).astype(BF16)
                xi_s[b, pl.ds(r0, 16), cs] = (ui_s[b, pl.ds(r0, 16), cs] + p_r * c_i + p_i * c_r).astype(BF16)
                return 0

            lax.fori_loop(0, S5_Q // 16, fix, 0, unroll=2)

    for b in range(BATCH):
        for m in range(S5_NSLAB):
            cs = slice(m * S5_SLAB, (m + 1) * S5_SLAB)
            y_ref[b, :, m * LANES:(m + 1) * LANES] = (
                jnp.dot(xr_s[b, :, cs], wcr_ref[m], preferred_element_type=F32)
                - jnp.dot(xi_s[b, :, cs], wci_ref[m], preferred_element_type=F32))


def _s5_direction(z3, perm, wbr, wbi, atab, apr, api, wcr, wci, reverse):
    steps = PB // S5_Q
    chunk = lambda s: _scan_chunk(s, reverse, S5_Q)
    const2 = lambda s: (0, 0)
    const3 = lambda s: (0, 0, 0)
    return pl.pallas_call(
        functools.partial(_s5_dir_kernel, reverse=reverse),
        grid=(steps,),
        in_specs=[pl.BlockSpec((BATCH, S5_Q, D_S5), lambda s: (0, chunk(s), Z_D // D_S5)),
                  pl.BlockSpec((S5_Q, S5_Q), const2),
                  pl.BlockSpec(wbr.shape, const3),
                  pl.BlockSpec(wbi.shape, const3),
                  pl.BlockSpec(atab.shape, const2),
                  pl.BlockSpec(apr.shape, const2),
                  pl.BlockSpec(api.shape, const2),
                  pl.BlockSpec(wcr.shape, const3),
                  pl.BlockSpec(wci.shape, const3)],
        out_specs=pl.BlockSpec((BATCH, S5_Q, D_S5), lambda s: (0, chunk(s), 0)),
        out_shape=jax.ShapeDtypeStruct((BATCH, PB, D_S5), F32),
        scratch_shapes=[pltpu.VMEM((BATCH, S5_Q, S5_LANES), F32), pltpu.VMEM((BATCH, S5_Q, S5_LANES), F32),
                        pltpu.VMEM((BATCH, S5_Q, S5_LANES), BF16), pltpu.VMEM((BATCH, S5_Q, S5_LANES), BF16),
                        pltpu.VMEM((BATCH, 8, S5_LANES), F32), pltpu.VMEM((BATCH, 16, S5_LANES), F32)],
        compiler_params=_cparams("arbitrary"),
        name="s5_bwd" if reverse else "s5_fwd",
    )(z3, perm, wbr, wbi, atab, apr, api, wcr, wci).reshape(T_ALL, D_S5)


def _s5_tables(lam_re, lam_im, log_dt, b_re, b_im, c_re, c_im):
    dt = jnp.exp(log_dt)[..., None]
    mag = jnp.exp(lam_re * dt)
    ar = mag * jnp.cos(lam_im * dt)
    ai = mag * jnp.sin(lam_im * dt)
    den = lam_re * lam_re + lam_im * lam_im
    cr_ = ((ar - 1.0) * lam_re + ai * lam_im) / den
    ci_ = (ai * lam_re - (ar - 1.0) * lam_im) / den
    bbr = cr_[..., None] * b_re - ci_[..., None] * b_im
    bbi = cr_[..., None] * b_im + ci_[..., None] * b_re
    gps = S5_SLAB // P_S5
    eye = jnp.eye(gps, dtype=F32)

    def drive_w(bb):
        t = bb.reshape(S5_NSLAB, gps, P_S5, S5_GROUP)
        w = jnp.einsum('mgpc,gh->mgchp', t, eye)
        return w.reshape(S5_NSLAB, gps * S5_GROUP, gps * P_S5).astype(BF16)

    def read_w(cc):
        t = cc.reshape(S5_NSLAB, gps, S5_GROUP, P_S5)
        w = jnp.einsum('mgcp,gh->mgphc', t, eye)
        return w.reshape(S5_NSLAB, gps * P_S5, gps * S5_GROUP).astype(BF16)

    steps = jnp.arange(1, S5_SEG + 1, dtype=F32)[:, None]
    out = []
    for d in range(2):
        decay = (lam_re[d] * dt[d]).reshape(1, S5_LANES)
        angle = (lam_im[d] * dt[d]).reshape(1, S5_LANES)
        pmag = jnp.exp(steps * decay)
        pr = pmag * jnp.cos(steps * angle)
        pi = pmag * jnp.sin(steps * angle)
        if d == 1:
            apr, api = jnp.repeat(pr[::-1], 8, axis=0), jnp.repeat(pi[::-1], 8, axis=0)
        else:
            apr, api = jnp.repeat(pr, 8, axis=0), jnp.repeat(pi, 8, axis=0)
        atab = jnp.concatenate([jnp.broadcast_to(pr[0:1], (8, S5_LANES)), jnp.broadcast_to(pi[0:1], (8, S5_LANES)),
                                pr[S5_SEG - 1:], pi[S5_SEG - 1:], jnp.zeros((6, S5_LANES), F32)], axis=0)
        out.append((drive_w(bbr[d]), drive_w(bbi[d]), atab, apr, api))
    r = jnp.arange(S5_Q)
    src = (r % 8) * S5_SEG + r // 8
    perm = (src[:, None] == jnp.arange(S5_Q)[None, :]).astype(BF16)
    return out, read_w(c_re), read_w(c_im), perm


def _s5_scans(z, p):
    dirs, wcr, wci, perm = _s5_tables(p["s5_lam_re"], p["s5_lam_im"], p["s5_log_dt"], p["s5_b_re"], p["s5_b_im"],
                                      p["s5_c_re"], p["s5_c_im"])
    z3 = z.reshape(BATCH, PB, Z_COLS)
    yf = _s5_direction(z3, perm, *dirs[0], wcr, wci, reverse=False)
    yb = _s5_direction(z3, perm, *dirs[1], wcr, wci, reverse=True)
    return yf, yb, perm.T


def _merge_kernel(ya_ref, sf_ref, sb_ref, sx_ref, sg_ref, mf_ref, mb_ref, mo_ref, df_ref, db_ref, du_ref,
                  permt_ref, sd_ref, dd_ref, gw_ref, gb_ref, onw_ref, w0, w1, w2, w3, x_ref, mod_ref,
                  o_ref, wb_s, *, g_idx):
    @pl.when(pl.program_id(0) == 0)
    def _():
        for r, w in enumerate((w0, w1, w2, w3)):
            wb_s[r] = w[...].astype(BF16)

    onw = onw_ref[...]
    out = {}

    def attn():
        out[0] = jnp.dot(ya_ref[...], wb_s[0], preferred_element_type=F32)
        yield

    def ssd():
        gate = sg_ref[...]
        y = (sf_ref[...] + sb_ref[...] + sd_ref[...] * sx_ref[...]) * (gate * jax.nn.sigmoid(gate))
        yield
        y = (_rms(y) * onw[:, D_MLA:D_MLA + D_SSD]).astype(BF16)
        yield
        out[1] = jnp.dot(y, wb_s[1], preferred_element_type=F32)
        yield

    def mlstm():
        gate = jax.nn.sigmoid(mo_ref[...])
        c0 = D_MLA + D_SSD
        parts = []
        for h in range(H_C):
            cs = slice(h * DV_C, (h + 1) * DV_C)
            hn = _rms(mf_ref[:, cs] + mb_ref[:, cs])
            parts.append((hn * gate[:, cs] * onw[:, c0 + h * DV_C:c0 + (h + 1) * DV_C]).astype(BF16))
            yield
        out[2] = jnp.dot(jnp.concatenate(parts, axis=1), wb_s[2], preferred_element_type=F32)
        yield

    def s5():
        y = _dot3_left(permt_ref[...], df_ref[...] + db_ref[...])
        yield
        y = jax.nn.gelu(y + dd_ref[...] * du_ref[...])
        yield
        gate = jax.nn.sigmoid(jnp.dot(y.astype(BF16), gw_ref[...], preferred_element_type=F32) + gb_ref[...])
        yield
        y = (_rms(y * gate) * onw[:, D_MLA + D_SSD + D_MLSTM:]).astype(BF16)
        yield
        out[3] = jnp.dot(y, wb_s[3], preferred_element_type=F32)
        yield

    _round_robin([attn(), ssd(), mlstm(), s5()])
    g = mod_ref[0][g_idx:g_idx + 1, :]
    o_ref[...] = x_ref[...] + g * (out[0] + out[1] + out[2] + out[3])


def _merge_proj(ya, ssd, mlstm, s5, z, p, w_out, l, x, modtab, g_idx):
    sf, sb, xact = ssd
    mf, mb = mlstm
    df, db, permt = s5
    kq = D_MODEL // 4
    row = lambda i: (i, 0)
    const = lambda i: (0, 0)
    grp = lambda col: pl.BlockSpec((TILE, kq), lambda i: (i, col // kq))
    w_specs = [pl.BlockSpec((None, kq, D_MODEL), functools.partial(lambda i, r: (l, r, 0), r=r),
                            pipeline_mode=pl.Buffered(1)) for r in range(4)]
    return pl.pallas_call(
        functools.partial(_merge_kernel, g_idx=g_idx),
        grid=(T_ALL // TILE,),
        in_specs=[pl.BlockSpec((TILE, kq), row),
                  pl.BlockSpec((TILE, kq), row), pl.BlockSpec((TILE, kq), row), grp(0), grp(Z_SZ),
                  pl.BlockSpec((TILE, kq), row), pl.BlockSpec((TILE, kq), row), grp(Z_MO),
                  pl.BlockSpec((TILE, kq), row), pl.BlockSpec((TILE, kq), row), grp(Z_D),
                  pl.BlockSpec((TILE, TILE), const),
                  pl.BlockSpec((1, kq), const), pl.BlockSpec((1, kq), const),
                  pl.BlockSpec((kq, kq), const), pl.BlockSpec((1, kq), const),
                  pl.BlockSpec((1, D_MODEL), const)] + w_specs + [
            pl.BlockSpec((TILE, D_MODEL), row),
            pl.BlockSpec((1, 6, D_MODEL), lambda i: (_tile_mod_row(i), 0, 0))],
        out_specs=pl.BlockSpec((TILE, D_MODEL), row),
        out_shape=jax.ShapeDtypeStruct((T_ALL, D_MODEL), F32),
        scratch_shapes=[pltpu.VMEM((4, kq, D_MODEL), BF16)],
        compiler_params=_cparams("arbitrary"),
        name="merge_proj",
    )(ya, sf, sb, xact, z, mf, mb, z, df, db, z, permt,
      jnp.repeat(p["ssd_d"], P_B).reshape(1, kq), p["s5_d"].reshape(1, kq),
      p["s5_glu_w"].astype(BF16), p["s5_glu_b"].reshape(1, kq), p["out_norm_w"].reshape(1, D_MODEL),
      w_out, w_out, w_out, w_out, x, modtab)


def _rope_tables():
    pos = np.arange(SEQ)
    row = (pos // GRID_W).astype(np.float32)
    col = (pos % GRID_W).astype(np.float32)
    inv_freq = (ROPE_BASE ** (-np.arange(ROPE_AXIS // 2, dtype=np.float32) * 2.0 / ROPE_AXIS)).astype(np.float32)
    ang_r = row[:, None] * inv_freq
    ang_c = col[:, None] * inv_freq
    zeros = np.zeros((SEQ, LANES - D_ROPE), np.float32)
    cos = np.concatenate([np.cos(ang_r), np.cos(ang_r), np.cos(ang_c), np.cos(ang_c), zeros], axis=1)
    sin = np.concatenate([np.sin(ang_r), np.sin(ang_r), np.sin(ang_c), np.sin(ang_c), zeros], axis=1)
    cos_c = np.concatenate([np.ones((TILE, D_ROPE), np.float32), np.zeros((TILE, LANES - D_ROPE), np.float32)], axis=1)
    sin_c = np.zeros((TILE, LANES), np.float32)
    table = np.concatenate([np.concatenate([cos, sin], axis=1), np.concatenate([cos_c, sin_c], axis=1)], axis=0)
    return jnp.asarray(table.astype(np.float32))


def _layout_mla(w_uq, w_ukv):
    k = w_uq.shape[0]
    qa, qb, wk, wv = [], [], [], []
    for h in range(H_A):
        base = h * (D_NOPE + D_ROPE)
        rope = w_uq[:, base + D_NOPE:base + D_NOPE + D_ROPE]
        qa += [w_uq[:, base:base + D_NOPE], rope, jnp.zeros((k, LANES - D_ROPE), w_uq.dtype)]
        qb += [_rot_cols(rope), jnp.zeros((k, LANES - D_ROPE), w_uq.dtype)]
        kb = h * (D_NOPE + D_V)
        wk.append(w_ukv[:, kb:kb + D_NOPE])
        wv.append(w_ukv[:, kb + D_NOPE:kb + D_NOPE + D_V])
    cat = lambda xs: jnp.concatenate(xs, axis=1).astype(BF16)
    return cat(qa), cat(qb), cat(wk), cat(wv)


def _layer(xall, hx, modtab, p, big, l, cs, norm_w, next_modtab, last):
    z = _mm(hx, _layout_w_in(big["w_in"], l), MM_ROWS, MM_COLS)
    onw = p["out_norm_w"]
    wqa, wqb, wk, wv = _layout_mla(p["mla_w_uq"], p["mla_w_ukv"])
    q, k, v = _mla_prep(z, cs, p["mla_q_norm_w"], p["mla_kv_norm_w"], wqa, wqb, wk, wv)
    ya = _attention(q, k, v, onw[:D_MLA].reshape(1, D_MLA))
    xact, dtp = _ssd_prep(z, p["ssd_conv_w"], p["ssd_conv_b"], p["ssd_dt_bias"])
    ssd = _ssd_scan(xact, dtp, p["ssd_a_log"]) + (xact,)
    mlstm = _mlstm_scan(z, p["mlstm_gate_b"])
    s5 = _s5_scans(z, p)
    xall = _merge_proj(ya, ssd, mlstm, s5, z, p, big["w_out"], l, xall, modtab, 2)

    w_router = jnp.concatenate([p["moe_w_group"], p["moe_w_expert"],
                                jnp.zeros((D_MODEL, ROUTER_COLS - N_GROUPS - N_EXPERTS), F32)], axis=1)
    wr_hi = w_router.astype(BF16)
    wr_lo = (w_router - wr_hi.astype(F32)).astype(BF16)
    h2, route, route_t, cnt = _prenorm_router(xall, p["norm2_w"], modtab, jnp.stack([wr_hi, wr_lo]), 3, 4, last)
    return _moe(xall, h2, route, route_t, cnt, modtab, big["moe_w_gate"], big["moe_w_up"], big["moe_w_down"], l,
                norm_w, next_modtab, last)


def kernel(x, c, ctx, c_ctx, mod_w, mod_b, norm1_w, w_in, mla_q_norm_w, mla_kv_norm_w, mla_w_uq, mla_w_ukv,
           ssd_conv_w, ssd_conv_b, ssd_a_log, ssd_dt_bias, ssd_d, mlstm_gate_b, s5_lam_re, s5_lam_im,
           s5_log_dt, s5_b_re, s5_b_im, s5_c_re, s5_c_im, s5_d, s5_glu_w, s5_glu_b, out_norm_w, w_out,
           norm2_w, moe_w_group, moe_w_expert, moe_w_gate, moe_w_up, moe_w_down, final_norm_w):
    stacked = {"norm1_w": norm1_w, "mla_q_norm_w": mla_q_norm_w, "mla_kv_norm_w": mla_kv_norm_w,
               "mla_w_uq": mla_w_uq, "mla_w_ukv": mla_w_ukv, "ssd_conv_w": ssd_conv_w, "ssd_conv_b": ssd_conv_b,
               "ssd_a_log": ssd_a_log, "ssd_dt_bias": ssd_dt_bias, "ssd_d": ssd_d, "mlstm_gate_b": mlstm_gate_b,
               "s5_lam_re": s5_lam_re, "s5_lam_im": s5_lam_im, "s5_log_dt": s5_log_dt, "s5_b_re": s5_b_re,
               "s5_b_im": s5_b_im, "s5_c_re": s5_c_re, "s5_c_im": s5_c_im, "s5_d": s5_d, "s5_glu_w": s5_glu_w,
               "s5_glu_b": s5_glu_b, "out_norm_w": out_norm_w, "norm2_w": norm2_w,
               "moe_w_group": moe_w_group, "moe_w_expert": moe_w_expert}
    big = {"w_in": w_in, "w_out": w_out, "moe_w_gate": moe_w_gate, "moe_w_up": moe_w_up, "moe_w_down": moe_w_down}
    cs = _rope_tables()
    cc = jnp.concatenate([c, c_ctx[None, :], jnp.zeros((8 - BATCH - 1, D_MODEL), F32)], axis=0)
    modtabs = [_modulation(cc, mod_w, mod_b, l)[:BATCH + 1].reshape(BATCH + 1, 6, D_MODEL) for l in range(DEPTH)]
    xall, hx = _prenorm(x, ctx, norm1_w[0], modtabs[0], 0, 1)
    for l in range(DEPTH):
        p = {name: val[l] for name, val in stacked.items()}
        if l == DEPTH - 1:
            out = _layer(xall, hx, modtabs[l], p, big, l, cs, final_norm_w, modtabs[l], True)
        else:
            xall, hx = _layer(xall, hx, modtabs[l], p, big, l, cs, norm1_w[l + 1], modtabs[l + 1], False)
    return out.reshape(BATCH, SEQ, D_MODEL)
```

```python
import functools
import math

import jax
import jax.numpy as jnp
import numpy as np
from jax import lax
from jax.experimental import pallas as pl
from jax.experimental.pallas import tpu as pltpu

F32 = jnp.float32
BF16 = jnp.bfloat16

D_MODEL = 2048
BATCH = 2
SEQ = 4096
DEPTH = 2
GRID_W = 64
CTX_LEN = 256
EPS = 1e-6
NEG_STATE = -1e30
NEG_BIG = -1e30

H_A = 4
D_NOPE = 128
D_ROPE = 64
D_V = 128
Q_RANK = 384
KV_RANK = 128
ROPE_AXIS = D_ROPE // 2
ROPE_BASE = 10000.0
D_MLA = H_A * D_V
D_SSD = 512
P_B = 64
H_B = D_SSD // P_B
G_B = 2
N_B = 128
SSD_CONV = 5
SSD_CHUNK = 128
CONV_CH = D_SSD + 2 * G_B * N_B
D_MLSTM = 512
H_C = 4
DV_C = D_MLSTM // H_C
DQK_C = DV_C // 2
MLSTM_CHUNK = 128
D_S5 = 512
S5_GROUP = 16
G_S5 = D_S5 // S5_GROUP
P_S5 = 64
A_COLS = Q_RANK + KV_RANK + D_ROPE
B_COLS = D_SSD + CONV_CH + 2 * H_B
C_COLS = 2 * H_C * DQK_C + 2 * D_MLSTM + 4 * H_C
N_GROUPS = 4
EXPERTS_PER_GROUP = 8
N_EXPERTS = N_GROUPS * EXPERTS_PER_GROUP
TOP_K = 2
D_EXPERT = 512

PB = CTX_LEN + SEQ
T_X = BATCH * SEQ
T_ALL = BATCH * PB

LANES = 128
VMEM_LIMIT_BYTES = 56 * 1024 * 1024

TILE = 256
TPB = PB // TILE
XT = SEQ // TILE
MM_ROWS = T_ALL // 8
MM_COLS = 2304

Z_CQ = 0
Z_CKV = 384
Z_KR = 512
Z_KRR = 640
Z_MQ = 768
Z_XBC = 1024
Z_SZ = 2048
Z_MV = 2560
Z_MO = 3072
Z_D = 3584
Z_MK = 4096
Z_DT = 4352
Z_MG = 4480
Z_COLS = 4608
GATE_LANE0 = 2 * H_B

MOE_ROWS = 256
ROUTER_COLS = 128

_NT = (((1,), (1,)), ((), ()))
_TN = (((0,), (0,)), ((), ()))


def _cparams(*sem):
    return pltpu.CompilerParams(dimension_semantics=sem, vmem_limit_bytes=VMEM_LIMIT_BYTES)


def _tile_mod_row(i):
    return jnp.where(i % TPB == 0, BATCH, i // TPB)


def _token_tile(i, latent_only):
    return (i // XT) * TPB + 1 + i % XT if latent_only else i


def _rms(x, w=None):
    y = x * lax.rsqrt(jnp.mean(x * x, axis=-1, keepdims=True) + EPS)
    return y if w is None else y * w


def _in_turn(chains):
    live = list(chains)
    while live:
        nxt = []
        for chain in live:
            try:
                next(chain)
                nxt.append(chain)
            except StopIteration:
                pass
        live = nxt
        yield


def _round_robin(chains):
    for _ in _in_turn(chains):
        pass


def _split3(x):
    hi = x.astype(BF16)
    r1 = x - hi.astype(F32)
    mid = r1.astype(BF16)
    lo = (r1 - mid.astype(F32)).astype(BF16)
    return hi, mid, lo


def _dot3_left(sel, x):
    hi, mid, lo = _split3(x)
    return (jnp.dot(sel, hi, preferred_element_type=F32) + jnp.dot(sel, mid, preferred_element_type=F32)
            + jnp.dot(sel, lo, preferred_element_type=F32))


def _dot3_right(x, sel):
    hi, mid, lo = _split3(x)
    return (jnp.dot(hi, sel, preferred_element_type=F32) + jnp.dot(mid, sel, preferred_element_type=F32)
            + jnp.dot(lo, sel, preferred_element_type=F32))


def _mod_kernel(a_ref, w_ref, b_ref, o_ref):
    a = a_ref[...]
    a = a * jax.nn.sigmoid(a)
    o_ref[...] = jnp.dot(a.astype(BF16), w_ref[...].astype(BF16), preferred_element_type=F32) + b_ref[...]


def _modulation(cc, mod_w, mod_b, l):
    n = mod_w.shape[2]
    tn = 1024
    return pl.pallas_call(
        _mod_kernel,
        grid=(n // tn,),
        in_specs=[pl.BlockSpec((8, D_MODEL), lambda j: (0, 0)),
                  pl.BlockSpec((None, D_MODEL, tn), lambda j: (l, 0, j)),
                  pl.BlockSpec((None, 1, tn), lambda j: (l, 0, j))],
        out_specs=pl.BlockSpec((8, tn), lambda j: (0, j)),
        out_shape=jax.ShapeDtypeStruct((8, n), F32),
        compiler_params=_cparams("arbitrary"),
        name="modulation",
    )(cc, mod_w, mod_b.reshape(DEPTH, 1, n))


def _prenorm_kernel(x_ref, c_ref, w_ref, mod_ref, xall_ref, o_ref, *, sh_idx, sc_idx):
    v = jnp.where(pl.program_id(0) % TPB == 0, c_ref[...], x_ref[...])
    xall_ref[...] = v
    y = _rms(v, w_ref[...])
    m = mod_ref[0]
    y = y * (1.0 + m[sc_idx:sc_idx + 1, :]) + m[sh_idx:sh_idx + 1, :]
    o_ref[...] = y.astype(o_ref.dtype)


def _prenorm(x, ctx, w, modtab, sh_idx, sc_idx):
    row = lambda i: (i, 0)
    return pl.pallas_call(
        functools.partial(_prenorm_kernel, sh_idx=sh_idx, sc_idx=sc_idx),
        grid=(T_ALL // TILE,),
        in_specs=[pl.BlockSpec((TILE, D_MODEL), lambda i: ((i // TPB) * XT + jnp.maximum(i % TPB - 1, 0), 0)),
                  pl.BlockSpec((TILE, D_MODEL), lambda i: (i // TPB, 0)),
                  pl.BlockSpec((1, D_MODEL), lambda i: (0, 0)),
                  pl.BlockSpec((1, 6, D_MODEL), lambda i: (_tile_mod_row(i), 0, 0))],
        out_specs=[pl.BlockSpec((TILE, D_MODEL), row), pl.BlockSpec((TILE, D_MODEL), row)],
        out_shape=[jax.ShapeDtypeStruct((T_ALL, D_MODEL), F32), jax.ShapeDtypeStruct((T_ALL, D_MODEL), BF16)],
        compiler_params=_cparams("arbitrary"),
        name="prenorm",
    )(x.reshape(T_X, D_MODEL), ctx.reshape(BATCH * CTX_LEN, D_MODEL), w.reshape(1, D_MODEL), modtab)


def _mm_kernel(a_ref, w_ref, o_ref):
    o_ref[...] = jnp.dot(a_ref[...], w_ref[...], preferred_element_type=F32).astype(o_ref.dtype)


def _mm(a, w, tm, tn, out_dtype=F32):
    m, k = a.shape
    n = w.shape[1]
    return pl.pallas_call(
        _mm_kernel,
        grid=(n // tn, m // tm),
        in_specs=[pl.BlockSpec((tm, k), lambda j, i: (i, 0)),
                  pl.BlockSpec((k, tn), lambda j, i: (0, j))],
        out_specs=pl.BlockSpec((tm, tn), lambda j, i: (i, j)),
        out_shape=jax.ShapeDtypeStruct((m, n), out_dtype),
        compiler_params=_cparams("arbitrary", "arbitrary"),
        name="in_proj",
    )(a, w)


W_IN_B0 = A_COLS
W_IN_C0 = A_COLS + B_COLS
W_IN_D0 = A_COLS + B_COLS + C_COLS
W_IN_CM = W_IN_C0 + 2 * H_C * DQK_C
W_IN_GB = W_IN_CM + 2 * D_MLSTM
W_IN_WIDE = ((Z_CQ, 0, Q_RANK + KV_RANK),
             (Z_MQ, W_IN_C0, H_C * DQK_C),
             (Z_XBC, W_IN_B0 + D_SSD, CONV_CH),
             (Z_SZ, W_IN_B0, D_SSD),
             (Z_MV, W_IN_CM, 2 * D_MLSTM),
             (Z_D, W_IN_D0, D_S5),
             (Z_MK, W_IN_C0 + H_C * DQK_C, H_C * DQK_C))
W_IN_ROWS = 256


def _rot_cols(w):
    q = ROPE_AXIS // 2
    return jnp.concatenate([-w[:, q:2 * q], w[:, 0:q], -w[:, 3 * q:4 * q], w[:, 2 * q:3 * q]], axis=1)


W_IN_KR0 = Q_RANK + KV_RANK
W_IN_DT0 = W_IN_B0 + D_SSD + CONV_CH
W_IN_WINDOWS = tuple(c // LANES * LANES for c in (W_IN_KR0, W_IN_DT0, W_IN_GB))


def _narrow_selector():
    sel = np.zeros((3 * LANES, 4 * LANES), np.float32)

    def put(window, src_col, dst_col, width, sign=1.0):
        base = window * LANES + src_col - W_IN_WINDOWS[window]
        for c in range(width):
            sel[base + c, dst_col + c] = sign

    q = ROPE_AXIS // 2
    put(0, W_IN_KR0, 0, D_ROPE)
    for dst, src, sign in ((0, q, -1.0), (q, 0, 1.0), (2 * q, 3 * q, -1.0), (3 * q, 2 * q, 1.0)):
        put(0, W_IN_KR0 + src, LANES + dst, q, sign)
    put(1, W_IN_DT0, 2 * LANES, 2 * H_B)
    for d in range(2):
        put(2, W_IN_GB + (2 * d + 1) * H_C, 2 * LANES + GATE_LANE0 + d * H_C, H_C)
        put(2, W_IN_GB + 2 * d * H_C, 3 * LANES + GATE_LANE0 + d * H_C, H_C)
    return jnp.asarray(sel, dtype=BF16)


def _w_in_layout_kernel(w_ref, sel_ref, o_ref):
    for dst, src, width in W_IN_WIDE:
        o_ref[:, dst:dst + width] = w_ref[:, src:src + width].astype(BF16)
    windows = jnp.concatenate([w_ref[:, c:c + LANES] for c in W_IN_WINDOWS], axis=1).astype(BF16)
    small = jnp.dot(windows, sel_ref[...], preferred_element_type=F32).astype(BF16)
    o_ref[:, Z_KR:Z_MQ] = small[:, 0:2 * LANES]
    o_ref[:, Z_DT:Z_COLS] = small[:, 2 * LANES:4 * LANES]


def _layout_w_in(w_in, l):
    _, k, n = w_in.shape
    sel = _narrow_selector()
    return pl.pallas_call(
        _w_in_layout_kernel,
        grid=(k // W_IN_ROWS,),
        in_specs=[pl.BlockSpec((None, W_IN_ROWS, n), lambda i: (l, i, 0)),
                  pl.BlockSpec(sel.shape, lambda i: (0, 0))],
        out_specs=pl.BlockSpec((W_IN_ROWS, Z_COLS), lambda i: (i, 0)),
        out_shape=jax.ShapeDtypeStruct((k, Z_COLS), BF16),
        compiler_params=_cparams("arbitrary"),
        name="w_in_layout",
    )(w_in, sel)


def _mla_prep_kernel(za_ref, cs_ref, qw_ref, kvw_ref, wqa_ref, wqb_ref, wk_ref, wv_ref, q_ref, k_ref, v_ref):
    za = za_ref[...]
    cos = cs_ref[:, :LANES]
    sin = cs_ref[:, LANES:]
    qn = _rms(za[:, Z_CQ:Z_CQ + Q_RANK], qw_ref[...]).astype(BF16)
    kvn = _rms(za[:, Z_CKV:Z_CKV + KV_RANK], kvw_ref[...]).astype(BF16)
    qa = jnp.dot(qn, wqa_ref[...], preferred_element_type=F32)
    qb = jnp.dot(qn, wqb_ref[...], preferred_element_type=F32)
    kn = jnp.dot(kvn, wk_ref[...], preferred_element_type=F32)
    v = jnp.dot(kvn, wv_ref[...], preferred_element_type=F32)
    kr = (za[:, Z_KR:Z_KR + LANES] * cos + za[:, Z_KRR:Z_KRR + LANES] * sin).astype(BF16)
    for h in range(H_A):
        c0 = h * 2 * LANES
        q_ref[:, c0:c0 + LANES] = qa[:, c0:c0 + LANES].astype(BF16)
        q_ref[:, c0 + LANES:c0 + 2 * LANES] = (
            qa[:, c0 + LANES:c0 + 2 * LANES] * cos + qb[:, h * LANES:(h + 1) * LANES] * sin).astype(BF16)
        k_ref[:, c0:c0 + LANES] = kn[:, h * LANES:(h + 1) * LANES].astype(BF16)
        k_ref[:, c0 + LANES:c0 + 2 * LANES] = kr
    v_ref[...] = v.astype(BF16)


ATT_W = H_A * 2 * LANES


def _mla_prep(z, cs, qw, kvw, wqa, wqb, wk, wv):
    const = lambda i: (0, 0)
    rope_blk = lambda i: (jnp.where(i % TPB == 0, XT, i % TPB - 1), 0)
    return pl.pallas_call(
        _mla_prep_kernel,
        grid=(T_ALL // TILE,),
        in_specs=[pl.BlockSpec((TILE, Z_MQ), lambda i: (i, 0)),
                  pl.BlockSpec((TILE, 2 * LANES), rope_blk),
                  pl.BlockSpec((1, Q_RANK), const),
                  pl.BlockSpec((1, KV_RANK), const),
                  pl.BlockSpec(wqa.shape, const),
                  pl.BlockSpec(wqb.shape, const),
                  pl.BlockSpec(wk.shape, const),
                  pl.BlockSpec(wv.shape, const)],
        out_specs=[pl.BlockSpec((TILE, ATT_W), lambda i: (i, 0)),
                   pl.BlockSpec((TILE, ATT_W), lambda i: (i, 0)),
                   pl.BlockSpec((TILE, D_MLA), lambda i: (i, 0))],
        out_shape=[jax.ShapeDtypeStruct((T_ALL, ATT_W), BF16),
                   jax.ShapeDtypeStruct((T_ALL, ATT_W), BF16),
                   jax.ShapeDtypeStruct((T_ALL, D_MLA), BF16)],
        compiler_params=_cparams("arbitrary"),
        name="mla_prep",
    )(z, cs, qw.reshape(1, Q_RANK), kvw.reshape(1, KV_RANK), wqa, wqb, wk, wv)


def _attn_tile(q_ref, k_ref, v_ref, w_ref, o_ref, acc_ref, n_keys):
    scale2 = (D_NOPE + D_ROPE) ** -0.5 * math.log2(math.e)
    for h in range(H_A):
        q = q_ref[:, h * 2 * LANES:(h + 1) * 2 * LANES]
        s = lax.dot_general(q, k_ref[0:n_keys, h * 2 * LANES:(h + 1) * 2 * LANES], _NT, preferred_element_type=F32)
        m = jnp.max(s, axis=-1, keepdims=True)
        p = jnp.exp2((s - m) * scale2)
        l = jnp.sum(p, axis=-1, keepdims=True)
        o = jnp.dot(p.astype(BF16), v_ref[0:n_keys, h * D_V:(h + 1) * D_V], preferred_element_type=F32)
        acc_ref[:, h * D_V:(h + 1) * D_V] = o / l
    o_ref[...] = (_rms(acc_ref[...]) * w_ref[...]).astype(o_ref.dtype)


def _attn_kernel(q_ref, k_ref, v_ref, w_ref, o_ref, acc_ref):
    @pl.when(pl.program_id(1) == 0)
    def _():
        _attn_tile(q_ref, k_ref, v_ref, w_ref, o_ref, acc_ref, CTX_LEN)

    @pl.when(pl.program_id(1) != 0)
    def _():
        _attn_tile(q_ref, k_ref, v_ref, w_ref, o_ref, acc_ref, PB)


def _attention(q, k, v, onw):
    return pl.pallas_call(
        _attn_kernel,
        grid=(BATCH, TPB),
        in_specs=[pl.BlockSpec((TILE, ATT_W), lambda b, i: (b * TPB + i, 0)),
                  pl.BlockSpec((None, PB, ATT_W), lambda b, i: (b, 0, 0)),
                  pl.BlockSpec((None, PB, D_MLA), lambda b, i: (b, 0, 0)),
                  pl.BlockSpec((1, D_MLA), lambda b, i: (0, 0))],
        out_specs=pl.BlockSpec((TILE, D_MLA), lambda b, i: (b * TPB + i, 0)),
        out_shape=jax.ShapeDtypeStruct((T_ALL, D_MLA), BF16),
        scratch_shapes=[pltpu.VMEM((TILE, D_MLA), F32)],
        compiler_params=_cparams("arbitrary", "arbitrary"),
        name="attention",
    )(q, k.reshape(BATCH, PB, ATT_W), v.reshape(BATCH, PB, D_MLA), onw)


R_E0, R_E1, R_W0, R_W1, R_K0, R_K1 = range(6)
NO_LANE = 2 * LANES


def _prenorm_router_kernel(x_ref, w_ref, mod_ref, wr_ref, tri_ref, h_ref, route_ref, route_t_ref, cnt_ref, cnt_s,
                           *, sh_idx, sc_idx):
    @pl.when(pl.program_id(0) == 0)
    def _():
        cnt_s[...] = jnp.zeros_like(cnt_s)

    y = _rms(x_ref[...], w_ref[...])
    m = mod_ref[0]
    y = y * (1.0 + m[sc_idx:sc_idx + 1, :]) + m[sh_idx:sh_idx + 1, :]
    h_ref[...] = y
    y_hi = y.astype(BF16)
    y_lo = (y - y_hi.astype(F32)).astype(BF16)
    lg = (jnp.dot(y_hi, wr_ref[0], preferred_element_type=F32) + jnp.dot(y_lo, wr_ref[0], preferred_element_type=F32)
          + jnp.dot(y_hi, wr_ref[1], preferred_element_type=F32))
    lane = lax.broadcasted_iota(jnp.int32, lg.shape, 1)

    def first_max(v):
        top = jnp.max(v, axis=1, keepdims=True)
        return top, jnp.min(jnp.where(v == top, lane, NO_LANE), axis=1, keepdims=True)

    is_g = lane < N_GROUPS
    g_top, g_idx = first_max(jnp.where(is_g, lg, NEG_BIG))
    g_w = 1.0 / jnp.sum(jnp.where(is_g, jnp.exp(lg - g_top), 0.0), axis=1, keepdims=True)
    lo = N_GROUPS + EXPERTS_PER_GROUP * g_idx
    el = jnp.where(jnp.logical_and(lane >= lo, lane < lo + EXPERTS_PER_GROUP), lg, NEG_BIG)
    v1, i1 = first_max(el)
    v2, i2 = first_max(jnp.where(lane == i1, NEG_BIG, el))
    t = jnp.exp(v2 - v1)
    w0 = g_w / (1.0 + t)
    w1 = g_w * t / (1.0 + t)
    e0 = i1 - N_GROUPS
    e1 = i2 - N_GROUPS
    hit0 = lane == e0
    hit1 = lane == e1
    onehot = jnp.logical_or(hit0, hit1).astype(F32)
    before = cnt_s[0:1, :] + jnp.dot(tri_ref[...], onehot.astype(BF16), preferred_element_type=F32)
    k0 = jnp.sum(jnp.where(hit0, before, 0.0), axis=1, keepdims=True)
    k1 = jnp.sum(jnp.where(hit1, before, 0.0), axis=1, keepdims=True)
    cnt_s[0:1, :] = cnt_s[0:1, :] + jnp.sum(onehot, axis=0, keepdims=True)
    cnt_ref[...] = cnt_s[...]
    rec = jnp.zeros(lg.shape, F32)
    for ln, val in ((R_E0, e0.astype(F32)), (R_E1, e1.astype(F32)), (R_W0, w0), (R_W1, w1), (R_K0, k0), (R_K1, k1)):
        rec = jnp.where(lane == ln, val, rec)
    route_ref[...] = rec
    route_t_ref[...] = rec.T[0:8, :]


def _prenorm_router(x, w, modtab, w_router, sh_idx, sc_idx, latent_only):
    n_tok = T_X if latent_only else T_ALL
    r = jnp.arange(TILE)
    tri = (r[None, :] < r[:, None]).astype(BF16)
    src = lambda i: _token_tile(i, latent_only)
    return pl.pallas_call(
        functools.partial(_prenorm_router_kernel, sh_idx=sh_idx, sc_idx=sc_idx),
        grid=(n_tok // TILE,),
        in_specs=[pl.BlockSpec((TILE, D_MODEL), lambda i: (src(i), 0)),
                  pl.BlockSpec((1, D_MODEL), lambda i: (0, 0)),
                  pl.BlockSpec((1, 6, D_MODEL), lambda i: (_tile_mod_row(src(i)), 0, 0)),
                  pl.BlockSpec((2, D_MODEL, ROUTER_COLS), lambda i: (0, 0, 0)),
                  pl.BlockSpec((TILE, TILE), lambda i: (0, 0))],
        out_specs=[pl.BlockSpec((TILE, D_MODEL), lambda i: (i, 0)),
                   pl.BlockSpec((TILE, ROUTER_COLS), lambda i: (i, 0)),
                   pl.BlockSpec((8, TILE), lambda i: (0, i)),
                   pl.BlockSpec((8, ROUTER_COLS), lambda i: (0, 0))],
        out_shape=[jax.ShapeDtypeStruct((n_tok, D_MODEL), F32),
                   jax.ShapeDtypeStruct((n_tok, ROUTER_COLS), F32),
                   jax.ShapeDtypeStruct((8, n_tok), F32),
                   jax.ShapeDtypeStruct((8, ROUTER_COLS), F32)],
        scratch_shapes=[pltpu.VMEM((8, ROUTER_COLS), F32)],
        compiler_params=_cparams("arbitrary"),
        name="prenorm_router",
    )(x, w.reshape(1, D_MODEL), modtab, w_router, tri)


def _row_copy(src, src_row, dst, dst_row, sem):
    return pltpu.make_async_copy(src.at[pl.ds(src_row, 1)], dst.at[pl.ds(dst_row, 1)], sem)


def _dispatch_kernel(dest_ref, h_ref, xb_ref, sem):
    def issue(r, carry):
        for k in range(TOP_K):
            _row_copy(h_ref, r, xb_ref, dest_ref[0, k, r], sem).start(priority=k)
        return carry

    lax.fori_loop(0, TILE, issue, 0, unroll=8)

    def drain(r, carry):
        for k in range(TOP_K):
            _row_copy(h_ref, 0, xb_ref, 0, sem).wait()
        return carry

    lax.fori_loop(0, TILE, drain, 0, unroll=8)


def _dispatch(h, dest3):
    n_tok = h.shape[0]
    return pl.pallas_call(
        _dispatch_kernel,
        grid=(n_tok // TILE,),
        in_specs=[pl.BlockSpec((1, TOP_K, TILE), lambda i: (i, 0, 0), memory_space=pltpu.SMEM),
                  pl.BlockSpec((TILE, D_MODEL), lambda i: (i, 0))],
        out_specs=pl.BlockSpec(memory_space=pl.ANY),
        out_shape=jax.ShapeDtypeStruct((n_tok * TOP_K, D_MODEL), F32),
        scratch_shapes=[pltpu.SemaphoreType.DMA(())],
        compiler_params=_cparams("arbitrary"),
        name="moe_dispatch",
    )(dest3, h)


def _moe_kernel(tile_ref, exp_ref, lo_ref, hi_ref, flag_ref, next_ref, x_ref, wg_hbm, wu_hbm, wd_hbm, o_ref,
                stage_g, stage_u, stage_d, wg_s, wu_s, wd_s, sem, *, layer):
    i = pl.program_id(0)
    flags = flag_ref[i]

    def fetch(e):
        return (pltpu.make_async_copy(wg_hbm.at[layer, e], stage_g, sem.at[0]),
                pltpu.make_async_copy(wu_hbm.at[layer, e], stage_u, sem.at[1]),
                pltpu.make_async_copy(wd_hbm.at[layer, e], stage_d, sem.at[2]))

    @pl.when(i == 0)
    def _():
        for copy in fetch(exp_ref[0]):
            copy.start()

    @pl.when(flags % 2 == 1)
    def _():
        for copy in fetch(exp_ref[i]):
            copy.wait()
        wg_s[...] = stage_g[...].astype(BF16)
        wu_s[...] = stage_u[...].astype(BF16)
        wd_s[...] = stage_d[...].astype(BF16)

        @pl.when(next_ref[i] >= 0)
        def _():
            for copy in fetch(next_ref[i]):
                copy.start()

    @pl.when(flags >= 4)
    def _():
        x = x_ref[...].astype(BF16)
        g = jnp.dot(x, wg_s[...], preferred_element_type=F32)
        u = jnp.dot(x, wu_s[...], preferred_element_type=F32)
        row = lax.broadcasted_iota(jnp.int32, g.shape, 0)
        mine = jnp.logical_and(row >= lo_ref[i], row < hi_ref[i])
        h = jnp.where(mine, g * jax.nn.sigmoid(g) * u, 0.0).astype(BF16)
        res = jnp.dot(h, wd_s[...], preferred_element_type=F32)

        @pl.when((flags // 2) % 2 == 1)
        def _():
            o_ref[...] = res

        @pl.when((flags // 2) % 2 == 0)
        def _():
            o_ref[...] += res


def _moe_experts(xb, meta, wg, wu, wd, l):
    n_items = meta[0].shape[0]
    xmap = lambda i, ti, ex, lo, hi, fl, nx: (ti[i], 0)
    hbm = pl.BlockSpec(memory_space=pl.ANY)
    grid_spec = pltpu.PrefetchScalarGridSpec(
        num_scalar_prefetch=6,
        grid=(n_items,),
        in_specs=[pl.BlockSpec((MOE_ROWS, D_MODEL), xmap), hbm, hbm, hbm],
        out_specs=pl.BlockSpec((MOE_ROWS, D_MODEL), xmap),
        scratch_shapes=[pltpu.VMEM((D_MODEL, D_EXPERT), F32),
                        pltpu.VMEM((D_MODEL, D_EXPERT), F32),
                        pltpu.VMEM((D_EXPERT, D_MODEL), F32),
                        pltpu.VMEM((D_MODEL, D_EXPERT), BF16),
                        pltpu.VMEM((D_MODEL, D_EXPERT), BF16),
                        pltpu.VMEM((D_EXPERT, D_MODEL), BF16),
                        pltpu.SemaphoreType.DMA((3,))])
    return pl.pallas_call(
        functools.partial(_moe_kernel, layer=l),
        grid_spec=grid_spec,
        out_shape=jax.ShapeDtypeStruct(xb.shape, F32),
        compiler_params=_cparams("arbitrary"),
        name="moe_experts",
    )(*meta, xb, wg, wu, wd)


def _combine_kernel(dest_ref, next_ref, yb_ref, x_ref, route_ref, mod_ref, nw_ref, nmod_ref, *rest, final):
    if final:
        o_ref, buf, sem = rest
    else:
        o_ref, hx_ref, buf, sem = rest
    i = pl.program_id(0)
    slot = i % 2
    n_groups = TILE // COMBINE_ROWS

    def issue(d_ref, to_slot, j):
        for rr in range(COMBINE_ROWS):
            r = j * COMBINE_ROWS + rr
            for k in range(TOP_K):
                _row_copy(yb_ref, d_ref[0, k, r], buf.at[to_slot, k], r, sem.at[to_slot]).start(priority=k)

    @pl.when(i == 0)
    def _():
        def first(j, carry):
            issue(dest_ref, 0, j)
            return carry

        lax.fori_loop(0, n_groups, first, 0)

    def drain(r, carry):
        for k in range(TOP_K):
            _row_copy(yb_ref, 0, buf.at[slot, k], 0, sem.at[slot]).wait()
        return carry

    lax.fori_loop(0, TILE, drain, 0, unroll=8)
    g2 = mod_ref[0][5:6, :]
    nw = nw_ref[...]
    nm = nmod_ref[0]

    def combine(j):
        rows = pl.ds(pl.multiple_of(j * COMBINE_ROWS, COMBINE_ROWS), COMBINE_ROWS)
        rt = route_ref[rows, :]
        f = buf[slot, 0, rows, :] * rt[:, R_W0:R_W0 + 1] + buf[slot, 1, rows, :] * rt[:, R_W1:R_W1 + 1]
        y = x_ref[rows, :] + g2 * f
        if final:
            o_ref[rows, :] = _rms(y, nw)
        else:
            o_ref[rows, :] = y
            hx_ref[rows, :] = (_rms(y, nw) * (1.0 + nm[1:2, :]) + nm[0:1, :]).astype(hx_ref.dtype)

    @pl.when(i + 1 < pl.num_programs(0))
    def _():
        def body(j, carry):
            combine(j)
            issue(next_ref, 1 - slot, j)
            return carry

        lax.fori_loop(0, n_groups, body, 0)

    @pl.when(i + 1 >= pl.num_programs(0))
    def _():
        def body(j, carry):
            combine(j)
            return carry

        lax.fori_loop(0, n_groups, body, 0)


COMBINE_ROWS = 128


def _combine(yb, dest3, x, route, modtab, norm_w, next_modtab, latent_only):
    n_tok = route.shape[0]
    n_tiles = n_tok // TILE
    src = lambda i: _token_tile(i, latent_only)
    row = lambda i: (i, 0)
    mod_spec = pl.BlockSpec((1, 6, D_MODEL), lambda i: (_tile_mod_row(src(i)), 0, 0))
    out_specs = [pl.BlockSpec((TILE, D_MODEL), row)]
    out_shape = [jax.ShapeDtypeStruct((n_tok, D_MODEL), F32)]
    if not latent_only:
        out_specs.append(pl.BlockSpec((TILE, D_MODEL), row))
        out_shape.append(jax.ShapeDtypeStruct((n_tok, D_MODEL), BF16))
    return pl.pallas_call(
        functools.partial(_combine_kernel, final=latent_only),
        grid=(n_tiles,),
        in_specs=[pl.BlockSpec((1, TOP_K, TILE), lambda i: (i, 0, 0), memory_space=pltpu.SMEM),
                  pl.BlockSpec((1, TOP_K, TILE), lambda i: (jnp.minimum(i + 1, n_tiles - 1), 0, 0),
                               memory_space=pltpu.SMEM),
                  pl.BlockSpec(memory_space=pl.ANY),
                  pl.BlockSpec((TILE, D_MODEL), lambda i: (src(i), 0)),
                  pl.BlockSpec((TILE, ROUTER_COLS), row),
                  mod_spec,
                  pl.BlockSpec((1, D_MODEL), lambda i: (0, 0)),
                  mod_spec],
        out_specs=out_specs,
        out_shape=out_shape,
        scratch_shapes=[pltpu.VMEM((2, TOP_K, TILE, D_MODEL), F32), pltpu.SemaphoreType.DMA((2,))],
        compiler_params=_cparams("arbitrary"),
        name="moe_combine",
    )(dest3, dest3, yb, x, route, modtab, norm_w.reshape(1, D_MODEL), next_modtab)


def _moe_plan(route_t, cnt):
    t = route_t.shape[1]
    n_tiles = t * TOP_K // MOE_ROWS
    n_items = n_tiles + N_EXPERTS - 1
    experts = route_t[R_E0:R_E1 + 1].astype(jnp.int32)
    rank = route_t[R_K0:R_K1 + 1].astype(jnp.int32)
    counts = cnt[0, :N_EXPERTS].astype(jnp.int32)
    ends = jnp.cumsum(counts)
    starts = ends - counts
    e_axis = jnp.arange(N_EXPERTS, dtype=jnp.int32)[:, None, None]
    dest = jnp.sum(jnp.where(experts[None] == e_axis, starts[:, None, None], 0), axis=0) + rank
    t_first = starts // MOE_ROWS
    per_e = jnp.where(counts > 0, (ends - 1) // MOE_ROWS - t_first + 1, 0)
    item_end = jnp.cumsum(per_e)
    item_start = item_end - per_e
    total = item_end[-1]
    w = jnp.arange(n_items, dtype=jnp.int32)
    valid = w < total
    wc = jnp.minimum(w, total - 1)
    ex = jnp.minimum(jnp.sum((item_end[None, :] <= wc[:, None]).astype(jnp.int32), axis=1), N_EXPERTS - 1)
    e_ids = jnp.arange(N_EXPERTS, dtype=jnp.int32)
    onehot = ex[:, None] == e_ids[None, :]
    pick = lambda table: jnp.sum(jnp.where(onehot, table[None, :], 0), axis=1)
    tile = (pick(t_first) + wc - pick(item_start)).astype(jnp.int32)
    lo = jnp.where(valid, jnp.maximum(pick(starts), tile * MOE_ROWS) - tile * MOE_ROWS, 0).astype(jnp.int32)
    hi = jnp.where(valid, jnp.minimum(pick(ends), (tile + 1) * MOE_ROWS) - tile * MOE_ROWS, 0).astype(jnp.int32)
    one = jnp.ones((1,), bool)
    new_e = jnp.concatenate([one, ex[1:] != ex[:-1]])
    new_t = jnp.concatenate([one, tile[1:] != tile[:-1]])
    flags = (new_e.astype(jnp.int32) + 2 * new_t.astype(jnp.int32) + 4 * valid.astype(jnp.int32))
    later = jnp.where(jnp.logical_and(counts[None, :] > 0, e_ids[None, :] > e_ids[:, None]), e_ids[None, :], N_EXPERTS)
    next_of = pick(jnp.min(later, axis=1))
    nxt = jnp.where(next_of < N_EXPERTS, next_of, -1).astype(jnp.int32)
    dest3 = dest.reshape(TOP_K, t // TILE, TILE).transpose(1, 0, 2)
    return dest3, (tile, ex, lo, hi, flags, nxt)


def _moe(x, h, route, route_t, cnt, modtab, wg, wu, wd, l, norm_w, next_modtab, latent_only):
    dest3, meta = _moe_plan(route_t, cnt)
    xb = _dispatch(h, dest3)
    yb = _moe_experts(xb, meta, wg, wu, wd, l)
    res = _combine(yb, dest3, x, route, modtab, norm_w, next_modtab, latent_only)
    return res[0] if latent_only else (res[0], res[1])


def _scan_chunk(s, rev, chunk):
    n_c = CTX_LEN // chunk
    n_all = PB // chunk
    if not rev:
        return s
    return jnp.where(s < n_c, n_c - 1 - s, n_all + n_c - 1 - s)


HALO = 8


def _ssd_prep_kernel(cur_ref, prev_ref, next_ref, dt_ref, cw_ref, cb_ref, dtb_ref, xo_ref, dto_ref, ext_s):
    j = pl.program_id(0) % TPB
    first = jnp.logical_or(j == 0, j == 1)
    last = jnp.logical_or(j == 0, j == TPB - 1)
    ext_s[0:HALO, :] = jnp.where(first, 0.0, prev_ref[...])
    ext_s[HALO:HALO + TILE, :] = cur_ref[...]
    ext_s[HALO + TILE:, :] = jnp.where(last, 0.0, next_ref[...])
    half = (SSD_CONV - 1) // 2
    acc = cb_ref[...] + cw_ref[0:1, :] * ext_s[HALO - half:HALO - half + TILE, :]
    for k in range(1, SSD_CONV):
        acc = acc + cw_ref[k:k + 1, :] * ext_s[HALO - half + k:HALO - half + k + TILE, :]
    xo_ref[...] = acc * jax.nn.sigmoid(acc)
    lane = lax.broadcasted_iota(jnp.int32, (TILE, LANES), 1)
    dto_ref[...] = jnp.where(lane < 2 * H_B, jax.nn.softplus(dt_ref[...] + dtb_ref[...]), 0.0)


def _ssd_prep(z, conv_w, conv_b, dt_bias):
    n_tiles = T_ALL // TILE
    per = TILE // HALO
    cwp = jnp.concatenate([conv_w, jnp.zeros((8 - SSD_CONV, CONV_CH), F32)], axis=0)
    dtb = jnp.concatenate([dt_bias.reshape(1, 2 * H_B), jnp.zeros((1, LANES - 2 * H_B), F32)], axis=1)
    xc = Z_XBC // CONV_CH
    return pl.pallas_call(
        _ssd_prep_kernel,
        grid=(n_tiles,),
        in_specs=[pl.BlockSpec((TILE, CONV_CH), lambda i: (i, xc)),
                  pl.BlockSpec((HALO, CONV_CH), lambda i: (jnp.maximum(i * per - 1, 0), xc)),
                  pl.BlockSpec((HALO, CONV_CH), lambda i: (jnp.minimum((i + 1) * per, T_ALL // HALO - 1), xc)),
                  pl.BlockSpec((TILE, LANES), lambda i: (i, Z_DT // LANES)),
                  pl.BlockSpec((8, CONV_CH), lambda i: (0, 0)),
                  pl.BlockSpec((1, CONV_CH), lambda i: (0, 0)),
                  pl.BlockSpec((1, LANES), lambda i: (0, 0))],
        out_specs=[pl.BlockSpec((TILE, CONV_CH), lambda i: (i, 0)),
                   pl.BlockSpec((TILE, LANES), lambda i: (i, 0))],
        out_shape=[jax.ShapeDtypeStruct((T_ALL, CONV_CH), F32),
                   jax.ShapeDtypeStruct((T_ALL, LANES), F32)],
        scratch_shapes=[pltpu.VMEM((TILE + 2 * HALO, CONV_CH), F32)],
        compiler_params=_cparams("arbitrary"),
        name="ssd_prep",
    )(z, z, z, z, cwp, conv_b.reshape(1, CONV_CH), dtb)


def _ssd_one_direction(xbc, dtp, arow, tri, expand, h_ref, y_ref, b, d, rev):
    q = SSD_CHUNK
    a = dtp * arow
    acum = _dot3_left(tri, a)
    yield
    acum_t = acum.T
    dt_t = dtp.T
    yield
    edge = 0 if rev else q - 1
    atot = acum[edge:edge + 1, :]
    pieces = jnp.concatenate([jnp.exp(atot - acum) * dtp, jnp.exp(acum),
                              jnp.broadcast_to(jnp.exp(atot), (8, LANES))], axis=0)
    ex = _dot3_right(pieces, expand)
    yield
    wend_x = ex[0:q]
    eacum_x = ex[q:2 * q]
    dec_x = ex[2 * q:2 * q + 1]
    xs = xbc[:, 0:D_SSD]
    xw = (xs * wend_x).astype(BF16)
    xs_b = xs.astype(BF16)
    h_old = h_ref[b]
    h_b = h_old.astype(BF16)
    ri = lax.broadcasted_iota(jnp.int32, (q, q), 0)
    ci = lax.broadcasted_iota(jnp.int32, (q, q), 1)
    mask = (ci >= ri) if rev else (ci <= ri)
    lo_half = lax.broadcasted_iota(jnp.int32, (q, LANES), 1) < P_B
    hpg = H_B // G_B
    gw = hpg * P_B
    yield

    def group(g):
        bg = xbc[:, D_SSD + g * N_B:D_SSD + (g + 1) * N_B].astype(BF16)
        cg = xbc[:, D_SSD + G_B * N_B + g * N_B:D_SSD + G_B * N_B + (g + 1) * N_B].astype(BF16)
        cb = lax.dot_general(cg, bg, _NT, preferred_element_type=F32)
        inter = jnp.dot(cg, h_b[:, g * gw:(g + 1) * gw], preferred_element_type=F32)
        upd = lax.dot_general(bg, xw[:, g * gw:(g + 1) * gw], _TN, preferred_element_type=F32)
        yield

        def pair(j):
            ms = []
            for hh in range(2):
                hc = H_B * d + hpg * g + 2 * j + hh
                seg = acum[:, hc:hc + 1] - acum_t[hc:hc + 1, :]
                dec = jnp.exp(jnp.where(mask, seg, NEG_BIG))
                ms.append((cb * dec * dt_t[hc:hc + 1, :]).astype(BF16))
                yield
            c0 = g * gw + 2 * j * P_B
            xp = xs_b[:, c0:c0 + LANES]
            zero = jnp.zeros_like(xp)
            rhs = jnp.concatenate([jnp.where(lo_half, xp, zero), jnp.where(lo_half, zero, xp)], axis=0)
            y_intra = jnp.dot(jnp.concatenate(ms, axis=1), rhs, preferred_element_type=F32)
            yield
            y_ref[b, :, c0:c0 + LANES] = (
                y_intra + eacum_x[:, c0:c0 + LANES] * inter[:, 2 * j * P_B:2 * j * P_B + LANES])
            yield

        yield from _in_turn([pair(j) for j in range(hpg // 2)])
        h_ref[b, :, g * gw:(g + 1) * gw] = dec_x[:, g * gw:(g + 1) * gw] * h_old[:, g * gw:(g + 1) * gw] + upd
        yield

    yield from _in_turn([group(g) for g in range(G_B)])


def _ssd_scan_kernel(xf_ref, dtf_ref, xb_ref, dtb_ref, arow_ref, trif_ref, trib_ref, ef_ref, eb_ref,
                     yf_ref, yb_ref, hf_s, hb_s):
    @pl.when(pl.program_id(0) == 0)
    def _():
        hf_s[...] = jnp.zeros_like(hf_s)
        hb_s[...] = jnp.zeros_like(hb_s)

    chains = []
    for b in range(BATCH):
        chains.append(_ssd_one_direction(xf_ref[b], dtf_ref[b], arow_ref[...], trif_ref[...], ef_ref[...],
                                         hf_s, yf_ref, b, 0, False))
        chains.append(_ssd_one_direction(xb_ref[b], dtb_ref[b], arow_ref[...], trib_ref[...], eb_ref[...],
                                         hb_s, yb_ref, b, 1, True))
    _round_robin(chains)


def _ssd_scan(xact, dtp, a_log):
    q = SSD_CHUNK
    steps = PB // q
    a_neg = -jnp.exp(a_log)
    arow = jnp.concatenate([a_neg.reshape(1, 2 * H_B), jnp.zeros((1, LANES - 2 * H_B), F32)], axis=1)
    r = jnp.arange(q)
    tri_f = (r[None, :] <= r[:, None]).astype(BF16)
    tri_b = (r[None, :] >= r[:, None]).astype(BF16)
    col_head = jnp.arange(D_SSD) // P_B
    lane = jnp.arange(LANES)
    exp_f = (lane[:, None] == col_head[None, :]).astype(BF16)
    exp_b = (lane[:, None] == col_head[None, :] + H_B).astype(BF16)
    fwd = lambda s: (0, _scan_chunk(s, False, q), 0)
    bwd = lambda s: (0, _scan_chunk(s, True, q), 0)
    const = lambda s: (0, 0)
    x3 = xact.reshape(BATCH, PB, CONV_CH)
    d3 = dtp.reshape(BATCH, PB, LANES)
    yf, yb = pl.pallas_call(
        _ssd_scan_kernel,
        grid=(steps,),
        in_specs=[pl.BlockSpec((BATCH, q, CONV_CH), fwd), pl.BlockSpec((BATCH, q, LANES), fwd),
                  pl.BlockSpec((BATCH, q, CONV_CH), bwd), pl.BlockSpec((BATCH, q, LANES), bwd),
                  pl.BlockSpec((1, LANES), const),
                  pl.BlockSpec((q, q), const), pl.BlockSpec((q, q), const),
                  pl.BlockSpec((LANES, D_SSD), const), pl.BlockSpec((LANES, D_SSD), const)],
        out_specs=[pl.BlockSpec((BATCH, q, D_SSD), fwd), pl.BlockSpec((BATCH, q, D_SSD), bwd)],
        out_shape=[jax.ShapeDtypeStruct((BATCH, PB, D_SSD), F32), jax.ShapeDtypeStruct((BATCH, PB, D_SSD), F32)],
        scratch_shapes=[pltpu.VMEM((BATCH, N_B, D_SSD), F32), pltpu.VMEM((BATCH, N_B, D_SSD), F32)],
        compiler_params=_cparams("arbitrary"),
        name="ssd_scan",
    )(x3, d3, x3, d3, arow, tri_f, tri_b, exp_f, exp_b)
    return yf.reshape(T_ALL, D_SSD), yb.reshape(T_ALL, D_SSD)


def _mlstm_one_direction(q, k, v, gi, gf, bi, bf, tri, st_ref, m_ref, h_ref, b, d, rev):
    n = MLSTM_CHUNK
    li = gi + bi
    lf = jax.nn.log_sigmoid(gf + bf)
    yield
    bc = _dot3_left(tri, lf)
    yield
    b_t = bc.T
    li_t = li.T
    yield
    edge = 0 if rev else n - 1
    gtot = bc[edge:edge + 1, :]
    m_old = m_ref[b, 0:1, :]
    w_log = gtot - bc + li
    m_new = jnp.maximum(gtot + m_old, jnp.max(w_log, axis=0, keepdims=True))
    wj = jnp.exp(w_log - m_new)
    dec = jnp.exp(gtot + m_old - m_new)
    inter_log = bc + m_old
    m_ref[b, 0:1, :] = m_new
    yield
    ri = lax.broadcasted_iota(jnp.int32, (n, n), 0)
    ci = lax.broadcasted_iota(jnp.int32, (n, n), 1)
    mask = (ci >= ri) if rev else (ci <= ri)
    lo_half = lax.broadcasted_iota(jnp.int32, (n, LANES), 1) < DQK_C
    row_lo = lax.broadcasted_iota(jnp.int32, (2 * DQK_C, 2 * DV_C), 0) < DQK_C
    ones = jnp.ones((n, DV_C), F32)
    st_old = [st_ref[b, j] for j in range(H_C // 2)]
    upds = {}

    def head(h):
        j, hh = divmod(h, 2)
        gl = GATE_LANE0 + H_C * d + h
        qp = q[:, j * LANES:(j + 1) * LANES] * DQK_C ** -0.5
        kp = k[:, j * LANES:(j + 1) * LANES].astype(BF16)
        qm = jnp.where(lo_half if hh == 0 else jnp.logical_not(lo_half), qp, 0.0).astype(BF16)
        qk = lax.dot_general(qm, kp, _NT, preferred_element_type=F32)
        qs = jnp.dot(qm, st_old[j].astype(BF16), preferred_element_type=F32)
        yield
        dmat = jnp.where(mask, bc[:, gl:gl + 1] - b_t[gl:gl + 1, :] + li_t[gl:gl + 1, :], NEG_BIG)
        il = inter_log[:, gl:gl + 1]
        m_row = jnp.maximum(il, jnp.max(dmat, axis=1, keepdims=True))
        yield
        s = qk * jnp.exp(dmat - m_row)
        w_inter = jnp.exp(il - m_row)
        vh = v[:, h * DV_C:(h + 1) * DV_C]
        yield
        num = jnp.dot(s.astype(BF16), vh.astype(BF16), preferred_element_type=F32) + w_inter * qs[:, :DV_C]
        den = jnp.sum(s, axis=1, keepdims=True) + w_inter * qs[:, DV_C:]
        yield
        h_ref[b, :, h * DV_C:(h + 1) * DV_C] = num / jnp.maximum(jnp.abs(den), jnp.exp(-m_row))
        rhs = (wj[:, gl:gl + 1] * jnp.concatenate([vh, ones], axis=1)).astype(BF16)
        upds[h] = lax.dot_general(kp, rhs, _TN, preferred_element_type=F32)
        yield

    yield from _in_turn([head(h) for h in range(H_C)])
    for j in range(H_C // 2):
        ga = GATE_LANE0 + H_C * d + 2 * j
        decv = jnp.where(row_lo, dec[:, ga:ga + 1], dec[:, ga + 1:ga + 2])
        st_ref[b, j] = decv * st_old[j] + jnp.where(row_lo, upds[2 * j], upds[2 * j + 1])
        yield


def _mlstm_scan_kernel(qf, kf, vf, gif, gff, qb, kb, vb, gib, gfb, bi_ref, bf_ref, trif_ref, trib_ref,
                       hf_ref, hb_ref, stf_s, stb_s, mf_s, mb_s):
    @pl.when(pl.program_id(0) == 0)
    def _():
        stf_s[...] = jnp.zeros_like(stf_s)
        stb_s[...] = jnp.zeros_like(stb_s)
        mf_s[...] = jnp.full_like(mf_s, NEG_STATE)
        mb_s[...] = jnp.full_like(mb_s, NEG_STATE)

    chains = []
    for b in range(BATCH):
        chains.append(_mlstm_one_direction(qf[b], kf[b], vf[b], gif[b], gff[b], bi_ref[...], bf_ref[...],
                                           trif_ref[...], stf_s, mf_s, hf_ref, b, 0, False))
        chains.append(_mlstm_one_direction(qb[b], kb[b], vb[b], gib[b], gfb[b], bi_ref[...], bf_ref[...],
                                           trib_ref[...], stb_s, mb_s, hb_ref, b, 1, True))
    _round_robin(chains)


def _mlstm_scan(z, gate_b):
    n = MLSTM_CHUNK
    steps = PB // n
    qkw = H_C * DQK_C
    pad = lambda t: jnp.concatenate([jnp.zeros((1, GATE_LANE0), F32), t.reshape(1, 2 * H_C),
                                     jnp.zeros((1, LANES - GATE_LANE0 - 2 * H_C), F32)], axis=1)
    bi = pad(gate_b[:, 0, :])
    bf = pad(gate_b[:, 1, :])
    r = jnp.arange(n)
    tri_f = (r[None, :] <= r[:, None]).astype(BF16)
    tri_b = (r[None, :] >= r[:, None]).astype(BF16)
    z3 = z.reshape(BATCH, PB, Z_COLS)

    def specs(rev):
        ch = lambda s: _scan_chunk(s, rev, n)
        return [pl.BlockSpec((BATCH, n, qkw), lambda s: (0, ch(s), Z_MQ // qkw)),
                pl.BlockSpec((BATCH, n, qkw), lambda s: (0, ch(s), Z_MK // qkw)),
                pl.BlockSpec((BATCH, n, D_MLSTM), lambda s: (0, ch(s), Z_MV // D_MLSTM)),
                pl.BlockSpec((BATCH, n, LANES), lambda s: (0, ch(s), Z_MG // LANES)),
                pl.BlockSpec((BATCH, n, LANES), lambda s: (0, ch(s), Z_DT // LANES))]

    const = lambda s: (0, 0)
    n_pairs = H_C // 2
    hf, hb = pl.pallas_call(
        _mlstm_scan_kernel,
        grid=(steps,),
        in_specs=specs(False) + specs(True) + [
            pl.BlockSpec((1, LANES), const), pl.BlockSpec((1, LANES), const),
            pl.BlockSpec((n, n), const), pl.BlockSpec((n, n), const)],
        out_specs=[pl.BlockSpec((BATCH, n, D_MLSTM), lambda s: (0, _scan_chunk(s, False, n), 0)),
                   pl.BlockSpec((BATCH, n, D_MLSTM), lambda s: (0, _scan_chunk(s, True, n), 0))],
        out_shape=[jax.ShapeDtypeStruct((BATCH, PB, D_MLSTM), F32), jax.ShapeDtypeStruct((BATCH, PB, D_MLSTM), F32)],
        scratch_shapes=[pltpu.VMEM((BATCH, n_pairs, 2 * DQK_C, 2 * DV_C), F32),
                        pltpu.VMEM((BATCH, n_pairs, 2 * DQK_C, 2 * DV_C), F32),
                        pltpu.VMEM((BATCH, 8, LANES), F32), pltpu.VMEM((BATCH, 8, LANES), F32)],
        compiler_params=_cparams("arbitrary"),
        name="mlstm_scan",
    )(z3, z3, z3, z3, z3, z3, z3, z3, z3, z3, bi, bf, tri_f, tri_b)
    return hf.reshape(T_ALL, D_MLSTM), hb.reshape(T_ALL, D_MLSTM)


S5_Q = 256
S5_SEG = S5_Q // 8
S5_LANES = G_S5 * P_S5
S5_SLAB = 512
S5_NSLAB = S5_LANES // S5_SLAB


def _s5_dir_kernel(u_ref, perm_ref, wbr_ref, wbi_ref, atab_ref, apr_ref, api_ref, wcr_ref, wci_ref, y_ref,
                   ur_s, ui_s, xr_s, xi_s, st_s, car_s, *, reverse):
    @pl.when(pl.program_id(0) == 0)
    def _():
        st_s[...] = jnp.zeros_like(st_s)

    for b in range(BATCH):
        up = jnp.dot(perm_ref[...], u_ref[b].astype(BF16), preferred_element_type=F32).astype(BF16)
        for m in range(S5_NSLAB):
            um = up[:, m * LANES:(m + 1) * LANES]
            ur_s[b, :, m * S5_SLAB:(m + 1) * S5_SLAB] = jnp.dot(um, wbr_ref[m], preferred_element_type=F32)
            ui_s[b, :, m * S5_SLAB:(m + 1) * S5_SLAB] = jnp.dot(um, wbi_ref[m], preferred_element_type=F32)

    per = 4
    for grp in range(S5_LANES // (per * LANES)):
        cols = [grp * per * LANES + j * LANES for j in range(per)]
        a_r = [atab_ref[0:8, c0:c0 + LANES] for c0 in cols]
        a_i = [atab_ref[8:16, c0:c0 + LANES] for c0 in cols]

        def body(i, carry, cols=cols, a_r=a_r, a_i=a_i):
            t = (S5_SEG - 1 - i) if reverse else i
            r0 = pl.multiple_of(t * 8, 8)
            new = []
            for b in range(BATCH):
                for j, c0 in enumerate(cols):
                    xr, xi = carry[2 * (b * per + j)], carry[2 * (b * per + j) + 1]
                    nr = a_r[j] * xr - a_i[j] * xi + ur_s[b, pl.ds(r0, 8), c0:c0 + LANES]
                    ni = a_r[j] * xi + a_i[j] * xr + ui_s[b, pl.ds(r0, 8), c0:c0 + LANES]
                    ur_s[b, pl.ds(r0, 8), c0:c0 + LANES] = nr
                    ui_s[b, pl.ds(r0, 8), c0:c0 + LANES] = ni
                    new += [nr, ni]
            return tuple(new)

        lax.fori_loop(0, S5_SEG, body, tuple(jnp.zeros((8, LANES), F32) for _ in range(2 * per * BATCH)), unroll=2)

    as_r = atab_ref[16:17, :]
    as_i = atab_ref[17:18, :]
    end_row = 0 if reverse else 8 * (S5_SEG - 1)
    for b in range(BATCH):
        cr = st_s[b, 0:1, :]
        ci = st_s[b, 1:2, :]
        for k in (range(7, -1, -1) if reverse else range(8)):
            car_s[b, k:k + 1, :] = cr
            car_s[b, 8 + k:9 + k, :] = ci
            er = ur_s[b, end_row + k:end_row + k + 1, :]
            ei = ui_s[b, end_row + k:end_row + k + 1, :]
            cr, ci = er + as_r * cr - as_i * ci, ei + as_r * ci + as_i * cr
        st_s[b, 0:1, :] = cr
        st_s[b, 1:2, :] = ci

    for b in range(BATCH):
        for m in range(S5_NSLAB):
            cs = slice(m * S5_SLAB, (m + 1) * S5_SLAB)
            c_r = jnp.concatenate([car_s[b, 0:8, cs], car_s[b, 0:8, cs]], axis=0)
            c_i = jnp.concatenate([car_s[b, 8:16, cs], car_s[b, 8:16, cs]], axis=0)

            def fix(i, _, b=b, cs=cs, c_r=c_r, c_i=c_i):
                r0 = pl.multiple_of(i * 16, 16)
                p_r = apr_ref[pl.ds(r0, 16), cs]
                p_i = api_ref[pl.ds(r0, 16), cs]
                xr_s[b, pl.ds(r0, 16), cs] = (ur_s[b, pl.ds(r0, 16), cs] + p_r * c_r - p_i * c_i).astype(BF16)
                xi_s[b, pl.ds(r0, 16), cs] = (ui_s[b, pl.ds(r0, 16), cs] + p_r * c_i + p_i * c_r).astype(BF16)
                return 0

            lax.fori_loop(0, S5_Q // 16, fix, 0, unroll=2)

    for b in range(BATCH):
        for m in range(S5_NSLAB):
            cs = slice(m * S5_SLAB, (m + 1) * S5_SLAB)
            y_ref[b, :, m * LANES:(m + 1) * LANES] = (
                jnp.dot(xr_s[b, :, cs], wcr_ref[m], preferred_element_type=F32)
                - jnp.dot(xi_s[b, :, cs], wci_ref[m], preferred_element_type=F32))


def _s5_direction(z3, perm, wbr, wbi, atab, apr, api, wcr, wci, reverse):
    steps = PB // S5_Q
    chunk = lambda s: _scan_chunk(s, reverse, S5_Q)
    const2 = lambda s: (0, 0)
    const3 = lambda s: (0, 0, 0)
    return pl.pallas_call(
        functools.partial(_s5_dir_kernel, reverse=reverse),
        grid=(steps,),
        in_specs=[pl.BlockSpec((BATCH, S5_Q, D_S5), lambda s: (0, chunk(s), Z_D // D_S5)),
                  pl.BlockSpec((S5_Q, S5_Q), const2),
                  pl.BlockSpec(wbr.shape, const3),
                  pl.BlockSpec(wbi.shape, const3),
                  pl.BlockSpec(atab.shape, const2),
                  pl.BlockSpec(apr.shape, const2),
                  pl.BlockSpec(api.shape, const2),
                  pl.BlockSpec(wcr.shape, const3),
                  pl.BlockSpec(wci.shape, const3)],
        out_specs=pl.BlockSpec((BATCH, S5_Q, D_S5), lambda s: (0, chunk(s), 0)),
        out_shape=jax.ShapeDtypeStruct((BATCH, PB, D_S5), F32),
        scratch_shapes=[pltpu.VMEM((BATCH, S5_Q, S5_LANES), F32), pltpu.VMEM((BATCH, S5_Q, S5_LANES), F32),
                        pltpu.VMEM((BATCH, S5_Q, S5_LANES), BF16), pltpu.VMEM((BATCH, S5_Q, S5_LANES), BF16),
                        pltpu.VMEM((BATCH, 8, S5_LANES), F32), pltpu.VMEM((BATCH, 16, S5_LANES), F32)],
        compiler_params=_cparams("arbitrary"),
        name="s5_bwd" if reverse else "s5_fwd",
    )(z3, perm, wbr, wbi, atab, apr, api, wcr, wci).reshape(T_ALL, D_S5)


def _s5_tables(lam_re, lam_im, log_dt, b_re, b_im, c_re, c_im):
    dt = jnp.exp(log_dt)[..., None]
    mag = jnp.exp(lam_re * dt)
    ar = mag * jnp.cos(lam_im * dt)
    ai = mag * jnp.sin(lam_im * dt)
    den = lam_re * lam_re + lam_im * lam_im
    cr_ = ((ar - 1.0) * lam_re + ai * lam_im) / den
    ci_ = (ai * lam_re - (ar - 1.0) * lam_im) / den
    bbr = cr_[..., None] * b_re - ci_[..., None] * b_im
    bbi = cr_[..., None] * b_im + ci_[..., None] * b_re
    gps = S5_SLAB // P_S5
    eye = jnp.eye(gps, dtype=F32)

    def drive_w(bb):
        t = bb.reshape(S5_NSLAB, gps, P_S5, S5_GROUP)
        w = jnp.einsum('mgpc,gh->mgchp', t, eye)
        return w.reshape(S5_NSLAB, gps * S5_GROUP, gps * P_S5).astype(BF16)

    def read_w(cc):
        t = cc.reshape(S5_NSLAB, gps, S5_GROUP, P_S5)
        w = jnp.einsum('mgcp,gh->mgphc', t, eye)
        return w.reshape(S5_NSLAB, gps * P_S5, gps * S5_GROUP).astype(BF16)

    steps = jnp.arange(1, S5_SEG + 1, dtype=F32)[:, None]
    out = []
    for d in range(2):
        decay = (lam_re[d] * dt[d]).reshape(1, S5_LANES)
        angle = (lam_im[d] * dt[d]).reshape(1, S5_LANES)
        pmag = jnp.exp(steps * decay)
        pr = pmag * jnp.cos(steps * angle)
        pi = pmag * jnp.sin(steps * angle)
        if d == 1:
            apr, api = jnp.repeat(pr[::-1], 8, axis=0), jnp.repeat(pi[::-1], 8, axis=0)
        else:
            apr, api = jnp.repeat(pr, 8, axis=0), jnp.repeat(pi, 8, axis=0)
        atab = jnp.concatenate([jnp.broadcast_to(pr[0:1], (8, S5_LANES)), jnp.broadcast_to(pi[0:1], (8, S5_LANES)),
                                pr[S5_SEG - 1:], pi[S5_SEG - 1:], jnp.zeros((6, S5_LANES), F32)], axis=0)
        out.append((drive_w(bbr[d]), drive_w(bbi[d]), atab, apr, api))
    r = jnp.arange(S5_Q)
    src = (r % 8) * S5_SEG + r // 8
    perm = (src[:, None] == jnp.arange(S5_Q)[None, :]).astype(BF16)
    return out, read_w(c_re), read_w(c_im), perm


def _s5_scans(z, p):
    dirs, wcr, wci, perm = _s5_tables(p["s5_lam_re"], p["s5_lam_im"], p["s5_log_dt"], p["s5_b_re"], p["s5_b_im"],
                                      p["s5_c_re"], p["s5_c_im"])
    z3 = z.reshape(BATCH, PB, Z_COLS)
    yf = _s5_direction(z3, perm, *dirs[0], wcr, wci, reverse=False)
    yb = _s5_direction(z3, perm, *dirs[1], wcr, wci, reverse=True)
    return yf, yb, perm.T


def _merge_kernel(ya_ref, sf_ref, sb_ref, sx_ref, sg_ref, mf_ref, mb_ref, mo_ref, df_ref, db_ref, du_ref,
                  permt_ref, sd_ref, dd_ref, gw_ref, gb_ref, onw_ref, w0, w1, w2, w3, x_ref, mod_ref,
                  o_ref, wb_s, *, g_idx):
    @pl.when(pl.program_id(0) == 0)
    def _():
        for r, w in enumerate((w0, w1, w2, w3)):
            wb_s[r] = w[...].astype(BF16)

    onw = onw_ref[...]
    out = {}

    def attn():
        out[0] = jnp.dot(ya_ref[...], wb_s[0], preferred_element_type=F32)
        yield

    def ssd():
        gate = sg_ref[...]
        y = (sf_ref[...] + sb_ref[...] + sd_ref[...] * sx_ref[...]) * (gate * jax.nn.sigmoid(gate))
        yield
        y = (_rms(y) * onw[:, D_MLA:D_MLA + D_SSD]).astype(BF16)
        yield
        out[1] = jnp.dot(y, wb_s[1], preferred_element_type=F32)
        yield

    def mlstm():
        gate = jax.nn.sigmoid(mo_ref[...])
        c0 = D_MLA + D_SSD
        parts = []
        for h in range(H_C):
            cs = slice(h * DV_C, (h + 1) * DV_C)
            hn = _rms(mf_ref[:, cs] + mb_ref[:, cs])
            parts.append((hn * gate[:, cs] * onw[:, c0 + h * DV_C:c0 + (h + 1) * DV_C]).astype(BF16))
            yield
        out[2] = jnp.dot(jnp.concatenate(parts, axis=1), wb_s[2], preferred_element_type=F32)
        yield

    def s5():
        y = _dot3_left(permt_ref[...], df_ref[...] + db_ref[...])
        yield
        y = jax.nn.gelu(y + dd_ref[...] * du_ref[...])
        yield
        gate = jax.nn.sigmoid(jnp.dot(y.astype(BF16), gw_ref[...], preferred_element_type=F32) + gb_ref[...])
        yield
        y = (_rms(y * gate) * onw[:, D_MLA + D_SSD + D_MLSTM:]).astype(BF16)
        yield
        out[3] = jnp.dot(y, wb_s[3], preferred_element_type=F32)
        yield

    _round_robin([attn(), ssd(), mlstm(), s5()])
    g = mod_ref[0][g_idx:g_idx + 1, :]
    o_ref[...] = x_ref[...] + g * (out[0] + out[1] + out[2] + out[3])


def _merge_proj(ya, ssd, mlstm, s5, z, p, w_out, l, x, modtab, g_idx):
    sf, sb, xact = ssd
    mf, mb = mlstm
    df, db, permt = s5
    kq = D_MODEL // 4
    row = lambda i: (i, 0)
    const = lambda i: (0, 0)
    grp = lambda col: pl.BlockSpec((TILE, kq), lambda i: (i, col // kq))
    w_specs = [pl.BlockSpec((None, kq, D_MODEL), functools.partial(lambda i, r: (l, r, 0), r=r),
                            pipeline_mode=pl.Buffered(1)) for r in range(4)]
    return pl.pallas_call(
        functools.partial(_merge_kernel, g_idx=g_idx),
        grid=(T_ALL // TILE,),
        in_specs=[pl.BlockSpec((TILE, kq), row),
                  pl.BlockSpec((TILE, kq), row), pl.BlockSpec((TILE, kq), row), grp(0), grp(Z_SZ),
                  pl.BlockSpec((TILE, kq), row), pl.BlockSpec((TILE, kq), row), grp(Z_MO),
                  pl.BlockSpec((TILE, kq), row), pl.BlockSpec((TILE, kq), row), grp(Z_D),
                  pl.BlockSpec((TILE, TILE), const),
                  pl.BlockSpec((1, kq), const), pl.BlockSpec((1, kq), const),
                  pl.BlockSpec((kq, kq), const), pl.BlockSpec((1, kq), const),
                  pl.BlockSpec((1, D_MODEL), const)] + w_specs + [
            pl.BlockSpec((TILE, D_MODEL), row),
            pl.BlockSpec((1, 6, D_MODEL), lambda i: (_tile_mod_row(i), 0, 0))],
        out_specs=pl.BlockSpec((TILE, D_MODEL), row),
        out_shape=jax.ShapeDtypeStruct((T_ALL, D_MODEL), F32),
        scratch_shapes=[pltpu.VMEM((4, kq, D_MODEL), BF16)],
        compiler_params=_cparams("arbitrary"),
        name="merge_proj",
    )(ya, sf, sb, xact, z, mf, mb, z, df, db, z, permt,
      jnp.repeat(p["ssd_d"], P_B).reshape(1, kq), p["s5_d"].reshape(1, kq),
      p["s5_glu_w"].astype(BF16), p["s5_glu_b"].reshape(1, kq), p["out_norm_w"].reshape(1, D_MODEL),
      w_out, w_out, w_out, w_out, x, modtab)


def _rope_tables():
    pos = np.arange(SEQ)
    row = (pos // GRID_W).astype(np.float32)
    col = (pos % GRID_W).astype(np.float32)
    inv_freq = (ROPE_BASE ** (-np.arange(ROPE_AXIS // 2, dtype=np.float32) * 2.0 / ROPE_AXIS)).astype(np.float32)
    ang_r = row[:, None] * inv_freq
    ang_c = col[:, None] * inv_freq
    zeros = np.zeros((SEQ, LANES - D_ROPE), np.float32)
    cos = np.concatenate([np.cos(ang_r), np.cos(ang_r), np.cos(ang_c), np.cos(ang_c), zeros], axis=1)
    sin = np.concatenate([np.sin(ang_r), np.sin(ang_r), np.sin(ang_c), np.sin(ang_c), zeros], axis=1)
    cos_c = np.concatenate([np.ones((TILE, D_ROPE), np.float32), np.zeros((TILE, LANES - D_ROPE), np.float32)], axis=1)
    sin_c = np.zeros((TILE, LANES), np.float32)
    table = np.concatenate([np.concatenate([cos, sin], axis=1), np.concatenate([cos_c, sin_c], axis=1)], axis=0)
    return jnp.asarray(table.astype(np.float32))


def _layout_mla(w_uq, w_ukv):
    k = w_uq.shape[0]
    qa, qb, wk, wv = [], [], [], []
    for h in range(H_A):
        base = h * (D_NOPE + D_ROPE)
        rope = w_uq[:, base + D_NOPE:base + D_NOPE + D_ROPE]
        qa += [w_uq[:, base:base + D_NOPE], rope, jnp.zeros((k, LANES - D_ROPE), w_uq.dtype)]
        qb += [_rot_cols(rope), jnp.zeros((k, LANES - D_ROPE), w_uq.dtype)]
        kb = h * (D_NOPE + D_V)
        wk.append(w_ukv[:, kb:kb + D_NOPE])
        wv.append(w_ukv[:, kb + D_NOPE:kb + D_NOPE + D_V])
    cat = lambda xs: jnp.concatenate(xs, axis=1).astype(BF16)
    return cat(qa), cat(qb), cat(wk), cat(wv)


def _layer(xall, hx, modtab, p, big, l, cs, norm_w, next_modtab, last):
    z = _mm(hx, _layout_w_in(big["w_in"], l), MM_ROWS, MM_COLS)
    onw = p["out_norm_w"]
    wqa, wqb, wk, wv = _layout_mla(p["mla_w_uq"], p["mla_w_ukv"])
    q, k, v = _mla_prep(z, cs, p["mla_q_norm_w"], p["mla_kv_norm_w"], wqa, wqb, wk, wv)
    ya = _attention(q, k, v, onw[:D_MLA].reshape(1, D_MLA))
    xact, dtp = _ssd_prep(z, p["ssd_conv_w"], p["ssd_conv_b"], p["ssd_dt_bias"])
    ssd = _ssd_scan(xact, dtp, p["ssd_a_log"]) + (xact,)
    mlstm = _mlstm_scan(z, p["mlstm_gate_b"])
    s5 = _s5_scans(z, p)
    xall = _merge_proj(ya, ssd, mlstm, s5, z, p, big["w_out"], l, xall, modtab, 2)

    w_router = jnp.concatenate([p["moe_w_group"], p["moe_w_expert"],
                                jnp.zeros((D_MODEL, ROUTER_COLS - N_GROUPS - N_EXPERTS), F32)], axis=1)
    wr_hi = w_router.astype(BF16)
    wr_lo = (w_router - wr_hi.astype(F32)).astype(BF16)
    h2, route, route_t, cnt = _prenorm_router(xall, p["norm2_w"], modtab, jnp.stack([wr_hi, wr_lo]), 3, 4, last)
    return _moe(xall, h2, route, route_t, cnt, modtab, big["moe_w_gate"], big["moe_w_up"], big["moe_w_down"], l,
                norm_w, next_modtab, last)


def kernel(x, c, ctx, c_ctx, mod_w, mod_b, norm1_w, w_in, mla_q_norm_w, mla_kv_norm_w, mla_w_uq, mla_w_ukv,
           ssd_conv_w, ssd_conv_b, ssd_a_log, ssd_dt_bias, ssd_d, mlstm_gate_b, s5_lam_re, s5_lam_im,
           s5_log_dt, s5_b_re, s5_b_im, s5_c_re, s5_c_im, s5_d, s5_glu_w, s5_glu_b, out_norm_w, w_out,
           norm2_w, moe_w_group, moe_w_expert, moe_w_gate, moe_w_up, moe_w_down, final_norm_w):
    stacked = {"norm1_w": norm1_w, "mla_q_norm_w": mla_q_norm_w, "mla_kv_norm_w": mla_kv_norm_w,
               "mla_w_uq": mla_w_uq, "mla_w_ukv": mla_w_ukv, "ssd_conv_w": ssd_conv_w, "ssd_conv_b": ssd_conv_b,
               "ssd_a_log": ssd_a_log, "ssd_dt_bias": ssd_dt_bias, "ssd_d": ssd_d, "mlstm_gate_b": mlstm_gate_b,
               "s5_lam_re": s5_lam_re, "s5_lam_im": s5_lam_im, "s5_log_dt": s5_log_dt, "s5_b_re": s5_b_re,
               "s5_b_im": s5_b_im, "s5_c_re": s5_c_re, "s5_c_im": s5_c_im, "s5_d": s5_d, "s5_glu_w": s5_glu_w,
               "s5_glu_b": s5_glu_b, "out_norm_w": out_norm_w, "norm2_w": norm2_w,
               "moe_w_group": moe_w_group, "moe_w_expert": moe_w_expert}
    big = {"w_in": w_in, "w_out": w_out, "moe_w_gate": moe_w_gate, "moe_w_up": moe_w_up, "moe_w_down": moe_w_down}
    cs = _rope_tables()
    cc = jnp.concatenate([c, c_ctx[None, :], jnp.zeros((8 - BATCH - 1, D_MODEL), F32)], axis=0)
    modtabs = [_modulation(cc, mod_w, mod_b, l)[:BATCH + 1].reshape(BATCH + 1, 6, D_MODEL) for l in range(DEPTH)]
    xall, hx = _prenorm(x, ctx, norm1_w[0], modtabs[0], 0, 1)
    for l in range(DEPTH):
        p = {name: val[l] for name, val in stacked.items()}
        if l == DEPTH - 1:
            out = _layer(xall, hx, modtabs[l], p, big, l, cs, final_norm_w, modtabs[l], True)
        else:
            xall, hx = _layer(xall, hx, modtabs[l], p, big, l, cs, norm1_w[l + 1], modtabs[l + 1], False)
    return out.reshape(BATCH, SEQ, D_MODEL)
```

```python
import functools
import math

import jax
import jax.numpy as jnp
import numpy as np
from jax import lax
from jax.experimental import pallas as pl
from jax.experimental.pallas import tpu as pltpu

F32 = jnp.float32
BF16 = jnp.bfloat16

D_MODEL = 2048
BATCH = 2
SEQ = 4096
DEPTH = 2
GRID_W = 64
CTX_LEN = 256
EPS = 1e-6
NEG_STATE = -1e30
NEG_BIG = -1e30

H_A = 4
D_NOPE = 128
D_ROPE = 64
D_V = 128
Q_RANK = 384
KV_RANK = 128
ROPE_AXIS = D_ROPE // 2
ROPE_BASE = 10000.0
D_MLA = H_A * D_V
D_SSD = 512
P_B = 64
H_B = D_SSD // P_B
G_B = 2
N_B = 128
SSD_CONV = 5
SSD_CHUNK = 128
CONV_CH = D_SSD + 2 * G_B * N_B
D_MLSTM = 512
H_C = 4
DV_C = D_MLSTM // H_C
DQK_C = DV_C // 2
MLSTM_CHUNK = 128
D_S5 = 512
S5_GROUP = 16
G_S5 = D_S5 // S5_GROUP
P_S5 = 64
A_COLS = Q_RANK + KV_RANK + D_ROPE
B_COLS = D_SSD + CONV_CH + 2 * H_B
C_COLS = 2 * H_C * DQK_C + 2 * D_MLSTM + 4 * H_C
N_GROUPS = 4
EXPERTS_PER_GROUP = 8
N_EXPERTS = N_GROUPS * EXPERTS_PER_GROUP
TOP_K = 2
D_EXPERT = 512

PB = CTX_LEN + SEQ
T_X = BATCH * SEQ
T_ALL = BATCH * PB

LANES = 128
VMEM_LIMIT_BYTES = 56 * 1024 * 1024

TILE = 256
TPB = PB // TILE
XT = SEQ // TILE
MM_ROWS = T_ALL // 8
MM_COLS = 2304

Z_CQ = 0
Z_CKV = 384
Z_KR = 512
Z_KRR = 640
Z_MQ = 768
Z_XBC = 1024
Z_SZ = 2048
Z_MV = 2560
Z_MO = 3072
Z_D = 3584
Z_MK = 4096
Z_DT = 4352
Z_MG = 4480
Z_COLS = 4608
GATE_LANE0 = 2 * H_B

MOE_ROWS = 256
ROUTER_COLS = 128

_NT = (((1,), (1,)), ((), ()))
_TN = (((0,), (0,)), ((), ()))


def _cparams(*sem):
    return pltpu.CompilerParams(dimension_semantics=sem, vmem_limit_bytes=VMEM_LIMIT_BYTES)


def _tile_mod_row(i):
    return jnp.where(i % TPB == 0, BATCH, i // TPB)


def _token_tile(i, latent_only):
    return (i // XT) * TPB + 1 + i % XT if latent_only else i


def _rms(x, w=None):
    y = x * lax.rsqrt(jnp.mean(x * x, axis=-1, keepdims=True) + EPS)
    return y if w is None else y * w


def _in_turn(chains):
    live = list(chains)
    while live:
        nxt = []
        for chain in live:
            try:
                next(chain)
                nxt.append(chain)
            except StopIteration:
                pass
        live = nxt
        yield


def _round_robin(chains):
    for _ in _in_turn(chains):
        pass


def _split3(x):
    hi = x.astype(BF16)
    r1 = x - hi.astype(F32)
    mid = r1.astype(BF16)
    lo = (r1 - mid.astype(F32)).astype(BF16)
    return hi, mid, lo


def _dot3_left(sel, x):
    hi, mid, lo = _split3(x)
    return (jnp.dot(sel, hi, preferred_element_type=F32) + jnp.dot(sel, mid, preferred_element_type=F32)
            + jnp.dot(sel, lo, preferred_element_type=F32))


def _dot3_right(x, sel):
    hi, mid, lo = _split3(x)
    return (jnp.dot(hi, sel, preferred_element_type=F32) + jnp.dot(mid, sel, preferred_element_type=F32)
            + jnp.dot(lo, sel, preferred_element_type=F32))


def _mod_kernel(a_ref, w_ref, b_ref, o_ref):
    a = a_ref[...]
    a = a * jax.nn.sigmoid(a)
    o_ref[...] = jnp.dot(a.astype(BF16), w_ref[...].astype(BF16), preferred_element_type=F32) + b_ref[...]


def _modulation(cc, mod_w, mod_b, l):
    n = mod_w.shape[2]
    tn = 1024
    return pl.pallas_call(
        _mod_kernel,
        grid=(n // tn,),
        in_specs=[pl.BlockSpec((8, D_MODEL), lambda j: (0, 0)),
                  pl.BlockSpec((None, D_MODEL, tn), lambda j: (l, 0, j)),
                  pl.BlockSpec((None, 1, tn), lambda j: (l, 0, j))],
        out_specs=pl.BlockSpec((8, tn), lambda j: (0, j)),
        out_shape=jax.ShapeDtypeStruct((8, n), F32),
        compiler_params=_cparams("arbitrary"),
        name="modulation",
    )(cc, mod_w, mod_b.reshape(DEPTH, 1, n))


def _prenorm_kernel(x_ref, c_ref, w_ref, mod_ref, xall_ref, o_ref, *, sh_idx, sc_idx):
    v = jnp.where(pl.program_id(0) % TPB == 0, c_ref[...], x_ref[...])
    xall_ref[...] = v
    y = _rms(v, w_ref[...])
    m = mod_ref[0]
    y = y * (1.0 + m[sc_idx:sc_idx + 1, :]) + m[sh_idx:sh_idx + 1, :]
    o_ref[...] = y.astype(o_ref.dtype)


def _prenorm(x, ctx, w, modtab, sh_idx, sc_idx):
    row = lambda i: (i, 0)
    return pl.pallas_call(
        functools.partial(_prenorm_kernel, sh_idx=sh_idx, sc_idx=sc_idx),
        grid=(T_ALL // TILE,),
        in_specs=[pl.BlockSpec((TILE, D_MODEL), lambda i: ((i // TPB) * XT + jnp.maximum(i % TPB - 1, 0), 0)),
                  pl.BlockSpec((TILE, D_MODEL), lambda i: (i // TPB, 0)),
                  pl.BlockSpec((1, D_MODEL), lambda i: (0, 0)),
                  pl.BlockSpec((1, 6, D_MODEL), lambda i: (_tile_mod_row(i), 0, 0))],
        out_specs=[pl.BlockSpec((TILE, D_MODEL), row), pl.BlockSpec((TILE, D_MODEL), row)],
        out_shape=[jax.ShapeDtypeStruct((T_ALL, D_MODEL), F32), jax.ShapeDtypeStruct((T_ALL, D_MODEL), BF16)],
        compiler_params=_cparams("arbitrary"),
        name="prenorm",
    )(x.reshape(T_X, D_MODEL), ctx.reshape(BATCH * CTX_LEN, D_MODEL), w.reshape(1, D_MODEL), modtab)


def _mm_kernel(a_ref, w_ref, o_ref):
    o_ref[...] = jnp.dot(a_ref[...], w_ref[...], preferred_element_type=F32).astype(o_ref.dtype)


def _mm(a, w, tm, tn, out_dtype=F32):
    m, k = a.shape
    n = w.shape[1]
    return pl.pallas_call(
        _mm_kernel,
        grid=(n // tn, m // tm),
        in_specs=[pl.BlockSpec((tm, k), lambda j, i: (i, 0)),
                  pl.BlockSpec((k, tn), lambda j, i: (0, j))],
        out_specs=pl.BlockSpec((tm, tn), lambda j, i: (i, j)),
        out_shape=jax.ShapeDtypeStruct((m, n), out_dtype),
        compiler_params=_cparams("arbitrary", "arbitrary"),
        name="in_proj",
    )(a, w)


W_IN_B0 = A_COLS
W_IN_C0 = A_COLS + B_COLS
W_IN_D0 = A_COLS + B_COLS + C_COLS
W_IN_CM = W_IN_C0 + 2 * H_C * DQK_C
W_IN_GB = W_IN_CM + 2 * D_MLSTM
W_IN_WIDE = ((Z_CQ, 0, Q_RANK + KV_RANK),
             (Z_MQ, W_IN_C0, H_C * DQK_C),
             (Z_XBC, W_IN_B0 + D_SSD, CONV_CH),
             (Z_SZ, W_IN_B0, D_SSD),
             (Z_MV, W_IN_CM, 2 * D_MLSTM),
             (Z_D, W_IN_D0, D_S5),
             (Z_MK, W_IN_C0 + H_C * DQK_C, H_C * DQK_C))
W_IN_ROWS = 256


def _rot_cols(w):
    q = ROPE_AXIS // 2
    return jnp.concatenate([-w[:, q:2 * q], w[:, 0:q], -w[:, 3 * q:4 * q], w[:, 2 * q:3 * q]], axis=1)


W_IN_KR0 = Q_RANK + KV_RANK
W_IN_DT0 = W_IN_B0 + D_SSD + CONV_CH
W_IN_WINDOWS = tuple(c // LANES * LANES for c in (W_IN_KR0, W_IN_DT0, W_IN_GB))


def _narrow_selector():
    sel = np.zeros((3 * LANES, 4 * LANES), np.float32)

    def put(window, src_col, dst_col, width, sign=1.0):
        base = window * LANES + src_col - W_IN_WINDOWS[window]
        for c in range(width):
            sel[base + c, dst_col + c] = sign

    q = ROPE_AXIS // 2
    put(0, W_IN_KR0, 0, D_ROPE)
    for dst, src, sign in ((0, q, -1.0), (q, 0, 1.0), (2 * q, 3 * q, -1.0), (3 * q, 2 * q, 1.0)):
        put(0, W_IN_KR0 + src, LANES + dst, q, sign)
    put(1, W_IN_DT0, 2 * LANES, 2 * H_B)
    for d in range(2):
        put(2, W_IN_GB + (2 * d + 1) * H_C, 2 * LANES + GATE_LANE0 + d * H_C, H_C)
        put(2, W_IN_GB + 2 * d * H_C, 3 * LANES + GATE_LANE0 + d * H_C, H_C)
    return jnp.asarray(sel, dtype=BF16)


def _w_in_layout_kernel(w_ref, sel_ref, o_ref):
    for dst, src, width in W_IN_WIDE:
        o_ref[:, dst:dst + width] = w_ref[:, src:src + width].astype(BF16)
    windows = jnp.concatenate([w_ref[:, c:c + LANES] for c in W_IN_WINDOWS], axis=1).astype(BF16)
    small = jnp.dot(windows, sel_ref[...], preferred_element_type=F32).astype(BF16)
    o_ref[:, Z_KR:Z_MQ] = small[:, 0:2 * LANES]
    o_ref[:, Z_DT:Z_COLS] = small[:, 2 * LANES:4 * LANES]


def _layout_w_in(w_in, l):
    _, k, n = w_in.shape
    sel = _narrow_selector()
    return pl.pallas_call(
        _w_in_layout_kernel,
        grid=(k // W_IN_ROWS,),
        in_specs=[pl.BlockSpec((None, W_IN_ROWS, n), lambda i: (l, i, 0)),
                  pl.BlockSpec(sel.shape, lambda i: (0, 0))],
        out_specs=pl.BlockSpec((W_IN_ROWS, Z_COLS), lambda i: (i, 0)),
        out_shape=jax.ShapeDtypeStruct((k, Z_COLS), BF16),
        compiler_params=_cparams("arbitrary"),
        name="w_in_layout",
    )(w_in, sel)


def _mla_prep_kernel(za_ref, cs_ref, qw_ref, kvw_ref, wqa_ref, wqb_ref, wk_ref, wv_ref, q_ref, k_ref, v_ref):
    za = za_ref[...]
    cos = cs_ref[:, :LANES]
    sin = cs_ref[:, LANES:]
    qn = _rms(za[:, Z_CQ:Z_CQ + Q_RANK], qw_ref[...]).astype(BF16)
    kvn = _rms(za[:, Z_CKV:Z_CKV + KV_RANK], kvw_ref[...]).astype(BF16)
    qa = jnp.dot(qn, wqa_ref[...], preferred_element_type=F32)
    qb = jnp.dot(qn, wqb_ref[...], preferred_element_type=F32)
    kn = jnp.dot(kvn, wk_ref[...], preferred_element_type=F32)
    v = jnp.dot(kvn, wv_ref[...], preferred_element_type=F32)
    kr = (za[:, Z_KR:Z_KR + LANES] * cos + za[:, Z_KRR:Z_KRR + LANES] * sin).astype(BF16)
    for h in range(H_A):
        c0 = h * 2 * LANES
        q_ref[:, c0:c0 + LANES] = qa[:, c0:c0 + LANES].astype(BF16)
        q_ref[:, c0 + LANES:c0 + 2 * LANES] = (
            qa[:, c0 + LANES:c0 + 2 * LANES] * cos + qb[:, h * LANES:(h + 1) * LANES] * sin).astype(BF16)
        k_ref[:, c0:c0 + LANES] = kn[:, h * LANES:(h + 1) * LANES].astype(BF16)
        k_ref[:, c0 + LANES:c0 + 2 * LANES] = kr
    v_ref[...] = v.astype(BF16)


ATT_W = H_A * 2 * LANES
ATT_KEY_CHUNK = 1024


def _mla_prep(z, cs, qw, kvw, wqa, wqb, wk, wv):
    const = lambda i: (0, 0)
    rope_blk = lambda i: (jnp.where(i % TPB == 0, XT, i % TPB - 1), 0)
    return pl.pallas_call(
        _mla_prep_kernel,
        grid=(T_ALL // TILE,),
        in_specs=[pl.BlockSpec((TILE, Z_MQ), lambda i: (i, 0)),
                  pl.BlockSpec((TILE, 2 * LANES), rope_blk),
                  pl.BlockSpec((1, Q_RANK), const),
                  pl.BlockSpec((1, KV_RANK), const),
                  pl.BlockSpec(wqa.shape, const),
                  pl.BlockSpec(wqb.shape, const),
                  pl.BlockSpec(wk.shape, const),
                  pl.BlockSpec(wv.shape, const)],
        out_specs=[pl.BlockSpec((TILE, ATT_W), lambda i: (i, 0)),
                   pl.BlockSpec((TILE, ATT_W), lambda i: (i, 0)),
                   pl.BlockSpec((TILE, D_MLA), lambda i: (i, 0))],
        out_shape=[jax.ShapeDtypeStruct((T_ALL, ATT_W), BF16),
                   jax.ShapeDtypeStruct((T_ALL, ATT_W), BF16),
                   jax.ShapeDtypeStruct((T_ALL, D_MLA), BF16)],
        compiler_params=_cparams("arbitrary"),
        name="mla_prep",
    )(z, cs, qw.reshape(1, Q_RANK), kvw.reshape(1, KV_RANK), wqa, wqb, wk, wv)


def _attn_tile(q_ref, k_ref, v_ref, w_ref, o_ref, acc_ref, n_keys):
    scale2 = (D_NOPE + D_ROPE) ** -0.5 * math.log2(math.e)
    for h in range(H_A):
        q = q_ref[:, h * 2 * LANES:(h + 1) * 2 * LANES]
        s = lax.dot_general(q, k_ref[0:n_keys, h * 2 * LANES:(h + 1) * 2 * LANES], _NT, preferred_element_type=F32)
        m = jnp.max(s, axis=-1, keepdims=True)
        bounds = list(range(0, n_keys, ATT_KEY_CHUNK))[:max(n_keys // ATT_KEY_CHUNK, 1)] + [n_keys]
        l = jnp.zeros_like(m)
        o = jnp.zeros((q.shape[0], D_V), F32)
        for c0, c1 in zip(bounds[:-1], bounds[1:]):
            p = jnp.exp2((s[:, c0:c1] - m) * scale2)
            l = l + jnp.sum(p, axis=-1, keepdims=True)
            o = o + jnp.dot(p.astype(BF16), v_ref[c0:c1, h * D_V:(h + 1) * D_V], preferred_element_type=F32)
        acc_ref[:, h * D_V:(h + 1) * D_V] = o / l
    o_ref[...] = (_rms(acc_ref[...]) * w_ref[...]).astype(o_ref.dtype)


def _attn_kernel(q_ref, k_ref, v_ref, w_ref, o_ref, acc_ref):
    @pl.when(pl.program_id(1) == 0)
    def _():
        _attn_tile(q_ref, k_ref, v_ref, w_ref, o_ref, acc_ref, CTX_LEN)

    @pl.when(pl.program_id(1) != 0)
    def _():
        _attn_tile(q_ref, k_ref, v_ref, w_ref, o_ref, acc_ref, PB)


def _attention(q, k, v, onw):
    return pl.pallas_call(
        _attn_kernel,
        grid=(BATCH, TPB),
        in_specs=[pl.BlockSpec((TILE, ATT_W), lambda b, i: (b * TPB + i, 0)),
                  pl.BlockSpec((None, PB, ATT_W), lambda b, i: (b, 0, 0)),
                  pl.BlockSpec((None, PB, D_MLA), lambda b, i: (b, 0, 0)),
                  pl.BlockSpec((1, D_MLA), lambda b, i: (0, 0))],
        out_specs=pl.BlockSpec((TILE, D_MLA), lambda b, i: (b * TPB + i, 0)),
        out_shape=jax.ShapeDtypeStruct((T_ALL, D_MLA), BF16),
        scratch_shapes=[pltpu.VMEM((TILE, D_MLA), F32)],
        compiler_params=_cparams("arbitrary", "arbitrary"),
        name="attention",
    )(q, k.reshape(BATCH, PB, ATT_W), v.reshape(BATCH, PB, D_MLA), onw)


R_E0, R_E1, R_W0, R_W1, R_K0, R_K1 = range(6)
NO_LANE = 2 * LANES


def _prenorm_router_kernel(x_ref, w_ref, mod_ref, wr_ref, tri_ref, h_ref, route_ref, route_t_ref, cnt_ref, cnt_s,
                           *, sh_idx, sc_idx):
    @pl.when(pl.program_id(0) == 0)
    def _():
        cnt_s[...] = jnp.zeros_like(cnt_s)

    y = _rms(x_ref[...], w_ref[...])
    m = mod_ref[0]
    y = y * (1.0 + m[sc_idx:sc_idx + 1, :]) + m[sh_idx:sh_idx + 1, :]
    h_ref[...] = y
    y_hi = y.astype(BF16)
    y_lo = (y - y_hi.astype(F32)).astype(BF16)
    lg = (jnp.dot(y_hi, wr_ref[0], preferred_element_type=F32) + jnp.dot(y_lo, wr_ref[0], preferred_element_type=F32)
          + jnp.dot(y_hi, wr_ref[1], preferred_element_type=F32))
    lane = lax.broadcasted_iota(jnp.int32, lg.shape, 1)

    def first_max(v):
        top = jnp.max(v, axis=1, keepdims=True)
        return top, jnp.min(jnp.where(v == top, lane, NO_LANE), axis=1, keepdims=True)

    is_g = lane < N_GROUPS
    g_top, g_idx = first_max(jnp.where(is_g, lg, NEG_BIG))
    g_w = 1.0 / jnp.sum(jnp.where(is_g, jnp.exp(lg - g_top), 0.0), axis=1, keepdims=True)
    lo = N_GROUPS + EXPERTS_PER_GROUP * g_idx
    el = jnp.where(jnp.logical_and(lane >= lo, lane < lo + EXPERTS_PER_GROUP), lg, NEG_BIG)
    v1, i1 = first_max(el)
    v2, i2 = first_max(jnp.where(lane == i1, NEG_BIG, el))
    t = jnp.exp(v2 - v1)
    w0 = g_w / (1.0 + t)
    w1 = g_w * t / (1.0 + t)
    e0 = i1 - N_GROUPS
    e1 = i2 - N_GROUPS
    hit0 = lane == e0
    hit1 = lane == e1
    onehot = jnp.logical_or(hit0, hit1).astype(F32)
    before = cnt_s[0:1, :] + jnp.dot(tri_ref[...], onehot.astype(BF16), preferred_element_type=F32)
    k0 = jnp.sum(jnp.where(hit0, before, 0.0), axis=1, keepdims=True)
    k1 = jnp.sum(jnp.where(hit1, before, 0.0), axis=1, keepdims=True)
    cnt_s[0:1, :] = cnt_s[0:1, :] + jnp.sum(onehot, axis=0, keepdims=True)
    cnt_ref[...] = cnt_s[...]
    rec = jnp.zeros(lg.shape, F32)
    for ln, val in ((R_E0, e0.astype(F32)), (R_E1, e1.astype(F32)), (R_W0, w0), (R_W1, w1), (R_K0, k0), (R_K1, k1)):
        rec = jnp.where(lane == ln, val, rec)
    route_ref[...] = rec
    route_t_ref[...] = rec.T[0:8, :]


def _prenorm_router(x, w, modtab, w_router, sh_idx, sc_idx, latent_only):
    n_tok = T_X if latent_only else T_ALL
    r = jnp.arange(TILE)
    tri = (r[None, :] < r[:, None]).astype(BF16)
    src = lambda i: _token_tile(i, latent_only)
    return pl.pallas_call(
        functools.partial(_prenorm_router_kernel, sh_idx=sh_idx, sc_idx=sc_idx),
        grid=(n_tok // TILE,),
        in_specs=[pl.BlockSpec((TILE, D_MODEL), lambda i: (src(i), 0)),
                  pl.BlockSpec((1, D_MODEL), lambda i: (0, 0)),
                  pl.BlockSpec((1, 6, D_MODEL), lambda i: (_tile_mod_row(src(i)), 0, 0)),
                  pl.BlockSpec((2, D_MODEL, ROUTER_COLS), lambda i: (0, 0, 0)),
                  pl.BlockSpec((TILE, TILE), lambda i: (0, 0))],
        out_specs=[pl.BlockSpec((TILE, D_MODEL), lambda i: (i, 0)),
                   pl.BlockSpec((TILE, ROUTER_COLS), lambda i: (i, 0)),
                   pl.BlockSpec((8, TILE), lambda i: (0, i)),
                   pl.BlockSpec((8, ROUTER_COLS), lambda i: (0, 0))],
        out_shape=[jax.ShapeDtypeStruct((n_tok, D_MODEL), F32),
                   jax.ShapeDtypeStruct((n_tok, ROUTER_COLS), F32),
                   jax.ShapeDtypeStruct((8, n_tok), F32),
                   jax.ShapeDtypeStruct((8, ROUTER_COLS), F32)],
        scratch_shapes=[pltpu.VMEM((8, ROUTER_COLS), F32)],
        compiler_params=_cparams("arbitrary"),
        name="prenorm_router",
    )(x, w.reshape(1, D_MODEL), modtab, w_router, tri)


def _row_copy(src, src_row, dst, dst_row, sem):
    return pltpu.make_async_copy(src.at[pl.ds(src_row, 1)], dst.at[pl.ds(dst_row, 1)], sem)


def _dispatch_kernel(dest_ref, h_ref, xb_ref, sem):
    def issue(r, carry):
        for k in range(TOP_K):
            _row_copy(h_ref, r, xb_ref, dest_ref[0, k, r], sem).start(priority=k)
        return carry

    lax.fori_loop(0, TILE, issue, 0, unroll=8)

    def drain(r, carry):
        for k in range(TOP_K):
            _row_copy(h_ref, 0, xb_ref, 0, sem).wait()
        return carry

    lax.fori_loop(0, TILE, drain, 0, unroll=8)


def _dispatch(h, dest3):
    n_tok = h.shape[0]
    return pl.pallas_call(
        _dispatch_kernel,
        grid=(n_tok // TILE,),
        in_specs=[pl.BlockSpec((1, TOP_K, TILE), lambda i: (i, 0, 0), memory_space=pltpu.SMEM),
                  pl.BlockSpec((TILE, D_MODEL), lambda i: (i, 0))],
        out_specs=pl.BlockSpec(memory_space=pl.ANY),
        out_shape=jax.ShapeDtypeStruct((n_tok * TOP_K, D_MODEL), F32),
        scratch_shapes=[pltpu.SemaphoreType.DMA(())],
        compiler_params=_cparams("arbitrary"),
        name="moe_dispatch",
    )(dest3, h)


def _moe_kernel(tile_ref, exp_ref, lo_ref, hi_ref, flag_ref, next_ref, x_ref, wg_hbm, wu_hbm, wd_hbm, o_ref,
                stage_g, stage_u, stage_d, wg_s, wu_s, wd_s, sem, *, layer):
    i = pl.program_id(0)
    flags = flag_ref[i]

    def fetch(e):
        return (pltpu.make_async_copy(wg_hbm.at[layer, e], stage_g, sem.at[0]),
                pltpu.make_async_copy(wu_hbm.at[layer, e], stage_u, sem.at[1]),
                pltpu.make_async_copy(wd_hbm.at[layer, e], stage_d, sem.at[2]))

    @pl.when(i == 0)
    def _():
        for copy in fetch(exp_ref[0]):
            copy.start()

    @pl.when(flags % 2 == 1)
    def _():
        for copy in fetch(exp_ref[i]):
            copy.wait()
        wg_s[...] = stage_g[...].astype(BF16)
        wu_s[...] = stage_u[...].astype(BF16)
        wd_s[...] = stage_d[...].astype(BF16)

        @pl.when(next_ref[i] >= 0)
        def _():
            for copy in fetch(next_ref[i]):
                copy.start()

    @pl.when(flags >= 4)
    def _():
        x = x_ref[...].astype(BF16)
        g = jnp.dot(x, wg_s[...], preferred_element_type=F32)
        u = jnp.dot(x, wu_s[...], preferred_element_type=F32)
        row = lax.broadcasted_iota(jnp.int32, g.shape, 0)
        mine = jnp.logical_and(row >= lo_ref[i], row < hi_ref[i])
        h = jnp.where(mine, g * jax.nn.sigmoid(g) * u, 0.0).astype(BF16)
        res = jnp.dot(h, wd_s[...], preferred_element_type=F32)

        @pl.when((flags // 2) % 2 == 1)
        def _():
            o_ref[...] = res

        @pl.when((flags // 2) % 2 == 0)
        def _():
            o_ref[...] += res


def _moe_experts(xb, meta, wg, wu, wd, l):
    n_items = meta[0].shape[0]
    xmap = lambda i, ti, ex, lo, hi, fl, nx: (ti[i], 0)
    hbm = pl.BlockSpec(memory_space=pl.ANY)
    grid_spec = pltpu.PrefetchScalarGridSpec(
        num_scalar_prefetch=6,
        grid=(n_items,),
        in_specs=[pl.BlockSpec((MOE_ROWS, D_MODEL), xmap), hbm, hbm, hbm],
        out_specs=pl.BlockSpec((MOE_ROWS, D_MODEL), xmap),
        scratch_shapes=[pltpu.VMEM((D_MODEL, D_EXPERT), F32),
                        pltpu.VMEM((D_MODEL, D_EXPERT), F32),
                        pltpu.VMEM((D_EXPERT, D_MODEL), F32),
                        pltpu.VMEM((D_MODEL, D_EXPERT), BF16),
                        pltpu.VMEM((D_MODEL, D_EXPERT), BF16),
                        pltpu.VMEM((D_EXPERT, D_MODEL), BF16),
                        pltpu.SemaphoreType.DMA((3,))])
    return pl.pallas_call(
        functools.partial(_moe_kernel, layer=l),
        grid_spec=grid_spec,
        out_shape=jax.ShapeDtypeStruct(xb.shape, F32),
        compiler_params=_cparams("arbitrary"),
        name="moe_experts",
    )(*meta, xb, wg, wu, wd)


def _combine_kernel(dest_ref, next_ref, yb_ref, x_ref, route_ref, mod_ref, nw_ref, nmod_ref, *rest, final):
    if final:
        o_ref, buf, sem = rest
    else:
        o_ref, hx_ref, buf, sem = rest
    i = pl.program_id(0)
    slot = i % 2
    n_groups = TILE // COMBINE_ROWS

    def issue(d_ref, to_slot, j):
        for rr in range(COMBINE_ROWS):
            r = j * COMBINE_ROWS + rr
            for k in range(TOP_K):
                _row_copy(yb_ref, d_ref[0, k, r], buf.at[to_slot, k], r, sem.at[to_slot]).start(priority=k)

    @pl.when(i == 0)
    def _():
        def first(j, carry):
            issue(dest_ref, 0, j)
            return carry

        lax.fori_loop(0, n_groups, first, 0)

    def drain(r, carry):
        for k in range(TOP_K):
            _row_copy(yb_ref, 0, buf.at[slot, k], 0, sem.at[slot]).wait()
        return carry

    lax.fori_loop(0, TILE, drain, 0, unroll=8)
    g2 = mod_ref[0][5:6, :]
    nw = nw_ref[...]
    nm = nmod_ref[0]

    def combine(j):
        rows = pl.ds(pl.multiple_of(j * COMBINE_ROWS, COMBINE_ROWS), COMBINE_ROWS)
        rt = route_ref[rows, :]
        f = buf[slot, 0, rows, :] * rt[:, R_W0:R_W0 + 1] + buf[slot, 1, rows, :] * rt[:, R_W1:R_W1 + 1]
        y = x_ref[rows, :] + g2 * f
        if final:
            o_ref[rows, :] = _rms(y, nw)
        else:
            o_ref[rows, :] = y
            hx_ref[rows, :] = (_rms(y, nw) * (1.0 + nm[1:2, :]) + nm[0:1, :]).astype(hx_ref.dtype)

    @pl.when(i + 1 < pl.num_programs(0))
    def _():
        def body(j, carry):
            combine(j)
            issue(next_ref, 1 - slot, j)
            return carry

        lax.fori_loop(0, n_groups, body, 0)

    @pl.when(i + 1 >= pl.num_programs(0))
    def _():
        def body(j, carry):
            combine(j)
            return carry

        lax.fori_loop(0, n_groups, body, 0)


COMBINE_ROWS = 128


def _combine(yb, dest3, x, route, modtab, norm_w, next_modtab, latent_only):
    n_tok = route.shape[0]
    n_tiles = n_tok // TILE
    src = lambda i: _token_tile(i, latent_only)
    row = lambda i: (i, 0)
    mod_spec = pl.BlockSpec((1, 6, D_MODEL), lambda i: (_tile_mod_row(src(i)), 0, 0))
    out_specs = [pl.BlockSpec((TILE, D_MODEL), row)]
    out_shape = [jax.ShapeDtypeStruct((n_tok, D_MODEL), F32)]
    if not latent_only:
        out_specs.append(pl.BlockSpec((TILE, D_MODEL), row))
        out_shape.append(jax.ShapeDtypeStruct((n_tok, D_MODEL), BF16))
    return pl.pallas_call(
        functools.partial(_combine_kernel, final=latent_only),
        grid=(n_tiles,),
        in_specs=[pl.BlockSpec((1, TOP_K, TILE), lambda i: (i, 0, 0), memory_space=pltpu.SMEM),
                  pl.BlockSpec((1, TOP_K, TILE), lambda i: (jnp.minimum(i + 1, n_tiles - 1), 0, 0),
                               memory_space=pltpu.SMEM),
                  pl.BlockSpec(memory_space=pl.ANY),
                  pl.BlockSpec((TILE, D_MODEL), lambda i: (src(i), 0)),
                  pl.BlockSpec((TILE, ROUTER_COLS), row),
                  mod_spec,
                  pl.BlockSpec((1, D_MODEL), lambda i: (0, 0)),
                  mod_spec],
        out_specs=out_specs,
        out_shape=out_shape,
        scratch_shapes=[pltpu.VMEM((2, TOP_K, TILE, D_MODEL), F32), pltpu.SemaphoreType.DMA((2,))],
        compiler_params=_cparams("arbitrary"),
        name="moe_combine",
    )(dest3, dest3, yb, x, route, modtab, norm_w.reshape(1, D_MODEL), next_modtab)


def _moe_plan(route_t, cnt):
    t = route_t.shape[1]
    n_tiles = t * TOP_K // MOE_ROWS
    n_items = n_tiles + N_EXPERTS - 1
    experts = route_t[R_E0:R_E1 + 1].astype(jnp.int32)
    rank = route_t[R_K0:R_K1 + 1].astype(jnp.int32)
    counts = cnt[0, :N_EXPERTS].astype(jnp.int32)
    ends = jnp.cumsum(counts)
    starts = ends - counts
    e_axis = jnp.arange(N_EXPERTS, dtype=jnp.int32)[:, None, None]
    dest = jnp.sum(jnp.where(experts[None] == e_axis, starts[:, None, None], 0), axis=0) + rank
    t_first = starts // MOE_ROWS
    per_e = jnp.where(counts > 0, (ends - 1) // MOE_ROWS - t_first + 1, 0)
    item_end = jnp.cumsum(per_e)
    item_start = item_end - per_e
    total = item_end[-1]
    w = jnp.arange(n_items, dtype=jnp.int32)
    valid = w < total
    wc = jnp.minimum(w, total - 1)
    ex = jnp.minimum(jnp.sum((item_end[None, :] <= wc[:, None]).astype(jnp.int32), axis=1), N_EXPERTS - 1)
    e_ids = jnp.arange(N_EXPERTS, dtype=jnp.int32)
    onehot = ex[:, None] == e_ids[None, :]
    pick = lambda table: jnp.sum(jnp.where(onehot, table[None, :], 0), axis=1)
    tile = (pick(t_first) + wc - pick(item_start)).astype(jnp.int32)
    lo = jnp.where(valid, jnp.maximum(pick(starts), tile * MOE_ROWS) - tile * MOE_ROWS, 0).astype(jnp.int32)
    hi = jnp.where(valid, jnp.minimum(pick(ends), (tile + 1) * MOE_ROWS) - tile * MOE_ROWS, 0).astype(jnp.int32)
    one = jnp.ones((1,), bool)
    new_e = jnp.concatenate([one, ex[1:] != ex[:-1]])
    new_t = jnp.concatenate([one, tile[1:] != tile[:-1]])
    flags = (new_e.astype(jnp.int32) + 2 * new_t.astype(jnp.int32) + 4 * valid.astype(jnp.int32))
    later = jnp.where(jnp.logical_and(counts[None, :] > 0, e_ids[None, :] > e_ids[:, None]), e_ids[None, :], N_EXPERTS)
    next_of = pick(jnp.min(later, axis=1))
    nxt = jnp.where(next_of < N_EXPERTS, next_of, -1).astype(jnp.int32)
    dest3 = dest.reshape(TOP_K, t // TILE, TILE).transpose(1, 0, 2)
    return dest3, (tile, ex, lo, hi, flags, nxt)


def _moe(x, h, route, route_t, cnt, modtab, wg, wu, wd, l, norm_w, next_modtab, latent_only):
    dest3, meta = _moe_plan(route_t, cnt)
    xb = _dispatch(h, dest3)
    yb = _moe_experts(xb, meta, wg, wu, wd, l)
    res = _combine(yb, dest3, x, route, modtab, norm_w, next_modtab, latent_only)
    return res[0] if latent_only else (res[0], res[1])


def _scan_chunk(s, rev, chunk):
    n_c = CTX_LEN // chunk
    n_all = PB // chunk
    if not rev:
        return s
    return jnp.where(s < n_c, n_c - 1 - s, n_all + n_c - 1 - s)


HALO = 8


def _ssd_prep_kernel(cur_ref, prev_ref, next_ref, dt_ref, cw_ref, cb_ref, dtb_ref, xo_ref, dto_ref, ext_s):
    j = pl.program_id(0) % TPB
    first = jnp.logical_or(j == 0, j == 1)
    last = jnp.logical_or(j == 0, j == TPB - 1)
    ext_s[0:HALO, :] = jnp.where(first, 0.0, prev_ref[...])
    ext_s[HALO:HALO + TILE, :] = cur_ref[...]
    ext_s[HALO + TILE:, :] = jnp.where(last, 0.0, next_ref[...])
    half = (SSD_CONV - 1) // 2
    acc = cb_ref[...] + cw_ref[0:1, :] * ext_s[HALO - half:HALO - half + TILE, :]
    for k in range(1, SSD_CONV):
        acc = acc + cw_ref[k:k + 1, :] * ext_s[HALO - half + k:HALO - half + k + TILE, :]
    xo_ref[...] = acc * jax.nn.sigmoid(acc)
    lane = lax.broadcasted_iota(jnp.int32, (TILE, LANES), 1)
    dto_ref[...] = jnp.where(lane < 2 * H_B, jax.nn.softplus(dt_ref[...] + dtb_ref[...]), 0.0)


def _ssd_prep(z, conv_w, conv_b, dt_bias):
    n_tiles = T_ALL // TILE
    per = TILE // HALO
    cwp = jnp.concatenate([conv_w, jnp.zeros((8 - SSD_CONV, CONV_CH), F32)], axis=0)
    dtb = jnp.concatenate([dt_bias.reshape(1, 2 * H_B), jnp.zeros((1, LANES - 2 * H_B), F32)], axis=1)
    xc = Z_XBC // CONV_CH
    return pl.pallas_call(
        _ssd_prep_kernel,
        grid=(n_tiles,),
        in_specs=[pl.BlockSpec((TILE, CONV_CH), lambda i: (i, xc)),
                  pl.BlockSpec((HALO, CONV_CH), lambda i: (jnp.maximum(i * per - 1, 0), xc)),
                  pl.BlockSpec((HALO, CONV_CH), lambda i: (jnp.minimum((i + 1) * per, T_ALL // HALO - 1), xc)),
                  pl.BlockSpec((TILE, LANES), lambda i: (i, Z_DT // LANES)),
                  pl.BlockSpec((8, CONV_CH), lambda i: (0, 0)),
                  pl.BlockSpec((1, CONV_CH), lambda i: (0, 0)),
                  pl.BlockSpec((1, LANES), lambda i: (0, 0))],
        out_specs=[pl.BlockSpec((TILE, CONV_CH), lambda i: (i, 0)),
                   pl.BlockSpec((TILE, LANES), lambda i: (i, 0))],
        out_shape=[jax.ShapeDtypeStruct((T_ALL, CONV_CH), F32),
                   jax.ShapeDtypeStruct((T_ALL, LANES), F32)],
        scratch_shapes=[pltpu.VMEM((TILE + 2 * HALO, CONV_CH), F32)],
        compiler_params=_cparams("arbitrary"),
        name="ssd_prep",
    )(z, z, z, z, cwp, conv_b.reshape(1, CONV_CH), dtb)


def _ssd_one_direction(xbc, dtp, arow, tri, expand, h_ref, y_ref, b, d, rev):
    q = SSD_CHUNK
    a = dtp * arow
    acum = _dot3_left(tri, a)
    yield
    acum_t = acum.T
    dt_t = dtp.T
    yield
    edge = 0 if rev else q - 1
    atot = acum[edge:edge + 1, :]
    pieces = jnp.concatenate([jnp.exp(atot - acum) * dtp, jnp.exp(acum),
                              jnp.broadcast_to(jnp.exp(atot), (8, LANES))], axis=0)
    ex = _dot3_right(pieces, expand)
    yield
    wend_x = ex[0:q]
    eacum_x = ex[q:2 * q]
    dec_x = ex[2 * q:2 * q + 1]
    xs = xbc[:, 0:D_SSD]
    xw = (xs * wend_x).astype(BF16)
    xs_b = xs.astype(BF16)
    h_old = h_ref[b]
    h_b = h_old.astype(BF16)
    ri = lax.broadcasted_iota(jnp.int32, (q, q), 0)
    ci = lax.broadcasted_iota(jnp.int32, (q, q), 1)
    mask = (ci >= ri) if rev else (ci <= ri)
    lo_half = lax.broadcasted_iota(jnp.int32, (q, LANES), 1) < P_B
    hpg = H_B // G_B
    gw = hpg * P_B
    yield

    def group(g):
        bg = xbc[:, D_SSD + g * N_B:D_SSD + (g + 1) * N_B].astype(BF16)
        cg = xbc[:, D_SSD + G_B * N_B + g * N_B:D_SSD + G_B * N_B + (g + 1) * N_B].astype(BF16)
        cb = lax.dot_general(cg, bg, _NT, preferred_element_type=F32)
        inter = jnp.dot(cg, h_b[:, g * gw:(g + 1) * gw], preferred_element_type=F32)
        upd = lax.dot_general(bg, xw[:, g * gw:(g + 1) * gw], _TN, preferred_element_type=F32)
        yield

        def pair(j):
            ms = []
            for hh in range(2):
                hc = H_B * d + hpg * g + 2 * j + hh
                seg = acum[:, hc:hc + 1] - acum_t[hc:hc + 1, :]
                dec = jnp.exp(jnp.where(mask, seg, NEG_BIG))
                ms.append((cb * dec * dt_t[hc:hc + 1, :]).astype(BF16))
                yield
            c0 = g * gw + 2 * j * P_B
            xp = xs_b[:, c0:c0 + LANES]
            zero = jnp.zeros_like(xp)
            rhs = jnp.concatenate([jnp.where(lo_half, xp, zero), jnp.where(lo_half, zero, xp)], axis=0)
            y_intra = jnp.dot(jnp.concatenate(ms, axis=1), rhs, preferred_element_type=F32)
            yield
            y_ref[b, :, c0:c0 + LANES] = (
                y_intra + eacum_x[:, c0:c0 + LANES] * inter[:, 2 * j * P_B:2 * j * P_B + LANES])
            yield

        yield from _in_turn([pair(j) for j in range(hpg // 2)])
        h_ref[b, :, g * gw:(g + 1) * gw] = dec_x[:, g * gw:(g + 1) * gw] * h_old[:, g * gw:(g + 1) * gw] + upd
        yield

    yield from _in_turn([group(g) for g in range(G_B)])


def _ssd_scan_kernel(xf_ref, dtf_ref, xb_ref, dtb_ref, arow_ref, trif_ref, trib_ref, ef_ref, eb_ref,
                     yf_ref, yb_ref, hf_s, hb_s):
    @pl.when(pl.program_id(0) == 0)
    def _():
        hf_s[...] = jnp.zeros_like(hf_s)
        hb_s[...] = jnp.zeros_like(hb_s)

    chains = []
    for b in range(BATCH):
        chains.append(_ssd_one_direction(xf_ref[b], dtf_ref[b], arow_ref[...], trif_ref[...], ef_ref[...],
                                         hf_s, yf_ref, b, 0, False))
        chains.append(_ssd_one_direction(xb_ref[b], dtb_ref[b], arow_ref[...], trib_ref[...], eb_ref[...],
                                         hb_s, yb_ref, b, 1, True))
    _round_robin(chains)


def _ssd_scan(xact, dtp, a_log):
    q = SSD_CHUNK
    steps = PB // q
    a_neg = -jnp.exp(a_log)
    arow = jnp.concatenate([a_neg.reshape(1, 2 * H_B), jnp.zeros((1, LANES - 2 * H_B), F32)], axis=1)
    r = jnp.arange(q)
    tri_f = (r[None, :] <= r[:, None]).astype(BF16)
    tri_b = (r[None, :] >= r[:, None]).astype(BF16)
    col_head = jnp.arange(D_SSD) // P_B
    lane = jnp.arange(LANES)
    exp_f = (lane[:, None] == col_head[None, :]).astype(BF16)
    exp_b = (lane[:, None] == col_head[None, :] + H_B).astype(BF16)
    fwd = lambda s: (0, _scan_chunk(s, False, q), 0)
    bwd = lambda s: (0, _scan_chunk(s, True, q), 0)
    const = lambda s: (0, 0)
    x3 = xact.reshape(BATCH, PB, CONV_CH)
    d3 = dtp.reshape(BATCH, PB, LANES)
    yf, yb = pl.pallas_call(
        _ssd_scan_kernel,
        grid=(steps,),
        in_specs=[pl.BlockSpec((BATCH, q, CONV_CH), fwd), pl.BlockSpec((BATCH, q, LANES), fwd),
                  pl.BlockSpec((BATCH, q, CONV_CH), bwd), pl.BlockSpec((BATCH, q, LANES), bwd),
                  pl.BlockSpec((1, LANES), const),
                  pl.BlockSpec((q, q), const), pl.BlockSpec((q, q), const),
                  pl.BlockSpec((LANES, D_SSD), const), pl.BlockSpec((LANES, D_SSD), const)],
        out_specs=[pl.BlockSpec((BATCH, q, D_SSD), fwd), pl.BlockSpec((BATCH, q, D_SSD), bwd)],
        out_shape=[jax.ShapeDtypeStruct((BATCH, PB, D_SSD), F32), jax.ShapeDtypeStruct((BATCH, PB, D_SSD), F32)],
        scratch_shapes=[pltpu.VMEM((BATCH, N_B, D_SSD), F32), pltpu.VMEM((BATCH, N_B, D_SSD), F32)],
        compiler_params=_cparams("arbitrary"),
        name="ssd_scan",
    )(x3, d3, x3, d3, arow, tri_f, tri_b, exp_f, exp_b)
    return yf.reshape(T_ALL, D_SSD), yb.reshape(T_ALL, D_SSD)


def _mlstm_one_direction(q, k, v, gi, gf, bi, bf, tri, st_ref, m_ref, h_ref, b, d, rev):
    n = MLSTM_CHUNK
    li = gi + bi
    lf = jax.nn.log_sigmoid(gf + bf)
    yield
    bc = _dot3_left(tri, lf)
    yield
    b_t = bc.T
    li_t = li.T
    yield
    edge = 0 if rev else n - 1
    gtot = bc[edge:edge + 1, :]
    m_old = m_ref[b, 0:1, :]
    w_log = gtot - bc + li
    m_new = jnp.maximum(gtot + m_old, jnp.max(w_log, axis=0, keepdims=True))
    wj = jnp.exp(w_log - m_new)
    dec = jnp.exp(gtot + m_old - m_new)
    inter_log = bc + m_old
    m_ref[b, 0:1, :] = m_new
    yield
    ri = lax.broadcasted_iota(jnp.int32, (n, n), 0)
    ci = lax.broadcasted_iota(jnp.int32, (n, n), 1)
    mask = (ci >= ri) if rev else (ci <= ri)
    lo_half = lax.broadcasted_iota(jnp.int32, (n, LANES), 1) < DQK_C
    row_lo = lax.broadcasted_iota(jnp.int32, (2 * DQK_C, 2 * DV_C), 0) < DQK_C
    ones = jnp.ones((n, DV_C), F32)
    st_old = [st_ref[b, j] for j in range(H_C // 2)]
    upds = {}

    def head(h):
        j, hh = divmod(h, 2)
        gl = GATE_LANE0 + H_C * d + h
        qp = q[:, j * LANES:(j + 1) * LANES] * DQK_C ** -0.5
        kp = k[:, j * LANES:(j + 1) * LANES].astype(BF16)
        qm = jnp.where(lo_half if hh == 0 else jnp.logical_not(lo_half), qp, 0.0).astype(BF16)
        qk = lax.dot_general(qm, kp, _NT, preferred_element_type=F32)
        qs = jnp.dot(qm, st_old[j].astype(BF16), preferred_element_type=F32)
        yield
        dmat = jnp.where(mask, bc[:, gl:gl + 1] - b_t[gl:gl + 1, :] + li_t[gl:gl + 1, :], NEG_BIG)
        il = inter_log[:, gl:gl + 1]
        m_row = jnp.maximum(il, jnp.max(dmat, axis=1, keepdims=True))
        yield
        s = qk * jnp.exp(dmat - m_row)
        w_inter = jnp.exp(il - m_row)
        vh = v[:, h * DV_C:(h + 1) * DV_C]
        yield
        num = jnp.dot(s.astype(BF16), vh.astype(BF16), preferred_element_type=F32) + w_inter * qs[:, :DV_C]
        den = jnp.sum(s, axis=1, keepdims=True) + w_inter * qs[:, DV_C:]
        yield
        h_ref[b, :, h * DV_C:(h + 1) * DV_C] = num / jnp.maximum(jnp.abs(den), jnp.exp(-m_row))
        rhs = (wj[:, gl:gl + 1] * jnp.concatenate([vh, ones], axis=1)).astype(BF16)
        upds[h] = lax.dot_general(kp, rhs, _TN, preferred_element_type=F32)
        yield

    yield from _in_turn([head(h) for h in range(H_C)])
    for j in range(H_C // 2):
        ga = GATE_LANE0 + H_C * d + 2 * j
        decv = jnp.where(row_lo, dec[:, ga:ga + 1], dec[:, ga + 1:ga + 2])
        st_ref[b, j] = decv * st_old[j] + jnp.where(row_lo, upds[2 * j], upds[2 * j + 1])
        yield


def _mlstm_scan_kernel(qf, kf, vf, gif, gff, qb, kb, vb, gib, gfb, bi_ref, bf_ref, trif_ref, trib_ref,
                       hf_ref, hb_ref, stf_s, stb_s, mf_s, mb_s):
    @pl.when(pl.program_id(0) == 0)
    def _():
        stf_s[...] = jnp.zeros_like(stf_s)
        stb_s[...] = jnp.zeros_like(stb_s)
        mf_s[...] = jnp.full_like(mf_s, NEG_STATE)
        mb_s[...] = jnp.full_like(mb_s, NEG_STATE)

    chains = []
    for b in range(BATCH):
        chains.append(_mlstm_one_direction(qf[b], kf[b], vf[b], gif[b], gff[b], bi_ref[...], bf_ref[...],
                                           trif_ref[...], stf_s, mf_s, hf_ref, b, 0, False))
        chains.append(_mlstm_one_direction(qb[b], kb[b], vb[b], gib[b], gfb[b], bi_ref[...], bf_ref[...],
                                           trib_ref[...], stb_s, mb_s, hb_ref, b, 1, True))
    _round_robin(chains)


def _mlstm_scan(z, gate_b):
    n = MLSTM_CHUNK
    steps = PB // n
    qkw = H_C * DQK_C
    pad = lambda t: jnp.concatenate([jnp.zeros((1, GATE_LANE0), F32), t.reshape(1, 2 * H_C),
                                     jnp.zeros((1, LANES - GATE_LANE0 - 2 * H_C), F32)], axis=1)
    bi = pad(gate_b[:, 0, :])
    bf = pad(gate_b[:, 1, :])
    r = jnp.arange(n)
    tri_f = (r[None, :] <= r[:, None]).astype(BF16)
    tri_b = (r[None, :] >= r[:, None]).astype(BF16)
    z3 = z.reshape(BATCH, PB, Z_COLS)

    def specs(rev):
        ch = lambda s: _scan_chunk(s, rev, n)
        return [pl.BlockSpec((BATCH, n, qkw), lambda s: (0, ch(s), Z_MQ // qkw)),
                pl.BlockSpec((BATCH, n, qkw), lambda s: (0, ch(s), Z_MK // qkw)),
                pl.BlockSpec((BATCH, n, D_MLSTM), lambda s: (0, ch(s), Z_MV // D_MLSTM)),
                pl.BlockSpec((BATCH, n, LANES), lambda s: (0, ch(s), Z_MG // LANES)),
                pl.BlockSpec((BATCH, n, LANES), lambda s: (0, ch(s), Z_DT // LANES))]

    const = lambda s: (0, 0)
    n_pairs = H_C // 2
    hf, hb = pl.pallas_call(
        _mlstm_scan_kernel,
        grid=(steps,),
        in_specs=specs(False) + specs(True) + [
            pl.BlockSpec((1, LANES), const), pl.BlockSpec((1, LANES), const),
            pl.BlockSpec((n, n), const), pl.BlockSpec((n, n), const)],
        out_specs=[pl.BlockSpec((BATCH, n, D_MLSTM), lambda s: (0, _scan_chunk(s, False, n), 0)),
                   pl.BlockSpec((BATCH, n, D_MLSTM), lambda s: (0, _scan_chunk(s, True, n), 0))],
        out_shape=[jax.ShapeDtypeStruct((BATCH, PB, D_MLSTM), F32), jax.ShapeDtypeStruct((BATCH, PB, D_MLSTM), F32)],
        scratch_shapes=[pltpu.VMEM((BATCH, n_pairs, 2 * DQK_C, 2 * DV_C), F32),
                        pltpu.VMEM((BATCH, n_pairs, 2 * DQK_C, 2 * DV_C), F32),
                        pltpu.VMEM((BATCH, 8, LANES), F32), pltpu.VMEM((BATCH, 8, LANES), F32)],
        compiler_params=_cparams("arbitrary"),
        name="mlstm_scan",
    )(z3, z3, z3, z3, z3, z3, z3, z3, z3, z3, bi, bf, tri_f, tri_b)
    return hf.reshape(T_ALL, D_MLSTM), hb.reshape(T_ALL, D_MLSTM)


S5_Q = 256
S5_SEG = S5_Q // 8
S5_LANES = G_S5 * P_S5
S5_SLAB = 512
S5_NSLAB = S5_LANES // S5_SLAB


def _s5_dir_kernel(u_ref, perm_ref, wbr_ref, wbi_ref, atab_ref, apr_ref, api_ref, wcr_ref, wci_ref, y_ref,
                   ur_s, ui_s, xr_s, xi_s, st_s, car_s, *, reverse):
    @pl.when(pl.program_id(0) == 0)
    def _():
        st_s[...] = jnp.zeros_like(st_s)

    for b in range(BATCH):
        up = jnp.dot(perm_ref[...], u_ref[b].astype(BF16), preferred_element_type=F32).astype(BF16)
        for m in range(S5_NSLAB):
            um = up[:, m * LANES:(m + 1) * LANES]
            ur_s[b, :, m * S5_SLAB:(m + 1) * S5_SLAB] = jnp.dot(um, wbr_ref[m], preferred_element_type=F32)
            ui_s[b, :, m * S5_SLAB:(m + 1) * S5_SLAB] = jnp.dot(um, wbi_ref[m], preferred_element_type=F32)

    per = 4
    for grp in range(S5_LANES // (per * LANES)):
        cols = [grp * per * LANES + j * LANES for j in range(per)]
        a_r = [atab_ref[0:8, c0:c0 + LANES] for c0 in cols]
        a_i = [atab_ref[8:16, c0:c0 + LANES] for c0 in cols]

        def body(i, carry, cols=cols, a_r=a_r, a_i=a_i):
            t = (S5_SEG - 1 - i) if reverse else i
            r0 = pl.multiple_of(t * 8, 8)
            new = []
            for b in range(BATCH):
                for j, c0 in enumerate(cols):
                    xr, xi = carry[2 * (b * per + j)], carry[2 * (b * per + j) + 1]
                    nr = a_r[j] * xr - a_i[j] * xi + ur_s[b, pl.ds(r0, 8), c0:c0 + LANES]
                    ni = a_r[j] * xi + a_i[j] * xr + ui_s[b, pl.ds(r0, 8), c0:c0 + LANES]
                    ur_s[b, pl.ds(r0, 8), c0:c0 + LANES] = nr
                    ui_s[b, pl.ds(r0, 8), c0:c0 + LANES] = ni
                    new += [nr, ni]
            return tuple(new)

        lax.fori_loop(0, S5_SEG, body, tuple(jnp.zeros((8, LANES), F32) for _ in range(2 * per * BATCH)), unroll=2)

    as_r = atab_ref[16:17, :]
    as_i = atab_ref[17:18, :]
    end_row = 0 if reverse else 8 * (S5_SEG - 1)
    for b in range(BATCH):
        cr = st_s[b, 0:1, :]
        ci = st_s[b, 1:2, :]
        for k in (range(7, -1, -1) if reverse else range(8)):
            car_s[b, k:k + 1, :] = cr
            car_s[b, 8 + k:9 + k, :] = ci
            er = ur_s[b, end_row + k:end_row + k + 1, :]
            ei = ui_s[b, end_row + k:end_row + k + 1, :]
            cr, ci = er + as_r * cr - as_i * ci, ei + as_r * ci + as_i * cr
        st_s[b, 0:1, :] = cr
        st_s[b, 1:2, :] = ci

    for b in range(BATCH):
        for m in range(S5_NSLAB):
            cs = slice(m * S5_SLAB, (m + 1) * S5_SLAB)
            c_r = jnp.concatenate([car_s[b, 0:8, cs], car_s[b, 0:8, cs]], axis=0)
            c_i = jnp.concatenate([car_s[b, 8:16, cs], car_s[b, 8:16, cs]], axis=0)

            def fix(i, _, b=b, cs=cs, c_r=c_r, c_i=c_i):
                r0 = pl.multiple_of(i * 16, 16)
                p_r = apr_ref[pl.ds(r0, 16), cs]
                p_i = api_ref[pl.ds(r0, 16), cs]
                xr_s[b, pl.ds(r0, 16), cs] = (ur_s[b, pl.ds(r0, 16), cs] + p_r * c_r - p_i * c_i).astype(BF16)
                xi_s[b, pl.ds(r0, 16), cs] = (ui_s[b, pl.ds(r0, 16), cs] + p_r * c_i + p_i * c_r).astype(BF16)
                return 0

            lax.fori_loop(0, S5_Q // 16, fix, 0, unroll=2)

    for b in range(BATCH):
        for m in range(S5_NSLAB):
            cs = slice(m * S5_SLAB, (m + 1) * S5_SLAB)
            y_ref[b, :, m * LANES:(m + 1) * LANES] = (
                jnp.dot(xr_s[b, :, cs], wcr_ref[m], preferred_element_type=F32)
                - jnp.dot(xi_s[b, :, cs], wci_ref[m], preferred_element_type=F32))


def _s5_direction(z3, perm, wbr, wbi, atab, apr, api, wcr, wci, reverse):
    steps = PB // S5_Q
    chunk = lambda s: _scan_chunk(s, reverse, S5_Q)
    const2 = lambda s: (0, 0)
    const3 = lambda s: (0, 0, 0)
    return pl.pallas_call(
        functools.partial(_s5_dir_kernel, reverse=reverse),
        grid=(steps,),
        in_specs=[pl.BlockSpec((BATCH, S5_Q, D_S5), lambda s: (0, chunk(s), Z_D // D_S5)),
                  pl.BlockSpec((S5_Q, S5_Q), const2),
                  pl.BlockSpec(wbr.shape, const3),
                  pl.BlockSpec(wbi.shape, const3),
                  pl.BlockSpec(atab.shape, const2),
                  pl.BlockSpec(apr.shape, const2),
                  pl.BlockSpec(api.shape, const2),
                  pl.BlockSpec(wcr.shape, const3),
                  pl.BlockSpec(wci.shape, const3)],
        out_specs=pl.BlockSpec((BATCH, S5_Q, D_S5), lambda s: (0, chunk(s), 0)),
        out_shape=jax.ShapeDtypeStruct((BATCH, PB, D_S5), F32),
        scratch_shapes=[pltpu.VMEM((BATCH, S5_Q, S5_LANES), F32), pltpu.VMEM((BATCH, S5_Q, S5_LANES), F32),
                        pltpu.VMEM((BATCH, S5_Q, S5_LANES), BF16), pltpu.VMEM((BATCH, S5_Q, S5_LANES), BF16),
                        pltpu.VMEM((BATCH, 8, S5_LANES), F32), pltpu.VMEM((BATCH, 16, S5_LANES), F32)],
        compiler_params=_cparams("arbitrary"),
        name="s5_bwd" if reverse else "s5_fwd",
    )(z3, perm, wbr, wbi, atab, apr, api, wcr, wci).reshape(T_ALL, D_S5)


def _s5_tables(lam_re, lam_im, log_dt, b_re, b_im, c_re, c_im):
    dt = jnp.exp(log_dt)[..., None]
    mag = jnp.exp(lam_re * dt)
    ar = mag * jnp.cos(lam_im * dt)
    ai = mag * jnp.sin(lam_im * dt)
    den = lam_re * lam_re + lam_im * lam_im
    cr_ = ((ar - 1.0) * lam_re + ai * lam_im) / den
    ci_ = (ai * lam_re - (ar - 1.0) * lam_im) / den
    bbr = cr_[..., None] * b_re - ci_[..., None] * b_im
    bbi = cr_[..., None] * b_im + ci_[..., None] * b_re
    gps = S5_SLAB // P_S5
    eye = jnp.eye(gps, dtype=F32)

    def drive_w(bb):
        t = bb.reshape(S5_NSLAB, gps, P_S5, S5_GROUP)
        w = jnp.einsum('mgpc,gh->mgchp', t, eye)
        return w.reshape(S5_NSLAB, gps * S5_GROUP, gps * P_S5).astype(BF16)

    def read_w(cc):
        t = cc.reshape(S5_NSLAB, gps, S5_GROUP, P_S5)
        w = jnp.einsum('mgcp,gh->mgphc', t, eye)
        return w.reshape(S5_NSLAB, gps * P_S5, gps * S5_GROUP).astype(BF16)

    steps = jnp.arange(1, S5_SEG + 1, dtype=F32)[:, None]
    out = []
    for d in range(2):
        decay = (lam_re[d] * dt[d]).reshape(1, S5_LANES)
        angle = (lam_im[d] * dt[d]).reshape(1, S5_LANES)
        pmag = jnp.exp(steps * decay)
        pr = pmag * jnp.cos(steps * angle)
        pi = pmag * jnp.sin(steps * angle)
        if d == 1:
            apr, api = jnp.repeat(pr[::-1], 8, axis=0), jnp.repeat(pi[::-1], 8, axis=0)
        else:
            apr, api = jnp.repeat(pr, 8, axis=0), jnp.repeat(pi, 8, axis=0)
        atab = jnp.concatenate([jnp.broadcast_to(pr[0:1], (8, S5_LANES)), jnp.broadcast_to(pi[0:1], (8, S5_LANES)),
                                pr[S5_SEG - 1:], pi[S5_SEG - 1:], jnp.zeros((6, S5_LANES), F32)], axis=0)
        out.append((drive_w(bbr[d]), drive_w(bbi[d]), atab, apr, api))
    r = jnp.arange(S5_Q)
    src = (r % 8) * S5_SEG + r // 8
    perm = (src[:, None] == jnp.arange(S5_Q)[None, :]).astype(BF16)
    return out, read_w(c_re), read_w(c_im), perm


def _s5_scans(z, p):
    dirs, wcr, wci, perm = _s5_tables(p["s5_lam_re"], p["s5_lam_im"], p["s5_log_dt"], p["s5_b_re"], p["s5_b_im"],
                                      p["s5_c_re"], p["s5_c_im"])
    z3 = z.reshape(BATCH, PB, Z_COLS)
    yf = _s5_direction(z3, perm, *dirs[0], wcr, wci, reverse=False)
    yb = _s5_direction(z3, perm, *dirs[1], wcr, wci, reverse=True)
    return yf, yb, perm.T


def _merge_kernel(ya_ref, sf_ref, sb_ref, sx_ref, sg_ref, mf_ref, mb_ref, mo_ref, df_ref, db_ref, du_ref,
                  permt_ref, sd_ref, dd_ref, gw_ref, gb_ref, onw_ref, w0, w1, w2, w3, x_ref, mod_ref,
                  o_ref, wb_s, *, g_idx):
    @pl.when(pl.program_id(0) == 0)
    def _():
        for r, w in enumerate((w0, w1, w2, w3)):
            wb_s[r] = w[...].astype(BF16)

    onw = onw_ref[...]
    out = {}

    def attn():
        out[0] = jnp.dot(ya_ref[...], wb_s[0], preferred_element_type=F32)
        yield

    def ssd():
        gate = sg_ref[...]
        y = (sf_ref[...] + sb_ref[...] + sd_ref[...] * sx_ref[...]) * (gate * jax.nn.sigmoid(gate))
        yield
        y = (_rms(y) * onw[:, D_MLA:D_MLA + D_SSD]).astype(BF16)
        yield
        out[1] = jnp.dot(y, wb_s[1], preferred_element_type=F32)
        yield

    def mlstm():
        gate = jax.nn.sigmoid(mo_ref[...])
        c0 = D_MLA + D_SSD
        parts = []
        for h in range(H_C):
            cs = slice(h * DV_C, (h + 1) * DV_C)
            hn = _rms(mf_ref[:, cs] + mb_ref[:, cs])
            parts.append((hn * gate[:, cs] * onw[:, c0 + h * DV_C:c0 + (h + 1) * DV_C]).astype(BF16))
            yield
        out[2] = jnp.dot(jnp.concatenate(parts, axis=1), wb_s[2], preferred_element_type=F32)
        yield

    def s5():
        y = _dot3_left(permt_ref[...], df_ref[...] + db_ref[...])
        yield
        y = jax.nn.gelu(y + dd_ref[...] * du_ref[...])
        yield
        gate = jax.nn.sigmoid(jnp.dot(y.astype(BF16), gw_ref[...], preferred_element_type=F32) + gb_ref[...])
        yield
        y = (_rms(y * gate) * onw[:, D_MLA + D_SSD + D_MLSTM:]).astype(BF16)
        yield
        out[3] = jnp.dot(y, wb_s[3], preferred_element_type=F32)
        yield

    _round_robin([attn(), ssd(), mlstm(), s5()])
    g = mod_ref[0][g_idx:g_idx + 1, :]
    o_ref[...] = x_ref[...] + g * (out[0] + out[1] + out[2] + out[3])


def _merge_proj(ya, ssd, mlstm, s5, z, p, w_out, l, x, modtab, g_idx):
    sf, sb, xact = ssd
    mf, mb = mlstm
    df, db, permt = s5
    kq = D_MODEL // 4
    row = lambda i: (i, 0)
    const = lambda i: (0, 0)
    grp = lambda col: pl.BlockSpec((TILE, kq), lambda i: (i, col // kq))
    w_specs = [pl.BlockSpec((None, kq, D_MODEL), functools.partial(lambda i, r: (l, r, 0), r=r),
                            pipeline_mode=pl.Buffered(1)) for r in range(4)]
    return pl.pallas_call(
        functools.partial(_merge_kernel, g_idx=g_idx),
        grid=(T_ALL // TILE,),
        in_specs=[pl.BlockSpec((TILE, kq), row),
                  pl.BlockSpec((TILE, kq), row), pl.BlockSpec((TILE, kq), row), grp(0), grp(Z_SZ),
                  pl.BlockSpec((TILE, kq), row), pl.BlockSpec((TILE, kq), row), grp(Z_MO),
                  pl.BlockSpec((TILE, kq), row), pl.BlockSpec((TILE, kq), row), grp(Z_D),
                  pl.BlockSpec((TILE, TILE), const),
                  pl.BlockSpec((1, kq), const), pl.BlockSpec((1, kq), const),
                  pl.BlockSpec((kq, kq), const), pl.BlockSpec((1, kq), const),
                  pl.BlockSpec((1, D_MODEL), const)] + w_specs + [
            pl.BlockSpec((TILE, D_MODEL), row),
            pl.BlockSpec((1, 6, D_MODEL), lambda i: (_tile_mod_row(i), 0, 0))],
        out_specs=pl.BlockSpec((TILE, D_MODEL), row),
        out_shape=jax.ShapeDtypeStruct((T_ALL, D_MODEL), F32),
        scratch_shapes=[pltpu.VMEM((4, kq, D_MODEL), BF16)],
        compiler_params=_cparams("arbitrary"),
        name="merge_proj",
    )(ya, sf, sb, xact, z, mf, mb, z, df, db, z, permt,
      jnp.repeat(p["ssd_d"], P_B).reshape(1, kq), p["s5_d"].reshape(1, kq),
      p["s5_glu_w"].astype(BF16), p["s5_glu_b"].reshape(1, kq), p["out_norm_w"].reshape(1, D_MODEL),
      w_out, w_out, w_out, w_out, x, modtab)


def _rope_tables():
    pos = np.arange(SEQ)
    row = (pos // GRID_W).astype(np.float32)
    col = (pos % GRID_W).astype(np.float32)
    inv_freq = (ROPE_BASE ** (-np.arange(ROPE_AXIS // 2, dtype=np.float32) * 2.0 / ROPE_AXIS)).astype(np.float32)
    ang_r = row[:, None] * inv_freq
    ang_c = col[:, None] * inv_freq
    zeros = np.zeros((SEQ, LANES - D_ROPE), np.float32)
    cos = np.concatenate([np.cos(ang_r), np.cos(ang_r), np.cos(ang_c), np.cos(ang_c), zeros], axis=1)
    sin = np.concatenate([np.sin(ang_r), np.sin(ang_r), np.sin(ang_c), np.sin(ang_c), zeros], axis=1)
    cos_c = np.concatenate([np.ones((TILE, D_ROPE), np.float32), np.zeros((TILE, LANES - D_ROPE), np.float32)], axis=1)
    sin_c = np.zeros((TILE, LANES), np.float32)
    table = np.concatenate([np.concatenate([cos, sin], axis=1), np.concatenate([cos_c, sin_c], axis=1)], axis=0)
    return jnp.asarray(table.astype(np.float32))


def _layout_mla(w_uq, w_ukv):
    k = w_uq.shape[0]
    qa, qb, wk, wv = [], [], [], []
    for h in range(H_A):
        base = h * (D_NOPE + D_ROPE)
        rope = w_uq[:, base + D_NOPE:base + D_NOPE + D_ROPE]
        qa += [w_uq[:, base:base + D_NOPE], rope, jnp.zeros((k, LANES - D_ROPE), w_uq.dtype)]
        qb += [_rot_cols(rope), jnp.zeros((k, LANES - D_ROPE), w_uq.dtype)]
        kb = h * (D_NOPE + D_V)
        wk.append(w_ukv[:, kb:kb + D_NOPE])
        wv.append(w_ukv[:, kb + D_NOPE:kb + D_NOPE + D_V])
    cat = lambda xs: jnp.concatenate(xs, axis=1).astype(BF16)
    return cat(qa), cat(qb), cat(wk), cat(wv)


def _layer(xall, hx, modtab, p, big, l, cs, norm_w, next_modtab, last):
    z = _mm(hx, _layout_w_in(big["w_in"], l), MM_ROWS, MM_COLS)
    onw = p["out_norm_w"]
    wqa, wqb, wk, wv = _layout_mla(p["mla_w_uq"], p["mla_w_ukv"])
    q, k, v = _mla_prep(z, cs, p["mla_q_norm_w"], p["mla_kv_norm_w"], wqa, wqb, wk, wv)
    ya = _attention(q, k, v, onw[:D_MLA].reshape(1, D_MLA))
    xact, dtp = _ssd_prep(z, p["ssd_conv_w"], p["ssd_conv_b"], p["ssd_dt_bias"])
    ssd = _ssd_scan(xact, dtp, p["ssd_a_log"]) + (xact,)
    mlstm = _mlstm_scan(z, p["mlstm_gate_b"])
    s5 = _s5_scans(z, p)
    xall = _merge_proj(ya, ssd, mlstm, s5, z, p, big["w_out"], l, xall, modtab, 2)

    w_router = jnp.concatenate([p["moe_w_group"], p["moe_w_expert"],
                                jnp.zeros((D_MODEL, ROUTER_COLS - N_GROUPS - N_EXPERTS), F32)], axis=1)
    wr_hi = w_router.astype(BF16)
    wr_lo = (w_router - wr_hi.astype(F32)).astype(BF16)
    h2, route, route_t, cnt = _prenorm_router(xall, p["norm2_w"], modtab, jnp.stack([wr_hi, wr_lo]), 3, 4, last)
    return _moe(xall, h2, route, route_t, cnt, modtab, big["moe_w_gate"], big["moe_w_up"], big["moe_w_down"], l,
                norm_w, next_modtab, last)


def kernel(x, c, ctx, c_ctx, mod_w, mod_b, norm1_w, w_in, mla_q_norm_w, mla_kv_norm_w, mla_w_uq, mla_w_ukv,
           ssd_conv_w, ssd_conv_b, ssd_a_log, ssd_dt_bias, ssd_d, mlstm_gate_b, s5_lam_re, s5_lam_im,
           s5_log_dt, s5_b_re, s5_b_im, s5_c_re, s5_c_im, s5_d, s5_glu_w, s5_glu_b, out_norm_w, w_out,
           norm2_w, moe_w_group, moe_w_expert, moe_w_gate, moe_w_up, moe_w_down, final_norm_w):
    stacked = {"norm1_w": norm1_w, "mla_q_norm_w": mla_q_norm_w, "mla_kv_norm_w": mla_kv_norm_w,
               "mla_w_uq": mla_w_uq, "mla_w_ukv": mla_w_ukv, "ssd_conv_w": ssd_conv_w, "ssd_conv_b": ssd_conv_b,
               "ssd_a_log": ssd_a_log, "ssd_dt_bias": ssd_dt_bias, "ssd_d": ssd_d, "mlstm_gate_b": mlstm_gate_b,
               "s5_lam_re": s5_lam_re, "s5_lam_im": s5_lam_im, "s5_log_dt": s5_log_dt, "s5_b_re": s5_b_re,
               "s5_b_im": s5_b_im, "s5_c_re": s5_c_re, "s5_c_im": s5_c_im, "s5_d": s5_d, "s5_glu_w": s5_glu_w,
               "s5_glu_b": s5_glu_b, "out_norm_w": out_norm_w, "norm2_w": norm2_w,
               "moe_w_group": moe_w_group, "moe_w_expert": moe_w_expert}
    big = {"w_in": w_in, "w_out": w_out, "moe_w_gate": moe_w_gate, "moe_w_up": moe_w_up, "moe_w_down": moe_w_down}
    cs = _rope_tables()
    cc = jnp.concatenate([c, c_ctx[None, :], jnp.zeros((8 - BATCH - 1, D_MODEL), F32)], axis=0)
    modtabs = [_modulation(cc, mod_w, mod_b, l)[:BATCH + 1].reshape(BATCH + 1, 6, D_MODEL) for l in range(DEPTH)]
    xall, hx = _prenorm(x, ctx, norm1_w[0], modtabs[0], 0, 1)
    for l in range(DEPTH):
        p = {name: val[l] for name, val in stacked.items()}
        if l == DEPTH - 1:
            out = _layer(xall, hx, modtabs[l], p, big, l, cs, final_norm_w, modtabs[l], True)
        else:
            xall, hx = _layer(xall, hx, modtabs[l], p, big, l, cs, norm1_w[l + 1], modtabs[l + 1], False)
    return out.reshape(BATCH, SEQ, D_MODEL)
```
